```python
import jax, jax.numpy as jnp
from jax import lax
import numpy as np

D_MODEL = 1024
BATCH = 8
SEQ = 8192
DEPTH = 2

CHUNK = 64
N_MIXERS = 2
N_HEADS = 16
HEAD_DIM = D_MODEL // N_HEADS
D_FF = 4 * D_MODEL
CONV_WIDTH = 31
Q_BLOCK = 128
RMS_EPS = 1e-6
LN_EPS = 1e-5
MASK_VALUE = -1e30
N_CONV_LAYERS = (DEPTH + 1) // 2
N_ATTN_LAYERS = DEPTH // 2

kernel_name = "hybrid_conformer_conv_fox_sqrelu_sandwich"


def rmsnorm(x, g):
    xf = x.astype(jnp.float32)
    y = xf * lax.rsqrt(jnp.mean(jnp.square(xf), axis=-1, keepdims=True) + RMS_EPS)
    return (y * g.astype(jnp.float32)).astype(x.dtype)


def layernorm(x, g, b):
    xf = x.astype(jnp.float32)
    mu = jnp.mean(xf, axis=-1, keepdims=True)
    var = jnp.mean(jnp.square(xf - mu), axis=-1, keepdims=True)
    y = (xf - mu) * lax.rsqrt(var + LN_EPS)
    return (y * g.astype(jnp.float32) + b.astype(jnp.float32)).astype(x.dtype)


def conformer_conv(h, pw1_w, pw1_b, dw_w, dw_b, ln_g, ln_b, pw2_w, pw2_b):
    a = h @ pw1_w + pw1_b
    u, gate = jnp.split(a, 2, axis=-1)
    u = u * jax.nn.sigmoid(gate)
    y = lax.conv_general_dilated(
        u, dw_w[:, None, :],
        window_strides=(1,),
        padding=((CONV_WIDTH - 1, 0),),
        dimension_numbers=("NWC", "WIO", "NWC"),
        feature_group_count=D_MODEL) + dw_b
    y = layernorm(y, ln_g, ln_b)
    y = jax.nn.silu(y)
    return y @ pw2_w + pw2_b


def forgetting_attention(h, w_in, b_f, w_o):
    B, S, _ = h.shape
    proj = h @ w_in
    q = proj[..., :D_MODEL]
    k = proj[..., D_MODEL:2 * D_MODEL]
    v = proj[..., 2 * D_MODEL:3 * D_MODEL]
    f_logit = proj[..., 3 * D_MODEL:].astype(jnp.float32) + b_f.astype(jnp.float32)
    log_f = jax.nn.log_sigmoid(f_logit)
    F = jnp.cumsum(log_f, axis=1).transpose(0, 2, 1)

    def heads(t):
        return t.reshape(B, S, N_HEADS, HEAD_DIM).transpose(0, 2, 1, 3)
    q, k, v = heads(q), heads(k), heads(v)

    n_blocks = S // Q_BLOCK
    qb = q.reshape(B, N_HEADS, n_blocks, Q_BLOCK, HEAD_DIM).transpose(2, 0, 1, 3, 4)
    Fq = F.reshape(B, N_HEADS, n_blocks, Q_BLOCK).transpose(2, 0, 1, 3)
    kpos = jnp.arange(S)
    scale = HEAD_DIM ** -0.5

    def attend(args):
        q_blk, f_blk, blk = args
        qpos = blk * Q_BLOCK + jnp.arange(Q_BLOCK)
        s = jnp.einsum("bhqd,bhkd->bhqk", q_blk, k,
                       preferred_element_type=jnp.float32) * scale
        s = s + (f_blk[..., None] - F[:, :, None, :])
        s = jnp.where(kpos[None, :] <= qpos[:, None], s, MASK_VALUE)
        p = jax.nn.softmax(s, axis=-1)
        return jnp.einsum("bhqk,bhkd->bhqd", p.astype(v.dtype), v)

    o = lax.map(attend, (qb, Fq, jnp.arange(n_blocks)))
    o = o.transpose(1, 0, 3, 2, 4).reshape(B, S, D_MODEL)
    return o @ w_o


def sqrelu_mlp(h, w_up, w_down):
    return jnp.square(jax.nn.relu(h @ w_up)) @ w_down


def _fwd_setup_inputs(seed: int = 0) -> dict:
    key = jax.random.key(seed)
    ks = jax.random.split(key, 20)
    f32 = jnp.float32
    D, H = D_MODEL, N_HEADS
    nrm = lambda k, shape, s: jax.random.normal(k, shape, f32) * s
    return {
        "x": jax.random.normal(ks[0], (BATCH, SEQ, D), f32),
        "g_mix_pre": 1.0 + nrm(ks[1], (DEPTH, D), 0.02),
        "g_mix_post": 1.0 + nrm(ks[2], (DEPTH, D), 0.02),
        "g_ffn_pre": 1.0 + nrm(ks[3], (DEPTH, D), 0.02),
        "g_ffn_post": 1.0 + nrm(ks[4], (DEPTH, D), 0.02),
        "conv_pw1_w": nrm(ks[5], (N_CONV_LAYERS, D, 2 * D), D ** -0.5),
        "conv_pw1_b": nrm(ks[6], (N_CONV_LAYERS, 2 * D), 0.02),
        "conv_dw_w": nrm(ks[7], (N_CONV_LAYERS, CONV_WIDTH, D), CONV_WIDTH ** -0.5),
        "conv_dw_b": nrm(ks[8], (N_CONV_LAYERS, D), 0.02),
        "conv_ln_g": 1.0 + nrm(ks[9], (N_CONV_LAYERS, D), 0.02),
        "conv_ln_b": nrm(ks[10], (N_CONV_LAYERS, D), 0.02),
        "conv_pw2_w": nrm(ks[11], (N_CONV_LAYERS, D, D), D ** -0.5),
        "conv_pw2_b": nrm(ks[12], (N_CONV_LAYERS, D), 0.02),
        "attn_w_in": nrm(ks[13], (N_ATTN_LAYERS, D, 3 * D + H), D ** -0.5),
        "attn_b_f": 1.0 + nrm(ks[14], (N_ATTN_LAYERS, H), 0.5),
        "attn_w_o": nrm(ks[15], (N_ATTN_LAYERS, D, D), D ** -0.5),
        "mlp_w_up": nrm(ks[16], (DEPTH, D, D_FF), D ** -0.5),
        "mlp_w_down": nrm(ks[17], (DEPTH, D_FF, D), D_FF ** -0.5),
    }


def _fwd_reference(x, g_mix_pre, g_mix_post, g_ffn_pre, g_ffn_post,
              conv_pw1_w, conv_pw1_b, conv_dw_w, conv_dw_b, conv_ln_g, conv_ln_b,
              conv_pw2_w, conv_pw2_b, attn_w_in, attn_b_f, attn_w_o,
              mlp_w_up, mlp_w_down):
    for i in range(DEPTH):
        j = i // N_MIXERS
        h = rmsnorm(x, g_mix_pre[i])
        if i % N_MIXERS == 0:
            m = conformer_conv(h, conv_pw1_w[j], conv_pw1_b[j], conv_dw_w[j], conv_dw_b[j],
                               conv_ln_g[j], conv_ln_b[j], conv_pw2_w[j], conv_pw2_b[j])
        else:
            m = forgetting_attention(h, attn_w_in[j], attn_b_f[j], attn_w_o[j])
        x = x + rmsnorm(m, g_mix_post[i])
        h = rmsnorm(x, g_ffn_pre[i])
        x = x + rmsnorm(sqrelu_mlp(h, mlp_w_up[i], mlp_w_down[i]), g_ffn_post[i])
    return x


import jax as _jax
import jax.numpy as _jnp

TWIN_FORMAT = 'train_step'
FWD_PARAMS = ['x', 'g_mix_pre', 'g_mix_post', 'g_ffn_pre', 'g_ffn_post', 'conv_pw1_w', 'conv_pw1_b', 'conv_dw_w', 'conv_dw_b', 'conv_ln_g', 'conv_ln_b', 'conv_pw2_w', 'conv_pw2_b', 'attn_w_in', 'attn_b_f', 'attn_w_o', 'mlp_w_up', 'mlp_w_down']
TWIN_WEIGHTS = ['g_mix_pre', 'g_mix_post', 'g_ffn_pre', 'g_ffn_post', 'conv_pw1_w', 'conv_pw1_b', 'conv_dw_w', 'conv_dw_b', 'conv_ln_g', 'conv_ln_b', 'conv_pw2_w', 'conv_pw2_b', 'attn_w_in', 'attn_b_f', 'attn_w_o', 'mlp_w_up', 'mlp_w_down']
TWIN_DIFF_INPUT = 'x'
TWIN_INPUTS = ['x', 'g_mix_pre', 'g_mix_post', 'g_ffn_pre', 'g_ffn_post', 'conv_pw1_w', 'conv_pw1_b', 'conv_dw_w', 'conv_dw_b', 'conv_ln_g', 'conv_ln_b', 'conv_pw2_w', 'conv_pw2_b', 'attn_w_in', 'attn_b_f', 'attn_w_o', 'mlp_w_up', 'mlp_w_down', 'loss_target', 'm_g_mix_pre', 'm_g_mix_post', 'm_g_ffn_pre', 'm_g_ffn_post', 'm_conv_pw1_w', 'm_conv_pw1_b', 'm_conv_dw_w', 'm_conv_dw_b', 'm_conv_ln_g', 'm_conv_ln_b', 'm_conv_pw2_w', 'm_conv_pw2_b', 'm_attn_w_in', 'm_attn_b_f', 'm_attn_w_o', 'm_mlp_w_up', 'm_mlp_w_down', 'v_g_mix_pre', 'v_g_mix_post', 'v_g_ffn_pre', 'v_g_ffn_post', 'v_conv_pw1_w', 'v_conv_pw1_b', 'v_conv_dw_w', 'v_conv_dw_b', 'v_conv_ln_g', 'v_conv_ln_b', 'v_conv_pw2_w', 'v_conv_pw2_b', 'v_attn_w_in', 'v_attn_b_f', 'v_attn_w_o', 'v_mlp_w_up', 'v_mlp_w_down']
TWIN_OUTPUTS = ['loss', 'grad_x', 'grad_g_mix_pre', 'grad_g_mix_post', 'grad_g_ffn_pre', 'grad_g_ffn_post', 'grad_conv_pw1_w', 'grad_conv_pw1_b', 'grad_conv_dw_w', 'grad_conv_dw_b', 'grad_conv_ln_g', 'grad_conv_ln_b', 'grad_conv_pw2_w', 'grad_conv_pw2_b', 'grad_attn_w_in', 'grad_attn_b_f', 'grad_attn_w_o', 'grad_mlp_w_up', 'grad_mlp_w_down', 'delta_g_mix_pre', 'delta_g_mix_post', 'delta_g_ffn_pre', 'delta_g_ffn_post', 'delta_conv_pw1_w', 'delta_conv_pw1_b', 'delta_conv_dw_w', 'delta_conv_dw_b', 'delta_conv_ln_g', 'delta_conv_ln_b', 'delta_conv_pw2_w', 'delta_conv_pw2_b', 'delta_attn_w_in', 'delta_attn_b_f', 'delta_attn_w_o', 'delta_mlp_w_up', 'delta_mlp_w_down', 'new_m_g_mix_pre', 'new_m_g_mix_post', 'new_m_g_ffn_pre', 'new_m_g_ffn_post', 'new_m_conv_pw1_w', 'new_m_conv_pw1_b', 'new_m_conv_dw_w', 'new_m_conv_dw_b', 'new_m_conv_ln_g', 'new_m_conv_ln_b', 'new_m_conv_pw2_w', 'new_m_conv_pw2_b', 'new_m_attn_w_in', 'new_m_attn_b_f', 'new_m_attn_w_o', 'new_m_mlp_w_up', 'new_m_mlp_w_down', 'new_v_g_mix_pre', 'new_v_g_mix_post', 'new_v_g_ffn_pre', 'new_v_g_ffn_post', 'new_v_conv_pw1_w', 'new_v_conv_pw1_b', 'new_v_conv_dw_w', 'new_v_conv_dw_b', 'new_v_conv_ln_g', 'new_v_conv_ln_b', 'new_v_conv_pw2_w', 'new_v_conv_pw2_b', 'new_v_attn_w_in', 'new_v_attn_b_f', 'new_v_attn_w_o', 'new_v_mlp_w_up', 'new_v_mlp_w_down']
TWIN_LEAF_KINDS = {'loss': 'loss', 'grad_x': 'grad_x', 'grad_g_mix_pre': 'grad_w', 'grad_g_mix_post': 'grad_w', 'grad_g_ffn_pre': 'grad_w', 'grad_g_ffn_post': 'grad_w', 'grad_conv_pw1_w': 'grad_w', 'grad_conv_pw1_b': 'grad_w', 'grad_conv_dw_w': 'grad_w', 'grad_conv_dw_b': 'grad_w', 'grad_conv_ln_g': 'grad_w', 'grad_conv_ln_b': 'grad_w', 'grad_conv_pw2_w': 'grad_w', 'grad_conv_pw2_b': 'grad_w', 'grad_attn_w_in': 'grad_w', 'grad_attn_b_f': 'grad_w', 'grad_attn_w_o': 'grad_w', 'grad_mlp_w_up': 'grad_w', 'grad_mlp_w_down': 'grad_w', 'delta_g_mix_pre': 'delta_w', 'delta_g_mix_post': 'delta_w', 'delta_g_ffn_pre': 'delta_w', 'delta_g_ffn_post': 'delta_w', 'delta_conv_pw1_w': 'delta_w', 'delta_conv_pw1_b': 'delta_w', 'delta_conv_dw_w': 'delta_w', 'delta_conv_dw_b': 'delta_w', 'delta_conv_ln_g': 'delta_w', 'delta_conv_ln_b': 'delta_w', 'delta_conv_pw2_w': 'delta_w', 'delta_conv_pw2_b': 'delta_w', 'delta_attn_w_in': 'delta_w', 'delta_attn_b_f': 'delta_w', 'delta_attn_w_o': 'delta_w', 'delta_mlp_w_up': 'delta_w', 'delta_mlp_w_down': 'delta_w', 'new_m_g_mix_pre': 'new_m', 'new_m_g_mix_post': 'new_m', 'new_m_g_ffn_pre': 'new_m', 'new_m_g_ffn_post': 'new_m', 'new_m_conv_pw1_w': 'new_m', 'new_m_conv_pw1_b': 'new_m', 'new_m_conv_dw_w': 'new_m', 'new_m_conv_dw_b': 'new_m', 'new_m_conv_ln_g': 'new_m', 'new_m_conv_ln_b': 'new_m', 'new_m_conv_pw2_w': 'new_m', 'new_m_conv_pw2_b': 'new_m', 'new_m_attn_w_in': 'new_m', 'new_m_attn_b_f': 'new_m', 'new_m_attn_w_o': 'new_m', 'new_m_mlp_w_up': 'new_m', 'new_m_mlp_w_down': 'new_m', 'new_v_g_mix_pre': 'new_v', 'new_v_g_mix_post': 'new_v', 'new_v_g_ffn_pre': 'new_v', 'new_v_g_ffn_post': 'new_v', 'new_v_conv_pw1_w': 'new_v', 'new_v_conv_pw1_b': 'new_v', 'new_v_conv_dw_w': 'new_v', 'new_v_conv_dw_b': 'new_v', 'new_v_conv_ln_g': 'new_v', 'new_v_conv_ln_b': 'new_v', 'new_v_conv_pw2_w': 'new_v', 'new_v_conv_pw2_b': 'new_v', 'new_v_attn_w_in': 'new_v', 'new_v_attn_b_f': 'new_v', 'new_v_attn_w_o': 'new_v', 'new_v_mlp_w_up': 'new_v', 'new_v_mlp_w_down': 'new_v'}


def _forward(args):
    return _fwd_reference(*[args[k] for k in FWD_PARAMS])


def _output_shape():
    def fwd():
        inp = _fwd_setup_inputs(0)
        return _fwd_reference(*[inp[k] for k in FWD_PARAMS])
    out = _jax.eval_shape(fwd)
    return out.shape, out.dtype

N_MICROBATCH = 1
ADAM_LR = 0.001
ADAM_B1 = 0.9
ADAM_B2 = 0.999
ADAM_EPS = 1e-08
ADAM_WD = 0.01
ADAM_STEP = 10
PER_EXAMPLE_BATCH_AXIS = {'x': 0, 'loss_target': 0}
SHARED_INPUTS = []
_WEIGHT_DTYPES = {'g_mix_pre': _jnp.float32, 'g_mix_post': _jnp.float32, 'g_ffn_pre': _jnp.float32, 'g_ffn_post': _jnp.float32, 'conv_pw1_w': _jnp.float32, 'conv_pw1_b': _jnp.float32, 'conv_dw_w': _jnp.float32, 'conv_dw_b': _jnp.float32, 'conv_ln_g': _jnp.float32, 'conv_ln_b': _jnp.float32, 'conv_pw2_w': _jnp.float32, 'conv_pw2_b': _jnp.float32, 'attn_w_in': _jnp.float32, 'attn_b_f': _jnp.float32, 'attn_w_o': _jnp.float32, 'mlp_w_up': _jnp.float32, 'mlp_w_down': _jnp.float32}
MOMENT_SCALE = {'g_mix_pre': 1.857328e+01, 'g_mix_post': 7.413332e+01, 'g_ffn_pre': 1.332774e+01, 'g_ffn_post': 7.397693e+01, 'conv_pw1_w': 1.143246e+00, 'conv_pw1_b': 4.320198e+01, 'conv_dw_w': 7.595293e+00, 'conv_dw_b': 1.205419e+02, 'conv_ln_g': 4.860528e+01, 'conv_ln_b': 6.881771e+01, 'conv_pw2_w': 2.860314e+01, 'conv_pw2_b': 1.352061e+02, 'attn_w_in': 1.646622e+01, 'attn_b_f': 1.160664e+01, 'attn_w_o': 2.905486e+01, 'mlp_w_up': 6.766376e+00, 'mlp_w_down': 2.930564e+01}


def _to_microbatches(a, axis):
    t = _jnp.moveaxis(a, axis, 0)
    t = t.reshape((N_MICROBATCH, t.shape[0] // N_MICROBATCH) + t.shape[1:])
    return _jnp.moveaxis(t, 1, axis + 1)


def setup_inputs(seed: int = 0) -> dict:
    inp = _fwd_setup_inputs(seed)
    key = _jax.random.fold_in(_jax.random.key(seed), 7919)
    shape, _ = _output_shape()
    out = dict(inp)
    out["loss_target"] = _jax.random.normal(_jax.random.fold_in(key, 0), shape, _jnp.float32)
    for i, name in enumerate(TWIN_WEIGHTS):
        w = inp[name].astype(_jnp.float32)
        if MOMENT_SCALE is None:
            s = _jnp.sqrt(_jnp.mean(_jnp.square(w)) + 1e-30)
        else:
            s = MOMENT_SCALE[name]
        km, kv = _jax.random.split(_jax.random.fold_in(key, i + 1))
        out[name] = w
        out["m_" + name] = s * _jax.random.normal(km, w.shape, _jnp.float32)
        out["v_" + name] = (s * s) * _jax.random.uniform(kv, w.shape, _jnp.float32, 0.5, 1.5)
    if N_MICROBATCH > 1:
        for name, axis in PER_EXAMPLE_BATCH_AXIS.items():
            out[name] = _to_microbatches(out[name], axis)
    return {'x': out['x'], 'g_mix_pre': out['g_mix_pre'], 'g_mix_post': out['g_mix_post'], 'g_ffn_pre': out['g_ffn_pre'], 'g_ffn_post': out['g_ffn_post'], 'conv_pw1_w': out['conv_pw1_w'], 'conv_pw1_b': out['conv_pw1_b'], 'conv_dw_w': out['conv_dw_w'], 'conv_dw_b': out['conv_dw_b'], 'conv_ln_g': out['conv_ln_g'], 'conv_ln_b': out['conv_ln_b'], 'conv_pw2_w': out['conv_pw2_w'], 'conv_pw2_b': out['conv_pw2_b'], 'attn_w_in': out['attn_w_in'], 'attn_b_f': out['attn_b_f'], 'attn_w_o': out['attn_w_o'], 'mlp_w_up': out['mlp_w_up'], 'mlp_w_down': out['mlp_w_down'], 'loss_target': out['loss_target'], 'm_g_mix_pre': out['m_g_mix_pre'], 'm_g_mix_post': out['m_g_mix_post'], 'm_g_ffn_pre': out['m_g_ffn_pre'], 'm_g_ffn_post': out['m_g_ffn_post'], 'm_conv_pw1_w': out['m_conv_pw1_w'], 'm_conv_pw1_b': out['m_conv_pw1_b'], 'm_conv_dw_w': out['m_conv_dw_w'], 'm_conv_dw_b': out['m_conv_dw_b'], 'm_conv_ln_g': out['m_conv_ln_g'], 'm_conv_ln_b': out['m_conv_ln_b'], 'm_conv_pw2_w': out['m_conv_pw2_w'], 'm_conv_pw2_b': out['m_conv_pw2_b'], 'm_attn_w_in': out['m_attn_w_in'], 'm_attn_b_f': out['m_attn_b_f'], 'm_attn_w_o': out['m_attn_w_o'], 'm_mlp_w_up': out['m_mlp_w_up'], 'm_mlp_w_down': out['m_mlp_w_down'], 'v_g_mix_pre': out['v_g_mix_pre'], 'v_g_mix_post': out['v_g_mix_post'], 'v_g_ffn_pre': out['v_g_ffn_pre'], 'v_g_ffn_post': out['v_g_ffn_post'], 'v_conv_pw1_w': out['v_conv_pw1_w'], 'v_conv_pw1_b': out['v_conv_pw1_b'], 'v_conv_dw_w': out['v_conv_dw_w'], 'v_conv_dw_b': out['v_conv_dw_b'], 'v_conv_ln_g': out['v_conv_ln_g'], 'v_conv_ln_b': out['v_conv_ln_b'], 'v_conv_pw2_w': out['v_conv_pw2_w'], 'v_conv_pw2_b': out['v_conv_pw2_b'], 'v_attn_w_in': out['v_attn_w_in'], 'v_attn_b_f': out['v_attn_b_f'], 'v_attn_w_o': out['v_attn_w_o'], 'v_mlp_w_up': out['v_mlp_w_up'], 'v_mlp_w_down': out['v_mlp_w_down']}


def _loss(weights, diff, rest, loss_target):
    with _jax.named_scope("forward"):
        args = {**rest, TWIN_DIFF_INPUT: diff, **{k: w.astype(_WEIGHT_DTYPES[k]) for k, w in weights.items()}}
        y = _forward(args)
    with _jax.named_scope("loss_head"):
        err = _jnp.square(y.astype(_jnp.float32) - loss_target)
        return 0.5 * _jnp.sum(_jnp.mean(err, axis=-1)) if err.ndim else 0.5 * err


def _adamw(w, g, m, v):
    m = ADAM_B1 * m + (1.0 - ADAM_B1) * g
    v = ADAM_B2 * v + (1.0 - ADAM_B2) * _jnp.square(g)
    m_hat = m / (1.0 - ADAM_B1 ** ADAM_STEP)
    v_hat = v / (1.0 - ADAM_B2 ** ADAM_STEP)
    delta = -ADAM_LR * (m_hat / (_jnp.sqrt(v_hat) + ADAM_EPS) + ADAM_WD * w)
    return delta, m, v


def reference(x, g_mix_pre, g_mix_post, g_ffn_pre, g_ffn_post, conv_pw1_w, conv_pw1_b, conv_dw_w, conv_dw_b, conv_ln_g, conv_ln_b, conv_pw2_w, conv_pw2_b, attn_w_in, attn_b_f, attn_w_o, mlp_w_up, mlp_w_down, loss_target, m_g_mix_pre, m_g_mix_post, m_g_ffn_pre, m_g_ffn_post, m_conv_pw1_w, m_conv_pw1_b, m_conv_dw_w, m_conv_dw_b, m_conv_ln_g, m_conv_ln_b, m_conv_pw2_w, m_conv_pw2_b, m_attn_w_in, m_attn_b_f, m_attn_w_o, m_mlp_w_up, m_mlp_w_down, v_g_mix_pre, v_g_mix_post, v_g_ffn_pre, v_g_ffn_post, v_conv_pw1_w, v_conv_pw1_b, v_conv_dw_w, v_conv_dw_b, v_conv_ln_g, v_conv_ln_b, v_conv_pw2_w, v_conv_pw2_b, v_attn_w_in, v_attn_b_f, v_attn_w_o, v_mlp_w_up, v_mlp_w_down):
    given = dict(x=x, g_mix_pre=g_mix_pre, g_mix_post=g_mix_post, g_ffn_pre=g_ffn_pre, g_ffn_post=g_ffn_post, conv_pw1_w=conv_pw1_w, conv_pw1_b=conv_pw1_b, conv_dw_w=conv_dw_w, conv_dw_b=conv_dw_b, conv_ln_g=conv_ln_g, conv_ln_b=conv_ln_b, conv_pw2_w=conv_pw2_w, conv_pw2_b=conv_pw2_b, attn_w_in=attn_w_in, attn_b_f=attn_b_f, attn_w_o=attn_w_o, mlp_w_up=mlp_w_up, mlp_w_down=mlp_w_down, loss_target=loss_target, m_g_mix_pre=m_g_mix_pre, m_g_mix_post=m_g_mix_post, m_g_ffn_pre=m_g_ffn_pre, m_g_ffn_post=m_g_ffn_post, m_conv_pw1_w=m_conv_pw1_w, m_conv_pw1_b=m_conv_pw1_b, m_conv_dw_w=m_conv_dw_w, m_conv_dw_b=m_conv_dw_b, m_conv_ln_g=m_conv_ln_g, m_conv_ln_b=m_conv_ln_b, m_conv_pw2_w=m_conv_pw2_w, m_conv_pw2_b=m_conv_pw2_b, m_attn_w_in=m_attn_w_in, m_attn_b_f=m_attn_b_f, m_attn_w_o=m_attn_w_o, m_mlp_w_up=m_mlp_w_up, m_mlp_w_down=m_mlp_w_down, v_g_mix_pre=v_g_mix_pre, v_g_mix_post=v_g_mix_post, v_g_ffn_pre=v_g_ffn_pre, v_g_ffn_post=v_g_ffn_post, v_conv_pw1_w=v_conv_pw1_w, v_conv_pw1_b=v_conv_pw1_b, v_conv_dw_w=v_conv_dw_w, v_conv_dw_b=v_conv_dw_b, v_conv_ln_g=v_conv_ln_g, v_conv_ln_b=v_conv_ln_b, v_conv_pw2_w=v_conv_pw2_w, v_conv_pw2_b=v_conv_pw2_b, v_attn_w_in=v_attn_w_in, v_attn_b_f=v_attn_b_f, v_attn_w_o=v_attn_w_o, v_mlp_w_up=v_mlp_w_up, v_mlp_w_down=v_mlp_w_down)
    weights = {n: given[n] for n in TWIN_WEIGHTS}
    shared = {n: given[n] for n in SHARED_INPUTS}
    per_example = {n: given[n] for n in ['x']}
    grad_fn = _jax.value_and_grad(_loss, argnums=(0, 1))

    def one_microbatch(ex, loss_target):
        ex = dict(ex)
        diff = ex.pop(TWIN_DIFF_INPUT)
        return grad_fn(weights, diff, {**shared, **ex}, loss_target)

    if N_MICROBATCH == 1:
        loss, (grad_w, grad_x) = one_microbatch(per_example, given["loss_target"])
    else:
        def body(carry, xs):
            loss_sum, grad_sum = carry
            l_k, (gw_k, gx_k) = one_microbatch(xs[0], xs[1])
            with _jax.named_scope("update"):
                return (loss_sum + l_k, _jax.tree.map(_jnp.add, grad_sum, gw_k)), gx_k

        init = (_jnp.zeros((), _jnp.float32), _jax.tree.map(_jnp.zeros_like, weights))
        (loss, grad_w), grad_x = _jax.lax.scan(body, init, (per_example, given["loss_target"]))
    with _jax.named_scope("update"):
        delta_w, new_m, new_v = {}, {}, {}
        for n in TWIN_WEIGHTS:
            delta_w[n], new_m[n], new_v[n] = _adamw(weights[n], grad_w[n], given["m_" + n], given["v_" + n])
    return (loss, grad_x, *[grad_w[n] for n in TWIN_WEIGHTS], *[delta_w[n] for n in TWIN_WEIGHTS],
            *[new_m[n] for n in TWIN_WEIGHTS], *[new_v[n] for n in TWIN_WEIGHTS])
```

```python
import functools

import jax
import jax.numpy as jnp
from jax import lax
from jax.experimental import pallas as pl
from jax.experimental.pallas import tpu as pltpu

F32 = jnp.float32
BF16 = jnp.bfloat16
MESH = pl.DeviceIdType.MESH

D = 1024
DFF = 4096
N_HEADS = 16
HEAD_DIM = 64
CONV_W = 31
HALO = 32
RMS_EPS = 1e-6
LN_EPS = 1e-5
MASK_VALUE = -1e30
W_IN_COLS = 3 * D + N_HEADS
W_IN_PAD = 3200
LANES = 128
N_CHIPS = 4
VMEM_LIMIT = 56 * 1024 * 1024

ADAM_LR = 0.001
ADAM_B1 = 0.9
ADAM_B2 = 0.999
ADAM_EPS = 1e-08
ADAM_WD = 0.01
ADAM_STEP = 10

AUG_F = 64
AUG_ONE = 67
AUG_L = 70

BIG = (
    ("conv_pw1_w", (1024, 512)),
    ("conv_dw_w", (CONV_W, 256)),
    ("conv_pw2_w", (256, 1024)),
    ("attn_w_in", (1024, 772)),
    ("attn_w_o", (256, 1024)),
    ("mlp_w_up", (2, 1024, 1024)),
    ("mlp_w_down", (2, 1024, 1024)),
)
SMALL = (
    ("g_mix_pre", (2, 1024)), ("g_mix_post", (2, 1024)), ("g_ffn_pre", (2, 1024)), ("g_ffn_post", (2, 1024)),
    ("conv_pw1_b", (1, 2048)), ("conv_dw_b", (1, 1024)), ("conv_ln_g", (1, 1024)), ("conv_ln_b", (1, 1024)),
    ("conv_pw2_b", (1, 1024)), ("attn_b_f", (1, 16)),
)
PACK_ROWS = 47360
HALF_ROWS = PACK_ROWS // 2
PACK_BLOCK = 4736
SMALL_ROWS = 120


def _size(shape):
    n = 1
    for s in shape:
        n *= s
    return n


def _pack(arrays, rows):
    flat = jnp.concatenate([a.reshape(-1) for a in arrays])
    return jnp.pad(flat, (0, rows * LANES - flat.shape[0])).reshape(rows, LANES)


def _unpack(packed, table):
    flat = packed.reshape(-1)
    out, off = {}, 0
    for name, shape in table:
        n = _size(shape)
        out[name] = flat[off:off + n].reshape(shape)
        off += n
    return out


def _params(sem):
    return pltpu.CompilerParams(dimension_semantics=sem, vmem_limit_bytes=VMEM_LIMIT)


def _rms(x, g):
    return x * lax.rsqrt(jnp.mean(x * x, axis=-1, keepdims=True) + RMS_EPS) * g


def _split3(v):
    p1 = v.astype(BF16).astype(F32)
    r = v - p1
    p2 = r.astype(BF16).astype(F32)
    p3 = (r - p2).astype(BF16).astype(F32)
    return p1, p2, p3


def _matmul(a, b, mode, out_dtypes, epi, name, extras=(), tm=1024, tn=1024, tk=1024):
    if mode == "nn":
        (M, K), (_, N) = a.shape, b.shape
    elif mode == "nt":
        (M, K), (N, _) = a.shape, b.shape
    else:
        (K, M), (_, N) = a.shape, b.shape
    tm, tn, tk = min(tm, M), min(tn, N), min(tk, K)
    nk = K // tk
    if mode == "tn":
        a_spec = pl.BlockSpec((tk, tm), lambda i, j, k: (k, i))
    else:
        a_spec = pl.BlockSpec((tm, tk), lambda i, j, k: (i, k))
    if mode == "nt":
        b_spec = pl.BlockSpec((tn, tk), lambda i, j, k: (j, k))
    else:
        b_spec = pl.BlockSpec((tk, tn), lambda i, j, k: (k, j))
    dims = {"nn": (((1,), (0,)), ((), ())), "nt": (((1,), (1,)), ((), ())), "tn": (((0,), (0,)), ((), ()))}[mode]
    ex_specs = []
    for _, kind in extras:
        if kind == "row":
            ex_specs.append(pl.BlockSpec((1, tn), lambda i, j, k: (0, j)))
        else:
            ex_specs.append(pl.BlockSpec((tm, tn), lambda i, j, k: (i, j)))
    n_ex, n_out = len(extras), len(out_dtypes)

    def body(*refs):
        a_ref, b_ref = refs[0], refs[1]
        ex = refs[2:2 + n_ex]
        outs = refs[2 + n_ex:2 + n_ex + n_out]
        prod = lax.dot_general(a_ref[...], b_ref[...], dims, preferred_element_type=F32)

        def finish(acc):
            res = epi(acc, *[e[...] for e in ex])
            for o, r in zip(outs, res):
                o[...] = r.astype(o.dtype)

        if nk == 1:
            finish(prod)
        else:
            acc_ref = refs[-1]
            k = pl.program_id(2)

            @pl.when(k == 0)
            def _():
                acc_ref[...] = prod

            @pl.when(k > 0)
            def _():
                acc_ref[...] += prod

            @pl.when(k == nk - 1)
            def _():
                finish(acc_ref[...])

    return pl.pallas_call(
        body,
        name=name,
        grid=(M // tm, N // tn, nk),
        in_specs=[a_spec, b_spec] + ex_specs,
        out_specs=[pl.BlockSpec((tm, tn), lambda i, j, k: (i, j)) for _ in out_dtypes],
        out_shape=[jax.ShapeDtypeStruct((M, N), dt) for dt in out_dtypes],
        scratch_shapes=[pltpu.VMEM((tm, tn), F32)] if nk > 1 else [],
        compiler_params=_params(("parallel", "parallel", "arbitrary")),
    )(a, b, *[e for e, _ in extras])


def _epi_plain(acc):
    return (acc,)


def _epi_bias(acc, bias):
    return (acc + bias,)


def _epi_sqrelu(acc):
    r = jnp.maximum(acc, 0.0)
    return acc, r * r


def _epi_dsqrelu(acc, up):
    return (acc * (2.0 * jnp.maximum(up, 0.0)),)


def _resnorm_fwd(x, m, g_post, g_next, name, tr=512):
    T = x.shape[0]
    has_m = m is not None
    row = pl.BlockSpec((tr, D), lambda i: (i, 0))
    vec = pl.BlockSpec((1, D), lambda i: (0, 0))

    def body(*refs):
        if has_m:
            x_ref, m_ref, gp_ref, gn_ref, x1_ref, h_ref = refs
            x1 = x_ref[...] + _rms(m_ref[...], gp_ref[...])
            x1_ref[...] = x1
        else:
            x_ref, gn_ref, h_ref = refs
            x1 = x_ref[...]
        h_ref[...] = _rms(x1, gn_ref[...]).astype(BF16)

    if has_m:
        args, in_specs = (x, m, g_post, g_next), [row, row, vec, vec]
        out_specs = [row, row]
        out_shape = [jax.ShapeDtypeStruct((T, D), F32), jax.ShapeDtypeStruct((T, D), BF16)]
    else:
        args, in_specs = (x, g_next), [row, vec]
        out_specs = [row]
        out_shape = [jax.ShapeDtypeStruct((T, D), BF16)]
    return pl.pallas_call(
        body, name=name, grid=(T // tr,), in_specs=in_specs, out_specs=out_specs, out_shape=out_shape,
        compiler_params=_params(("parallel",)),
    )(*args)


def _resnorm_loss(x, m, g_post, target, name, tr=512):
    T = x.shape[0]
    row = pl.BlockSpec((tr, D), lambda i: (i, 0))
    vec = pl.BlockSpec((1, D), lambda i: (0, 0))

    def body(x_ref, m_ref, gp_ref, t_ref, dy_ref, loss_ref):
        diff = x_ref[...] + _rms(m_ref[...], gp_ref[...]) - t_ref[...]
        dy_ref[...] = diff * (1.0 / D)

        @pl.when(pl.program_id(0) == 0)
        def _():
            loss_ref[...] = jnp.zeros_like(loss_ref)

        per_token = jnp.mean(diff * diff, axis=-1, keepdims=True)
        loss_ref[...] += 0.5 * jnp.sum(per_token)

    return pl.pallas_call(
        body, name=name, grid=(T // tr,), in_specs=[row, row, vec, row],
        out_specs=[row, pl.BlockSpec((8, LANES), lambda i: (0, 0))],
        out_shape=[jax.ShapeDtypeStruct((T, D), F32), jax.ShapeDtypeStruct((8, LANES), F32)],
        compiler_params=_params(("arbitrary",)),
    )(x, m, g_post, target)


def _norm_bwd(x, g, dy, resid, out_dtype, name, tr=512):
    T = x.shape[0]
    has_r = resid is not None
    row = pl.BlockSpec((tr, D), lambda i: (i, 0))
    vec = pl.BlockSpec((1, D), lambda i: (0, 0))

    def body(*refs):
        if has_r:
            x_ref, g_ref, dy_ref, r_ref, dx_ref, dg_ref, cs_ref = refs
        else:
            x_ref, g_ref, dy_ref, dx_ref, dg_ref, cs_ref = refs
        xv, dyv = x_ref[...], dy_ref[...]
        r = lax.rsqrt(jnp.mean(xv * xv, axis=-1, keepdims=True) + RMS_EPS)
        gy = dyv * g_ref[...]
        c = jnp.sum(gy * xv, axis=-1, keepdims=True) * (1.0 / D)
        dx = r * gy - xv * (r * r * r * c)

        @pl.when(pl.program_id(0) == 0)
        def _():
            dg_ref[...] = jnp.zeros_like(dg_ref)
            cs_ref[...] = jnp.zeros_like(cs_ref)

        dg_ref[...] += jnp.sum(dyv * (xv * r), axis=0, keepdims=True)
        cs_ref[...] += jnp.sum(dx, axis=0, keepdims=True)
        if has_r:
            dx = dx + r_ref[...]
        dx_ref[...] = dx.astype(dx_ref.dtype)

    args = (x, g, dy) + ((resid,) if has_r else ())
    return pl.pallas_call(
        body, name=name, grid=(T // tr,), in_specs=[row, vec, row] + ([row] if has_r else []),
        out_specs=[row, vec, vec],
        out_shape=[jax.ShapeDtypeStruct((T, D), out_dtype), jax.ShapeDtypeStruct((1, D), F32),
                   jax.ShapeDtypeStruct((1, D), F32)],
        compiler_params=_params(("arbitrary",)),
    )(*args)


CONV_TB = 256
CONV_RC = 32


def _glu(a):
    return a[:, :D] * jax.nn.sigmoid(a[:, D:])


def _conv_fwd(a, dw_w, dw_b, ln_g, ln_b, name):
    T = a.shape[0]
    tb = min(CONV_TB, T)
    per = tb // HALO
    vec = pl.BlockSpec((1, D), lambda i: (0, 0))

    def body(cur_ref, prev_ref, w_ref, b_ref, lg_ref, lb_ref, y_ref, z_ref, uwin):
        i = pl.program_id(0)
        uwin[0:HALO, :] = jnp.where(i > 0, _glu(prev_ref[...]), 0.0)
        uwin[HALO:, :] = _glu(cur_ref[...])
        for r in range(tb // CONV_RC):
            acc = jnp.broadcast_to(b_ref[...], (CONV_RC, D))
            for k in range(CONV_W):
                start = r * CONV_RC + HALO - (CONV_W - 1) + k
                acc = acc + uwin[start:start + CONV_RC, :] * w_ref[k:k + 1, :]
            rows = slice(r * CONV_RC, (r + 1) * CONV_RC)
            y_ref[rows, :] = acc
            mu = jnp.mean(acc, axis=-1, keepdims=True)
            xc = acc - mu
            var = jnp.mean(xc * xc, axis=-1, keepdims=True)
            n = xc * lax.rsqrt(var + LN_EPS) * lg_ref[...] + lb_ref[...]
            z_ref[rows, :] = (n * jax.nn.sigmoid(n)).astype(BF16)

    return pl.pallas_call(
        body, name=name, grid=(T // tb,),
        in_specs=[pl.BlockSpec((tb, 2 * D), lambda i: (i, 0)),
                  pl.BlockSpec((HALO, 2 * D), lambda i: (jnp.maximum(i * per - 1, 0), 0)),
                  pl.BlockSpec((HALO, D), lambda i: (0, 0)), vec, vec, vec],
        out_specs=[pl.BlockSpec((tb, D), lambda i: (i, 0)), pl.BlockSpec((tb, D), lambda i: (i, 0))],
        out_shape=[jax.ShapeDtypeStruct((T, D), F32), jax.ShapeDtypeStruct((T, D), BF16)],
        scratch_shapes=[pltpu.VMEM((tb + HALO, D), F32)],
        compiler_params=_params(("parallel",)),
    )(a, a, dw_w, dw_b, ln_g, ln_b)


def _conv_bwd_ln(dz, y, ln_g, ln_b, name, tr=256):
    T = y.shape[0]
    tr = min(tr, T)
    row = pl.BlockSpec((tr, D), lambda i: (i, 0))
    vec = pl.BlockSpec((1, D), lambda i: (0, 0))

    def body(dz_ref, y_ref, lg_ref, lb_ref, dy_ref, dlg_ref, dlb_ref, db_ref):
        yv = y_ref[...]
        mu = jnp.mean(yv, axis=-1, keepdims=True)
        xc = yv - mu
        rstd = lax.rsqrt(jnp.mean(xc * xc, axis=-1, keepdims=True) + LN_EPS)
        yh = xc * rstd
        n = yh * lg_ref[...] + lb_ref[...]
        sg = jax.nn.sigmoid(n)
        dn = dz_ref[...] * (sg * (1.0 + n * (1.0 - sg)))
        dyh = dn * lg_ref[...]
        dyv = rstd * (dyh - jnp.mean(dyh, axis=-1, keepdims=True) - yh * jnp.mean(dyh * yh, axis=-1, keepdims=True))
        dy_ref[...] = dyv

        @pl.when(pl.program_id(0) == 0)
        def _():
            dlg_ref[...] = jnp.zeros_like(dlg_ref)
            dlb_ref[...] = jnp.zeros_like(dlb_ref)
            db_ref[...] = jnp.zeros_like(db_ref)

        dlg_ref[...] += jnp.sum(dn * yh, axis=0, keepdims=True)
        dlb_ref[...] += jnp.sum(dn, axis=0, keepdims=True)
        db_ref[...] += jnp.sum(dyv, axis=0, keepdims=True)

    return pl.pallas_call(
        body, name=name, grid=(T // tr,), in_specs=[row, row, vec, vec], out_specs=[row, vec, vec, vec],
        out_shape=[jax.ShapeDtypeStruct((T, D), F32)] + [jax.ShapeDtypeStruct((1, D), F32)] * 3,
        compiler_params=_params(("arbitrary",)),
    )(dz, y, ln_g, ln_b)


def _conv_bwd_dw(dy, a, dw_w, name):
    T = a.shape[0]
    tb = min(CONV_TB, T)
    per = tb // HALO
    last_halo = T // HALO - 1
    n_steps = T // tb

    def body(dyc_ref, dyn_ref, cur_ref, prev_ref, w_ref, da_ref, dbias_ref, dw_ref, uwin, dywin, dwacc):
        i = pl.program_id(0)

        @pl.when(i == 0)
        def _():
            dbias_ref[...] = jnp.zeros_like(dbias_ref)
            dwacc[...] = jnp.zeros_like(dwacc)

        uwin[0:HALO, :] = jnp.where(i > 0, _glu(prev_ref[...]), 0.0)
        uwin[HALO:, :] = _glu(cur_ref[...])
        dywin[0:tb, :] = dyc_ref[...]
        dywin[tb:, :] = jnp.where(i < n_steps - 1, dyn_ref[...], 0.0)
        for r in range(tb // CONV_RC):
            rows = slice(r * CONV_RC, (r + 1) * CONV_RC)
            dy_c = dywin[rows, :]
            du = jnp.zeros((CONV_RC, D), F32)
            for k in range(CONV_W):
                back = r * CONV_RC + (CONV_W - 1) - k
                du = du + dywin[back:back + CONV_RC, :] * w_ref[k:k + 1, :]
                fwd = r * CONV_RC + HALO - (CONV_W - 1) + k
                prod = dy_c * uwin[fwd:fwd + CONV_RC, :]
                part = prod[0:8, :]
                for q in range(1, CONV_RC // 8):
                    part = part + prod[8 * q:8 * q + 8, :]
                dwacc[8 * k:8 * k + 8, :] += part
            av = cur_ref[rows, :]
            a1, sg = av[:, :D], jax.nn.sigmoid(av[:, D:])
            da1 = du * sg
            da2 = du * a1 * (sg * (1.0 - sg))
            da_ref[rows, 0:D] = da1.astype(BF16)
            da_ref[rows, D:] = da2.astype(BF16)
            dbias_ref[:, 0:D] += jnp.sum(da1, axis=0, keepdims=True)
            dbias_ref[:, D:] += jnp.sum(da2, axis=0, keepdims=True)

        @pl.when(i == n_steps - 1)
        def _():
            for k in range(CONV_W):
                dw_ref[k:k + 1, :] = jnp.sum(dwacc[8 * k:8 * k + 8, :], axis=0, keepdims=True)
            dw_ref[CONV_W:, :] = jnp.zeros((HALO - CONV_W, D), F32)

    return pl.pallas_call(
        body, name=name, grid=(n_steps,),
        in_specs=[pl.BlockSpec((tb, D), lambda i: (i, 0)),
                  pl.BlockSpec((HALO, D), lambda i: (jnp.minimum((i + 1) * per, last_halo), 0)),
                  pl.BlockSpec((tb, 2 * D), lambda i: (i, 0)),
                  pl.BlockSpec((HALO, 2 * D), lambda i: (jnp.maximum(i * per - 1, 0), 0)),
                  pl.BlockSpec((HALO, D), lambda i: (0, 0))],
        out_specs=[pl.BlockSpec((tb, 2 * D), lambda i: (i, 0)), pl.BlockSpec((1, 2 * D), lambda i: (0, 0)),
                   pl.BlockSpec((HALO, D), lambda i: (0, 0))],
        out_shape=[jax.ShapeDtypeStruct((T, 2 * D), BF16), jax.ShapeDtypeStruct((1, 2 * D), F32),
                   jax.ShapeDtypeStruct((HALO, D), F32)],
        scratch_shapes=[pltpu.VMEM((tb + HALO, D), F32), pltpu.VMEM((tb + HALO, D), F32),
                        pltpu.VMEM((8 * HALO, D), F32)],
        compiler_params=_params(("arbitrary",)),
    )(dy, dy, a, a, dw_w)


GATE_TB = 512


def _gate_fwd(fl, b_f, name):
    T = fl.shape[0]
    tb = min(GATE_TB, T)
    row = pl.BlockSpec((tb, LANES), lambda i: (i, 0))

    def body(fl_ref, b_ref, f1_ref, f2_ref, f3_ref, carry):
        @pl.when(pl.program_id(0) == 0)
        def _():
            carry[...] = jnp.zeros_like(carry)

        z = fl_ref[...] + b_ref[...]
        lf = jnp.minimum(z, 0.0) - jnp.log(1.0 + jnp.exp(-jnp.abs(z)))
        tri = (lax.broadcasted_iota(jnp.int32, (tb, tb), 0) >= lax.broadcasted_iota(jnp.int32, (tb, tb), 1)).astype(F32)
        f = jnp.dot(tri, lf, precision=lax.Precision.HIGHEST, preferred_element_type=F32) + carry[...]
        carry[...] = f[tb - 1:tb, :]
        f1_ref[...], f2_ref[...], f3_ref[...] = _split3(f)

    return pl.pallas_call(
        body, name=name, grid=(T // tb,), in_specs=[row, pl.BlockSpec((1, LANES), lambda i: (0, 0))],
        out_specs=[row, row, row], out_shape=[jax.ShapeDtypeStruct((T, LANES), F32)] * 3,
        scratch_shapes=[pltpu.VMEM((1, LANES), F32)],
        compiler_params=_params(("arbitrary",)),
    )(fl, b_f)


def _gate_bwd(dF, fl, b_f, name):
    T = fl.shape[0]
    tb = min(GATE_TB, T)
    nb = T // tb
    row = pl.BlockSpec((tb, LANES), lambda i: (nb - 1 - i, 0))
    vec = pl.BlockSpec((1, LANES), lambda i: (0, 0))

    def body(df_ref, fl_ref, b_ref, dfl_ref, db_ref, carry):
        @pl.when(pl.program_id(0) == 0)
        def _():
            carry[...] = jnp.zeros_like(carry)
            db_ref[...] = jnp.zeros_like(db_ref)

        tri = (lax.broadcasted_iota(jnp.int32, (tb, tb), 0) <= lax.broadcasted_iota(jnp.int32, (tb, tb), 1)).astype(F32)
        rc = jnp.dot(tri, df_ref[...], precision=lax.Precision.HIGHEST, preferred_element_type=F32) + carry[...]
        carry[...] = rc[0:1, :]
        dfl = rc * jax.nn.sigmoid(-(fl_ref[...] + b_ref[...]))
        dfl_ref[...] = dfl
        db_ref[...] += jnp.sum(dfl, axis=0, keepdims=True)

    return pl.pallas_call(
        body, name=name, grid=(nb,), in_specs=[row, row, vec], out_specs=[row, vec],
        out_shape=[jax.ShapeDtypeStruct((T, LANES), F32), jax.ShapeDtypeStruct((1, LANES), F32)],
        scratch_shapes=[pltpu.VMEM((1, LANES), F32)],
        compiler_params=_params(("arbitrary",)),
    )(dF, fl, b_f)


ATT_TB = 512
NT = (((1,), (1,)), ((), ()))
NN = (((1,), (0,)), ((), ()))


def _lane_insert(base, parts, first_lane):
    lane = lax.broadcasted_iota(jnp.int32, base.shape, 1)
    out = base
    for n, p in enumerate(parts):
        out = jnp.where(lane == first_lane + n, p, out)
    return out


def _causal(tb):
    return lax.broadcasted_iota(jnp.int32, (tb, tb), 0) >= lax.broadcasted_iota(jnp.int32, (tb, tb), 1)


def _flash_fwd(q_aug, k_aug, v_aug, name):
    H, T, _ = q_aug.shape
    tb = min(ATT_TB, T)
    blk = pl.BlockSpec((None, tb, LANES), lambda h, i: (h, i, 0))
    whole = pl.BlockSpec((None, T, LANES), lambda h, i: (h, 0, 0))

    def body(q_ref, k_ref, v_ref, o_ref, q2_ref, m_s, l_s, acc_s):
        i = pl.program_id(1)
        q = q_ref[...]
        m_s[...] = jnp.full(m_s.shape, -jnp.inf, F32)
        l_s[...] = jnp.zeros_like(l_s)
        acc_s[...] = jnp.zeros_like(acc_s)

        def step(kb, masked):
            rows = pl.ds(pl.multiple_of(kb * tb, tb), tb)
            s = lax.dot_general(q, k_ref[rows, :], NT, preferred_element_type=F32)
            if masked:
                s = jnp.where(_causal(tb), s, MASK_VALUE)
            m_old = m_s[...]
            m_new = jnp.maximum(m_old, jnp.max(s, axis=-1, keepdims=True))
            p = jnp.exp(s - m_new)
            alpha = jnp.exp(m_old - m_new)
            l_s[...] = alpha * l_s[...] + jnp.sum(p, axis=-1, keepdims=True)
            acc_s[...] = alpha * acc_s[...] + lax.dot_general(p.astype(BF16), v_ref[rows, :], NN,
                                                              preferred_element_type=F32)
            m_s[...] = m_new

        def loop_body(kb, carry):
            step(kb, False)
            return carry

        lax.fori_loop(0, i, loop_body, 0)
        step(i, True)
        l = l_s[...]
        o_ref[...] = acc_s[...] / l
        lse = m_s[...] + jnp.log(l)
        q2_ref[...] = _lane_insert(q.astype(F32), [-p for p in _split3(lse)], AUG_L).astype(BF16)

    return pl.pallas_call(
        body, name=name, grid=(H, T // tb), in_specs=[blk, whole, whole], out_specs=[blk, blk],
        out_shape=[jax.ShapeDtypeStruct((H, T, LANES), F32), jax.ShapeDtypeStruct((H, T, LANES), BF16)],
        scratch_shapes=[pltpu.VMEM((tb, 1), F32), pltpu.VMEM((tb, 1), F32), pltpu.VMEM((tb, LANES), F32)],
        compiler_params=_params(("parallel", "arbitrary")),
    )(q_aug, k_aug, v_aug)


def _attn_prep(do_pad, o, name):
    H, T, _ = o.shape
    tb = min(ATT_TB, T)
    blk = pl.BlockSpec((None, tb, LANES), lambda h, i: (h, i, 0))

    def body(do_ref, o_ref, out_ref):
        dov = do_ref[...]
        delta = jnp.sum(dov * o_ref[...], axis=-1, keepdims=True)
        out_ref[...] = _lane_insert(dov, _split3(delta), AUG_F).astype(BF16)

    return pl.pallas_call(
        body, name=name, grid=(H, T // tb), in_specs=[blk, blk], out_specs=blk,
        out_shape=jax.ShapeDtypeStruct((H, T, LANES), BF16),
        compiler_params=_params(("parallel", "parallel")),
    )(do_pad, o)


def _flash_dq(q2, k_aug, do_aug, v_aug, name):
    H, T, _ = q2.shape
    tb = min(ATT_TB, T)
    blk = pl.BlockSpec((None, tb, LANES), lambda h, i: (h, i, 0))
    whole = pl.BlockSpec((None, T, LANES), lambda h, i: (h, 0, 0))

    def body(q_ref, k_ref, do_ref, v_ref, dq_ref, acc_s):
        i = pl.program_id(1)
        q, dov = q_ref[...], do_ref[...]
        acc_s[...] = jnp.zeros_like(acc_s)

        def step(kb, masked):
            rows = pl.ds(pl.multiple_of(kb * tb, tb), tb)
            k = k_ref[rows, :]
            p = jnp.exp(lax.dot_general(q, k, NT, preferred_element_type=F32))
            if masked:
                p = jnp.where(_causal(tb), p, 0.0)
            ds = p * lax.dot_general(dov, v_ref[rows, :], NT, preferred_element_type=F32)
            acc_s[...] += lax.dot_general(ds.astype(BF16), k, NN, preferred_element_type=F32)

        def loop_body(kb, carry):
            step(kb, False)
            return carry

        lax.fori_loop(0, i, loop_body, 0)
        step(i, True)
        dq_ref[...] = acc_s[...]

    return pl.pallas_call(
        body, name=name, grid=(H, T // tb), in_specs=[blk, whole, blk, whole], out_specs=blk,
        out_shape=jax.ShapeDtypeStruct((H, T, LANES), F32),
        scratch_shapes=[pltpu.VMEM((tb, LANES), F32)],
        compiler_params=_params(("parallel", "arbitrary")),
    )(q2, k_aug, do_aug, v_aug)


def _flash_dkv(q2, k_aug, do_aug, v_aug, name):
    H, T, _ = q2.shape
    tb = min(ATT_TB, T)
    nb = T // tb
    blk = pl.BlockSpec((None, tb, LANES), lambda h, i: (h, i, 0))
    whole = pl.BlockSpec((None, T, LANES), lambda h, i: (h, 0, 0))

    def body(q_ref, k_ref, do_ref, v_ref, dk_ref, dv_ref, dk_s, dv_s):
        i = pl.program_id(1)
        k, v = k_ref[...], v_ref[...]
        dk_s[...] = jnp.zeros_like(dk_s)
        dv_s[...] = jnp.zeros_like(dv_s)

        def step(qb, masked):
            rows = pl.ds(pl.multiple_of(qb * tb, tb), tb)
            q, dov = q_ref[rows, :], do_ref[rows, :]
            pt = jnp.exp(lax.dot_general(k, q, NT, preferred_element_type=F32))
            if masked:
                pt = jnp.where(lax.broadcasted_iota(jnp.int32, (tb, tb), 1) >= lax.broadcasted_iota(jnp.int32, (tb, tb), 0),
                               pt, 0.0)
            dst = pt * lax.dot_general(v, dov, NT, preferred_element_type=F32)
            dv_s[...] += lax.dot_general(pt.astype(BF16), dov, NN, preferred_element_type=F32)
            dk_s[...] += lax.dot_general(dst.astype(BF16), q, NN, preferred_element_type=F32)

        step(i, True)

        def loop_body(qb, carry):
            step(qb, False)
            return carry

        lax.fori_loop(i + 1, nb, loop_body, 0)
        dk_ref[...] = dk_s[...]
        dv_ref[...] = dv_s[...]

    return pl.pallas_call(
        body, name=name, grid=(H, nb), in_specs=[whole, blk, whole, blk], out_specs=[blk, blk],
        out_shape=[jax.ShapeDtypeStruct((H, T, LANES), F32)] * 2,
        scratch_shapes=[pltpu.VMEM((tb, LANES), F32), pltpu.VMEM((tb, LANES), F32)],
        compiler_params=_params(("parallel", "arbitrary")),
    )(q2, k_aug, do_aug, v_aug)


def _add_pair(a, b, name):
    rows = a.shape[0]
    blk = pl.BlockSpec((PACK_BLOCK, LANES), lambda i: (i, 0))

    def body(a_ref, b_ref, o_ref):
        o_ref[...] = a_ref[...] + b_ref[...]

    return pl.pallas_call(
        body, name=name, grid=(rows // PACK_BLOCK,), in_specs=[blk, blk], out_specs=blk,
        out_shape=jax.ShapeDtypeStruct(a.shape, F32), compiler_params=_params(("parallel",)),
    )(a, b)


def _add_slots(b, name):
    n, rows, _ = b.shape
    tb = PACK_BLOCK if rows % PACK_BLOCK == 0 else rows
    specs = [pl.BlockSpec((None, tb, LANES), functools.partial(lambda j, i: (j, i, 0), j)) for j in range(n)]

    def body(*refs):
        acc = refs[0][...]
        for r in refs[1:n]:
            acc = acc + r[...]
        refs[n][...] = acc

    return pl.pallas_call(
        body, name=name, grid=(rows // tb,), in_specs=specs, out_specs=pl.BlockSpec((tb, LANES), lambda i: (i, 0)),
        out_shape=jax.ShapeDtypeStruct((rows, LANES), F32), compiler_params=_params(("parallel",)),
    )(*([b] * n))


def _adamw(w, g, m, v, name):
    rows, cols = w.shape
    tr = 256 if rows % 256 == 0 else rows
    blk = pl.BlockSpec((tr, cols), lambda i: (i, 0))

    def body(w_ref, g_ref, m_ref, v_ref, d_ref, nm_ref, nv_ref):
        gv = g_ref[...]
        nm = ADAM_B1 * m_ref[...] + (1.0 - ADAM_B1) * gv
        nv = ADAM_B2 * v_ref[...] + (1.0 - ADAM_B2) * (gv * gv)
        m_hat = nm / (1.0 - ADAM_B1 ** ADAM_STEP)
        v_hat = nv / (1.0 - ADAM_B2 ** ADAM_STEP)
        d_ref[...] = -ADAM_LR * (m_hat / (jnp.sqrt(v_hat) + ADAM_EPS) + ADAM_WD * w_ref[...])
        nm_ref[...] = nm
        nv_ref[...] = nv

    return pl.pallas_call(
        body, name=name, grid=(rows // tr,), in_specs=[blk] * 4, out_specs=[blk] * 3,
        out_shape=[jax.ShapeDtypeStruct((rows, cols), F32)] * 3, compiler_params=_params(("parallel",)),
    )(w, g, m, v)


ANY = pl.BlockSpec(memory_space=pl.ANY)


def _place():
    x, y, c = lax.axis_index("x"), lax.axis_index("y"), lax.axis_index("c")
    others = [(1 - x, y), (x, 1 - y), (1 - x, 1 - y)]
    return x, y, c, 2 * x + y, others


def _allgather_weights(wpack):
    def body(w_ref, out_ref, send_sems, recv_sems, local_sem):
        x, y, c, me, others = _place()
        sibling = (x, y, 1 - c)

        def half(chip, h):
            return out_ref.at[chip, pl.ds(h * HALF_ROWS, HALF_ROWS), :]

        def copy(k, chip, h, to, src=None):
            return pltpu.make_async_remote_copy(
                src_ref=half(chip, h) if src is None else src, dst_ref=half(chip, h),
                send_sem=send_sems.at[k], recv_sem=recv_sems.at[k], device_id=to, device_id_type=MESH)

        mine = pltpu.make_async_copy(w_ref, out_ref.at[me], local_sem)
        mine.start()
        my_half = w_ref.at[pl.ds(c * HALF_ROWS, HALF_ROWS), :]
        first = [copy(j, me, c, (ox, oy, c), src=my_half) for j, (ox, oy) in enumerate(others)]
        for cp in first:
            cp.start()
        passed = [copy(3 + j, 2 * ox + oy, c, sibling) for j, (ox, oy) in enumerate(others)]
        for j, (ox, oy) in enumerate(others):
            copy(j, 2 * ox + oy, c, (x, y, c)).wait_recv()
            passed[j].start()
        for j, (ox, oy) in enumerate(others):
            copy(3 + j, 2 * ox + oy, 1 - c, (x, y, c)).wait_recv()
        for cp in first + passed:
            cp.wait_send()
        mine.wait()

    return pl.pallas_call(
        body, name="allgather_weights", in_specs=[ANY], out_specs=ANY,
        out_shape=jax.ShapeDtypeStruct((N_CHIPS, PACK_ROWS, LANES), BF16),
        scratch_shapes=[pltpu.SemaphoreType.DMA((6,)), pltpu.SemaphoreType.DMA((6,)), pltpu.SemaphoreType.DMA],
    )(wpack)


def _sibling_exchange(g):
    def body(g_ref, a_ref, send_sems, recv_sems):
        x, y, c, _, _ = _place()
        copies = [pltpu.make_async_remote_copy(
            src_ref=g_ref.at[j, 1 - c], dst_ref=a_ref.at[j], send_sem=send_sems.at[j], recv_sem=recv_sems.at[j],
            device_id=(x, y, 1 - c), device_id_type=MESH) for j in range(N_CHIPS)]
        for cp in copies:
            cp.start()
        for cp in copies:
            cp.wait()

    return pl.pallas_call(
        body, name="grad_sibling_exchange", in_specs=[ANY], out_specs=ANY,
        out_shape=jax.ShapeDtypeStruct((N_CHIPS, HALF_ROWS, LANES), g.dtype),
        scratch_shapes=[pltpu.SemaphoreType.DMA((N_CHIPS,)), pltpu.SemaphoreType.DMA((N_CHIPS,))],
    )(g)


def _chip_exchange(p, small):
    def body(p_ref, s_ref, b_ref, sall_ref, send_sems, recv_sems, ssend_sems, srecv_sems, local_sems):
        x, y, c, me, others = _place()
        dev = 4 * x + 2 * y + c
        own = pltpu.make_async_copy(p_ref.at[me], b_ref.at[me], local_sems.at[0])
        own_small = pltpu.make_async_copy(s_ref, sall_ref.at[dev], local_sems.at[1])
        own.start()
        own_small.start()
        sends = [pltpu.make_async_remote_copy(
            src_ref=p_ref.at[2 * ox + oy], dst_ref=b_ref.at[me], send_sem=send_sems.at[j], recv_sem=recv_sems.at[j],
            device_id=(ox, oy, c), device_id_type=MESH) for j, (ox, oy) in enumerate(others)]
        flips = [(fx, fy, fc) for fx in (0, 1) for fy in (0, 1) for fc in (0, 1)][1:]
        small_sends = [pltpu.make_async_remote_copy(
            src_ref=s_ref, dst_ref=sall_ref.at[dev], send_sem=ssend_sems.at[n], recv_sem=srecv_sems.at[n],
            device_id=(x ^ fx, y ^ fy, c ^ fc), device_id_type=MESH) for n, (fx, fy, fc) in enumerate(flips)]
        for cp in sends + small_sends:
            cp.start()
        for j, (ox, oy) in enumerate(others):
            pltpu.make_async_remote_copy(
                src_ref=p_ref.at[me], dst_ref=b_ref.at[2 * ox + oy], send_sem=send_sems.at[j],
                recv_sem=recv_sems.at[j], device_id=(x, y, c), device_id_type=MESH).wait_recv()
        for n, (fx, fy, fc) in enumerate(flips):
            src_dev = 4 * (x ^ fx) + 2 * (y ^ fy) + (c ^ fc)
            pltpu.make_async_remote_copy(
                src_ref=s_ref, dst_ref=sall_ref.at[src_dev], send_sem=ssend_sems.at[n], recv_sem=srecv_sems.at[n],
                device_id=(x, y, c), device_id_type=MESH).wait_recv()
        for cp in sends + small_sends:
            cp.wait_send()
        own.wait()
        own_small.wait()

    return pl.pallas_call(
        body, name="grad_chip_exchange", in_specs=[ANY, ANY], out_specs=[ANY, ANY],
        out_shape=[jax.ShapeDtypeStruct((N_CHIPS, HALF_ROWS, LANES), p.dtype),
                   jax.ShapeDtypeStruct((8, SMALL_ROWS, LANES), F32)],
        scratch_shapes=[pltpu.SemaphoreType.DMA((3,)), pltpu.SemaphoreType.DMA((3,)), pltpu.SemaphoreType.DMA((7,)),
                        pltpu.SemaphoreType.DMA((7,)), pltpu.SemaphoreType.DMA((2,))],
    )(p, small)


def _sibling_share(q):
    def body(q_ref, out_ref, send_sem, recv_sem, local_sem):
        x, y, c, _, _ = _place()
        own = pltpu.make_async_copy(q_ref, out_ref.at[c], local_sem)
        own.start()
        cp = pltpu.make_async_remote_copy(src_ref=q_ref, dst_ref=out_ref.at[c], send_sem=send_sem, recv_sem=recv_sem,
                                          device_id=(x, y, 1 - c), device_id_type=MESH)
        cp.start()
        pltpu.make_async_remote_copy(src_ref=q_ref, dst_ref=out_ref.at[1 - c], send_sem=send_sem, recv_sem=recv_sem,
                                     device_id=(x, y, c), device_id_type=MESH).wait_recv()
        cp.wait_send()
        own.wait()

    return pl.pallas_call(
        body, name="grad_sibling_share", in_specs=[ANY], out_specs=ANY,
        out_shape=jax.ShapeDtypeStruct((2, HALF_ROWS, LANES), F32),
        scratch_shapes=[pltpu.SemaphoreType.DMA, pltpu.SemaphoreType.DMA, pltpu.SemaphoreType.DMA],
    )(q)


def _full_weights(gathered):
    shards = [_unpack(gathered[j], BIG) for j in range(N_CHIPS)]
    cat = lambda name, axis: jnp.concatenate([s[name] for s in shards], axis=axis)
    return {
        "conv_pw1_w": cat("conv_pw1_w", 1), "conv_dw_w": cat("conv_dw_w", 1), "conv_pw2_w": cat("conv_pw2_w", 0),
        "attn_w_in": cat("attn_w_in", 1), "attn_w_o": cat("attn_w_o", 0),
        "mlp_w_up": cat("mlp_w_up", 2), "mlp_w_down": cat("mlp_w_down", 1),
    }


def _shard_major(grads):
    axis = {"conv_pw1_w": 1, "conv_dw_w": 1, "conv_pw2_w": 0, "attn_w_in": 1, "attn_w_o": 0, "mlp_w_up": 2,
            "mlp_w_down": 1}
    slots = []
    for j in range(N_CHIPS):
        parts = []
        for name, shape in BIG:
            ax = axis[name]
            width = grads[name].shape[ax] // N_CHIPS
            parts.append(lax.slice_in_dim(grads[name], j * width, (j + 1) * width, axis=ax))
        slots.append(_pack(parts, PACK_ROWS))
    return jnp.stack(slots)


def _heads(t):
    return t.reshape(t.shape[0], N_HEADS, HEAD_DIM).transpose(1, 0, 2)


def _unheads(t):
    return t.transpose(1, 0, 2).reshape(t.shape[1], D)


def _per_head(col):
    return col[:, :N_HEADS].T[:, :, None]


def kernel(x, g_mix_pre, g_mix_post, g_ffn_pre, g_ffn_post, conv_pw1_w, conv_pw1_b, conv_dw_w, conv_dw_b, conv_ln_g, conv_ln_b, conv_pw2_w, conv_pw2_b, attn_w_in, attn_b_f, attn_w_o, mlp_w_up, mlp_w_down, loss_target, m_g_mix_pre, m_g_mix_post, m_g_ffn_pre, m_g_ffn_post, m_conv_pw1_w, m_conv_pw1_b, m_conv_dw_w, m_conv_dw_b, m_conv_ln_g, m_conv_ln_b, m_conv_pw2_w, m_conv_pw2_b, m_attn_w_in, m_attn_b_f, m_attn_w_o, m_mlp_w_up, m_mlp_w_down, v_g_mix_pre, v_g_mix_post, v_g_ffn_pre, v_g_ffn_post, v_conv_pw1_w, v_conv_pw1_b, v_conv_dw_w, v_conv_dw_b, v_conv_ln_g, v_conv_ln_b, v_conv_pw2_w, v_conv_pw2_b, v_attn_w_in, v_attn_b_f, v_attn_w_o, v_mlp_w_up, v_mlp_w_down):
    weights = dict(g_mix_pre=g_mix_pre, g_mix_post=g_mix_post, g_ffn_pre=g_ffn_pre, g_ffn_post=g_ffn_post, conv_pw1_w=conv_pw1_w, conv_pw1_b=conv_pw1_b, conv_dw_w=conv_dw_w, conv_dw_b=conv_dw_b, conv_ln_g=conv_ln_g, conv_ln_b=conv_ln_b, conv_pw2_w=conv_pw2_w, conv_pw2_b=conv_pw2_b, attn_w_in=attn_w_in, attn_b_f=attn_b_f, attn_w_o=attn_w_o, mlp_w_up=mlp_w_up, mlp_w_down=mlp_w_down)
    mom_m = dict(g_mix_pre=m_g_mix_pre, g_mix_post=m_g_mix_post, g_ffn_pre=m_g_ffn_pre, g_ffn_post=m_g_ffn_post, conv_pw1_w=m_conv_pw1_w, conv_pw1_b=m_conv_pw1_b, conv_dw_w=m_conv_dw_w, conv_dw_b=m_conv_dw_b, conv_ln_g=m_conv_ln_g, conv_ln_b=m_conv_ln_b, conv_pw2_w=m_conv_pw2_w, conv_pw2_b=m_conv_pw2_b, attn_w_in=m_attn_w_in, attn_b_f=m_attn_b_f, attn_w_o=m_attn_w_o, mlp_w_up=m_mlp_w_up, mlp_w_down=m_mlp_w_down)
    mom_v = dict(g_mix_pre=v_g_mix_pre, g_mix_post=v_g_mix_post, g_ffn_pre=v_g_ffn_pre, g_ffn_post=v_g_ffn_post, conv_pw1_w=v_conv_pw1_w, conv_pw1_b=v_conv_pw1_b, conv_dw_w=v_conv_dw_w, conv_dw_b=v_conv_dw_b, conv_ln_g=v_conv_ln_g, conv_ln_b=v_conv_ln_b, conv_pw2_w=v_conv_pw2_w, conv_pw2_b=v_conv_pw2_b, attn_w_in=v_attn_w_in, attn_b_f=v_attn_b_f, attn_w_o=v_attn_w_o, mlp_w_up=v_mlp_w_up, mlp_w_down=v_mlp_w_down)
    order = [n for n, _ in SMALL[:4]] + ["conv_pw1_w", "conv_pw1_b", "conv_dw_w", "conv_dw_b", "conv_ln_g", "conv_ln_b",
                                          "conv_pw2_w", "conv_pw2_b", "attn_w_in", "attn_b_f", "attn_w_o", "mlp_w_up",
                                          "mlp_w_down"]
    T = x.shape[1]
    x0 = x.reshape(T, D)
    target = loss_target.reshape(T, D)
    row = lambda a, l: a[l:l + 1, :]

    wpack = _pack([weights[n].astype(BF16) for n, _ in BIG], PACK_ROWS)
    W = _full_weights(_allgather_weights(wpack))
    w_pw1, w_pw2, w_o = W["conv_pw1_w"], W["conv_pw2_w"], W["attn_w_o"]
    w_in = jnp.pad(W["attn_w_in"], ((0, 0), (0, W_IN_PAD - W_IN_COLS)))
    w_dw = jnp.pad(W["conv_dw_w"].astype(F32), ((0, HALO - CONV_W), (0, 0)))
    w_up, w_down = W["mlp_w_up"], W["mlp_w_down"]
    b_f = jnp.pad(attn_b_f, ((0, 0), (0, LANES - N_HEADS)))

    def mlp_fwd(h, l):
        up, act = _matmul(h, w_up[l], "nn", (F32, BF16), _epi_sqrelu, f"mlp{l}_up")
        (mo,) = _matmul(act, w_down[l], "nn", (F32,), _epi_plain, f"mlp{l}_down")
        return up, act, mo

    (h0,) = _resnorm_fwd(x0, None, None, row(g_mix_pre, 0), "norm_in")
    (a0,) = _matmul(h0, w_pw1, "nn", (F32,), _epi_bias, "conv_pw1", extras=((conv_pw1_b, "row"),))
    y0, z0 = _conv_fwd(a0, w_dw, conv_dw_b, conv_ln_g, conv_ln_b, "conv_fwd")
    (m0,) = _matmul(z0, w_pw2, "nn", (F32,), _epi_bias, "conv_pw2", extras=((conv_pw2_b, "row"),))
    x1, h1 = _resnorm_fwd(x0, m0, row(g_mix_post, 0), row(g_ffn_pre, 0), "norm_conv_out")
    up0, act0, mo0 = mlp_fwd(h1, 0)
    x2, h2 = _resnorm_fwd(x1, mo0, row(g_ffn_post, 0), row(g_mix_pre, 1), "norm_mlp0_out")

    (proj,) = _matmul(h2, w_in, "nn", (F32,), _epi_plain, "attn_in", tn=640)
    fl = proj[:, 3 * D:]
    f1, f2, f3 = _gate_fwd(fl, b_f, "gate_fwd")
    fparts = [_per_head(f).astype(BF16) for f in (f1, f2, f3)]
    ones = jnp.ones((N_HEADS, T, 3), BF16)

    def aug(data, *cols):
        body = jnp.concatenate([data] + list(cols), axis=-1)
        return jnp.pad(body, ((0, 0), (0, 0), (0, LANES - body.shape[-1])))

    q_aug = aug(_heads(proj[:, :D] * 0.125).astype(BF16), *fparts, ones)
    k_aug = aug(_heads(proj[:, D:2 * D]).astype(BF16), ones, *[-f for f in fparts], ones)
    v_aug = aug(_heads(proj[:, 2 * D:3 * D]).astype(BF16), -ones)
    o_aug, q2_aug = _flash_fwd(q_aug, k_aug, v_aug, "flash_fwd")
    o_bf = _unheads(o_aug[:, :, :HEAD_DIM]).astype(BF16)
    (m1,) = _matmul(o_bf, w_o, "nn", (F32,), _epi_plain, "attn_out")
    x3, h3 = _resnorm_fwd(x2, m1, row(g_mix_post, 1), row(g_ffn_pre, 1), "norm_attn_out")
    up1, act1, mo1 = mlp_fwd(h3, 1)
    dy, loss_part = _resnorm_loss(x3, mo1, row(g_ffn_post, 1), target, "loss")
    loss = lax.psum(loss_part[0, 0], ("x", "y", "c"))

    G = {}

    def mlp_bwd(d_out, mo, up, act, h, x_in, l):
        d_mo, dg_post, _ = _norm_bwd(mo, row(g_ffn_post, l), d_out, None, BF16, f"mlp{l}_post_bwd")
        (d_up,) = _matmul(d_mo, w_down[l], "nt", (BF16,), _epi_dsqrelu, f"mlp{l}_dact", extras=((up, "tile"),))
        (dw_down,) = _matmul(act, d_mo, "tn", (F32,), _epi_plain, f"mlp{l}_dwdown")
        (dw_up,) = _matmul(h, d_up, "tn", (F32,), _epi_plain, f"mlp{l}_dwup")
        (d_h,) = _matmul(d_up, w_up[l], "nt", (F32,), _epi_plain, f"mlp{l}_dh")
        d_in, dg_pre, _ = _norm_bwd(x_in, row(g_ffn_pre, l), d_h, d_out, F32, f"mlp{l}_pre_bwd")
        return d_in, dg_post, dg_pre, dw_up, dw_down

    d_x3, dg_ffn_post1, dg_ffn_pre1, dw_up1, dw_down1 = mlp_bwd(dy, mo1, up1, act1, h3, x3, 1)

    d_m1, dg_mix_post1, _ = _norm_bwd(m1, row(g_mix_post, 1), d_x3, None, BF16, "attn_post_bwd")
    (G["attn_w_o"],) = _matmul(o_bf, d_m1, "tn", (F32,), _epi_plain, "attn_dwo")
    (d_o,) = _matmul(d_m1, w_o, "nt", (F32,), _epi_plain, "attn_do")
    do_pad = jnp.pad(_heads(d_o), ((0, 0), (0, 0), (0, LANES - HEAD_DIM)))
    do_aug = _attn_prep(do_pad, o_aug, "attn_prep")
    dq_aug = _flash_dq(q2_aug, k_aug, do_aug, v_aug, "flash_dq")
    dk_aug, dv_aug = _flash_dkv(q2_aug, k_aug, do_aug, v_aug, "flash_dkv")
    dF = (dq_aug[:, :, AUG_F] - dk_aug[:, :, AUG_ONE]).T
    d_fl, db_f = _gate_bwd(jnp.pad(dF, ((0, 0), (0, LANES - N_HEADS))), fl, b_f, "gate_bwd")
    d_proj = jnp.concatenate([
        _unheads(dq_aug[:, :, :HEAD_DIM] * 0.125), _unheads(dk_aug[:, :, :HEAD_DIM]), _unheads(dv_aug[:, :, :HEAD_DIM]),
        d_fl], axis=1).astype(BF16)
    (dw_in,) = _matmul(h2, d_proj, "tn", (F32,), _epi_plain, "attn_dwin", tn=640)
    G["attn_w_in"] = dw_in[:, :W_IN_COLS]
    (d_h2,) = _matmul(d_proj, w_in, "nt", (F32,), _epi_plain, "attn_dh", tk=640)
    d_x2, dg_mix_pre1, _ = _norm_bwd(x2, row(g_mix_pre, 1), d_h2, d_x3, F32, "attn_pre_bwd")

    d_x1, dg_ffn_post0, dg_ffn_pre0, dw_up0, dw_down0 = mlp_bwd(d_x2, mo0, up0, act0, h1, x1, 0)

    d_m0, dg_mix_post0, db_pw2 = _norm_bwd(m0, row(g_mix_post, 0), d_x1, None, BF16, "conv_post_bwd")
    (G["conv_pw2_w"],) = _matmul(z0, d_m0, "tn", (F32,), _epi_plain, "conv_dwpw2")
    (d_z0,) = _matmul(d_m0, w_pw2, "nt", (F32,), _epi_plain, "conv_dz")
    d_y0, dln_g, dln_b, ddw_b = _conv_bwd_ln(d_z0, y0, conv_ln_g, conv_ln_b, "conv_bwd_ln")
    d_a0, db_pw1, ddw_w = _conv_bwd_dw(d_y0, a0, w_dw, "conv_bwd_dw")
    (G["conv_pw1_w"],) = _matmul(h0, d_a0, "tn", (F32,), _epi_plain, "conv_dwpw1")
    (d_h0,) = _matmul(d_a0, w_pw1, "nt", (F32,), _epi_plain, "conv_dh")
    d_x0, dg_mix_pre0, _ = _norm_bwd(x0, row(g_mix_pre, 0), d_h0, d_x1, F32, "conv_pre_bwd")

    G["conv_dw_w"] = ddw_w[:CONV_W, :]
    G["mlp_w_up"] = jnp.stack([dw_up0, dw_up1])
    G["mlp_w_down"] = jnp.stack([dw_down0, dw_down1])
    small_local = {
        "g_mix_pre": jnp.concatenate([dg_mix_pre0, dg_mix_pre1]), "g_mix_post": jnp.concatenate([dg_mix_post0, dg_mix_post1]),
        "g_ffn_pre": jnp.concatenate([dg_ffn_pre0, dg_ffn_pre1]), "g_ffn_post": jnp.concatenate([dg_ffn_post0, dg_ffn_post1]),
        "conv_pw1_b": db_pw1, "conv_dw_b": ddw_b, "conv_ln_g": dln_g, "conv_ln_b": dln_b, "conv_pw2_b": db_pw2,
        "attn_b_f": db_f[:, :N_HEADS],
    }

    c = lax.axis_index("c")
    g_all = _shard_major(G).reshape(N_CHIPS, 2, HALF_ROWS, LANES)
    from_sibling = _sibling_exchange(g_all)
    mine = lax.dynamic_index_in_dim(g_all, c, axis=1, keepdims=False)
    pair = _add_pair(mine.reshape(N_CHIPS * HALF_ROWS, LANES), from_sibling.reshape(N_CHIPS * HALF_ROWS, LANES),
                     "grad_add_pair").reshape(N_CHIPS, HALF_ROWS, LANES)
    by_chip, small_all = _chip_exchange(pair, _pack([small_local[n] for n, _ in SMALL], SMALL_ROWS))
    half_sum = _add_slots(by_chip, "grad_add_chips")
    grads = _unpack(_sibling_share(half_sum).reshape(PACK_ROWS, LANES), BIG)
    grads.update(_unpack(_add_slots(small_all, "grad_add_small"), SMALL))

    delta, new_m, new_v = {}, {}, {}
    for name, shape in BIG:
        two_d = (shape[-2] * (shape[0] if len(shape) == 3 else 1), shape[-1])
        d, nm, nv = _adamw(weights[name].reshape(two_d), grads[name].reshape(two_d), mom_m[name].reshape(two_d),
                           mom_v[name].reshape(two_d), f"adamw_{name}")
        full = weights[name].shape
        grads[name] = grads[name].reshape(full)
        delta[name], new_m[name], new_v[name] = d.reshape(full), nm.reshape(full), nv.reshape(full)
    packs = [_pack([src[n] for n, _ in SMALL], SMALL_ROWS) for src in (weights, grads, mom_m, mom_v)]
    d, nm, nv = _adamw(*packs, "adamw_small")
    for dst, packed in ((delta, d), (new_m, nm), (new_v, nv)):
        dst.update(_unpack(packed, SMALL))

    grad_x = d_x0.reshape(x.shape)
    return (loss, grad_x, *[grads[n] for n in order], *[delta[n] for n in order], *[new_m[n] for n in order],
            *[new_v[n] for n in order])
```

```python
import functools

import jax
import jax.numpy as jnp
from jax import lax
from jax.experimental import pallas as pl
from jax.experimental.pallas import tpu as pltpu

F32 = jnp.float32
BF16 = jnp.bfloat16
MESH = pl.DeviceIdType.MESH

D = 1024
DFF = 4096
N_HEADS = 16
HEAD_DIM = 64
CONV_W = 31
HALO = 32
RMS_EPS = 1e-6
LN_EPS = 1e-5
MASK_VALUE = -1e30
W_IN_COLS = 3 * D + N_HEADS
W_IN_PAD = 3200
LANES = 128
N_CHIPS = 4
VMEM_LIMIT = 56 * 1024 * 1024

ADAM_LR = 0.001
ADAM_B1 = 0.9
ADAM_B2 = 0.999
ADAM_EPS = 1e-08
ADAM_WD = 0.01
ADAM_STEP = 10

AUG_F = 64
AUG_ONE = 67
AUG_L = 70
SKIP_BELOW = -104.0

GATHER_UNITS = (
    ("conv_pw1_w", (1024, 512)),
    ("conv_dw_w", (HALO, 256)),
    ("conv_pw2_w", (256, 1024)),
    ("attn_w_in", (1024, 772)),
    ("attn_w_o", (256, 1024)),
    ("mlp_w_up", (2, 1024, 1024)),
    ("mlp_w_down", (2, 1024, 1024)),
)
GRAD_UNITS = (
    ("conv_pw1_w", (1024, 512), ("conv_pw1_w", None)),
    ("conv_pw2_w", (256, 1024), ("conv_pw2_w", None)),
    ("attn_w_in", (1024, 772), ("attn_w_in", None)),
    ("attn_w_o", (256, 1024), ("attn_w_o", None)),
    ("up0", (1024, 1024), ("mlp_w_up", 0)),
    ("up1", (1024, 1024), ("mlp_w_up", 1)),
    ("down0", (1024, 1024), ("mlp_w_down", 0)),
    ("down1", (1024, 1024), ("mlp_w_down", 1)),
)
GRAD_OUT = (("conv_pw1_w", (1024, 512)), ("conv_pw2_w", (256, 1024)), ("attn_w_in", (1024, 772)),
            ("attn_w_o", (256, 1024)), ("mlp_w_up", (2, 1024, 1024)), ("mlp_w_down", (2, 1024, 1024)))
SMALL = (
    ("g_mix_pre", (2, 1024)), ("g_mix_post", (2, 1024)), ("g_ffn_pre", (2, 1024)), ("g_ffn_post", (2, 1024)),
    ("conv_pw1_b", (1, 2048)), ("conv_dw_b", (1, 1024)), ("conv_ln_g", (1, 1024)), ("conv_ln_b", (1, 1024)),
    ("conv_pw2_b", (1, 1024)), ("attn_b_f", (1, 16)),
)
SMALL_ROWS = 120
SMALL_ALL = SMALL + (("conv_dw_w_full", (HALO, 1024)),)
SMALL_ALL_ROWS = 376


def _size(shape):
    n = 1
    for s in shape:
        n *= s
    return n


def _pack(arrays, rows):
    flat = jnp.concatenate([a.reshape(-1) for a in arrays])
    return jnp.pad(flat, (0, rows * LANES - flat.shape[0])).reshape(rows, LANES)


def _unpack(packed, table):
    flat = packed.reshape(-1)
    out, off = {}, 0
    for name, shape in table:
        n = _size(shape)
        out[name] = flat[off:off + n].reshape(shape)
        off += n
    return out


def _params(sem):
    return pltpu.CompilerParams(dimension_semantics=sem, vmem_limit_bytes=VMEM_LIMIT)


def _rms(x, g):
    return x * lax.rsqrt(jnp.mean(x * x, axis=-1, keepdims=True) + RMS_EPS) * g


def _split3(v):
    p1 = v.astype(BF16).astype(F32)
    r = v - p1
    p2 = r.astype(BF16).astype(F32)
    p3 = (r - p2).astype(BF16).astype(F32)
    return p1, p2, p3


def _matmul(a, b, mode, out_dtypes, epi, name, extras=(), tm=1024, tn=1024, tk=1024, n=None, b_place=None,
            shard_out=False):
    (K, M) = a.shape if mode == "tn" else a.shape[::-1]
    if n is None:
        N = b.shape[0] if mode == "nt" else b.shape[1]
    else:
        N = n
    tm, tn, tk = min(tm, M), min(tn, N), min(tk, K)
    nk = K // tk
    if mode == "tn":
        a_spec = pl.BlockSpec((tk, tm), lambda i, j, k: (k, i))
    else:
        a_spec = pl.BlockSpec((tm, tk), lambda i, j, k: (i, k))
    if b_place is not None:
        b_spec = b_place(tk, tn)
    elif mode == "nt":
        b_spec = pl.BlockSpec((tn, tk), lambda i, j, k: (j, k))
    else:
        b_spec = pl.BlockSpec((tk, tn), lambda i, j, k: (k, j))
    if shard_out:
        per = N // N_CHIPS // tn
        o_spec = pl.BlockSpec((None, tm, tn), lambda i, j, k: (j // per, i, j % per))
        o_shape = (N_CHIPS, M, N // N_CHIPS)
    else:
        o_spec = pl.BlockSpec((tm, tn), lambda i, j, k: (i, j))
        o_shape = (M, N)
    dims = {"nn": (((1,), (0,)), ((), ())), "nt": (((1,), (1,)), ((), ())), "tn": (((0,), (0,)), ((), ()))}[mode]
    ex_specs = []
    for _, kind in extras:
        if kind == "row":
            ex_specs.append(pl.BlockSpec((1, tn), lambda i, j, k: (0, j)))
        else:
            ex_specs.append(pl.BlockSpec((tm, tn), lambda i, j, k: (i, j)))
    n_ex, n_out = len(extras), len(out_dtypes)

    def body(*refs):
        a_ref, b_ref = refs[0], refs[1]
        ex = refs[2:2 + n_ex]
        outs = refs[2 + n_ex:2 + n_ex + n_out]
        prod = lax.dot_general(a_ref[...], b_ref[...], dims, preferred_element_type=F32)

        def finish(acc):
            res = epi(acc, *[e[...] for e in ex])
            for o, r in zip(outs, res):
                o[...] = r.astype(o.dtype)

        if nk == 1:
            finish(prod)
        else:
            acc_ref = refs[-1]
            k = pl.program_id(2)

            @pl.when(k == 0)
            def _():
                acc_ref[...] = prod

            @pl.when(k > 0)
            def _():
                acc_ref[...] += prod

            @pl.when(k == nk - 1)
            def _():
                finish(acc_ref[...])

    return pl.pallas_call(
        body,
        name=name,
        grid=(M // tm, N // tn, nk),
        in_specs=[a_spec, b_spec] + ex_specs,
        out_specs=[o_spec for _ in out_dtypes],
        out_shape=[jax.ShapeDtypeStruct(o_shape, dt) for dt in out_dtypes],
        scratch_shapes=[pltpu.VMEM((tm, tn), F32)] if nk > 1 else [],
        compiler_params=_params(("parallel", "parallel", "arbitrary")),
    )(a, b, *[e for e, _ in extras])


def _place_cols_nn(width, layer=None):
    def place(tk, tn):
        per = width // tn
        if layer is None:
            return pl.BlockSpec((None, tk, tn), lambda i, j, k: (j // per, k, j % per))
        return pl.BlockSpec((None, None, tk, tn), lambda i, j, k: (j // per, layer, k, j % per))
    return place


def _place_cols_nt(width, layer=None):
    def place(tk, tn):
        per = width // tk
        if layer is None:
            return pl.BlockSpec((None, tn, tk), lambda i, j, k: (k // per, j, k % per))
        return pl.BlockSpec((None, None, tn, tk), lambda i, j, k: (k // per, layer, j, k % per))
    return place


def _place_rows_nn(height, layer):
    def place(tk, tn):
        per = height // tk
        return pl.BlockSpec((None, None, tk, tn), lambda i, j, k: (k // per, layer, k % per, j))
    return place


def _place_rows_nt(height, layer):
    def place(tk, tn):
        per = height // tn
        return pl.BlockSpec((None, None, tn, tk), lambda i, j, k: (j // per, layer, j % per, k))
    return place


def _epi_plain(acc):
    return (acc,)


def _epi_bias(acc, bias):
    return (acc + bias,)


def _epi_sqrelu(acc):
    r = jnp.maximum(acc, 0.0)
    return acc, r * r


def _epi_dsqrelu(acc, up):
    return (acc * (2.0 * jnp.maximum(up, 0.0)),)


def _resnorm_fwd(x, m, g_post, g_next, name, tr=512):
    T = x.shape[0]
    has_m = m is not None
    row = pl.BlockSpec((tr, D), lambda i: (i, 0))
    vec = pl.BlockSpec((1, D), lambda i: (0, 0))

    def body(*refs):
        if has_m:
            x_ref, m_ref, gp_ref, gn_ref, x1_ref, h_ref = refs
            x1 = x_ref[...] + _rms(m_ref[...], gp_ref[...])
            x1_ref[...] = x1
        else:
            x_ref, gn_ref, h_ref = refs
            x1 = x_ref[...]
        h_ref[...] = _rms(x1, gn_ref[...]).astype(BF16)

    if has_m:
        args, in_specs = (x, m, g_post, g_next), [row, row, vec, vec]
        out_specs = [row, row]
        out_shape = [jax.ShapeDtypeStruct((T, D), F32), jax.ShapeDtypeStruct((T, D), BF16)]
    else:
        args, in_specs = (x, g_next), [row, vec]
        out_specs = [row]
        out_shape = [jax.ShapeDtypeStruct((T, D), BF16)]
    return pl.pallas_call(
        body, name=name, grid=(T // tr,), in_specs=in_specs, out_specs=out_specs, out_shape=out_shape,
        compiler_params=_params(("parallel",)),
    )(*args)


def _resnorm_loss(x, m, g_post, target, name, tr=512):
    T = x.shape[0]
    row = pl.BlockSpec((tr, D), lambda i: (i, 0))
    vec = pl.BlockSpec((1, D), lambda i: (0, 0))

    def body(x_ref, m_ref, gp_ref, t_ref, dy_ref, loss_ref):
        diff = x_ref[...] + _rms(m_ref[...], gp_ref[...]) - t_ref[...]
        dy_ref[...] = diff * (1.0 / D)

        @pl.when(pl.program_id(0) == 0)
        def _():
            loss_ref[...] = jnp.zeros_like(loss_ref)

        per_token = jnp.mean(diff * diff, axis=-1, keepdims=True)
        loss_ref[...] += 0.5 * jnp.sum(per_token)

    return pl.pallas_call(
        body, name=name, grid=(T // tr,), in_specs=[row, row, vec, row],
        out_specs=[row, pl.BlockSpec((8, LANES), lambda i: (0, 0))],
        out_shape=[jax.ShapeDtypeStruct((T, D), F32), jax.ShapeDtypeStruct((8, LANES), F32)],
        compiler_params=_params(("arbitrary",)),
    )(x, m, g_post, target)


def _norm_bwd(x, g, dy, resid, out_dtype, name, tr=512):
    T = x.shape[0]
    has_r = resid is not None
    row = pl.BlockSpec((tr, D), lambda i: (i, 0))
    vec = pl.BlockSpec((1, D), lambda i: (0, 0))

    def body(*refs):
        if has_r:
            x_ref, g_ref, dy_ref, r_ref, dx_ref, dg_ref, cs_ref = refs
        else:
            x_ref, g_ref, dy_ref, dx_ref, dg_ref, cs_ref = refs
        xv, dyv = x_ref[...], dy_ref[...]
        r = lax.rsqrt(jnp.mean(xv * xv, axis=-1, keepdims=True) + RMS_EPS)
        gy = dyv * g_ref[...]
        c = jnp.sum(gy * xv, axis=-1, keepdims=True) * (1.0 / D)
        dx = r * gy - xv * (r * r * r * c)

        @pl.when(pl.program_id(0) == 0)
        def _():
            dg_ref[...] = jnp.zeros_like(dg_ref)
            cs_ref[...] = jnp.zeros_like(cs_ref)

        dg_ref[...] += jnp.sum(dyv * (xv * r), axis=0, keepdims=True)
        cs_ref[...] += jnp.sum(dx, axis=0, keepdims=True)
        if has_r:
            dx = dx + r_ref[...]
        dx_ref[...] = dx.astype(dx_ref.dtype)

    args = (x, g, dy) + ((resid,) if has_r else ())
    return pl.pallas_call(
        body, name=name, grid=(T // tr,), in_specs=[row, vec, row] + ([row] if has_r else []),
        out_specs=[row, vec, vec],
        out_shape=[jax.ShapeDtypeStruct((T, D), out_dtype), jax.ShapeDtypeStruct((1, D), F32),
                   jax.ShapeDtypeStruct((1, D), F32)],
        compiler_params=_params(("arbitrary",)),
    )(*args)


CONV_TB = 256
CONV_RC = 32


def _glu(a):
    return a[:, :D] * jax.nn.sigmoid(a[:, D:])


def _conv_fwd(a, dw_w, dw_b, ln_g, ln_b, name):
    T = a.shape[0]
    tb = min(CONV_TB, T)
    per = tb // HALO
    vec = pl.BlockSpec((1, D), lambda i: (0, 0))

    def body(cur_ref, prev_ref, w_ref, b_ref, lg_ref, lb_ref, y_ref, z_ref, uwin):
        i = pl.program_id(0)
        uwin[0:HALO, :] = jnp.where(i > 0, _glu(prev_ref[...]), 0.0)
        uwin[HALO:, :] = _glu(cur_ref[...])
        for r in range(tb // CONV_RC):
            acc = jnp.broadcast_to(b_ref[...], (CONV_RC, D))
            for k in range(CONV_W):
                start = r * CONV_RC + HALO - (CONV_W - 1) + k
                acc = acc + uwin[start:start + CONV_RC, :] * w_ref[k:k + 1, :]
            rows = slice(r * CONV_RC, (r + 1) * CONV_RC)
            y_ref[rows, :] = acc
            mu = jnp.mean(acc, axis=-1, keepdims=True)
            xc = acc - mu
            var = jnp.mean(xc * xc, axis=-1, keepdims=True)
            n = xc * lax.rsqrt(var + LN_EPS) * lg_ref[...] + lb_ref[...]
            z_ref[rows, :] = (n * jax.nn.sigmoid(n)).astype(BF16)

    return pl.pallas_call(
        body, name=name, grid=(T // tb,),
        in_specs=[pl.BlockSpec((tb, 2 * D), lambda i: (i, 0)),
                  pl.BlockSpec((HALO, 2 * D), lambda i: (jnp.maximum(i * per - 1, 0), 0)),
                  pl.BlockSpec((HALO, D), lambda i: (0, 0)), vec, vec, vec],
        out_specs=[pl.BlockSpec((tb, D), lambda i: (i, 0)), pl.BlockSpec((tb, D), lambda i: (i, 0))],
        out_shape=[jax.ShapeDtypeStruct((T, D), F32), jax.ShapeDtypeStruct((T, D), BF16)],
        scratch_shapes=[pltpu.VMEM((tb + HALO, D), F32)],
        compiler_params=_params(("parallel",)),
    )(a, a, dw_w, dw_b, ln_g, ln_b)


def _conv_bwd_ln(dz, y, ln_g, ln_b, name, tr=256):
    T = y.shape[0]
    tr = min(tr, T)
    row = pl.BlockSpec((tr, D), lambda i: (i, 0))
    vec = pl.BlockSpec((1, D), lambda i: (0, 0))

    def body(dz_ref, y_ref, lg_ref, lb_ref, dy_ref, dlg_ref, dlb_ref, db_ref):
        yv = y_ref[...]
        mu = jnp.mean(yv, axis=-1, keepdims=True)
        xc = yv - mu
        rstd = lax.rsqrt(jnp.mean(xc * xc, axis=-1, keepdims=True) + LN_EPS)
        yh = xc * rstd
        n = yh * lg_ref[...] + lb_ref[...]
        sg = jax.nn.sigmoid(n)
        dn = dz_ref[...] * (sg * (1.0 + n * (1.0 - sg)))
        dyh = dn * lg_ref[...]
        dyv = rstd * (dyh - jnp.mean(dyh, axis=-1, keepdims=True) - yh * jnp.mean(dyh * yh, axis=-1, keepdims=True))
        dy_ref[...] = dyv

        @pl.when(pl.program_id(0) == 0)
        def _():
            dlg_ref[...] = jnp.zeros_like(dlg_ref)
            dlb_ref[...] = jnp.zeros_like(dlb_ref)
            db_ref[...] = jnp.zeros_like(db_ref)

        dlg_ref[...] += jnp.sum(dn * yh, axis=0, keepdims=True)
        dlb_ref[...] += jnp.sum(dn, axis=0, keepdims=True)
        db_ref[...] += jnp.sum(dyv, axis=0, keepdims=True)

    return pl.pallas_call(
        body, name=name, grid=(T // tr,), in_specs=[row, row, vec, vec], out_specs=[row, vec, vec, vec],
        out_shape=[jax.ShapeDtypeStruct((T, D), F32)] + [jax.ShapeDtypeStruct((1, D), F32)] * 3,
        compiler_params=_params(("arbitrary",)),
    )(dz, y, ln_g, ln_b)


def _conv_bwd_dw(dy, a, dw_w, name):
    T = a.shape[0]
    tb = min(CONV_TB, T)
    per = tb // HALO
    last_halo = T // HALO - 1
    n_steps = T // tb

    def body(dyc_ref, dyn_ref, cur_ref, prev_ref, w_ref, da_ref, dbias_ref, dw_ref, uwin, dywin, dwacc):
        i = pl.program_id(0)

        @pl.when(i == 0)
        def _():
            dbias_ref[...] = jnp.zeros_like(dbias_ref)
            dwacc[...] = jnp.zeros_like(dwacc)

        uwin[0:HALO, :] = jnp.where(i > 0, _glu(prev_ref[...]), 0.0)
        uwin[HALO:, :] = _glu(cur_ref[...])
        dywin[0:tb, :] = dyc_ref[...]
        dywin[tb:, :] = jnp.where(i < n_steps - 1, dyn_ref[...], 0.0)
        for r in range(tb // CONV_RC):
            rows = slice(r * CONV_RC, (r + 1) * CONV_RC)
            dy_c = dywin[rows, :]
            du = jnp.zeros((CONV_RC, D), F32)
            for k in range(CONV_W):
                back = r * CONV_RC + (CONV_W - 1) - k
                du = du + dywin[back:back + CONV_RC, :] * w_ref[k:k + 1, :]
                fwd = r * CONV_RC + HALO - (CONV_W - 1) + k
                prod = dy_c * uwin[fwd:fwd + CONV_RC, :]
                part = prod[0:8, :]
                for q in range(1, CONV_RC // 8):
                    part = part + prod[8 * q:8 * q + 8, :]
                dwacc[8 * k:8 * k + 8, :] += part
            av = cur_ref[rows, :]
            a1, sg = av[:, :D], jax.nn.sigmoid(av[:, D:])
            da1 = du * sg
            da2 = du * a1 * (sg * (1.0 - sg))
            da_ref[rows, 0:D] = da1.astype(BF16)
            da_ref[rows, D:] = da2.astype(BF16)
            dbias_ref[:, 0:D] += jnp.sum(da1, axis=0, keepdims=True)
            dbias_ref[:, D:] += jnp.sum(da2, axis=0, keepdims=True)

        @pl.when(i == n_steps - 1)
        def _():
            for k in range(CONV_W):
                dw_ref[k:k + 1, :] = jnp.sum(dwacc[8 * k:8 * k + 8, :], axis=0, keepdims=True)
            dw_ref[CONV_W:, :] = jnp.zeros((HALO - CONV_W, D), F32)

    return pl.pallas_call(
        body, name=name, grid=(n_steps,),
        in_specs=[pl.BlockSpec((tb, D), lambda i: (i, 0)),
                  pl.BlockSpec((HALO, D), lambda i: (jnp.minimum((i + 1) * per, last_halo), 0)),
                  pl.BlockSpec((tb, 2 * D), lambda i: (i, 0)),
                  pl.BlockSpec((HALO, 2 * D), lambda i: (jnp.maximum(i * per - 1, 0), 0)),
                  pl.BlockSpec((HALO, D), lambda i: (0, 0))],
        out_specs=[pl.BlockSpec((tb, 2 * D), lambda i: (i, 0)), pl.BlockSpec((1, 2 * D), lambda i: (0, 0)),
                   pl.BlockSpec((HALO, D), lambda i: (0, 0))],
        out_shape=[jax.ShapeDtypeStruct((T, 2 * D), BF16), jax.ShapeDtypeStruct((1, 2 * D), F32),
                   jax.ShapeDtypeStruct((HALO, D), F32)],
        scratch_shapes=[pltpu.VMEM((tb + HALO, D), F32), pltpu.VMEM((tb + HALO, D), F32),
                        pltpu.VMEM((8 * HALO, D), F32)],
        compiler_params=_params(("arbitrary",)),
    )(dy, dy, a, a, dw_w)


GATE_TB = 512


def _gate_fwd(fl, b_f, name):
    T = fl.shape[0]
    tb = min(GATE_TB, T)
    row = pl.BlockSpec((tb, LANES), lambda i: (i, 0))

    def body(fl_ref, b_ref, f_ref, f1_ref, f2_ref, f3_ref, carry):
        @pl.when(pl.program_id(0) == 0)
        def _():
            carry[...] = jnp.zeros_like(carry)

        z = fl_ref[...] + b_ref[...]
        lf = jnp.minimum(z, 0.0) - jnp.log(1.0 + jnp.exp(-jnp.abs(z)))
        tri = (lax.broadcasted_iota(jnp.int32, (tb, tb), 0) >= lax.broadcasted_iota(jnp.int32, (tb, tb), 1)).astype(F32)
        f = jnp.dot(tri, lf, precision=lax.Precision.HIGHEST, preferred_element_type=F32) + carry[...]
        carry[...] = f[tb - 1:tb, :]
        f_ref[...] = f
        f1_ref[...], f2_ref[...], f3_ref[...] = _split3(f)

    return pl.pallas_call(
        body, name=name, grid=(T // tb,), in_specs=[row, pl.BlockSpec((1, LANES), lambda i: (0, 0))],
        out_specs=[row] * 4, out_shape=[jax.ShapeDtypeStruct((T, LANES), F32)] * 4,
        scratch_shapes=[pltpu.VMEM((1, LANES), F32)],
        compiler_params=_params(("arbitrary",)),
    )(fl, b_f)


def _gate_bwd(dF, fl, b_f, name):
    T = fl.shape[0]
    tb = min(GATE_TB, T)
    nb = T // tb
    row = pl.BlockSpec((tb, LANES), lambda i: (nb - 1 - i, 0))
    vec = pl.BlockSpec((1, LANES), lambda i: (0, 0))

    def body(df_ref, fl_ref, b_ref, dfl_ref, db_ref, carry):
        @pl.when(pl.program_id(0) == 0)
        def _():
            carry[...] = jnp.zeros_like(carry)
            db_ref[...] = jnp.zeros_like(db_ref)

        tri = (lax.broadcasted_iota(jnp.int32, (tb, tb), 0) <= lax.broadcasted_iota(jnp.int32, (tb, tb), 1)).astype(F32)
        rc = jnp.dot(tri, df_ref[...], precision=lax.Precision.HIGHEST, preferred_element_type=F32) + carry[...]
        carry[...] = rc[0:1, :]
        dfl = rc * jax.nn.sigmoid(-(fl_ref[...] + b_ref[...]))
        dfl_ref[...] = dfl
        db_ref[...] += jnp.sum(dfl, axis=0, keepdims=True)

    return pl.pallas_call(
        body, name=name, grid=(nb,), in_specs=[row, row, vec], out_specs=[row, vec],
        out_shape=[jax.ShapeDtypeStruct((T, LANES), F32), jax.ShapeDtypeStruct((1, LANES), F32)],
        scratch_shapes=[pltpu.VMEM((1, LANES), F32)],
        compiler_params=_params(("arbitrary",)),
    )(dF, fl, b_f)


ATT_TB = 512
NT = (((1,), (1,)), ((), ()))
NN = (((1,), (0,)), ((), ()))


def _lane_insert(base, parts, first_lane):
    lane = lax.broadcasted_iota(jnp.int32, base.shape, 1)
    out = base
    for n, p in enumerate(parts):
        out = jnp.where(lane == first_lane + n, p, out)
    return out


def _causal(tb):
    return lax.broadcasted_iota(jnp.int32, (tb, tb), 0) >= lax.broadcasted_iota(jnp.int32, (tb, tb), 1)


def _other_head(x):
    return pltpu.roll(x, HEAD_DIM, 1)


def _attn_pack(proj, f1, f2, f3, name):
    T = proj.shape[0]
    tb = min(ATT_TB, T)
    pairs = N_HEADS // 2
    cols = lambda first: pl.BlockSpec((tb, LANES), lambda p, i: (i, first + p))
    gate = pl.BlockSpec((tb, LANES), lambda p, i: (i, 0))
    heads = pl.BlockSpec((2, tb, LANES), lambda p, i: (p, i, 0))
    norm = pl.BlockSpec((2, 8, LANES), lambda p, i: (p, 0, 0))

    def body(q_ref, k_ref, v_ref, f1_ref, f2_ref, f3_ref, qa_ref, ka_ref, va_ref, qn_ref, kn_ref):
        p, i = pl.program_id(0), pl.program_id(1)
        lane = lax.broadcasted_iota(jnp.int32, (tb, LANES), 1)
        data = lane < HEAD_DIM

        @pl.when(i == 0)
        def _():
            qn_ref[...] = jnp.zeros_like(qn_ref)
            kn_ref[...] = jnp.zeros_like(kn_ref)

        qv, kv, vv = q_ref[...] * 0.125, k_ref[...], v_ref[...]
        for e in range(2):
            fcol = [jnp.sum(jnp.where(lane == 2 * p + e, f[...], 0.0), axis=-1, keepdims=True)
                    for f in (f1_ref, f2_ref, f3_ref)]
            pick = (lambda t: t) if e == 0 else _other_head
            qd = jnp.where(data, pick(qv), 0.0).astype(BF16).astype(F32)
            kd = jnp.where(data, pick(kv), 0.0).astype(BF16).astype(F32)
            vd = jnp.where(data, pick(vv), 0.0)
            qa_ref[e] = _lane_insert(qd, fcol + [1.0, 1.0, 1.0], AUG_F).astype(BF16)
            ka_ref[e] = _lane_insert(kd, [1.0, 1.0, 1.0] + [-f for f in fcol] + [1.0, 1.0, 1.0], AUG_F).astype(BF16)
            va_ref[e] = _lane_insert(vd, [-1.0, -1.0, -1.0, 1.0], AUG_F).astype(BF16)
            qn_ref[e] = jnp.maximum(qn_ref[e], jnp.max(jnp.sum(qd * qd, axis=-1, keepdims=True)))
            kn_ref[e] = jnp.maximum(kn_ref[e], jnp.max(jnp.sum(kd * kd, axis=-1, keepdims=True)))

    aug = jax.ShapeDtypeStruct((N_HEADS, T, LANES), BF16)
    nrm = jax.ShapeDtypeStruct((N_HEADS, 8, LANES), F32)
    return pl.pallas_call(
        body, name=name, grid=(pairs, T // tb),
        in_specs=[cols(0), cols(pairs), cols(2 * pairs), gate, gate, gate],
        out_specs=[heads, heads, heads, norm, norm], out_shape=[aug, aug, aug, nrm, nrm],
        compiler_params=_params(("parallel", "arbitrary")),
    )(proj, proj, proj, f1, f2, f3)


def _merge_heads(x_aug, scale, out_dtype, name):
    H, T, _ = x_aug.shape
    tb = min(ATT_TB, T)

    def body(x_ref, o_ref):
        lane = lax.broadcasted_iota(jnp.int32, (tb, LANES), 1)
        o_ref[...] = (jnp.where(lane < HEAD_DIM, x_ref[0], _other_head(x_ref[1])) * scale).astype(out_dtype)

    return pl.pallas_call(
        body, name=name, grid=(H // 2, T // tb),
        in_specs=[pl.BlockSpec((2, tb, LANES), lambda p, i: (p, i, 0))],
        out_specs=pl.BlockSpec((tb, LANES), lambda p, i: (i, p)),
        out_shape=jax.ShapeDtypeStruct((T, D), out_dtype),
        compiler_params=_params(("parallel", "parallel")),
    )(x_aug)


def _attn_prep(d_o, o_aug, name):
    H, T, _ = o_aug.shape
    tb = min(ATT_TB, T)
    heads = pl.BlockSpec((2, tb, LANES), lambda p, i: (p, i, 0))

    def body(do_ref, o_ref, out_ref):
        lane = lax.broadcasted_iota(jnp.int32, (tb, LANES), 1)
        dov = do_ref[...]
        for e in range(2):
            d_e = jnp.where(lane < HEAD_DIM, dov if e == 0 else _other_head(dov), 0.0)
            delta = jnp.sum(d_e * o_ref[e], axis=-1, keepdims=True)
            out_ref[e] = _lane_insert(d_e, _split3(delta), AUG_F).astype(BF16)

    return pl.pallas_call(
        body, name=name, grid=(H // 2, T // tb),
        in_specs=[pl.BlockSpec((tb, LANES), lambda p, i: (i, p)), heads], out_specs=heads,
        out_shape=jax.ShapeDtypeStruct((H, T, LANES), BF16),
        compiler_params=_params(("parallel", "parallel")),
    )(d_o, o_aug)


def _flash_spec(n_in, n_out, H, nb, tb, T, resident, scratch):
    blk = pl.BlockSpec((None, tb, LANES), lambda h, i, *_: (h, i, 0))
    whole = pl.BlockSpec((None, T, LANES), lambda h, i, *_: (h, 0, 0))
    return pltpu.PrefetchScalarGridSpec(
        num_scalar_prefetch=3, grid=(H, nb),
        in_specs=[whole if resident[n] else blk for n in range(n_in)],
        out_specs=[blk] * n_out, scratch_shapes=scratch)


def _first_live(qk_ref, fs_ref, fe_ref, h, i):
    top = qk_ref[h] + fs_ref[h, i]
    return lax.fori_loop(0, i, lambda j, n: n + jnp.where(top - fe_ref[h, j] < SKIP_BELOW, 1, 0), 0)


def _flash_fwd(qk, fs, fe, q_aug, k_aug, v_aug, name):
    H, T, _ = q_aug.shape
    tb = min(ATT_TB, T)

    def body(qk_ref, fs_ref, fe_ref, q_ref, k_ref, v_ref, o_ref, q2_ref, m_s, acc_s):
        h, i = pl.program_id(0), pl.program_id(1)
        q = q_ref[...]
        m_s[...] = jnp.full(m_s.shape, -jnp.inf, F32)
        acc_s[...] = jnp.zeros_like(acc_s)

        def step(kb, masked):
            rows = pl.ds(pl.multiple_of(kb * tb, tb), tb)
            s = lax.dot_general(q, k_ref[rows, :], NT, preferred_element_type=F32)
            if masked:
                s = jnp.where(_causal(tb), s, MASK_VALUE)
            m_old = m_s[...]
            m_new = jnp.maximum(m_old, jnp.max(s, axis=-1, keepdims=True))
            p = jnp.exp(s - m_new)
            acc_s[...] = jnp.exp(m_old - m_new) * acc_s[...] + lax.dot_general(
                p.astype(BF16), v_ref[rows, :], NN, preferred_element_type=F32)
            m_s[...] = m_new

        def loop_body(kb, carry):
            step(kb, False)
            return carry

        lax.fori_loop(_first_live(qk_ref, fs_ref, fe_ref, h, i), i, loop_body, 0)
        step(i, True)
        acc = acc_s[...]
        lane = lax.broadcasted_iota(jnp.int32, acc.shape, 1)
        l = jnp.sum(jnp.where(lane == AUG_ONE, acc, 0.0), axis=-1, keepdims=True)
        o_ref[...] = acc / l
        lse = m_s[...] + jnp.log(l)
        q2_ref[...] = _lane_insert(q.astype(F32), [-p for p in _split3(lse)], AUG_L).astype(BF16)

    return pl.pallas_call(
        body, name=name,
        grid_spec=_flash_spec(3, 2, H, T // tb, tb, T, (False, True, True),
                              [pltpu.VMEM((tb, 1), F32), pltpu.VMEM((tb, LANES), F32)]),
        out_shape=[jax.ShapeDtypeStruct((H, T, LANES), F32), jax.ShapeDtypeStruct((H, T, LANES), BF16)],
        compiler_params=_params(("parallel", "arbitrary")),
    )(qk, fs, fe, q_aug, k_aug, v_aug)


def _flash_dq(qk, fs, fe, q2, k_aug, do_aug, v_aug, name):
    H, T, _ = q2.shape
    tb = min(ATT_TB, T)

    def body(qk_ref, fs_ref, fe_ref, q_ref, k_ref, do_ref, v_ref, dq_ref, acc_s):
        h, i = pl.program_id(0), pl.program_id(1)
        q, dov = q_ref[...], do_ref[...]
        acc_s[...] = jnp.zeros_like(acc_s)

        def step(kb, masked):
            rows = pl.ds(pl.multiple_of(kb * tb, tb), tb)
            k = k_ref[rows, :]
            p = jnp.exp(lax.dot_general(q, k, NT, preferred_element_type=F32))
            if masked:
                p = jnp.where(_causal(tb), p, 0.0)
            ds = p * lax.dot_general(dov, v_ref[rows, :], NT, preferred_element_type=F32)
            acc_s[...] += lax.dot_general(ds.astype(BF16), k, NN, preferred_element_type=F32)

        def loop_body(kb, carry):
            step(kb, False)
            return carry

        lax.fori_loop(_first_live(qk_ref, fs_ref, fe_ref, h, i), i, loop_body, 0)
        step(i, True)
        dq_ref[...] = acc_s[...]

    return pl.pallas_call(
        body, name=name,
        grid_spec=_flash_spec(4, 1, H, T // tb, tb, T, (False, True, False, True), [pltpu.VMEM((tb, LANES), F32)]),
        out_shape=[jax.ShapeDtypeStruct((H, T, LANES), F32)],
        compiler_params=_params(("parallel", "arbitrary")),
    )(qk, fs, fe, q2, k_aug, do_aug, v_aug)[0]


def _flash_dkv(qk, fs, fe, q2, k_aug, do_aug, v_aug, name):
    H, T, _ = q2.shape
    tb = min(ATT_TB, T)
    nb = T // tb

    def body(qk_ref, fs_ref, fe_ref, q_ref, k_ref, do_ref, v_ref, dk_ref, dv_ref, dk_s, dv_s):
        h, i = pl.program_id(0), pl.program_id(1)
        k, v = k_ref[...], v_ref[...]
        dk_s[...] = jnp.zeros_like(dk_s)
        dv_s[...] = jnp.zeros_like(dv_s)

        def step(qb, masked):
            rows = pl.ds(pl.multiple_of(qb * tb, tb), tb)
            q, dov = q_ref[rows, :], do_ref[rows, :]
            pt = jnp.exp(lax.dot_general(k, q, NT, preferred_element_type=F32))
            if masked:
                pt = jnp.where(lax.broadcasted_iota(jnp.int32, (tb, tb), 1) >= lax.broadcasted_iota(jnp.int32, (tb, tb), 0),
                               pt, 0.0)
            dst = pt * lax.dot_general(v, dov, NT, preferred_element_type=F32)
            dv_s[...] += lax.dot_general(pt.astype(BF16), dov, NN, preferred_element_type=F32)
            dk_s[...] += lax.dot_general(dst.astype(BF16), q, NN, preferred_element_type=F32)

        step(i, True)

        def loop_body(qb, carry):
            step(qb, False)
            return carry

        base = qk_ref[h] - fe_ref[h, i]
        live = lax.fori_loop(i + 1, nb, lambda b, n: n + jnp.where(base + fs_ref[h, b] < SKIP_BELOW, 0, 1), 0)
        lax.fori_loop(i + 1, i + 1 + live, loop_body, 0)
        dk_ref[...] = dk_s[...]
        dv_ref[...] = dv_s[...]

    return pl.pallas_call(
        body, name=name,
        grid_spec=_flash_spec(4, 2, H, nb, tb, T, (True, False, True, False),
                              [pltpu.VMEM((tb, LANES), F32), pltpu.VMEM((tb, LANES), F32)]),
        out_shape=[jax.ShapeDtypeStruct((H, T, LANES), F32)] * 2,
        compiler_params=_params(("parallel", "arbitrary")),
    )(qk, fs, fe, q2, k_aug, do_aug, v_aug)


def _add_pair(a, b, name):
    _, rows, cols = a.shape
    blk = pl.BlockSpec((None, rows, cols), lambda j: (j, 0, 0))

    def body(a_ref, b_ref, o_ref):
        o_ref[...] = a_ref[...] + b_ref[...]

    return pl.pallas_call(
        body, name=name, grid=(a.shape[0],), in_specs=[blk, blk], out_specs=blk,
        out_shape=jax.ShapeDtypeStruct(a.shape, F32), compiler_params=_params(("parallel",)),
    )(a, b)


def _add_slots(b, name, tb=128):
    n, rows, cols = b.shape
    tb = tb if rows % tb == 0 else rows
    specs = [pl.BlockSpec((None, tb, cols), functools.partial(lambda j, i: (j, i, 0), j)) for j in range(n)]

    def body(*refs):
        acc = refs[0][...]
        for r in refs[1:n]:
            acc = acc + r[...]
        refs[n][...] = acc

    return pl.pallas_call(
        body, name=name, grid=(rows // tb,), in_specs=specs, out_specs=pl.BlockSpec((tb, cols), lambda i: (i, 0)),
        out_shape=jax.ShapeDtypeStruct((rows, cols), F32), compiler_params=_params(("parallel",)),
    )(*([b] * n))


def _adamw(w, g, m, v, name):
    rows, cols = w.shape
    tr = 256 if rows % 256 == 0 else rows
    blk = pl.BlockSpec((tr, cols), lambda i: (i, 0))

    def body(w_ref, g_ref, m_ref, v_ref, d_ref, nm_ref, nv_ref):
        gv = g_ref[...]
        nm = ADAM_B1 * m_ref[...] + (1.0 - ADAM_B1) * gv
        nv = ADAM_B2 * v_ref[...] + (1.0 - ADAM_B2) * (gv * gv)
        m_hat = nm / (1.0 - ADAM_B1 ** ADAM_STEP)
        v_hat = nv / (1.0 - ADAM_B2 ** ADAM_STEP)
        d_ref[...] = -ADAM_LR * (m_hat / (jnp.sqrt(v_hat) + ADAM_EPS) + ADAM_WD * w_ref[...])
        nm_ref[...] = nm
        nv_ref[...] = nv

    return pl.pallas_call(
        body, name=name, grid=(rows // tr,), in_specs=[blk] * 4, out_specs=[blk] * 3,
        out_shape=[jax.ShapeDtypeStruct((rows, cols), F32)] * 3, compiler_params=_params(("parallel",)),
    )(w, g, m, v)


ANY = pl.BlockSpec(memory_space=pl.ANY)


def _place():
    x, y, c = lax.axis_index("x"), lax.axis_index("y"), lax.axis_index("c")
    others = [(1 - x, y), (x, 1 - y), (1 - x, 1 - y)]
    return x, y, c, 2 * x + y, others


def _half(ref, shape, h):
    if len(shape) == 3:
        return ref.at[h]
    return ref.at[pl.ds(h * (shape[0] // 2), shape[0] // 2), :]


def _allgather_weights(shards):
    n_u = len(GATHER_UNITS)

    def body(*refs):
        ins, outs = refs[:n_u], refs[n_u:2 * n_u]
        send_sems, recv_sems, local_sems = refs[2 * n_u:]
        x, y, c, me, others = _place()
        sibling = (x, y, 1 - c)

        def copy(u, k, chip, h, to, src=None):
            dst = _half(outs[u].at[chip], GATHER_UNITS[u][1], h)
            return pltpu.make_async_remote_copy(
                src_ref=dst if src is None else src, dst_ref=dst, send_sem=send_sems.at[6 * u + k],
                recv_sem=recv_sems.at[6 * u + k], device_id=to, device_id_type=MESH)

        mine = [pltpu.make_async_copy(ins[u], outs[u].at[me], local_sems.at[u]) for u in range(n_u)]
        for cp in mine:
            cp.start()
        first = [copy(u, j, me, c, (ox, oy, c), src=_half(ins[u], GATHER_UNITS[u][1], c))
                 for u in range(n_u) for j, (ox, oy) in enumerate(others)]
        for cp in first:
            cp.start()
        passed = []
        for u in range(n_u):
            for j, (ox, oy) in enumerate(others):
                copy(u, j, 2 * ox + oy, c, (x, y, c)).wait_recv()
                passed.append(copy(u, 3 + j, 2 * ox + oy, c, sibling))
                passed[-1].start()
        for u in range(n_u):
            for j, (ox, oy) in enumerate(others):
                copy(u, 3 + j, 2 * ox + oy, 1 - c, (x, y, c)).wait_recv()
        for cp in first + passed:
            cp.wait_send()
        for cp in mine:
            cp.wait()

    return pl.pallas_call(
        body, name="allgather_weights", in_specs=[ANY] * n_u, out_specs=[ANY] * n_u,
        out_shape=[jax.ShapeDtypeStruct((N_CHIPS,) + shape, BF16) for _, shape in GATHER_UNITS],
        scratch_shapes=[pltpu.SemaphoreType.DMA((6 * n_u,)), pltpu.SemaphoreType.DMA((6 * n_u,)),
                        pltpu.SemaphoreType.DMA((n_u,))],
    )(*shards)


def _sibling_exchange(slabs):
    n_u = len(GRAD_UNITS)

    def body(*refs):
        ins, kept, got = refs[:n_u], refs[n_u:2 * n_u], refs[2 * n_u:3 * n_u]
        send_sems, recv_sems, local_sems = refs[3 * n_u:]
        x, y, c, _, _ = _place()
        local, remote = [], []
        for u, (_, (rows, _), _) in enumerate(GRAD_UNITS):
            half = rows // 2
            local.append(pltpu.make_async_copy(ins[u].at[:, pl.ds(c * half, half), :], kept[u], local_sems.at[u]))
            remote.append(pltpu.make_async_remote_copy(
                src_ref=ins[u].at[:, pl.ds((1 - c) * half, half), :], dst_ref=got[u], send_sem=send_sems.at[u],
                recv_sem=recv_sems.at[u], device_id=(x, y, 1 - c), device_id_type=MESH))
        for cp in local + remote:
            cp.start()
        for cp in remote + local:
            cp.wait()

    shapes = [jax.ShapeDtypeStruct((N_CHIPS, rows // 2, cols), F32) for _, (rows, cols), _ in GRAD_UNITS]
    outs = pl.pallas_call(
        body, name="grad_sibling_exchange", in_specs=[ANY] * n_u, out_specs=[ANY] * (2 * n_u), out_shape=shapes * 2,
        scratch_shapes=[pltpu.SemaphoreType.DMA((n_u,)), pltpu.SemaphoreType.DMA((n_u,)), pltpu.SemaphoreType.DMA((n_u,))],
    )(*slabs)
    return outs[:n_u], outs[n_u:]


def _chip_exchange(pairs, small):
    n_u = len(GRAD_UNITS)

    def body(*refs):
        ins, s_ref = refs[:n_u], refs[n_u]
        outs, sall_ref = refs[n_u + 1:2 * n_u + 1], refs[2 * n_u + 1]
        send_sems, recv_sems, ssend_sems, srecv_sems, local_sems = refs[2 * n_u + 2:]
        x, y, c, me, others = _place()
        dev = 4 * x + 2 * y + c
        local = [pltpu.make_async_copy(ins[u].at[me], outs[u].at[me], local_sems.at[u]) for u in range(n_u)]
        local.append(pltpu.make_async_copy(s_ref, sall_ref.at[dev], local_sems.at[n_u]))
        sends = [pltpu.make_async_remote_copy(
            src_ref=ins[u].at[2 * ox + oy], dst_ref=outs[u].at[me], send_sem=send_sems.at[3 * u + j],
            recv_sem=recv_sems.at[3 * u + j], device_id=(ox, oy, c), device_id_type=MESH)
            for u in range(n_u) for j, (ox, oy) in enumerate(others)]
        flips = [(fx, fy, fc) for fx in (0, 1) for fy in (0, 1) for fc in (0, 1)][1:]
        sends += [pltpu.make_async_remote_copy(
            src_ref=s_ref, dst_ref=sall_ref.at[dev], send_sem=ssend_sems.at[n], recv_sem=srecv_sems.at[n],
            device_id=(x ^ fx, y ^ fy, c ^ fc), device_id_type=MESH) for n, (fx, fy, fc) in enumerate(flips)]
        for cp in local + sends:
            cp.start()
        for u in range(n_u):
            for j, (ox, oy) in enumerate(others):
                pltpu.make_async_remote_copy(
                    src_ref=ins[u].at[me], dst_ref=outs[u].at[2 * ox + oy], send_sem=send_sems.at[3 * u + j],
                    recv_sem=recv_sems.at[3 * u + j], device_id=(x, y, c), device_id_type=MESH).wait_recv()
        for n, (fx, fy, fc) in enumerate(flips):
            src_dev = 4 * (x ^ fx) + 2 * (y ^ fy) + (c ^ fc)
            pltpu.make_async_remote_copy(
                src_ref=s_ref, dst_ref=sall_ref.at[src_dev], send_sem=ssend_sems.at[n], recv_sem=srecv_sems.at[n],
                device_id=(x, y, c), device_id_type=MESH).wait_recv()
        for cp in sends:
            cp.wait_send()
        for cp in local:
            cp.wait()

    outs = pl.pallas_call(
        body, name="grad_chip_exchange", in_specs=[ANY] * (n_u + 1), out_specs=[ANY] * (n_u + 1),
        out_shape=[jax.ShapeDtypeStruct(p.shape, F32) for p in pairs]
        + [jax.ShapeDtypeStruct((8, SMALL_ALL_ROWS, LANES), F32)],
        scratch_shapes=[pltpu.SemaphoreType.DMA((3 * n_u,)), pltpu.SemaphoreType.DMA((3 * n_u,)),
                        pltpu.SemaphoreType.DMA((7,)), pltpu.SemaphoreType.DMA((7,)),
                        pltpu.SemaphoreType.DMA((n_u + 1,))],
    )(*pairs, small)
    return outs[:n_u], outs[n_u]


def _sibling_share(halves):
    n_u = len(GRAD_UNITS)
    names = [n for n, _ in GRAD_OUT]

    def body(*refs):
        ins, outs = refs[:n_u], refs[n_u:n_u + len(GRAD_OUT)]
        send_sems, recv_sems, local_sems = refs[n_u + len(GRAD_OUT):]
        x, y, c, _, _ = _place()

        def dst(u, h):
            _, (rows, _), (out_name, layer) = GRAD_UNITS[u]
            ref = outs[names.index(out_name)]
            ref = ref if layer is None else ref.at[layer]
            return ref.at[pl.ds(h * (rows // 2), rows // 2), :]

        local = [pltpu.make_async_copy(ins[u], dst(u, c), local_sems.at[u]) for u in range(n_u)]
        sends = [pltpu.make_async_remote_copy(src_ref=ins[u], dst_ref=dst(u, c), send_sem=send_sems.at[u],
                                              recv_sem=recv_sems.at[u], device_id=(x, y, 1 - c), device_id_type=MESH)
                 for u in range(n_u)]
        for cp in local + sends:
            cp.start()
        for u in range(n_u):
            pltpu.make_async_remote_copy(src_ref=ins[u], dst_ref=dst(u, 1 - c), send_sem=send_sems.at[u],
                                         recv_sem=recv_sems.at[u], device_id=(x, y, c), device_id_type=MESH).wait_recv()
        for cp in sends:
            cp.wait_send()
        for cp in local:
            cp.wait()

    return pl.pallas_call(
        body, name="grad_sibling_share", in_specs=[ANY] * n_u, out_specs=[ANY] * len(GRAD_OUT),
        out_shape=[jax.ShapeDtypeStruct(shape, F32) for _, shape in GRAD_OUT],
        scratch_shapes=[pltpu.SemaphoreType.DMA((n_u,)), pltpu.SemaphoreType.DMA((n_u,)), pltpu.SemaphoreType.DMA((n_u,))],
    )(*halves)


def kernel(x, g_mix_pre, g_mix_post, g_ffn_pre, g_ffn_post, conv_pw1_w, conv_pw1_b, conv_dw_w, conv_dw_b, conv_ln_g, conv_ln_b, conv_pw2_w, conv_pw2_b, attn_w_in, attn_b_f, attn_w_o, mlp_w_up, mlp_w_down, loss_target, m_g_mix_pre, m_g_mix_post, m_g_ffn_pre, m_g_ffn_post, m_conv_pw1_w, m_conv_pw1_b, m_conv_dw_w, m_conv_dw_b, m_conv_ln_g, m_conv_ln_b, m_conv_pw2_w, m_conv_pw2_b, m_attn_w_in, m_attn_b_f, m_attn_w_o, m_mlp_w_up, m_mlp_w_down, v_g_mix_pre, v_g_mix_post, v_g_ffn_pre, v_g_ffn_post, v_conv_pw1_w, v_conv_pw1_b, v_conv_dw_w, v_conv_dw_b, v_conv_ln_g, v_conv_ln_b, v_conv_pw2_w, v_conv_pw2_b, v_attn_w_in, v_attn_b_f, v_attn_w_o, v_mlp_w_up, v_mlp_w_down):
    weights = dict(g_mix_pre=g_mix_pre, g_mix_post=g_mix_post, g_ffn_pre=g_ffn_pre, g_ffn_post=g_ffn_post, conv_pw1_w=conv_pw1_w, conv_pw1_b=conv_pw1_b, conv_dw_w=conv_dw_w, conv_dw_b=conv_dw_b, conv_ln_g=conv_ln_g, conv_ln_b=conv_ln_b, conv_pw2_w=conv_pw2_w, conv_pw2_b=conv_pw2_b, attn_w_in=attn_w_in, attn_b_f=attn_b_f, attn_w_o=attn_w_o, mlp_w_up=mlp_w_up, mlp_w_down=mlp_w_down)
    mom_m = dict(g_mix_pre=m_g_mix_pre, g_mix_post=m_g_mix_post, g_ffn_pre=m_g_ffn_pre, g_ffn_post=m_g_ffn_post, conv_pw1_w=m_conv_pw1_w, conv_pw1_b=m_conv_pw1_b, conv_dw_w=m_conv_dw_w, conv_dw_b=m_conv_dw_b, conv_ln_g=m_conv_ln_g, conv_ln_b=m_conv_ln_b, conv_pw2_w=m_conv_pw2_w, conv_pw2_b=m_conv_pw2_b, attn_w_in=m_attn_w_in, attn_b_f=m_attn_b_f, attn_w_o=m_attn_w_o, mlp_w_up=m_mlp_w_up, mlp_w_down=m_mlp_w_down)
    mom_v = dict(g_mix_pre=v_g_mix_pre, g_mix_post=v_g_mix_post, g_ffn_pre=v_g_ffn_pre, g_ffn_post=v_g_ffn_post, conv_pw1_w=v_conv_pw1_w, conv_pw1_b=v_conv_pw1_b, conv_dw_w=v_conv_dw_w, conv_dw_b=v_conv_dw_b, conv_ln_g=v_conv_ln_g, conv_ln_b=v_conv_ln_b, conv_pw2_w=v_conv_pw2_w, conv_pw2_b=v_conv_pw2_b, attn_w_in=v_attn_w_in, attn_b_f=v_attn_b_f, attn_w_o=v_attn_w_o, mlp_w_up=v_mlp_w_up, mlp_w_down=v_mlp_w_down)
    order = [n for n, _ in SMALL[:4]] + ["conv_pw1_w", "conv_pw1_b", "conv_dw_w", "conv_dw_b", "conv_ln_g", "conv_ln_b",
                                          "conv_pw2_w", "conv_pw2_b", "attn_w_in", "attn_b_f", "attn_w_o", "mlp_w_up",
                                          "mlp_w_down"]
    T = x.shape[1]
    x0 = x.reshape(T, D)
    target = loss_target.reshape(T, D)
    row = lambda a, l: a[l:l + 1, :]

    shards = []
    for name, shape in GATHER_UNITS:
        w = weights[name].astype(BF16)
        w = w.reshape(w.shape[-len(shape):])
        if name == "conv_dw_w":
            w = jnp.pad(w, ((0, HALO - CONV_W), (0, 0)))
        shards.append(w)
    g_pw1, g_dw, g_pw2, g_win, g_wo, g_up, g_down = _allgather_weights(shards)
    w_pw2, w_o = g_pw2.reshape(D, D), g_wo.reshape(D, D)
    w_in = jnp.pad(g_win.transpose(1, 0, 2).reshape(D, W_IN_COLS), ((0, 0), (0, W_IN_PAD - W_IN_COLS)))
    w_dw = g_dw.transpose(1, 0, 2).reshape(HALO, D).astype(F32)
    b_f = jnp.pad(attn_b_f, ((0, 0), (0, LANES - N_HEADS)))

    def mlp_fwd(h, l):
        up, act = _matmul(h, g_up, "nn", (F32, BF16), _epi_sqrelu, f"mlp{l}_up", n=DFF, b_place=_place_cols_nn(D, l))
        (mo,) = _matmul(act, g_down, "nn", (F32,), _epi_plain, f"mlp{l}_down", n=D, b_place=_place_rows_nn(D, l))
        return up, act, mo

    (h0,) = _resnorm_fwd(x0, None, None, row(g_mix_pre, 0), "norm_in")
    (a0,) = _matmul(h0, g_pw1, "nn", (F32,), _epi_bias, "conv_pw1", extras=((conv_pw1_b, "row"),), tn=512, n=2 * D,
                    b_place=_place_cols_nn(512))
    y0, z0 = _conv_fwd(a0, w_dw, conv_dw_b, conv_ln_g, conv_ln_b, "conv_fwd")
    (m0,) = _matmul(z0, w_pw2, "nn", (F32,), _epi_bias, "conv_pw2", extras=((conv_pw2_b, "row"),))
    x1, h1 = _resnorm_fwd(x0, m0, row(g_mix_post, 0), row(g_ffn_pre, 0), "norm_conv_out")
    up0, act0, mo0 = mlp_fwd(h1, 0)
    x2, h2 = _resnorm_fwd(x1, mo0, row(g_ffn_post, 0), row(g_mix_pre, 1), "norm_mlp0_out")

    (proj,) = _matmul(h2, w_in, "nn", (F32,), _epi_plain, "attn_in", tn=640)
    fl = proj[:, 3 * D:]
    f_all, f1, f2, f3 = _gate_fwd(fl, b_f, "gate_fwd")
    q_aug, k_aug, v_aug, qn2, kn2 = _attn_pack(proj, f1, f2, f3, "attn_pack")
    tb = min(ATT_TB, T)
    qk = 2.02 * jnp.sqrt(qn2[:, 0, 0] * kn2[:, 0, 0]) + 1.0
    fs, fe = f_all[0::tb, :N_HEADS].T, f_all[tb - 1::tb, :N_HEADS].T
    o_aug, q2_aug = _flash_fwd(qk, fs, fe, q_aug, k_aug, v_aug, "flash_fwd")
    o_bf = _merge_heads(o_aug, 1.0, BF16, "attn_merge_o")
    (m1,) = _matmul(o_bf, w_o, "nn", (F32,), _epi_plain, "attn_out")
    x3, h3 = _resnorm_fwd(x2, m1, row(g_mix_post, 1), row(g_ffn_pre, 1), "norm_attn_out")
    up1, act1, mo1 = mlp_fwd(h3, 1)
    dy, loss_part = _resnorm_loss(x3, mo1, row(g_ffn_post, 1), target, "loss")
    loss = lax.psum(loss_part[0, 0], ("x", "y", "c"))

    G = {}

    def mlp_bwd(d_out, mo, up, act, h, x_in, l):
        d_mo, dg_post, _ = _norm_bwd(mo, row(g_ffn_post, l), d_out, None, BF16, f"mlp{l}_post_bwd")
        (d_up,) = _matmul(d_mo, g_down, "nt", (BF16,), _epi_dsqrelu, f"mlp{l}_dact", extras=((up, "tile"),), n=DFF,
                          b_place=_place_rows_nt(D, l))
        (dw_down,) = _matmul(act, d_mo, "tn", (F32,), _epi_plain, f"mlp{l}_dwdown")
        (dw_up,) = _matmul(h, d_up, "tn", (F32,), _epi_plain, f"mlp{l}_dwup", shard_out=True)
        (d_h,) = _matmul(d_up, g_up, "nt", (F32,), _epi_plain, f"mlp{l}_dh", n=D, b_place=_place_cols_nt(D, l))
        d_in, dg_pre, _ = _norm_bwd(x_in, row(g_ffn_pre, l), d_h, d_out, F32, f"mlp{l}_pre_bwd")
        G[f"up{l}"], G[f"down{l}"] = dw_up, dw_down.reshape(N_CHIPS, D, D)
        return d_in, dg_post, dg_pre

    d_x3, dg_ffn_post1, dg_ffn_pre1 = mlp_bwd(dy, mo1, up1, act1, h3, x3, 1)

    d_m1, dg_mix_post1, _ = _norm_bwd(m1, row(g_mix_post, 1), d_x3, None, BF16, "attn_post_bwd")
    (dw_o,) = _matmul(o_bf, d_m1, "tn", (F32,), _epi_plain, "attn_dwo")
    G["attn_w_o"] = dw_o.reshape(N_CHIPS, D // N_CHIPS, D)
    (d_o,) = _matmul(d_m1, w_o, "nt", (F32,), _epi_plain, "attn_do")
    do_aug = _attn_prep(d_o, o_aug, "attn_prep")
    dq_aug = _flash_dq(qk, fs, fe, q2_aug, k_aug, do_aug, v_aug, "flash_dq")
    dk_aug, dv_aug = _flash_dkv(qk, fs, fe, q2_aug, k_aug, do_aug, v_aug, "flash_dkv")
    dF = (dq_aug[:, :, AUG_F] - dk_aug[:, :, AUG_ONE]).T
    d_fl, db_f = _gate_bwd(jnp.pad(dF, ((0, 0), (0, LANES - N_HEADS))), fl, b_f, "gate_bwd")
    d_proj = jnp.concatenate([
        _merge_heads(dq_aug, 0.125, BF16, "attn_merge_dq"), _merge_heads(dk_aug, 1.0, BF16, "attn_merge_dk"),
        _merge_heads(dv_aug, 1.0, BF16, "attn_merge_dv"), d_fl.astype(BF16)], axis=1)
    (dw_in,) = _matmul(h2, d_proj, "tn", (F32,), _epi_plain, "attn_dwin", tn=640)
    G["attn_w_in"] = dw_in[:, :W_IN_COLS].reshape(D, N_CHIPS, W_IN_COLS // N_CHIPS).transpose(1, 0, 2)
    (d_h2,) = _matmul(d_proj, w_in, "nt", (F32,), _epi_plain, "attn_dh", tk=640)
    d_x2, dg_mix_pre1, _ = _norm_bwd(x2, row(g_mix_pre, 1), d_h2, d_x3, F32, "attn_pre_bwd")

    d_x1, dg_ffn_post0, dg_ffn_pre0 = mlp_bwd(d_x2, mo0, up0, act0, h1, x1, 0)

    d_m0, dg_mix_post0, db_pw2 = _norm_bwd(m0, row(g_mix_post, 0), d_x1, None, BF16, "conv_post_bwd")
    (dw_pw2,) = _matmul(z0, d_m0, "tn", (F32,), _epi_plain, "conv_dwpw2")
    G["conv_pw2_w"] = dw_pw2.reshape(N_CHIPS, D // N_CHIPS, D)
    (d_z0,) = _matmul(d_m0, w_pw2, "nt", (F32,), _epi_plain, "conv_dz")
    d_y0, dln_g, dln_b, ddw_b = _conv_bwd_ln(d_z0, y0, conv_ln_g, conv_ln_b, "conv_bwd_ln")
    d_a0, db_pw1, ddw_w = _conv_bwd_dw(d_y0, a0, w_dw, "conv_bwd_dw")
    (G["conv_pw1_w"],) = _matmul(h0, d_a0, "tn", (F32,), _epi_plain, "conv_dwpw1", tn=512, shard_out=True)
    (d_h0,) = _matmul(d_a0, g_pw1, "nt", (F32,), _epi_plain, "conv_dh", tk=512, n=D, b_place=_place_cols_nt(512))
    d_x0, dg_mix_pre0, _ = _norm_bwd(x0, row(g_mix_pre, 0), d_h0, d_x1, F32, "conv_pre_bwd")

    small_local = {
        "g_mix_pre": jnp.concatenate([dg_mix_pre0, dg_mix_pre1]), "g_mix_post": jnp.concatenate([dg_mix_post0, dg_mix_post1]),
        "g_ffn_pre": jnp.concatenate([dg_ffn_pre0, dg_ffn_pre1]), "g_ffn_post": jnp.concatenate([dg_ffn_post0, dg_ffn_post1]),
        "conv_pw1_b": db_pw1, "conv_dw_b": ddw_b, "conv_ln_g": dln_g, "conv_ln_b": dln_b, "conv_pw2_b": db_pw2,
        "attn_b_f": db_f[:, :N_HEADS], "conv_dw_w_full": ddw_w,
    }

    kept, got = _sibling_exchange([G[n] for n, _, _ in GRAD_UNITS])
    pairs = [_add_pair(a, b, f"grad_pair_{n}") for a, b, (n, _, _) in zip(kept, got, GRAD_UNITS)]
    by_chip, small_all = _chip_exchange(pairs, _pack([small_local[n] for n, _ in SMALL_ALL], SMALL_ALL_ROWS))
    halves = [_add_slots(b, f"grad_chips_{n}") for b, (n, _, _) in zip(by_chip, GRAD_UNITS)]
    grads = dict(zip([n for n, _ in GRAD_OUT], _sibling_share(halves)))
    grads.update(_unpack(_add_slots(small_all, "grad_add_small", tb=SMALL_ALL_ROWS), SMALL_ALL))
    chip = 2 * lax.axis_index("x") + lax.axis_index("y")
    grads["conv_dw_w"] = lax.dynamic_slice(grads.pop("conv_dw_w_full"), (0, chip * 256), (CONV_W, 256))

    delta, new_m, new_v = {}, {}, {}
    for name, _ in GATHER_UNITS:
        full = weights[name].shape
        two_d = (_size(full) // full[-1], full[-1])
        d, nm, nv = _adamw(weights[name].reshape(two_d), grads[name].reshape(two_d), mom_m[name].reshape(two_d),
                           mom_v[name].reshape(two_d), f"adamw_{name}")
        grads[name] = grads[name].reshape(full)
        delta[name], new_m[name], new_v[name] = d.reshape(full), nm.reshape(full), nv.reshape(full)
    packs = [_pack([src[n] for n, _ in SMALL], SMALL_ROWS) for src in (weights, grads, mom_m, mom_v)]
    d, nm, nv = _adamw(*packs, "adamw_small")
    for dst, packed in ((delta, d), (new_m, nm), (new_v, nv)):
        dst.update(_unpack(packed, SMALL))

    grad_x = d_x0.reshape(x.shape)
    return (loss, grad_x, *[grads[n] for n in order], *[delta[n] for n in order], *[new_m[n] for n in order],
            *[new_v[n] for n in order])
```

```python
import functools

import jax
import jax.numpy as jnp
from jax import lax
from jax.experimental import pallas as pl
from jax.experimental.pallas import tpu as pltpu

F32 = jnp.float32
BF16 = jnp.bfloat16
MESH = pl.DeviceIdType.MESH

D = 1024
DFF = 4096
N_HEADS = 16
HEAD_DIM = 64
CONV_W = 31
HALO = 32
RMS_EPS = 1e-6
LN_EPS = 1e-5
MASK_VALUE = -1e30
W_IN_COLS = 3 * D + N_HEADS
W_IN_PAD = 3200
LANES = 128
N_CHIPS = 4
VMEM_LIMIT = 56 * 1024 * 1024

ADAM_LR = 0.001
ADAM_B1 = 0.9
ADAM_B2 = 0.999
ADAM_EPS = 1e-08
ADAM_WD = 0.01
ADAM_STEP = 10

AUG_F = 64
AUG_ONE = 67
AUG_L = 70
SKIP_BELOW = -104.0

GATHER_UNITS = (
    ("conv_pw1_w", (1024, 512)),
    ("conv_dw_w", (HALO, 256)),
    ("conv_pw2_w", (256, 1024)),
    ("attn_w_in", (1024, 772)),
    ("attn_w_o", (256, 1024)),
    ("mlp_w_up", (2, 1024, 1024)),
    ("mlp_w_down", (2, 1024, 1024)),
)
GRAD_UNITS = (
    ("conv_pw1_w", (1024, 512), ("conv_pw1_w", None)),
    ("conv_pw2_w", (256, 1024), ("conv_pw2_w", None)),
    ("attn_w_in", (1024, 772), ("attn_w_in", None)),
    ("attn_w_o", (256, 1024), ("attn_w_o", None)),
    ("up0", (1024, 1024), ("mlp_w_up", 0)),
    ("up1", (1024, 1024), ("mlp_w_up", 1)),
    ("down0", (1024, 1024), ("mlp_w_down", 0)),
    ("down1", (1024, 1024), ("mlp_w_down", 1)),
)
SMALL = (
    ("g_mix_pre", (2, 1024)), ("g_mix_post", (2, 1024)), ("g_ffn_pre", (2, 1024)), ("g_ffn_post", (2, 1024)),
    ("conv_pw1_b", (1, 2048)), ("conv_dw_b", (1, 1024)), ("conv_ln_g", (1, 1024)), ("conv_ln_b", (1, 1024)),
    ("conv_pw2_b", (1, 1024)), ("attn_b_f", (1, 16)),
)
SMALL_ROWS = 120
SMALL_ALL = SMALL + (("conv_dw_w_full", (HALO, 1024)),)
SMALL_ALL_ROWS = 376


def _size(shape):
    n = 1
    for s in shape:
        n *= s
    return n


def _pack(arrays, rows):
    flat = jnp.concatenate([a.reshape(-1) for a in arrays])
    return jnp.pad(flat, (0, rows * LANES - flat.shape[0])).reshape(rows, LANES)


def _unpack(packed, table):
    flat = packed.reshape(-1)
    out, off = {}, 0
    for name, shape in table:
        n = _size(shape)
        out[name] = flat[off:off + n].reshape(shape)
        off += n
    return out


def _params(sem):
    return pltpu.CompilerParams(dimension_semantics=sem, vmem_limit_bytes=VMEM_LIMIT)


def _rms(x, g):
    return x * lax.rsqrt(jnp.mean(x * x, axis=-1, keepdims=True) + RMS_EPS) * g


def _split3(v):
    p1 = v.astype(BF16).astype(F32)
    r = v - p1
    p2 = r.astype(BF16).astype(F32)
    p3 = (r - p2).astype(BF16).astype(F32)
    return p1, p2, p3


def _matmul(a, b, mode, out_dtypes, epi, name, extras=(), tm=1024, tn=1024, tk=1024, n=None, b_place=None,
            shard_out=False):
    (K, M) = a.shape if mode == "tn" else a.shape[::-1]
    if n is None:
        N = b.shape[0] if mode == "nt" else b.shape[1]
    else:
        N = n
    tm, tn, tk = min(tm, M), min(tn, N), min(tk, K)
    nk = K // tk
    if mode == "tn":
        a_spec = pl.BlockSpec((tk, tm), lambda i, j, k: (k, i))
    else:
        a_spec = pl.BlockSpec((tm, tk), lambda i, j, k: (i, k))
    if b_place is not None:
        b_spec = b_place(tk, tn)
    elif mode == "nt":
        b_spec = pl.BlockSpec((tn, tk), lambda i, j, k: (j, k))
    else:
        b_spec = pl.BlockSpec((tk, tn), lambda i, j, k: (k, j))
    if shard_out:
        per = N // N_CHIPS // tn
        o_spec = pl.BlockSpec((None, tm, tn), lambda i, j, k: (j // per, i, j % per))
        o_shape = (N_CHIPS, M, N // N_CHIPS)
    else:
        o_spec = pl.BlockSpec((tm, tn), lambda i, j, k: (i, j))
        o_shape = (M, N)
    dims = {"nn": (((1,), (0,)), ((), ())), "nt": (((1,), (1,)), ((), ())), "tn": (((0,), (0,)), ((), ()))}[mode]
    ex_specs = []
    for _, kind in extras:
        if kind == "row":
            ex_specs.append(pl.BlockSpec((1, tn), lambda i, j, k: (0, j)))
        else:
            ex_specs.append(pl.BlockSpec((tm, tn), lambda i, j, k: (i, j)))
    n_ex, n_out = len(extras), len(out_dtypes)

    def body(*refs):
        a_ref, b_ref = refs[0], refs[1]
        ex = refs[2:2 + n_ex]
        outs = refs[2 + n_ex:2 + n_ex + n_out]
        prod = lax.dot_general(a_ref[...], b_ref[...], dims, preferred_element_type=F32)

        def finish(acc):
            res = epi(acc, *[e[...] for e in ex])
            for o, r in zip(outs, res):
                o[...] = r.astype(o.dtype)

        if nk == 1:
            finish(prod)
        else:
            acc_ref = refs[-1]
            k = pl.program_id(2)

            @pl.when(k == 0)
            def _():
                acc_ref[...] = prod

            @pl.when(k > 0)
            def _():
                acc_ref[...] += prod

            @pl.when(k == nk - 1)
            def _():
                finish(acc_ref[...])

    return pl.pallas_call(
        body,
        name=name,
        grid=(M // tm, N // tn, nk),
        in_specs=[a_spec, b_spec] + ex_specs,
        out_specs=[o_spec for _ in out_dtypes],
        out_shape=[jax.ShapeDtypeStruct(o_shape, dt) for dt in out_dtypes],
        scratch_shapes=[pltpu.VMEM((tm, tn), F32)] if nk > 1 else [],
        compiler_params=_params(("parallel", "parallel", "arbitrary")),
    )(a, b, *[e for e, _ in extras])


def _place_cols_nn(width, layer=None):
    def place(tk, tn):
        per = width // tn
        if layer is None:
            return pl.BlockSpec((None, tk, tn), lambda i, j, k: (j // per, k, j % per))
        return pl.BlockSpec((None, None, tk, tn), lambda i, j, k: (j // per, layer, k, j % per))
    return place


def _place_cols_nt(width, layer=None):
    def place(tk, tn):
        per = width // tk
        if layer is None:
            return pl.BlockSpec((None, tn, tk), lambda i, j, k: (k // per, j, k % per))
        return pl.BlockSpec((None, None, tn, tk), lambda i, j, k: (k // per, layer, j, k % per))
    return place


def _place_rows_nn(height, layer):
    def place(tk, tn):
        per = height // tk
        return pl.BlockSpec((None, None, tk, tn), lambda i, j, k: (k // per, layer, k % per, j))
    return place


def _place_rows_nt(height, layer):
    def place(tk, tn):
        per = height // tn
        return pl.BlockSpec((None, None, tn, tk), lambda i, j, k: (j // per, layer, j % per, k))
    return place


def _epi_plain(acc):
    return (acc,)


def _epi_bias(acc, bias):
    return (acc + bias,)


def _epi_sqrelu(acc):
    r = jnp.maximum(acc, 0.0)
    return acc, r * r


def _epi_dsqrelu(acc, up):
    return (acc * (2.0 * jnp.maximum(up, 0.0)),)


def _resnorm_fwd(x, m, g_post, g_next, name, tr=512):
    T = x.shape[0]
    has_m = m is not None
    row = pl.BlockSpec((tr, D), lambda i: (i, 0))
    vec = pl.BlockSpec((1, D), lambda i: (0, 0))

    def body(*refs):
        if has_m:
            x_ref, m_ref, gp_ref, gn_ref, x1_ref, h_ref = refs
            x1 = x_ref[...] + _rms(m_ref[...], gp_ref[...])
            x1_ref[...] = x1
        else:
            x_ref, gn_ref, h_ref = refs
            x1 = x_ref[...]
        h_ref[...] = _rms(x1, gn_ref[...]).astype(BF16)

    if has_m:
        args, in_specs = (x, m, g_post, g_next), [row, row, vec, vec]
        out_specs = [row, row]
        out_shape = [jax.ShapeDtypeStruct((T, D), F32), jax.ShapeDtypeStruct((T, D), BF16)]
    else:
        args, in_specs = (x, g_next), [row, vec]
        out_specs = [row]
        out_shape = [jax.ShapeDtypeStruct((T, D), BF16)]
    return pl.pallas_call(
        body, name=name, grid=(T // tr,), in_specs=in_specs, out_specs=out_specs, out_shape=out_shape,
        compiler_params=_params(("parallel",)),
    )(*args)


def _resnorm_loss(x, m, g_post, target, name, tr=512):
    T = x.shape[0]
    row = pl.BlockSpec((tr, D), lambda i: (i, 0))
    vec = pl.BlockSpec((1, D), lambda i: (0, 0))

    def body(x_ref, m_ref, gp_ref, t_ref, dy_ref, loss_ref):
        diff = x_ref[...] + _rms(m_ref[...], gp_ref[...]) - t_ref[...]
        dy_ref[...] = diff * (1.0 / D)

        @pl.when(pl.program_id(0) == 0)
        def _():
            loss_ref[...] = jnp.zeros_like(loss_ref)

        per_token = jnp.mean(diff * diff, axis=-1, keepdims=True)
        loss_ref[...] += 0.5 * jnp.sum(per_token)

    return pl.pallas_call(
        body, name=name, grid=(T // tr,), in_specs=[row, row, vec, row],
        out_specs=[row, pl.BlockSpec((8, LANES), lambda i: (0, 0))],
        out_shape=[jax.ShapeDtypeStruct((T, D), F32), jax.ShapeDtypeStruct((8, LANES), F32)],
        compiler_params=_params(("arbitrary",)),
    )(x, m, g_post, target)


def _norm_bwd(x, g, dy, resid, out_dtype, name, tr=512):
    T = x.shape[0]
    has_r = resid is not None
    row = pl.BlockSpec((tr, D), lambda i: (i, 0))
    vec = pl.BlockSpec((1, D), lambda i: (0, 0))

    def body(*refs):
        if has_r:
            x_ref, g_ref, dy_ref, r_ref, dx_ref, dg_ref, cs_ref = refs
        else:
            x_ref, g_ref, dy_ref, dx_ref, dg_ref, cs_ref = refs
        xv, dyv = x_ref[...], dy_ref[...]
        r = lax.rsqrt(jnp.mean(xv * xv, axis=-1, keepdims=True) + RMS_EPS)
        gy = dyv * g_ref[...]
        c = jnp.sum(gy * xv, axis=-1, keepdims=True) * (1.0 / D)
        dx = r * gy - xv * (r * r * r * c)

        @pl.when(pl.program_id(0) == 0)
        def _():
            dg_ref[...] = jnp.zeros_like(dg_ref)
            cs_ref[...] = jnp.zeros_like(cs_ref)

        dg_ref[...] += jnp.sum(dyv * (xv * r), axis=0, keepdims=True)
        cs_ref[...] += jnp.sum(dx, axis=0, keepdims=True)
        if has_r:
            dx = dx + r_ref[...]
        dx_ref[...] = dx.astype(dx_ref.dtype)

    args = (x, g, dy) + ((resid,) if has_r else ())
    return pl.pallas_call(
        body, name=name, grid=(T // tr,), in_specs=[row, vec, row] + ([row] if has_r else []),
        out_specs=[row, vec, vec],
        out_shape=[jax.ShapeDtypeStruct((T, D), out_dtype), jax.ShapeDtypeStruct((1, D), F32),
                   jax.ShapeDtypeStruct((1, D), F32)],
        compiler_params=_params(("arbitrary",)),
    )(*args)


CONV_TB = 256
CONV_RC = 32


def _glu(a):
    return a[:, :D] * jax.nn.sigmoid(a[:, D:])


def _conv_fwd(a, dw_w, dw_b, ln_g, ln_b, name):
    T = a.shape[0]
    tb = min(CONV_TB, T)
    per = tb // HALO
    vec = pl.BlockSpec((1, D), lambda i: (0, 0))

    def body(cur_ref, prev_ref, w_ref, b_ref, lg_ref, lb_ref, y_ref, z_ref, uwin):
        i = pl.program_id(0)
        uwin[0:HALO, :] = jnp.where(i > 0, _glu(prev_ref[...]), 0.0)
        uwin[HALO:, :] = _glu(cur_ref[...])
        for r in range(tb // CONV_RC):
            acc = jnp.broadcast_to(b_ref[...], (CONV_RC, D))
            for k in range(CONV_W):
                start = r * CONV_RC + HALO - (CONV_W - 1) + k
                acc = acc + uwin[start:start + CONV_RC, :] * w_ref[k:k + 1, :]
            rows = slice(r * CONV_RC, (r + 1) * CONV_RC)
            y_ref[rows, :] = acc
            mu = jnp.mean(acc, axis=-1, keepdims=True)
            xc = acc - mu
            var = jnp.mean(xc * xc, axis=-1, keepdims=True)
            n = xc * lax.rsqrt(var + LN_EPS) * lg_ref[...] + lb_ref[...]
            z_ref[rows, :] = (n * jax.nn.sigmoid(n)).astype(BF16)

    return pl.pallas_call(
        body, name=name, grid=(T // tb,),
        in_specs=[pl.BlockSpec((tb, 2 * D), lambda i: (i, 0)),
                  pl.BlockSpec((HALO, 2 * D), lambda i: (jnp.maximum(i * per - 1, 0), 0)),
                  pl.BlockSpec((HALO, D), lambda i: (0, 0)), vec, vec, vec],
        out_specs=[pl.BlockSpec((tb, D), lambda i: (i, 0)), pl.BlockSpec((tb, D), lambda i: (i, 0))],
        out_shape=[jax.ShapeDtypeStruct((T, D), F32), jax.ShapeDtypeStruct((T, D), BF16)],
        scratch_shapes=[pltpu.VMEM((tb + HALO, D), F32)],
        compiler_params=_params(("parallel",)),
    )(a, a, dw_w, dw_b, ln_g, ln_b)


def _conv_bwd_ln(dz, y, ln_g, ln_b, name, tr=256):
    T = y.shape[0]
    tr = min(tr, T)
    row = pl.BlockSpec((tr, D), lambda i: (i, 0))
    vec = pl.BlockSpec((1, D), lambda i: (0, 0))

    def body(dz_ref, y_ref, lg_ref, lb_ref, dy_ref, dlg_ref, dlb_ref, db_ref):
        yv = y_ref[...]
        mu = jnp.mean(yv, axis=-1, keepdims=True)
        xc = yv - mu
        rstd = lax.rsqrt(jnp.mean(xc * xc, axis=-1, keepdims=True) + LN_EPS)
        yh = xc * rstd
        n = yh * lg_ref[...] + lb_ref[...]
        sg = jax.nn.sigmoid(n)
        dn = dz_ref[...] * (sg * (1.0 + n * (1.0 - sg)))
        dyh = dn * lg_ref[...]
        dyv = rstd * (dyh - jnp.mean(dyh, axis=-1, keepdims=True) - yh * jnp.mean(dyh * yh, axis=-1, keepdims=True))
        dy_ref[...] = dyv

        @pl.when(pl.program_id(0) == 0)
        def _():
            dlg_ref[...] = jnp.zeros_like(dlg_ref)
            dlb_ref[...] = jnp.zeros_like(dlb_ref)
            db_ref[...] = jnp.zeros_like(db_ref)

        dlg_ref[...] += jnp.sum(dn * yh, axis=0, keepdims=True)
        dlb_ref[...] += jnp.sum(dn, axis=0, keepdims=True)
        db_ref[...] += jnp.sum(dyv, axis=0, keepdims=True)

    return pl.pallas_call(
        body, name=name, grid=(T // tr,), in_specs=[row, row, vec, vec], out_specs=[row, vec, vec, vec],
        out_shape=[jax.ShapeDtypeStruct((T, D), F32)] + [jax.ShapeDtypeStruct((1, D), F32)] * 3,
        compiler_params=_params(("arbitrary",)),
    )(dz, y, ln_g, ln_b)


def _conv_bwd_dw(dy, a, dw_w, name):
    T = a.shape[0]
    tb = min(CONV_TB, T)
    per = tb // HALO
    last_halo = T // HALO - 1
    n_steps = T // tb

    def body(dyc_ref, dyn_ref, cur_ref, prev_ref, w_ref, da_ref, dbias_ref, dw_ref, uwin, dywin, dwacc):
        i = pl.program_id(0)

        @pl.when(i == 0)
        def _():
            dbias_ref[...] = jnp.zeros_like(dbias_ref)
            dwacc[...] = jnp.zeros_like(dwacc)

        uwin[0:HALO, :] = jnp.where(i > 0, _glu(prev_ref[...]), 0.0)
        uwin[HALO:, :] = _glu(cur_ref[...])
        dywin[0:tb, :] = dyc_ref[...]
        dywin[tb:, :] = jnp.where(i < n_steps - 1, dyn_ref[...], 0.0)
        for r in range(tb // CONV_RC):
            rows = slice(r * CONV_RC, (r + 1) * CONV_RC)
            dy_c = dywin[rows, :]
            du = jnp.zeros((CONV_RC, D), F32)
            for k in range(CONV_W):
                back = r * CONV_RC + (CONV_W - 1) - k
                du = du + dywin[back:back + CONV_RC, :] * w_ref[k:k + 1, :]
                fwd = r * CONV_RC + HALO - (CONV_W - 1) + k
                prod = dy_c * uwin[fwd:fwd + CONV_RC, :]
                part = prod[0:8, :]
                for q in range(1, CONV_RC // 8):
                    part = part + prod[8 * q:8 * q + 8, :]
                dwacc[8 * k:8 * k + 8, :] += part
            av = cur_ref[rows, :]
            a1, sg = av[:, :D], jax.nn.sigmoid(av[:, D:])
            da1 = du * sg
            da2 = du * a1 * (sg * (1.0 - sg))
            da_ref[rows, 0:D] = da1.astype(BF16)
            da_ref[rows, D:] = da2.astype(BF16)
            dbias_ref[:, 0:D] += jnp.sum(da1, axis=0, keepdims=True)
            dbias_ref[:, D:] += jnp.sum(da2, axis=0, keepdims=True)

        @pl.when(i == n_steps - 1)
        def _():
            for k in range(CONV_W):
                dw_ref[k:k + 1, :] = jnp.sum(dwacc[8 * k:8 * k + 8, :], axis=0, keepdims=True)
            dw_ref[CONV_W:, :] = jnp.zeros((HALO - CONV_W, D), F32)

    return pl.pallas_call(
        body, name=name, grid=(n_steps,),
        in_specs=[pl.BlockSpec((tb, D), lambda i: (i, 0)),
                  pl.BlockSpec((HALO, D), lambda i: (jnp.minimum((i + 1) * per, last_halo), 0)),
                  pl.BlockSpec((tb, 2 * D), lambda i: (i, 0)),
                  pl.BlockSpec((HALO, 2 * D), lambda i: (jnp.maximum(i * per - 1, 0), 0)),
                  pl.BlockSpec((HALO, D), lambda i: (0, 0))],
        out_specs=[pl.BlockSpec((tb, 2 * D), lambda i: (i, 0)), pl.BlockSpec((1, 2 * D), lambda i: (0, 0)),
                   pl.BlockSpec((HALO, D), lambda i: (0, 0))],
        out_shape=[jax.ShapeDtypeStruct((T, 2 * D), BF16), jax.ShapeDtypeStruct((1, 2 * D), F32),
                   jax.ShapeDtypeStruct((HALO, D), F32)],
        scratch_shapes=[pltpu.VMEM((tb + HALO, D), F32), pltpu.VMEM((tb + HALO, D), F32),
                        pltpu.VMEM((8 * HALO, D), F32)],
        compiler_params=_params(("arbitrary",)),
    )(dy, dy, a, a, dw_w)


GATE_TB = 512


def _gate_fwd(fl, b_f, name):
    T = fl.shape[0]
    tb = min(GATE_TB, T)
    row = pl.BlockSpec((tb, LANES), lambda i: (i, 0))

    def body(fl_ref, b_ref, f_ref, f1_ref, f2_ref, f3_ref, carry):
        @pl.when(pl.program_id(0) == 0)
        def _():
            carry[...] = jnp.zeros_like(carry)

        z = fl_ref[...] + b_ref[...]
        lf = jnp.minimum(z, 0.0) - jnp.log(1.0 + jnp.exp(-jnp.abs(z)))
        tri = (lax.broadcasted_iota(jnp.int32, (tb, tb), 0) >= lax.broadcasted_iota(jnp.int32, (tb, tb), 1)).astype(F32)
        f = jnp.dot(tri, lf, precision=lax.Precision.HIGHEST, preferred_element_type=F32) + carry[...]
        carry[...] = f[tb - 1:tb, :]
        f_ref[...] = f
        f1_ref[...], f2_ref[...], f3_ref[...] = _split3(f)

    return pl.pallas_call(
        body, name=name, grid=(T // tb,), in_specs=[row, pl.BlockSpec((1, LANES), lambda i: (0, 0))],
        out_specs=[row] * 4, out_shape=[jax.ShapeDtypeStruct((T, LANES), F32)] * 4,
        scratch_shapes=[pltpu.VMEM((1, LANES), F32)],
        compiler_params=_params(("arbitrary",)),
    )(fl, b_f)


def _gate_bwd(dF_q, dF_k, fl, b_f, name):
    T = fl.shape[0]
    tb = min(GATE_TB, T)
    nb = T // tb
    row = pl.BlockSpec((tb, LANES), lambda i: (nb - 1 - i, 0))
    vec = pl.BlockSpec((1, LANES), lambda i: (0, 0))

    def body(dfq_ref, dfk_ref, fl_ref, b_ref, dfl_ref, db_ref, carry):
        @pl.when(pl.program_id(0) == 0)
        def _():
            carry[...] = jnp.zeros_like(carry)
            db_ref[...] = jnp.zeros_like(db_ref)

        tri = (lax.broadcasted_iota(jnp.int32, (tb, tb), 0) <= lax.broadcasted_iota(jnp.int32, (tb, tb), 1)).astype(F32)
        rc = jnp.dot(tri, dfq_ref[...] - dfk_ref[...], precision=lax.Precision.HIGHEST,
                     preferred_element_type=F32) + carry[...]
        carry[...] = rc[0:1, :]
        dfl = rc * jax.nn.sigmoid(-(fl_ref[...] + b_ref[...]))
        dfl_ref[...] = dfl
        db_ref[...] += jnp.sum(dfl, axis=0, keepdims=True)

    return pl.pallas_call(
        body, name=name, grid=(nb,), in_specs=[row, row, row, vec], out_specs=[row, vec],
        out_shape=[jax.ShapeDtypeStruct((T, LANES), F32), jax.ShapeDtypeStruct((1, LANES), F32)],
        scratch_shapes=[pltpu.VMEM((1, LANES), F32)],
        compiler_params=_params(("arbitrary",)),
    )(dF_q, dF_k, fl, b_f)


ATT_TB = 512
LAYOUT_TB = 2048
NT = (((1,), (1,)), ((), ()))
NN = (((1,), (0,)), ((), ()))


def _lane_insert(base, parts, first_lane):
    lane = lax.broadcasted_iota(jnp.int32, base.shape, 1)
    out = base
    for n, p in enumerate(parts):
        out = jnp.where(lane == first_lane + n, p, out)
    return out


def _causal(tb):
    return lax.broadcasted_iota(jnp.int32, (tb, tb), 0) >= lax.broadcasted_iota(jnp.int32, (tb, tb), 1)


def _other_head(x):
    return pltpu.roll(x, HEAD_DIM, 1)


def _attn_pack(proj, f1, f2, f3, name):
    T = proj.shape[0]
    tb = min(LAYOUT_TB, T)
    pairs = N_HEADS // 2
    cols = lambda first: pl.BlockSpec((tb, LANES), lambda p, i: (i, first + p))
    gate = pl.BlockSpec((tb, LANES), lambda p, i: (i, 0))
    heads = pl.BlockSpec((2, tb, LANES), lambda p, i: (p, i, 0))
    norm = pl.BlockSpec((2, 8, LANES), lambda p, i: (p, 0, 0))

    def body(q_ref, k_ref, v_ref, f1_ref, f2_ref, f3_ref, qa_ref, ka_ref, va_ref, qn_ref, kn_ref):
        p, i = pl.program_id(0), pl.program_id(1)
        lane = lax.broadcasted_iota(jnp.int32, (tb, LANES), 1)
        data = lane < HEAD_DIM

        @pl.when(i == 0)
        def _():
            qn_ref[...] = jnp.zeros_like(qn_ref)
            kn_ref[...] = jnp.zeros_like(kn_ref)

        qv, kv, vv = q_ref[...] * 0.125, k_ref[...], v_ref[...]
        for e in range(2):
            fcol = [jnp.sum(jnp.where(lane == 2 * p + e, f[...], 0.0), axis=-1, keepdims=True)
                    for f in (f1_ref, f2_ref, f3_ref)]
            pick = (lambda t: t) if e == 0 else _other_head
            qd = jnp.where(data, pick(qv), 0.0).astype(BF16).astype(F32)
            kd = jnp.where(data, pick(kv), 0.0).astype(BF16).astype(F32)
            vd = jnp.where(data, pick(vv), 0.0)
            qa_ref[e] = _lane_insert(qd, fcol + [1.0, 1.0, 1.0], AUG_F).astype(BF16)
            ka_ref[e] = _lane_insert(kd, [1.0, 1.0, 1.0] + [-f for f in fcol] + [1.0, 1.0, 1.0], AUG_F).astype(BF16)
            va_ref[e] = _lane_insert(vd, [-1.0, -1.0, -1.0, 1.0], AUG_F).astype(BF16)
            qn_ref[e] = jnp.maximum(qn_ref[e], jnp.max(jnp.sum(qd * qd, axis=-1, keepdims=True)))
            kn_ref[e] = jnp.maximum(kn_ref[e], jnp.max(jnp.sum(kd * kd, axis=-1, keepdims=True)))

    aug = jax.ShapeDtypeStruct((N_HEADS, T, LANES), BF16)
    nrm = jax.ShapeDtypeStruct((N_HEADS, 8, LANES), F32)
    return pl.pallas_call(
        body, name=name, grid=(pairs, T // tb),
        in_specs=[cols(0), cols(pairs), cols(2 * pairs), gate, gate, gate],
        out_specs=[heads, heads, heads, norm, norm], out_shape=[aug, aug, aug, nrm, nrm],
        compiler_params=_params(("parallel", "arbitrary")),
    )(proj, proj, proj, f1, f2, f3)


def _merge_heads(x_aug, scale, out_dtype, name, column=None):
    H, T, _ = x_aug.shape
    tb = min(LAYOUT_TB, T)

    def body(x_ref, o_ref, *col_ref):
        p = pl.program_id(1)
        lane = lax.broadcasted_iota(jnp.int32, (tb, LANES), 1)
        x0, x1 = x_ref[0], x_ref[1]
        o_ref[...] = (jnp.where(lane < HEAD_DIM, x0, _other_head(x1)) * scale).astype(out_dtype)
        if column is not None:
            @pl.when(p == 0)
            def _():
                col_ref[0][...] = jnp.zeros_like(col_ref[0])

            c0 = jnp.sum(jnp.where(lane == column, x0, 0.0), axis=-1, keepdims=True)
            c1 = jnp.sum(jnp.where(lane == column, x1, 0.0), axis=-1, keepdims=True)
            col_ref[0][...] += jnp.where(lane == 2 * p, c0, 0.0) + jnp.where(lane == 2 * p + 1, c1, 0.0)

    out_specs = [pl.BlockSpec((tb, LANES), lambda i, p: (i, p))]
    out_shape = [jax.ShapeDtypeStruct((T, D), out_dtype)]
    if column is not None:
        out_specs.append(pl.BlockSpec((tb, LANES), lambda i, p: (i, 0)))
        out_shape.append(jax.ShapeDtypeStruct((T, LANES), F32))
    return pl.pallas_call(
        body, name=name, grid=(T // tb, H // 2),
        in_specs=[pl.BlockSpec((2, tb, LANES), lambda i, p: (p, i, 0))],
        out_specs=out_specs, out_shape=out_shape,
        compiler_params=_params(("parallel", "arbitrary")),
    )(x_aug)


def _attn_prep(d_o, o_aug, name):
    H, T, _ = o_aug.shape
    tb = min(LAYOUT_TB, T)
    heads = pl.BlockSpec((2, tb, LANES), lambda p, i: (p, i, 0))

    def body(do_ref, o_ref, out_ref):
        lane = lax.broadcasted_iota(jnp.int32, (tb, LANES), 1)
        dov = do_ref[...]
        for e in range(2):
            d_e = jnp.where(lane < HEAD_DIM, dov if e == 0 else _other_head(dov), 0.0)
            delta = jnp.sum(d_e * o_ref[e], axis=-1, keepdims=True)
            out_ref[e] = _lane_insert(d_e, _split3(delta), AUG_F).astype(BF16)

    return pl.pallas_call(
        body, name=name, grid=(H // 2, T // tb),
        in_specs=[pl.BlockSpec((tb, LANES), lambda p, i: (i, p)), heads], out_specs=heads,
        out_shape=jax.ShapeDtypeStruct((H, T, LANES), BF16),
        compiler_params=_params(("parallel", "parallel")),
    )(d_o, o_aug)


def _flash_spec(n_in, n_out, H, nb, tb, T, resident, scratch):
    blk = pl.BlockSpec((None, tb, LANES), lambda h, i, *_: (h, i, 0))
    whole = pl.BlockSpec((None, T, LANES), lambda h, i, *_: (h, 0, 0))
    return pltpu.PrefetchScalarGridSpec(
        num_scalar_prefetch=3, grid=(H, nb),
        in_specs=[whole if resident[n] else blk for n in range(n_in)],
        out_specs=[blk] * n_out, scratch_shapes=scratch)


def _first_live(qk_ref, fs_ref, fe_ref, h, i):
    top = qk_ref[h] + fs_ref[h, i]
    return lax.fori_loop(0, i, lambda j, n: n + jnp.where(top - fe_ref[h, j] < SKIP_BELOW, 1, 0), 0)


def _flash_fwd(qk, fs, fe, q_aug, k_aug, v_aug, name):
    H, T, _ = q_aug.shape
    tb = min(ATT_TB, T)

    def body(qk_ref, fs_ref, fe_ref, q_ref, k_ref, v_ref, o_ref, q2_ref, m_s, acc_s):
        h, i = pl.program_id(0), pl.program_id(1)
        q = q_ref[...]
        m_s[...] = jnp.full(m_s.shape, -jnp.inf, F32)
        acc_s[...] = jnp.zeros_like(acc_s)

        def step(kb, masked):
            rows = pl.ds(pl.multiple_of(kb * tb, tb), tb)
            s = lax.dot_general(q, k_ref[rows, :], NT, preferred_element_type=F32)
            if masked:
                s = jnp.where(_causal(tb), s, MASK_VALUE)
            m_old = m_s[...]
            m_new = jnp.maximum(m_old, jnp.max(s, axis=-1, keepdims=True))
            p = jnp.exp(s - m_new)
            acc_s[...] = jnp.exp(m_old - m_new) * acc_s[...] + lax.dot_general(
                p.astype(BF16), v_ref[rows, :], NN, preferred_element_type=F32)
            m_s[...] = m_new

        def loop_body(kb, carry):
            step(kb, False)
            return carry

        lax.fori_loop(_first_live(qk_ref, fs_ref, fe_ref, h, i), i, loop_body, 0)
        step(i, True)
        acc = acc_s[...]
        lane = lax.broadcasted_iota(jnp.int32, acc.shape, 1)
        l = jnp.sum(jnp.where(lane == AUG_ONE, acc, 0.0), axis=-1, keepdims=True)
        o_ref[...] = acc / l
        lse = m_s[...] + jnp.log(l)
        q2_ref[...] = _lane_insert(q.astype(F32), [-p for p in _split3(lse)], AUG_L).astype(BF16)

    return pl.pallas_call(
        body, name=name,
        grid_spec=_flash_spec(3, 2, H, T // tb, tb, T, (False, True, True),
                              [pltpu.VMEM((tb, 1), F32), pltpu.VMEM((tb, LANES), F32)]),
        out_shape=[jax.ShapeDtypeStruct((H, T, LANES), F32), jax.ShapeDtypeStruct((H, T, LANES), BF16)],
        compiler_params=_params(("parallel", "arbitrary")),
    )(qk, fs, fe, q_aug, k_aug, v_aug)


def _flash_dq(qk, fs, fe, q2, k_aug, do_aug, v_aug, name):
    H, T, _ = q2.shape
    tb = min(ATT_TB, T)

    def body(qk_ref, fs_ref, fe_ref, q_ref, k_ref, do_ref, v_ref, dq_ref, acc_s):
        h, i = pl.program_id(0), pl.program_id(1)
        q, dov = q_ref[...], do_ref[...]
        acc_s[...] = jnp.zeros_like(acc_s)

        def step(kb, masked):
            rows = pl.ds(pl.multiple_of(kb * tb, tb), tb)
            k = k_ref[rows, :]
            p = jnp.exp(lax.dot_general(q, k, NT, preferred_element_type=F32))
            if masked:
                p = jnp.where(_causal(tb), p, 0.0)
            ds = p * lax.dot_general(dov, v_ref[rows, :], NT, preferred_element_type=F32)
            acc_s[...] += lax.dot_general(ds.astype(BF16), k, NN, preferred_element_type=F32)

        def loop_body(kb, carry):
            step(kb, False)
            return carry

        lax.fori_loop(_first_live(qk_ref, fs_ref, fe_ref, h, i), i, loop_body, 0)
        step(i, True)
        dq_ref[...] = acc_s[...]

    return pl.pallas_call(
        body, name=name,
        grid_spec=_flash_spec(4, 1, H, T // tb, tb, T, (False, True, False, True), [pltpu.VMEM((tb, LANES), F32)]),
        out_shape=[jax.ShapeDtypeStruct((H, T, LANES), F32)],
        compiler_params=_params(("parallel", "arbitrary")),
    )(qk, fs, fe, q2, k_aug, do_aug, v_aug)[0]


def _flash_dkv(qk, fs, fe, q2, k_aug, do_aug, v_aug, name):
    H, T, _ = q2.shape
    tb = min(ATT_TB, T)
    nb = T // tb

    def body(qk_ref, fs_ref, fe_ref, q_ref, k_ref, do_ref, v_ref, dk_ref, dv_ref, dk_s, dv_s):
        h, i = pl.program_id(0), pl.program_id(1)
        k, v = k_ref[...], v_ref[...]
        dk_s[...] = jnp.zeros_like(dk_s)
        dv_s[...] = jnp.zeros_like(dv_s)

        def step(qb, masked):
            rows = pl.ds(pl.multiple_of(qb * tb, tb), tb)
            q, dov = q_ref[rows, :], do_ref[rows, :]
            pt = jnp.exp(lax.dot_general(k, q, NT, preferred_element_type=F32))
            if masked:
                pt = jnp.where(lax.broadcasted_iota(jnp.int32, (tb, tb), 1) >= lax.broadcasted_iota(jnp.int32, (tb, tb), 0),
                               pt, 0.0)
            dst = pt * lax.dot_general(v, dov, NT, preferred_element_type=F32)
            dv_s[...] += lax.dot_general(pt.astype(BF16), dov, NN, preferred_element_type=F32)
            dk_s[...] += lax.dot_general(dst.astype(BF16), q, NN, preferred_element_type=F32)

        step(i, True)

        def loop_body(qb, carry):
            step(qb, False)
            return carry

        base = qk_ref[h] - fe_ref[h, i]
        live = lax.fori_loop(i + 1, nb, lambda b, n: n + jnp.where(base + fs_ref[h, b] < SKIP_BELOW, 0, 1), 0)
        lax.fori_loop(i + 1, i + 1 + live, loop_body, 0)
        dk_ref[...] = dk_s[...]
        dv_ref[...] = dv_s[...]

    return pl.pallas_call(
        body, name=name,
        grid_spec=_flash_spec(4, 2, H, nb, tb, T, (True, False, True, False),
                              [pltpu.VMEM((tb, LANES), F32), pltpu.VMEM((tb, LANES), F32)]),
        out_shape=[jax.ShapeDtypeStruct((H, T, LANES), F32)] * 2,
        compiler_params=_params(("parallel", "arbitrary")),
    )(qk, fs, fe, q2, k_aug, do_aug, v_aug)


def _add_pair(place, slab, got, name):
    _, half, cols = got.shape
    grid_spec = pltpu.PrefetchScalarGridSpec(
        num_scalar_prefetch=1, grid=(N_CHIPS,),
        in_specs=[pl.BlockSpec((None, half, cols), lambda j, s: (j, s[4], 0)),
                  pl.BlockSpec((None, half, cols), lambda j, s: (j, 0, 0))],
        out_specs=pl.BlockSpec((None, half, cols), lambda j, s: (j, 0, 0)))

    def body(s_ref, a_ref, b_ref, o_ref):
        o_ref[...] = (a_ref[...] + b_ref[...]).astype(BF16)

    return pl.pallas_call(
        body, name=name, grid_spec=grid_spec, out_shape=jax.ShapeDtypeStruct(got.shape, BF16),
        compiler_params=_params(("parallel",)),
    )(place, slab, got)


def _add_chips(place, pair, by_chip, name):
    _, half, cols = pair.shape
    tb = min(256, half)
    steps = half // tb
    slot = lambda n: pl.BlockSpec((None, tb, cols), lambda i, s: (s[n], i, 0))
    grid_spec = pltpu.PrefetchScalarGridSpec(
        num_scalar_prefetch=1, grid=(steps,), in_specs=[slot(0), slot(1), slot(2), slot(3)],
        out_specs=pl.BlockSpec((tb, cols), lambda i, s: (s[4] * steps + i, 0)))

    def body(s_ref, own_ref, b1_ref, b2_ref, b3_ref, o_ref):
        o_ref[...] = ((own_ref[...].astype(F32) + b1_ref[...].astype(F32)) + b2_ref[...].astype(F32)) \
            + b3_ref[...].astype(F32)

    return pl.pallas_call(
        body, name=name, grid_spec=grid_spec, out_shape=jax.ShapeDtypeStruct((2 * half, cols), F32),
        compiler_params=_params(("parallel",)),
    )(place, pair, by_chip, by_chip, by_chip)


def _add_slots(b, name, tb=128):
    n, rows, cols = b.shape
    tb = tb if rows % tb == 0 else rows
    specs = [pl.BlockSpec((None, tb, cols), functools.partial(lambda j, i: (j, i, 0), j)) for j in range(n)]

    def body(*refs):
        acc = refs[0][...]
        for r in refs[1:n]:
            acc = acc + r[...]
        refs[n][...] = acc

    return pl.pallas_call(
        body, name=name, grid=(rows // tb,), in_specs=specs, out_specs=pl.BlockSpec((tb, cols), lambda i: (i, 0)),
        out_shape=jax.ShapeDtypeStruct((rows, cols), F32), compiler_params=_params(("parallel",)),
    )(*([b] * n))


def _adamw(w, g, m, v, name):
    rows, cols = w.shape
    tr = 256 if rows % 256 == 0 else rows
    blk = pl.BlockSpec((tr, cols), lambda i: (i, 0))

    def body(w_ref, g_ref, m_ref, v_ref, d_ref, nm_ref, nv_ref):
        gv = g_ref[...]
        nm = ADAM_B1 * m_ref[...] + (1.0 - ADAM_B1) * gv
        nv = ADAM_B2 * v_ref[...] + (1.0 - ADAM_B2) * (gv * gv)
        m_hat = nm / (1.0 - ADAM_B1 ** ADAM_STEP)
        v_hat = nv / (1.0 - ADAM_B2 ** ADAM_STEP)
        d_ref[...] = -ADAM_LR * (m_hat / (jnp.sqrt(v_hat) + ADAM_EPS) + ADAM_WD * w_ref[...])
        nm_ref[...] = nm
        nv_ref[...] = nv

    return pl.pallas_call(
        body, name=name, grid=(rows // tr,), in_specs=[blk] * 4, out_specs=[blk] * 3,
        out_shape=[jax.ShapeDtypeStruct((rows, cols), F32)] * 3, compiler_params=_params(("parallel",)),
    )(w, g, m, v)


ANY = pl.BlockSpec(memory_space=pl.ANY)


def _place():
    x, y, c = lax.axis_index("x"), lax.axis_index("y"), lax.axis_index("c")
    others = [(1 - x, y), (x, 1 - y), (1 - x, 1 - y)]
    return x, y, c, 2 * x + y, others


def _half(ref, shape, h):
    if len(shape) == 3:
        return ref.at[h]
    return ref.at[pl.ds(h * (shape[0] // 2), shape[0] // 2), :]


def _allgather_weights(shards):
    n_u = len(GATHER_UNITS)

    def body(*refs):
        ins, outs = refs[:n_u], refs[n_u:2 * n_u]
        send_sems, recv_sems, own_send_sems, own_recv_sems = refs[2 * n_u:]
        x, y, c, me, others = _place()
        sibling = (x, y, 1 - c)

        def copy(u, k, chip, h, to, src=None):
            dst = _half(outs[u].at[chip], GATHER_UNITS[u][1], h)
            return pltpu.make_async_remote_copy(
                src_ref=dst if src is None else src, dst_ref=dst, send_sem=send_sems.at[6 * u + k],
                recv_sem=recv_sems.at[6 * u + k], device_id=to, device_id_type=MESH)

        mine = [pltpu.make_async_remote_copy(
            src_ref=ins[u], dst_ref=outs[u].at[me], send_sem=own_send_sems.at[u], recv_sem=own_recv_sems.at[u],
            device_id=sibling, device_id_type=MESH) for u in range(n_u)]
        for cp in mine:
            cp.start()
        first = [copy(u, j, me, c, (ox, oy, c), src=_half(ins[u], GATHER_UNITS[u][1], c))
                 for u in range(n_u) for j, (ox, oy) in enumerate(others)]
        for cp in first:
            cp.start()
        passed = []
        for u in range(n_u):
            for j, (ox, oy) in enumerate(others):
                copy(u, j, 2 * ox + oy, c, (x, y, c)).wait_recv()
                passed.append(copy(u, 3 + j, 2 * ox + oy, c, sibling))
                passed[-1].start()
        for u in range(n_u):
            for j, (ox, oy) in enumerate(others):
                copy(u, 3 + j, 2 * ox + oy, 1 - c, (x, y, c)).wait_recv()
        for cp in first + passed:
            cp.wait_send()
        for cp in mine:
            cp.wait()

    return pl.pallas_call(
        body, name="allgather_weights", in_specs=[ANY] * n_u, out_specs=[ANY] * n_u,
        out_shape=[jax.ShapeDtypeStruct((N_CHIPS,) + shape, BF16) for _, shape in GATHER_UNITS],
        scratch_shapes=[pltpu.SemaphoreType.DMA((6 * n_u,)), pltpu.SemaphoreType.DMA((6 * n_u,)),
                        pltpu.SemaphoreType.DMA((n_u,)), pltpu.SemaphoreType.DMA((n_u,))],
    )(*shards)


def _sibling_exchange(slabs):
    n_u = len(GRAD_UNITS)

    def body(*refs):
        ins, got = refs[:n_u], refs[n_u:2 * n_u]
        send_sems, recv_sems = refs[2 * n_u:]
        x, y, c, _, _ = _place()
        copies = []
        for u, (_, (rows, _), _) in enumerate(GRAD_UNITS):
            half = rows // 2
            copies.append(pltpu.make_async_remote_copy(
                src_ref=ins[u].at[:, pl.ds((1 - c) * half, half), :], dst_ref=got[u], send_sem=send_sems.at[u],
                recv_sem=recv_sems.at[u], device_id=(x, y, 1 - c), device_id_type=MESH))
        for cp in copies:
            cp.start()
        for cp in copies:
            cp.wait()

    return pl.pallas_call(
        body, name="grad_sibling_exchange", in_specs=[ANY] * n_u, out_specs=[ANY] * n_u,
        out_shape=[jax.ShapeDtypeStruct((N_CHIPS, rows // 2, cols), F32) for _, (rows, cols), _ in GRAD_UNITS],
        scratch_shapes=[pltpu.SemaphoreType.DMA((n_u,)), pltpu.SemaphoreType.DMA((n_u,))],
    )(*slabs)


def _chip_exchange(pairs, small):
    n_u = len(GRAD_UNITS)

    def body(*refs):
        ins, s_ref = refs[:n_u], refs[n_u]
        outs, sall_ref = refs[n_u + 1:2 * n_u + 1], refs[2 * n_u + 1]
        send_sems, recv_sems, ssend_sems, srecv_sems, local_sem = refs[2 * n_u + 2:]
        x, y, c, me, others = _place()
        dev = 4 * x + 2 * y + c
        local = [pltpu.make_async_copy(s_ref, sall_ref.at[dev], local_sem)]
        sends = [pltpu.make_async_remote_copy(
            src_ref=ins[u].at[2 * ox + oy], dst_ref=outs[u].at[me], send_sem=send_sems.at[3 * u + j],
            recv_sem=recv_sems.at[3 * u + j], device_id=(ox, oy, c), device_id_type=MESH)
            for u in range(n_u) for j, (ox, oy) in enumerate(others)]
        flips = [(fx, fy, fc) for fx in (0, 1) for fy in (0, 1) for fc in (0, 1)][1:]
        sends += [pltpu.make_async_remote_copy(
            src_ref=s_ref, dst_ref=sall_ref.at[dev], send_sem=ssend_sems.at[n], recv_sem=srecv_sems.at[n],
            device_id=(x ^ fx, y ^ fy, c ^ fc), device_id_type=MESH) for n, (fx, fy, fc) in enumerate(flips)]
        for cp in local + sends:
            cp.start()
        for u in range(n_u):
            for j, (ox, oy) in enumerate(others):
                pltpu.make_async_remote_copy(
                    src_ref=ins[u].at[me], dst_ref=outs[u].at[2 * ox + oy], send_sem=send_sems.at[3 * u + j],
                    recv_sem=recv_sems.at[3 * u + j], device_id=(x, y, c), device_id_type=MESH).wait_recv()
        for n, (fx, fy, fc) in enumerate(flips):
            src_dev = 4 * (x ^ fx) + 2 * (y ^ fy) + (c ^ fc)
            pltpu.make_async_remote_copy(
                src_ref=s_ref, dst_ref=sall_ref.at[src_dev], send_sem=ssend_sems.at[n], recv_sem=srecv_sems.at[n],
                device_id=(x, y, c), device_id_type=MESH).wait_recv()
        for cp in sends:
            cp.wait_send()
        for cp in local:
            cp.wait()

    outs = pl.pallas_call(
        body, name="grad_chip_exchange", in_specs=[ANY] * (n_u + 1), out_specs=[ANY] * (n_u + 1),
        out_shape=[jax.ShapeDtypeStruct(p.shape, p.dtype) for p in pairs]
        + [jax.ShapeDtypeStruct((8, SMALL_ALL_ROWS, LANES), F32)],
        scratch_shapes=[pltpu.SemaphoreType.DMA((3 * n_u,)), pltpu.SemaphoreType.DMA((3 * n_u,)),
                        pltpu.SemaphoreType.DMA((7,)), pltpu.SemaphoreType.DMA((7,)), pltpu.SemaphoreType.DMA],
    )(*pairs, small)
    return outs[:n_u], outs[n_u]


def _sibling_share(grads):
    n_u = len(GRAD_UNITS)

    def body(*refs):
        ins, outs = refs[:n_u], refs[n_u:2 * n_u]
        send_sems, recv_sems = refs[2 * n_u:]
        x, y, c, _, _ = _place()

        def rows_of(ref, u, h):
            half = GRAD_UNITS[u][1][0] // 2
            return ref.at[pl.ds(h * half, half), :]

        sends = [pltpu.make_async_remote_copy(
            src_ref=rows_of(ins[u], u, c), dst_ref=rows_of(outs[u], u, c), send_sem=send_sems.at[u],
            recv_sem=recv_sems.at[u], device_id=(x, y, 1 - c), device_id_type=MESH) for u in range(n_u)]
        for cp in sends:
            cp.start()
        for u in range(n_u):
            pltpu.make_async_remote_copy(
                src_ref=rows_of(ins[u], u, c), dst_ref=rows_of(outs[u], u, 1 - c), send_sem=send_sems.at[u],
                recv_sem=recv_sems.at[u], device_id=(x, y, c), device_id_type=MESH).wait_recv()
        for cp in sends:
            cp.wait_send()

    return pl.pallas_call(
        body, name="grad_sibling_share", in_specs=[ANY] * n_u, out_specs=[ANY] * n_u,
        out_shape=[jax.ShapeDtypeStruct(g.shape, F32) for g in grads],
        input_output_aliases={u: u for u in range(n_u)},
        scratch_shapes=[pltpu.SemaphoreType.DMA((n_u,)), pltpu.SemaphoreType.DMA((n_u,))],
    )(*grads)


def kernel(x, g_mix_pre, g_mix_post, g_ffn_pre, g_ffn_post, conv_pw1_w, conv_pw1_b, conv_dw_w, conv_dw_b, conv_ln_g, conv_ln_b, conv_pw2_w, conv_pw2_b, attn_w_in, attn_b_f, attn_w_o, mlp_w_up, mlp_w_down, loss_target, m_g_mix_pre, m_g_mix_post, m_g_ffn_pre, m_g_ffn_post, m_conv_pw1_w, m_conv_pw1_b, m_conv_dw_w, m_conv_dw_b, m_conv_ln_g, m_conv_ln_b, m_conv_pw2_w, m_conv_pw2_b, m_attn_w_in, m_attn_b_f, m_attn_w_o, m_mlp_w_up, m_mlp_w_down, v_g_mix_pre, v_g_mix_post, v_g_ffn_pre, v_g_ffn_post, v_conv_pw1_w, v_conv_pw1_b, v_conv_dw_w, v_conv_dw_b, v_conv_ln_g, v_conv_ln_b, v_conv_pw2_w, v_conv_pw2_b, v_attn_w_in, v_attn_b_f, v_attn_w_o, v_mlp_w_up, v_mlp_w_down):
    weights = dict(g_mix_pre=g_mix_pre, g_mix_post=g_mix_post, g_ffn_pre=g_ffn_pre, g_ffn_post=g_ffn_post, conv_pw1_w=conv_pw1_w, conv_pw1_b=conv_pw1_b, conv_dw_w=conv_dw_w, conv_dw_b=conv_dw_b, conv_ln_g=conv_ln_g, conv_ln_b=conv_ln_b, conv_pw2_w=conv_pw2_w, conv_pw2_b=conv_pw2_b, attn_w_in=attn_w_in, attn_b_f=attn_b_f, attn_w_o=attn_w_o, mlp_w_up=mlp_w_up, mlp_w_down=mlp_w_down)
    mom_m = dict(g_mix_pre=m_g_mix_pre, g_mix_post=m_g_mix_post, g_ffn_pre=m_g_ffn_pre, g_ffn_post=m_g_ffn_post, conv_pw1_w=m_conv_pw1_w, conv_pw1_b=m_conv_pw1_b, conv_dw_w=m_conv_dw_w, conv_dw_b=m_conv_dw_b, conv_ln_g=m_conv_ln_g, conv_ln_b=m_conv_ln_b, conv_pw2_w=m_conv_pw2_w, conv_pw2_b=m_conv_pw2_b, attn_w_in=m_attn_w_in, attn_b_f=m_attn_b_f, attn_w_o=m_attn_w_o, mlp_w_up=m_mlp_w_up, mlp_w_down=m_mlp_w_down)
    mom_v = dict(g_mix_pre=v_g_mix_pre, g_mix_post=v_g_mix_post, g_ffn_pre=v_g_ffn_pre, g_ffn_post=v_g_ffn_post, conv_pw1_w=v_conv_pw1_w, conv_pw1_b=v_conv_pw1_b, conv_dw_w=v_conv_dw_w, conv_dw_b=v_conv_dw_b, conv_ln_g=v_conv_ln_g, conv_ln_b=v_conv_ln_b, conv_pw2_w=v_conv_pw2_w, conv_pw2_b=v_conv_pw2_b, attn_w_in=v_attn_w_in, attn_b_f=v_attn_b_f, attn_w_o=v_attn_w_o, mlp_w_up=v_mlp_w_up, mlp_w_down=v_mlp_w_down)
    order = [n for n, _ in SMALL[:4]] + ["conv_pw1_w", "conv_pw1_b", "conv_dw_w", "conv_dw_b", "conv_ln_g", "conv_ln_b",
                                          "conv_pw2_w", "conv_pw2_b", "attn_w_in", "attn_b_f", "attn_w_o", "mlp_w_up",
                                          "mlp_w_down"]
    T = x.shape[1]
    x0 = x.reshape(T, D)
    target = loss_target.reshape(T, D)
    row = lambda a, l: a[l:l + 1, :]

    shards = []
    for name, shape in GATHER_UNITS:
        w = weights[name].astype(BF16)
        w = w.reshape(w.shape[-len(shape):])
        if name == "conv_dw_w":
            w = jnp.pad(w, ((0, HALO - CONV_W), (0, 0)))
        shards.append(w)
    g_pw1, g_dw, g_pw2, g_win, g_wo, g_up, g_down = _allgather_weights(shards)
    w_pw2, w_o = g_pw2.reshape(D, D), g_wo.reshape(D, D)
    w_in = jnp.pad(g_win.transpose(1, 0, 2).reshape(D, W_IN_COLS), ((0, 0), (0, W_IN_PAD - W_IN_COLS)))
    w_dw = g_dw.transpose(1, 0, 2).reshape(HALO, D).astype(F32)
    b_f = jnp.pad(attn_b_f, ((0, 0), (0, LANES - N_HEADS)))

    def mlp_fwd(h, l):
        up, act = _matmul(h, g_up, "nn", (F32, BF16), _epi_sqrelu, f"mlp{l}_up", n=DFF, b_place=_place_cols_nn(D, l))
        (mo,) = _matmul(act, g_down, "nn", (F32,), _epi_plain, f"mlp{l}_down", n=D, b_place=_place_rows_nn(D, l))
        return up, act, mo

    (h0,) = _resnorm_fwd(x0, None, None, row(g_mix_pre, 0), "norm_in")
    (a0,) = _matmul(h0, g_pw1, "nn", (F32,), _epi_bias, "conv_pw1", extras=((conv_pw1_b, "row"),), tn=512, n=2 * D,
                    b_place=_place_cols_nn(512))
    y0, z0 = _conv_fwd(a0, w_dw, conv_dw_b, conv_ln_g, conv_ln_b, "conv_fwd")
    (m0,) = _matmul(z0, w_pw2, "nn", (F32,), _epi_bias, "conv_pw2", extras=((conv_pw2_b, "row"),))
    x1, h1 = _resnorm_fwd(x0, m0, row(g_mix_post, 0), row(g_ffn_pre, 0), "norm_conv_out")
    up0, act0, mo0 = mlp_fwd(h1, 0)
    x2, h2 = _resnorm_fwd(x1, mo0, row(g_ffn_post, 0), row(g_mix_pre, 1), "norm_mlp0_out")

    (proj,) = _matmul(h2, w_in, "nn", (F32,), _epi_plain, "attn_in", tn=640)
    fl = proj[:, 3 * D:]
    f_all, f1, f2, f3 = _gate_fwd(fl, b_f, "gate_fwd")
    q_aug, k_aug, v_aug, qn2, kn2 = _attn_pack(proj, f1, f2, f3, "attn_pack")
    tb = min(ATT_TB, T)
    qk = 2.02 * jnp.sqrt(qn2[:, 0, 0] * kn2[:, 0, 0]) + 1.0
    fs, fe = f_all[0::tb, :N_HEADS].T, f_all[tb - 1::tb, :N_HEADS].T
    o_aug, q2_aug = _flash_fwd(qk, fs, fe, q_aug, k_aug, v_aug, "flash_fwd")
    (o_bf,) = _merge_heads(o_aug, 1.0, BF16, "attn_merge_o")
    (m1,) = _matmul(o_bf, w_o, "nn", (F32,), _epi_plain, "attn_out")
    x3, h3 = _resnorm_fwd(x2, m1, row(g_mix_post, 1), row(g_ffn_pre, 1), "norm_attn_out")
    up1, act1, mo1 = mlp_fwd(h3, 1)
    dy, loss_part = _resnorm_loss(x3, mo1, row(g_ffn_post, 1), target, "loss")
    loss = lax.psum(loss_part[0, 0], ("x", "y", "c"))

    G = {}

    def mlp_bwd(d_out, mo, up, act, h, x_in, l):
        d_mo, dg_post, _ = _norm_bwd(mo, row(g_ffn_post, l), d_out, None, BF16, f"mlp{l}_post_bwd")
        (d_up,) = _matmul(d_mo, g_down, "nt", (BF16,), _epi_dsqrelu, f"mlp{l}_dact", extras=((up, "tile"),), n=DFF,
                          b_place=_place_rows_nt(D, l))
        (dw_down,) = _matmul(act, d_mo, "tn", (F32,), _epi_plain, f"mlp{l}_dwdown")
        (dw_up,) = _matmul(h, d_up, "tn", (F32,), _epi_plain, f"mlp{l}_dwup", shard_out=True)
        (d_h,) = _matmul(d_up, g_up, "nt", (F32,), _epi_plain, f"mlp{l}_dh", n=D, b_place=_place_cols_nt(D, l))
        d_in, dg_pre, _ = _norm_bwd(x_in, row(g_ffn_pre, l), d_h, d_out, F32, f"mlp{l}_pre_bwd")
        G[f"up{l}"], G[f"down{l}"] = dw_up, dw_down.reshape(N_CHIPS, D, D)
        return d_in, dg_post, dg_pre

    d_x3, dg_ffn_post1, dg_ffn_pre1 = mlp_bwd(dy, mo1, up1, act1, h3, x3, 1)

    d_m1, dg_mix_post1, _ = _norm_bwd(m1, row(g_mix_post, 1), d_x3, None, BF16, "attn_post_bwd")
    (dw_o,) = _matmul(o_bf, d_m1, "tn", (F32,), _epi_plain, "attn_dwo")
    G["attn_w_o"] = dw_o.reshape(N_CHIPS, D // N_CHIPS, D)
    (d_o,) = _matmul(d_m1, w_o, "nt", (F32,), _epi_plain, "attn_do")
    do_aug = _attn_prep(d_o, o_aug, "attn_prep")
    dq_aug = _flash_dq(qk, fs, fe, q2_aug, k_aug, do_aug, v_aug, "flash_dq")
    dk_aug, dv_aug = _flash_dkv(qk, fs, fe, q2_aug, k_aug, do_aug, v_aug, "flash_dkv")
    d_q, dF_q = _merge_heads(dq_aug, 0.125, BF16, "attn_merge_dq", column=AUG_F)
    d_k, dF_k = _merge_heads(dk_aug, 1.0, BF16, "attn_merge_dk", column=AUG_ONE)
    (d_v,) = _merge_heads(dv_aug, 1.0, BF16, "attn_merge_dv")
    d_fl, db_f = _gate_bwd(dF_q, dF_k, fl, b_f, "gate_bwd")
    d_proj = jnp.concatenate([d_q, d_k, d_v, d_fl.astype(BF16)], axis=1)
    (dw_in,) = _matmul(h2, d_proj, "tn", (F32,), _epi_plain, "attn_dwin", tn=640)
    G["attn_w_in"] = dw_in[:, :W_IN_COLS].reshape(D, N_CHIPS, W_IN_COLS // N_CHIPS).transpose(1, 0, 2)
    (d_h2,) = _matmul(d_proj, w_in, "nt", (F32,), _epi_plain, "attn_dh", tk=640)
    d_x2, dg_mix_pre1, _ = _norm_bwd(x2, row(g_mix_pre, 1), d_h2, d_x3, F32, "attn_pre_bwd")

    d_x1, dg_ffn_post0, dg_ffn_pre0 = mlp_bwd(d_x2, mo0, up0, act0, h1, x1, 0)

    d_m0, dg_mix_post0, db_pw2 = _norm_bwd(m0, row(g_mix_post, 0), d_x1, None, BF16, "conv_post_bwd")
    (dw_pw2,) = _matmul(z0, d_m0, "tn", (F32,), _epi_plain, "conv_dwpw2")
    G["conv_pw2_w"] = dw_pw2.reshape(N_CHIPS, D // N_CHIPS, D)
    (d_z0,) = _matmul(d_m0, w_pw2, "nt", (F32,), _epi_plain, "conv_dz")
    d_y0, dln_g, dln_b, ddw_b = _conv_bwd_ln(d_z0, y0, conv_ln_g, conv_ln_b, "conv_bwd_ln")
    d_a0, db_pw1, ddw_w = _conv_bwd_dw(d_y0, a0, w_dw, "conv_bwd_dw")
    (G["conv_pw1_w"],) = _matmul(h0, d_a0, "tn", (F32,), _epi_plain, "conv_dwpw1", tn=512, shard_out=True)
    (d_h0,) = _matmul(d_a0, g_pw1, "nt", (F32,), _epi_plain, "conv_dh", tk=512, n=D, b_place=_place_cols_nt(512))
    d_x0, dg_mix_pre0, _ = _norm_bwd(x0, row(g_mix_pre, 0), d_h0, d_x1, F32, "conv_pre_bwd")

    small_local = {
        "g_mix_pre": jnp.concatenate([dg_mix_pre0, dg_mix_pre1]), "g_mix_post": jnp.concatenate([dg_mix_post0, dg_mix_post1]),
        "g_ffn_pre": jnp.concatenate([dg_ffn_pre0, dg_ffn_pre1]), "g_ffn_post": jnp.concatenate([dg_ffn_post0, dg_ffn_post1]),
        "conv_pw1_b": db_pw1, "conv_dw_b": ddw_b, "conv_ln_g": dln_g, "conv_ln_b": dln_b, "conv_pw2_b": db_pw2,
        "attn_b_f": db_f[:, :N_HEADS], "conv_dw_w_full": ddw_w,
    }

    chip = 2 * lax.axis_index("x") + lax.axis_index("y")
    place = jnp.stack([chip] + [k + (k >= chip).astype(jnp.int32) for k in range(N_CHIPS - 1)]
                      + [lax.axis_index("c")]).astype(jnp.int32)
    slabs = [G[n] for n, _, _ in GRAD_UNITS]
    got = _sibling_exchange(slabs)
    pairs = [_add_pair(place, s, g, f"grad_pair_{n}") for s, g, (n, _, _) in zip(slabs, got, GRAD_UNITS)]
    by_chip, small_all = _chip_exchange(pairs, _pack([small_local[n] for n, _ in SMALL_ALL], SMALL_ALL_ROWS))
    unit_grads = _sibling_share([_add_chips(place, p, b, f"grad_chips_{n}")
                                 for p, b, (n, _, _) in zip(pairs, by_chip, GRAD_UNITS)])
    grads = _unpack(_add_slots(small_all, "grad_add_small", tb=SMALL_ALL_ROWS), SMALL_ALL)
    grads["conv_dw_w"] = lax.dynamic_slice(grads.pop("conv_dw_w_full"), (0, chip * 256), (CONV_W, 256))[None]

    delta, new_m, new_v = {}, {}, {}
    per_layer = {}
    for g, (unit, shape, (name, layer)) in zip(unit_grads, GRAD_UNITS):
        view = lambda d: (d[name] if layer is None else d[name][layer]).reshape(shape)
        per_layer.setdefault(name, []).append((g,) + tuple(_adamw(view(weights), g, view(mom_m), view(mom_v),
                                                                  f"adamw_{unit}")))
    for name, parts in per_layer.items():
        full = weights[name].shape
        join = lambda n: (parts[0][n] if len(parts) == 1 else jnp.stack([p[n] for p in parts])).reshape(full)
        grads[name], delta[name], new_m[name], new_v[name] = join(0), join(1), join(2), join(3)
    dw2 = lambda d: d["conv_dw_w"].reshape(CONV_W, 256)
    d, nm, nv = _adamw(dw2(weights), dw2(grads), dw2(mom_m), dw2(mom_v), "adamw_conv_dw_w")
    full = weights["conv_dw_w"].shape
    delta["conv_dw_w"], new_m["conv_dw_w"], new_v["conv_dw_w"] = d.reshape(full), nm.reshape(full), nv.reshape(full)
    packs = [_pack([src[n] for n, _ in SMALL], SMALL_ROWS) for src in (weights, grads, mom_m, mom_v)]
    d, nm, nv = _adamw(*packs, "adamw_small")
    for dst, packed in ((delta, d), (new_m, nm), (new_v, nv)):
        dst.update(_unpack(packed, SMALL))

    grad_x = d_x0.reshape(x.shape)
    return (loss, grad_x, *[grads[n] for n in order], *[delta[n] for n in order], *[new_m[n] for n in order],
            *[new_v[n] for n in order])
```

```python
import functools

import jax
import jax.numpy as jnp
from jax import lax
from jax.experimental import pallas as pl
from jax.experimental.pallas import tpu as pltpu

F32 = jnp.float32
BF16 = jnp.bfloat16
MESH = pl.DeviceIdType.MESH

D = 1024
DFF = 4096
N_HEADS = 16
HEAD_DIM = 64
CONV_W = 31
HALO = 32
RMS_EPS = 1e-6
LN_EPS = 1e-5
MASK_VALUE = -1e30
W_IN_COLS = 3 * D + N_HEADS
W_IN_PAD = 3200
LANES = 128
SUBLANES = 8
N_CHIPS = 4
VMEM_LIMIT = 56 * 1024 * 1024

ADAM_LR = 0.001
ADAM_B1 = 0.9
ADAM_B2 = 0.999
ADAM_EPS = 1e-08
ADAM_WD = 0.01
ADAM_STEP = 10

AUG_F = 64
AUG_ONE = 67
AUG_L = 70
SKIP_BELOW = -104.0

GATHER_UNITS = (
    ("conv_pw1_w", (1024, 512)),
    ("conv_dw_w", (HALO, 256)),
    ("conv_pw2_w", (256, 1024)),
    ("attn_w_in", (1024, 772)),
    ("attn_w_o", (256, 1024)),
    ("mlp_w_up", (2, 1024, 1024)),
    ("mlp_w_down", (2, 1024, 1024)),
)
GRAD_UNITS = (
    ("conv_pw1_w", (1024, 512), ("conv_pw1_w", None)),
    ("conv_pw2_w", (256, 1024), ("conv_pw2_w", None)),
    ("attn_w_in", (1024, 772), ("attn_w_in", None)),
    ("attn_w_o", (256, 1024), ("attn_w_o", None)),
    ("up0", (1024, 1024), ("mlp_w_up", 0)),
    ("up1", (1024, 1024), ("mlp_w_up", 1)),
    ("down0", (1024, 1024), ("mlp_w_down", 0)),
    ("down1", (1024, 1024), ("mlp_w_down", 1)),
)
SMALL = (
    ("g_mix_pre", (2, 1024)), ("g_mix_post", (2, 1024)), ("g_ffn_pre", (2, 1024)), ("g_ffn_post", (2, 1024)),
    ("conv_pw1_b", (1, 2048)), ("conv_dw_b", (1, 1024)), ("conv_ln_g", (1, 1024)), ("conv_ln_b", (1, 1024)),
    ("conv_pw2_b", (1, 1024)), ("attn_b_f", (1, 16)),
)
SMALL_ROWS = 120
SMALL_ALL = SMALL + (("conv_dw_w_full", (HALO, 1024)),)
SMALL_ALL_ROWS = 376


def _size(shape):
    n = 1
    for s in shape:
        n *= s
    return n


def _pack(arrays, rows):
    flat = jnp.concatenate([a.reshape(-1) for a in arrays])
    return jnp.pad(flat, (0, rows * LANES - flat.shape[0])).reshape(rows, LANES)


def _unpack(packed, table):
    flat = packed.reshape(-1)
    out, off = {}, 0
    for name, shape in table:
        n = _size(shape)
        out[name] = flat[off:off + n].reshape(shape)
        off += n
    return out


def _params(sem):
    return pltpu.CompilerParams(dimension_semantics=sem, vmem_limit_bytes=VMEM_LIMIT)


def _rms(x, g):
    return x * lax.rsqrt(jnp.mean(x * x, axis=-1, keepdims=True) + RMS_EPS) * g


def _split3(v):
    p1 = v.astype(BF16).astype(F32)
    r = v - p1
    p2 = r.astype(BF16).astype(F32)
    p3 = (r - p2).astype(BF16).astype(F32)
    return p1, p2, p3


def _matmul(a, b, mode, out_dtypes, epi, name, extras=(), tm=1024, tn=1024, tk=1024, n=None, b_place=None,
            shard_out=False):
    (K, M) = a.shape if mode == "tn" else a.shape[::-1]
    if n is None:
        N = b.shape[0] if mode == "nt" else b.shape[1]
    else:
        N = n
    tm, tn, tk = min(tm, M), min(tn, N), min(tk, K)
    nk = K // tk
    if mode == "tn":
        a_spec = pl.BlockSpec((tk, tm), lambda i, j, k: (k, i))
    else:
        a_spec = pl.BlockSpec((tm, tk), lambda i, j, k: (i, k))
    if b_place is not None:
        b_spec = b_place(tk, tn)
    elif mode == "nt":
        b_spec = pl.BlockSpec((tn, tk), lambda i, j, k: (j, k))
    else:
        b_spec = pl.BlockSpec((tk, tn), lambda i, j, k: (k, j))
    if shard_out:
        per = N // N_CHIPS // tn
        o_spec = pl.BlockSpec((None, tm, tn), lambda i, j, k: (j // per, i, j % per))
        o_shape = (N_CHIPS, M, N // N_CHIPS)
    else:
        o_spec = pl.BlockSpec((tm, tn), lambda i, j, k: (i, j))
        o_shape = (M, N)
    dims = {"nn": (((1,), (0,)), ((), ())), "nt": (((1,), (1,)), ((), ())), "tn": (((0,), (0,)), ((), ()))}[mode]
    ex_specs = []
    for _, kind in extras:
        if kind == "row":
            ex_specs.append(pl.BlockSpec((1, tn), lambda i, j, k: (0, j)))
        else:
            ex_specs.append(pl.BlockSpec((tm, tn), lambda i, j, k: (i, j)))
    n_ex, n_out = len(extras), len(out_dtypes)

    def body(*refs):
        a_ref, b_ref = refs[0], refs[1]
        ex = refs[2:2 + n_ex]
        outs = refs[2 + n_ex:2 + n_ex + n_out]
        prod = lax.dot_general(a_ref[...], b_ref[...], dims, preferred_element_type=F32)

        def finish(acc):
            res = epi(acc, *[e[...] for e in ex])
            for o, r in zip(outs, res):
                o[...] = r.astype(o.dtype)

        if nk == 1:
            finish(prod)
        else:
            acc_ref = refs[-1]
            k = pl.program_id(2)

            @pl.when(k == 0)
            def _():
                acc_ref[...] = prod

            @pl.when(k > 0)
            def _():
                acc_ref[...] += prod

            @pl.when(k == nk - 1)
            def _():
                finish(acc_ref[...])

    return pl.pallas_call(
        body,
        name=name,
        grid=(M // tm, N // tn, nk),
        in_specs=[a_spec, b_spec] + ex_specs,
        out_specs=[o_spec for _ in out_dtypes],
        out_shape=[jax.ShapeDtypeStruct(o_shape, dt) for dt in out_dtypes],
        scratch_shapes=[pltpu.VMEM((tm, tn), F32)] if nk > 1 else [],
        compiler_params=_params(("parallel", "parallel", "arbitrary")),
    )(a, b, *[e for e, _ in extras])


def _place_cols_nn(width, layer=None):
    def place(tk, tn):
        per = width // tn
        if layer is None:
            return pl.BlockSpec((None, tk, tn), lambda i, j, k: (j // per, k, j % per))
        return pl.BlockSpec((None, None, tk, tn), lambda i, j, k: (j // per, layer, k, j % per))
    return place


def _place_cols_nt(width, layer=None):
    def place(tk, tn):
        per = width // tk
        if layer is None:
            return pl.BlockSpec((None, tn, tk), lambda i, j, k: (k // per, j, k % per))
        return pl.BlockSpec((None, None, tn, tk), lambda i, j, k: (k // per, layer, j, k % per))
    return place


def _place_rows_nn(height, layer):
    def place(tk, tn):
        per = height // tk
        return pl.BlockSpec((None, None, tk, tn), lambda i, j, k: (k // per, layer, k % per, j))
    return place


def _place_rows_nt(height, layer):
    def place(tk, tn):
        per = height // tn
        return pl.BlockSpec((None, None, tn, tk), lambda i, j, k: (j // per, layer, j % per, k))
    return place


def _epi_plain(acc):
    return (acc,)


def _epi_bias(acc, bias):
    return (acc + bias,)


def _epi_sqrelu(acc):
    r = jnp.maximum(acc, 0.0)
    return (r * r,)


def _epi_dsqrelu(acc, act):
    return (acc * (2.0 * jnp.sqrt(act.astype(F32))),)


def _resnorm_fwd(x, m, g_post, g_next, name, tr=512):
    T = x.shape[0]
    has_m = m is not None
    row = pl.BlockSpec((tr, D), lambda i: (i, 0))
    vec = pl.BlockSpec((1, D), lambda i: (0, 0))

    def body(*refs):
        if has_m:
            x_ref, m_ref, gp_ref, gn_ref, x1_ref, h_ref = refs
            x1 = x_ref[...] + _rms(m_ref[...], gp_ref[...])
            x1_ref[...] = x1
        else:
            x_ref, gn_ref, h_ref = refs
            x1 = x_ref[...]
        h_ref[...] = _rms(x1, gn_ref[...]).astype(BF16)

    if has_m:
        args, in_specs = (x, m, g_post, g_next), [row, row, vec, vec]
        out_specs = [row, row]
        out_shape = [jax.ShapeDtypeStruct((T, D), F32), jax.ShapeDtypeStruct((T, D), BF16)]
    else:
        args, in_specs = (x, g_next), [row, vec]
        out_specs = [row]
        out_shape = [jax.ShapeDtypeStruct((T, D), BF16)]
    return pl.pallas_call(
        body, name=name, grid=(T // tr,), in_specs=in_specs, out_specs=out_specs, out_shape=out_shape,
        compiler_params=_params(("parallel",)),
    )(*args)


def _resnorm_loss(x, m, g_post, target, name, tr=512):
    T = x.shape[0]
    row = pl.BlockSpec((tr, D), lambda i: (i, 0))
    vec = pl.BlockSpec((1, D), lambda i: (0, 0))

    def body(x_ref, m_ref, gp_ref, t_ref, dy_ref, loss_ref):
        diff = x_ref[...] + _rms(m_ref[...], gp_ref[...]) - t_ref[...]
        dy_ref[...] = diff * (1.0 / D)

        @pl.when(pl.program_id(0) == 0)
        def _():
            loss_ref[...] = jnp.zeros_like(loss_ref)

        per_token = jnp.mean(diff * diff, axis=-1, keepdims=True)
        loss_ref[...] += 0.5 * jnp.sum(per_token)

    return pl.pallas_call(
        body, name=name, grid=(T // tr,), in_specs=[row, row, vec, row],
        out_specs=[row, pl.BlockSpec((8, LANES), lambda i: (0, 0))],
        out_shape=[jax.ShapeDtypeStruct((T, D), F32), jax.ShapeDtypeStruct((8, LANES), F32)],
        compiler_params=_params(("arbitrary",)),
    )(x, m, g_post, target)


def _norm_bwd(x, g, dy, resid, out_dtype, name, tr=512):
    T = x.shape[0]
    has_r = resid is not None
    row = pl.BlockSpec((tr, D), lambda i: (i, 0))
    vec = pl.BlockSpec((1, D), lambda i: (0, 0))

    def body(*refs):
        if has_r:
            x_ref, g_ref, dy_ref, r_ref, dx_ref, dg_ref, cs_ref = refs
        else:
            x_ref, g_ref, dy_ref, dx_ref, dg_ref, cs_ref = refs
        xv, dyv = x_ref[...], dy_ref[...]
        r = lax.rsqrt(jnp.mean(xv * xv, axis=-1, keepdims=True) + RMS_EPS)
        gy = dyv * g_ref[...]
        c = jnp.sum(gy * xv, axis=-1, keepdims=True) * (1.0 / D)
        dx = r * gy - xv * (r * r * r * c)

        @pl.when(pl.program_id(0) == 0)
        def _():
            dg_ref[...] = jnp.zeros_like(dg_ref)
            cs_ref[...] = jnp.zeros_like(cs_ref)

        dg_ref[...] += jnp.sum(dyv * (xv * r), axis=0, keepdims=True)
        cs_ref[...] += jnp.sum(dx, axis=0, keepdims=True)
        if has_r:
            dx = dx + r_ref[...]
        dx_ref[...] = dx.astype(dx_ref.dtype)

    args = (x, g, dy) + ((resid,) if has_r else ())
    return pl.pallas_call(
        body, name=name, grid=(T // tr,), in_specs=[row, vec, row] + ([row] if has_r else []),
        out_specs=[row, vec, vec],
        out_shape=[jax.ShapeDtypeStruct((T, D), out_dtype), jax.ShapeDtypeStruct((1, D), F32),
                   jax.ShapeDtypeStruct((1, D), F32)],
        compiler_params=_params(("arbitrary",)),
    )(*args)


CONV_TB = 256
CONV_RC = 32


def _glu(a):
    return a[:, :D] * jax.nn.sigmoid(a[:, D:])


def _shifted_copies(win):
    rows = win.shape[1]
    for p in range(1, SUBLANES):
        win[p, 0:rows - SUBLANES, :] = win[0, p:p + rows - SUBLANES, :]


def _window(win, start):
    p = start % SUBLANES
    return win[p, start - p:start - p + CONV_RC, :]


def _conv_fwd(a, dw_w, dw_b, ln_g, ln_b, name):
    T = a.shape[0]
    tb = min(CONV_TB, T)
    per = tb // HALO
    vec = pl.BlockSpec((1, D), lambda i: (0, 0))

    def body(cur_ref, prev_ref, w_ref, b_ref, lg_ref, lb_ref, y_ref, z_ref, uwin):
        i = pl.program_id(0)
        uwin[0, 0:HALO, :] = jnp.where(i > 0, _glu(prev_ref[...]), 0.0)
        uwin[0, HALO:, :] = _glu(cur_ref[...])
        _shifted_copies(uwin)
        for r in range(tb // CONV_RC):
            acc = jnp.broadcast_to(b_ref[...], (CONV_RC, D))
            for k in range(CONV_W):
                start = r * CONV_RC + HALO - (CONV_W - 1) + k
                acc = acc + _window(uwin, start) * w_ref[k:k + 1, :]
            rows = slice(r * CONV_RC, (r + 1) * CONV_RC)
            y_ref[rows, :] = acc
            mu = jnp.mean(acc, axis=-1, keepdims=True)
            xc = acc - mu
            var = jnp.mean(xc * xc, axis=-1, keepdims=True)
            n = xc * lax.rsqrt(var + LN_EPS) * lg_ref[...] + lb_ref[...]
            z_ref[rows, :] = (n * jax.nn.sigmoid(n)).astype(BF16)

    return pl.pallas_call(
        body, name=name, grid=(T // tb,),
        in_specs=[pl.BlockSpec((tb, 2 * D), lambda i: (i, 0)),
                  pl.BlockSpec((HALO, 2 * D), lambda i: (jnp.maximum(i * per - 1, 0), 0)),
                  pl.BlockSpec((HALO, D), lambda i: (0, 0)), vec, vec, vec],
        out_specs=[pl.BlockSpec((tb, D), lambda i: (i, 0)), pl.BlockSpec((tb, D), lambda i: (i, 0))],
        out_shape=[jax.ShapeDtypeStruct((T, D), F32), jax.ShapeDtypeStruct((T, D), BF16)],
        scratch_shapes=[pltpu.VMEM((SUBLANES, tb + HALO, D), F32)],
        compiler_params=_params(("parallel",)),
    )(a, a, dw_w, dw_b, ln_g, ln_b)


def _conv_bwd_ln(dz, y, ln_g, ln_b, name, tr=256):
    T = y.shape[0]
    tr = min(tr, T)
    row = pl.BlockSpec((tr, D), lambda i: (i, 0))
    vec = pl.BlockSpec((1, D), lambda i: (0, 0))

    def body(dz_ref, y_ref, lg_ref, lb_ref, dy_ref, dlg_ref, dlb_ref, db_ref):
        yv = y_ref[...]
        mu = jnp.mean(yv, axis=-1, keepdims=True)
        xc = yv - mu
        rstd = lax.rsqrt(jnp.mean(xc * xc, axis=-1, keepdims=True) + LN_EPS)
        yh = xc * rstd
        n = yh * lg_ref[...] + lb_ref[...]
        sg = jax.nn.sigmoid(n)
        dn = dz_ref[...] * (sg * (1.0 + n * (1.0 - sg)))
        dyh = dn * lg_ref[...]
        dyv = rstd * (dyh - jnp.mean(dyh, axis=-1, keepdims=True) - yh * jnp.mean(dyh * yh, axis=-1, keepdims=True))
        dy_ref[...] = dyv

        @pl.when(pl.program_id(0) == 0)
        def _():
            dlg_ref[...] = jnp.zeros_like(dlg_ref)
            dlb_ref[...] = jnp.zeros_like(dlb_ref)
            db_ref[...] = jnp.zeros_like(db_ref)

        dlg_ref[...] += jnp.sum(dn * yh, axis=0, keepdims=True)
        dlb_ref[...] += jnp.sum(dn, axis=0, keepdims=True)
        db_ref[...] += jnp.sum(dyv, axis=0, keepdims=True)

    return pl.pallas_call(
        body, name=name, grid=(T // tr,), in_specs=[row, row, vec, vec], out_specs=[row, vec, vec, vec],
        out_shape=[jax.ShapeDtypeStruct((T, D), F32)] + [jax.ShapeDtypeStruct((1, D), F32)] * 3,
        compiler_params=_params(("arbitrary",)),
    )(dz, y, ln_g, ln_b)


def _conv_bwd_dw(dy, a, dw_w, name):
    T = a.shape[0]
    tb = min(CONV_TB, T)
    per = tb // HALO
    last_halo = T // HALO - 1
    n_steps = T // tb

    def body(dyc_ref, dyn_ref, cur_ref, prev_ref, w_ref, da_ref, dbias_ref, dw_ref, uwin, dywin, dwacc):
        i = pl.program_id(0)

        @pl.when(i == 0)
        def _():
            dbias_ref[...] = jnp.zeros_like(dbias_ref)
            dwacc[...] = jnp.zeros_like(dwacc)

        uwin[0, 0:HALO, :] = jnp.where(i > 0, _glu(prev_ref[...]), 0.0)
        uwin[0, HALO:, :] = _glu(cur_ref[...])
        dywin[0, 0:tb, :] = dyc_ref[...]
        dywin[0, tb:, :] = jnp.where(i < n_steps - 1, dyn_ref[...], 0.0)
        _shifted_copies(uwin)
        _shifted_copies(dywin)
        for r in range(tb // CONV_RC):
            rows = slice(r * CONV_RC, (r + 1) * CONV_RC)
            dy_c = dywin[0, rows, :]
            du = jnp.zeros((CONV_RC, D), F32)
            for k in range(CONV_W):
                du = du + _window(dywin, r * CONV_RC + (CONV_W - 1) - k) * w_ref[k:k + 1, :]
                prod = dy_c * _window(uwin, r * CONV_RC + HALO - (CONV_W - 1) + k)
                part = prod[0:8, :]
                for q in range(1, CONV_RC // 8):
                    part = part + prod[8 * q:8 * q + 8, :]
                dwacc[8 * k:8 * k + 8, :] += part
            av = cur_ref[rows, :]
            a1, sg = av[:, :D], jax.nn.sigmoid(av[:, D:])
            da1 = du * sg
            da2 = du * a1 * (sg * (1.0 - sg))
            da_ref[rows, 0:D] = da1.astype(BF16)
            da_ref[rows, D:] = da2.astype(BF16)
            dbias_ref[:, 0:D] += jnp.sum(da1, axis=0, keepdims=True)
            dbias_ref[:, D:] += jnp.sum(da2, axis=0, keepdims=True)

        @pl.when(i == n_steps - 1)
        def _():
            for k in range(CONV_W):
                dw_ref[k:k + 1, :] = jnp.sum(dwacc[8 * k:8 * k + 8, :], axis=0, keepdims=True)
            dw_ref[CONV_W:, :] = jnp.zeros((HALO - CONV_W, D), F32)

    return pl.pallas_call(
        body, name=name, grid=(n_steps,),
        in_specs=[pl.BlockSpec((tb, D), lambda i: (i, 0)),
                  pl.BlockSpec((HALO, D), lambda i: (jnp.minimum((i + 1) * per, last_halo), 0)),
                  pl.BlockSpec((tb, 2 * D), lambda i: (i, 0)),
                  pl.BlockSpec((HALO, 2 * D), lambda i: (jnp.maximum(i * per - 1, 0), 0)),
                  pl.BlockSpec((HALO, D), lambda i: (0, 0))],
        out_specs=[pl.BlockSpec((tb, 2 * D), lambda i: (i, 0)), pl.BlockSpec((1, 2 * D), lambda i: (0, 0)),
                   pl.BlockSpec((HALO, D), lambda i: (0, 0))],
        out_shape=[jax.ShapeDtypeStruct((T, 2 * D), BF16), jax.ShapeDtypeStruct((1, 2 * D), F32),
                   jax.ShapeDtypeStruct((HALO, D), F32)],
        scratch_shapes=[pltpu.VMEM((SUBLANES, tb + HALO, D), F32), pltpu.VMEM((SUBLANES, tb + HALO, D), F32),
                        pltpu.VMEM((8 * HALO, D), F32)],
        compiler_params=_params(("arbitrary",)),
    )(dy, dy, a, a, dw_w)


GATE_TB = 512


def _gate_fwd(fl, b_f, name):
    T = fl.shape[0]
    tb = min(GATE_TB, T)
    row = pl.BlockSpec((tb, LANES), lambda i: (i, 0))

    def body(fl_ref, b_ref, f_ref, f1_ref, f2_ref, f3_ref, carry):
        @pl.when(pl.program_id(0) == 0)
        def _():
            carry[...] = jnp.zeros_like(carry)

        z = fl_ref[...] + b_ref[...]
        lf = jnp.minimum(z, 0.0) - jnp.log(1.0 + jnp.exp(-jnp.abs(z)))
        tri = (lax.broadcasted_iota(jnp.int32, (tb, tb), 0) >= lax.broadcasted_iota(jnp.int32, (tb, tb), 1)).astype(F32)
        f = jnp.dot(tri, lf, precision=lax.Precision.HIGHEST, preferred_element_type=F32) + carry[...]
        carry[...] = f[tb - 1:tb, :]
        f_ref[...] = f
        f1_ref[...], f2_ref[...], f3_ref[...] = _split3(f)

    return pl.pallas_call(
        body, name=name, grid=(T // tb,), in_specs=[row, pl.BlockSpec((1, LANES), lambda i: (0, 0))],
        out_specs=[row] * 4, out_shape=[jax.ShapeDtypeStruct((T, LANES), F32)] * 4,
        scratch_shapes=[pltpu.VMEM((1, LANES), F32)],
        compiler_params=_params(("arbitrary",)),
    )(fl, b_f)


def _gate_bwd(dF_q, dF_k, fl, b_f, name):
    T = fl.shape[0]
    tb = min(GATE_TB, T)
    nb = T // tb
    row = pl.BlockSpec((tb, LANES), lambda i: (nb - 1 - i, 0))
    vec = pl.BlockSpec((1, LANES), lambda i: (0, 0))

    def body(dfq_ref, dfk_ref, fl_ref, b_ref, dfl_ref, db_ref, carry):
        @pl.when(pl.program_id(0) == 0)
        def _():
            carry[...] = jnp.zeros_like(carry)
            db_ref[...] = jnp.zeros_like(db_ref)

        tri = (lax.broadcasted_iota(jnp.int32, (tb, tb), 0) <= lax.broadcasted_iota(jnp.int32, (tb, tb), 1)).astype(F32)
        rc = jnp.dot(tri, dfq_ref[...] - dfk_ref[...], precision=lax.Precision.HIGHEST,
                     preferred_element_type=F32) + carry[...]
        carry[...] = rc[0:1, :]
        dfl = rc * jax.nn.sigmoid(-(fl_ref[...] + b_ref[...]))
        dfl_ref[...] = dfl
        db_ref[...] += jnp.sum(dfl, axis=0, keepdims=True)

    return pl.pallas_call(
        body, name=name, grid=(nb,), in_specs=[row, row, row, vec], out_specs=[row, vec],
        out_shape=[jax.ShapeDtypeStruct((T, LANES), F32), jax.ShapeDtypeStruct((1, LANES), F32)],
        scratch_shapes=[pltpu.VMEM((1, LANES), F32)],
        compiler_params=_params(("arbitrary",)),
    )(dF_q, dF_k, fl, b_f)


ATT_TB = 512
LAYOUT_TB = 2048
NT = (((1,), (1,)), ((), ()))
NN = (((1,), (0,)), ((), ()))


def _lane_insert(base, parts, first_lane):
    lane = lax.broadcasted_iota(jnp.int32, base.shape, 1)
    out = base
    for n, p in enumerate(parts):
        out = jnp.where(lane == first_lane + n, p, out)
    return out


def _causal(tb):
    return lax.broadcasted_iota(jnp.int32, (tb, tb), 0) >= lax.broadcasted_iota(jnp.int32, (tb, tb), 1)


def _other_head(x):
    return pltpu.roll(x, HEAD_DIM, 1)


def _attn_pack(proj, f1, f2, f3, name):
    T = proj.shape[0]
    tb = min(LAYOUT_TB, T)
    pairs = N_HEADS // 2
    cols = lambda first: pl.BlockSpec((tb, LANES), lambda p, i: (i, first + p))
    gate = pl.BlockSpec((tb, LANES), lambda p, i: (i, 0))
    heads = pl.BlockSpec((2, tb, LANES), lambda p, i: (p, i, 0))
    norm = pl.BlockSpec((2, 8, LANES), lambda p, i: (p, 0, 0))

    def body(q_ref, k_ref, v_ref, f1_ref, f2_ref, f3_ref, qa_ref, ka_ref, va_ref, qn_ref, kn_ref):
        p, i = pl.program_id(0), pl.program_id(1)
        lane = lax.broadcasted_iota(jnp.int32, (tb, LANES), 1)
        data = lane < HEAD_DIM

        @pl.when(i == 0)
        def _():
            qn_ref[...] = jnp.zeros_like(qn_ref)
            kn_ref[...] = jnp.zeros_like(kn_ref)

        qv, kv, vv = q_ref[...] * 0.125, k_ref[...], v_ref[...]
        for e in range(2):
            fcol = [jnp.sum(jnp.where(lane == 2 * p + e, f[...], 0.0), axis=-1, keepdims=True)
                    for f in (f1_ref, f2_ref, f3_ref)]
            pick = (lambda t: t) if e == 0 else _other_head
            qd = jnp.where(data, pick(qv), 0.0).astype(BF16).astype(F32)
            kd = jnp.where(data, pick(kv), 0.0).astype(BF16).astype(F32)
            vd = jnp.where(data, pick(vv), 0.0)
            qa_ref[e] = _lane_insert(qd, fcol + [1.0, 1.0, 1.0], AUG_F).astype(BF16)
            ka_ref[e] = _lane_insert(kd, [1.0, 1.0, 1.0] + [-f for f in fcol] + [1.0, 1.0, 1.0], AUG_F).astype(BF16)
            va_ref[e] = _lane_insert(vd, [-1.0, -1.0, -1.0, 1.0], AUG_F).astype(BF16)
            qn_ref[e] = jnp.maximum(qn_ref[e], jnp.max(jnp.sum(qd * qd, axis=-1, keepdims=True)))
            kn_ref[e] = jnp.maximum(kn_ref[e], jnp.max(jnp.sum(kd * kd, axis=-1, keepdims=True)))

    aug = jax.ShapeDtypeStruct((N_HEADS, T, LANES), BF16)
    nrm = jax.ShapeDtypeStruct((N_HEADS, 8, LANES), F32)
    return pl.pallas_call(
        body, name=name, grid=(pairs, T // tb),
        in_specs=[cols(0), cols(pairs), cols(2 * pairs), gate, gate, gate],
        out_specs=[heads, heads, heads, norm, norm], out_shape=[aug, aug, aug, nrm, nrm],
        compiler_params=_params(("parallel", "arbitrary")),
    )(proj, proj, proj, f1, f2, f3)


def _merge_heads(x_aug, scale, out_dtype, name, column=None):
    H, T, _ = x_aug.shape
    tb = min(LAYOUT_TB, T)

    def body(x_ref, o_ref, *col_ref):
        p = pl.program_id(1)
        lane = lax.broadcasted_iota(jnp.int32, (tb, LANES), 1)
        x0, x1 = x_ref[0], x_ref[1]
        o_ref[...] = (jnp.where(lane < HEAD_DIM, x0, _other_head(x1)) * scale).astype(out_dtype)
        if column is not None:
            @pl.when(p == 0)
            def _():
                col_ref[0][...] = jnp.zeros_like(col_ref[0])

            c0 = jnp.sum(jnp.where(lane == column, x0, 0.0), axis=-1, keepdims=True)
            c1 = jnp.sum(jnp.where(lane == column, x1, 0.0), axis=-1, keepdims=True)
            col_ref[0][...] += jnp.where(lane == 2 * p, c0, 0.0) + jnp.where(lane == 2 * p + 1, c1, 0.0)

    out_specs = [pl.BlockSpec((tb, LANES), lambda i, p: (i, p))]
    out_shape = [jax.ShapeDtypeStruct((T, D), out_dtype)]
    if column is not None:
        out_specs.append(pl.BlockSpec((tb, LANES), lambda i, p: (i, 0)))
        out_shape.append(jax.ShapeDtypeStruct((T, LANES), F32))
    return pl.pallas_call(
        body, name=name, grid=(T // tb, H // 2),
        in_specs=[pl.BlockSpec((2, tb, LANES), lambda i, p: (p, i, 0))],
        out_specs=out_specs, out_shape=out_shape,
        compiler_params=_params(("parallel", "arbitrary")),
    )(x_aug)


def _attn_prep(d_o, o_aug, name):
    H, T, _ = o_aug.shape
    tb = min(LAYOUT_TB, T)
    heads = pl.BlockSpec((2, tb, LANES), lambda p, i: (p, i, 0))

    def body(do_ref, o_ref, out_ref):
        lane = lax.broadcasted_iota(jnp.int32, (tb, LANES), 1)
        dov = do_ref[...]
        for e in range(2):
            d_e = jnp.where(lane < HEAD_DIM, dov if e == 0 else _other_head(dov), 0.0)
            delta = jnp.sum(d_e * o_ref[e], axis=-1, keepdims=True)
            out_ref[e] = _lane_insert(d_e, _split3(delta), AUG_F).astype(BF16)

    return pl.pallas_call(
        body, name=name, grid=(H // 2, T // tb),
        in_specs=[pl.BlockSpec((tb, LANES), lambda p, i: (i, p)), heads], out_specs=heads,
        out_shape=jax.ShapeDtypeStruct((H, T, LANES), BF16),
        compiler_params=_params(("parallel", "parallel")),
    )(d_o, o_aug)


def _flash_spec(n_in, n_out, H, nb, tb, T, resident, scratch):
    blk = pl.BlockSpec((2, tb, LANES), lambda h, i, *_: (h, i, 0))
    whole = pl.BlockSpec((2, T, LANES), lambda h, i, *_: (h, 0, 0))
    return pltpu.PrefetchScalarGridSpec(
        num_scalar_prefetch=3, grid=(H // 2, nb),
        in_specs=[whole if resident[n] else blk for n in range(n_in)],
        out_specs=[blk] * n_out, scratch_shapes=scratch)


def _first_live(qk_ref, fs_ref, fe_ref, h, i):
    top = qk_ref[h] + fs_ref[h, i]
    return lax.fori_loop(0, i, lambda j, n: n + jnp.where(top - fe_ref[h, j] < SKIP_BELOW, 1, 0), 0)


def _flash_fwd(qk, fs, fe, q_aug, k_aug, v_aug, name):
    H, T, _ = q_aug.shape
    tb = min(ATT_TB, T)

    def body(qk_ref, fs_ref, fe_ref, q_ref, k_ref, v_ref, o_ref, q2_ref, m_s, acc_s):
        h, i = 2 * pl.program_id(0), pl.program_id(1)
        m_s[...] = jnp.full(m_s.shape, -jnp.inf, F32)
        acc_s[...] = jnp.zeros_like(acc_s)

        def step(kb, masked):
            rows = pl.ds(pl.multiple_of(kb * tb, tb), tb)
            for e in range(2):
                s = lax.dot_general(q_ref[e], k_ref[e, rows, :], NT, preferred_element_type=F32)
                if masked:
                    s = jnp.where(_causal(tb), s, MASK_VALUE)
                m_old = m_s[e]
                m_new = jnp.maximum(m_old, jnp.max(s, axis=-1, keepdims=True))
                p = jnp.exp(s - m_new)
                acc_s[e] = jnp.exp(m_old - m_new) * acc_s[e] + lax.dot_general(
                    p.astype(BF16), v_ref[e, rows, :], NN, preferred_element_type=F32)
                m_s[e] = m_new

        def loop_body(kb, carry):
            step(kb, False)
            return carry

        first = jnp.minimum(_first_live(qk_ref, fs_ref, fe_ref, h, i), _first_live(qk_ref, fs_ref, fe_ref, h + 1, i))
        lax.fori_loop(first, i, loop_body, 0)
        step(i, True)
        for e in range(2):
            acc = acc_s[e]
            lane = lax.broadcasted_iota(jnp.int32, acc.shape, 1)
            l = jnp.sum(jnp.where(lane == AUG_ONE, acc, 0.0), axis=-1, keepdims=True)
            o_ref[e] = acc / l
            lse = m_s[e] + jnp.log(l)
            q2_ref[e] = _lane_insert(q_ref[e].astype(F32), [-p for p in _split3(lse)], AUG_L).astype(BF16)

    return pl.pallas_call(
        body, name=name,
        grid_spec=_flash_spec(3, 2, H, T // tb, tb, T, (False, True, True),
                              [pltpu.VMEM((2, tb, 1), F32), pltpu.VMEM((2, tb, LANES), F32)]),
        out_shape=[jax.ShapeDtypeStruct((H, T, LANES), F32), jax.ShapeDtypeStruct((H, T, LANES), BF16)],
        compiler_params=_params(("parallel", "arbitrary")),
    )(qk, fs, fe, q_aug, k_aug, v_aug)


def _flash_dq(qk, fs, fe, q2, k_aug, do_aug, v_aug, name):
    H, T, _ = q2.shape
    tb = min(ATT_TB, T)

    def body(qk_ref, fs_ref, fe_ref, q_ref, k_ref, do_ref, v_ref, dq_ref, acc_s):
        h, i = 2 * pl.program_id(0), pl.program_id(1)
        acc_s[...] = jnp.zeros_like(acc_s)

        def step(kb, masked):
            rows = pl.ds(pl.multiple_of(kb * tb, tb), tb)
            for e in range(2):
                k = k_ref[e, rows, :]
                p = jnp.exp(lax.dot_general(q_ref[e], k, NT, preferred_element_type=F32))
                if masked:
                    p = jnp.where(_causal(tb), p, 0.0)
                ds = p * lax.dot_general(do_ref[e], v_ref[e, rows, :], NT, preferred_element_type=F32)
                acc_s[e] += lax.dot_general(ds.astype(BF16), k, NN, preferred_element_type=F32)

        def loop_body(kb, carry):
            step(kb, False)
            return carry

        first = jnp.minimum(_first_live(qk_ref, fs_ref, fe_ref, h, i), _first_live(qk_ref, fs_ref, fe_ref, h + 1, i))
        lax.fori_loop(first, i, loop_body, 0)
        step(i, True)
        dq_ref[...] = acc_s[...]

    return pl.pallas_call(
        body, name=name,
        grid_spec=_flash_spec(4, 1, H, T // tb, tb, T, (False, True, False, True), [pltpu.VMEM((2, tb, LANES), F32)]),
        out_shape=[jax.ShapeDtypeStruct((H, T, LANES), F32)],
        compiler_params=_params(("parallel", "arbitrary")),
    )(qk, fs, fe, q2, k_aug, do_aug, v_aug)[0]


def _flash_dkv(qk, fs, fe, q2, k_aug, do_aug, v_aug, name):
    H, T, _ = q2.shape
    tb = min(ATT_TB, T)
    nb = T // tb

    def body(qk_ref, fs_ref, fe_ref, q_ref, k_ref, do_ref, v_ref, dk_ref, dv_ref, dk_s, dv_s):
        h, i = 2 * pl.program_id(0), pl.program_id(1)
        dk_s[...] = jnp.zeros_like(dk_s)
        dv_s[...] = jnp.zeros_like(dv_s)

        def step(qb, masked):
            rows = pl.ds(pl.multiple_of(qb * tb, tb), tb)
            for e in range(2):
                q, dov = q_ref[e, rows, :], do_ref[e, rows, :]
                pt = jnp.exp(lax.dot_general(k_ref[e], q, NT, preferred_element_type=F32))
                if masked:
                    pt = jnp.where(
                        lax.broadcasted_iota(jnp.int32, (tb, tb), 1) >= lax.broadcasted_iota(jnp.int32, (tb, tb), 0),
                        pt, 0.0)
                dst = pt * lax.dot_general(v_ref[e], dov, NT, preferred_element_type=F32)
                dv_s[e] += lax.dot_general(pt.astype(BF16), dov, NN, preferred_element_type=F32)
                dk_s[e] += lax.dot_general(dst.astype(BF16), q, NN, preferred_element_type=F32)

        step(i, True)

        def loop_body(qb, carry):
            step(qb, False)
            return carry

        def live_after(head):
            base = qk_ref[head] - fe_ref[head, i]
            return lax.fori_loop(i + 1, nb, lambda b, n: n + jnp.where(base + fs_ref[head, b] < SKIP_BELOW, 0, 1), 0)

        lax.fori_loop(i + 1, i + 1 + jnp.maximum(live_after(h), live_after(h + 1)), loop_body, 0)
        dk_ref[...] = dk_s[...]
        dv_ref[...] = dv_s[...]

    return pl.pallas_call(
        body, name=name,
        grid_spec=_flash_spec(4, 2, H, nb, tb, T, (True, False, True, False),
                              [pltpu.VMEM((2, tb, LANES), F32), pltpu.VMEM((2, tb, LANES), F32)]),
        out_shape=[jax.ShapeDtypeStruct((H, T, LANES), F32)] * 2,
        compiler_params=_params(("parallel", "arbitrary")),
    )(qk, fs, fe, q2, k_aug, do_aug, v_aug)


def _add_pair(place, slab, got, name):
    _, half, cols = got.shape
    grid_spec = pltpu.PrefetchScalarGridSpec(
        num_scalar_prefetch=1, grid=(N_CHIPS,),
        in_specs=[pl.BlockSpec((None, half, cols), lambda j, s: (j, s[4], 0)),
                  pl.BlockSpec((None, half, cols), lambda j, s: (j, 0, 0))],
        out_specs=pl.BlockSpec((None, half, cols), lambda j, s: (j, 0, 0)))

    def body(s_ref, a_ref, b_ref, o_ref):
        o_ref[...] = (a_ref[...] + b_ref[...]).astype(BF16)

    return pl.pallas_call(
        body, name=name, grid_spec=grid_spec, out_shape=jax.ShapeDtypeStruct(got.shape, BF16),
        compiler_params=_params(("parallel",)),
    )(place, slab, got)


def _add_chips(place, pair, by_chip, name):
    _, half, cols = pair.shape
    tb = min(256, half)
    steps = half // tb
    slot = lambda n: pl.BlockSpec((None, tb, cols), lambda i, s: (s[n], i, 0))
    grid_spec = pltpu.PrefetchScalarGridSpec(
        num_scalar_prefetch=1, grid=(steps,), in_specs=[slot(0), slot(1), slot(2), slot(3)],
        out_specs=pl.BlockSpec((tb, cols), lambda i, s: (s[4] * steps + i, 0)))

    def body(s_ref, own_ref, b1_ref, b2_ref, b3_ref, o_ref):
        o_ref[...] = ((own_ref[...].astype(F32) + b1_ref[...].astype(F32)) + b2_ref[...].astype(F32)) \
            + b3_ref[...].astype(F32)

    return pl.pallas_call(
        body, name=name, grid_spec=grid_spec, out_shape=jax.ShapeDtypeStruct((2 * half, cols), F32),
        compiler_params=_params(("parallel",)),
    )(place, pair, by_chip, by_chip, by_chip)


def _add_slots(b, name, tb=128):
    n, rows, cols = b.shape
    tb = tb if rows % tb == 0 else rows
    specs = [pl.BlockSpec((None, tb, cols), functools.partial(lambda j, i: (j, i, 0), j)) for j in range(n)]

    def body(*refs):
        acc = refs[0][...]
        for r in refs[1:n]:
            acc = acc + r[...]
        refs[n][...] = acc

    return pl.pallas_call(
        body, name=name, grid=(rows // tb,), in_specs=specs, out_specs=pl.BlockSpec((tb, cols), lambda i: (i, 0)),
        out_shape=jax.ShapeDtypeStruct((rows, cols), F32), compiler_params=_params(("parallel",)),
    )(*([b] * n))


def _adamw(w, g, m, v, name):
    rows, cols = w.shape
    tr = 256 if rows % 256 == 0 else rows
    blk = pl.BlockSpec((tr, cols), lambda i: (i, 0))

    def body(w_ref, g_ref, m_ref, v_ref, d_ref, nm_ref, nv_ref):
        gv = g_ref[...]
        nm = ADAM_B1 * m_ref[...] + (1.0 - ADAM_B1) * gv
        nv = ADAM_B2 * v_ref[...] + (1.0 - ADAM_B2) * (gv * gv)
        m_hat = nm / (1.0 - ADAM_B1 ** ADAM_STEP)
        v_hat = nv / (1.0 - ADAM_B2 ** ADAM_STEP)
        d_ref[...] = -ADAM_LR * (m_hat / (jnp.sqrt(v_hat) + ADAM_EPS) + ADAM_WD * w_ref[...])
        nm_ref[...] = nm
        nv_ref[...] = nv

    return pl.pallas_call(
        body, name=name, grid=(rows // tr,), in_specs=[blk] * 4, out_specs=[blk] * 3,
        out_shape=[jax.ShapeDtypeStruct((rows, cols), F32)] * 3, compiler_params=_params(("parallel",)),
    )(w, g, m, v)


ANY = pl.BlockSpec(memory_space=pl.ANY)


def _place():
    x, y, c = lax.axis_index("x"), lax.axis_index("y"), lax.axis_index("c")
    others = [(1 - x, y), (x, 1 - y), (1 - x, 1 - y)]
    return x, y, c, 2 * x + y, others


def _half(ref, shape, h):
    if len(shape) == 3:
        return ref.at[h]
    return ref.at[pl.ds(h * (shape[0] // 2), shape[0] // 2), :]


def _allgather_weights(shards):
    n_u = len(GATHER_UNITS)

    def body(*refs):
        ins, outs = refs[:n_u], refs[n_u:2 * n_u]
        send_sems, recv_sems, own_send_sems, own_recv_sems = refs[2 * n_u:]
        x, y, c, me, others = _place()
        sibling = (x, y, 1 - c)

        def copy(u, k, chip, h, to, src=None):
            dst = _half(outs[u].at[chip], GATHER_UNITS[u][1], h)
            return pltpu.make_async_remote_copy(
                src_ref=dst if src is None else src, dst_ref=dst, send_sem=send_sems.at[6 * u + k],
                recv_sem=recv_sems.at[6 * u + k], device_id=to, device_id_type=MESH)

        mine = [pltpu.make_async_remote_copy(
            src_ref=ins[u], dst_ref=outs[u].at[me], send_sem=own_send_sems.at[u], recv_sem=own_recv_sems.at[u],
            device_id=sibling, device_id_type=MESH) for u in range(n_u)]
        for cp in mine:
            cp.start()
        first = [copy(u, j, me, c, (ox, oy, c), src=_half(ins[u], GATHER_UNITS[u][1], c))
                 for u in range(n_u) for j, (ox, oy) in enumerate(others)]
        for cp in first:
            cp.start()
        passed = []
        for u in range(n_u):
            for j, (ox, oy) in enumerate(others):
                copy(u, j, 2 * ox + oy, c, (x, y, c)).wait_recv()
                passed.append(copy(u, 3 + j, 2 * ox + oy, c, sibling))
                passed[-1].start()
        for u in range(n_u):
            for j, (ox, oy) in enumerate(others):
                copy(u, 3 + j, 2 * ox + oy, 1 - c, (x, y, c)).wait_recv()
        for cp in first + passed:
            cp.wait_send()
        for cp in mine:
            cp.wait()

    return pl.pallas_call(
        body, name="allgather_weights", in_specs=[ANY] * n_u, out_specs=[ANY] * n_u,
        out_shape=[jax.ShapeDtypeStruct((N_CHIPS,) + shape, BF16) for _, shape in GATHER_UNITS],
        scratch_shapes=[pltpu.SemaphoreType.DMA((6 * n_u,)), pltpu.SemaphoreType.DMA((6 * n_u,)),
                        pltpu.SemaphoreType.DMA((n_u,)), pltpu.SemaphoreType.DMA((n_u,))],
    )(*shards)


def _sibling_exchange(slabs):
    n_u = len(GRAD_UNITS)

    def body(*refs):
        ins, got = refs[:n_u], refs[n_u:2 * n_u]
        send_sems, recv_sems = refs[2 * n_u:]
        x, y, c, _, _ = _place()
        copies = []
        for u, (_, (rows, _), _) in enumerate(GRAD_UNITS):
            half = rows // 2
            copies.append(pltpu.make_async_remote_copy(
                src_ref=ins[u].at[:, pl.ds((1 - c) * half, half), :], dst_ref=got[u], send_sem=send_sems.at[u],
                recv_sem=recv_sems.at[u], device_id=(x, y, 1 - c), device_id_type=MESH))
        for cp in copies:
            cp.start()
        for cp in copies:
            cp.wait()

    return pl.pallas_call(
        body, name="grad_sibling_exchange", in_specs=[ANY] * n_u, out_specs=[ANY] * n_u,
        out_shape=[jax.ShapeDtypeStruct((N_CHIPS, rows // 2, cols), F32) for _, (rows, cols), _ in GRAD_UNITS],
        scratch_shapes=[pltpu.SemaphoreType.DMA((n_u,)), pltpu.SemaphoreType.DMA((n_u,))],
    )(*slabs)


def _chip_exchange(pairs, small):
    n_u = len(GRAD_UNITS)

    def body(*refs):
        ins, s_ref = refs[:n_u], refs[n_u]
        outs, sall_ref = refs[n_u + 1:2 * n_u + 1], refs[2 * n_u + 1]
        send_sems, recv_sems, ssend_sems, srecv_sems, local_sem = refs[2 * n_u + 2:]
        x, y, c, me, others = _place()
        dev = 4 * x + 2 * y + c
        local = [pltpu.make_async_copy(s_ref, sall_ref.at[dev], local_sem)]
        sends = [pltpu.make_async_remote_copy(
            src_ref=ins[u].at[2 * ox + oy], dst_ref=outs[u].at[me], send_sem=send_sems.at[3 * u + j],
            recv_sem=recv_sems.at[3 * u + j], device_id=(ox, oy, c), device_id_type=MESH)
            for u in range(n_u) for j, (ox, oy) in enumerate(others)]
        flips = [(fx, fy, fc) for fx in (0, 1) for fy in (0, 1) for fc in (0, 1)][1:]
        sends += [pltpu.make_async_remote_copy(
            src_ref=s_ref, dst_ref=sall_ref.at[dev], send_sem=ssend_sems.at[n], recv_sem=srecv_sems.at[n],
            device_id=(x ^ fx, y ^ fy, c ^ fc), device_id_type=MESH) for n, (fx, fy, fc) in enumerate(flips)]
        for cp in local + sends:
            cp.start()
        for u in range(n_u):
            for j, (ox, oy) in enumerate(others):
                pltpu.make_async_remote_copy(
                    src_ref=ins[u].at[me], dst_ref=outs[u].at[2 * ox + oy], send_sem=send_sems.at[3 * u + j],
                    recv_sem=recv_sems.at[3 * u + j], device_id=(x, y, c), device_id_type=MESH).wait_recv()
        for n, (fx, fy, fc) in enumerate(flips):
            src_dev = 4 * (x ^ fx) + 2 * (y ^ fy) + (c ^ fc)
            pltpu.make_async_remote_copy(
                src_ref=s_ref, dst_ref=sall_ref.at[src_dev], send_sem=ssend_sems.at[n], recv_sem=srecv_sems.at[n],
                device_id=(x, y, c), device_id_type=MESH).wait_recv()
        for cp in sends:
            cp.wait_send()
        for cp in local:
            cp.wait()

    outs = pl.pallas_call(
        body, name="grad_chip_exchange", in_specs=[ANY] * (n_u + 1), out_specs=[ANY] * (n_u + 1),
        out_shape=[jax.ShapeDtypeStruct(p.shape, p.dtype) for p in pairs]
        + [jax.ShapeDtypeStruct((8, SMALL_ALL_ROWS, LANES), F32)],
        scratch_shapes=[pltpu.SemaphoreType.DMA((3 * n_u,)), pltpu.SemaphoreType.DMA((3 * n_u,)),
                        pltpu.SemaphoreType.DMA((7,)), pltpu.SemaphoreType.DMA((7,)), pltpu.SemaphoreType.DMA],
    )(*pairs, small)
    return outs[:n_u], outs[n_u]


def _sibling_share(grads):
    n_u = len(GRAD_UNITS)

    def body(*refs):
        ins, outs = refs[:n_u], refs[n_u:2 * n_u]
        send_sems, recv_sems = refs[2 * n_u:]
        x, y, c, _, _ = _place()

        def rows_of(ref, u, h):
            half = GRAD_UNITS[u][1][0] // 2
            return ref.at[pl.ds(h * half, half), :]

        sends = [pltpu.make_async_remote_copy(
            src_ref=rows_of(ins[u], u, c), dst_ref=rows_of(outs[u], u, c), send_sem=send_sems.at[u],
            recv_sem=recv_sems.at[u], device_id=(x, y, 1 - c), device_id_type=MESH) for u in range(n_u)]
        for cp in sends:
            cp.start()
        for u in range(n_u):
            pltpu.make_async_remote_copy(
                src_ref=rows_of(ins[u], u, c), dst_ref=rows_of(outs[u], u, 1 - c), send_sem=send_sems.at[u],
                recv_sem=recv_sems.at[u], device_id=(x, y, c), device_id_type=MESH).wait_recv()
        for cp in sends:
            cp.wait_send()

    return pl.pallas_call(
        body, name="grad_sibling_share", in_specs=[ANY] * n_u, out_specs=[ANY] * n_u,
        out_shape=[jax.ShapeDtypeStruct(g.shape, F32) for g in grads],
        input_output_aliases={u: u for u in range(n_u)},
        scratch_shapes=[pltpu.SemaphoreType.DMA((n_u,)), pltpu.SemaphoreType.DMA((n_u,))],
    )(*grads)


def kernel(x, g_mix_pre, g_mix_post, g_ffn_pre, g_ffn_post, conv_pw1_w, conv_pw1_b, conv_dw_w, conv_dw_b, conv_ln_g, conv_ln_b, conv_pw2_w, conv_pw2_b, attn_w_in, attn_b_f, attn_w_o, mlp_w_up, mlp_w_down, loss_target, m_g_mix_pre, m_g_mix_post, m_g_ffn_pre, m_g_ffn_post, m_conv_pw1_w, m_conv_pw1_b, m_conv_dw_w, m_conv_dw_b, m_conv_ln_g, m_conv_ln_b, m_conv_pw2_w, m_conv_pw2_b, m_attn_w_in, m_attn_b_f, m_attn_w_o, m_mlp_w_up, m_mlp_w_down, v_g_mix_pre, v_g_mix_post, v_g_ffn_pre, v_g_ffn_post, v_conv_pw1_w, v_conv_pw1_b, v_conv_dw_w, v_conv_dw_b, v_conv_ln_g, v_conv_ln_b, v_conv_pw2_w, v_conv_pw2_b, v_attn_w_in, v_attn_b_f, v_attn_w_o, v_mlp_w_up, v_mlp_w_down):
    weights = dict(g_mix_pre=g_mix_pre, g_mix_post=g_mix_post, g_ffn_pre=g_ffn_pre, g_ffn_post=g_ffn_post, conv_pw1_w=conv_pw1_w, conv_pw1_b=conv_pw1_b, conv_dw_w=conv_dw_w, conv_dw_b=conv_dw_b, conv_ln_g=conv_ln_g, conv_ln_b=conv_ln_b, conv_pw2_w=conv_pw2_w, conv_pw2_b=conv_pw2_b, attn_w_in=attn_w_in, attn_b_f=attn_b_f, attn_w_o=attn_w_o, mlp_w_up=mlp_w_up, mlp_w_down=mlp_w_down)
    mom_m = dict(g_mix_pre=m_g_mix_pre, g_mix_post=m_g_mix_post, g_ffn_pre=m_g_ffn_pre, g_ffn_post=m_g_ffn_post, conv_pw1_w=m_conv_pw1_w, conv_pw1_b=m_conv_pw1_b, conv_dw_w=m_conv_dw_w, conv_dw_b=m_conv_dw_b, conv_ln_g=m_conv_ln_g, conv_ln_b=m_conv_ln_b, conv_pw2_w=m_conv_pw2_w, conv_pw2_b=m_conv_pw2_b, attn_w_in=m_attn_w_in, attn_b_f=m_attn_b_f, attn_w_o=m_attn_w_o, mlp_w_up=m_mlp_w_up, mlp_w_down=m_mlp_w_down)
    mom_v = dict(g_mix_pre=v_g_mix_pre, g_mix_post=v_g_mix_post, g_ffn_pre=v_g_ffn_pre, g_ffn_post=v_g_ffn_post, conv_pw1_w=v_conv_pw1_w, conv_pw1_b=v_conv_pw1_b, conv_dw_w=v_conv_dw_w, conv_dw_b=v_conv_dw_b, conv_ln_g=v_conv_ln_g, conv_ln_b=v_conv_ln_b, conv_pw2_w=v_conv_pw2_w, conv_pw2_b=v_conv_pw2_b, attn_w_in=v_attn_w_in, attn_b_f=v_attn_b_f, attn_w_o=v_attn_w_o, mlp_w_up=v_mlp_w_up, mlp_w_down=v_mlp_w_down)
    order = [n for n, _ in SMALL[:4]] + ["conv_pw1_w", "conv_pw1_b", "conv_dw_w", "conv_dw_b", "conv_ln_g", "conv_ln_b",
                                          "conv_pw2_w", "conv_pw2_b", "attn_w_in", "attn_b_f", "attn_w_o", "mlp_w_up",
                                          "mlp_w_down"]
    T = x.shape[1]
    x0 = x.reshape(T, D)
    target = loss_target.reshape(T, D)
    row = lambda a, l: a[l:l + 1, :]

    shards = []
    for name, shape in GATHER_UNITS:
        w = weights[name].astype(BF16)
        w = w.reshape(w.shape[-len(shape):])
        if name == "conv_dw_w":
            w = jnp.pad(w, ((0, HALO - CONV_W), (0, 0)))
        shards.append(w)
    g_pw1, g_dw, g_pw2, g_win, g_wo, g_up, g_down = _allgather_weights(shards)
    w_pw2, w_o = g_pw2.reshape(D, D), g_wo.reshape(D, D)
    w_in = jnp.pad(g_win.transpose(1, 0, 2).reshape(D, W_IN_COLS), ((0, 0), (0, W_IN_PAD - W_IN_COLS)))
    w_dw = g_dw.transpose(1, 0, 2).reshape(HALO, D).astype(F32)
    b_f = jnp.pad(attn_b_f, ((0, 0), (0, LANES - N_HEADS)))

    def mlp_fwd(h, l):
        (act,) = _matmul(h, g_up, "nn", (BF16,), _epi_sqrelu, f"mlp{l}_up", n=DFF, b_place=_place_cols_nn(D, l))
        (mo,) = _matmul(act, g_down, "nn", (F32,), _epi_plain, f"mlp{l}_down", n=D, b_place=_place_rows_nn(D, l))
        return act, mo

    (h0,) = _resnorm_fwd(x0, None, None, row(g_mix_pre, 0), "norm_in")
    (a0,) = _matmul(h0, g_pw1, "nn", (F32,), _epi_bias, "conv_pw1", extras=((conv_pw1_b, "row"),), tn=512, n=2 * D,
                    b_place=_place_cols_nn(512))
    y0, z0 = _conv_fwd(a0, w_dw, conv_dw_b, conv_ln_g, conv_ln_b, "conv_fwd")
    (m0,) = _matmul(z0, w_pw2, "nn", (F32,), _epi_bias, "conv_pw2", extras=((conv_pw2_b, "row"),))
    x1, h1 = _resnorm_fwd(x0, m0, row(g_mix_post, 0), row(g_ffn_pre, 0), "norm_conv_out")
    act0, mo0 = mlp_fwd(h1, 0)
    x2, h2 = _resnorm_fwd(x1, mo0, row(g_ffn_post, 0), row(g_mix_pre, 1), "norm_mlp0_out")

    (proj,) = _matmul(h2, w_in, "nn", (F32,), _epi_plain, "attn_in", tn=640)
    fl = proj[:, 3 * D:]
    f_all, f1, f2, f3 = _gate_fwd(fl, b_f, "gate_fwd")
    q_aug, k_aug, v_aug, qn2, kn2 = _attn_pack(proj, f1, f2, f3, "attn_pack")
    tb = min(ATT_TB, T)
    qk = 2.02 * jnp.sqrt(qn2[:, 0, 0] * kn2[:, 0, 0]) + 1.0
    fs, fe = f_all[0::tb, :N_HEADS].T, f_all[tb - 1::tb, :N_HEADS].T
    o_aug, q2_aug = _flash_fwd(qk, fs, fe, q_aug, k_aug, v_aug, "flash_fwd")
    (o_bf,) = _merge_heads(o_aug, 1.0, BF16, "attn_merge_o")
    (m1,) = _matmul(o_bf, w_o, "nn", (F32,), _epi_plain, "attn_out")
    x3, h3 = _resnorm_fwd(x2, m1, row(g_mix_post, 1), row(g_ffn_pre, 1), "norm_attn_out")
    act1, mo1 = mlp_fwd(h3, 1)
    dy, loss_part = _resnorm_loss(x3, mo1, row(g_ffn_post, 1), target, "loss")
    loss = lax.psum(loss_part[0, 0], ("x", "y", "c"))

    G = {}

    def mlp_bwd(d_out, mo, act, h, x_in, l):
        d_mo, dg_post, _ = _norm_bwd(mo, row(g_ffn_post, l), d_out, None, BF16, f"mlp{l}_post_bwd")
        (d_up,) = _matmul(d_mo, g_down, "nt", (BF16,), _epi_dsqrelu, f"mlp{l}_dact", extras=((act, "tile"),), n=DFF,
                          b_place=_place_rows_nt(D, l))
        (dw_down,) = _matmul(act, d_mo, "tn", (F32,), _epi_plain, f"mlp{l}_dwdown")
        (dw_up,) = _matmul(h, d_up, "tn", (F32,), _epi_plain, f"mlp{l}_dwup", shard_out=True)
        (d_h,) = _matmul(d_up, g_up, "nt", (F32,), _epi_plain, f"mlp{l}_dh", n=D, b_place=_place_cols_nt(D, l))
        d_in, dg_pre, _ = _norm_bwd(x_in, row(g_ffn_pre, l), d_h, d_out, F32, f"mlp{l}_pre_bwd")
        G[f"up{l}"], G[f"down{l}"] = dw_up, dw_down.reshape(N_CHIPS, D, D)
        return d_in, dg_post, dg_pre

    d_x3, dg_ffn_post1, dg_ffn_pre1 = mlp_bwd(dy, mo1, act1, h3, x3, 1)

    d_m1, dg_mix_post1, _ = _norm_bwd(m1, row(g_mix_post, 1), d_x3, None, BF16, "attn_post_bwd")
    (dw_o,) = _matmul(o_bf, d_m1, "tn", (F32,), _epi_plain, "attn_dwo")
    G["attn_w_o"] = dw_o.reshape(N_CHIPS, D // N_CHIPS, D)
    (d_o,) = _matmul(d_m1, w_o, "nt", (F32,), _epi_plain, "attn_do")
    do_aug = _attn_prep(d_o, o_aug, "attn_prep")
    dq_aug = _flash_dq(qk, fs, fe, q2_aug, k_aug, do_aug, v_aug, "flash_dq")
    dk_aug, dv_aug = _flash_dkv(qk, fs, fe, q2_aug, k_aug, do_aug, v_aug, "flash_dkv")
    d_q, dF_q = _merge_heads(dq_aug, 0.125, BF16, "attn_merge_dq", column=AUG_F)
    d_k, dF_k = _merge_heads(dk_aug, 1.0, BF16, "attn_merge_dk", column=AUG_ONE)
    (d_v,) = _merge_heads(dv_aug, 1.0, BF16, "attn_merge_dv")
    d_fl, db_f = _gate_bwd(dF_q, dF_k, fl, b_f, "gate_bwd")
    d_proj = jnp.concatenate([d_q, d_k, d_v, d_fl.astype(BF16)], axis=1)
    (dw_in,) = _matmul(h2, d_proj, "tn", (F32,), _epi_plain, "attn_dwin", tn=640)
    G["attn_w_in"] = dw_in[:, :W_IN_COLS].reshape(D, N_CHIPS, W_IN_COLS // N_CHIPS).transpose(1, 0, 2)
    (d_h2,) = _matmul(d_proj, w_in, "nt", (F32,), _epi_plain, "attn_dh", tk=640)
    d_x2, dg_mix_pre1, _ = _norm_bwd(x2, row(g_mix_pre, 1), d_h2, d_x3, F32, "attn_pre_bwd")

    d_x1, dg_ffn_post0, dg_ffn_pre0 = mlp_bwd(d_x2, mo0, act0, h1, x1, 0)

    d_m0, dg_mix_post0, db_pw2 = _norm_bwd(m0, row(g_mix_post, 0), d_x1, None, BF16, "conv_post_bwd")
    (dw_pw2,) = _matmul(z0, d_m0, "tn", (F32,), _epi_plain, "conv_dwpw2")
    G["conv_pw2_w"] = dw_pw2.reshape(N_CHIPS, D // N_CHIPS, D)
    (d_z0,) = _matmul(d_m0, w_pw2, "nt", (F32,), _epi_plain, "conv_dz")
    d_y0, dln_g, dln_b, ddw_b = _conv_bwd_ln(d_z0, y0, conv_ln_g, conv_ln_b, "conv_bwd_ln")
    d_a0, db_pw1, ddw_w = _conv_bwd_dw(d_y0, a0, w_dw, "conv_bwd_dw")
    (G["conv_pw1_w"],) = _matmul(h0, d_a0, "tn", (F32,), _epi_plain, "conv_dwpw1", tn=512, shard_out=True)
    (d_h0,) = _matmul(d_a0, g_pw1, "nt", (F32,), _epi_plain, "conv_dh", tk=512, n=D, b_place=_place_cols_nt(512))
    d_x0, dg_mix_pre0, _ = _norm_bwd(x0, row(g_mix_pre, 0), d_h0, d_x1, F32, "conv_pre_bwd")

    small_local = {
        "g_mix_pre": jnp.concatenate([dg_mix_pre0, dg_mix_pre1]), "g_mix_post": jnp.concatenate([dg_mix_post0, dg_mix_post1]),
        "g_ffn_pre": jnp.concatenate([dg_ffn_pre0, dg_ffn_pre1]), "g_ffn_post": jnp.concatenate([dg_ffn_post0, dg_ffn_post1]),
        "conv_pw1_b": db_pw1, "conv_dw_b": ddw_b, "conv_ln_g": dln_g, "conv_ln_b": dln_b, "conv_pw2_b": db_pw2,
        "attn_b_f": db_f[:, :N_HEADS], "conv_dw_w_full": ddw_w,
    }

    chip = 2 * lax.axis_index("x") + lax.axis_index("y")
    place = jnp.stack([chip] + [k + (k >= chip).astype(jnp.int32) for k in range(N_CHIPS - 1)]
                      + [lax.axis_index("c")]).astype(jnp.int32)
    slabs = [G[n] for n, _, _ in GRAD_UNITS]
    got = _sibling_exchange(slabs)
    pairs = [_add_pair(place, s, g, f"grad_pair_{n}") for s, g, (n, _, _) in zip(slabs, got, GRAD_UNITS)]
    by_chip, small_all = _chip_exchange(pairs, _pack([small_local[n] for n, _ in SMALL_ALL], SMALL_ALL_ROWS))
    unit_grads = _sibling_share([_add_chips(place, p, b, f"grad_chips_{n}")
                                 for p, b, (n, _, _) in zip(pairs, by_chip, GRAD_UNITS)])
    grads = _unpack(_add_slots(small_all, "grad_add_small", tb=SMALL_ALL_ROWS), SMALL_ALL)
    grads["conv_dw_w"] = lax.dynamic_slice(grads.pop("conv_dw_w_full"), (0, chip * 256), (CONV_W, 256))[None]

    delta, new_m, new_v = {}, {}, {}
    per_layer = {}
    for g, (unit, shape, (name, layer)) in zip(unit_grads, GRAD_UNITS):
        view = lambda d: (d[name] if layer is None else d[name][layer]).reshape(shape)
        per_layer.setdefault(name, []).append((g,) + tuple(_adamw(view(weights), g, view(mom_m), view(mom_v),
                                                                  f"adamw_{unit}")))
    for name, parts in per_layer.items():
        full = weights[name].shape
        join = lambda n: (parts[0][n] if len(parts) == 1 else jnp.stack([p[n] for p in parts])).reshape(full)
        grads[name], delta[name], new_m[name], new_v[name] = join(0), join(1), join(2), join(3)
    dw2 = lambda d: d["conv_dw_w"].reshape(CONV_W, 256)
    d, nm, nv = _adamw(dw2(weights), dw2(grads), dw2(mom_m), dw2(mom_v), "adamw_conv_dw_w")
    full = weights["conv_dw_w"].shape
    delta["conv_dw_w"], new_m["conv_dw_w"], new_v["conv_dw_w"] = d.reshape(full), nm.reshape(full), nv.reshape(full)
    packs = [_pack([src[n] for n, _ in SMALL], SMALL_ROWS) for src in (weights, grads, mom_m, mom_v)]
    d, nm, nv = _adamw(*packs, "adamw_small")
    for dst, packed in ((delta, d), (new_m, nm), (new_v, nv)):
        dst.update(_unpack(packed, SMALL))

    grad_x = d_x0.reshape(x.shape)
    return (loss, grad_x, *[grads[n] for n in order], *[delta[n] for n in order], *[new_m[n] for n in order],
            *[new_v[n] for n in order])
```

```python
import functools

import jax
import jax.numpy as jnp
from jax import lax
from jax.experimental import pallas as pl
from jax.experimental.pallas import tpu as pltpu

F32 = jnp.float32
BF16 = jnp.bfloat16
MESH = pl.DeviceIdType.MESH

D = 1024
DFF = 4096
N_HEADS = 16
HEAD_DIM = 64
CONV_W = 31
HALO = 32
RMS_EPS = 1e-6
LN_EPS = 1e-5
MASK_VALUE = -1e30
W_IN_COLS = 3 * D + N_HEADS
W_IN_PAD = 3200
LANES = 128
SUBLANES = 8
N_CHIPS = 4
VMEM_LIMIT = 56 * 1024 * 1024

ADAM_LR = 0.001
ADAM_B1 = 0.9
ADAM_B2 = 0.999
ADAM_EPS = 1e-08
ADAM_WD = 0.01
ADAM_STEP = 10

AUG_F = 64
AUG_ONE = 67
AUG_L = 70
SKIP_BELOW = -104.0
FIXED_MAX_BELOW = 40.0

GATHER_UNITS = (
    ("conv_pw1_w", (1024, 512)),
    ("conv_dw_w", (HALO, 256)),
    ("conv_pw2_w", (256, 1024)),
    ("attn_w_in", (1024, 772)),
    ("attn_w_o", (256, 1024)),
    ("mlp_w_up", (2, 1024, 1024)),
    ("mlp_w_down", (2, 1024, 1024)),
)
GRAD_UNITS = (
    ("conv_pw1_w", (1024, 512), ("conv_pw1_w", None)),
    ("conv_pw2_w", (256, 1024), ("conv_pw2_w", None)),
    ("attn_w_in", (1024, 772), ("attn_w_in", None)),
    ("attn_w_o", (256, 1024), ("attn_w_o", None)),
    ("up0", (1024, 1024), ("mlp_w_up", 0)),
    ("up1", (1024, 1024), ("mlp_w_up", 1)),
    ("down0", (1024, 1024), ("mlp_w_down", 0)),
    ("down1", (1024, 1024), ("mlp_w_down", 1)),
)
SMALL = (
    ("g_mix_pre", (2, 1024)), ("g_mix_post", (2, 1024)), ("g_ffn_pre", (2, 1024)), ("g_ffn_post", (2, 1024)),
    ("conv_pw1_b", (1, 2048)), ("conv_dw_b", (1, 1024)), ("conv_ln_g", (1, 1024)), ("conv_ln_b", (1, 1024)),
    ("conv_pw2_b", (1, 1024)), ("attn_b_f", (1, 16)),
)
SMALL_ROWS = 120
SMALL_ALL = SMALL + (("conv_dw_w_full", (HALO, 1024)),)
SMALL_ALL_ROWS = 376


def _size(shape):
    n = 1
    for s in shape:
        n *= s
    return n


def _pack(arrays, rows):
    flat = jnp.concatenate([a.reshape(-1) for a in arrays])
    return jnp.pad(flat, (0, rows * LANES - flat.shape[0])).reshape(rows, LANES)


def _unpack(packed, table):
    flat = packed.reshape(-1)
    out, off = {}, 0
    for name, shape in table:
        n = _size(shape)
        out[name] = flat[off:off + n].reshape(shape)
        off += n
    return out


def _params(sem):
    return pltpu.CompilerParams(dimension_semantics=sem, vmem_limit_bytes=VMEM_LIMIT)


def _rms(x, g):
    return x * lax.rsqrt(jnp.mean(x * x, axis=-1, keepdims=True) + RMS_EPS) * g


def _split3(v):
    p1 = v.astype(BF16).astype(F32)
    r = v - p1
    p2 = r.astype(BF16).astype(F32)
    p3 = (r - p2).astype(BF16).astype(F32)
    return p1, p2, p3


def _matmul(a, b, mode, out_dtypes, epi, name, extras=(), tm=1024, tn=1024, tk=1024, n=None, b_place=None,
            shard_out=False):
    (K, M) = a.shape if mode == "tn" else a.shape[::-1]
    if n is None:
        N = b.shape[0] if mode == "nt" else b.shape[1]
    else:
        N = n
    tm, tn, tk = min(tm, M), min(tn, N), min(tk, K)
    nk = K // tk
    if mode == "tn":
        a_spec = pl.BlockSpec((tk, tm), lambda i, j, k: (k, i))
    else:
        a_spec = pl.BlockSpec((tm, tk), lambda i, j, k: (i, k))
    if b_place is not None:
        b_spec = b_place(tk, tn)
    elif mode == "nt":
        b_spec = pl.BlockSpec((tn, tk), lambda i, j, k: (j, k))
    else:
        b_spec = pl.BlockSpec((tk, tn), lambda i, j, k: (k, j))
    if shard_out:
        per = N // N_CHIPS // tn
        o_spec = pl.BlockSpec((None, tm, tn), lambda i, j, k: (j // per, i, j % per))
        o_shape = (N_CHIPS, M, N // N_CHIPS)
    else:
        o_spec = pl.BlockSpec((tm, tn), lambda i, j, k: (i, j))
        o_shape = (M, N)
    dims = {"nn": (((1,), (0,)), ((), ())), "nt": (((1,), (1,)), ((), ())), "tn": (((0,), (0,)), ((), ()))}[mode]
    ex_specs = []
    for _, kind in extras:
        if kind == "row":
            ex_specs.append(pl.BlockSpec((1, tn), lambda i, j, k: (0, j)))
        else:
            ex_specs.append(pl.BlockSpec((tm, tn), lambda i, j, k: (i, j)))
    n_ex, n_out = len(extras), len(out_dtypes)

    def body(*refs):
        a_ref, b_ref = refs[0], refs[1]
        ex = refs[2:2 + n_ex]
        outs = refs[2 + n_ex:2 + n_ex + n_out]
        prod = lax.dot_general(a_ref[...], b_ref[...], dims, preferred_element_type=F32)

        def finish(acc):
            res = epi(acc, *[e[...] for e in ex])
            for o, r in zip(outs, res):
                o[...] = r.astype(o.dtype)

        if nk == 1:
            finish(prod)
        else:
            acc_ref = refs[-1]
            k = pl.program_id(2)

            @pl.when(k == 0)
            def _():
                acc_ref[...] = prod

            @pl.when(k > 0)
            def _():
                acc_ref[...] += prod

            @pl.when(k == nk - 1)
            def _():
                finish(acc_ref[...])

    return pl.pallas_call(
        body,
        name=name,
        grid=(M // tm, N // tn, nk),
        in_specs=[a_spec, b_spec] + ex_specs,
        out_specs=[o_spec for _ in out_dtypes],
        out_shape=[jax.ShapeDtypeStruct(o_shape, dt) for dt in out_dtypes],
        scratch_shapes=[pltpu.VMEM((tm, tn), F32)] if nk > 1 else [],
        compiler_params=_params(("parallel", "parallel", "arbitrary")),
    )(a, b, *[e for e, _ in extras])


def _place_cols_nn(width, layer=None):
    def place(tk, tn):
        per = width // tn
        if layer is None:
            return pl.BlockSpec((None, tk, tn), lambda i, j, k: (j // per, k, j % per))
        return pl.BlockSpec((None, None, tk, tn), lambda i, j, k: (j // per, layer, k, j % per))
    return place


def _place_cols_nt(width, layer=None):
    def place(tk, tn):
        per = width // tk
        if layer is None:
            return pl.BlockSpec((None, tn, tk), lambda i, j, k: (k // per, j, k % per))
        return pl.BlockSpec((None, None, tn, tk), lambda i, j, k: (k // per, layer, j, k % per))
    return place


def _place_rows_nn(height, layer):
    def place(tk, tn):
        per = height // tk
        return pl.BlockSpec((None, None, tk, tn), lambda i, j, k: (k // per, layer, k % per, j))
    return place


def _place_rows_nt(height, layer):
    def place(tk, tn):
        per = height // tn
        return pl.BlockSpec((None, None, tn, tk), lambda i, j, k: (j // per, layer, j % per, k))
    return place


def _epi_plain(acc):
    return (acc,)


def _epi_bias(acc, bias):
    return (acc + bias,)


def _epi_sqrelu(acc):
    r = jnp.maximum(acc, 0.0)
    return (r * r,)


def _epi_dsqrelu(acc, act):
    return (acc * (2.0 * jnp.sqrt(act.astype(F32))),)


def _resnorm_fwd(x, m, g_post, g_next, name, tr=512):
    T = x.shape[0]
    has_m = m is not None
    row = pl.BlockSpec((tr, D), lambda i: (i, 0))
    vec = pl.BlockSpec((1, D), lambda i: (0, 0))

    def body(*refs):
        if has_m:
            x_ref, m_ref, gp_ref, gn_ref, x1_ref, h_ref = refs
            x1 = x_ref[...] + _rms(m_ref[...], gp_ref[...])
            x1_ref[...] = x1
        else:
            x_ref, gn_ref, h_ref = refs
            x1 = x_ref[...]
        h_ref[...] = _rms(x1, gn_ref[...]).astype(BF16)

    if has_m:
        args, in_specs = (x, m, g_post, g_next), [row, row, vec, vec]
        out_specs = [row, row]
        out_shape = [jax.ShapeDtypeStruct((T, D), F32), jax.ShapeDtypeStruct((T, D), BF16)]
    else:
        args, in_specs = (x, g_next), [row, vec]
        out_specs = [row]
        out_shape = [jax.ShapeDtypeStruct((T, D), BF16)]
    return pl.pallas_call(
        body, name=name, grid=(T // tr,), in_specs=in_specs, out_specs=out_specs, out_shape=out_shape,
        compiler_params=_params(("parallel",)),
    )(*args)


def _resnorm_loss(x, m, g_post, target, name, tr=512):
    T = x.shape[0]
    row = pl.BlockSpec((tr, D), lambda i: (i, 0))
    vec = pl.BlockSpec((1, D), lambda i: (0, 0))

    def body(x_ref, m_ref, gp_ref, t_ref, dy_ref, loss_ref):
        diff = x_ref[...] + _rms(m_ref[...], gp_ref[...]) - t_ref[...]
        dy_ref[...] = diff * (1.0 / D)

        @pl.when(pl.program_id(0) == 0)
        def _():
            loss_ref[...] = jnp.zeros_like(loss_ref)

        per_token = jnp.mean(diff * diff, axis=-1, keepdims=True)
        loss_ref[...] += 0.5 * jnp.sum(per_token)

    return pl.pallas_call(
        body, name=name, grid=(T // tr,), in_specs=[row, row, vec, row],
        out_specs=[row, pl.BlockSpec((8, LANES), lambda i: (0, 0))],
        out_shape=[jax.ShapeDtypeStruct((T, D), F32), jax.ShapeDtypeStruct((8, LANES), F32)],
        compiler_params=_params(("arbitrary",)),
    )(x, m, g_post, target)


def _norm_bwd(x, g, dy, resid, out_dtype, name, tr=512):
    T = x.shape[0]
    has_r = resid is not None
    row = pl.BlockSpec((tr, D), lambda i: (i, 0))
    vec = pl.BlockSpec((1, D), lambda i: (0, 0))

    def body(*refs):
        if has_r:
            x_ref, g_ref, dy_ref, r_ref, dx_ref, dg_ref, cs_ref = refs
        else:
            x_ref, g_ref, dy_ref, dx_ref, dg_ref, cs_ref = refs
        xv, dyv = x_ref[...], dy_ref[...]
        r = lax.rsqrt(jnp.mean(xv * xv, axis=-1, keepdims=True) + RMS_EPS)
        gy = dyv * g_ref[...]
        c = jnp.sum(gy * xv, axis=-1, keepdims=True) * (1.0 / D)
        dx = r * gy - xv * (r * r * r * c)

        @pl.when(pl.program_id(0) == 0)
        def _():
            dg_ref[...] = jnp.zeros_like(dg_ref)
            cs_ref[...] = jnp.zeros_like(cs_ref)

        dg_ref[...] += jnp.sum(dyv * (xv * r), axis=0, keepdims=True)
        cs_ref[...] += jnp.sum(dx, axis=0, keepdims=True)
        if has_r:
            dx = dx + r_ref[...]
        dx_ref[...] = dx.astype(dx_ref.dtype)

    args = (x, g, dy) + ((resid,) if has_r else ())
    return pl.pallas_call(
        body, name=name, grid=(T // tr,), in_specs=[row, vec, row] + ([row] if has_r else []),
        out_specs=[row, vec, vec],
        out_shape=[jax.ShapeDtypeStruct((T, D), out_dtype), jax.ShapeDtypeStruct((1, D), F32),
                   jax.ShapeDtypeStruct((1, D), F32)],
        compiler_params=_params(("arbitrary",)),
    )(*args)


CONV_TB = 256
CONV_RC = 32


def _glu(a):
    return a[:, :D] * jax.nn.sigmoid(a[:, D:])


def _shifted_copies(win):
    rows = win.shape[1]
    for p in range(1, SUBLANES):
        win[p, 0:rows - SUBLANES, :] = win[0, p:p + rows - SUBLANES, :]


def _window(win, start):
    p = start % SUBLANES
    return win[p, start - p:start - p + CONV_RC, :]


def _conv_fwd(a, dw_w, dw_b, ln_g, ln_b, name):
    T = a.shape[0]
    tb = min(CONV_TB, T)
    per = tb // HALO
    vec = pl.BlockSpec((1, D), lambda i: (0, 0))

    def body(cur_ref, prev_ref, w_ref, b_ref, lg_ref, lb_ref, y_ref, z_ref, uwin):
        i = pl.program_id(0)
        uwin[0, 0:HALO, :] = jnp.where(i > 0, _glu(prev_ref[...]), 0.0)
        uwin[0, HALO:, :] = _glu(cur_ref[...])
        _shifted_copies(uwin)
        for r in range(tb // CONV_RC):
            acc = jnp.broadcast_to(b_ref[...], (CONV_RC, D))
            for k in range(CONV_W):
                start = r * CONV_RC + HALO - (CONV_W - 1) + k
                acc = acc + _window(uwin, start) * w_ref[k:k + 1, :]
            rows = slice(r * CONV_RC, (r + 1) * CONV_RC)
            y_ref[rows, :] = acc
            mu = jnp.mean(acc, axis=-1, keepdims=True)
            xc = acc - mu
            var = jnp.mean(xc * xc, axis=-1, keepdims=True)
            n = xc * lax.rsqrt(var + LN_EPS) * lg_ref[...] + lb_ref[...]
            z_ref[rows, :] = (n * jax.nn.sigmoid(n)).astype(BF16)

    return pl.pallas_call(
        body, name=name, grid=(T // tb,),
        in_specs=[pl.BlockSpec((tb, 2 * D), lambda i: (i, 0)),
                  pl.BlockSpec((HALO, 2 * D), lambda i: (jnp.maximum(i * per - 1, 0), 0)),
                  pl.BlockSpec((HALO, D), lambda i: (0, 0)), vec, vec, vec],
        out_specs=[pl.BlockSpec((tb, D), lambda i: (i, 0)), pl.BlockSpec((tb, D), lambda i: (i, 0))],
        out_shape=[jax.ShapeDtypeStruct((T, D), F32), jax.ShapeDtypeStruct((T, D), BF16)],
        scratch_shapes=[pltpu.VMEM((SUBLANES, tb + HALO, D), F32)],
        compiler_params=_params(("parallel",)),
    )(a, a, dw_w, dw_b, ln_g, ln_b)


def _conv_bwd_ln(dz, y, ln_g, ln_b, name, tr=256):
    T = y.shape[0]
    tr = min(tr, T)
    row = pl.BlockSpec((tr, D), lambda i: (i, 0))
    vec = pl.BlockSpec((1, D), lambda i: (0, 0))

    def body(dz_ref, y_ref, lg_ref, lb_ref, dy_ref, dlg_ref, dlb_ref, db_ref):
        yv = y_ref[...]
        mu = jnp.mean(yv, axis=-1, keepdims=True)
        xc = yv - mu
        rstd = lax.rsqrt(jnp.mean(xc * xc, axis=-1, keepdims=True) + LN_EPS)
        yh = xc * rstd
        n = yh * lg_ref[...] + lb_ref[...]
        sg = jax.nn.sigmoid(n)
        dn = dz_ref[...] * (sg * (1.0 + n * (1.0 - sg)))
        dyh = dn * lg_ref[...]
        dyv = rstd * (dyh - jnp.mean(dyh, axis=-1, keepdims=True) - yh * jnp.mean(dyh * yh, axis=-1, keepdims=True))
        dy_ref[...] = dyv

        @pl.when(pl.program_id(0) == 0)
        def _():
            dlg_ref[...] = jnp.zeros_like(dlg_ref)
            dlb_ref[...] = jnp.zeros_like(dlb_ref)
            db_ref[...] = jnp.zeros_like(db_ref)

        dlg_ref[...] += jnp.sum(dn * yh, axis=0, keepdims=True)
        dlb_ref[...] += jnp.sum(dn, axis=0, keepdims=True)
        db_ref[...] += jnp.sum(dyv, axis=0, keepdims=True)

    return pl.pallas_call(
        body, name=name, grid=(T // tr,), in_specs=[row, row, vec, vec], out_specs=[row, vec, vec, vec],
        out_shape=[jax.ShapeDtypeStruct((T, D), F32)] + [jax.ShapeDtypeStruct((1, D), F32)] * 3,
        compiler_params=_params(("arbitrary",)),
    )(dz, y, ln_g, ln_b)


def _conv_bwd_dw(dy, a, dw_w, name):
    T = a.shape[0]
    tb = min(CONV_TB, T)
    per = tb // HALO
    last_halo = T // HALO - 1
    n_steps = T // tb

    def body(dyc_ref, dyn_ref, cur_ref, prev_ref, w_ref, da_ref, dbias_ref, dw_ref, uwin, dywin, dwacc):
        i = pl.program_id(0)

        @pl.when(i == 0)
        def _():
            dbias_ref[...] = jnp.zeros_like(dbias_ref)
            dwacc[...] = jnp.zeros_like(dwacc)

        uwin[0, 0:HALO, :] = jnp.where(i > 0, _glu(prev_ref[...]), 0.0)
        uwin[0, HALO:, :] = _glu(cur_ref[...])
        dywin[0, 0:tb, :] = dyc_ref[...]
        dywin[0, tb:, :] = jnp.where(i < n_steps - 1, dyn_ref[...], 0.0)
        _shifted_copies(uwin)
        _shifted_copies(dywin)
        for r in range(tb // CONV_RC):
            rows = slice(r * CONV_RC, (r + 1) * CONV_RC)
            dy_c = dywin[0, rows, :]
            du = jnp.zeros((CONV_RC, D), F32)
            for k in range(CONV_W):
                du = du + _window(dywin, r * CONV_RC + (CONV_W - 1) - k) * w_ref[k:k + 1, :]
                prod = dy_c * _window(uwin, r * CONV_RC + HALO - (CONV_W - 1) + k)
                part = prod[0:8, :]
                for q in range(1, CONV_RC // 8):
                    part = part + prod[8 * q:8 * q + 8, :]
                dwacc[8 * k:8 * k + 8, :] += part
            av = cur_ref[rows, :]
            a1, sg = av[:, :D], jax.nn.sigmoid(av[:, D:])
            da1 = du * sg
            da2 = du * a1 * (sg * (1.0 - sg))
            da_ref[rows, 0:D] = da1.astype(BF16)
            da_ref[rows, D:] = da2.astype(BF16)
            dbias_ref[:, 0:D] += jnp.sum(da1, axis=0, keepdims=True)
            dbias_ref[:, D:] += jnp.sum(da2, axis=0, keepdims=True)

        @pl.when(i == n_steps - 1)
        def _():
            for k in range(CONV_W):
                dw_ref[k:k + 1, :] = jnp.sum(dwacc[8 * k:8 * k + 8, :], axis=0, keepdims=True)
            dw_ref[CONV_W:, :] = jnp.zeros((HALO - CONV_W, D), F32)

    return pl.pallas_call(
        body, name=name, grid=(n_steps,),
        in_specs=[pl.BlockSpec((tb, D), lambda i: (i, 0)),
                  pl.BlockSpec((HALO, D), lambda i: (jnp.minimum((i + 1) * per, last_halo), 0)),
                  pl.BlockSpec((tb, 2 * D), lambda i: (i, 0)),
                  pl.BlockSpec((HALO, 2 * D), lambda i: (jnp.maximum(i * per - 1, 0), 0)),
                  pl.BlockSpec((HALO, D), lambda i: (0, 0))],
        out_specs=[pl.BlockSpec((tb, 2 * D), lambda i: (i, 0)), pl.BlockSpec((1, 2 * D), lambda i: (0, 0)),
                   pl.BlockSpec((HALO, D), lambda i: (0, 0))],
        out_shape=[jax.ShapeDtypeStruct((T, 2 * D), BF16), jax.ShapeDtypeStruct((1, 2 * D), F32),
                   jax.ShapeDtypeStruct((HALO, D), F32)],
        scratch_shapes=[pltpu.VMEM((SUBLANES, tb + HALO, D), F32), pltpu.VMEM((SUBLANES, tb + HALO, D), F32),
                        pltpu.VMEM((8 * HALO, D), F32)],
        compiler_params=_params(("arbitrary",)),
    )(dy, dy, a, a, dw_w)


GATE_TB = 512


def _gate_fwd(fl, b_f, name):
    T = fl.shape[0]
    tb = min(GATE_TB, T)
    row = pl.BlockSpec((tb, LANES), lambda i: (i, 0))

    def body(fl_ref, b_ref, f_ref, f1_ref, f2_ref, f3_ref, carry):
        @pl.when(pl.program_id(0) == 0)
        def _():
            carry[...] = jnp.zeros_like(carry)

        z = fl_ref[...] + b_ref[...]
        lf = jnp.minimum(z, 0.0) - jnp.log(1.0 + jnp.exp(-jnp.abs(z)))
        tri = (lax.broadcasted_iota(jnp.int32, (tb, tb), 0) >= lax.broadcasted_iota(jnp.int32, (tb, tb), 1)).astype(F32)
        f = jnp.dot(tri, lf, precision=lax.Precision.HIGHEST, preferred_element_type=F32) + carry[...]
        carry[...] = f[tb - 1:tb, :]
        f_ref[...] = f
        f1_ref[...], f2_ref[...], f3_ref[...] = _split3(f)

    return pl.pallas_call(
        body, name=name, grid=(T // tb,), in_specs=[row, pl.BlockSpec((1, LANES), lambda i: (0, 0))],
        out_specs=[row] * 4, out_shape=[jax.ShapeDtypeStruct((T, LANES), F32)] * 4,
        scratch_shapes=[pltpu.VMEM((1, LANES), F32)],
        compiler_params=_params(("arbitrary",)),
    )(fl, b_f)


def _gate_bwd(dF_q, dF_k, fl, b_f, name):
    T = fl.shape[0]
    tb = min(GATE_TB, T)
    nb = T // tb
    row = pl.BlockSpec((tb, LANES), lambda i: (nb - 1 - i, 0))
    vec = pl.BlockSpec((1, LANES), lambda i: (0, 0))

    def body(dfq_ref, dfk_ref, fl_ref, b_ref, dfl_ref, db_ref, carry):
        @pl.when(pl.program_id(0) == 0)
        def _():
            carry[...] = jnp.zeros_like(carry)
            db_ref[...] = jnp.zeros_like(db_ref)

        tri = (lax.broadcasted_iota(jnp.int32, (tb, tb), 0) <= lax.broadcasted_iota(jnp.int32, (tb, tb), 1)).astype(F32)
        rc = jnp.dot(tri, dfq_ref[...] - dfk_ref[...], precision=lax.Precision.HIGHEST,
                     preferred_element_type=F32) + carry[...]
        carry[...] = rc[0:1, :]
        dfl = rc * jax.nn.sigmoid(-(fl_ref[...] + b_ref[...]))
        dfl_ref[...] = dfl
        db_ref[...] += jnp.sum(dfl, axis=0, keepdims=True)

    return pl.pallas_call(
        body, name=name, grid=(nb,), in_specs=[row, row, row, vec], out_specs=[row, vec],
        out_shape=[jax.ShapeDtypeStruct((T, LANES), F32), jax.ShapeDtypeStruct((1, LANES), F32)],
        scratch_shapes=[pltpu.VMEM((1, LANES), F32)],
        compiler_params=_params(("arbitrary",)),
    )(dF_q, dF_k, fl, b_f)


ATT_TB = 512
LAYOUT_TB = 2048
NT = (((1,), (1,)), ((), ()))
NN = (((1,), (0,)), ((), ()))


def _lane_insert(base, parts, first_lane):
    lane = lax.broadcasted_iota(jnp.int32, base.shape, 1)
    out = base
    for n, p in enumerate(parts):
        out = jnp.where(lane == first_lane + n, p, out)
    return out


def _causal(tb):
    return lax.broadcasted_iota(jnp.int32, (tb, tb), 0) >= lax.broadcasted_iota(jnp.int32, (tb, tb), 1)


def _other_head(x):
    return pltpu.roll(x, HEAD_DIM, 1)


def _attn_pack(proj, f1, f2, f3, name):
    T = proj.shape[0]
    tb = min(LAYOUT_TB, T)
    pairs = N_HEADS // 2
    cols = lambda first: pl.BlockSpec((tb, LANES), lambda p, i: (i, first + p))
    gate = pl.BlockSpec((tb, LANES), lambda p, i: (i, 0))
    heads = pl.BlockSpec((2, tb, LANES), lambda p, i: (p, i, 0))
    norm = pl.BlockSpec((2, 8, LANES), lambda p, i: (p, 0, 0))

    def body(q_ref, k_ref, v_ref, f1_ref, f2_ref, f3_ref, qa_ref, ka_ref, va_ref, qn_ref, kn_ref):
        p, i = pl.program_id(0), pl.program_id(1)
        lane = lax.broadcasted_iota(jnp.int32, (tb, LANES), 1)
        data = lane < HEAD_DIM

        @pl.when(i == 0)
        def _():
            qn_ref[...] = jnp.zeros_like(qn_ref)
            kn_ref[...] = jnp.zeros_like(kn_ref)

        qv, kv, vv = q_ref[...] * 0.125, k_ref[...], v_ref[...]
        for e in range(2):
            fcol = [jnp.sum(jnp.where(lane == 2 * p + e, f[...], 0.0), axis=-1, keepdims=True)
                    for f in (f1_ref, f2_ref, f3_ref)]
            pick = (lambda t: t) if e == 0 else _other_head
            qd = jnp.where(data, pick(qv), 0.0).astype(BF16).astype(F32)
            kd = jnp.where(data, pick(kv), 0.0).astype(BF16).astype(F32)
            vd = jnp.where(data, pick(vv), 0.0)
            qa_ref[e] = _lane_insert(qd, fcol + [1.0, 1.0, 1.0], AUG_F).astype(BF16)
            ka_ref[e] = _lane_insert(kd, [1.0, 1.0, 1.0] + [-f for f in fcol] + [1.0, 1.0, 1.0], AUG_F).astype(BF16)
            va_ref[e] = _lane_insert(vd, [-1.0, -1.0, -1.0, 1.0], AUG_F).astype(BF16)
            qn_ref[e] = jnp.maximum(qn_ref[e], jnp.max(jnp.sum(qd * qd, axis=-1, keepdims=True)))
            kn_ref[e] = jnp.maximum(kn_ref[e], jnp.max(jnp.sum(kd * kd, axis=-1, keepdims=True)))

    aug = jax.ShapeDtypeStruct((N_HEADS, T, LANES), BF16)
    nrm = jax.ShapeDtypeStruct((N_HEADS, 8, LANES), F32)
    return pl.pallas_call(
        body, name=name, grid=(pairs, T // tb),
        in_specs=[cols(0), cols(pairs), cols(2 * pairs), gate, gate, gate],
        out_specs=[heads, heads, heads, norm, norm], out_shape=[aug, aug, aug, nrm, nrm],
        compiler_params=_params(("parallel", "arbitrary")),
    )(proj, proj, proj, f1, f2, f3)


def _merge_heads(x_aug, scale, out_dtype, name, column=None):
    H, T, _ = x_aug.shape
    tb = min(LAYOUT_TB, T)

    def body(x_ref, o_ref, *col_ref):
        p = pl.program_id(1)
        lane = lax.broadcasted_iota(jnp.int32, (tb, LANES), 1)
        x0, x1 = x_ref[0], x_ref[1]
        o_ref[...] = (jnp.where(lane < HEAD_DIM, x0, _other_head(x1)) * scale).astype(out_dtype)
        if column is not None:
            @pl.when(p == 0)
            def _():
                col_ref[0][...] = jnp.zeros_like(col_ref[0])

            c0 = jnp.sum(jnp.where(lane == column, x0, 0.0), axis=-1, keepdims=True)
            c1 = jnp.sum(jnp.where(lane == column, x1, 0.0), axis=-1, keepdims=True)
            col_ref[0][...] += jnp.where(lane == 2 * p, c0, 0.0) + jnp.where(lane == 2 * p + 1, c1, 0.0)

    out_specs = [pl.BlockSpec((tb, LANES), lambda i, p: (i, p))]
    out_shape = [jax.ShapeDtypeStruct((T, D), out_dtype)]
    if column is not None:
        out_specs.append(pl.BlockSpec((tb, LANES), lambda i, p: (i, 0)))
        out_shape.append(jax.ShapeDtypeStruct((T, LANES), F32))
    return pl.pallas_call(
        body, name=name, grid=(T // tb, H // 2),
        in_specs=[pl.BlockSpec((2, tb, LANES), lambda i, p: (p, i, 0))],
        out_specs=out_specs, out_shape=out_shape,
        compiler_params=_params(("parallel", "arbitrary")),
    )(x_aug)


def _attn_prep(d_o, o_aug, name):
    H, T, _ = o_aug.shape
    tb = min(LAYOUT_TB, T)
    heads = pl.BlockSpec((2, tb, LANES), lambda p, i: (p, i, 0))

    def body(do_ref, o_ref, out_ref):
        lane = lax.broadcasted_iota(jnp.int32, (tb, LANES), 1)
        dov = do_ref[...]
        for e in range(2):
            d_e = jnp.where(lane < HEAD_DIM, dov if e == 0 else _other_head(dov), 0.0)
            delta = jnp.sum(d_e * o_ref[e], axis=-1, keepdims=True)
            out_ref[e] = _lane_insert(d_e, _split3(delta), AUG_F).astype(BF16)

    return pl.pallas_call(
        body, name=name, grid=(H // 2, T // tb),
        in_specs=[pl.BlockSpec((tb, LANES), lambda p, i: (i, p)), heads], out_specs=heads,
        out_shape=jax.ShapeDtypeStruct((H, T, LANES), BF16),
        compiler_params=_params(("parallel", "parallel")),
    )(d_o, o_aug)


def _flash_spec(n_in, n_out, H, nb, tb, T, resident, scratch, n_tables=3):
    blk = pl.BlockSpec((2, tb, LANES), lambda h, i, *_: (h, i, 0))
    whole = pl.BlockSpec((2, T, LANES), lambda h, i, *_: (h, 0, 0))
    return pltpu.PrefetchScalarGridSpec(
        num_scalar_prefetch=n_tables, grid=(H // 2, nb),
        in_specs=[whole if resident[n] else blk for n in range(n_in)],
        out_specs=[blk] * n_out, scratch_shapes=scratch)


def _first_live(qk_ref, fs_ref, fe_ref, h, i):
    top = qk_ref[h] + fs_ref[h, i]
    return lax.fori_loop(0, i, lambda j, n: n + jnp.where(top - fe_ref[h, j] < SKIP_BELOW, 1, 0), 0)


def _flash_fwd(qk, fs, fe, kmax, q_aug, k_aug, v_aug, name):
    H, T, _ = q_aug.shape
    tb = min(ATT_TB, T)

    def body(qk_ref, fs_ref, fe_ref, km_ref, q_ref, k_ref, v_ref, o_ref, q2_ref, m_s, acc_s, qm_s):
        h, i = 2 * pl.program_id(0), pl.program_id(1)
        acc_s[...] = jnp.zeros_like(acc_s)
        first = jnp.minimum(_first_live(qk_ref, fs_ref, fe_ref, h, i), _first_live(qk_ref, fs_ref, fe_ref, h + 1, i))
        bounded = jnp.logical_and(qk_ref[h] < FIXED_MAX_BELOW, qk_ref[h + 1] < FIXED_MAX_BELOW)

        def sweep(step):
            def loop_body(kb, carry):
                step(kb, False)
                return carry

            lax.fori_loop(first, i, loop_body, 0)
            step(i, True)

        @pl.when(bounded)
        def _():
            lane = lax.broadcasted_iota(jnp.int32, (tb, LANES), 1)
            for e in range(2):
                qf = q_ref[e].astype(F32)
                norm = jnp.sqrt(jnp.sum(jnp.where(lane < HEAD_DIM, qf * qf, 0.0), axis=-1, keepdims=True))
                bound = norm * (1.01 * km_ref[h + e]) + 1e-3
                m_s[e] = bound
                qm_s[e] = _lane_insert(qf, [-p for p in _split3(bound)], AUG_L).astype(BF16)

            def step(kb, masked):
                rows = pl.ds(pl.multiple_of(kb * tb, tb), tb)
                for e in range(2):
                    p = jnp.exp(lax.dot_general(qm_s[e], k_ref[e, rows, :], NT, preferred_element_type=F32))
                    if masked:
                        p = jnp.where(_causal(tb), p, 0.0)
                    acc_s[e] += lax.dot_general(p.astype(BF16), v_ref[e, rows, :], NN, preferred_element_type=F32)

            sweep(step)

        @pl.when(jnp.logical_not(bounded))
        def _():
            m_s[...] = jnp.full(m_s.shape, -jnp.inf, F32)

            def step(kb, masked):
                rows = pl.ds(pl.multiple_of(kb * tb, tb), tb)
                for e in range(2):
                    s = lax.dot_general(q_ref[e], k_ref[e, rows, :], NT, preferred_element_type=F32)
                    if masked:
                        s = jnp.where(_causal(tb), s, MASK_VALUE)
                    m_old = m_s[e]
                    m_new = jnp.maximum(m_old, jnp.max(s, axis=-1, keepdims=True))
                    p = jnp.exp(s - m_new)
                    acc_s[e] = jnp.exp(m_old - m_new) * acc_s[e] + lax.dot_general(
                        p.astype(BF16), v_ref[e, rows, :], NN, preferred_element_type=F32)
                    m_s[e] = m_new

            sweep(step)

        for e in range(2):
            acc = acc_s[e]
            lane = lax.broadcasted_iota(jnp.int32, acc.shape, 1)
            l = jnp.sum(jnp.where(lane == AUG_ONE, acc, 0.0), axis=-1, keepdims=True)
            o_ref[e] = acc / l
            lse = m_s[e] + jnp.log(l)
            q2_ref[e] = _lane_insert(q_ref[e].astype(F32), [-p for p in _split3(lse)], AUG_L).astype(BF16)

    return pl.pallas_call(
        body, name=name,
        grid_spec=_flash_spec(3, 2, H, T // tb, tb, T, (False, True, True),
                              [pltpu.VMEM((2, tb, 1), F32), pltpu.VMEM((2, tb, LANES), F32),
                               pltpu.VMEM((2, tb, LANES), BF16)], n_tables=4),
        out_shape=[jax.ShapeDtypeStruct((H, T, LANES), F32), jax.ShapeDtypeStruct((H, T, LANES), BF16)],
        compiler_params=_params(("parallel", "arbitrary")),
    )(qk, fs, fe, kmax, q_aug, k_aug, v_aug)


def _flash_dq(qk, fs, fe, q2, k_aug, do_aug, v_aug, name):
    H, T, _ = q2.shape
    tb = min(ATT_TB, T)

    def body(qk_ref, fs_ref, fe_ref, q_ref, k_ref, do_ref, v_ref, dq_ref, acc_s):
        h, i = 2 * pl.program_id(0), pl.program_id(1)
        acc_s[...] = jnp.zeros_like(acc_s)

        def step(kb, masked):
            rows = pl.ds(pl.multiple_of(kb * tb, tb), tb)
            for e in range(2):
                k = k_ref[e, rows, :]
                p = jnp.exp(lax.dot_general(q_ref[e], k, NT, preferred_element_type=F32))
                if masked:
                    p = jnp.where(_causal(tb), p, 0.0)
                ds = p * lax.dot_general(do_ref[e], v_ref[e, rows, :], NT, preferred_element_type=F32)
                acc_s[e] += lax.dot_general(ds.astype(BF16), k, NN, preferred_element_type=F32)

        def loop_body(kb, carry):
            step(kb, False)
            return carry

        first = jnp.minimum(_first_live(qk_ref, fs_ref, fe_ref, h, i), _first_live(qk_ref, fs_ref, fe_ref, h + 1, i))
        lax.fori_loop(first, i, loop_body, 0)
        step(i, True)
        dq_ref[...] = acc_s[...]

    return pl.pallas_call(
        body, name=name,
        grid_spec=_flash_spec(4, 1, H, T // tb, tb, T, (False, True, False, True), [pltpu.VMEM((2, tb, LANES), F32)]),
        out_shape=[jax.ShapeDtypeStruct((H, T, LANES), F32)],
        compiler_params=_params(("parallel", "arbitrary")),
    )(qk, fs, fe, q2, k_aug, do_aug, v_aug)[0]


def _flash_dkv(qk, fs, fe, q2, k_aug, do_aug, v_aug, name):
    H, T, _ = q2.shape
    tb = min(ATT_TB, T)
    nb = T // tb

    def body(qk_ref, fs_ref, fe_ref, q_ref, k_ref, do_ref, v_ref, dk_ref, dv_ref, dk_s, dv_s):
        h, i = 2 * pl.program_id(0), pl.program_id(1)
        dk_s[...] = jnp.zeros_like(dk_s)
        dv_s[...] = jnp.zeros_like(dv_s)

        def step(qb, masked):
            rows = pl.ds(pl.multiple_of(qb * tb, tb), tb)
            for e in range(2):
                q, dov = q_ref[e, rows, :], do_ref[e, rows, :]
                pt = jnp.exp(lax.dot_general(k_ref[e], q, NT, preferred_element_type=F32))
                if masked:
                    pt = jnp.where(
                        lax.broadcasted_iota(jnp.int32, (tb, tb), 1) >= lax.broadcasted_iota(jnp.int32, (tb, tb), 0),
                        pt, 0.0)
                dst = pt * lax.dot_general(v_ref[e], dov, NT, preferred_element_type=F32)
                dv_s[e] += lax.dot_general(pt.astype(BF16), dov, NN, preferred_element_type=F32)
                dk_s[e] += lax.dot_general(dst.astype(BF16), q, NN, preferred_element_type=F32)

        step(i, True)

        def loop_body(qb, carry):
            step(qb, False)
            return carry

        def live_after(head):
            base = qk_ref[head] - fe_ref[head, i]
            return lax.fori_loop(i + 1, nb, lambda b, n: n + jnp.where(base + fs_ref[head, b] < SKIP_BELOW, 0, 1), 0)

        lax.fori_loop(i + 1, i + 1 + jnp.maximum(live_after(h), live_after(h + 1)), loop_body, 0)
        dk_ref[...] = dk_s[...]
        dv_ref[...] = dv_s[...]

    return pl.pallas_call(
        body, name=name,
        grid_spec=_flash_spec(4, 2, H, nb, tb, T, (True, False, True, False),
                              [pltpu.VMEM((2, tb, LANES), F32), pltpu.VMEM((2, tb, LANES), F32)]),
        out_shape=[jax.ShapeDtypeStruct((H, T, LANES), F32)] * 2,
        compiler_params=_params(("parallel", "arbitrary")),
    )(qk, fs, fe, q2, k_aug, do_aug, v_aug)


def _add_pair(place, slab, got, name):
    _, half, cols = got.shape
    grid_spec = pltpu.PrefetchScalarGridSpec(
        num_scalar_prefetch=1, grid=(N_CHIPS,),
        in_specs=[pl.BlockSpec((None, half, cols), lambda j, s: (j, s[4], 0)),
                  pl.BlockSpec((None, half, cols), lambda j, s: (j, 0, 0))],
        out_specs=pl.BlockSpec((None, half, cols), lambda j, s: (j, 0, 0)))

    def body(s_ref, a_ref, b_ref, o_ref):
        o_ref[...] = (a_ref[...] + b_ref[...]).astype(BF16)

    return pl.pallas_call(
        body, name=name, grid_spec=grid_spec, out_shape=jax.ShapeDtypeStruct(got.shape, BF16),
        compiler_params=_params(("parallel",)),
    )(place, slab, got)


def _add_chips(place, pair, by_chip, name):
    _, half, cols = pair.shape
    tb = min(256, half)
    steps = half // tb
    slot = lambda n: pl.BlockSpec((None, tb, cols), lambda i, s: (s[n], i, 0))
    grid_spec = pltpu.PrefetchScalarGridSpec(
        num_scalar_prefetch=1, grid=(steps,), in_specs=[slot(0), slot(1), slot(2), slot(3)],
        out_specs=pl.BlockSpec((tb, cols), lambda i, s: (s[4] * steps + i, 0)))

    def body(s_ref, own_ref, b1_ref, b2_ref, b3_ref, o_ref):
        o_ref[...] = ((own_ref[...].astype(F32) + b1_ref[...].astype(F32)) + b2_ref[...].astype(F32)) \
            + b3_ref[...].astype(F32)

    return pl.pallas_call(
        body, name=name, grid_spec=grid_spec, out_shape=jax.ShapeDtypeStruct((2 * half, cols), F32),
        compiler_params=_params(("parallel",)),
    )(place, pair, by_chip, by_chip, by_chip)


def _add_slots(b, name, tb=128):
    n, rows, cols = b.shape
    tb = tb if rows % tb == 0 else rows
    specs = [pl.BlockSpec((None, tb, cols), functools.partial(lambda j, i: (j, i, 0), j)) for j in range(n)]

    def body(*refs):
        acc = refs[0][...]
        for r in refs[1:n]:
            acc = acc + r[...]
        refs[n][...] = acc

    return pl.pallas_call(
        body, name=name, grid=(rows // tb,), in_specs=specs, out_specs=pl.BlockSpec((tb, cols), lambda i: (i, 0)),
        out_shape=jax.ShapeDtypeStruct((rows, cols), F32), compiler_params=_params(("parallel",)),
    )(*([b] * n))


def _adamw(w, g, m, v, name):
    rows, cols = w.shape
    tr = 256 if rows % 256 == 0 else rows
    blk = pl.BlockSpec((tr, cols), lambda i: (i, 0))

    def body(w_ref, g_ref, m_ref, v_ref, d_ref, nm_ref, nv_ref):
        gv = g_ref[...]
        nm = ADAM_B1 * m_ref[...] + (1.0 - ADAM_B1) * gv
        nv = ADAM_B2 * v_ref[...] + (1.0 - ADAM_B2) * (gv * gv)
        m_hat = nm / (1.0 - ADAM_B1 ** ADAM_STEP)
        v_hat = nv / (1.0 - ADAM_B2 ** ADAM_STEP)
        d_ref[...] = -ADAM_LR * (m_hat / (jnp.sqrt(v_hat) + ADAM_EPS) + ADAM_WD * w_ref[...])
        nm_ref[...] = nm
        nv_ref[...] = nv

    return pl.pallas_call(
        body, name=name, grid=(rows // tr,), in_specs=[blk] * 4, out_specs=[blk] * 3,
        out_shape=[jax.ShapeDtypeStruct((rows, cols), F32)] * 3, compiler_params=_params(("parallel",)),
    )(w, g, m, v)


ANY = pl.BlockSpec(memory_space=pl.ANY)


def _place():
    x, y, c = lax.axis_index("x"), lax.axis_index("y"), lax.axis_index("c")
    others = [(1 - x, y), (x, 1 - y), (1 - x, 1 - y)]
    return x, y, c, 2 * x + y, others


def _half(ref, shape, h):
    if len(shape) == 3:
        return ref.at[h]
    return ref.at[pl.ds(h * (shape[0] // 2), shape[0] // 2), :]


def _allgather_weights(shards):
    n_u = len(GATHER_UNITS)

    def body(*refs):
        ins, outs = refs[:n_u], refs[n_u:2 * n_u]
        send_sems, recv_sems, own_send_sems, own_recv_sems = refs[2 * n_u:]
        x, y, c, me, others = _place()
        sibling = (x, y, 1 - c)

        def copy(u, k, chip, h, to, src=None):
            dst = _half(outs[u].at[chip], GATHER_UNITS[u][1], h)
            return pltpu.make_async_remote_copy(
                src_ref=dst if src is None else src, dst_ref=dst, send_sem=send_sems.at[6 * u + k],
                recv_sem=recv_sems.at[6 * u + k], device_id=to, device_id_type=MESH)

        mine = [pltpu.make_async_remote_copy(
            src_ref=ins[u], dst_ref=outs[u].at[me], send_sem=own_send_sems.at[u], recv_sem=own_recv_sems.at[u],
            device_id=sibling, device_id_type=MESH) for u in range(n_u)]
        for cp in mine:
            cp.start()
        first = [copy(u, j, me, c, (ox, oy, c), src=_half(ins[u], GATHER_UNITS[u][1], c))
                 for u in range(n_u) for j, (ox, oy) in enumerate(others)]
        for cp in first:
            cp.start()
        passed = []
        for u in range(n_u):
            for j, (ox, oy) in enumerate(others):
                copy(u, j, 2 * ox + oy, c, (x, y, c)).wait_recv()
                passed.append(copy(u, 3 + j, 2 * ox + oy, c, sibling))
                passed[-1].start()
        for u in range(n_u):
            for j, (ox, oy) in enumerate(others):
                copy(u, 3 + j, 2 * ox + oy, 1 - c, (x, y, c)).wait_recv()
        for cp in first + passed:
            cp.wait_send()
        for cp in mine:
            cp.wait()

    return pl.pallas_call(
        body, name="allgather_weights", in_specs=[ANY] * n_u, out_specs=[ANY] * n_u,
        out_shape=[jax.ShapeDtypeStruct((N_CHIPS,) + shape, BF16) for _, shape in GATHER_UNITS],
        scratch_shapes=[pltpu.SemaphoreType.DMA((6 * n_u,)), pltpu.SemaphoreType.DMA((6 * n_u,)),
                        pltpu.SemaphoreType.DMA((n_u,)), pltpu.SemaphoreType.DMA((n_u,))],
    )(*shards)


def _sibling_exchange(slabs):
    n_u = len(GRAD_UNITS)

    def body(*refs):
        ins, got = refs[:n_u], refs[n_u:2 * n_u]
        send_sems, recv_sems = refs[2 * n_u:]
        x, y, c, _, _ = _place()
        copies = []
        for u, (_, (rows, _), _) in enumerate(GRAD_UNITS):
            half = rows // 2
            copies.append(pltpu.make_async_remote_copy(
                src_ref=ins[u].at[:, pl.ds((1 - c) * half, half), :], dst_ref=got[u], send_sem=send_sems.at[u],
                recv_sem=recv_sems.at[u], device_id=(x, y, 1 - c), device_id_type=MESH))
        for cp in copies:
            cp.start()
        for cp in copies:
            cp.wait()

    return pl.pallas_call(
        body, name="grad_sibling_exchange", in_specs=[ANY] * n_u, out_specs=[ANY] * n_u,
        out_shape=[jax.ShapeDtypeStruct((N_CHIPS, rows // 2, cols), F32) for _, (rows, cols), _ in GRAD_UNITS],
        scratch_shapes=[pltpu.SemaphoreType.DMA((n_u,)), pltpu.SemaphoreType.DMA((n_u,))],
    )(*slabs)


def _chip_exchange(pairs, small):
    n_u = len(GRAD_UNITS)

    def body(*refs):
        ins, s_ref = refs[:n_u], refs[n_u]
        outs, sall_ref = refs[n_u + 1:2 * n_u + 1], refs[2 * n_u + 1]
        send_sems, recv_sems, ssend_sems, srecv_sems, local_sem = refs[2 * n_u + 2:]
        x, y, c, me, others = _place()
        dev = 4 * x + 2 * y + c
        local = [pltpu.make_async_copy(s_ref, sall_ref.at[dev], local_sem)]
        sends = [pltpu.make_async_remote_copy(
            src_ref=ins[u].at[2 * ox + oy], dst_ref=outs[u].at[me], send_sem=send_sems.at[3 * u + j],
            recv_sem=recv_sems.at[3 * u + j], device_id=(ox, oy, c), device_id_type=MESH)
            for u in range(n_u) for j, (ox, oy) in enumerate(others)]
        flips = [(fx, fy, fc) for fx in (0, 1) for fy in (0, 1) for fc in (0, 1)][1:]
        sends += [pltpu.make_async_remote_copy(
            src_ref=s_ref, dst_ref=sall_ref.at[dev], send_sem=ssend_sems.at[n], recv_sem=srecv_sems.at[n],
            device_id=(x ^ fx, y ^ fy, c ^ fc), device_id_type=MESH) for n, (fx, fy, fc) in enumerate(flips)]
        for cp in local + sends:
            cp.start()
        for u in range(n_u):
            for j, (ox, oy) in enumerate(others):
                pltpu.make_async_remote_copy(
                    src_ref=ins[u].at[me], dst_ref=outs[u].at[2 * ox + oy], send_sem=send_sems.at[3 * u + j],
                    recv_sem=recv_sems.at[3 * u + j], device_id=(x, y, c), device_id_type=MESH).wait_recv()
        for n, (fx, fy, fc) in enumerate(flips):
            src_dev = 4 * (x ^ fx) + 2 * (y ^ fy) + (c ^ fc)
            pltpu.make_async_remote_copy(
                src_ref=s_ref, dst_ref=sall_ref.at[src_dev], send_sem=ssend_sems.at[n], recv_sem=srecv_sems.at[n],
                device_id=(x, y, c), device_id_type=MESH).wait_recv()
        for cp in sends:
            cp.wait_send()
        for cp in local:
            cp.wait()

    outs = pl.pallas_call(
        body, name="grad_chip_exchange", in_specs=[ANY] * (n_u + 1), out_specs=[ANY] * (n_u + 1),
        out_shape=[jax.ShapeDtypeStruct(p.shape, p.dtype) for p in pairs]
        + [jax.ShapeDtypeStruct((8, SMALL_ALL_ROWS, LANES), F32)],
        scratch_shapes=[pltpu.SemaphoreType.DMA((3 * n_u,)), pltpu.SemaphoreType.DMA((3 * n_u,)),
                        pltpu.SemaphoreType.DMA((7,)), pltpu.SemaphoreType.DMA((7,)), pltpu.SemaphoreType.DMA],
    )(*pairs, small)
    return outs[:n_u], outs[n_u]


def _sibling_share(grads):
    n_u = len(GRAD_UNITS)

    def body(*refs):
        ins, outs = refs[:n_u], refs[n_u:2 * n_u]
        send_sems, recv_sems = refs[2 * n_u:]
        x, y, c, _, _ = _place()

        def rows_of(ref, u, h):
            half = GRAD_UNITS[u][1][0] // 2
            return ref.at[pl.ds(h * half, half), :]

        sends = [pltpu.make_async_remote_copy(
            src_ref=rows_of(ins[u], u, c), dst_ref=rows_of(outs[u], u, c), send_sem=send_sems.at[u],
            recv_sem=recv_sems.at[u], device_id=(x, y, 1 - c), device_id_type=MESH) for u in range(n_u)]
        for cp in sends:
            cp.start()
        for u in range(n_u):
            pltpu.make_async_remote_copy(
                src_ref=rows_of(ins[u], u, c), dst_ref=rows_of(outs[u], u, 1 - c), send_sem=send_sems.at[u],
                recv_sem=recv_sems.at[u], device_id=(x, y, c), device_id_type=MESH).wait_recv()
        for cp in sends:
            cp.wait_send()

    return pl.pallas_call(
        body, name="grad_sibling_share", in_specs=[ANY] * n_u, out_specs=[ANY] * n_u,
        out_shape=[jax.ShapeDtypeStruct(g.shape, F32) for g in grads],
        input_output_aliases={u: u for u in range(n_u)},
        scratch_shapes=[pltpu.SemaphoreType.DMA((n_u,)), pltpu.SemaphoreType.DMA((n_u,))],
    )(*grads)


def kernel(x, g_mix_pre, g_mix_post, g_ffn_pre, g_ffn_post, conv_pw1_w, conv_pw1_b, conv_dw_w, conv_dw_b, conv_ln_g, conv_ln_b, conv_pw2_w, conv_pw2_b, attn_w_in, attn_b_f, attn_w_o, mlp_w_up, mlp_w_down, loss_target, m_g_mix_pre, m_g_mix_post, m_g_ffn_pre, m_g_ffn_post, m_conv_pw1_w, m_conv_pw1_b, m_conv_dw_w, m_conv_dw_b, m_conv_ln_g, m_conv_ln_b, m_conv_pw2_w, m_conv_pw2_b, m_attn_w_in, m_attn_b_f, m_attn_w_o, m_mlp_w_up, m_mlp_w_down, v_g_mix_pre, v_g_mix_post, v_g_ffn_pre, v_g_ffn_post, v_conv_pw1_w, v_conv_pw1_b, v_conv_dw_w, v_conv_dw_b, v_conv_ln_g, v_conv_ln_b, v_conv_pw2_w, v_conv_pw2_b, v_attn_w_in, v_attn_b_f, v_attn_w_o, v_mlp_w_up, v_mlp_w_down):
    weights = dict(g_mix_pre=g_mix_pre, g_mix_post=g_mix_post, g_ffn_pre=g_ffn_pre, g_ffn_post=g_ffn_post, conv_pw1_w=conv_pw1_w, conv_pw1_b=conv_pw1_b, conv_dw_w=conv_dw_w, conv_dw_b=conv_dw_b, conv_ln_g=conv_ln_g, conv_ln_b=conv_ln_b, conv_pw2_w=conv_pw2_w, conv_pw2_b=conv_pw2_b, attn_w_in=attn_w_in, attn_b_f=attn_b_f, attn_w_o=attn_w_o, mlp_w_up=mlp_w_up, mlp_w_down=mlp_w_down)
    mom_m = dict(g_mix_pre=m_g_mix_pre, g_mix_post=m_g_mix_post, g_ffn_pre=m_g_ffn_pre, g_ffn_post=m_g_ffn_post, conv_pw1_w=m_conv_pw1_w, conv_pw1_b=m_conv_pw1_b, conv_dw_w=m_conv_dw_w, conv_dw_b=m_conv_dw_b, conv_ln_g=m_conv_ln_g, conv_ln_b=m_conv_ln_b, conv_pw2_w=m_conv_pw2_w, conv_pw2_b=m_conv_pw2_b, attn_w_in=m_attn_w_in, attn_b_f=m_attn_b_f, attn_w_o=m_attn_w_o, mlp_w_up=m_mlp_w_up, mlp_w_down=m_mlp_w_down)
    mom_v = dict(g_mix_pre=v_g_mix_pre, g_mix_post=v_g_mix_post, g_ffn_pre=v_g_ffn_pre, g_ffn_post=v_g_ffn_post, conv_pw1_w=v_conv_pw1_w, conv_pw1_b=v_conv_pw1_b, conv_dw_w=v_conv_dw_w, conv_dw_b=v_conv_dw_b, conv_ln_g=v_conv_ln_g, conv_ln_b=v_conv_ln_b, conv_pw2_w=v_conv_pw2_w, conv_pw2_b=v_conv_pw2_b, attn_w_in=v_attn_w_in, attn_b_f=v_attn_b_f, attn_w_o=v_attn_w_o, mlp_w_up=v_mlp_w_up, mlp_w_down=v_mlp_w_down)
    order = [n for n, _ in SMALL[:4]] + ["conv_pw1_w", "conv_pw1_b", "conv_dw_w", "conv_dw_b", "conv_ln_g", "conv_ln_b",
                                          "conv_pw2_w", "conv_pw2_b", "attn_w_in", "attn_b_f", "attn_w_o", "mlp_w_up",
                                          "mlp_w_down"]
    T = x.shape[1]
    x0 = x.reshape(T, D)
    target = loss_target.reshape(T, D)
    row = lambda a, l: a[l:l + 1, :]

    shards = []
    for name, shape in GATHER_UNITS:
        w = weights[name].astype(BF16)
        w = w.reshape(w.shape[-len(shape):])
        if name == "conv_dw_w":
            w = jnp.pad(w, ((0, HALO - CONV_W), (0, 0)))
        shards.append(w)
    g_pw1, g_dw, g_pw2, g_win, g_wo, g_up, g_down = _allgather_weights(shards)
    w_pw2, w_o = g_pw2.reshape(D, D), g_wo.reshape(D, D)
    w_in = jnp.pad(g_win.transpose(1, 0, 2).reshape(D, W_IN_COLS), ((0, 0), (0, W_IN_PAD - W_IN_COLS)))
    w_dw = g_dw.transpose(1, 0, 2).reshape(HALO, D).astype(F32)
    b_f = jnp.pad(attn_b_f, ((0, 0), (0, LANES - N_HEADS)))

    def mlp_fwd(h, l):
        (act,) = _matmul(h, g_up, "nn", (BF16,), _epi_sqrelu, f"mlp{l}_up", n=DFF, b_place=_place_cols_nn(D, l))
        (mo,) = _matmul(act, g_down, "nn", (F32,), _epi_plain, f"mlp{l}_down", n=D, b_place=_place_rows_nn(D, l))
        return act, mo

    (h0,) = _resnorm_fwd(x0, None, None, row(g_mix_pre, 0), "norm_in")
    (a0,) = _matmul(h0, g_pw1, "nn", (F32,), _epi_bias, "conv_pw1", extras=((conv_pw1_b, "row"),), tn=512, n=2 * D,
                    b_place=_place_cols_nn(512))
    y0, z0 = _conv_fwd(a0, w_dw, conv_dw_b, conv_ln_g, conv_ln_b, "conv_fwd")
    (m0,) = _matmul(z0, w_pw2, "nn", (F32,), _epi_bias, "conv_pw2", extras=((conv_pw2_b, "row"),))
    x1, h1 = _resnorm_fwd(x0, m0, row(g_mix_post, 0), row(g_ffn_pre, 0), "norm_conv_out")
    act0, mo0 = mlp_fwd(h1, 0)
    x2, h2 = _resnorm_fwd(x1, mo0, row(g_ffn_post, 0), row(g_mix_pre, 1), "norm_mlp0_out")

    (proj,) = _matmul(h2, w_in, "nn", (F32,), _epi_plain, "attn_in", tn=640)
    fl = proj[:, 3 * D:]
    f_all, f1, f2, f3 = _gate_fwd(fl, b_f, "gate_fwd")
    q_aug, k_aug, v_aug, qn2, kn2 = _attn_pack(proj, f1, f2, f3, "attn_pack")
    tb = min(ATT_TB, T)
    qk = 2.02 * jnp.sqrt(qn2[:, 0, 0] * kn2[:, 0, 0]) + 1.0
    fs, fe = f_all[0::tb, :N_HEADS].T, f_all[tb - 1::tb, :N_HEADS].T
    o_aug, q2_aug = _flash_fwd(qk, fs, fe, jnp.sqrt(kn2[:, 0, 0]), q_aug, k_aug, v_aug, "flash_fwd")
    (o_bf,) = _merge_heads(o_aug, 1.0, BF16, "attn_merge_o")
    (m1,) = _matmul(o_bf, w_o, "nn", (F32,), _epi_plain, "attn_out")
    x3, h3 = _resnorm_fwd(x2, m1, row(g_mix_post, 1), row(g_ffn_pre, 1), "norm_attn_out")
    act1, mo1 = mlp_fwd(h3, 1)
    dy, loss_part = _resnorm_loss(x3, mo1, row(g_ffn_post, 1), target, "loss")
    loss = lax.psum(loss_part[0, 0], ("x", "y", "c"))

    G = {}

    def mlp_bwd(d_out, mo, act, h, x_in, l):
        d_mo, dg_post, _ = _norm_bwd(mo, row(g_ffn_post, l), d_out, None, BF16, f"mlp{l}_post_bwd")
        (d_up,) = _matmul(d_mo, g_down, "nt", (BF16,), _epi_dsqrelu, f"mlp{l}_dact", extras=((act, "tile"),), n=DFF,
                          b_place=_place_rows_nt(D, l))
        (dw_down,) = _matmul(act, d_mo, "tn", (F32,), _epi_plain, tk=2048, name=f"mlp{l}_dwdown")
        (dw_up,) = _matmul(h, d_up, "tn", (F32,), _epi_plain, tk=2048, name=f"mlp{l}_dwup", shard_out=True)
        (d_h,) = _matmul(d_up, g_up, "nt", (F32,), _epi_plain, f"mlp{l}_dh", n=D, b_place=_place_cols_nt(D, l))
        d_in, dg_pre, _ = _norm_bwd(x_in, row(g_ffn_pre, l), d_h, d_out, F32, f"mlp{l}_pre_bwd")
        G[f"up{l}"], G[f"down{l}"] = dw_up, dw_down.reshape(N_CHIPS, D, D)
        return d_in, dg_post, dg_pre

    d_x3, dg_ffn_post1, dg_ffn_pre1 = mlp_bwd(dy, mo1, act1, h3, x3, 1)

    d_m1, dg_mix_post1, _ = _norm_bwd(m1, row(g_mix_post, 1), d_x3, None, BF16, "attn_post_bwd")
    (dw_o,) = _matmul(o_bf, d_m1, "tn", (F32,), _epi_plain, tk=2048, name="attn_dwo")
    G["attn_w_o"] = dw_o.reshape(N_CHIPS, D // N_CHIPS, D)
    (d_o,) = _matmul(d_m1, w_o, "nt", (F32,), _epi_plain, "attn_do")
    do_aug = _attn_prep(d_o, o_aug, "attn_prep")
    dq_aug = _flash_dq(qk, fs, fe, q2_aug, k_aug, do_aug, v_aug, "flash_dq")
    dk_aug, dv_aug = _flash_dkv(qk, fs, fe, q2_aug, k_aug, do_aug, v_aug, "flash_dkv")
    d_q, dF_q = _merge_heads(dq_aug, 0.125, BF16, "attn_merge_dq", column=AUG_F)
    d_k, dF_k = _merge_heads(dk_aug, 1.0, BF16, "attn_merge_dk", column=AUG_ONE)
    (d_v,) = _merge_heads(dv_aug, 1.0, BF16, "attn_merge_dv")
    d_fl, db_f = _gate_bwd(dF_q, dF_k, fl, b_f, "gate_bwd")
    d_proj = jnp.concatenate([d_q, d_k, d_v, d_fl.astype(BF16)], axis=1)
    (dw_in,) = _matmul(h2, d_proj, "tn", (F32,), _epi_plain, tk=2048, name="attn_dwin", tn=640)
    G["attn_w_in"] = dw_in[:, :W_IN_COLS].reshape(D, N_CHIPS, W_IN_COLS // N_CHIPS).transpose(1, 0, 2)
    (d_h2,) = _matmul(d_proj, w_in, "nt", (F32,), _epi_plain, "attn_dh", tk=W_IN_PAD)
    d_x2, dg_mix_pre1, _ = _norm_bwd(x2, row(g_mix_pre, 1), d_h2, d_x3, F32, "attn_pre_bwd")

    d_x1, dg_ffn_post0, dg_ffn_pre0 = mlp_bwd(d_x2, mo0, act0, h1, x1, 0)

    d_m0, dg_mix_post0, db_pw2 = _norm_bwd(m0, row(g_mix_post, 0), d_x1, None, BF16, "conv_post_bwd")
    (dw_pw2,) = _matmul(z0, d_m0, "tn", (F32,), _epi_plain, tk=2048, name="conv_dwpw2")
    G["conv_pw2_w"] = dw_pw2.reshape(N_CHIPS, D // N_CHIPS, D)
    (d_z0,) = _matmul(d_m0, w_pw2, "nt", (F32,), _epi_plain, "conv_dz")
    d_y0, dln_g, dln_b, ddw_b = _conv_bwd_ln(d_z0, y0, conv_ln_g, conv_ln_b, "conv_bwd_ln")
    d_a0, db_pw1, ddw_w = _conv_bwd_dw(d_y0, a0, w_dw, "conv_bwd_dw")
    (G["conv_pw1_w"],) = _matmul(h0, d_a0, "tn", (F32,), _epi_plain, tk=2048, name="conv_dwpw1", tn=512, shard_out=True)
    (d_h0,) = _matmul(d_a0, g_pw1, "nt", (F32,), _epi_plain, "conv_dh", tk=512, n=D, b_place=_place_cols_nt(512))
    d_x0, dg_mix_pre0, _ = _norm_bwd(x0, row(g_mix_pre, 0), d_h0, d_x1, F32, "conv_pre_bwd")

    small_local = {
        "g_mix_pre": jnp.concatenate([dg_mix_pre0, dg_mix_pre1]), "g_mix_post": jnp.concatenate([dg_mix_post0, dg_mix_post1]),
        "g_ffn_pre": jnp.concatenate([dg_ffn_pre0, dg_ffn_pre1]), "g_ffn_post": jnp.concatenate([dg_ffn_post0, dg_ffn_post1]),
        "conv_pw1_b": db_pw1, "conv_dw_b": ddw_b, "conv_ln_g": dln_g, "conv_ln_b": dln_b, "conv_pw2_b": db_pw2,
        "attn_b_f": db_f[:, :N_HEADS], "conv_dw_w_full": ddw_w,
    }

    chip = 2 * lax.axis_index("x") + lax.axis_index("y")
    place = jnp.stack([chip] + [k + (k >= chip).astype(jnp.int32) for k in range(N_CHIPS - 1)]
                      + [lax.axis_index("c")]).astype(jnp.int32)
    slabs = [G[n] for n, _, _ in GRAD_UNITS]
    got = _sibling_exchange(slabs)
    pairs = [_add_pair(place, s, g, f"grad_pair_{n}") for s, g, (n, _, _) in zip(slabs, got, GRAD_UNITS)]
    by_chip, small_all = _chip_exchange(pairs, _pack([small_local[n] for n, _ in SMALL_ALL], SMALL_ALL_ROWS))
    unit_grads = _sibling_share([_add_chips(place, p, b, f"grad_chips_{n}")
                                 for p, b, (n, _, _) in zip(pairs, by_chip, GRAD_UNITS)])
    grads = _unpack(_add_slots(small_all, "grad_add_small", tb=SMALL_ALL_ROWS), SMALL_ALL)
    grads["conv_dw_w"] = lax.dynamic_slice(grads.pop("conv_dw_w_full"), (0, chip * 256), (CONV_W, 256))[None]

    delta, new_m, new_v = {}, {}, {}
    per_layer = {}
    for g, (unit, shape, (name, layer)) in zip(unit_grads, GRAD_UNITS):
        view = lambda d: (d[name] if layer is None else d[name][layer]).reshape(shape)
        per_layer.setdefault(name, []).append((g,) + tuple(_adamw(view(weights), g, view(mom_m), view(mom_v),
                                                                  f"adamw_{unit}")))
    for name, parts in per_layer.items():
        full = weights[name].shape
        join = lambda n: (parts[0][n] if len(parts) == 1 else jnp.stack([p[n] for p in parts])).reshape(full)
        grads[name], delta[name], new_m[name], new_v[name] = join(0), join(1), join(2), join(3)
    dw2 = lambda d: d["conv_dw_w"].reshape(CONV_W, 256)
    d, nm, nv = _adamw(dw2(weights), dw2(grads), dw2(mom_m), dw2(mom_v), "adamw_conv_dw_w")
    full = weights["conv_dw_w"].shape
    delta["conv_dw_w"], new_m["conv_dw_w"], new_v["conv_dw_w"] = d.reshape(full), nm.reshape(full), nv.reshape(full)
    packs = [_pack([src[n] for n, _ in SMALL], SMALL_ROWS) for src in (weights, grads, mom_m, mom_v)]
    d, nm, nv = _adamw(*packs, "adamw_small")
    for dst, packed in ((delta, d), (new_m, nm), (new_v, nv)):
        dst.update(_unpack(packed, SMALL))

    grad_x = d_x0.reshape(x.shape)
    return (loss, grad_x, *[grads[n] for n in order], *[delta[n] for n in order], *[new_m[n] for n in order],
            *[new_v[n] for n in order])
```

```python
import functools

import jax
import jax.numpy as jnp
from jax import lax
from jax.experimental import pallas as pl
from jax.experimental.pallas import tpu as pltpu

F32 = jnp.float32
BF16 = jnp.bfloat16
MESH = pl.DeviceIdType.MESH

D = 1024
DFF = 4096
N_HEADS = 16
HEAD_DIM = 64
CONV_W = 31
HALO = 32
RMS_EPS = 1e-6
LN_EPS = 1e-5
MASK_VALUE = -1e30
W_IN_COLS = 3 * D + N_HEADS
W_IN_PAD = 3200
LANES = 128
SUBLANES = 8
N_CHIPS = 4
VMEM_LIMIT = 56 * 1024 * 1024

ADAM_LR = 0.001
ADAM_B1 = 0.9
ADAM_B2 = 0.999
ADAM_EPS = 1e-08
ADAM_WD = 0.01
ADAM_STEP = 10

AUG_F = 64
AUG_ONE = 67
AUG_L = 70
SKIP_BELOW = -104.0
FIXED_MAX_BELOW = 40.0

GATHER_UNITS = (
    ("conv_pw1_w", (1024, 512)),
    ("conv_dw_w", (HALO, 256)),
    ("conv_pw2_w", (256, 1024)),
    ("attn_w_in", (1024, 772)),
    ("attn_w_o", (256, 1024)),
    ("mlp_w_up", (2, 1024, 1024)),
    ("mlp_w_down", (2, 1024, 1024)),
)
GRAD_UNITS = (
    ("conv_pw1_w", (1024, 512), ("conv_pw1_w", None)),
    ("conv_pw2_w", (256, 1024), ("conv_pw2_w", None)),
    ("attn_w_in", (1024, 772), ("attn_w_in", None)),
    ("attn_w_o", (256, 1024), ("attn_w_o", None)),
    ("up0", (1024, 1024), ("mlp_w_up", 0)),
    ("up1", (1024, 1024), ("mlp_w_up", 1)),
    ("down0", (1024, 1024), ("mlp_w_down", 0)),
    ("down1", (1024, 1024), ("mlp_w_down", 1)),
)
SMALL = (
    ("g_mix_pre", (2, 1024)), ("g_mix_post", (2, 1024)), ("g_ffn_pre", (2, 1024)), ("g_ffn_post", (2, 1024)),
    ("conv_pw1_b", (1, 2048)), ("conv_dw_b", (1, 1024)), ("conv_ln_g", (1, 1024)), ("conv_ln_b", (1, 1024)),
    ("conv_pw2_b", (1, 1024)), ("attn_b_f", (1, 16)),
)
SMALL_ROWS = 120
SMALL_ALL = SMALL + (("conv_dw_w_full", (HALO, 1024)),)
SMALL_ALL_ROWS = 376


def _size(shape):
    n = 1
    for s in shape:
        n *= s
    return n


def _pack(arrays, rows):
    flat = jnp.concatenate([a.reshape(-1) for a in arrays])
    return jnp.pad(flat, (0, rows * LANES - flat.shape[0])).reshape(rows, LANES)


def _unpack(packed, table):
    flat = packed.reshape(-1)
    out, off = {}, 0
    for name, shape in table:
        n = _size(shape)
        out[name] = flat[off:off + n].reshape(shape)
        off += n
    return out


def _params(sem):
    return pltpu.CompilerParams(dimension_semantics=sem, vmem_limit_bytes=VMEM_LIMIT)


def _rms(x, g):
    return x * lax.rsqrt(jnp.mean(x * x, axis=-1, keepdims=True) + RMS_EPS) * g


def _split3(v):
    p1 = v.astype(BF16).astype(F32)
    r = v - p1
    p2 = r.astype(BF16).astype(F32)
    p3 = (r - p2).astype(BF16).astype(F32)
    return p1, p2, p3


def _matmul(a, b, mode, out_dtypes, epi, name, extras=(), tm=1024, tn=1024, tk=1024, shard_out=False):
    (K, M) = a.shape if mode == "tn" else a.shape[::-1]
    N = b.shape[0] if mode == "nt" else b.shape[1]
    tm, tn, tk = min(tm, M), min(tn, N), min(tk, K)
    nk = K // tk
    if mode == "tn":
        a_spec = pl.BlockSpec((tk, tm), lambda i, j, k: (k, i))
    else:
        a_spec = pl.BlockSpec((tm, tk), lambda i, j, k: (i, k))
    if mode == "nt":
        b_spec = pl.BlockSpec((tn, tk), lambda i, j, k: (j, k))
    else:
        b_spec = pl.BlockSpec((tk, tn), lambda i, j, k: (k, j))
    if shard_out:
        per = N // N_CHIPS // tn
        o_spec = pl.BlockSpec((None, tm, tn), lambda i, j, k: (j // per, i, j % per))
        o_shape = (N_CHIPS, M, N // N_CHIPS)
    else:
        o_spec = pl.BlockSpec((tm, tn), lambda i, j, k: (i, j))
        o_shape = (M, N)
    dims = {"nn": (((1,), (0,)), ((), ())), "nt": (((1,), (1,)), ((), ())), "tn": (((0,), (0,)), ((), ()))}[mode]
    ex_specs = []
    for _, kind in extras:
        if kind == "row":
            ex_specs.append(pl.BlockSpec((1, tn), lambda i, j, k: (0, j)))
        else:
            ex_specs.append(pl.BlockSpec((tm, tn), lambda i, j, k: (i, j)))
    n_ex, n_out = len(extras), len(out_dtypes)

    def body(*refs):
        a_ref, b_ref = refs[0], refs[1]
        ex = refs[2:2 + n_ex]
        outs = refs[2 + n_ex:2 + n_ex + n_out]
        prod = lax.dot_general(a_ref[...], b_ref[...], dims, preferred_element_type=F32)

        def finish(acc):
            res = epi(acc, *[e[...] for e in ex])
            for o, r in zip(outs, res):
                o[...] = r.astype(o.dtype)

        if nk == 1:
            finish(prod)
        else:
            acc_ref = refs[-1]
            k = pl.program_id(2)

            @pl.when(k == 0)
            def _():
                acc_ref[...] = prod

            @pl.when(k > 0)
            def _():
                acc_ref[...] += prod

            @pl.when(k == nk - 1)
            def _():
                finish(acc_ref[...])

    return pl.pallas_call(
        body,
        name=name,
        grid=(M // tm, N // tn, nk),
        in_specs=[a_spec, b_spec] + ex_specs,
        out_specs=[o_spec for _ in out_dtypes],
        out_shape=[jax.ShapeDtypeStruct(o_shape, dt) for dt in out_dtypes],
        scratch_shapes=[pltpu.VMEM((tm, tn), F32)] if nk > 1 else [],
        compiler_params=_params(("parallel", "parallel", "arbitrary")),
    )(a, b, *[e for e, _ in extras])


def _epi_plain(acc):
    return (acc,)


def _epi_bias(acc, bias):
    return (acc + bias,)


def _epi_sqrelu(acc):
    r = jnp.maximum(acc, 0.0)
    return (r * r,)


def _epi_dsqrelu(acc, act):
    return (acc * (2.0 * jnp.sqrt(act.astype(F32))),)


def _resnorm_fwd(x, m, g_post, g_next, name, tr=512):
    T = x.shape[0]
    has_m = m is not None
    row = pl.BlockSpec((tr, D), lambda i: (i, 0))
    vec = pl.BlockSpec((1, D), lambda i: (0, 0))

    def body(*refs):
        if has_m:
            x_ref, m_ref, gp_ref, gn_ref, x1_ref, h_ref = refs
            x1 = x_ref[...] + _rms(m_ref[...], gp_ref[...])
            x1_ref[...] = x1
        else:
            x_ref, gn_ref, h_ref = refs
            x1 = x_ref[...]
        h_ref[...] = _rms(x1, gn_ref[...]).astype(BF16)

    if has_m:
        args, in_specs = (x, m, g_post, g_next), [row, row, vec, vec]
        out_specs = [row, row]
        out_shape = [jax.ShapeDtypeStruct((T, D), F32), jax.ShapeDtypeStruct((T, D), BF16)]
    else:
        args, in_specs = (x, g_next), [row, vec]
        out_specs = [row]
        out_shape = [jax.ShapeDtypeStruct((T, D), BF16)]
    return pl.pallas_call(
        body, name=name, grid=(T // tr,), in_specs=in_specs, out_specs=out_specs, out_shape=out_shape,
        compiler_params=_params(("parallel",)),
    )(*args)


def _resnorm_loss(x, m, g_post, target, name, tr=512):
    T = x.shape[0]
    row = pl.BlockSpec((tr, D), lambda i: (i, 0))
    vec = pl.BlockSpec((1, D), lambda i: (0, 0))

    def body(x_ref, m_ref, gp_ref, t_ref, dy_ref, loss_ref):
        diff = x_ref[...] + _rms(m_ref[...], gp_ref[...]) - t_ref[...]
        dy_ref[...] = diff * (1.0 / D)

        @pl.when(pl.program_id(0) == 0)
        def _():
            loss_ref[...] = jnp.zeros_like(loss_ref)

        per_token = jnp.mean(diff * diff, axis=-1, keepdims=True)
        loss_ref[...] += 0.5 * jnp.sum(per_token)

    return pl.pallas_call(
        body, name=name, grid=(T // tr,), in_specs=[row, row, vec, row],
        out_specs=[row, pl.BlockSpec((8, LANES), lambda i: (0, 0))],
        out_shape=[jax.ShapeDtypeStruct((T, D), F32), jax.ShapeDtypeStruct((8, LANES), F32)],
        compiler_params=_params(("arbitrary",)),
    )(x, m, g_post, target)


def _norm_bwd(x, g, dy, resid, out_dtype, name, tr=512):
    T = x.shape[0]
    has_r = resid is not None
    row = pl.BlockSpec((tr, D), lambda i: (i, 0))
    vec = pl.BlockSpec((1, D), lambda i: (0, 0))

    def body(*refs):
        if has_r:
            x_ref, g_ref, dy_ref, r_ref, dx_ref, dg_ref, cs_ref = refs
        else:
            x_ref, g_ref, dy_ref, dx_ref, dg_ref, cs_ref = refs
        xv, dyv = x_ref[...], dy_ref[...]
        r = lax.rsqrt(jnp.mean(xv * xv, axis=-1, keepdims=True) + RMS_EPS)
        gy = dyv * g_ref[...]
        c = jnp.sum(gy * xv, axis=-1, keepdims=True) * (1.0 / D)
        dx = r * gy - xv * (r * r * r * c)

        @pl.when(pl.program_id(0) == 0)
        def _():
            dg_ref[...] = jnp.zeros_like(dg_ref)
            cs_ref[...] = jnp.zeros_like(cs_ref)

        dg_ref[...] += jnp.sum(dyv * (xv * r), axis=0, keepdims=True)
        cs_ref[...] += jnp.sum(dx, axis=0, keepdims=True)
        if has_r:
            dx = dx + r_ref[...]
        dx_ref[...] = dx.astype(dx_ref.dtype)

    args = (x, g, dy) + ((resid,) if has_r else ())
    return pl.pallas_call(
        body, name=name, grid=(T // tr,), in_specs=[row, vec, row] + ([row] if has_r else []),
        out_specs=[row, vec, vec],
        out_shape=[jax.ShapeDtypeStruct((T, D), out_dtype), jax.ShapeDtypeStruct((1, D), F32),
                   jax.ShapeDtypeStruct((1, D), F32)],
        compiler_params=_params(("arbitrary",)),
    )(*args)


CONV_TB = 256
CONV_RC = 32


def _glu(a):
    return a[:, :D] * jax.nn.sigmoid(a[:, D:])


def _shifted_copies(win):
    rows = win.shape[1]
    for p in range(1, SUBLANES):
        win[p, 0:rows - SUBLANES, :] = win[0, p:p + rows - SUBLANES, :]


def _window(win, start):
    p = start % SUBLANES
    return win[p, start - p:start - p + CONV_RC, :]


def _conv_fwd(a, dw_w, dw_b, ln_g, ln_b, name):
    T = a.shape[0]
    tb = min(CONV_TB, T)
    per = tb // HALO
    vec = pl.BlockSpec((1, D), lambda i: (0, 0))

    def body(cur_ref, prev_ref, w_ref, b_ref, lg_ref, lb_ref, y_ref, z_ref, uwin):
        i = pl.program_id(0)
        uwin[0, 0:HALO, :] = jnp.where(i > 0, _glu(prev_ref[...]), 0.0)
        uwin[0, HALO:, :] = _glu(cur_ref[...])
        _shifted_copies(uwin)
        for r in range(tb // CONV_RC):
            acc = jnp.broadcast_to(b_ref[...], (CONV_RC, D))
            for k in range(CONV_W):
                start = r * CONV_RC + HALO - (CONV_W - 1) + k
                acc = acc + _window(uwin, start) * w_ref[k:k + 1, :]
            rows = slice(r * CONV_RC, (r + 1) * CONV_RC)
            y_ref[rows, :] = acc
            mu = jnp.mean(acc, axis=-1, keepdims=True)
            xc = acc - mu
            var = jnp.mean(xc * xc, axis=-1, keepdims=True)
            n = xc * lax.rsqrt(var + LN_EPS) * lg_ref[...] + lb_ref[...]
            z_ref[rows, :] = (n * jax.nn.sigmoid(n)).astype(BF16)

    return pl.pallas_call(
        body, name=name, grid=(T // tb,),
        in_specs=[pl.BlockSpec((tb, 2 * D), lambda i: (i, 0)),
                  pl.BlockSpec((HALO, 2 * D), lambda i: (jnp.maximum(i * per - 1, 0), 0)),
                  pl.BlockSpec((HALO, D), lambda i: (0, 0)), vec, vec, vec],
        out_specs=[pl.BlockSpec((tb, D), lambda i: (i, 0)), pl.BlockSpec((tb, D), lambda i: (i, 0))],
        out_shape=[jax.ShapeDtypeStruct((T, D), F32), jax.ShapeDtypeStruct((T, D), BF16)],
        scratch_shapes=[pltpu.VMEM((SUBLANES, tb + HALO, D), F32)],
        compiler_params=_params(("parallel",)),
    )(a, a, dw_w, dw_b, ln_g, ln_b)


def _conv_bwd_ln(dz, y, ln_g, ln_b, name, tr=256):
    T = y.shape[0]
    tr = min(tr, T)
    row = pl.BlockSpec((tr, D), lambda i: (i, 0))
    vec = pl.BlockSpec((1, D), lambda i: (0, 0))

    def body(dz_ref, y_ref, lg_ref, lb_ref, dy_ref, dlg_ref, dlb_ref, db_ref):
        yv = y_ref[...]
        mu = jnp.mean(yv, axis=-1, keepdims=True)
        xc = yv - mu
        rstd = lax.rsqrt(jnp.mean(xc * xc, axis=-1, keepdims=True) + LN_EPS)
        yh = xc * rstd
        n = yh * lg_ref[...] + lb_ref[...]
        sg = jax.nn.sigmoid(n)
        dn = dz_ref[...] * (sg * (1.0 + n * (1.0 - sg)))
        dyh = dn * lg_ref[...]
        dyv = rstd * (dyh - jnp.mean(dyh, axis=-1, keepdims=True) - yh * jnp.mean(dyh * yh, axis=-1, keepdims=True))
        dy_ref[...] = dyv

        @pl.when(pl.program_id(0) == 0)
        def _():
            dlg_ref[...] = jnp.zeros_like(dlg_ref)
            dlb_ref[...] = jnp.zeros_like(dlb_ref)
            db_ref[...] = jnp.zeros_like(db_ref)

        dlg_ref[...] += jnp.sum(dn * yh, axis=0, keepdims=True)
        dlb_ref[...] += jnp.sum(dn, axis=0, keepdims=True)
        db_ref[...] += jnp.sum(dyv, axis=0, keepdims=True)

    return pl.pallas_call(
        body, name=name, grid=(T // tr,), in_specs=[row, row, vec, vec], out_specs=[row, vec, vec, vec],
        out_shape=[jax.ShapeDtypeStruct((T, D), F32)] + [jax.ShapeDtypeStruct((1, D), F32)] * 3,
        compiler_params=_params(("arbitrary",)),
    )(dz, y, ln_g, ln_b)


def _conv_bwd_dw(dy, a, dw_w, name):
    T = a.shape[0]
    tb = min(CONV_TB, T)
    per = tb // HALO
    last_halo = T // HALO - 1
    n_steps = T // tb

    def body(dyc_ref, dyn_ref, cur_ref, prev_ref, w_ref, da_ref, dbias_ref, dw_ref, uwin, dywin, dwacc):
        i = pl.program_id(0)

        @pl.when(i == 0)
        def _():
            dbias_ref[...] = jnp.zeros_like(dbias_ref)
            dwacc[...] = jnp.zeros_like(dwacc)

        uwin[0, 0:HALO, :] = jnp.where(i > 0, _glu(prev_ref[...]), 0.0)
        uwin[0, HALO:, :] = _glu(cur_ref[...])
        dywin[0, 0:tb, :] = dyc_ref[...]
        dywin[0, tb:, :] = jnp.where(i < n_steps - 1, dyn_ref[...], 0.0)
        _shifted_copies(uwin)
        _shifted_copies(dywin)
        for r in range(tb // CONV_RC):
            rows = slice(r * CONV_RC, (r + 1) * CONV_RC)
            dy_c = dywin[0, rows, :]
            du = jnp.zeros((CONV_RC, D), F32)
            for k in range(CONV_W):
                du = du + _window(dywin, r * CONV_RC + (CONV_W - 1) - k) * w_ref[k:k + 1, :]
                prod = dy_c * _window(uwin, r * CONV_RC + HALO - (CONV_W - 1) + k)
                part = prod[0:8, :]
                for q in range(1, CONV_RC // 8):
                    part = part + prod[8 * q:8 * q + 8, :]
                dwacc[8 * k:8 * k + 8, :] += part
            av = cur_ref[rows, :]
            a1, sg = av[:, :D], jax.nn.sigmoid(av[:, D:])
            da1 = du * sg
            da2 = du * a1 * (sg * (1.0 - sg))
            da_ref[rows, 0:D] = da1.astype(BF16)
            da_ref[rows, D:] = da2.astype(BF16)
            dbias_ref[:, 0:D] += jnp.sum(da1, axis=0, keepdims=True)
            dbias_ref[:, D:] += jnp.sum(da2, axis=0, keepdims=True)

        @pl.when(i == n_steps - 1)
        def _():
            for k in range(CONV_W):
                dw_ref[k:k + 1, :] = jnp.sum(dwacc[8 * k:8 * k + 8, :], axis=0, keepdims=True)
            dw_ref[CONV_W:, :] = jnp.zeros((HALO - CONV_W, D), F32)

    return pl.pallas_call(
        body, name=name, grid=(n_steps,),
        in_specs=[pl.BlockSpec((tb, D), lambda i: (i, 0)),
                  pl.BlockSpec((HALO, D), lambda i: (jnp.minimum((i + 1) * per, last_halo), 0)),
                  pl.BlockSpec((tb, 2 * D), lambda i: (i, 0)),
                  pl.BlockSpec((HALO, 2 * D), lambda i: (jnp.maximum(i * per - 1, 0), 0)),
                  pl.BlockSpec((HALO, D), lambda i: (0, 0))],
        out_specs=[pl.BlockSpec((tb, 2 * D), lambda i: (i, 0)), pl.BlockSpec((1, 2 * D), lambda i: (0, 0)),
                   pl.BlockSpec((HALO, D), lambda i: (0, 0))],
        out_shape=[jax.ShapeDtypeStruct((T, 2 * D), BF16), jax.ShapeDtypeStruct((1, 2 * D), F32),
                   jax.ShapeDtypeStruct((HALO, D), F32)],
        scratch_shapes=[pltpu.VMEM((SUBLANES, tb + HALO, D), F32), pltpu.VMEM((SUBLANES, tb + HALO, D), F32),
                        pltpu.VMEM((8 * HALO, D), F32)],
        compiler_params=_params(("arbitrary",)),
    )(dy, dy, a, a, dw_w)


GATE_TB = 512


def _gate_fwd(fl, b_f, name):
    T = fl.shape[0]
    tb = min(GATE_TB, T)
    row = pl.BlockSpec((tb, LANES), lambda i: (i, 0))

    def body(fl_ref, b_ref, f_ref, f1_ref, f2_ref, f3_ref, carry):
        @pl.when(pl.program_id(0) == 0)
        def _():
            carry[...] = jnp.zeros_like(carry)

        z = fl_ref[...] + b_ref[...]
        lf = jnp.minimum(z, 0.0) - jnp.log(1.0 + jnp.exp(-jnp.abs(z)))
        tri = (lax.broadcasted_iota(jnp.int32, (tb, tb), 0) >= lax.broadcasted_iota(jnp.int32, (tb, tb), 1)).astype(F32)
        f = jnp.dot(tri, lf, precision=lax.Precision.HIGHEST, preferred_element_type=F32) + carry[...]
        carry[...] = f[tb - 1:tb, :]
        f_ref[...] = f
        f1_ref[...], f2_ref[...], f3_ref[...] = _split3(f)

    return pl.pallas_call(
        body, name=name, grid=(T // tb,), in_specs=[row, pl.BlockSpec((1, LANES), lambda i: (0, 0))],
        out_specs=[row] * 4, out_shape=[jax.ShapeDtypeStruct((T, LANES), F32)] * 4,
        scratch_shapes=[pltpu.VMEM((1, LANES), F32)],
        compiler_params=_params(("arbitrary",)),
    )(fl, b_f)


def _gate_bwd(dF_q, dF_k, fl, b_f, name):
    T = fl.shape[0]
    tb = min(GATE_TB, T)
    nb = T // tb
    row = pl.BlockSpec((tb, LANES), lambda i: (nb - 1 - i, 0))
    vec = pl.BlockSpec((1, LANES), lambda i: (0, 0))

    def body(dfq_ref, dfk_ref, fl_ref, b_ref, dfl_ref, db_ref, carry):
        @pl.when(pl.program_id(0) == 0)
        def _():
            carry[...] = jnp.zeros_like(carry)
            db_ref[...] = jnp.zeros_like(db_ref)

        tri = (lax.broadcasted_iota(jnp.int32, (tb, tb), 0) <= lax.broadcasted_iota(jnp.int32, (tb, tb), 1)).astype(F32)
        rc = jnp.dot(tri, dfq_ref[...] - dfk_ref[...], precision=lax.Precision.HIGHEST,
                     preferred_element_type=F32) + carry[...]
        carry[...] = rc[0:1, :]
        dfl = rc * jax.nn.sigmoid(-(fl_ref[...] + b_ref[...]))
        dfl_ref[...] = dfl
        db_ref[...] += jnp.sum(dfl, axis=0, keepdims=True)

    return pl.pallas_call(
        body, name=name, grid=(nb,), in_specs=[row, row, row, vec], out_specs=[row, vec],
        out_shape=[jax.ShapeDtypeStruct((T, LANES), F32), jax.ShapeDtypeStruct((1, LANES), F32)],
        scratch_shapes=[pltpu.VMEM((1, LANES), F32)],
        compiler_params=_params(("arbitrary",)),
    )(dF_q, dF_k, fl, b_f)


ATT_TB = 512
LAYOUT_TB = 2048
NT = (((1,), (1,)), ((), ()))
NN = (((1,), (0,)), ((), ()))


def _lane_insert(base, parts, first_lane):
    lane = lax.broadcasted_iota(jnp.int32, base.shape, 1)
    out = base
    for n, p in enumerate(parts):
        out = jnp.where(lane == first_lane + n, p, out)
    return out


def _causal(tb):
    return lax.broadcasted_iota(jnp.int32, (tb, tb), 0) >= lax.broadcasted_iota(jnp.int32, (tb, tb), 1)


def _other_head(x):
    return pltpu.roll(x, HEAD_DIM, 1)


def _attn_pack(proj, f1, f2, f3, name):
    T = proj.shape[0]
    tb = min(LAYOUT_TB, T)
    pairs = N_HEADS // 2
    cols = lambda first: pl.BlockSpec((tb, LANES), lambda p, i: (i, first + p))
    gate = pl.BlockSpec((tb, LANES), lambda p, i: (i, 0))
    heads = pl.BlockSpec((2, tb, LANES), lambda p, i: (p, i, 0))
    norm = pl.BlockSpec((2, 8, LANES), lambda p, i: (p, 0, 0))

    def body(q_ref, k_ref, v_ref, f1_ref, f2_ref, f3_ref, qa_ref, ka_ref, va_ref, qn_ref, kn_ref):
        p, i = pl.program_id(0), pl.program_id(1)
        lane = lax.broadcasted_iota(jnp.int32, (tb, LANES), 1)
        data = lane < HEAD_DIM

        @pl.when(i == 0)
        def _():
            qn_ref[...] = jnp.zeros_like(qn_ref)
            kn_ref[...] = jnp.zeros_like(kn_ref)

        qv, kv, vv = q_ref[...] * 0.125, k_ref[...], v_ref[...]
        for e in range(2):
            fcol = [jnp.sum(jnp.where(lane == 2 * p + e, f[...], 0.0), axis=-1, keepdims=True)
                    for f in (f1_ref, f2_ref, f3_ref)]
            pick = (lambda t: t) if e == 0 else _other_head
            qd = jnp.where(data, pick(qv), 0.0).astype(BF16).astype(F32)
            kd = jnp.where(data, pick(kv), 0.0).astype(BF16).astype(F32)
            vd = jnp.where(data, pick(vv), 0.0)
            qa_ref[e] = _lane_insert(qd, fcol + [1.0, 1.0, 1.0], AUG_F).astype(BF16)
            ka_ref[e] = _lane_insert(kd, [1.0, 1.0, 1.0] + [-f for f in fcol] + [1.0, 1.0, 1.0], AUG_F).astype(BF16)
            va_ref[e] = _lane_insert(vd, [-1.0, -1.0, -1.0, 1.0], AUG_F).astype(BF16)
            qn_ref[e] = jnp.maximum(qn_ref[e], jnp.max(jnp.sum(qd * qd, axis=-1, keepdims=True)))
            kn_ref[e] = jnp.maximum(kn_ref[e], jnp.max(jnp.sum(kd * kd, axis=-1, keepdims=True)))

    aug = jax.ShapeDtypeStruct((N_HEADS, T, LANES), BF16)
    nrm = jax.ShapeDtypeStruct((N_HEADS, 8, LANES), F32)
    return pl.pallas_call(
        body, name=name, grid=(pairs, T // tb),
        in_specs=[cols(0), cols(pairs), cols(2 * pairs), gate, gate, gate],
        out_specs=[heads, heads, heads, norm, norm], out_shape=[aug, aug, aug, nrm, nrm],
        compiler_params=_params(("parallel", "arbitrary")),
    )(proj, proj, proj, f1, f2, f3)


def _merge_heads(x_aug, scale, out_dtype, name, column=None):
    H, T, _ = x_aug.shape
    tb = min(LAYOUT_TB, T)

    def body(x_ref, o_ref, *col_ref):
        p = pl.program_id(1)
        lane = lax.broadcasted_iota(jnp.int32, (tb, LANES), 1)
        x0, x1 = x_ref[0], x_ref[1]
        o_ref[...] = (jnp.where(lane < HEAD_DIM, x0, _other_head(x1)) * scale).astype(out_dtype)
        if column is not None:
            @pl.when(p == 0)
            def _():
                col_ref[0][...] = jnp.zeros_like(col_ref[0])

            c0 = jnp.sum(jnp.where(lane == column, x0, 0.0), axis=-1, keepdims=True)
            c1 = jnp.sum(jnp.where(lane == column, x1, 0.0), axis=-1, keepdims=True)
            col_ref[0][...] += jnp.where(lane == 2 * p, c0, 0.0) + jnp.where(lane == 2 * p + 1, c1, 0.0)

    out_specs = [pl.BlockSpec((tb, LANES), lambda i, p: (i, p))]
    out_shape = [jax.ShapeDtypeStruct((T, D), out_dtype)]
    if column is not None:
        out_specs.append(pl.BlockSpec((tb, LANES), lambda i, p: (i, 0)))
        out_shape.append(jax.ShapeDtypeStruct((T, LANES), F32))
    return pl.pallas_call(
        body, name=name, grid=(T // tb, H // 2),
        in_specs=[pl.BlockSpec((2, tb, LANES), lambda i, p: (p, i, 0))],
        out_specs=out_specs, out_shape=out_shape,
        compiler_params=_params(("parallel", "arbitrary")),
    )(x_aug)


def _attn_prep(d_o, o_aug, name):
    H, T, _ = o_aug.shape
    tb = min(LAYOUT_TB, T)
    heads = pl.BlockSpec((2, tb, LANES), lambda p, i: (p, i, 0))

    def body(do_ref, o_ref, out_ref):
        lane = lax.broadcasted_iota(jnp.int32, (tb, LANES), 1)
        dov = do_ref[...]
        for e in range(2):
            d_e = jnp.where(lane < HEAD_DIM, dov if e == 0 else _other_head(dov), 0.0)
            delta = jnp.sum(d_e * o_ref[e], axis=-1, keepdims=True)
            out_ref[e] = _lane_insert(d_e, _split3(delta), AUG_F).astype(BF16)

    return pl.pallas_call(
        body, name=name, grid=(H // 2, T // tb),
        in_specs=[pl.BlockSpec((tb, LANES), lambda p, i: (i, p)), heads], out_specs=heads,
        out_shape=jax.ShapeDtypeStruct((H, T, LANES), BF16),
        compiler_params=_params(("parallel", "parallel")),
    )(d_o, o_aug)


def _flash_spec(n_in, n_out, H, nb, tb, T, resident, scratch, n_tables=3):
    blk = pl.BlockSpec((2, tb, LANES), lambda h, i, *_: (h, i, 0))
    whole = pl.BlockSpec((2, T, LANES), lambda h, i, *_: (h, 0, 0))
    return pltpu.PrefetchScalarGridSpec(
        num_scalar_prefetch=n_tables, grid=(H // 2, nb),
        in_specs=[whole if resident[n] else blk for n in range(n_in)],
        out_specs=[blk] * n_out, scratch_shapes=scratch)


def _first_live(qk_ref, fs_ref, fe_ref, h, i):
    top = qk_ref[h] + fs_ref[h, i]
    return lax.fori_loop(0, i, lambda j, n: n + jnp.where(top - fe_ref[h, j] < SKIP_BELOW, 1, 0), 0)


def _flash_fwd(qk, fs, fe, kmax, q_aug, k_aug, v_aug, name):
    H, T, _ = q_aug.shape
    tb = min(ATT_TB, T)

    def body(qk_ref, fs_ref, fe_ref, km_ref, q_ref, k_ref, v_ref, o_ref, q2_ref, m_s, acc_s, qm_s):
        h, i = 2 * pl.program_id(0), pl.program_id(1)
        acc_s[...] = jnp.zeros_like(acc_s)
        first = jnp.minimum(_first_live(qk_ref, fs_ref, fe_ref, h, i), _first_live(qk_ref, fs_ref, fe_ref, h + 1, i))
        bounded = jnp.logical_and(qk_ref[h] < FIXED_MAX_BELOW, qk_ref[h + 1] < FIXED_MAX_BELOW)

        def sweep(step):
            def loop_body(kb, carry):
                step(kb, False)
                return carry

            lax.fori_loop(first, i, loop_body, 0)
            step(i, True)

        @pl.when(bounded)
        def _():
            lane = lax.broadcasted_iota(jnp.int32, (tb, LANES), 1)
            for e in range(2):
                qf = q_ref[e].astype(F32)
                norm = jnp.sqrt(jnp.sum(jnp.where(lane < HEAD_DIM, qf * qf, 0.0), axis=-1, keepdims=True))
                bound = norm * (1.01 * km_ref[h + e]) + 1e-3
                m_s[e] = bound
                qm_s[e] = _lane_insert(qf, [-p for p in _split3(bound)], AUG_L).astype(BF16)

            def step(kb, masked):
                rows = pl.ds(pl.multiple_of(kb * tb, tb), tb)
                for e in range(2):
                    p = jnp.exp(lax.dot_general(qm_s[e], k_ref[e, rows, :], NT, preferred_element_type=F32))
                    if masked:
                        p = jnp.where(_causal(tb), p, 0.0)
                    acc_s[e] += lax.dot_general(p.astype(BF16), v_ref[e, rows, :], NN, preferred_element_type=F32)

            sweep(step)

        @pl.when(jnp.logical_not(bounded))
        def _():
            m_s[...] = jnp.full(m_s.shape, -jnp.inf, F32)

            def step(kb, masked):
                rows = pl.ds(pl.multiple_of(kb * tb, tb), tb)
                for e in range(2):
                    s = lax.dot_general(q_ref[e], k_ref[e, rows, :], NT, preferred_element_type=F32)
                    if masked:
                        s = jnp.where(_causal(tb), s, MASK_VALUE)
                    m_old = m_s[e]
                    m_new = jnp.maximum(m_old, jnp.max(s, axis=-1, keepdims=True))
                    p = jnp.exp(s - m_new)
                    acc_s[e] = jnp.exp(m_old - m_new) * acc_s[e] + lax.dot_general(
                        p.astype(BF16), v_ref[e, rows, :], NN, preferred_element_type=F32)
                    m_s[e] = m_new

            sweep(step)

        for e in range(2):
            acc = acc_s[e]
            lane = lax.broadcasted_iota(jnp.int32, acc.shape, 1)
            l = jnp.sum(jnp.where(lane == AUG_ONE, acc, 0.0), axis=-1, keepdims=True)
            o_ref[e] = acc / l
            lse = m_s[e] + jnp.log(l)
            q2_ref[e] = _lane_insert(q_ref[e].astype(F32), [-p for p in _split3(lse)], AUG_L).astype(BF16)

    return pl.pallas_call(
        body, name=name,
        grid_spec=_flash_spec(3, 2, H, T // tb, tb, T, (False, True, True),
                              [pltpu.VMEM((2, tb, 1), F32), pltpu.VMEM((2, tb, LANES), F32),
                               pltpu.VMEM((2, tb, LANES), BF16)], n_tables=4),
        out_shape=[jax.ShapeDtypeStruct((H, T, LANES), F32), jax.ShapeDtypeStruct((H, T, LANES), BF16)],
        compiler_params=_params(("parallel", "arbitrary")),
    )(qk, fs, fe, kmax, q_aug, k_aug, v_aug)


def _flash_dq(qk, fs, fe, q2, k_aug, do_aug, v_aug, name):
    H, T, _ = q2.shape
    tb = min(ATT_TB, T)

    def body(qk_ref, fs_ref, fe_ref, q_ref, k_ref, do_ref, v_ref, dq_ref, acc_s):
        h, i = 2 * pl.program_id(0), pl.program_id(1)
        acc_s[...] = jnp.zeros_like(acc_s)

        def step(kb, masked):
            rows = pl.ds(pl.multiple_of(kb * tb, tb), tb)
            for e in range(2):
                k = k_ref[e, rows, :]
                p = jnp.exp(lax.dot_general(q_ref[e], k, NT, preferred_element_type=F32))
                if masked:
                    p = jnp.where(_causal(tb), p, 0.0)
                ds = p * lax.dot_general(do_ref[e], v_ref[e, rows, :], NT, preferred_element_type=F32)
                acc_s[e] += lax.dot_general(ds.astype(BF16), k, NN, preferred_element_type=F32)

        def loop_body(kb, carry):
            step(kb, False)
            return carry

        first = jnp.minimum(_first_live(qk_ref, fs_ref, fe_ref, h, i), _first_live(qk_ref, fs_ref, fe_ref, h + 1, i))
        lax.fori_loop(first, i, loop_body, 0)
        step(i, True)
        dq_ref[...] = acc_s[...]

    return pl.pallas_call(
        body, name=name,
        grid_spec=_flash_spec(4, 1, H, T // tb, tb, T, (False, True, False, True), [pltpu.VMEM((2, tb, LANES), F32)]),
        out_shape=[jax.ShapeDtypeStruct((H, T, LANES), F32)],
        compiler_params=_params(("parallel", "arbitrary")),
    )(qk, fs, fe, q2, k_aug, do_aug, v_aug)[0]


def _flash_dkv(qk, fs, fe, q2, k_aug, do_aug, v_aug, name):
    H, T, _ = q2.shape
    tb = min(ATT_TB, T)
    nb = T // tb

    def body(qk_ref, fs_ref, fe_ref, q_ref, k_ref, do_ref, v_ref, dk_ref, dv_ref, dk_s, dv_s):
        h, i = 2 * pl.program_id(0), pl.program_id(1)
        dk_s[...] = jnp.zeros_like(dk_s)
        dv_s[...] = jnp.zeros_like(dv_s)

        def step(qb, masked):
            rows = pl.ds(pl.multiple_of(qb * tb, tb), tb)
            for e in range(2):
                q, dov = q_ref[e, rows, :], do_ref[e, rows, :]
                pt = jnp.exp(lax.dot_general(k_ref[e], q, NT, preferred_element_type=F32))
                if masked:
                    pt = jnp.where(
                        lax.broadcasted_iota(jnp.int32, (tb, tb), 1) >= lax.broadcasted_iota(jnp.int32, (tb, tb), 0),
                        pt, 0.0)
                dst = pt * lax.dot_general(v_ref[e], dov, NT, preferred_element_type=F32)
                dv_s[e] += lax.dot_general(pt.astype(BF16), dov, NN, preferred_element_type=F32)
                dk_s[e] += lax.dot_general(dst.astype(BF16), q, NN, preferred_element_type=F32)

        step(i, True)

        def loop_body(qb, carry):
            step(qb, False)
            return carry

        def live_after(head):
            base = qk_ref[head] - fe_ref[head, i]
            return lax.fori_loop(i + 1, nb, lambda b, n: n + jnp.where(base + fs_ref[head, b] < SKIP_BELOW, 0, 1), 0)

        lax.fori_loop(i + 1, i + 1 + jnp.maximum(live_after(h), live_after(h + 1)), loop_body, 0)
        dk_ref[...] = dk_s[...]
        dv_ref[...] = dv_s[...]

    return pl.pallas_call(
        body, name=name,
        grid_spec=_flash_spec(4, 2, H, nb, tb, T, (True, False, True, False),
                              [pltpu.VMEM((2, tb, LANES), F32), pltpu.VMEM((2, tb, LANES), F32)]),
        out_shape=[jax.ShapeDtypeStruct((H, T, LANES), F32)] * 2,
        compiler_params=_params(("parallel", "arbitrary")),
    )(qk, fs, fe, q2, k_aug, do_aug, v_aug)


def _add_pair(place, slab, got, name):
    _, half, cols = got.shape
    grid_spec = pltpu.PrefetchScalarGridSpec(
        num_scalar_prefetch=1, grid=(N_CHIPS,),
        in_specs=[pl.BlockSpec((None, half, cols), lambda j, s: (j, s[4], 0)),
                  pl.BlockSpec((None, half, cols), lambda j, s: (j, 0, 0))],
        out_specs=pl.BlockSpec((None, half, cols), lambda j, s: (j, 0, 0)))

    def body(s_ref, a_ref, b_ref, o_ref):
        o_ref[...] = (a_ref[...] + b_ref[...]).astype(BF16)

    return pl.pallas_call(
        body, name=name, grid_spec=grid_spec, out_shape=jax.ShapeDtypeStruct(got.shape, BF16),
        compiler_params=_params(("parallel",)),
    )(place, slab, got)


def _add_chips(place, pair, by_chip, name):
    _, half, cols = pair.shape
    tb = min(256, half)
    steps = half // tb
    slot = lambda n: pl.BlockSpec((None, tb, cols), lambda i, s: (s[n], i, 0))
    grid_spec = pltpu.PrefetchScalarGridSpec(
        num_scalar_prefetch=1, grid=(steps,), in_specs=[slot(0), slot(1), slot(2), slot(3)],
        out_specs=pl.BlockSpec((tb, cols), lambda i, s: (s[4] * steps + i, 0)))

    def body(s_ref, own_ref, b1_ref, b2_ref, b3_ref, o_ref):
        o_ref[...] = ((own_ref[...].astype(F32) + b1_ref[...].astype(F32)) + b2_ref[...].astype(F32)) \
            + b3_ref[...].astype(F32)

    return pl.pallas_call(
        body, name=name, grid_spec=grid_spec, out_shape=jax.ShapeDtypeStruct((2 * half, cols), F32),
        compiler_params=_params(("parallel",)),
    )(place, pair, by_chip, by_chip, by_chip)


def _add_slots(b, name, tb=128):
    n, rows, cols = b.shape
    tb = tb if rows % tb == 0 else rows
    specs = [pl.BlockSpec((None, tb, cols), functools.partial(lambda j, i: (j, i, 0), j)) for j in range(n)]

    def body(*refs):
        acc = refs[0][...]
        for r in refs[1:n]:
            acc = acc + r[...]
        refs[n][...] = acc

    return pl.pallas_call(
        body, name=name, grid=(rows // tb,), in_specs=specs, out_specs=pl.BlockSpec((tb, cols), lambda i: (i, 0)),
        out_shape=jax.ShapeDtypeStruct((rows, cols), F32), compiler_params=_params(("parallel",)),
    )(*([b] * n))


def _adamw(w, g, m, v, name):
    rows, cols = w.shape
    tr = 256 if rows % 256 == 0 else rows
    blk = pl.BlockSpec((tr, cols), lambda i: (i, 0))

    def body(w_ref, g_ref, m_ref, v_ref, d_ref, nm_ref, nv_ref):
        gv = g_ref[...]
        nm = ADAM_B1 * m_ref[...] + (1.0 - ADAM_B1) * gv
        nv = ADAM_B2 * v_ref[...] + (1.0 - ADAM_B2) * (gv * gv)
        m_hat = nm / (1.0 - ADAM_B1 ** ADAM_STEP)
        v_hat = nv / (1.0 - ADAM_B2 ** ADAM_STEP)
        d_ref[...] = -ADAM_LR * (m_hat / (jnp.sqrt(v_hat) + ADAM_EPS) + ADAM_WD * w_ref[...])
        nm_ref[...] = nm
        nv_ref[...] = nv

    return pl.pallas_call(
        body, name=name, grid=(rows // tr,), in_specs=[blk] * 4, out_specs=[blk] * 3,
        out_shape=[jax.ShapeDtypeStruct((rows, cols), F32)] * 3, compiler_params=_params(("parallel",)),
    )(w, g, m, v)


ANY = pl.BlockSpec(memory_space=pl.ANY)


def _place():
    x, y, c = lax.axis_index("x"), lax.axis_index("y"), lax.axis_index("c")
    others = [(1 - x, y), (x, 1 - y), (1 - x, 1 - y)]
    return x, y, c, 2 * x + y, others


def _half(ref, shape, h):
    if len(shape) == 3:
        return ref.at[h]
    return ref.at[pl.ds(h * (shape[0] // 2), shape[0] // 2), :]


def _allgather_weights(shards):
    n_u = len(GATHER_UNITS)

    def body(*refs):
        ins, outs = refs[:n_u], refs[n_u:2 * n_u]
        send_sems, recv_sems, own_send_sems, own_recv_sems = refs[2 * n_u:]
        x, y, c, me, others = _place()
        sibling = (x, y, 1 - c)

        def copy(u, k, chip, h, to, src=None):
            dst = _half(outs[u].at[chip], GATHER_UNITS[u][1], h)
            return pltpu.make_async_remote_copy(
                src_ref=dst if src is None else src, dst_ref=dst, send_sem=send_sems.at[6 * u + k],
                recv_sem=recv_sems.at[6 * u + k], device_id=to, device_id_type=MESH)

        mine = [pltpu.make_async_remote_copy(
            src_ref=ins[u], dst_ref=outs[u].at[me], send_sem=own_send_sems.at[u], recv_sem=own_recv_sems.at[u],
            device_id=sibling, device_id_type=MESH) for u in range(n_u)]
        for cp in mine:
            cp.start()
        first = [copy(u, j, me, c, (ox, oy, c), src=_half(ins[u], GATHER_UNITS[u][1], c))
                 for u in range(n_u) for j, (ox, oy) in enumerate(others)]
        for cp in first:
            cp.start()
        passed = []
        for u in range(n_u):
            for j, (ox, oy) in enumerate(others):
                copy(u, j, 2 * ox + oy, c, (x, y, c)).wait_recv()
                passed.append(copy(u, 3 + j, 2 * ox + oy, c, sibling))
                passed[-1].start()
        for u in range(n_u):
            for j, (ox, oy) in enumerate(others):
                copy(u, 3 + j, 2 * ox + oy, 1 - c, (x, y, c)).wait_recv()
        for cp in first + passed:
            cp.wait_send()
        for cp in mine:
            cp.wait()

    return pl.pallas_call(
        body, name="allgather_weights", in_specs=[ANY] * n_u, out_specs=[ANY] * n_u,
        out_shape=[jax.ShapeDtypeStruct((N_CHIPS,) + shape, BF16) for _, shape in GATHER_UNITS],
        scratch_shapes=[pltpu.SemaphoreType.DMA((6 * n_u,)), pltpu.SemaphoreType.DMA((6 * n_u,)),
                        pltpu.SemaphoreType.DMA((n_u,)), pltpu.SemaphoreType.DMA((n_u,))],
    )(*shards)


def _sibling_exchange(slabs):
    n_u = len(GRAD_UNITS)

    def body(*refs):
        ins, got = refs[:n_u], refs[n_u:2 * n_u]
        send_sems, recv_sems = refs[2 * n_u:]
        x, y, c, _, _ = _place()
        copies = []
        for u, (_, (rows, _), _) in enumerate(GRAD_UNITS):
            half = rows // 2
            copies.append(pltpu.make_async_remote_copy(
                src_ref=ins[u].at[:, pl.ds((1 - c) * half, half), :], dst_ref=got[u], send_sem=send_sems.at[u],
                recv_sem=recv_sems.at[u], device_id=(x, y, 1 - c), device_id_type=MESH))
        for cp in copies:
            cp.start()
        for cp in copies:
            cp.wait()

    return pl.pallas_call(
        body, name="grad_sibling_exchange", in_specs=[ANY] * n_u, out_specs=[ANY] * n_u,
        out_shape=[jax.ShapeDtypeStruct((N_CHIPS, rows // 2, cols), F32) for _, (rows, cols), _ in GRAD_UNITS],
        scratch_shapes=[pltpu.SemaphoreType.DMA((n_u,)), pltpu.SemaphoreType.DMA((n_u,))],
    )(*slabs)


def _chip_exchange(pairs, small):
    n_u = len(GRAD_UNITS)

    def body(*refs):
        ins, s_ref = refs[:n_u], refs[n_u]
        outs, sall_ref = refs[n_u + 1:2 * n_u + 1], refs[2 * n_u + 1]
        send_sems, recv_sems, ssend_sems, srecv_sems, local_sem = refs[2 * n_u + 2:]
        x, y, c, me, others = _place()
        dev = 4 * x + 2 * y + c
        local = [pltpu.make_async_copy(s_ref, sall_ref.at[dev], local_sem)]
        sends = [pltpu.make_async_remote_copy(
            src_ref=ins[u].at[2 * ox + oy], dst_ref=outs[u].at[me], send_sem=send_sems.at[3 * u + j],
            recv_sem=recv_sems.at[3 * u + j], device_id=(ox, oy, c), device_id_type=MESH)
            for u in range(n_u) for j, (ox, oy) in enumerate(others)]
        flips = [(fx, fy, fc) for fx in (0, 1) for fy in (0, 1) for fc in (0, 1)][1:]
        sends += [pltpu.make_async_remote_copy(
            src_ref=s_ref, dst_ref=sall_ref.at[dev], send_sem=ssend_sems.at[n], recv_sem=srecv_sems.at[n],
            device_id=(x ^ fx, y ^ fy, c ^ fc), device_id_type=MESH) for n, (fx, fy, fc) in enumerate(flips)]
        for cp in local + sends:
            cp.start()
        for u in range(n_u):
            for j, (ox, oy) in enumerate(others):
                pltpu.make_async_remote_copy(
                    src_ref=ins[u].at[me], dst_ref=outs[u].at[2 * ox + oy], send_sem=send_sems.at[3 * u + j],
                    recv_sem=recv_sems.at[3 * u + j], device_id=(x, y, c), device_id_type=MESH).wait_recv()
        for n, (fx, fy, fc) in enumerate(flips):
            src_dev = 4 * (x ^ fx) + 2 * (y ^ fy) + (c ^ fc)
            pltpu.make_async_remote_copy(
                src_ref=s_ref, dst_ref=sall_ref.at[src_dev], send_sem=ssend_sems.at[n], recv_sem=srecv_sems.at[n],
                device_id=(x, y, c), device_id_type=MESH).wait_recv()
        for cp in sends:
            cp.wait_send()
        for cp in local:
            cp.wait()

    outs = pl.pallas_call(
        body, name="grad_chip_exchange", in_specs=[ANY] * (n_u + 1), out_specs=[ANY] * (n_u + 1),
        out_shape=[jax.ShapeDtypeStruct(p.shape, p.dtype) for p in pairs]
        + [jax.ShapeDtypeStruct((8, SMALL_ALL_ROWS, LANES), F32)],
        scratch_shapes=[pltpu.SemaphoreType.DMA((3 * n_u,)), pltpu.SemaphoreType.DMA((3 * n_u,)),
                        pltpu.SemaphoreType.DMA((7,)), pltpu.SemaphoreType.DMA((7,)), pltpu.SemaphoreType.DMA],
    )(*pairs, small)
    return outs[:n_u], outs[n_u]


def _sibling_share(grads):
    n_u = len(GRAD_UNITS)

    def body(*refs):
        ins, outs = refs[:n_u], refs[n_u:2 * n_u]
        send_sems, recv_sems = refs[2 * n_u:]
        x, y, c, _, _ = _place()

        def rows_of(ref, u, h):
            half = GRAD_UNITS[u][1][0] // 2
            return ref.at[pl.ds(h * half, half), :]

        sends = [pltpu.make_async_remote_copy(
            src_ref=rows_of(ins[u], u, c), dst_ref=rows_of(outs[u], u, c), send_sem=send_sems.at[u],
            recv_sem=recv_sems.at[u], device_id=(x, y, 1 - c), device_id_type=MESH) for u in range(n_u)]
        for cp in sends:
            cp.start()
        for u in range(n_u):
            pltpu.make_async_remote_copy(
                src_ref=rows_of(ins[u], u, c), dst_ref=rows_of(outs[u], u, 1 - c), send_sem=send_sems.at[u],
                recv_sem=recv_sems.at[u], device_id=(x, y, c), device_id_type=MESH).wait_recv()
        for cp in sends:
            cp.wait_send()

    return pl.pallas_call(
        body, name="grad_sibling_share", in_specs=[ANY] * n_u, out_specs=[ANY] * n_u,
        out_shape=[jax.ShapeDtypeStruct(g.shape, F32) for g in grads],
        input_output_aliases={u: u for u in range(n_u)},
        scratch_shapes=[pltpu.SemaphoreType.DMA((n_u,)), pltpu.SemaphoreType.DMA((n_u,))],
    )(*grads)


def kernel(x, g_mix_pre, g_mix_post, g_ffn_pre, g_ffn_post, conv_pw1_w, conv_pw1_b, conv_dw_w, conv_dw_b, conv_ln_g, conv_ln_b, conv_pw2_w, conv_pw2_b, attn_w_in, attn_b_f, attn_w_o, mlp_w_up, mlp_w_down, loss_target, m_g_mix_pre, m_g_mix_post, m_g_ffn_pre, m_g_ffn_post, m_conv_pw1_w, m_conv_pw1_b, m_conv_dw_w, m_conv_dw_b, m_conv_ln_g, m_conv_ln_b, m_conv_pw2_w, m_conv_pw2_b, m_attn_w_in, m_attn_b_f, m_attn_w_o, m_mlp_w_up, m_mlp_w_down, v_g_mix_pre, v_g_mix_post, v_g_ffn_pre, v_g_ffn_post, v_conv_pw1_w, v_conv_pw1_b, v_conv_dw_w, v_conv_dw_b, v_conv_ln_g, v_conv_ln_b, v_conv_pw2_w, v_conv_pw2_b, v_attn_w_in, v_attn_b_f, v_attn_w_o, v_mlp_w_up, v_mlp_w_down):
    weights = dict(g_mix_pre=g_mix_pre, g_mix_post=g_mix_post, g_ffn_pre=g_ffn_pre, g_ffn_post=g_ffn_post, conv_pw1_w=conv_pw1_w, conv_pw1_b=conv_pw1_b, conv_dw_w=conv_dw_w, conv_dw_b=conv_dw_b, conv_ln_g=conv_ln_g, conv_ln_b=conv_ln_b, conv_pw2_w=conv_pw2_w, conv_pw2_b=conv_pw2_b, attn_w_in=attn_w_in, attn_b_f=attn_b_f, attn_w_o=attn_w_o, mlp_w_up=mlp_w_up, mlp_w_down=mlp_w_down)
    mom_m = dict(g_mix_pre=m_g_mix_pre, g_mix_post=m_g_mix_post, g_ffn_pre=m_g_ffn_pre, g_ffn_post=m_g_ffn_post, conv_pw1_w=m_conv_pw1_w, conv_pw1_b=m_conv_pw1_b, conv_dw_w=m_conv_dw_w, conv_dw_b=m_conv_dw_b, conv_ln_g=m_conv_ln_g, conv_ln_b=m_conv_ln_b, conv_pw2_w=m_conv_pw2_w, conv_pw2_b=m_conv_pw2_b, attn_w_in=m_attn_w_in, attn_b_f=m_attn_b_f, attn_w_o=m_attn_w_o, mlp_w_up=m_mlp_w_up, mlp_w_down=m_mlp_w_down)
    mom_v = dict(g_mix_pre=v_g_mix_pre, g_mix_post=v_g_mix_post, g_ffn_pre=v_g_ffn_pre, g_ffn_post=v_g_ffn_post, conv_pw1_w=v_conv_pw1_w, conv_pw1_b=v_conv_pw1_b, conv_dw_w=v_conv_dw_w, conv_dw_b=v_conv_dw_b, conv_ln_g=v_conv_ln_g, conv_ln_b=v_conv_ln_b, conv_pw2_w=v_conv_pw2_w, conv_pw2_b=v_conv_pw2_b, attn_w_in=v_attn_w_in, attn_b_f=v_attn_b_f, attn_w_o=v_attn_w_o, mlp_w_up=v_mlp_w_up, mlp_w_down=v_mlp_w_down)
    order = [n for n, _ in SMALL[:4]] + ["conv_pw1_w", "conv_pw1_b", "conv_dw_w", "conv_dw_b", "conv_ln_g", "conv_ln_b",
                                          "conv_pw2_w", "conv_pw2_b", "attn_w_in", "attn_b_f", "attn_w_o", "mlp_w_up",
                                          "mlp_w_down"]
    T = x.shape[1]
    x0 = x.reshape(T, D)
    target = loss_target.reshape(T, D)
    row = lambda a, l: a[l:l + 1, :]

    shards = []
    for name, shape in GATHER_UNITS:
        w = weights[name].astype(BF16)
        w = w.reshape(w.shape[-len(shape):])
        if name == "conv_dw_w":
            w = jnp.pad(w, ((0, HALO - CONV_W), (0, 0)))
        shards.append(w)
    g_pw1, g_dw, g_pw2, g_win, g_wo, g_up, g_down = _allgather_weights(shards)
    w_pw2, w_o = g_pw2.reshape(D, D), g_wo.reshape(D, D)
    w_pw1 = g_pw1.transpose(1, 0, 2).reshape(D, 2 * D)
    w_up = [g_up[:, l].transpose(1, 0, 2).reshape(D, DFF) for l in range(2)]
    w_down = [g_down[:, l].reshape(DFF, D) for l in range(2)]
    w_in = jnp.pad(g_win.transpose(1, 0, 2).reshape(D, W_IN_COLS), ((0, 0), (0, W_IN_PAD - W_IN_COLS)))
    w_dw = g_dw.transpose(1, 0, 2).reshape(HALO, D).astype(F32)
    b_f = jnp.pad(attn_b_f, ((0, 0), (0, LANES - N_HEADS)))

    def mlp_fwd(h, l):
        (act,) = _matmul(h, w_up[l], "nn", (BF16,), _epi_sqrelu, f"mlp{l}_up")
        (mo,) = _matmul(act, w_down[l], "nn", (F32,), _epi_plain, f"mlp{l}_down", tk=DFF)
        return act, mo

    (h0,) = _resnorm_fwd(x0, None, None, row(g_mix_pre, 0), "norm_in")
    (a0,) = _matmul(h0, w_pw1, "nn", (F32,), _epi_bias, "conv_pw1", extras=((conv_pw1_b, "row"),))
    y0, z0 = _conv_fwd(a0, w_dw, conv_dw_b, conv_ln_g, conv_ln_b, "conv_fwd")
    (m0,) = _matmul(z0, w_pw2, "nn", (F32,), _epi_bias, "conv_pw2", extras=((conv_pw2_b, "row"),))
    x1, h1 = _resnorm_fwd(x0, m0, row(g_mix_post, 0), row(g_ffn_pre, 0), "norm_conv_out")
    act0, mo0 = mlp_fwd(h1, 0)
    x2, h2 = _resnorm_fwd(x1, mo0, row(g_ffn_post, 0), row(g_mix_pre, 1), "norm_mlp0_out")

    (proj,) = _matmul(h2, w_in, "nn", (F32,), _epi_plain, "attn_in", tn=640)
    fl = proj[:, 3 * D:]
    f_all, f1, f2, f3 = _gate_fwd(fl, b_f, "gate_fwd")
    q_aug, k_aug, v_aug, qn2, kn2 = _attn_pack(proj, f1, f2, f3, "attn_pack")
    tb = min(ATT_TB, T)
    qk = 2.02 * jnp.sqrt(qn2[:, 0, 0] * kn2[:, 0, 0]) + 1.0
    fs, fe = f_all[0::tb, :N_HEADS].T, f_all[tb - 1::tb, :N_HEADS].T
    o_aug, q2_aug = _flash_fwd(qk, fs, fe, jnp.sqrt(kn2[:, 0, 0]), q_aug, k_aug, v_aug, "flash_fwd")
    (o_bf,) = _merge_heads(o_aug, 1.0, BF16, "attn_merge_o")
    (m1,) = _matmul(o_bf, w_o, "nn", (F32,), _epi_plain, "attn_out")
    x3, h3 = _resnorm_fwd(x2, m1, row(g_mix_post, 1), row(g_ffn_pre, 1), "norm_attn_out")
    act1, mo1 = mlp_fwd(h3, 1)
    dy, loss_part = _resnorm_loss(x3, mo1, row(g_ffn_post, 1), target, "loss")
    loss = lax.psum(loss_part[0, 0], ("x", "y", "c"))

    G = {}

    def mlp_bwd(d_out, mo, act, h, x_in, l):
        d_mo, dg_post, _ = _norm_bwd(mo, row(g_ffn_post, l), d_out, None, BF16, f"mlp{l}_post_bwd")
        (d_up,) = _matmul(d_mo, w_down[l], "nt", (BF16,), _epi_dsqrelu, f"mlp{l}_dact", extras=((act, "tile"),))
        (dw_down,) = _matmul(act, d_mo, "tn", (F32,), _epi_plain, tk=2048, name=f"mlp{l}_dwdown")
        (dw_up,) = _matmul(h, d_up, "tn", (F32,), _epi_plain, tk=2048, name=f"mlp{l}_dwup", shard_out=True)
        (d_h,) = _matmul(d_up, w_up[l], "nt", (F32,), _epi_plain, f"mlp{l}_dh", tk=DFF)
        d_in, dg_pre, _ = _norm_bwd(x_in, row(g_ffn_pre, l), d_h, d_out, F32, f"mlp{l}_pre_bwd")
        G[f"up{l}"], G[f"down{l}"] = dw_up, dw_down.reshape(N_CHIPS, D, D)
        return d_in, dg_post, dg_pre

    d_x3, dg_ffn_post1, dg_ffn_pre1 = mlp_bwd(dy, mo1, act1, h3, x3, 1)

    d_m1, dg_mix_post1, _ = _norm_bwd(m1, row(g_mix_post, 1), d_x3, None, BF16, "attn_post_bwd")
    (dw_o,) = _matmul(o_bf, d_m1, "tn", (F32,), _epi_plain, tk=2048, name="attn_dwo")
    G["attn_w_o"] = dw_o.reshape(N_CHIPS, D // N_CHIPS, D)
    (d_o,) = _matmul(d_m1, w_o, "nt", (F32,), _epi_plain, "attn_do")
    do_aug = _attn_prep(d_o, o_aug, "attn_prep")
    dq_aug = _flash_dq(qk, fs, fe, q2_aug, k_aug, do_aug, v_aug, "flash_dq")
    dk_aug, dv_aug = _flash_dkv(qk, fs, fe, q2_aug, k_aug, do_aug, v_aug, "flash_dkv")
    d_q, dF_q = _merge_heads(dq_aug, 0.125, BF16, "attn_merge_dq", column=AUG_F)
    d_k, dF_k = _merge_heads(dk_aug, 1.0, BF16, "attn_merge_dk", column=AUG_ONE)
    (d_v,) = _merge_heads(dv_aug, 1.0, BF16, "attn_merge_dv")
    d_fl, db_f = _gate_bwd(dF_q, dF_k, fl, b_f, "gate_bwd")
    d_proj = jnp.concatenate([d_q, d_k, d_v, d_fl.astype(BF16)], axis=1)
    (dw_in,) = _matmul(h2, d_proj, "tn", (F32,), _epi_plain, tk=2048, name="attn_dwin", tn=640)
    G["attn_w_in"] = dw_in[:, :W_IN_COLS].reshape(D, N_CHIPS, W_IN_COLS // N_CHIPS).transpose(1, 0, 2)
    (d_h2,) = _matmul(d_proj, w_in, "nt", (F32,), _epi_plain, "attn_dh", tk=W_IN_PAD)
    d_x2, dg_mix_pre1, _ = _norm_bwd(x2, row(g_mix_pre, 1), d_h2, d_x3, F32, "attn_pre_bwd")

    d_x1, dg_ffn_post0, dg_ffn_pre0 = mlp_bwd(d_x2, mo0, act0, h1, x1, 0)

    d_m0, dg_mix_post0, db_pw2 = _norm_bwd(m0, row(g_mix_post, 0), d_x1, None, BF16, "conv_post_bwd")
    (dw_pw2,) = _matmul(z0, d_m0, "tn", (F32,), _epi_plain, tk=2048, name="conv_dwpw2")
    G["conv_pw2_w"] = dw_pw2.reshape(N_CHIPS, D // N_CHIPS, D)
    (d_z0,) = _matmul(d_m0, w_pw2, "nt", (F32,), _epi_plain, "conv_dz")
    d_y0, dln_g, dln_b, ddw_b = _conv_bwd_ln(d_z0, y0, conv_ln_g, conv_ln_b, "conv_bwd_ln")
    d_a0, db_pw1, ddw_w = _conv_bwd_dw(d_y0, a0, w_dw, "conv_bwd_dw")
    (G["conv_pw1_w"],) = _matmul(h0, d_a0, "tn", (F32,), _epi_plain, tk=2048, name="conv_dwpw1", tn=512, shard_out=True)
    (d_h0,) = _matmul(d_a0, w_pw1, "nt", (F32,), _epi_plain, "conv_dh", tk=2 * D)
    d_x0, dg_mix_pre0, _ = _norm_bwd(x0, row(g_mix_pre, 0), d_h0, d_x1, F32, "conv_pre_bwd")

    small_local = {
        "g_mix_pre": jnp.concatenate([dg_mix_pre0, dg_mix_pre1]), "g_mix_post": jnp.concatenate([dg_mix_post0, dg_mix_post1]),
        "g_ffn_pre": jnp.concatenate([dg_ffn_pre0, dg_ffn_pre1]), "g_ffn_post": jnp.concatenate([dg_ffn_post0, dg_ffn_post1]),
        "conv_pw1_b": db_pw1, "conv_dw_b": ddw_b, "conv_ln_g": dln_g, "conv_ln_b": dln_b, "conv_pw2_b": db_pw2,
        "attn_b_f": db_f[:, :N_HEADS], "conv_dw_w_full": ddw_w,
    }

    chip = 2 * lax.axis_index("x") + lax.axis_index("y")
    place = jnp.stack([chip] + [k + (k >= chip).astype(jnp.int32) for k in range(N_CHIPS - 1)]
                      + [lax.axis_index("c")]).astype(jnp.int32)
    slabs = [G[n] for n, _, _ in GRAD_UNITS]
    got = _sibling_exchange(slabs)
    pairs = [_add_pair(place, s, g, f"grad_pair_{n}") for s, g, (n, _, _) in zip(slabs, got, GRAD_UNITS)]
    by_chip, small_all = _chip_exchange(pairs, _pack([small_local[n] for n, _ in SMALL_ALL], SMALL_ALL_ROWS))
    unit_grads = _sibling_share([_add_chips(place, p, b, f"grad_chips_{n}")
                                 for p, b, (n, _, _) in zip(pairs, by_chip, GRAD_UNITS)])
    grads = _unpack(_add_slots(small_all, "grad_add_small", tb=SMALL_ALL_ROWS), SMALL_ALL)
    grads["conv_dw_w"] = lax.dynamic_slice(grads.pop("conv_dw_w_full"), (0, chip * 256), (CONV_W, 256))[None]

    delta, new_m, new_v = {}, {}, {}
    per_layer = {}
    for g, (unit, shape, (name, layer)) in zip(unit_grads, GRAD_UNITS):
        view = lambda d: (d[name] if layer is None else d[name][layer]).reshape(shape)
        per_layer.setdefault(name, []).append((g,) + tuple(_adamw(view(weights), g, view(mom_m), view(mom_v),
                                                                  f"adamw_{unit}")))
    for name, parts in per_layer.items():
        full = weights[name].shape
        join = lambda n: (parts[0][n] if len(parts) == 1 else jnp.stack([p[n] for p in parts])).reshape(full)
        grads[name], delta[name], new_m[name], new_v[name] = join(0), join(1), join(2), join(3)
    dw2 = lambda d: d["conv_dw_w"].reshape(CONV_W, 256)
    d, nm, nv = _adamw(dw2(weights), dw2(grads), dw2(mom_m), dw2(mom_v), "adamw_conv_dw_w")
    full = weights["conv_dw_w"].shape
    delta["conv_dw_w"], new_m["conv_dw_w"], new_v["conv_dw_w"] = d.reshape(full), nm.reshape(full), nv.reshape(full)
    packs = [_pack([src[n] for n, _ in SMALL], SMALL_ROWS) for src in (weights, grads, mom_m, mom_v)]
    d, nm, nv = _adamw(*packs, "adamw_small")
    for dst, packed in ((delta, d), (new_m, nm), (new_v, nv)):
        dst.update(_unpack(packed, SMALL))

    grad_x = d_x0.reshape(x.shape)
    return (loss, grad_x, *[grads[n] for n in order], *[delta[n] for n in order], *[new_m[n] for n in order],
            *[new_v[n] for n in order])
```

```python
import functools

import jax
import jax.numpy as jnp
from jax import lax
from jax.experimental import pallas as pl
from jax.experimental.pallas import tpu as pltpu

F32 = jnp.float32
BF16 = jnp.bfloat16
MESH = pl.DeviceIdType.MESH

D = 1024
DFF = 4096
N_HEADS = 16
HEAD_DIM = 64
CONV_W = 31
HALO = 32
RMS_EPS = 1e-6
LN_EPS = 1e-5
MASK_VALUE = -1e30
W_IN_COLS = 3 * D + N_HEADS
W_IN_PAD = 3200
LANES = 128
SUBLANES = 8
N_CHIPS = 4
VMEM_LIMIT = 56 * 1024 * 1024
MATMUL_CHUNK = 256

ADAM_LR = 0.001
ADAM_B1 = 0.9
ADAM_B2 = 0.999
ADAM_EPS = 1e-08
ADAM_WD = 0.01
ADAM_STEP = 10

AUG_F = 64
AUG_ONE = 67
AUG_L = 70
SKIP_BELOW = -104.0
FIXED_MAX_BELOW = 40.0

GATHER_CONV_LAYER = (
    ("conv_pw1_w", (1024, 512)), ("conv_dw_w", (HALO, 256)), ("conv_pw2_w", (256, 1024)),
    ("up0", (1024, 1024)), ("down0", (1024, 1024)),
)
GATHER_ATTN_LAYER = (("attn_w_in", (1024, 772)), ("attn_w_o", (256, 1024)), ("up1", (1024, 1024)), ("down1", (1024, 1024)))
GRAD_UNITS = (
    ("conv_pw1_w", (1024, 512), ("conv_pw1_w", None)),
    ("conv_pw2_w", (256, 1024), ("conv_pw2_w", None)),
    ("attn_w_in", (1024, 772), ("attn_w_in", None)),
    ("attn_w_o", (256, 1024), ("attn_w_o", None)),
    ("up0", (1024, 1024), ("mlp_w_up", 0)),
    ("up1", (1024, 1024), ("mlp_w_up", 1)),
    ("down0", (1024, 1024), ("mlp_w_down", 0)),
    ("down1", (1024, 1024), ("mlp_w_down", 1)),
)
SMALL = (
    ("g_mix_pre", (2, 1024)), ("g_mix_post", (2, 1024)), ("g_ffn_pre", (2, 1024)), ("g_ffn_post", (2, 1024)),
    ("conv_pw1_b", (1, 2048)), ("conv_dw_b", (1, 1024)), ("conv_ln_g", (1, 1024)), ("conv_ln_b", (1, 1024)),
    ("conv_pw2_b", (1, 1024)), ("attn_b_f", (1, 16)),
)
SMALL_ROWS = 120
SMALL_ALL = SMALL + (("conv_dw_w_full", (HALO, 1024)),)
SMALL_ALL_ROWS = 376


def _size(shape):
    n = 1
    for s in shape:
        n *= s
    return n


def _pack(arrays, rows):
    flat = jnp.concatenate([a.reshape(-1) for a in arrays])
    return jnp.pad(flat, (0, rows * LANES - flat.shape[0])).reshape(rows, LANES)


def _unpack(packed, table):
    flat = packed.reshape(-1)
    out, off = {}, 0
    for name, shape in table:
        n = _size(shape)
        out[name] = flat[off:off + n].reshape(shape)
        off += n
    return out


def _params(sem):
    return pltpu.CompilerParams(dimension_semantics=sem, vmem_limit_bytes=VMEM_LIMIT)


def _rms(x, g):
    return x * lax.rsqrt(jnp.mean(x * x, axis=-1, keepdims=True) + RMS_EPS) * g


def _split3(v):
    p1 = v.astype(BF16).astype(F32)
    r = v - p1
    p2 = r.astype(BF16).astype(F32)
    p3 = (r - p2).astype(BF16).astype(F32)
    return p1, p2, p3


def _matmul(a, b, mode, out_dtypes, epi, name, extras=(), tm=1024, tn=1024, tk=1024, shard_out=False):
    (K, M) = a.shape if mode == "tn" else a.shape[::-1]
    N = b.shape[0] if mode == "nt" else b.shape[1]
    tm, tn, tk = min(tm, M), min(tn, N), min(tk, K)
    nk = K // tk
    if mode == "tn":
        a_spec = pl.BlockSpec((tk, tm), lambda i, j, k: (k, i))
    else:
        a_spec = pl.BlockSpec((tm, tk), lambda i, j, k: (i, k))
    if mode == "nt":
        b_spec = pl.BlockSpec((tn, tk), lambda i, j, k: (j, k))
    else:
        b_spec = pl.BlockSpec((tk, tn), lambda i, j, k: (k, j))
    if shard_out:
        per = N // N_CHIPS // tn
        o_spec = pl.BlockSpec((None, tm, tn), lambda i, j, k: (j // per, i, j % per))
        o_shape = (N_CHIPS, M, N // N_CHIPS)
    else:
        o_spec = pl.BlockSpec((tm, tn), lambda i, j, k: (i, j))
        o_shape = (M, N)
    dims = {"nn": (((1,), (0,)), ((), ())), "nt": (((1,), (1,)), ((), ())), "tn": (((0,), (0,)), ((), ()))}[mode]
    ex_specs = []
    for _, kind in extras:
        if kind == "row":
            ex_specs.append(pl.BlockSpec((1, tn), lambda i, j, k: (0, j)))
        else:
            ex_specs.append(pl.BlockSpec((tm, tn), lambda i, j, k: (i, j)))
    n_ex, n_out = len(extras), len(out_dtypes)

    cw = MATMUL_CHUNK if tn % MATMUL_CHUNK == 0 else tn

    def body(*refs):
        a_ref, b_ref = refs[0], refs[1]
        ex = refs[2:2 + n_ex]
        outs = refs[2 + n_ex:2 + n_ex + n_out]

        def for_chunks(use):
            for c in range(tn // cw):
                cols = slice(c * cw, (c + 1) * cw)
                b_chunk = b_ref[cols, :] if mode == "nt" else b_ref[:, cols]
                use(cols, lax.dot_general(a_ref[...], b_chunk, dims, preferred_element_type=F32))

        def finish(cols, acc):
            res = epi(acc, *[e[:, cols] for e in ex])
            for o, r in zip(outs, res):
                o[:, cols] = r.astype(o.dtype)

        if nk == 1:
            for_chunks(finish)
        else:
            acc_ref = refs[-1]
            k = pl.program_id(2)

            def first(cols, prod):
                acc_ref[:, cols] = prod

            def middle(cols, prod):
                acc_ref[:, cols] += prod

            pl.when(k == 0)(lambda: for_chunks(first))
            pl.when(jnp.logical_and(k > 0, k < nk - 1))(lambda: for_chunks(middle))
            pl.when(k == nk - 1)(lambda: for_chunks(lambda cols, prod: finish(cols, acc_ref[:, cols] + prod)))

    return pl.pallas_call(
        body,
        name=name,
        grid=(M // tm, N // tn, nk),
        in_specs=[a_spec, b_spec] + ex_specs,
        out_specs=[o_spec for _ in out_dtypes],
        out_shape=[jax.ShapeDtypeStruct(o_shape, dt) for dt in out_dtypes],
        scratch_shapes=[pltpu.VMEM((tm, tn), F32)] if nk > 1 else [],
        compiler_params=_params(("parallel", "parallel", "arbitrary")),
    )(a, b, *[e for e, _ in extras])


def _epi_plain(acc):
    return (acc,)


def _epi_bias(acc, bias):
    return (acc + bias,)


def _epi_sqrelu(acc):
    r = jnp.maximum(acc, 0.0)
    return (r * r,)


def _epi_dsqrelu(acc, act):
    return (acc * (2.0 * jnp.sqrt(act.astype(F32))),)


def _resnorm_fwd(x, m, g_post, g_next, name, tr=512):
    T = x.shape[0]
    has_m = m is not None
    row = pl.BlockSpec((tr, D), lambda i: (i, 0))
    vec = pl.BlockSpec((1, D), lambda i: (0, 0))

    def body(*refs):
        if has_m:
            x_ref, m_ref, gp_ref, gn_ref, x1_ref, h_ref = refs
            x1 = x_ref[...] + _rms(m_ref[...], gp_ref[...])
            x1_ref[...] = x1
        else:
            x_ref, gn_ref, h_ref = refs
            x1 = x_ref[...]
        h_ref[...] = _rms(x1, gn_ref[...]).astype(BF16)

    if has_m:
        args, in_specs = (x, m, g_post, g_next), [row, row, vec, vec]
        out_specs = [row, row]
        out_shape = [jax.ShapeDtypeStruct((T, D), F32), jax.ShapeDtypeStruct((T, D), BF16)]
    else:
        args, in_specs = (x, g_next), [row, vec]
        out_specs = [row]
        out_shape = [jax.ShapeDtypeStruct((T, D), BF16)]
    return pl.pallas_call(
        body, name=name, grid=(T // tr,), in_specs=in_specs, out_specs=out_specs, out_shape=out_shape,
        compiler_params=_params(("parallel",)),
    )(*args)


def _resnorm_loss(x, m, g_post, target, name, tr=512):
    T = x.shape[0]
    row = pl.BlockSpec((tr, D), lambda i: (i, 0))
    vec = pl.BlockSpec((1, D), lambda i: (0, 0))

    def body(x_ref, m_ref, gp_ref, t_ref, dy_ref, loss_ref):
        diff = x_ref[...] + _rms(m_ref[...], gp_ref[...]) - t_ref[...]
        dy_ref[...] = diff * (1.0 / D)

        @pl.when(pl.program_id(0) == 0)
        def _():
            loss_ref[...] = jnp.zeros_like(loss_ref)

        per_token = jnp.mean(diff * diff, axis=-1, keepdims=True)
        loss_ref[...] += 0.5 * jnp.sum(per_token)

    return pl.pallas_call(
        body, name=name, grid=(T // tr,), in_specs=[row, row, vec, row],
        out_specs=[row, pl.BlockSpec((8, LANES), lambda i: (0, 0))],
        out_shape=[jax.ShapeDtypeStruct((T, D), F32), jax.ShapeDtypeStruct((8, LANES), F32)],
        compiler_params=_params(("arbitrary",)),
    )(x, m, g_post, target)


def _norm_bwd(x, g, dy, resid, out_dtype, name, tr=512):
    T = x.shape[0]
    has_r = resid is not None
    row = pl.BlockSpec((tr, D), lambda i: (i, 0))
    vec = pl.BlockSpec((1, D), lambda i: (0, 0))

    def body(*refs):
        if has_r:
            x_ref, g_ref, dy_ref, r_ref, dx_ref, dg_ref, cs_ref = refs
        else:
            x_ref, g_ref, dy_ref, dx_ref, dg_ref, cs_ref = refs
        xv, dyv = x_ref[...], dy_ref[...]
        r = lax.rsqrt(jnp.mean(xv * xv, axis=-1, keepdims=True) + RMS_EPS)
        gy = dyv * g_ref[...]
        c = jnp.sum(gy * xv, axis=-1, keepdims=True) * (1.0 / D)
        dx = r * gy - xv * (r * r * r * c)

        @pl.when(pl.program_id(0) == 0)
        def _():
            dg_ref[...] = jnp.zeros_like(dg_ref)
            cs_ref[...] = jnp.zeros_like(cs_ref)

        dg_ref[...] += jnp.sum(dyv * (xv * r), axis=0, keepdims=True)
        cs_ref[...] += jnp.sum(dx, axis=0, keepdims=True)
        if has_r:
            dx = dx + r_ref[...]
        dx_ref[...] = dx.astype(dx_ref.dtype)

    args = (x, g, dy) + ((resid,) if has_r else ())
    return pl.pallas_call(
        body, name=name, grid=(T // tr,), in_specs=[row, vec, row] + ([row] if has_r else []),
        out_specs=[row, vec, vec],
        out_shape=[jax.ShapeDtypeStruct((T, D), out_dtype), jax.ShapeDtypeStruct((1, D), F32),
                   jax.ShapeDtypeStruct((1, D), F32)],
        compiler_params=_params(("arbitrary",)),
    )(*args)


CONV_TB = 256
CONV_RC = 32


def _glu(a):
    return a[:, :D] * jax.nn.sigmoid(a[:, D:])


def _shifted_copies(win):
    rows = win.shape[1]
    for p in range(1, SUBLANES):
        win[p, 0:rows - SUBLANES, :] = win[0, p:p + rows - SUBLANES, :]


def _window(win, start):
    p = start % SUBLANES
    return win[p, start - p:start - p + CONV_RC, :]


def _conv_fwd(a, dw_w, dw_b, ln_g, ln_b, name, gather_units, gather_shards):
    T = a.shape[0]
    tb = min(CONV_TB, T)
    per = tb // HALO
    n_steps = T // tb
    n_g = len(gather_units)
    vec = pl.BlockSpec((1, D), lambda i: (0, 0))

    def body(*refs):
        cur_ref, prev_ref, w_ref, b_ref, lg_ref, lb_ref = refs[:6]
        g_ins = refs[6:6 + n_g]
        y_ref, z_ref = refs[6 + n_g:8 + n_g]
        g_outs = refs[8 + n_g:8 + 2 * n_g]
        uwin = refs[8 + 2 * n_g]
        sems = refs[9 + 2 * n_g:]
        i = pl.program_id(0)
        pl.when(i == 0)(lambda: _gather_start(gather_units, g_ins, g_outs, sems))
        uwin[0, 0:HALO, :] = jnp.where(i > 0, _glu(prev_ref[...]), 0.0)
        uwin[0, HALO:, :] = _glu(cur_ref[...])
        _shifted_copies(uwin)
        for r in range(tb // CONV_RC):
            acc = jnp.broadcast_to(b_ref[...], (CONV_RC, D))
            for k in range(CONV_W):
                start = r * CONV_RC + HALO - (CONV_W - 1) + k
                acc = acc + _window(uwin, start) * w_ref[k:k + 1, :]
            rows = slice(r * CONV_RC, (r + 1) * CONV_RC)
            y_ref[rows, :] = acc
            mu = jnp.mean(acc, axis=-1, keepdims=True)
            xc = acc - mu
            var = jnp.mean(xc * xc, axis=-1, keepdims=True)
            n = xc * lax.rsqrt(var + LN_EPS) * lg_ref[...] + lb_ref[...]
            z_ref[rows, :] = (n * jax.nn.sigmoid(n)).astype(BF16)
        pl.when(i == n_steps - 1)(lambda: _gather_finish(gather_units, g_ins, g_outs, sems))

    outs = pl.pallas_call(
        body, name=name, grid=(n_steps,),
        in_specs=[pl.BlockSpec((tb, 2 * D), lambda i: (i, 0)),
                  pl.BlockSpec((HALO, 2 * D), lambda i: (jnp.maximum(i * per - 1, 0), 0)),
                  pl.BlockSpec((HALO, D), lambda i: (0, 0)), vec, vec, vec] + [ANY] * n_g,
        out_specs=[pl.BlockSpec((tb, D), lambda i: (i, 0)), pl.BlockSpec((tb, D), lambda i: (i, 0))] + [ANY] * n_g,
        out_shape=[jax.ShapeDtypeStruct((T, D), F32), jax.ShapeDtypeStruct((T, D), BF16)]
        + [jax.ShapeDtypeStruct((N_CHIPS,) + shape, BF16) for _, shape in gather_units],
        scratch_shapes=[pltpu.VMEM((SUBLANES, tb + HALO, D), F32)] + _gather_sems(n_g),
        compiler_params=_params(("arbitrary",)),
    )(a, a, dw_w, dw_b, ln_g, ln_b, *gather_shards)
    return outs[0], outs[1], outs[2:]


def _conv_bwd_ln(dz, y, ln_g, ln_b, name, tr=256):
    T = y.shape[0]
    tr = min(tr, T)
    row = pl.BlockSpec((tr, D), lambda i: (i, 0))
    vec = pl.BlockSpec((1, D), lambda i: (0, 0))

    def body(dz_ref, y_ref, lg_ref, lb_ref, dy_ref, dlg_ref, dlb_ref, db_ref):
        yv = y_ref[...]
        mu = jnp.mean(yv, axis=-1, keepdims=True)
        xc = yv - mu
        rstd = lax.rsqrt(jnp.mean(xc * xc, axis=-1, keepdims=True) + LN_EPS)
        yh = xc * rstd
        n = yh * lg_ref[...] + lb_ref[...]
        sg = jax.nn.sigmoid(n)
        dn = dz_ref[...] * (sg * (1.0 + n * (1.0 - sg)))
        dyh = dn * lg_ref[...]
        dyv = rstd * (dyh - jnp.mean(dyh, axis=-1, keepdims=True) - yh * jnp.mean(dyh * yh, axis=-1, keepdims=True))
        dy_ref[...] = dyv

        @pl.when(pl.program_id(0) == 0)
        def _():
            dlg_ref[...] = jnp.zeros_like(dlg_ref)
            dlb_ref[...] = jnp.zeros_like(dlb_ref)
            db_ref[...] = jnp.zeros_like(db_ref)

        dlg_ref[...] += jnp.sum(dn * yh, axis=0, keepdims=True)
        dlb_ref[...] += jnp.sum(dn, axis=0, keepdims=True)
        db_ref[...] += jnp.sum(dyv, axis=0, keepdims=True)

    return pl.pallas_call(
        body, name=name, grid=(T // tr,), in_specs=[row, row, vec, vec], out_specs=[row, vec, vec, vec],
        out_shape=[jax.ShapeDtypeStruct((T, D), F32)] + [jax.ShapeDtypeStruct((1, D), F32)] * 3,
        compiler_params=_params(("arbitrary",)),
    )(dz, y, ln_g, ln_b)


def _conv_bwd_dw(dy, a, dw_w, name, exchange):
    T = a.shape[0]
    tb = min(CONV_TB, T)
    per = tb // HALO
    last_halo = T // HALO - 1
    n_steps = T // tb
    n_x = len(exchange)

    def body(*refs):
        dyc_ref, dyn_ref, cur_ref, prev_ref, w_ref = refs[:5]
        x_ins = refs[5:5 + n_x]
        da_ref, dbias_ref, dw_ref = refs[5 + n_x:8 + n_x]
        x_outs = refs[8 + n_x:8 + 2 * n_x]
        uwin, dywin, dwacc = refs[8 + 2 * n_x:11 + 2 * n_x]
        sems = refs[11 + 2 * n_x:]
        i = pl.program_id(0)

        @pl.when(i == 0)
        def _():
            _exchange_start(x_ins, x_outs, sems)
            dbias_ref[...] = jnp.zeros_like(dbias_ref)
            dwacc[...] = jnp.zeros_like(dwacc)

        uwin[0, 0:HALO, :] = jnp.where(i > 0, _glu(prev_ref[...]), 0.0)
        uwin[0, HALO:, :] = _glu(cur_ref[...])
        dywin[0, 0:tb, :] = dyc_ref[...]
        dywin[0, tb:, :] = jnp.where(i < n_steps - 1, dyn_ref[...], 0.0)
        _shifted_copies(uwin)
        _shifted_copies(dywin)
        for r in range(tb // CONV_RC):
            rows = slice(r * CONV_RC, (r + 1) * CONV_RC)
            dy_c = dywin[0, rows, :]
            du = jnp.zeros((CONV_RC, D), F32)
            for k in range(CONV_W):
                du = du + _window(dywin, r * CONV_RC + (CONV_W - 1) - k) * w_ref[k:k + 1, :]
                prod = dy_c * _window(uwin, r * CONV_RC + HALO - (CONV_W - 1) + k)
                part = prod[0:8, :]
                for q in range(1, CONV_RC // 8):
                    part = part + prod[8 * q:8 * q + 8, :]
                dwacc[8 * k:8 * k + 8, :] += part
            av = cur_ref[rows, :]
            a1, sg = av[:, :D], jax.nn.sigmoid(av[:, D:])
            da1 = du * sg
            da2 = du * a1 * (sg * (1.0 - sg))
            da_ref[rows, 0:D] = da1.astype(BF16)
            da_ref[rows, D:] = da2.astype(BF16)
            dbias_ref[:, 0:D] += jnp.sum(da1, axis=0, keepdims=True)
            dbias_ref[:, D:] += jnp.sum(da2, axis=0, keepdims=True)

        @pl.when(i == n_steps - 1)
        def _():
            for k in range(CONV_W):
                dw_ref[k:k + 1, :] = jnp.sum(dwacc[8 * k:8 * k + 8, :], axis=0, keepdims=True)
            dw_ref[CONV_W:, :] = jnp.zeros((HALO - CONV_W, D), F32)
            _exchange_finish(x_ins, x_outs, sems)

    outs = pl.pallas_call(
        body, name=name, grid=(n_steps,),
        in_specs=[pl.BlockSpec((tb, D), lambda i: (i, 0)),
                  pl.BlockSpec((HALO, D), lambda i: (jnp.minimum((i + 1) * per, last_halo), 0)),
                  pl.BlockSpec((tb, 2 * D), lambda i: (i, 0)),
                  pl.BlockSpec((HALO, 2 * D), lambda i: (jnp.maximum(i * per - 1, 0), 0)),
                  pl.BlockSpec((HALO, D), lambda i: (0, 0))] + [ANY] * n_x,
        out_specs=[pl.BlockSpec((tb, 2 * D), lambda i: (i, 0)), pl.BlockSpec((1, 2 * D), lambda i: (0, 0)),
                   pl.BlockSpec((HALO, D), lambda i: (0, 0))] + [ANY] * n_x,
        out_shape=[jax.ShapeDtypeStruct((T, 2 * D), BF16), jax.ShapeDtypeStruct((1, 2 * D), F32),
                   jax.ShapeDtypeStruct((HALO, D), F32)] + [jax.ShapeDtypeStruct(p.shape, p.dtype) for p in exchange],
        scratch_shapes=[pltpu.VMEM((SUBLANES, tb + HALO, D), F32), pltpu.VMEM((SUBLANES, tb + HALO, D), F32),
                        pltpu.VMEM((8 * HALO, D), F32)] + _exchange_sems(n_x),
        compiler_params=_params(("arbitrary",)),
    )(dy, dy, a, a, dw_w, *exchange)
    return outs[0], outs[1], outs[2], outs[3:]


GATE_TB = 512


def _gate_fwd(fl, b_f, name):
    T = fl.shape[0]
    tb = min(GATE_TB, T)
    row = pl.BlockSpec((tb, LANES), lambda i: (i, 0))

    def body(fl_ref, b_ref, f_ref, f1_ref, f2_ref, f3_ref, carry):
        @pl.when(pl.program_id(0) == 0)
        def _():
            carry[...] = jnp.zeros_like(carry)

        z = fl_ref[...] + b_ref[...]
        lf = jnp.minimum(z, 0.0) - jnp.log(1.0 + jnp.exp(-jnp.abs(z)))
        tri = (lax.broadcasted_iota(jnp.int32, (tb, tb), 0) >= lax.broadcasted_iota(jnp.int32, (tb, tb), 1)).astype(F32)
        f = jnp.dot(tri, lf, precision=lax.Precision.HIGHEST, preferred_element_type=F32) + carry[...]
        carry[...] = f[tb - 1:tb, :]
        f_ref[...] = f
        f1_ref[...], f2_ref[...], f3_ref[...] = _split3(f)

    return pl.pallas_call(
        body, name=name, grid=(T // tb,), in_specs=[row, pl.BlockSpec((1, LANES), lambda i: (0, 0))],
        out_specs=[row] * 4, out_shape=[jax.ShapeDtypeStruct((T, LANES), F32)] * 4,
        scratch_shapes=[pltpu.VMEM((1, LANES), F32)],
        compiler_params=_params(("arbitrary",)),
    )(fl, b_f)


def _gate_bwd(dF_q, dF_k, fl, b_f, name):
    T = fl.shape[0]
    tb = min(GATE_TB, T)
    nb = T // tb
    row = pl.BlockSpec((tb, LANES), lambda i: (nb - 1 - i, 0))
    vec = pl.BlockSpec((1, LANES), lambda i: (0, 0))

    def body(dfq_ref, dfk_ref, fl_ref, b_ref, dfl_ref, db_ref, carry):
        @pl.when(pl.program_id(0) == 0)
        def _():
            carry[...] = jnp.zeros_like(carry)
            db_ref[...] = jnp.zeros_like(db_ref)

        tri = (lax.broadcasted_iota(jnp.int32, (tb, tb), 0) <= lax.broadcasted_iota(jnp.int32, (tb, tb), 1)).astype(F32)
        rc = jnp.dot(tri, dfq_ref[...] - dfk_ref[...], precision=lax.Precision.HIGHEST,
                     preferred_element_type=F32) + carry[...]
        carry[...] = rc[0:1, :]
        dfl = rc * jax.nn.sigmoid(-(fl_ref[...] + b_ref[...]))
        dfl_ref[...] = dfl
        db_ref[...] += jnp.sum(dfl, axis=0, keepdims=True)

    return pl.pallas_call(
        body, name=name, grid=(nb,), in_specs=[row, row, row, vec], out_specs=[row, vec],
        out_shape=[jax.ShapeDtypeStruct((T, LANES), F32), jax.ShapeDtypeStruct((1, LANES), F32)],
        scratch_shapes=[pltpu.VMEM((1, LANES), F32)],
        compiler_params=_params(("arbitrary",)),
    )(dF_q, dF_k, fl, b_f)


ATT_TB = 512
LAYOUT_TB = 2048
NT = (((1,), (1,)), ((), ()))
NN = (((1,), (0,)), ((), ()))


def _lane_insert(base, parts, first_lane):
    lane = lax.broadcasted_iota(jnp.int32, base.shape, 1)
    out = base
    for n, p in enumerate(parts):
        out = jnp.where(lane == first_lane + n, p, out)
    return out


def _causal(tb):
    return lax.broadcasted_iota(jnp.int32, (tb, tb), 0) >= lax.broadcasted_iota(jnp.int32, (tb, tb), 1)


def _other_head(x):
    return pltpu.roll(x, HEAD_DIM, 1)


def _attn_pack(proj, f1, f2, f3, name):
    T = proj.shape[0]
    tb = min(LAYOUT_TB, T)
    pairs = N_HEADS // 2
    cols = lambda first: pl.BlockSpec((tb, LANES), lambda p, i: (i, first + p))
    gate = pl.BlockSpec((tb, LANES), lambda p, i: (i, 0))
    heads = pl.BlockSpec((2, tb, LANES), lambda p, i: (p, i, 0))
    norm = pl.BlockSpec((2, 8, LANES), lambda p, i: (p, 0, 0))

    def body(q_ref, k_ref, v_ref, f1_ref, f2_ref, f3_ref, qa_ref, ka_ref, va_ref, qn_ref, kn_ref):
        p, i = pl.program_id(0), pl.program_id(1)
        lane = lax.broadcasted_iota(jnp.int32, (tb, LANES), 1)
        data = lane < HEAD_DIM

        @pl.when(i == 0)
        def _():
            qn_ref[...] = jnp.zeros_like(qn_ref)
            kn_ref[...] = jnp.zeros_like(kn_ref)

        qv, kv, vv = q_ref[...] * 0.125, k_ref[...], v_ref[...]
        for e in range(2):
            fcol = [jnp.sum(jnp.where(lane == 2 * p + e, f[...], 0.0), axis=-1, keepdims=True)
                    for f in (f1_ref, f2_ref, f3_ref)]
            pick = (lambda t: t) if e == 0 else _other_head
            qd = jnp.where(data, pick(qv), 0.0).astype(BF16).astype(F32)
            kd = jnp.where(data, pick(kv), 0.0).astype(BF16).astype(F32)
            vd = jnp.where(data, pick(vv), 0.0)
            qa_ref[e] = _lane_insert(qd, fcol + [1.0, 1.0, 1.0], AUG_F).astype(BF16)
            ka_ref[e] = _lane_insert(kd, [1.0, 1.0, 1.0] + [-f for f in fcol] + [1.0, 1.0, 1.0], AUG_F).astype(BF16)
            va_ref[e] = _lane_insert(vd, [-1.0, -1.0, -1.0, 1.0], AUG_F).astype(BF16)
            qn_ref[e] = jnp.maximum(qn_ref[e], jnp.max(jnp.sum(qd * qd, axis=-1, keepdims=True)))
            kn_ref[e] = jnp.maximum(kn_ref[e], jnp.max(jnp.sum(kd * kd, axis=-1, keepdims=True)))

    aug = jax.ShapeDtypeStruct((N_HEADS, T, LANES), BF16)
    nrm = jax.ShapeDtypeStruct((N_HEADS, 8, LANES), F32)
    return pl.pallas_call(
        body, name=name, grid=(pairs, T // tb),
        in_specs=[cols(0), cols(pairs), cols(2 * pairs), gate, gate, gate],
        out_specs=[heads, heads, heads, norm, norm], out_shape=[aug, aug, aug, nrm, nrm],
        compiler_params=_params(("parallel", "arbitrary")),
    )(proj, proj, proj, f1, f2, f3)


def _merge_heads(x_aug, scale, out_dtype, name, column=None):
    H, T, _ = x_aug.shape
    tb = min(LAYOUT_TB, T)

    def body(x_ref, o_ref, *col_ref):
        p = pl.program_id(1)
        lane = lax.broadcasted_iota(jnp.int32, (tb, LANES), 1)
        x0, x1 = x_ref[0], x_ref[1]
        o_ref[...] = (jnp.where(lane < HEAD_DIM, x0, _other_head(x1)) * scale).astype(out_dtype)
        if column is not None:
            @pl.when(p == 0)
            def _():
                col_ref[0][...] = jnp.zeros_like(col_ref[0])

            c0 = jnp.sum(jnp.where(lane == column, x0, 0.0), axis=-1, keepdims=True)
            c1 = jnp.sum(jnp.where(lane == column, x1, 0.0), axis=-1, keepdims=True)
            col_ref[0][...] += jnp.where(lane == 2 * p, c0, 0.0) + jnp.where(lane == 2 * p + 1, c1, 0.0)

    out_specs = [pl.BlockSpec((tb, LANES), lambda i, p: (i, p))]
    out_shape = [jax.ShapeDtypeStruct((T, D), out_dtype)]
    if column is not None:
        out_specs.append(pl.BlockSpec((tb, LANES), lambda i, p: (i, 0)))
        out_shape.append(jax.ShapeDtypeStruct((T, LANES), F32))
    return pl.pallas_call(
        body, name=name, grid=(T // tb, H // 2),
        in_specs=[pl.BlockSpec((2, tb, LANES), lambda i, p: (p, i, 0))],
        out_specs=out_specs, out_shape=out_shape,
        compiler_params=_params(("parallel", "arbitrary")),
    )(x_aug)


def _attn_prep(d_o, o_aug, name):
    H, T, _ = o_aug.shape
    tb = min(LAYOUT_TB, T)
    heads = pl.BlockSpec((2, tb, LANES), lambda p, i: (p, i, 0))

    def body(do_ref, o_ref, out_ref):
        lane = lax.broadcasted_iota(jnp.int32, (tb, LANES), 1)
        dov = do_ref[...]
        for e in range(2):
            d_e = jnp.where(lane < HEAD_DIM, dov if e == 0 else _other_head(dov), 0.0)
            delta = jnp.sum(d_e * o_ref[e], axis=-1, keepdims=True)
            out_ref[e] = _lane_insert(d_e, _split3(delta), AUG_F).astype(BF16)

    return pl.pallas_call(
        body, name=name, grid=(H // 2, T // tb),
        in_specs=[pl.BlockSpec((tb, LANES), lambda p, i: (i, p)), heads], out_specs=heads,
        out_shape=jax.ShapeDtypeStruct((H, T, LANES), BF16),
        compiler_params=_params(("parallel", "parallel")),
    )(d_o, o_aug)


def _flash_spec(n_in, n_out, H, nb, tb, T, resident, scratch, n_tables=3):
    blk = pl.BlockSpec((2, tb, LANES), lambda h, i, *_: (h, i, 0))
    whole = pl.BlockSpec((2, T, LANES), lambda h, i, *_: (h, 0, 0))
    return pltpu.PrefetchScalarGridSpec(
        num_scalar_prefetch=n_tables, grid=(H // 2, nb),
        in_specs=[whole if resident[n] else blk for n in range(n_in)],
        out_specs=[blk] * n_out, scratch_shapes=scratch)


def _first_live(qk_ref, fs_ref, fe_ref, h, i):
    top = qk_ref[h] + fs_ref[h, i]
    return lax.fori_loop(0, i, lambda j, n: n + jnp.where(top - fe_ref[h, j] < SKIP_BELOW, 1, 0), 0)


def _flash_fwd(qk, fs, fe, kmax, q_aug, k_aug, v_aug, name):
    H, T, _ = q_aug.shape
    tb = min(ATT_TB, T)

    def body(qk_ref, fs_ref, fe_ref, km_ref, q_ref, k_ref, v_ref, o_ref, q2_ref, m_s, acc_s, qm_s):
        h, i = 2 * pl.program_id(0), pl.program_id(1)
        acc_s[...] = jnp.zeros_like(acc_s)
        first = jnp.minimum(_first_live(qk_ref, fs_ref, fe_ref, h, i), _first_live(qk_ref, fs_ref, fe_ref, h + 1, i))
        bounded = jnp.logical_and(qk_ref[h] < FIXED_MAX_BELOW, qk_ref[h + 1] < FIXED_MAX_BELOW)

        def sweep(step):
            def loop_body(kb, carry):
                step(kb, False)
                return carry

            lax.fori_loop(first, i, loop_body, 0)
            step(i, True)

        @pl.when(bounded)
        def _():
            lane = lax.broadcasted_iota(jnp.int32, (tb, LANES), 1)
            for e in range(2):
                qf = q_ref[e].astype(F32)
                norm = jnp.sqrt(jnp.sum(jnp.where(lane < HEAD_DIM, qf * qf, 0.0), axis=-1, keepdims=True))
                bound = norm * (1.01 * km_ref[h + e]) + 1e-3
                m_s[e] = bound
                qm_s[e] = _lane_insert(qf, [-p for p in _split3(bound)], AUG_L).astype(BF16)

            def step(kb, masked):
                rows = pl.ds(pl.multiple_of(kb * tb, tb), tb)
                for e in range(2):
                    p = jnp.exp(lax.dot_general(qm_s[e], k_ref[e, rows, :], NT, preferred_element_type=F32))
                    if masked:
                        p = jnp.where(_causal(tb), p, 0.0)
                    acc_s[e] += lax.dot_general(p.astype(BF16), v_ref[e, rows, :], NN, preferred_element_type=F32)

            sweep(step)

        @pl.when(jnp.logical_not(bounded))
        def _():
            m_s[...] = jnp.full(m_s.shape, -jnp.inf, F32)

            def step(kb, masked):
                rows = pl.ds(pl.multiple_of(kb * tb, tb), tb)
                for e in range(2):
                    s = lax.dot_general(q_ref[e], k_ref[e, rows, :], NT, preferred_element_type=F32)
                    if masked:
                        s = jnp.where(_causal(tb), s, MASK_VALUE)
                    m_old = m_s[e]
                    m_new = jnp.maximum(m_old, jnp.max(s, axis=-1, keepdims=True))
                    p = jnp.exp(s - m_new)
                    acc_s[e] = jnp.exp(m_old - m_new) * acc_s[e] + lax.dot_general(
                        p.astype(BF16), v_ref[e, rows, :], NN, preferred_element_type=F32)
                    m_s[e] = m_new

            sweep(step)

        for e in range(2):
            acc = acc_s[e]
            lane = lax.broadcasted_iota(jnp.int32, acc.shape, 1)
            l = jnp.sum(jnp.where(lane == AUG_ONE, acc, 0.0), axis=-1, keepdims=True)
            o_ref[e] = acc / l
            lse = m_s[e] + jnp.log(l)
            q2_ref[e] = _lane_insert(q_ref[e].astype(F32), [-p for p in _split3(lse)], AUG_L).astype(BF16)

    return pl.pallas_call(
        body, name=name,
        grid_spec=_flash_spec(3, 2, H, T // tb, tb, T, (False, True, True),
                              [pltpu.VMEM((2, tb, 1), F32), pltpu.VMEM((2, tb, LANES), F32),
                               pltpu.VMEM((2, tb, LANES), BF16)], n_tables=4),
        out_shape=[jax.ShapeDtypeStruct((H, T, LANES), F32), jax.ShapeDtypeStruct((H, T, LANES), BF16)],
        compiler_params=_params(("parallel", "arbitrary")),
    )(qk, fs, fe, kmax, q_aug, k_aug, v_aug)


def _flash_dq(qk, fs, fe, q2, k_aug, do_aug, v_aug, name):
    H, T, _ = q2.shape
    tb = min(ATT_TB, T)

    def body(qk_ref, fs_ref, fe_ref, q_ref, k_ref, do_ref, v_ref, dq_ref, acc_s):
        h, i = 2 * pl.program_id(0), pl.program_id(1)
        acc_s[...] = jnp.zeros_like(acc_s)

        def step(kb, masked):
            rows = pl.ds(pl.multiple_of(kb * tb, tb), tb)
            for e in range(2):
                k = k_ref[e, rows, :]
                p = jnp.exp(lax.dot_general(q_ref[e], k, NT, preferred_element_type=F32))
                if masked:
                    p = jnp.where(_causal(tb), p, 0.0)
                ds = p * lax.dot_general(do_ref[e], v_ref[e, rows, :], NT, preferred_element_type=F32)
                acc_s[e] += lax.dot_general(ds.astype(BF16), k, NN, preferred_element_type=F32)

        def loop_body(kb, carry):
            step(kb, False)
            return carry

        first = jnp.minimum(_first_live(qk_ref, fs_ref, fe_ref, h, i), _first_live(qk_ref, fs_ref, fe_ref, h + 1, i))
        lax.fori_loop(first, i, loop_body, 0)
        step(i, True)
        dq_ref[...] = acc_s[...]

    return pl.pallas_call(
        body, name=name,
        grid_spec=_flash_spec(4, 1, H, T // tb, tb, T, (False, True, False, True), [pltpu.VMEM((2, tb, LANES), F32)]),
        out_shape=[jax.ShapeDtypeStruct((H, T, LANES), F32)],
        compiler_params=_params(("parallel", "arbitrary")),
    )(qk, fs, fe, q2, k_aug, do_aug, v_aug)[0]


def _flash_dkv(qk, fs, fe, q2, k_aug, do_aug, v_aug, name):
    H, T, _ = q2.shape
    tb = min(ATT_TB, T)
    nb = T // tb

    def body(qk_ref, fs_ref, fe_ref, q_ref, k_ref, do_ref, v_ref, dk_ref, dv_ref, dk_s, dv_s):
        h, i = 2 * pl.program_id(0), pl.program_id(1)
        dk_s[...] = jnp.zeros_like(dk_s)
        dv_s[...] = jnp.zeros_like(dv_s)

        def step(qb, masked):
            rows = pl.ds(pl.multiple_of(qb * tb, tb), tb)
            for e in range(2):
                q, dov = q_ref[e, rows, :], do_ref[e, rows, :]
                pt = jnp.exp(lax.dot_general(k_ref[e], q, NT, preferred_element_type=F32))
                if masked:
                    pt = jnp.where(
                        lax.broadcasted_iota(jnp.int32, (tb, tb), 1) >= lax.broadcasted_iota(jnp.int32, (tb, tb), 0),
                        pt, 0.0)
                dst = pt * lax.dot_general(v_ref[e], dov, NT, preferred_element_type=F32)
                dv_s[e] += lax.dot_general(pt.astype(BF16), dov, NN, preferred_element_type=F32)
                dk_s[e] += lax.dot_general(dst.astype(BF16), q, NN, preferred_element_type=F32)

        step(i, True)

        def loop_body(qb, carry):
            step(qb, False)
            return carry

        def live_after(head):
            base = qk_ref[head] - fe_ref[head, i]
            return lax.fori_loop(i + 1, nb, lambda b, n: n + jnp.where(base + fs_ref[head, b] < SKIP_BELOW, 0, 1), 0)

        lax.fori_loop(i + 1, i + 1 + jnp.maximum(live_after(h), live_after(h + 1)), loop_body, 0)
        dk_ref[...] = dk_s[...]
        dv_ref[...] = dv_s[...]

    return pl.pallas_call(
        body, name=name,
        grid_spec=_flash_spec(4, 2, H, nb, tb, T, (True, False, True, False),
                              [pltpu.VMEM((2, tb, LANES), F32), pltpu.VMEM((2, tb, LANES), F32)]),
        out_shape=[jax.ShapeDtypeStruct((H, T, LANES), F32)] * 2,
        compiler_params=_params(("parallel", "arbitrary")),
    )(qk, fs, fe, q2, k_aug, do_aug, v_aug)


def _add_pair(place, slab, got, name):
    _, half, cols = got.shape
    grid_spec = pltpu.PrefetchScalarGridSpec(
        num_scalar_prefetch=1, grid=(N_CHIPS,),
        in_specs=[pl.BlockSpec((None, half, cols), lambda j, s: (j, s[4], 0)),
                  pl.BlockSpec((None, half, cols), lambda j, s: (j, 0, 0))],
        out_specs=pl.BlockSpec((None, half, cols), lambda j, s: (j, 0, 0)))

    def body(s_ref, a_ref, b_ref, o_ref):
        o_ref[...] = (a_ref[...] + b_ref[...]).astype(BF16)

    return pl.pallas_call(
        body, name=name, grid_spec=grid_spec, out_shape=jax.ShapeDtypeStruct(got.shape, BF16),
        compiler_params=_params(("parallel",)),
    )(place, slab, got)


def _add_chips(place, pair, by_chip, name):
    _, half, cols = pair.shape
    tb = min(256, half)
    steps = half // tb
    slot = lambda n: pl.BlockSpec((None, tb, cols), lambda i, s: (s[n], i, 0))
    grid_spec = pltpu.PrefetchScalarGridSpec(
        num_scalar_prefetch=1, grid=(steps,), in_specs=[slot(0), slot(1), slot(2), slot(3)],
        out_specs=pl.BlockSpec((tb, cols), lambda i, s: (s[4] * steps + i, 0)))

    def body(s_ref, own_ref, b1_ref, b2_ref, b3_ref, o_ref):
        o_ref[...] = ((own_ref[...].astype(F32) + b1_ref[...].astype(F32)) + b2_ref[...].astype(F32)) \
            + b3_ref[...].astype(F32)

    return pl.pallas_call(
        body, name=name, grid_spec=grid_spec, out_shape=jax.ShapeDtypeStruct((2 * half, cols), F32),
        compiler_params=_params(("parallel",)),
    )(place, pair, by_chip, by_chip, by_chip)


def _add_slots(b, name, tb=128):
    n, rows, cols = b.shape
    tb = tb if rows % tb == 0 else rows
    specs = [pl.BlockSpec((None, tb, cols), functools.partial(lambda j, i: (j, i, 0), j)) for j in range(n)]

    def body(*refs):
        acc = refs[0][...]
        for r in refs[1:n]:
            acc = acc + r[...]
        refs[n][...] = acc

    return pl.pallas_call(
        body, name=name, grid=(rows // tb,), in_specs=specs, out_specs=pl.BlockSpec((tb, cols), lambda i: (i, 0)),
        out_shape=jax.ShapeDtypeStruct((rows, cols), F32), compiler_params=_params(("parallel",)),
    )(*([b] * n))


def _adamw(w, g, m, v, name):
    rows, cols = w.shape
    tr = 256 if rows % 256 == 0 else rows
    blk = pl.BlockSpec((tr, cols), lambda i: (i, 0))

    def body(w_ref, g_ref, m_ref, v_ref, d_ref, nm_ref, nv_ref):
        gv = g_ref[...]
        nm = ADAM_B1 * m_ref[...] + (1.0 - ADAM_B1) * gv
        nv = ADAM_B2 * v_ref[...] + (1.0 - ADAM_B2) * (gv * gv)
        m_hat = nm / (1.0 - ADAM_B1 ** ADAM_STEP)
        v_hat = nv / (1.0 - ADAM_B2 ** ADAM_STEP)
        d_ref[...] = -ADAM_LR * (m_hat / (jnp.sqrt(v_hat) + ADAM_EPS) + ADAM_WD * w_ref[...])
        nm_ref[...] = nm
        nv_ref[...] = nv

    return pl.pallas_call(
        body, name=name, grid=(rows // tr,), in_specs=[blk] * 4, out_specs=[blk] * 3,
        out_shape=[jax.ShapeDtypeStruct((rows, cols), F32)] * 3, compiler_params=_params(("parallel",)),
    )(w, g, m, v)


ANY = pl.BlockSpec(memory_space=pl.ANY)


def _place():
    x, y, c = lax.axis_index("x"), lax.axis_index("y"), lax.axis_index("c")
    others = [(1 - x, y), (x, 1 - y), (1 - x, 1 - y)]
    return x, y, c, 2 * x + y, others


def _half(ref, rows, h):
    return ref.at[pl.ds(h * (rows // 2), rows // 2), :]


def _gather_sems(n_u):
    return [pltpu.SemaphoreType.DMA((6 * n_u,)), pltpu.SemaphoreType.DMA((6 * n_u,)),
            pltpu.SemaphoreType.DMA((n_u,)), pltpu.SemaphoreType.DMA((n_u,))]


def _gather_copies(units, ins, outs, sems):
    send_sems, recv_sems, own_send_sems, own_recv_sems = sems
    x, y, c, me, others = _place()
    sibling = (x, y, 1 - c)

    def copy(u, k, chip, h, to, src=None):
        dst = _half(outs[u].at[chip], units[u][1][0], h)
        return pltpu.make_async_remote_copy(
            src_ref=dst if src is None else src, dst_ref=dst, send_sem=send_sems.at[6 * u + k],
            recv_sem=recv_sems.at[6 * u + k], device_id=to, device_id_type=MESH)

    pairs = [(u, j, 2 * ox + oy, (ox, oy)) for u in range(len(units)) for j, (ox, oy) in enumerate(others)]
    own = [pltpu.make_async_remote_copy(
        src_ref=ins[u], dst_ref=outs[u].at[me], send_sem=own_send_sems.at[u], recv_sem=own_recv_sems.at[u],
        device_id=sibling, device_id_type=MESH) for u in range(len(units))]
    first = [copy(u, j, me, c, (*to, c), src=_half(ins[u], units[u][1][0], c)) for u, j, _, to in pairs]
    arrivals = [copy(u, j, chip, c, (x, y, c)) for u, j, chip, _ in pairs]
    passed = [copy(u, 3 + j, chip, c, sibling) for u, j, chip, _ in pairs]
    from_sibling = [copy(u, 3 + j, chip, 1 - c, (x, y, c)) for u, j, chip, _ in pairs]
    return own, first, arrivals, passed, from_sibling


def _gather_start(units, ins, outs, sems):
    own, first, _, _, _ = _gather_copies(units, ins, outs, sems)
    for cp in own + first:
        cp.start()


def _gather_finish(units, ins, outs, sems):
    own, first, arrivals, passed, from_sibling = _gather_copies(units, ins, outs, sems)
    for arrived, onward in zip(arrivals, passed):
        arrived.wait_recv()
        onward.start()
    for cp in from_sibling:
        cp.wait_recv()
    for cp in first + passed:
        cp.wait_send()
    for cp in own:
        cp.wait()


def _allgather_weights(units, shards, name):
    n_u = len(units)

    def body(*refs):
        ins, outs, sems = refs[:n_u], refs[n_u:2 * n_u], refs[2 * n_u:]
        _gather_start(units, ins, outs, sems)
        _gather_finish(units, ins, outs, sems)

    return pl.pallas_call(
        body, name=name, in_specs=[ANY] * n_u, out_specs=[ANY] * n_u,
        out_shape=[jax.ShapeDtypeStruct((N_CHIPS,) + shape, BF16) for _, shape in units],
        scratch_shapes=_gather_sems(n_u),
    )(*shards)


def _sibling_exchange(units, slabs, name):
    n_u = len(units)

    def body(*refs):
        ins, got = refs[:n_u], refs[n_u:2 * n_u]
        send_sems, recv_sems = refs[2 * n_u:]
        x, y, c, _, _ = _place()
        copies = []
        for u, (_, (rows, _), _) in enumerate(units):
            half = rows // 2
            copies.append(pltpu.make_async_remote_copy(
                src_ref=ins[u].at[:, pl.ds((1 - c) * half, half), :], dst_ref=got[u], send_sem=send_sems.at[u],
                recv_sem=recv_sems.at[u], device_id=(x, y, 1 - c), device_id_type=MESH))
        for cp in copies:
            cp.start()
        for cp in copies:
            cp.wait()

    return pl.pallas_call(
        body, name=name, in_specs=[ANY] * n_u, out_specs=[ANY] * n_u,
        out_shape=[jax.ShapeDtypeStruct((N_CHIPS, rows // 2, cols), F32) for _, (rows, cols), _ in units],
        scratch_shapes=[pltpu.SemaphoreType.DMA((n_u,)), pltpu.SemaphoreType.DMA((n_u,))],
    )(*slabs)


def _exchange_sems(n_u):
    return [pltpu.SemaphoreType.DMA((3 * n_u,)), pltpu.SemaphoreType.DMA((3 * n_u,))]


def _exchange_copies(ins, outs, sems):
    send_sems, recv_sems = sems
    x, y, c, me, others = _place()
    sends = [pltpu.make_async_remote_copy(
        src_ref=ins[u].at[2 * ox + oy], dst_ref=outs[u].at[me], send_sem=send_sems.at[3 * u + j],
        recv_sem=recv_sems.at[3 * u + j], device_id=(ox, oy, c), device_id_type=MESH)
        for u in range(len(ins)) for j, (ox, oy) in enumerate(others)]
    arrivals = [pltpu.make_async_remote_copy(
        src_ref=ins[u].at[me], dst_ref=outs[u].at[2 * ox + oy], send_sem=send_sems.at[3 * u + j],
        recv_sem=recv_sems.at[3 * u + j], device_id=(x, y, c), device_id_type=MESH)
        for u in range(len(ins)) for j, (ox, oy) in enumerate(others)]
    return sends, arrivals


def _exchange_start(ins, outs, sems):
    for cp in _exchange_copies(ins, outs, sems)[0]:
        cp.start()


def _exchange_finish(ins, outs, sems):
    sends, arrivals = _exchange_copies(ins, outs, sems)
    for cp in arrivals:
        cp.wait_recv()
    for cp in sends:
        cp.wait_send()


def _chip_exchange(pairs, small):
    n_u = len(pairs)

    def body(*refs):
        ins, s_ref = refs[:n_u], refs[n_u]
        outs, sall_ref = refs[n_u + 1:2 * n_u + 1], refs[2 * n_u + 1]
        send_sems, recv_sems, ssend_sems, srecv_sems, local_sem = refs[2 * n_u + 2:]
        x, y, c, _, _ = _place()
        dev = 4 * x + 2 * y + c
        local = pltpu.make_async_copy(s_ref, sall_ref.at[dev], local_sem)
        flips = [(fx, fy, fc) for fx in (0, 1) for fy in (0, 1) for fc in (0, 1)][1:]
        small_sends = [pltpu.make_async_remote_copy(
            src_ref=s_ref, dst_ref=sall_ref.at[dev], send_sem=ssend_sems.at[n], recv_sem=srecv_sems.at[n],
            device_id=(x ^ fx, y ^ fy, c ^ fc), device_id_type=MESH) for n, (fx, fy, fc) in enumerate(flips)]
        local.start()
        _exchange_start(ins, outs, (send_sems, recv_sems))
        for cp in small_sends:
            cp.start()
        _exchange_finish(ins, outs, (send_sems, recv_sems))
        for n, (fx, fy, fc) in enumerate(flips):
            src_dev = 4 * (x ^ fx) + 2 * (y ^ fy) + (c ^ fc)
            pltpu.make_async_remote_copy(
                src_ref=s_ref, dst_ref=sall_ref.at[src_dev], send_sem=ssend_sems.at[n], recv_sem=srecv_sems.at[n],
                device_id=(x, y, c), device_id_type=MESH).wait_recv()
        for cp in small_sends:
            cp.wait_send()
        local.wait()

    outs = pl.pallas_call(
        body, name="grad_chip_exchange", in_specs=[ANY] * (n_u + 1), out_specs=[ANY] * (n_u + 1),
        out_shape=[jax.ShapeDtypeStruct(p.shape, p.dtype) for p in pairs]
        + [jax.ShapeDtypeStruct((8, SMALL_ALL_ROWS, LANES), F32)],
        scratch_shapes=_exchange_sems(n_u) + [pltpu.SemaphoreType.DMA((7,)), pltpu.SemaphoreType.DMA((7,)),
                                              pltpu.SemaphoreType.DMA],
    )(*pairs, small)
    return outs[:n_u], outs[n_u]


def _sibling_share(grads):
    n_u = len(GRAD_UNITS)

    def body(*refs):
        ins, outs = refs[:n_u], refs[n_u:2 * n_u]
        send_sems, recv_sems = refs[2 * n_u:]
        x, y, c, _, _ = _place()

        def rows_of(ref, u, h):
            half = GRAD_UNITS[u][1][0] // 2
            return ref.at[pl.ds(h * half, half), :]

        sends = [pltpu.make_async_remote_copy(
            src_ref=rows_of(ins[u], u, c), dst_ref=rows_of(outs[u], u, c), send_sem=send_sems.at[u],
            recv_sem=recv_sems.at[u], device_id=(x, y, 1 - c), device_id_type=MESH) for u in range(n_u)]
        for cp in sends:
            cp.start()
        for u in range(n_u):
            pltpu.make_async_remote_copy(
                src_ref=rows_of(ins[u], u, c), dst_ref=rows_of(outs[u], u, 1 - c), send_sem=send_sems.at[u],
                recv_sem=recv_sems.at[u], device_id=(x, y, c), device_id_type=MESH).wait_recv()
        for cp in sends:
            cp.wait_send()

    return pl.pallas_call(
        body, name="grad_sibling_share", in_specs=[ANY] * n_u, out_specs=[ANY] * n_u,
        out_shape=[jax.ShapeDtypeStruct(g.shape, F32) for g in grads],
        input_output_aliases={u: u for u in range(n_u)},
        scratch_shapes=[pltpu.SemaphoreType.DMA((n_u,)), pltpu.SemaphoreType.DMA((n_u,))],
    )(*grads)


def kernel(x, g_mix_pre, g_mix_post, g_ffn_pre, g_ffn_post, conv_pw1_w, conv_pw1_b, conv_dw_w, conv_dw_b, conv_ln_g, conv_ln_b, conv_pw2_w, conv_pw2_b, attn_w_in, attn_b_f, attn_w_o, mlp_w_up, mlp_w_down, loss_target, m_g_mix_pre, m_g_mix_post, m_g_ffn_pre, m_g_ffn_post, m_conv_pw1_w, m_conv_pw1_b, m_conv_dw_w, m_conv_dw_b, m_conv_ln_g, m_conv_ln_b, m_conv_pw2_w, m_conv_pw2_b, m_attn_w_in, m_attn_b_f, m_attn_w_o, m_mlp_w_up, m_mlp_w_down, v_g_mix_pre, v_g_mix_post, v_g_ffn_pre, v_g_ffn_post, v_conv_pw1_w, v_conv_pw1_b, v_conv_dw_w, v_conv_dw_b, v_conv_ln_g, v_conv_ln_b, v_conv_pw2_w, v_conv_pw2_b, v_attn_w_in, v_attn_b_f, v_attn_w_o, v_mlp_w_up, v_mlp_w_down):
    weights = dict(g_mix_pre=g_mix_pre, g_mix_post=g_mix_post, g_ffn_pre=g_ffn_pre, g_ffn_post=g_ffn_post, conv_pw1_w=conv_pw1_w, conv_pw1_b=conv_pw1_b, conv_dw_w=conv_dw_w, conv_dw_b=conv_dw_b, conv_ln_g=conv_ln_g, conv_ln_b=conv_ln_b, conv_pw2_w=conv_pw2_w, conv_pw2_b=conv_pw2_b, attn_w_in=attn_w_in, attn_b_f=attn_b_f, attn_w_o=attn_w_o, mlp_w_up=mlp_w_up, mlp_w_down=mlp_w_down)
    mom_m = dict(g_mix_pre=m_g_mix_pre, g_mix_post=m_g_mix_post, g_ffn_pre=m_g_ffn_pre, g_ffn_post=m_g_ffn_post, conv_pw1_w=m_conv_pw1_w, conv_pw1_b=m_conv_pw1_b, conv_dw_w=m_conv_dw_w, conv_dw_b=m_conv_dw_b, conv_ln_g=m_conv_ln_g, conv_ln_b=m_conv_ln_b, conv_pw2_w=m_conv_pw2_w, conv_pw2_b=m_conv_pw2_b, attn_w_in=m_attn_w_in, attn_b_f=m_attn_b_f, attn_w_o=m_attn_w_o, mlp_w_up=m_mlp_w_up, mlp_w_down=m_mlp_w_down)
    mom_v = dict(g_mix_pre=v_g_mix_pre, g_mix_post=v_g_mix_post, g_ffn_pre=v_g_ffn_pre, g_ffn_post=v_g_ffn_post, conv_pw1_w=v_conv_pw1_w, conv_pw1_b=v_conv_pw1_b, conv_dw_w=v_conv_dw_w, conv_dw_b=v_conv_dw_b, conv_ln_g=v_conv_ln_g, conv_ln_b=v_conv_ln_b, conv_pw2_w=v_conv_pw2_w, conv_pw2_b=v_conv_pw2_b, attn_w_in=v_attn_w_in, attn_b_f=v_attn_b_f, attn_w_o=v_attn_w_o, mlp_w_up=v_mlp_w_up, mlp_w_down=v_mlp_w_down)
    order = [n for n, _ in SMALL[:4]] + ["conv_pw1_w", "conv_pw1_b", "conv_dw_w", "conv_dw_b", "conv_ln_g", "conv_ln_b",
                                          "conv_pw2_w", "conv_pw2_b", "attn_w_in", "attn_b_f", "attn_w_o", "mlp_w_up",
                                          "mlp_w_down"]
    T = x.shape[1]
    x0 = x.reshape(T, D)
    target = loss_target.reshape(T, D)
    row = lambda a, l: a[l:l + 1, :]

    def shard(unit):
        if unit[:-1] in ("up", "down"):
            return weights["mlp_w_" + unit[:-1]][int(unit[-1])].astype(BF16)
        w = weights[unit][0].astype(BF16)
        return jnp.pad(w, ((0, HALO - CONV_W), (0, 0))) if unit == "conv_dw_w" else w

    by_cols = lambda g: g.transpose(1, 0, 2).reshape(g.shape[1], N_CHIPS * g.shape[2])
    by_rows = lambda g: g.reshape(N_CHIPS * g.shape[1], g.shape[2])
    g_pw1, g_dw, g_pw2, g_up0, g_down0 = _allgather_weights(
        GATHER_CONV_LAYER, [shard(n) for n, _ in GATHER_CONV_LAYER], "allgather_conv_layer")
    w_pw1, w_pw2, w_dw = by_cols(g_pw1), by_rows(g_pw2), by_cols(g_dw).astype(F32)
    b_f = jnp.pad(attn_b_f, ((0, 0), (0, LANES - N_HEADS)))

    def mlp_fwd(h, l):
        (act,) = _matmul(h, w_up[l], "nn", (BF16,), _epi_sqrelu, f"mlp{l}_up")
        (mo,) = _matmul(act, w_down[l], "nn", (F32,), _epi_plain, f"mlp{l}_down", tk=DFF)
        return act, mo

    (h0,) = _resnorm_fwd(x0, None, None, row(g_mix_pre, 0), "norm_in")
    (a0,) = _matmul(h0, w_pw1, "nn", (F32,), _epi_bias, "conv_pw1", extras=((conv_pw1_b, "row"),))
    y0, z0, (g_win, g_wo, g_up1, g_down1) = _conv_fwd(
        a0, w_dw, conv_dw_b, conv_ln_g, conv_ln_b, "conv_fwd", GATHER_ATTN_LAYER,
        [shard(n) for n, _ in GATHER_ATTN_LAYER])
    w_up, w_down = [by_cols(g_up0), by_cols(g_up1)], [by_rows(g_down0), by_rows(g_down1)]
    w_o = by_rows(g_wo)
    w_in = jnp.pad(by_cols(g_win), ((0, 0), (0, W_IN_PAD - W_IN_COLS)))
    (m0,) = _matmul(z0, w_pw2, "nn", (F32,), _epi_bias, "conv_pw2", extras=((conv_pw2_b, "row"),))
    x1, h1 = _resnorm_fwd(x0, m0, row(g_mix_post, 0), row(g_ffn_pre, 0), "norm_conv_out")
    act0, mo0 = mlp_fwd(h1, 0)
    x2, h2 = _resnorm_fwd(x1, mo0, row(g_ffn_post, 0), row(g_mix_pre, 1), "norm_mlp0_out")

    (proj,) = _matmul(h2, w_in, "nn", (F32,), _epi_plain, "attn_in", tn=640)
    fl = proj[:, 3 * D:]
    f_all, f1, f2, f3 = _gate_fwd(fl, b_f, "gate_fwd")
    q_aug, k_aug, v_aug, qn2, kn2 = _attn_pack(proj, f1, f2, f3, "attn_pack")
    tb = min(ATT_TB, T)
    qk = 2.02 * jnp.sqrt(qn2[:, 0, 0] * kn2[:, 0, 0]) + 1.0
    fs, fe = f_all[0::tb, :N_HEADS].T, f_all[tb - 1::tb, :N_HEADS].T
    o_aug, q2_aug = _flash_fwd(qk, fs, fe, jnp.sqrt(kn2[:, 0, 0]), q_aug, k_aug, v_aug, "flash_fwd")
    (o_bf,) = _merge_heads(o_aug, 1.0, BF16, "attn_merge_o")
    (m1,) = _matmul(o_bf, w_o, "nn", (F32,), _epi_plain, "attn_out")
    x3, h3 = _resnorm_fwd(x2, m1, row(g_mix_post, 1), row(g_ffn_pre, 1), "norm_attn_out")
    act1, mo1 = mlp_fwd(h3, 1)
    dy, loss_part = _resnorm_loss(x3, mo1, row(g_ffn_post, 1), target, "loss")
    loss = lax.psum(loss_part[0, 0], ("x", "y", "c"))

    G = {}
    chip = 2 * lax.axis_index("x") + lax.axis_index("y")
    place = jnp.stack([chip] + [k + (k >= chip).astype(jnp.int32) for k in range(N_CHIPS - 1)]
                      + [lax.axis_index("c")]).astype(jnp.int32)

    def mlp_bwd(d_out, mo, act, h, x_in, l):
        d_mo, dg_post, _ = _norm_bwd(mo, row(g_ffn_post, l), d_out, None, BF16, f"mlp{l}_post_bwd")
        (d_up,) = _matmul(d_mo, w_down[l], "nt", (BF16,), _epi_dsqrelu, f"mlp{l}_dact", extras=((act, "tile"),))
        (dw_down,) = _matmul(act, d_mo, "tn", (F32,), _epi_plain, tk=2048, name=f"mlp{l}_dwdown")
        (dw_up,) = _matmul(h, d_up, "tn", (F32,), _epi_plain, tk=2048, name=f"mlp{l}_dwup", shard_out=True)
        (d_h,) = _matmul(d_up, w_up[l], "nt", (F32,), _epi_plain, f"mlp{l}_dh", tk=DFF)
        d_in, dg_pre, _ = _norm_bwd(x_in, row(g_ffn_pre, l), d_h, d_out, F32, f"mlp{l}_pre_bwd")
        G[f"up{l}"], G[f"down{l}"] = dw_up, dw_down.reshape(N_CHIPS, D, D)
        return d_in, dg_post, dg_pre

    d_x3, dg_ffn_post1, dg_ffn_pre1 = mlp_bwd(dy, mo1, act1, h3, x3, 1)

    d_m1, dg_mix_post1, _ = _norm_bwd(m1, row(g_mix_post, 1), d_x3, None, BF16, "attn_post_bwd")
    (dw_o,) = _matmul(o_bf, d_m1, "tn", (F32,), _epi_plain, tk=2048, name="attn_dwo")
    G["attn_w_o"] = dw_o.reshape(N_CHIPS, D // N_CHIPS, D)
    (d_o,) = _matmul(d_m1, w_o, "nt", (F32,), _epi_plain, "attn_do")
    do_aug = _attn_prep(d_o, o_aug, "attn_prep")
    dq_aug = _flash_dq(qk, fs, fe, q2_aug, k_aug, do_aug, v_aug, "flash_dq")
    dk_aug, dv_aug = _flash_dkv(qk, fs, fe, q2_aug, k_aug, do_aug, v_aug, "flash_dkv")
    d_q, dF_q = _merge_heads(dq_aug, 0.125, BF16, "attn_merge_dq", column=AUG_F)
    d_k, dF_k = _merge_heads(dk_aug, 1.0, BF16, "attn_merge_dk", column=AUG_ONE)
    (d_v,) = _merge_heads(dv_aug, 1.0, BF16, "attn_merge_dv")
    d_fl, db_f = _gate_bwd(dF_q, dF_k, fl, b_f, "gate_bwd")
    d_proj = jnp.concatenate([d_q, d_k, d_v, d_fl.astype(BF16)], axis=1)
    (dw_in,) = _matmul(h2, d_proj, "tn", (F32,), _epi_plain, tk=2048, name="attn_dwin", tn=640)
    G["attn_w_in"] = dw_in[:, :W_IN_COLS].reshape(D, N_CHIPS, W_IN_COLS // N_CHIPS).transpose(1, 0, 2)
    (d_h2,) = _matmul(d_proj, w_in, "nt", (F32,), _epi_plain, "attn_dh", tk=W_IN_PAD)
    d_x2, dg_mix_pre1, _ = _norm_bwd(x2, row(g_mix_pre, 1), d_h2, d_x3, F32, "attn_pre_bwd")

    d_x1, dg_ffn_post0, dg_ffn_pre0 = mlp_bwd(d_x2, mo0, act0, h1, x1, 0)

    d_m0, dg_mix_post0, db_pw2 = _norm_bwd(m0, row(g_mix_post, 0), d_x1, None, BF16, "conv_post_bwd")
    (dw_pw2,) = _matmul(z0, d_m0, "tn", (F32,), _epi_plain, tk=2048, name="conv_dwpw2")
    G["conv_pw2_w"] = dw_pw2.reshape(N_CHIPS, D // N_CHIPS, D)
    (d_z0,) = _matmul(d_m0, w_pw2, "nt", (F32,), _epi_plain, "conv_dz")
    d_y0, dln_g, dln_b, ddw_b = _conv_bwd_ln(d_z0, y0, conv_ln_g, conv_ln_b, "conv_bwd_ln")
    attn_units = [u for u in GRAD_UNITS if u[2][0].startswith("attn") or u[0].endswith("1")]
    conv_units = [u for u in GRAD_UNITS if u not in attn_units]
    attn_slabs = [G[n] for n, _, _ in attn_units]
    attn_got = _sibling_exchange(attn_units, attn_slabs, "grad_sibling_exchange_attn")
    pairs = {n: _add_pair(place, s, g, f"grad_pair_{n}") for s, g, (n, _, _) in zip(attn_slabs, attn_got, attn_units)}
    d_a0, db_pw1, ddw_w, attn_by_chip = _conv_bwd_dw(d_y0, a0, w_dw, "conv_bwd_dw",
                                                    [pairs[n] for n, _, _ in attn_units])
    by_chip = {n: b for b, (n, _, _) in zip(attn_by_chip, attn_units)}
    (G["conv_pw1_w"],) = _matmul(h0, d_a0, "tn", (F32,), _epi_plain, tk=2048, name="conv_dwpw1", tn=512, shard_out=True)
    (d_h0,) = _matmul(d_a0, w_pw1, "nt", (F32,), _epi_plain, "conv_dh", tk=2 * D)
    d_x0, dg_mix_pre0, _ = _norm_bwd(x0, row(g_mix_pre, 0), d_h0, d_x1, F32, "conv_pre_bwd")

    small_local = {
        "g_mix_pre": jnp.concatenate([dg_mix_pre0, dg_mix_pre1]), "g_mix_post": jnp.concatenate([dg_mix_post0, dg_mix_post1]),
        "g_ffn_pre": jnp.concatenate([dg_ffn_pre0, dg_ffn_pre1]), "g_ffn_post": jnp.concatenate([dg_ffn_post0, dg_ffn_post1]),
        "conv_pw1_b": db_pw1, "conv_dw_b": ddw_b, "conv_ln_g": dln_g, "conv_ln_b": dln_b, "conv_pw2_b": db_pw2,
        "attn_b_f": db_f[:, :N_HEADS], "conv_dw_w_full": ddw_w,
    }

    conv_slabs = [G[n] for n, _, _ in conv_units]
    conv_got = _sibling_exchange(conv_units, conv_slabs, "grad_sibling_exchange_conv")
    pairs.update({n: _add_pair(place, s, g, f"grad_pair_{n}")
                  for s, g, (n, _, _) in zip(conv_slabs, conv_got, conv_units)})
    conv_by_chip, small_all = _chip_exchange([pairs[n] for n, _, _ in conv_units],
                                             _pack([small_local[n] for n, _ in SMALL_ALL], SMALL_ALL_ROWS))
    by_chip.update({n: b for b, (n, _, _) in zip(conv_by_chip, conv_units)})
    unit_grads = _sibling_share([_add_chips(place, pairs[n], by_chip[n], f"grad_chips_{n}") for n, _, _ in GRAD_UNITS])
    grads = _unpack(_add_slots(small_all, "grad_add_small", tb=SMALL_ALL_ROWS), SMALL_ALL)
    grads["conv_dw_w"] = lax.dynamic_slice(grads.pop("conv_dw_w_full"), (0, chip * 256), (CONV_W, 256))[None]

    delta, new_m, new_v = {}, {}, {}
    per_layer = {}
    for g, (unit, shape, (name, layer)) in zip(unit_grads, GRAD_UNITS):
        view = lambda d: (d[name] if layer is None else d[name][layer]).reshape(shape)
        per_layer.setdefault(name, []).append((g,) + tuple(_adamw(view(weights), g, view(mom_m), view(mom_v),
                                                                  f"adamw_{unit}")))
    for name, parts in per_layer.items():
        full = weights[name].shape
        join = lambda n: (parts[0][n] if len(parts) == 1 else jnp.stack([p[n] for p in parts])).reshape(full)
        grads[name], delta[name], new_m[name], new_v[name] = join(0), join(1), join(2), join(3)
    dw2 = lambda d: d["conv_dw_w"].reshape(CONV_W, 256)
    d, nm, nv = _adamw(dw2(weights), dw2(grads), dw2(mom_m), dw2(mom_v), "adamw_conv_dw_w")
    full = weights["conv_dw_w"].shape
    delta["conv_dw_w"], new_m["conv_dw_w"], new_v["conv_dw_w"] = d.reshape(full), nm.reshape(full), nv.reshape(full)
    packs = [_pack([src[n] for n, _ in SMALL], SMALL_ROWS) for src in (weights, grads, mom_m, mom_v)]
    d, nm, nv = _adamw(*packs, "adamw_small")
    for dst, packed in ((delta, d), (new_m, nm), (new_v, nv)):
        dst.update(_unpack(packed, SMALL))

    grad_x = d_x0.reshape(x.shape)
    return (loss, grad_x, *[grads[n] for n in order], *[delta[n] for n in order], *[new_m[n] for n in order],
            *[new_v[n] for n in order])
```

```python
import functools

import jax
import jax.numpy as jnp
from jax import lax
from jax.experimental import pallas as pl
from jax.experimental.pallas import tpu as pltpu

F32 = jnp.float32
BF16 = jnp.bfloat16
MESH = pl.DeviceIdType.MESH

D = 1024
DFF = 4096
N_HEADS = 16
HEAD_DIM = 64
CONV_W = 31
HALO = 32
RMS_EPS = 1e-6
LN_EPS = 1e-5
MASK_VALUE = -1e30
W_IN_COLS = 3 * D + N_HEADS
W_IN_PAD = 3200
LANES = 128
SUBLANES = 8
N_CHIPS = 4
VMEM_LIMIT = 56 * 1024 * 1024
MATMUL_CHUNK = 256

ADAM_LR = 0.001
ADAM_B1 = 0.9
ADAM_B2 = 0.999
ADAM_EPS = 1e-08
ADAM_WD = 0.01
ADAM_STEP = 10

AUG_F = 64
AUG_ONE = 67
AUG_L = 70
SKIP_BELOW = -104.0
FIXED_MAX_BELOW = 40.0

GATHER_AT_START = (("conv_pw1_w", (1024, 512)), ("conv_dw_w", (HALO, 256)), ("conv_pw2_w", (256, 1024)))
GATHER_BEHIND_CONV = (("up0", (1024, 1024)), ("down0", (1024, 1024)), ("attn_w_in", (1024, 772)),
                      ("attn_w_o", (256, 1024)))
GATHER_BEHIND_PACK = (("up1", (1024, 1024)), ("down1", (1024, 1024)))
GRAD_UNITS = (
    ("conv_pw1_w", (1024, 512), ("conv_pw1_w", None)),
    ("conv_pw2_w", (256, 1024), ("conv_pw2_w", None)),
    ("attn_w_in", (1024, 772), ("attn_w_in", None)),
    ("attn_w_o", (256, 1024), ("attn_w_o", None)),
    ("up0", (1024, 1024), ("mlp_w_up", 0)),
    ("up1", (1024, 1024), ("mlp_w_up", 1)),
    ("down0", (1024, 1024), ("mlp_w_down", 0)),
    ("down1", (1024, 1024), ("mlp_w_down", 1)),
)
SMALL = (
    ("g_mix_pre", (2, 1024)), ("g_mix_post", (2, 1024)), ("g_ffn_pre", (2, 1024)), ("g_ffn_post", (2, 1024)),
    ("conv_pw1_b", (1, 2048)), ("conv_dw_b", (1, 1024)), ("conv_ln_g", (1, 1024)), ("conv_ln_b", (1, 1024)),
    ("conv_pw2_b", (1, 1024)), ("attn_b_f", (1, 16)),
)
SMALL_ROWS = 120
SMALL_ALL = SMALL + (("conv_dw_w_full", (HALO, 1024)),)
SMALL_ALL_ROWS = 376


def _size(shape):
    n = 1
    for s in shape:
        n *= s
    return n


def _pack(arrays, rows):
    flat = jnp.concatenate([a.reshape(-1) for a in arrays])
    return jnp.pad(flat, (0, rows * LANES - flat.shape[0])).reshape(rows, LANES)


def _unpack(packed, table):
    flat = packed.reshape(-1)
    out, off = {}, 0
    for name, shape in table:
        n = _size(shape)
        out[name] = flat[off:off + n].reshape(shape)
        off += n
    return out


def _params(sem):
    return pltpu.CompilerParams(dimension_semantics=sem, vmem_limit_bytes=VMEM_LIMIT)


def _rms(x, g):
    return x * lax.rsqrt(jnp.mean(x * x, axis=-1, keepdims=True) + RMS_EPS) * g


def _split3(v):
    p1 = v.astype(BF16).astype(F32)
    r = v - p1
    p2 = r.astype(BF16).astype(F32)
    p3 = (r - p2).astype(BF16).astype(F32)
    return p1, p2, p3


def _matmul(a, b, mode, out_dtypes, epi, name, extras=(), tm=1024, tn=1024, tk=1024, shard_out=False):
    (K, M) = a.shape if mode == "tn" else a.shape[::-1]
    N = b.shape[0] if mode == "nt" else b.shape[1]
    tm, tn, tk = min(tm, M), min(tn, N), min(tk, K)
    nk = K // tk
    if mode == "tn":
        a_spec = pl.BlockSpec((tk, tm), lambda i, j, k: (k, i))
    else:
        a_spec = pl.BlockSpec((tm, tk), lambda i, j, k: (i, k))
    if mode == "nt":
        b_spec = pl.BlockSpec((tn, tk), lambda i, j, k: (j, k))
    else:
        b_spec = pl.BlockSpec((tk, tn), lambda i, j, k: (k, j))
    if shard_out:
        per = N // N_CHIPS // tn
        o_spec = pl.BlockSpec((None, tm, tn), lambda i, j, k: (j // per, i, j % per))
        o_shape = (N_CHIPS, M, N // N_CHIPS)
    else:
        o_spec = pl.BlockSpec((tm, tn), lambda i, j, k: (i, j))
        o_shape = (M, N)
    dims = {"nn": (((1,), (0,)), ((), ())), "nt": (((1,), (1,)), ((), ())), "tn": (((0,), (0,)), ((), ()))}[mode]
    ex_specs = []
    for _, kind in extras:
        if kind == "row":
            ex_specs.append(pl.BlockSpec((1, tn), lambda i, j, k: (0, j)))
        else:
            ex_specs.append(pl.BlockSpec((tm, tn), lambda i, j, k: (i, j)))
    n_ex, n_out = len(extras), len(out_dtypes)

    cw = MATMUL_CHUNK if tn % MATMUL_CHUNK == 0 else tn

    def body(*refs):
        a_ref, b_ref = refs[0], refs[1]
        ex = refs[2:2 + n_ex]
        outs = refs[2 + n_ex:2 + n_ex + n_out]

        def for_chunks(use):
            for c in range(tn // cw):
                cols = slice(c * cw, (c + 1) * cw)
                b_chunk = b_ref[cols, :] if mode == "nt" else b_ref[:, cols]
                use(cols, lax.dot_general(a_ref[...], b_chunk, dims, preferred_element_type=F32))

        def finish(cols, acc):
            res = epi(acc, *[e[:, cols] for e in ex])
            for o, r in zip(outs, res):
                o[:, cols] = r.astype(o.dtype)

        if nk == 1:
            for_chunks(finish)
        else:
            acc_ref = refs[-1]
            k = pl.program_id(2)

            def first(cols, prod):
                acc_ref[:, cols] = prod

            def middle(cols, prod):
                acc_ref[:, cols] += prod

            pl.when(k == 0)(lambda: for_chunks(first))
            pl.when(jnp.logical_and(k > 0, k < nk - 1))(lambda: for_chunks(middle))
            pl.when(k == nk - 1)(lambda: for_chunks(lambda cols, prod: finish(cols, acc_ref[:, cols] + prod)))

    return pl.pallas_call(
        body,
        name=name,
        grid=(M // tm, N // tn, nk),
        in_specs=[a_spec, b_spec] + ex_specs,
        out_specs=[o_spec for _ in out_dtypes],
        out_shape=[jax.ShapeDtypeStruct(o_shape, dt) for dt in out_dtypes],
        scratch_shapes=[pltpu.VMEM((tm, tn), F32)] if nk > 1 else [],
        compiler_params=_params(("parallel", "parallel", "arbitrary")),
    )(a, b, *[e for e, _ in extras])


def _epi_plain(acc):
    return (acc,)


def _epi_bias(acc, bias):
    return (acc + bias,)


def _epi_sqrelu(acc):
    r = jnp.maximum(acc, 0.0)
    return (r * r,)


def _epi_dsqrelu(acc, act):
    return (acc * (2.0 * jnp.sqrt(act.astype(F32))),)


def _resnorm_fwd(x, m, g_post, g_next, name, tr=512):
    T = x.shape[0]
    has_m = m is not None
    row = pl.BlockSpec((tr, D), lambda i: (i, 0))
    vec = pl.BlockSpec((1, D), lambda i: (0, 0))

    def body(*refs):
        if has_m:
            x_ref, m_ref, gp_ref, gn_ref, x1_ref, h_ref = refs
            x1 = x_ref[...] + _rms(m_ref[...], gp_ref[...])
            x1_ref[...] = x1
        else:
            x_ref, gn_ref, h_ref = refs
            x1 = x_ref[...]
        h_ref[...] = _rms(x1, gn_ref[...]).astype(BF16)

    if has_m:
        args, in_specs = (x, m, g_post, g_next), [row, row, vec, vec]
        out_specs = [row, row]
        out_shape = [jax.ShapeDtypeStruct((T, D), F32), jax.ShapeDtypeStruct((T, D), BF16)]
    else:
        args, in_specs = (x, g_next), [row, vec]
        out_specs = [row]
        out_shape = [jax.ShapeDtypeStruct((T, D), BF16)]
    return pl.pallas_call(
        body, name=name, grid=(T // tr,), in_specs=in_specs, out_specs=out_specs, out_shape=out_shape,
        compiler_params=_params(("parallel",)),
    )(*args)


def _resnorm_loss(x, m, g_post, target, name, tr=512):
    T = x.shape[0]
    row = pl.BlockSpec((tr, D), lambda i: (i, 0))
    vec = pl.BlockSpec((1, D), lambda i: (0, 0))

    def body(x_ref, m_ref, gp_ref, t_ref, dy_ref, loss_ref):
        diff = x_ref[...] + _rms(m_ref[...], gp_ref[...]) - t_ref[...]
        dy_ref[...] = diff * (1.0 / D)

        @pl.when(pl.program_id(0) == 0)
        def _():
            loss_ref[...] = jnp.zeros_like(loss_ref)

        per_token = jnp.mean(diff * diff, axis=-1, keepdims=True)
        loss_ref[...] += 0.5 * jnp.sum(per_token)

    return pl.pallas_call(
        body, name=name, grid=(T // tr,), in_specs=[row, row, vec, row],
        out_specs=[row, pl.BlockSpec((8, LANES), lambda i: (0, 0))],
        out_shape=[jax.ShapeDtypeStruct((T, D), F32), jax.ShapeDtypeStruct((8, LANES), F32)],
        compiler_params=_params(("arbitrary",)),
    )(x, m, g_post, target)


def _norm_bwd(x, g, dy, resid, out_dtype, name, tr=512):
    T = x.shape[0]
    has_r = resid is not None
    row = pl.BlockSpec((tr, D), lambda i: (i, 0))
    vec = pl.BlockSpec((1, D), lambda i: (0, 0))

    def body(*refs):
        if has_r:
            x_ref, g_ref, dy_ref, r_ref, dx_ref, dg_ref, cs_ref = refs
        else:
            x_ref, g_ref, dy_ref, dx_ref, dg_ref, cs_ref = refs
        xv, dyv = x_ref[...], dy_ref[...]
        r = lax.rsqrt(jnp.mean(xv * xv, axis=-1, keepdims=True) + RMS_EPS)
        gy = dyv * g_ref[...]
        c = jnp.sum(gy * xv, axis=-1, keepdims=True) * (1.0 / D)
        dx = r * gy - xv * (r * r * r * c)

        @pl.when(pl.program_id(0) == 0)
        def _():
            dg_ref[...] = jnp.zeros_like(dg_ref)
            cs_ref[...] = jnp.zeros_like(cs_ref)

        dg_ref[...] += jnp.sum(dyv * (xv * r), axis=0, keepdims=True)
        cs_ref[...] += jnp.sum(dx, axis=0, keepdims=True)
        if has_r:
            dx = dx + r_ref[...]
        dx_ref[...] = dx.astype(dx_ref.dtype)

    args = (x, g, dy) + ((resid,) if has_r else ())
    return pl.pallas_call(
        body, name=name, grid=(T // tr,), in_specs=[row, vec, row] + ([row] if has_r else []),
        out_specs=[row, vec, vec],
        out_shape=[jax.ShapeDtypeStruct((T, D), out_dtype), jax.ShapeDtypeStruct((1, D), F32),
                   jax.ShapeDtypeStruct((1, D), F32)],
        compiler_params=_params(("arbitrary",)),
    )(*args)


CONV_TB = 256
CONV_RC = 32


def _glu(a):
    return a[:, :D] * jax.nn.sigmoid(a[:, D:])


def _shifted_copies(win):
    rows = win.shape[1]
    for p in range(1, SUBLANES):
        win[p, 0:rows - SUBLANES, :] = win[0, p:p + rows - SUBLANES, :]


def _window(win, start):
    p = start % SUBLANES
    return win[p, start - p:start - p + CONV_RC, :]


def _conv_fwd(a, dw_w, dw_b, ln_g, ln_b, name, gather_units, gather_shards):
    T = a.shape[0]
    tb = min(CONV_TB, T)
    per = tb // HALO
    n_steps = T // tb
    n_g = len(gather_units)
    vec = pl.BlockSpec((1, D), lambda i: (0, 0))

    def body(*refs):
        cur_ref, prev_ref, w_ref, b_ref, lg_ref, lb_ref = refs[:6]
        g_ins = refs[6:6 + n_g]
        y_ref, z_ref = refs[6 + n_g:8 + n_g]
        g_outs = refs[8 + n_g:8 + 2 * n_g]
        uwin = refs[8 + 2 * n_g]
        sems = refs[9 + 2 * n_g:]
        i = pl.program_id(0)
        pl.when(i == 0)(lambda: _gather_start(gather_units, g_ins, g_outs, sems))
        uwin[0, 0:HALO, :] = jnp.where(i > 0, _glu(prev_ref[...]), 0.0)
        uwin[0, HALO:, :] = _glu(cur_ref[...])
        _shifted_copies(uwin)
        for r in range(tb // CONV_RC):
            acc = jnp.broadcast_to(b_ref[...], (CONV_RC, D))
            for k in range(CONV_W):
                start = r * CONV_RC + HALO - (CONV_W - 1) + k
                acc = acc + _window(uwin, start) * w_ref[k:k + 1, :]
            rows = slice(r * CONV_RC, (r + 1) * CONV_RC)
            y_ref[rows, :] = acc
            mu = jnp.mean(acc, axis=-1, keepdims=True)
            xc = acc - mu
            var = jnp.mean(xc * xc, axis=-1, keepdims=True)
            n = xc * lax.rsqrt(var + LN_EPS) * lg_ref[...] + lb_ref[...]
            z_ref[rows, :] = (n * jax.nn.sigmoid(n)).astype(BF16)
        pl.when(i == n_steps - 1)(lambda: _gather_finish(gather_units, g_ins, g_outs, sems))

    outs = pl.pallas_call(
        body, name=name, grid=(n_steps,),
        in_specs=[pl.BlockSpec((tb, 2 * D), lambda i: (i, 0)),
                  pl.BlockSpec((HALO, 2 * D), lambda i: (jnp.maximum(i * per - 1, 0), 0)),
                  pl.BlockSpec((HALO, D), lambda i: (0, 0)), vec, vec, vec] + [ANY] * n_g,
        out_specs=[pl.BlockSpec((tb, D), lambda i: (i, 0)), pl.BlockSpec((tb, D), lambda i: (i, 0))] + [ANY] * n_g,
        out_shape=[jax.ShapeDtypeStruct((T, D), F32), jax.ShapeDtypeStruct((T, D), BF16)]
        + [jax.ShapeDtypeStruct((N_CHIPS,) + shape, BF16) for _, shape in gather_units],
        scratch_shapes=[pltpu.VMEM((SUBLANES, tb + HALO, D), F32)] + _gather_sems(n_g),
        compiler_params=_params(("arbitrary",)),
    )(a, a, dw_w, dw_b, ln_g, ln_b, *gather_shards)
    return outs[0], outs[1], outs[2:]


def _conv_bwd_ln(dz, y, ln_g, ln_b, name, tr=256):
    T = y.shape[0]
    tr = min(tr, T)
    row = pl.BlockSpec((tr, D), lambda i: (i, 0))
    vec = pl.BlockSpec((1, D), lambda i: (0, 0))

    def body(dz_ref, y_ref, lg_ref, lb_ref, dy_ref, dlg_ref, dlb_ref, db_ref):
        yv = y_ref[...]
        mu = jnp.mean(yv, axis=-1, keepdims=True)
        xc = yv - mu
        rstd = lax.rsqrt(jnp.mean(xc * xc, axis=-1, keepdims=True) + LN_EPS)
        yh = xc * rstd
        n = yh * lg_ref[...] + lb_ref[...]
        sg = jax.nn.sigmoid(n)
        dn = dz_ref[...] * (sg * (1.0 + n * (1.0 - sg)))
        dyh = dn * lg_ref[...]
        dyv = rstd * (dyh - jnp.mean(dyh, axis=-1, keepdims=True) - yh * jnp.mean(dyh * yh, axis=-1, keepdims=True))
        dy_ref[...] = dyv

        @pl.when(pl.program_id(0) == 0)
        def _():
            dlg_ref[...] = jnp.zeros_like(dlg_ref)
            dlb_ref[...] = jnp.zeros_like(dlb_ref)
            db_ref[...] = jnp.zeros_like(db_ref)

        dlg_ref[...] += jnp.sum(dn * yh, axis=0, keepdims=True)
        dlb_ref[...] += jnp.sum(dn, axis=0, keepdims=True)
        db_ref[...] += jnp.sum(dyv, axis=0, keepdims=True)

    return pl.pallas_call(
        body, name=name, grid=(T // tr,), in_specs=[row, row, vec, vec], out_specs=[row, vec, vec, vec],
        out_shape=[jax.ShapeDtypeStruct((T, D), F32)] + [jax.ShapeDtypeStruct((1, D), F32)] * 3,
        compiler_params=_params(("arbitrary",)),
    )(dz, y, ln_g, ln_b)


def _conv_bwd_dw(dy, a, dw_w, name, exchange):
    T = a.shape[0]
    tb = min(CONV_TB, T)
    per = tb // HALO
    last_halo = T // HALO - 1
    n_steps = T // tb
    n_x = len(exchange)

    def body(*refs):
        dyc_ref, dyn_ref, cur_ref, prev_ref, w_ref = refs[:5]
        x_ins = refs[5:5 + n_x]
        da_ref, dbias_ref, dw_ref = refs[5 + n_x:8 + n_x]
        x_outs = refs[8 + n_x:8 + 2 * n_x]
        uwin, dywin, dwacc = refs[8 + 2 * n_x:11 + 2 * n_x]
        sems = refs[11 + 2 * n_x:]
        i = pl.program_id(0)

        @pl.when(i == 0)
        def _():
            _exchange_start(x_ins, x_outs, sems)
            dbias_ref[...] = jnp.zeros_like(dbias_ref)
            dwacc[...] = jnp.zeros_like(dwacc)

        uwin[0, 0:HALO, :] = jnp.where(i > 0, _glu(prev_ref[...]), 0.0)
        uwin[0, HALO:, :] = _glu(cur_ref[...])
        dywin[0, 0:tb, :] = dyc_ref[...]
        dywin[0, tb:, :] = jnp.where(i < n_steps - 1, dyn_ref[...], 0.0)
        _shifted_copies(uwin)
        _shifted_copies(dywin)
        for r in range(tb // CONV_RC):
            rows = slice(r * CONV_RC, (r + 1) * CONV_RC)
            dy_c = dywin[0, rows, :]
            du = jnp.zeros((CONV_RC, D), F32)
            for k in range(CONV_W):
                du = du + _window(dywin, r * CONV_RC + (CONV_W - 1) - k) * w_ref[k:k + 1, :]
                prod = dy_c * _window(uwin, r * CONV_RC + HALO - (CONV_W - 1) + k)
                part = prod[0:8, :]
                for q in range(1, CONV_RC // 8):
                    part = part + prod[8 * q:8 * q + 8, :]
                dwacc[8 * k:8 * k + 8, :] += part
            av = cur_ref[rows, :]
            a1, sg = av[:, :D], jax.nn.sigmoid(av[:, D:])
            da1 = du * sg
            da2 = du * a1 * (sg * (1.0 - sg))
            da_ref[rows, 0:D] = da1.astype(BF16)
            da_ref[rows, D:] = da2.astype(BF16)
            dbias_ref[:, 0:D] += jnp.sum(da1, axis=0, keepdims=True)
            dbias_ref[:, D:] += jnp.sum(da2, axis=0, keepdims=True)

        @pl.when(i == n_steps - 1)
        def _():
            for k in range(CONV_W):
                dw_ref[k:k + 1, :] = jnp.sum(dwacc[8 * k:8 * k + 8, :], axis=0, keepdims=True)
            dw_ref[CONV_W:, :] = jnp.zeros((HALO - CONV_W, D), F32)
            _exchange_finish(x_ins, x_outs, sems)

    outs = pl.pallas_call(
        body, name=name, grid=(n_steps,),
        in_specs=[pl.BlockSpec((tb, D), lambda i: (i, 0)),
                  pl.BlockSpec((HALO, D), lambda i: (jnp.minimum((i + 1) * per, last_halo), 0)),
                  pl.BlockSpec((tb, 2 * D), lambda i: (i, 0)),
                  pl.BlockSpec((HALO, 2 * D), lambda i: (jnp.maximum(i * per - 1, 0), 0)),
                  pl.BlockSpec((HALO, D), lambda i: (0, 0))] + [ANY] * n_x,
        out_specs=[pl.BlockSpec((tb, 2 * D), lambda i: (i, 0)), pl.BlockSpec((1, 2 * D), lambda i: (0, 0)),
                   pl.BlockSpec((HALO, D), lambda i: (0, 0))] + [ANY] * n_x,
        out_shape=[jax.ShapeDtypeStruct((T, 2 * D), BF16), jax.ShapeDtypeStruct((1, 2 * D), F32),
                   jax.ShapeDtypeStruct((HALO, D), F32)] + [jax.ShapeDtypeStruct(p.shape, p.dtype) for p in exchange],
        scratch_shapes=[pltpu.VMEM((SUBLANES, tb + HALO, D), F32), pltpu.VMEM((SUBLANES, tb + HALO, D), F32),
                        pltpu.VMEM((8 * HALO, D), F32)] + _exchange_sems(n_x),
        compiler_params=_params(("arbitrary",)),
    )(dy, dy, a, a, dw_w, *exchange)
    return outs[0], outs[1], outs[2], outs[3:]


GATE_TB = 512


def _gate_fwd(fl, b_f, name):
    T = fl.shape[0]
    tb = min(GATE_TB, T)
    row = pl.BlockSpec((tb, LANES), lambda i: (i, 0))

    def body(fl_ref, b_ref, f_ref, f1_ref, f2_ref, f3_ref, carry):
        @pl.when(pl.program_id(0) == 0)
        def _():
            carry[...] = jnp.zeros_like(carry)

        z = fl_ref[...] + b_ref[...]
        lf = jnp.minimum(z, 0.0) - jnp.log(1.0 + jnp.exp(-jnp.abs(z)))
        tri = (lax.broadcasted_iota(jnp.int32, (tb, tb), 0) >= lax.broadcasted_iota(jnp.int32, (tb, tb), 1)).astype(F32)
        f = jnp.dot(tri, lf, precision=lax.Precision.HIGHEST, preferred_element_type=F32) + carry[...]
        carry[...] = f[tb - 1:tb, :]
        f_ref[...] = f
        f1_ref[...], f2_ref[...], f3_ref[...] = _split3(f)

    return pl.pallas_call(
        body, name=name, grid=(T // tb,), in_specs=[row, pl.BlockSpec((1, LANES), lambda i: (0, 0))],
        out_specs=[row] * 4, out_shape=[jax.ShapeDtypeStruct((T, LANES), F32)] * 4,
        scratch_shapes=[pltpu.VMEM((1, LANES), F32)],
        compiler_params=_params(("arbitrary",)),
    )(fl, b_f)


def _gate_bwd(dF_q, dF_k, fl, b_f, name):
    T = fl.shape[0]
    tb = min(GATE_TB, T)
    nb = T // tb
    row = pl.BlockSpec((tb, LANES), lambda i: (nb - 1 - i, 0))
    vec = pl.BlockSpec((1, LANES), lambda i: (0, 0))

    def body(dfq_ref, dfk_ref, fl_ref, b_ref, dfl_ref, db_ref, carry):
        @pl.when(pl.program_id(0) == 0)
        def _():
            carry[...] = jnp.zeros_like(carry)
            db_ref[...] = jnp.zeros_like(db_ref)

        tri = (lax.broadcasted_iota(jnp.int32, (tb, tb), 0) <= lax.broadcasted_iota(jnp.int32, (tb, tb), 1)).astype(F32)
        rc = jnp.dot(tri, dfq_ref[...] - dfk_ref[...], precision=lax.Precision.HIGHEST,
                     preferred_element_type=F32) + carry[...]
        carry[...] = rc[0:1, :]
        dfl = rc * jax.nn.sigmoid(-(fl_ref[...] + b_ref[...]))
        dfl_ref[...] = dfl
        db_ref[...] += jnp.sum(dfl, axis=0, keepdims=True)

    return pl.pallas_call(
        body, name=name, grid=(nb,), in_specs=[row, row, row, vec], out_specs=[row, vec],
        out_shape=[jax.ShapeDtypeStruct((T, LANES), F32), jax.ShapeDtypeStruct((1, LANES), F32)],
        scratch_shapes=[pltpu.VMEM((1, LANES), F32)],
        compiler_params=_params(("arbitrary",)),
    )(dF_q, dF_k, fl, b_f)


ATT_TB = 512
LAYOUT_TB = 2048
NT = (((1,), (1,)), ((), ()))
NN = (((1,), (0,)), ((), ()))


def _lane_insert(base, parts, first_lane):
    lane = lax.broadcasted_iota(jnp.int32, base.shape, 1)
    out = base
    for n, p in enumerate(parts):
        out = jnp.where(lane == first_lane + n, p, out)
    return out


def _causal(tb):
    return lax.broadcasted_iota(jnp.int32, (tb, tb), 0) >= lax.broadcasted_iota(jnp.int32, (tb, tb), 1)


def _other_head(x):
    return pltpu.roll(x, HEAD_DIM, 1)


def _attn_pack(proj, f1, f2, f3, name, gather_units, gather_shards):
    T = proj.shape[0]
    tb = min(LAYOUT_TB, T)
    pairs = N_HEADS // 2
    n_steps = T // tb
    n_g = len(gather_units)
    cols = lambda first: pl.BlockSpec((tb, LANES), lambda p, i: (i, first + p))
    gate = pl.BlockSpec((tb, LANES), lambda p, i: (i, 0))
    heads = pl.BlockSpec((2, tb, LANES), lambda p, i: (p, i, 0))
    norm = pl.BlockSpec((2, 8, LANES), lambda p, i: (p, 0, 0))

    def body(*refs):
        q_ref, k_ref, v_ref, f1_ref, f2_ref, f3_ref = refs[:6]
        g_ins = refs[6:6 + n_g]
        qa_ref, ka_ref, va_ref, qn_ref, kn_ref = refs[6 + n_g:11 + n_g]
        g_outs = refs[11 + n_g:11 + 2 * n_g]
        sems = refs[11 + 2 * n_g:]
        p, i = pl.program_id(0), pl.program_id(1)
        pl.when(jnp.logical_and(p == 0, i == 0))(lambda: _gather_start(gather_units, g_ins, g_outs, sems))
        lane = lax.broadcasted_iota(jnp.int32, (tb, LANES), 1)
        data = lane < HEAD_DIM

        @pl.when(i == 0)
        def _():
            qn_ref[...] = jnp.zeros_like(qn_ref)
            kn_ref[...] = jnp.zeros_like(kn_ref)

        qv, kv, vv = q_ref[...] * 0.125, k_ref[...], v_ref[...]
        for e in range(2):
            fcol = [jnp.sum(jnp.where(lane == 2 * p + e, f[...], 0.0), axis=-1, keepdims=True)
                    for f in (f1_ref, f2_ref, f3_ref)]
            pick = (lambda t: t) if e == 0 else _other_head
            qd = jnp.where(data, pick(qv), 0.0).astype(BF16).astype(F32)
            kd = jnp.where(data, pick(kv), 0.0).astype(BF16).astype(F32)
            vd = jnp.where(data, pick(vv), 0.0)
            qa_ref[e] = _lane_insert(qd, fcol + [1.0, 1.0, 1.0], AUG_F).astype(BF16)
            ka_ref[e] = _lane_insert(kd, [1.0, 1.0, 1.0] + [-f for f in fcol] + [1.0, 1.0, 1.0], AUG_F).astype(BF16)
            va_ref[e] = _lane_insert(vd, [-1.0, -1.0, -1.0, 1.0], AUG_F).astype(BF16)
            qn_ref[e] = jnp.maximum(qn_ref[e], jnp.max(jnp.sum(qd * qd, axis=-1, keepdims=True)))
            kn_ref[e] = jnp.maximum(kn_ref[e], jnp.max(jnp.sum(kd * kd, axis=-1, keepdims=True)))
        pl.when(jnp.logical_and(p == pairs - 1, i == n_steps - 1))(
            lambda: _gather_finish(gather_units, g_ins, g_outs, sems))

    aug = jax.ShapeDtypeStruct((N_HEADS, T, LANES), BF16)
    nrm = jax.ShapeDtypeStruct((N_HEADS, 8, LANES), F32)
    outs = pl.pallas_call(
        body, name=name, grid=(pairs, n_steps),
        in_specs=[cols(0), cols(pairs), cols(2 * pairs), gate, gate, gate] + [ANY] * n_g,
        out_specs=[heads, heads, heads, norm, norm] + [ANY] * n_g,
        out_shape=[aug, aug, aug, nrm, nrm]
        + [jax.ShapeDtypeStruct((N_CHIPS,) + shape, BF16) for _, shape in gather_units],
        scratch_shapes=_gather_sems(n_g),
        compiler_params=_params(("arbitrary", "arbitrary")),
    )(proj, proj, proj, f1, f2, f3, *gather_shards)
    return outs[:5], outs[5:]


def _merge_heads(x_aug, scale, out_dtype, name, column=None):
    H, T, _ = x_aug.shape
    tb = min(LAYOUT_TB, T)

    def body(x_ref, o_ref, *col_ref):
        p = pl.program_id(1)
        lane = lax.broadcasted_iota(jnp.int32, (tb, LANES), 1)
        x0, x1 = x_ref[0], x_ref[1]
        o_ref[...] = (jnp.where(lane < HEAD_DIM, x0, _other_head(x1)) * scale).astype(out_dtype)
        if column is not None:
            @pl.when(p == 0)
            def _():
                col_ref[0][...] = jnp.zeros_like(col_ref[0])

            c0 = jnp.sum(jnp.where(lane == column, x0, 0.0), axis=-1, keepdims=True)
            c1 = jnp.sum(jnp.where(lane == column, x1, 0.0), axis=-1, keepdims=True)
            col_ref[0][...] += jnp.where(lane == 2 * p, c0, 0.0) + jnp.where(lane == 2 * p + 1, c1, 0.0)

    out_specs = [pl.BlockSpec((tb, LANES), lambda i, p: (i, p))]
    out_shape = [jax.ShapeDtypeStruct((T, D), out_dtype)]
    if column is not None:
        out_specs.append(pl.BlockSpec((tb, LANES), lambda i, p: (i, 0)))
        out_shape.append(jax.ShapeDtypeStruct((T, LANES), F32))
    return pl.pallas_call(
        body, name=name, grid=(T // tb, H // 2),
        in_specs=[pl.BlockSpec((2, tb, LANES), lambda i, p: (p, i, 0))],
        out_specs=out_specs, out_shape=out_shape,
        compiler_params=_params(("parallel", "arbitrary")),
    )(x_aug)


def _attn_prep(d_o, o_aug, name):
    H, T, _ = o_aug.shape
    tb = min(LAYOUT_TB, T)
    heads = pl.BlockSpec((2, tb, LANES), lambda p, i: (p, i, 0))

    def body(do_ref, o_ref, out_ref):
        lane = lax.broadcasted_iota(jnp.int32, (tb, LANES), 1)
        dov = do_ref[...]
        for e in range(2):
            d_e = jnp.where(lane < HEAD_DIM, dov if e == 0 else _other_head(dov), 0.0)
            delta = jnp.sum(d_e * o_ref[e], axis=-1, keepdims=True)
            out_ref[e] = _lane_insert(d_e, _split3(delta), AUG_F).astype(BF16)

    return pl.pallas_call(
        body, name=name, grid=(H // 2, T // tb),
        in_specs=[pl.BlockSpec((tb, LANES), lambda p, i: (i, p)), heads], out_specs=heads,
        out_shape=jax.ShapeDtypeStruct((H, T, LANES), BF16),
        compiler_params=_params(("parallel", "parallel")),
    )(d_o, o_aug)


def _flash_spec(n_in, n_out, H, nb, tb, T, resident, scratch, n_tables=3):
    blk = pl.BlockSpec((2, tb, LANES), lambda h, i, *_: (h, i, 0))
    whole = pl.BlockSpec((2, T, LANES), lambda h, i, *_: (h, 0, 0))
    return pltpu.PrefetchScalarGridSpec(
        num_scalar_prefetch=n_tables, grid=(H // 2, nb),
        in_specs=[whole if resident[n] else blk for n in range(n_in)],
        out_specs=[blk] * n_out, scratch_shapes=scratch)


def _first_live(qk_ref, fs_ref, fe_ref, h, i):
    top = qk_ref[h] + fs_ref[h, i]
    return lax.fori_loop(0, i, lambda j, n: n + jnp.where(top - fe_ref[h, j] < SKIP_BELOW, 1, 0), 0)


def _flash_fwd(qk, fs, fe, kmax, q_aug, k_aug, v_aug, name):
    H, T, _ = q_aug.shape
    tb = min(ATT_TB, T)

    def body(qk_ref, fs_ref, fe_ref, km_ref, q_ref, k_ref, v_ref, o_ref, q2_ref, m_s, acc_s, qm_s):
        h, i = 2 * pl.program_id(0), pl.program_id(1)
        acc_s[...] = jnp.zeros_like(acc_s)
        first = jnp.minimum(_first_live(qk_ref, fs_ref, fe_ref, h, i), _first_live(qk_ref, fs_ref, fe_ref, h + 1, i))
        bounded = jnp.logical_and(qk_ref[h] < FIXED_MAX_BELOW, qk_ref[h + 1] < FIXED_MAX_BELOW)

        def sweep(step):
            def loop_body(kb, carry):
                step(kb, False)
                return carry

            lax.fori_loop(first, i, loop_body, 0)
            step(i, True)

        @pl.when(bounded)
        def _():
            lane = lax.broadcasted_iota(jnp.int32, (tb, LANES), 1)
            for e in range(2):
                qf = q_ref[e].astype(F32)
                norm = jnp.sqrt(jnp.sum(jnp.where(lane < HEAD_DIM, qf * qf, 0.0), axis=-1, keepdims=True))
                bound = norm * (1.01 * km_ref[h + e]) + 1e-3
                m_s[e] = bound
                qm_s[e] = _lane_insert(qf, [-p for p in _split3(bound)], AUG_L).astype(BF16)

            def step(kb, masked):
                rows = pl.ds(pl.multiple_of(kb * tb, tb), tb)
                for e in range(2):
                    p = jnp.exp(lax.dot_general(qm_s[e], k_ref[e, rows, :], NT, preferred_element_type=F32))
                    if masked:
                        p = jnp.where(_causal(tb), p, 0.0)
                    acc_s[e] += lax.dot_general(p.astype(BF16), v_ref[e, rows, :], NN, preferred_element_type=F32)

            sweep(step)

        @pl.when(jnp.logical_not(bounded))
        def _():
            m_s[...] = jnp.full(m_s.shape, -jnp.inf, F32)

            def step(kb, masked):
                rows = pl.ds(pl.multiple_of(kb * tb, tb), tb)
                for e in range(2):
                    s = lax.dot_general(q_ref[e], k_ref[e, rows, :], NT, preferred_element_type=F32)
                    if masked:
                        s = jnp.where(_causal(tb), s, MASK_VALUE)
                    m_old = m_s[e]
                    m_new = jnp.maximum(m_old, jnp.max(s, axis=-1, keepdims=True))
                    p = jnp.exp(s - m_new)
                    acc_s[e] = jnp.exp(m_old - m_new) * acc_s[e] + lax.dot_general(
                        p.astype(BF16), v_ref[e, rows, :], NN, preferred_element_type=F32)
                    m_s[e] = m_new

            sweep(step)

        for e in range(2):
            acc = acc_s[e]
            lane = lax.broadcasted_iota(jnp.int32, acc.shape, 1)
            l = jnp.sum(jnp.where(lane == AUG_ONE, acc, 0.0), axis=-1, keepdims=True)
            o_ref[e] = acc / l
            lse = m_s[e] + jnp.log(l)
            q2_ref[e] = _lane_insert(q_ref[e].astype(F32), [-p for p in _split3(lse)], AUG_L).astype(BF16)

    return pl.pallas_call(
        body, name=name,
        grid_spec=_flash_spec(3, 2, H, T // tb, tb, T, (False, True, True),
                              [pltpu.VMEM((2, tb, 1), F32), pltpu.VMEM((2, tb, LANES), F32),
                               pltpu.VMEM((2, tb, LANES), BF16)], n_tables=4),
        out_shape=[jax.ShapeDtypeStruct((H, T, LANES), F32), jax.ShapeDtypeStruct((H, T, LANES), BF16)],
        compiler_params=_params(("parallel", "arbitrary")),
    )(qk, fs, fe, kmax, q_aug, k_aug, v_aug)


def _flash_dq(qk, fs, fe, q2, k_aug, do_aug, v_aug, name):
    H, T, _ = q2.shape
    tb = min(ATT_TB, T)

    def body(qk_ref, fs_ref, fe_ref, q_ref, k_ref, do_ref, v_ref, dq_ref, acc_s):
        h, i = 2 * pl.program_id(0), pl.program_id(1)
        acc_s[...] = jnp.zeros_like(acc_s)

        def step(kb, masked):
            rows = pl.ds(pl.multiple_of(kb * tb, tb), tb)
            for e in range(2):
                k = k_ref[e, rows, :]
                p = jnp.exp(lax.dot_general(q_ref[e], k, NT, preferred_element_type=F32))
                if masked:
                    p = jnp.where(_causal(tb), p, 0.0)
                ds = p * lax.dot_general(do_ref[e], v_ref[e, rows, :], NT, preferred_element_type=F32)
                acc_s[e] += lax.dot_general(ds.astype(BF16), k, NN, preferred_element_type=F32)

        def loop_body(kb, carry):
            step(kb, False)
            return carry

        first = jnp.minimum(_first_live(qk_ref, fs_ref, fe_ref, h, i), _first_live(qk_ref, fs_ref, fe_ref, h + 1, i))
        lax.fori_loop(first, i, loop_body, 0)
        step(i, True)
        dq_ref[...] = acc_s[...]

    return pl.pallas_call(
        body, name=name,
        grid_spec=_flash_spec(4, 1, H, T // tb, tb, T, (False, True, False, True), [pltpu.VMEM((2, tb, LANES), F32)]),
        out_shape=[jax.ShapeDtypeStruct((H, T, LANES), F32)],
        compiler_params=_params(("parallel", "arbitrary")),
    )(qk, fs, fe, q2, k_aug, do_aug, v_aug)[0]


def _flash_dkv(qk, fs, fe, q2, k_aug, do_aug, v_aug, name):
    H, T, _ = q2.shape
    tb = min(ATT_TB, T)
    nb = T // tb

    def body(qk_ref, fs_ref, fe_ref, q_ref, k_ref, do_ref, v_ref, dk_ref, dv_ref, dk_s, dv_s):
        h, i = 2 * pl.program_id(0), pl.program_id(1)
        dk_s[...] = jnp.zeros_like(dk_s)
        dv_s[...] = jnp.zeros_like(dv_s)

        def step(qb, masked):
            rows = pl.ds(pl.multiple_of(qb * tb, tb), tb)
            for e in range(2):
                q, dov = q_ref[e, rows, :], do_ref[e, rows, :]
                pt = jnp.exp(lax.dot_general(k_ref[e], q, NT, preferred_element_type=F32))
                if masked:
                    pt = jnp.where(
                        lax.broadcasted_iota(jnp.int32, (tb, tb), 1) >= lax.broadcasted_iota(jnp.int32, (tb, tb), 0),
                        pt, 0.0)
                dst = pt * lax.dot_general(v_ref[e], dov, NT, preferred_element_type=F32)
                dv_s[e] += lax.dot_general(pt.astype(BF16), dov, NN, preferred_element_type=F32)
                dk_s[e] += lax.dot_general(dst.astype(BF16), q, NN, preferred_element_type=F32)

        step(i, True)

        def loop_body(qb, carry):
            step(qb, False)
            return carry

        def live_after(head):
            base = qk_ref[head] - fe_ref[head, i]
            return lax.fori_loop(i + 1, nb, lambda b, n: n + jnp.where(base + fs_ref[head, b] < SKIP_BELOW, 0, 1), 0)

        lax.fori_loop(i + 1, i + 1 + jnp.maximum(live_after(h), live_after(h + 1)), loop_body, 0)
        dk_ref[...] = dk_s[...]
        dv_ref[...] = dv_s[...]

    return pl.pallas_call(
        body, name=name,
        grid_spec=_flash_spec(4, 2, H, nb, tb, T, (True, False, True, False),
                              [pltpu.VMEM((2, tb, LANES), F32), pltpu.VMEM((2, tb, LANES), F32)]),
        out_shape=[jax.ShapeDtypeStruct((H, T, LANES), F32)] * 2,
        compiler_params=_params(("parallel", "arbitrary")),
    )(qk, fs, fe, q2, k_aug, do_aug, v_aug)


def _add_pair(place, slab, got, name):
    _, half, cols = got.shape
    grid_spec = pltpu.PrefetchScalarGridSpec(
        num_scalar_prefetch=1, grid=(N_CHIPS,),
        in_specs=[pl.BlockSpec((None, half, cols), lambda j, s: (j, s[4], 0)),
                  pl.BlockSpec((None, half, cols), lambda j, s: (j, 0, 0))],
        out_specs=pl.BlockSpec((None, half, cols), lambda j, s: (j, 0, 0)))

    def body(s_ref, a_ref, b_ref, o_ref):
        o_ref[...] = (a_ref[...] + b_ref[...]).astype(BF16)

    return pl.pallas_call(
        body, name=name, grid_spec=grid_spec, out_shape=jax.ShapeDtypeStruct(got.shape, BF16),
        compiler_params=_params(("parallel",)),
    )(place, slab, got)


def _add_chips(place, pair, by_chip, name):
    _, half, cols = pair.shape
    tb = min(256, half)
    steps = half // tb
    slot = lambda n: pl.BlockSpec((None, tb, cols), lambda i, s: (s[n], i, 0))
    grid_spec = pltpu.PrefetchScalarGridSpec(
        num_scalar_prefetch=1, grid=(steps,), in_specs=[slot(0), slot(1), slot(2), slot(3)],
        out_specs=pl.BlockSpec((tb, cols), lambda i, s: (s[4] * steps + i, 0)))

    def body(s_ref, own_ref, b1_ref, b2_ref, b3_ref, o_ref):
        o_ref[...] = ((own_ref[...].astype(F32) + b1_ref[...].astype(F32)) + b2_ref[...].astype(F32)) \
            + b3_ref[...].astype(F32)

    return pl.pallas_call(
        body, name=name, grid_spec=grid_spec, out_shape=jax.ShapeDtypeStruct((2 * half, cols), F32),
        compiler_params=_params(("parallel",)),
    )(place, pair, by_chip, by_chip, by_chip)


def _add_slots(b, name, tb=128):
    n, rows, cols = b.shape
    tb = tb if rows % tb == 0 else rows
    specs = [pl.BlockSpec((None, tb, cols), functools.partial(lambda j, i: (j, i, 0), j)) for j in range(n)]

    def body(*refs):
        acc = refs[0][...]
        for r in refs[1:n]:
            acc = acc + r[...]
        refs[n][...] = acc

    return pl.pallas_call(
        body, name=name, grid=(rows // tb,), in_specs=specs, out_specs=pl.BlockSpec((tb, cols), lambda i: (i, 0)),
        out_shape=jax.ShapeDtypeStruct((rows, cols), F32), compiler_params=_params(("parallel",)),
    )(*([b] * n))


def _adamw(w, g, m, v, name):
    rows, cols = w.shape
    tr = 256 if rows % 256 == 0 else rows
    blk = pl.BlockSpec((tr, cols), lambda i: (i, 0))

    def body(w_ref, g_ref, m_ref, v_ref, d_ref, nm_ref, nv_ref):
        gv = g_ref[...]
        nm = ADAM_B1 * m_ref[...] + (1.0 - ADAM_B1) * gv
        nv = ADAM_B2 * v_ref[...] + (1.0 - ADAM_B2) * (gv * gv)
        m_hat = nm / (1.0 - ADAM_B1 ** ADAM_STEP)
        v_hat = nv / (1.0 - ADAM_B2 ** ADAM_STEP)
        d_ref[...] = -ADAM_LR * (m_hat / (jnp.sqrt(v_hat) + ADAM_EPS) + ADAM_WD * w_ref[...])
        nm_ref[...] = nm
        nv_ref[...] = nv

    return pl.pallas_call(
        body, name=name, grid=(rows // tr,), in_specs=[blk] * 4, out_specs=[blk] * 3,
        out_shape=[jax.ShapeDtypeStruct((rows, cols), F32)] * 3, compiler_params=_params(("parallel",)),
    )(w, g, m, v)


ANY = pl.BlockSpec(memory_space=pl.ANY)


def _place():
    x, y, c = lax.axis_index("x"), lax.axis_index("y"), lax.axis_index("c")
    others = [(1 - x, y), (x, 1 - y), (1 - x, 1 - y)]
    return x, y, c, 2 * x + y, others


def _half(ref, rows, h):
    return ref.at[pl.ds(h * (rows // 2), rows // 2), :]


def _gather_sems(n_u):
    return [pltpu.SemaphoreType.DMA((6 * n_u,)), pltpu.SemaphoreType.DMA((6 * n_u,)),
            pltpu.SemaphoreType.DMA((n_u,)), pltpu.SemaphoreType.DMA((n_u,))]


def _gather_copies(units, ins, outs, sems):
    send_sems, recv_sems, own_send_sems, own_recv_sems = sems
    x, y, c, me, others = _place()
    sibling = (x, y, 1 - c)

    def copy(u, k, chip, h, to, src=None):
        dst = _half(outs[u].at[chip], units[u][1][0], h)
        return pltpu.make_async_remote_copy(
            src_ref=dst if src is None else src, dst_ref=dst, send_sem=send_sems.at[6 * u + k],
            recv_sem=recv_sems.at[6 * u + k], device_id=to, device_id_type=MESH)

    pairs = [(u, j, 2 * ox + oy, (ox, oy)) for u in range(len(units)) for j, (ox, oy) in enumerate(others)]
    own = [pltpu.make_async_remote_copy(
        src_ref=ins[u], dst_ref=outs[u].at[me], send_sem=own_send_sems.at[u], recv_sem=own_recv_sems.at[u],
        device_id=sibling, device_id_type=MESH) for u in range(len(units))]
    first = [copy(u, j, me, c, (*to, c), src=_half(ins[u], units[u][1][0], c)) for u, j, _, to in pairs]
    arrivals = [copy(u, j, chip, c, (x, y, c)) for u, j, chip, _ in pairs]
    passed = [copy(u, 3 + j, chip, c, sibling) for u, j, chip, _ in pairs]
    from_sibling = [copy(u, 3 + j, chip, 1 - c, (x, y, c)) for u, j, chip, _ in pairs]
    return own, first, arrivals, passed, from_sibling


def _gather_start(units, ins, outs, sems):
    own, first, _, _, _ = _gather_copies(units, ins, outs, sems)
    for cp in own + first:
        cp.start()


def _gather_finish(units, ins, outs, sems):
    own, first, arrivals, passed, from_sibling = _gather_copies(units, ins, outs, sems)
    for arrived, onward in zip(arrivals, passed):
        arrived.wait_recv()
        onward.start()
    for cp in from_sibling:
        cp.wait_recv()
    for cp in first + passed:
        cp.wait_send()
    for cp in own:
        cp.wait()


def _allgather_weights(units, shards, name):
    n_u = len(units)

    def body(*refs):
        ins, outs, sems = refs[:n_u], refs[n_u:2 * n_u], refs[2 * n_u:]
        _gather_start(units, ins, outs, sems)
        _gather_finish(units, ins, outs, sems)

    return pl.pallas_call(
        body, name=name, in_specs=[ANY] * n_u, out_specs=[ANY] * n_u,
        out_shape=[jax.ShapeDtypeStruct((N_CHIPS,) + shape, BF16) for _, shape in units],
        scratch_shapes=_gather_sems(n_u),
    )(*shards)


def _sibling_exchange(units, slabs, name):
    n_u = len(units)

    def body(*refs):
        ins, got = refs[:n_u], refs[n_u:2 * n_u]
        send_sems, recv_sems = refs[2 * n_u:]
        x, y, c, _, _ = _place()
        copies = []
        for u, (_, (rows, _), _) in enumerate(units):
            half = rows // 2
            copies.append(pltpu.make_async_remote_copy(
                src_ref=ins[u].at[:, pl.ds((1 - c) * half, half), :], dst_ref=got[u], send_sem=send_sems.at[u],
                recv_sem=recv_sems.at[u], device_id=(x, y, 1 - c), device_id_type=MESH))
        for cp in copies:
            cp.start()
        for cp in copies:
            cp.wait()

    return pl.pallas_call(
        body, name=name, in_specs=[ANY] * n_u, out_specs=[ANY] * n_u,
        out_shape=[jax.ShapeDtypeStruct((N_CHIPS, rows // 2, cols), F32) for _, (rows, cols), _ in units],
        scratch_shapes=[pltpu.SemaphoreType.DMA((n_u,)), pltpu.SemaphoreType.DMA((n_u,))],
    )(*slabs)


def _exchange_sems(n_u):
    return [pltpu.SemaphoreType.DMA((3 * n_u,)), pltpu.SemaphoreType.DMA((3 * n_u,))]


def _exchange_copies(ins, outs, sems):
    send_sems, recv_sems = sems
    x, y, c, me, others = _place()
    sends = [pltpu.make_async_remote_copy(
        src_ref=ins[u].at[2 * ox + oy], dst_ref=outs[u].at[me], send_sem=send_sems.at[3 * u + j],
        recv_sem=recv_sems.at[3 * u + j], device_id=(ox, oy, c), device_id_type=MESH)
        for u in range(len(ins)) for j, (ox, oy) in enumerate(others)]
    arrivals = [pltpu.make_async_remote_copy(
        src_ref=ins[u].at[me], dst_ref=outs[u].at[2 * ox + oy], send_sem=send_sems.at[3 * u + j],
        recv_sem=recv_sems.at[3 * u + j], device_id=(x, y, c), device_id_type=MESH)
        for u in range(len(ins)) for j, (ox, oy) in enumerate(others)]
    return sends, arrivals


def _exchange_start(ins, outs, sems):
    for cp in _exchange_copies(ins, outs, sems)[0]:
        cp.start()


def _exchange_finish(ins, outs, sems):
    sends, arrivals = _exchange_copies(ins, outs, sems)
    for cp in arrivals:
        cp.wait_recv()
    for cp in sends:
        cp.wait_send()


def _chip_exchange(pairs, small):
    n_u = len(pairs)

    def body(*refs):
        ins, s_ref = refs[:n_u], refs[n_u]
        outs, sall_ref = refs[n_u + 1:2 * n_u + 1], refs[2 * n_u + 1]
        send_sems, recv_sems, ssend_sems, srecv_sems, local_sem = refs[2 * n_u + 2:]
        x, y, c, _, _ = _place()
        dev = 4 * x + 2 * y + c
        local = pltpu.make_async_copy(s_ref, sall_ref.at[dev], local_sem)
        flips = [(fx, fy, fc) for fx in (0, 1) for fy in (0, 1) for fc in (0, 1)][1:]
        small_sends = [pltpu.make_async_remote_copy(
            src_ref=s_ref, dst_ref=sall_ref.at[dev], send_sem=ssend_sems.at[n], recv_sem=srecv_sems.at[n],
            device_id=(x ^ fx, y ^ fy, c ^ fc), device_id_type=MESH) for n, (fx, fy, fc) in enumerate(flips)]
        local.start()
        _exchange_start(ins, outs, (send_sems, recv_sems))
        for cp in small_sends:
            cp.start()
        _exchange_finish(ins, outs, (send_sems, recv_sems))
        for n, (fx, fy, fc) in enumerate(flips):
            src_dev = 4 * (x ^ fx) + 2 * (y ^ fy) + (c ^ fc)
            pltpu.make_async_remote_copy(
                src_ref=s_ref, dst_ref=sall_ref.at[src_dev], send_sem=ssend_sems.at[n], recv_sem=srecv_sems.at[n],
                device_id=(x, y, c), device_id_type=MESH).wait_recv()
        for cp in small_sends:
            cp.wait_send()
        local.wait()

    outs = pl.pallas_call(
        body, name="grad_chip_exchange", in_specs=[ANY] * (n_u + 1), out_specs=[ANY] * (n_u + 1),
        out_shape=[jax.ShapeDtypeStruct(p.shape, p.dtype) for p in pairs]
        + [jax.ShapeDtypeStruct((8, SMALL_ALL_ROWS, LANES), F32)],
        scratch_shapes=_exchange_sems(n_u) + [pltpu.SemaphoreType.DMA((7,)), pltpu.SemaphoreType.DMA((7,)),
                                              pltpu.SemaphoreType.DMA],
    )(*pairs, small)
    return outs[:n_u], outs[n_u]


def _sibling_share(grads):
    n_u = len(GRAD_UNITS)

    def body(*refs):
        ins, outs = refs[:n_u], refs[n_u:2 * n_u]
        send_sems, recv_sems = refs[2 * n_u:]
        x, y, c, _, _ = _place()

        def rows_of(ref, u, h):
            half = GRAD_UNITS[u][1][0] // 2
            return ref.at[pl.ds(h * half, half), :]

        sends = [pltpu.make_async_remote_copy(
            src_ref=rows_of(ins[u], u, c), dst_ref=rows_of(outs[u], u, c), send_sem=send_sems.at[u],
            recv_sem=recv_sems.at[u], device_id=(x, y, 1 - c), device_id_type=MESH) for u in range(n_u)]
        for cp in sends:
            cp.start()
        for u in range(n_u):
            pltpu.make_async_remote_copy(
                src_ref=rows_of(ins[u], u, c), dst_ref=rows_of(outs[u], u, 1 - c), send_sem=send_sems.at[u],
                recv_sem=recv_sems.at[u], device_id=(x, y, c), device_id_type=MESH).wait_recv()
        for cp in sends:
            cp.wait_send()

    return pl.pallas_call(
        body, name="grad_sibling_share", in_specs=[ANY] * n_u, out_specs=[ANY] * n_u,
        out_shape=[jax.ShapeDtypeStruct(g.shape, F32) for g in grads],
        input_output_aliases={u: u for u in range(n_u)},
        scratch_shapes=[pltpu.SemaphoreType.DMA((n_u,)), pltpu.SemaphoreType.DMA((n_u,))],
    )(*grads)


def kernel(x, g_mix_pre, g_mix_post, g_ffn_pre, g_ffn_post, conv_pw1_w, conv_pw1_b, conv_dw_w, conv_dw_b, conv_ln_g, conv_ln_b, conv_pw2_w, conv_pw2_b, attn_w_in, attn_b_f, attn_w_o, mlp_w_up, mlp_w_down, loss_target, m_g_mix_pre, m_g_mix_post, m_g_ffn_pre, m_g_ffn_post, m_conv_pw1_w, m_conv_pw1_b, m_conv_dw_w, m_conv_dw_b, m_conv_ln_g, m_conv_ln_b, m_conv_pw2_w, m_conv_pw2_b, m_attn_w_in, m_attn_b_f, m_attn_w_o, m_mlp_w_up, m_mlp_w_down, v_g_mix_pre, v_g_mix_post, v_g_ffn_pre, v_g_ffn_post, v_conv_pw1_w, v_conv_pw1_b, v_conv_dw_w, v_conv_dw_b, v_conv_ln_g, v_conv_ln_b, v_conv_pw2_w, v_conv_pw2_b, v_attn_w_in, v_attn_b_f, v_attn_w_o, v_mlp_w_up, v_mlp_w_down):
    weights = dict(g_mix_pre=g_mix_pre, g_mix_post=g_mix_post, g_ffn_pre=g_ffn_pre, g_ffn_post=g_ffn_post, conv_pw1_w=conv_pw1_w, conv_pw1_b=conv_pw1_b, conv_dw_w=conv_dw_w, conv_dw_b=conv_dw_b, conv_ln_g=conv_ln_g, conv_ln_b=conv_ln_b, conv_pw2_w=conv_pw2_w, conv_pw2_b=conv_pw2_b, attn_w_in=attn_w_in, attn_b_f=attn_b_f, attn_w_o=attn_w_o, mlp_w_up=mlp_w_up, mlp_w_down=mlp_w_down)
    mom_m = dict(g_mix_pre=m_g_mix_pre, g_mix_post=m_g_mix_post, g_ffn_pre=m_g_ffn_pre, g_ffn_post=m_g_ffn_post, conv_pw1_w=m_conv_pw1_w, conv_pw1_b=m_conv_pw1_b, conv_dw_w=m_conv_dw_w, conv_dw_b=m_conv_dw_b, conv_ln_g=m_conv_ln_g, conv_ln_b=m_conv_ln_b, conv_pw2_w=m_conv_pw2_w, conv_pw2_b=m_conv_pw2_b, attn_w_in=m_attn_w_in, attn_b_f=m_attn_b_f, attn_w_o=m_attn_w_o, mlp_w_up=m_mlp_w_up, mlp_w_down=m_mlp_w_down)
    mom_v = dict(g_mix_pre=v_g_mix_pre, g_mix_post=v_g_mix_post, g_ffn_pre=v_g_ffn_pre, g_ffn_post=v_g_ffn_post, conv_pw1_w=v_conv_pw1_w, conv_pw1_b=v_conv_pw1_b, conv_dw_w=v_conv_dw_w, conv_dw_b=v_conv_dw_b, conv_ln_g=v_conv_ln_g, conv_ln_b=v_conv_ln_b, conv_pw2_w=v_conv_pw2_w, conv_pw2_b=v_conv_pw2_b, attn_w_in=v_attn_w_in, attn_b_f=v_attn_b_f, attn_w_o=v_attn_w_o, mlp_w_up=v_mlp_w_up, mlp_w_down=v_mlp_w_down)
    order = [n for n, _ in SMALL[:4]] + ["conv_pw1_w", "conv_pw1_b", "conv_dw_w", "conv_dw_b", "conv_ln_g", "conv_ln_b",
                                          "conv_pw2_w", "conv_pw2_b", "attn_w_in", "attn_b_f", "attn_w_o", "mlp_w_up",
                                          "mlp_w_down"]
    T = x.shape[1]
    x0 = x.reshape(T, D)
    target = loss_target.reshape(T, D)
    row = lambda a, l: a[l:l + 1, :]

    def shard(unit):
        if unit[:-1] in ("up", "down"):
            return weights["mlp_w_" + unit[:-1]][int(unit[-1])].astype(BF16)
        w = weights[unit][0].astype(BF16)
        return jnp.pad(w, ((0, HALO - CONV_W), (0, 0))) if unit == "conv_dw_w" else w

    by_cols = lambda g: g.transpose(1, 0, 2).reshape(g.shape[1], N_CHIPS * g.shape[2])
    by_rows = lambda g: g.reshape(N_CHIPS * g.shape[1], g.shape[2])
    g_pw1, g_dw, g_pw2 = _allgather_weights(GATHER_AT_START, [shard(n) for n, _ in GATHER_AT_START], "allgather_conv")
    w_pw1, w_pw2, w_dw = by_cols(g_pw1), by_rows(g_pw2), by_cols(g_dw).astype(F32)
    b_f = jnp.pad(attn_b_f, ((0, 0), (0, LANES - N_HEADS)))

    def mlp_fwd(h, l):
        (act,) = _matmul(h, w_up[l], "nn", (BF16,), _epi_sqrelu, f"mlp{l}_up")
        (mo,) = _matmul(act, w_down[l], "nn", (F32,), _epi_plain, f"mlp{l}_down", tk=DFF)
        return act, mo

    (h0,) = _resnorm_fwd(x0, None, None, row(g_mix_pre, 0), "norm_in")
    (a0,) = _matmul(h0, w_pw1, "nn", (F32,), _epi_bias, "conv_pw1", extras=((conv_pw1_b, "row"),))
    y0, z0, (g_up0, g_down0, g_win, g_wo) = _conv_fwd(
        a0, w_dw, conv_dw_b, conv_ln_g, conv_ln_b, "conv_fwd", GATHER_BEHIND_CONV,
        [shard(n) for n, _ in GATHER_BEHIND_CONV])
    w_up, w_down = [by_cols(g_up0)], [by_rows(g_down0)]
    w_o = by_rows(g_wo)
    w_in = jnp.pad(by_cols(g_win), ((0, 0), (0, W_IN_PAD - W_IN_COLS)))
    (m0,) = _matmul(z0, w_pw2, "nn", (F32,), _epi_bias, "conv_pw2", extras=((conv_pw2_b, "row"),))
    x1, h1 = _resnorm_fwd(x0, m0, row(g_mix_post, 0), row(g_ffn_pre, 0), "norm_conv_out")
    act0, mo0 = mlp_fwd(h1, 0)
    x2, h2 = _resnorm_fwd(x1, mo0, row(g_ffn_post, 0), row(g_mix_pre, 1), "norm_mlp0_out")

    (proj,) = _matmul(h2, w_in, "nn", (F32,), _epi_plain, "attn_in", tn=640)
    fl = proj[:, 3 * D:]
    f_all, f1, f2, f3 = _gate_fwd(fl, b_f, "gate_fwd")
    (q_aug, k_aug, v_aug, qn2, kn2), (g_up1, g_down1) = _attn_pack(
        proj, f1, f2, f3, "attn_pack", GATHER_BEHIND_PACK, [shard(n) for n, _ in GATHER_BEHIND_PACK])
    w_up.append(by_cols(g_up1))
    w_down.append(by_rows(g_down1))
    tb = min(ATT_TB, T)
    qk = 2.02 * jnp.sqrt(qn2[:, 0, 0] * kn2[:, 0, 0]) + 1.0
    fs, fe = f_all[0::tb, :N_HEADS].T, f_all[tb - 1::tb, :N_HEADS].T
    o_aug, q2_aug = _flash_fwd(qk, fs, fe, jnp.sqrt(kn2[:, 0, 0]), q_aug, k_aug, v_aug, "flash_fwd")
    (o_bf,) = _merge_heads(o_aug, 1.0, BF16, "attn_merge_o")
    (m1,) = _matmul(o_bf, w_o, "nn", (F32,), _epi_plain, "attn_out")
    x3, h3 = _resnorm_fwd(x2, m1, row(g_mix_post, 1), row(g_ffn_pre, 1), "norm_attn_out")
    act1, mo1 = mlp_fwd(h3, 1)
    dy, loss_part = _resnorm_loss(x3, mo1, row(g_ffn_post, 1), target, "loss")
    loss = lax.psum(loss_part[0, 0], ("x", "y", "c"))

    G = {}
    chip = 2 * lax.axis_index("x") + lax.axis_index("y")
    place = jnp.stack([chip] + [k + (k >= chip).astype(jnp.int32) for k in range(N_CHIPS - 1)]
                      + [lax.axis_index("c")]).astype(jnp.int32)

    def mlp_bwd(d_out, mo, act, h, x_in, l):
        d_mo, dg_post, _ = _norm_bwd(mo, row(g_ffn_post, l), d_out, None, BF16, f"mlp{l}_post_bwd")
        (d_up,) = _matmul(d_mo, w_down[l], "nt", (BF16,), _epi_dsqrelu, f"mlp{l}_dact", extras=((act, "tile"),))
        (dw_down,) = _matmul(act, d_mo, "tn", (F32,), _epi_plain, tk=2048, name=f"mlp{l}_dwdown")
        (dw_up,) = _matmul(h, d_up, "tn", (F32,), _epi_plain, tk=2048, name=f"mlp{l}_dwup", shard_out=True)
        (d_h,) = _matmul(d_up, w_up[l], "nt", (F32,), _epi_plain, f"mlp{l}_dh", tk=DFF)
        d_in, dg_pre, _ = _norm_bwd(x_in, row(g_ffn_pre, l), d_h, d_out, F32, f"mlp{l}_pre_bwd")
        G[f"up{l}"], G[f"down{l}"] = dw_up, dw_down.reshape(N_CHIPS, D, D)
        return d_in, dg_post, dg_pre

    d_x3, dg_ffn_post1, dg_ffn_pre1 = mlp_bwd(dy, mo1, act1, h3, x3, 1)

    d_m1, dg_mix_post1, _ = _norm_bwd(m1, row(g_mix_post, 1), d_x3, None, BF16, "attn_post_bwd")
    (dw_o,) = _matmul(o_bf, d_m1, "tn", (F32,), _epi_plain, tk=2048, name="attn_dwo")
    G["attn_w_o"] = dw_o.reshape(N_CHIPS, D // N_CHIPS, D)
    (d_o,) = _matmul(d_m1, w_o, "nt", (F32,), _epi_plain, "attn_do")
    do_aug = _attn_prep(d_o, o_aug, "attn_prep")
    dq_aug = _flash_dq(qk, fs, fe, q2_aug, k_aug, do_aug, v_aug, "flash_dq")
    dk_aug, dv_aug = _flash_dkv(qk, fs, fe, q2_aug, k_aug, do_aug, v_aug, "flash_dkv")
    d_q, dF_q = _merge_heads(dq_aug, 0.125, BF16, "attn_merge_dq", column=AUG_F)
    d_k, dF_k = _merge_heads(dk_aug, 1.0, BF16, "attn_merge_dk", column=AUG_ONE)
    (d_v,) = _merge_heads(dv_aug, 1.0, BF16, "attn_merge_dv")
    d_fl, db_f = _gate_bwd(dF_q, dF_k, fl, b_f, "gate_bwd")
    d_proj = jnp.concatenate([d_q, d_k, d_v, d_fl.astype(BF16)], axis=1)
    (dw_in,) = _matmul(h2, d_proj, "tn", (F32,), _epi_plain, tk=2048, name="attn_dwin", tn=640)
    G["attn_w_in"] = dw_in[:, :W_IN_COLS].reshape(D, N_CHIPS, W_IN_COLS // N_CHIPS).transpose(1, 0, 2)
    (d_h2,) = _matmul(d_proj, w_in, "nt", (F32,), _epi_plain, "attn_dh", tk=W_IN_PAD)
    d_x2, dg_mix_pre1, _ = _norm_bwd(x2, row(g_mix_pre, 1), d_h2, d_x3, F32, "attn_pre_bwd")

    d_x1, dg_ffn_post0, dg_ffn_pre0 = mlp_bwd(d_x2, mo0, act0, h1, x1, 0)

    d_m0, dg_mix_post0, db_pw2 = _norm_bwd(m0, row(g_mix_post, 0), d_x1, None, BF16, "conv_post_bwd")
    (dw_pw2,) = _matmul(z0, d_m0, "tn", (F32,), _epi_plain, tk=2048, name="conv_dwpw2")
    G["conv_pw2_w"] = dw_pw2.reshape(N_CHIPS, D // N_CHIPS, D)
    (d_z0,) = _matmul(d_m0, w_pw2, "nt", (F32,), _epi_plain, "conv_dz")
    d_y0, dln_g, dln_b, ddw_b = _conv_bwd_ln(d_z0, y0, conv_ln_g, conv_ln_b, "conv_bwd_ln")
    conv_units = [u for u in GRAD_UNITS if u[0].startswith("conv")]
    attn_units = [u for u in GRAD_UNITS if u not in conv_units]
    attn_slabs = [G[n] for n, _, _ in attn_units]
    attn_got = _sibling_exchange(attn_units, attn_slabs, "grad_sibling_exchange_attn")
    pairs = {n: _add_pair(place, s, g, f"grad_pair_{n}") for s, g, (n, _, _) in zip(attn_slabs, attn_got, attn_units)}
    d_a0, db_pw1, ddw_w, attn_by_chip = _conv_bwd_dw(d_y0, a0, w_dw, "conv_bwd_dw",
                                                    [pairs[n] for n, _, _ in attn_units])
    by_chip = {n: b for b, (n, _, _) in zip(attn_by_chip, attn_units)}
    (G["conv_pw1_w"],) = _matmul(h0, d_a0, "tn", (F32,), _epi_plain, tk=2048, name="conv_dwpw1", tn=512, shard_out=True)
    (d_h0,) = _matmul(d_a0, w_pw1, "nt", (F32,), _epi_plain, "conv_dh", tk=2 * D)
    d_x0, dg_mix_pre0, _ = _norm_bwd(x0, row(g_mix_pre, 0), d_h0, d_x1, F32, "conv_pre_bwd")

    small_local = {
        "g_mix_pre": jnp.concatenate([dg_mix_pre0, dg_mix_pre1]), "g_mix_post": jnp.concatenate([dg_mix_post0, dg_mix_post1]),
        "g_ffn_pre": jnp.concatenate([dg_ffn_pre0, dg_ffn_pre1]), "g_ffn_post": jnp.concatenate([dg_ffn_post0, dg_ffn_post1]),
        "conv_pw1_b": db_pw1, "conv_dw_b": ddw_b, "conv_ln_g": dln_g, "conv_ln_b": dln_b, "conv_pw2_b": db_pw2,
        "attn_b_f": db_f[:, :N_HEADS], "conv_dw_w_full": ddw_w,
    }

    conv_slabs = [G[n] for n, _, _ in conv_units]
    conv_got = _sibling_exchange(conv_units, conv_slabs, "grad_sibling_exchange_conv")
    pairs.update({n: _add_pair(place, s, g, f"grad_pair_{n}")
                  for s, g, (n, _, _) in zip(conv_slabs, conv_got, conv_units)})
    conv_by_chip, small_all = _chip_exchange([pairs[n] for n, _, _ in conv_units],
                                             _pack([small_local[n] for n, _ in SMALL_ALL], SMALL_ALL_ROWS))
    by_chip.update({n: b for b, (n, _, _) in zip(conv_by_chip, conv_units)})
    unit_grads = _sibling_share([_add_chips(place, pairs[n], by_chip[n], f"grad_chips_{n}") for n, _, _ in GRAD_UNITS])
    grads = _unpack(_add_slots(small_all, "grad_add_small", tb=SMALL_ALL_ROWS), SMALL_ALL)
    grads["conv_dw_w"] = lax.dynamic_slice(grads.pop("conv_dw_w_full"), (0, chip * 256), (CONV_W, 256))[None]

    delta, new_m, new_v = {}, {}, {}
    per_layer = {}
    for g, (unit, shape, (name, layer)) in zip(unit_grads, GRAD_UNITS):
        view = lambda d: (d[name] if layer is None else d[name][layer]).reshape(shape)
        per_layer.setdefault(name, []).append((g,) + tuple(_adamw(view(weights), g, view(mom_m), view(mom_v),
                                                                  f"adamw_{unit}")))
    for name, parts in per_layer.items():
        full = weights[name].shape
        join = lambda n: (parts[0][n] if len(parts) == 1 else jnp.stack([p[n] for p in parts])).reshape(full)
        grads[name], delta[name], new_m[name], new_v[name] = join(0), join(1), join(2), join(3)
    dw2 = lambda d: d["conv_dw_w"].reshape(CONV_W, 256)
    d, nm, nv = _adamw(dw2(weights), dw2(grads), dw2(mom_m), dw2(mom_v), "adamw_conv_dw_w")
    full = weights["conv_dw_w"].shape
    delta["conv_dw_w"], new_m["conv_dw_w"], new_v["conv_dw_w"] = d.reshape(full), nm.reshape(full), nv.reshape(full)
    packs = [_pack([src[n] for n, _ in SMALL], SMALL_ROWS) for src in (weights, grads, mom_m, mom_v)]
    d, nm, nv = _adamw(*packs, "adamw_small")
    for dst, packed in ((delta, d), (new_m, nm), (new_v, nv)):
        dst.update(_unpack(packed, SMALL))

    grad_x = d_x0.reshape(x.shape)
    return (loss, grad_x, *[grads[n] for n in order], *[delta[n] for n in order], *[new_m[n] for n in order],
            *[new_v[n] for n in order])
```

```python
import functools

import jax
import jax.numpy as jnp
from jax import lax
from jax.experimental import pallas as pl
from jax.experimental.pallas import tpu as pltpu

F32 = jnp.float32
BF16 = jnp.bfloat16
MESH = pl.DeviceIdType.MESH

D = 1024
DFF = 4096
N_HEADS = 16
HEAD_DIM = 64
CONV_W = 31
HALO = 32
RMS_EPS = 1e-6
LN_EPS = 1e-5
MASK_VALUE = -1e30
W_IN_COLS = 3 * D + N_HEADS
W_IN_PAD = 3200
LANES = 128
SUBLANES = 8
N_CHIPS = 4
VMEM_LIMIT = 56 * 1024 * 1024
MATMUL_CHUNK = 256

ADAM_LR = 0.001
ADAM_B1 = 0.9
ADAM_B2 = 0.999
ADAM_EPS = 1e-08
ADAM_WD = 0.01
ADAM_STEP = 10

AUG_F = 64
AUG_ONE = 67
AUG_L = 70
SKIP_BELOW = -104.0
FIXED_MAX_BELOW = 40.0

GATHER_AT_START = (("conv_pw1_w", (1024, 512)), ("conv_dw_w", (HALO, 256)), ("conv_pw2_w", (256, 1024)))
GATHER_BEHIND_CONV = (("up0", (1024, 1024)), ("down0", (1024, 1024)), ("attn_w_in", (1024, 772)),
                      ("attn_w_o", (256, 1024)))
GATHER_BEHIND_PACK = (("up1", (1024, 1024)), ("down1", (1024, 1024)))
GRAD_UNITS = (
    ("conv_pw1_w", (1024, 512), ("conv_pw1_w", None)),
    ("conv_pw2_w", (256, 1024), ("conv_pw2_w", None)),
    ("attn_w_in", (1024, 772), ("attn_w_in", None)),
    ("attn_w_o", (256, 1024), ("attn_w_o", None)),
    ("up0", (1024, 1024), ("mlp_w_up", 0)),
    ("up1", (1024, 1024), ("mlp_w_up", 1)),
    ("down0", (1024, 1024), ("mlp_w_down", 0)),
    ("down1", (1024, 1024), ("mlp_w_down", 1)),
)
SMALL = (
    ("g_mix_pre", (2, 1024)), ("g_mix_post", (2, 1024)), ("g_ffn_pre", (2, 1024)), ("g_ffn_post", (2, 1024)),
    ("conv_pw1_b", (1, 2048)), ("conv_dw_b", (1, 1024)), ("conv_ln_g", (1, 1024)), ("conv_ln_b", (1, 1024)),
    ("conv_pw2_b", (1, 1024)), ("attn_b_f", (1, 16)),
)
SMALL_ROWS = 120
SMALL_ALL = SMALL + (("conv_dw_w_full", (HALO, 1024)),)
SMALL_ALL_ROWS = 376


def _size(shape):
    n = 1
    for s in shape:
        n *= s
    return n


def _pack(arrays, rows):
    flat = jnp.concatenate([a.reshape(-1) for a in arrays])
    return jnp.pad(flat, (0, rows * LANES - flat.shape[0])).reshape(rows, LANES)


def _unpack(packed, table):
    flat = packed.reshape(-1)
    out, off = {}, 0
    for name, shape in table:
        n = _size(shape)
        out[name] = flat[off:off + n].reshape(shape)
        off += n
    return out


def _params(sem):
    return pltpu.CompilerParams(dimension_semantics=sem, vmem_limit_bytes=VMEM_LIMIT)


def _rms(x, g):
    return x * lax.rsqrt(jnp.mean(x * x, axis=-1, keepdims=True) + RMS_EPS) * g


def _split3(v):
    p1 = v.astype(BF16).astype(F32)
    r = v - p1
    p2 = r.astype(BF16).astype(F32)
    p3 = (r - p2).astype(BF16).astype(F32)
    return p1, p2, p3


def _matmul(a, b, mode, out_dtypes, epi, name, extras=(), tm=1024, tn=1024, tk=1024, shard_out=False):
    (K, M) = a.shape if mode == "tn" else a.shape[::-1]
    N = b.shape[0] if mode == "nt" else b.shape[1]
    tm, tn, tk = min(tm, M), min(tn, N), min(tk, K)
    nk = K // tk
    if mode == "tn":
        a_spec = pl.BlockSpec((tk, tm), lambda i, j, k: (k, i))
    else:
        a_spec = pl.BlockSpec((tm, tk), lambda i, j, k: (i, k))
    if mode == "nt":
        b_spec = pl.BlockSpec((tn, tk), lambda i, j, k: (j, k))
    else:
        b_spec = pl.BlockSpec((tk, tn), lambda i, j, k: (k, j))
    if shard_out:
        per = N // N_CHIPS // tn
        o_spec = pl.BlockSpec((None, tm, tn), lambda i, j, k: (j // per, i, j % per))
        o_shape = (N_CHIPS, M, N // N_CHIPS)
    else:
        o_spec = pl.BlockSpec((tm, tn), lambda i, j, k: (i, j))
        o_shape = (M, N)
    dims = {"nn": (((1,), (0,)), ((), ())), "nt": (((1,), (1,)), ((), ())), "tn": (((0,), (0,)), ((), ()))}[mode]
    ex_specs = []
    for _, kind in extras:
        if kind == "row":
            ex_specs.append(pl.BlockSpec((1, tn), lambda i, j, k: (0, j)))
        else:
            ex_specs.append(pl.BlockSpec((tm, tn), lambda i, j, k: (i, j)))
    n_ex, n_out = len(extras), len(out_dtypes)

    cw = MATMUL_CHUNK if tn % MATMUL_CHUNK == 0 else tn

    def body(*refs):
        a_ref, b_ref = refs[0], refs[1]
        ex = refs[2:2 + n_ex]
        outs = refs[2 + n_ex:2 + n_ex + n_out]

        def for_chunks(use):
            for c in range(tn // cw):
                cols = slice(c * cw, (c + 1) * cw)
                b_chunk = b_ref[cols, :] if mode == "nt" else b_ref[:, cols]
                use(cols, lax.dot_general(a_ref[...], b_chunk, dims, preferred_element_type=F32))

        def finish(cols, acc):
            res = epi(acc, *[e[:, cols] for e in ex])
            for o, r in zip(outs, res):
                o[:, cols] = r.astype(o.dtype)

        if nk == 1:
            for_chunks(finish)
        else:
            acc_ref = refs[-1]
            k = pl.program_id(2)

            def first(cols, prod):
                acc_ref[:, cols] = prod

            def middle(cols, prod):
                acc_ref[:, cols] += prod

            pl.when(k == 0)(lambda: for_chunks(first))
            pl.when(jnp.logical_and(k > 0, k < nk - 1))(lambda: for_chunks(middle))
            pl.when(k == nk - 1)(lambda: for_chunks(lambda cols, prod: finish(cols, acc_ref[:, cols] + prod)))

    return pl.pallas_call(
        body,
        name=name,
        grid=(M // tm, N // tn, nk),
        in_specs=[a_spec, b_spec] + ex_specs,
        out_specs=[o_spec for _ in out_dtypes],
        out_shape=[jax.ShapeDtypeStruct(o_shape, dt) for dt in out_dtypes],
        scratch_shapes=[pltpu.VMEM((tm, tn), F32)] if nk > 1 else [],
        compiler_params=_params(("parallel", "parallel", "arbitrary")),
    )(a, b, *[e for e, _ in extras])


def _epi_plain(acc):
    return (acc,)


def _epi_bias(acc, bias):
    return (acc + bias,)


def _epi_sqrelu(acc):
    r = jnp.maximum(acc, 0.0)
    return (r * r,)


def _epi_dsqrelu(acc, act):
    return (acc * (2.0 * jnp.sqrt(act.astype(F32))),)


def _resnorm_fwd(x, m, g_post, g_next, name, tr=512):
    T = x.shape[0]
    has_m = m is not None
    row = pl.BlockSpec((tr, D), lambda i: (i, 0))
    vec = pl.BlockSpec((1, D), lambda i: (0, 0))

    def body(*refs):
        if has_m:
            x_ref, m_ref, gp_ref, gn_ref, x1_ref, h_ref = refs
            x1 = x_ref[...] + _rms(m_ref[...], gp_ref[...])
            x1_ref[...] = x1
        else:
            x_ref, gn_ref, h_ref = refs
            x1 = x_ref[...]
        h_ref[...] = _rms(x1, gn_ref[...]).astype(BF16)

    if has_m:
        args, in_specs = (x, m, g_post, g_next), [row, row, vec, vec]
        out_specs = [row, row]
        out_shape = [jax.ShapeDtypeStruct((T, D), F32), jax.ShapeDtypeStruct((T, D), BF16)]
    else:
        args, in_specs = (x, g_next), [row, vec]
        out_specs = [row]
        out_shape = [jax.ShapeDtypeStruct((T, D), BF16)]
    return pl.pallas_call(
        body, name=name, grid=(T // tr,), in_specs=in_specs, out_specs=out_specs, out_shape=out_shape,
        compiler_params=_params(("parallel",)),
    )(*args)


def _resnorm_loss(x, m, g_post, target, name, tr=512):
    T = x.shape[0]
    row = pl.BlockSpec((tr, D), lambda i: (i, 0))
    vec = pl.BlockSpec((1, D), lambda i: (0, 0))

    def body(x_ref, m_ref, gp_ref, t_ref, dy_ref, loss_ref):
        diff = x_ref[...] + _rms(m_ref[...], gp_ref[...]) - t_ref[...]
        dy_ref[...] = diff * (1.0 / D)

        @pl.when(pl.program_id(0) == 0)
        def _():
            loss_ref[...] = jnp.zeros_like(loss_ref)

        per_token = jnp.mean(diff * diff, axis=-1, keepdims=True)
        loss_ref[...] += 0.5 * jnp.sum(per_token)

    return pl.pallas_call(
        body, name=name, grid=(T // tr,), in_specs=[row, row, vec, row],
        out_specs=[row, pl.BlockSpec((8, LANES), lambda i: (0, 0))],
        out_shape=[jax.ShapeDtypeStruct((T, D), F32), jax.ShapeDtypeStruct((8, LANES), F32)],
        compiler_params=_params(("arbitrary",)),
    )(x, m, g_post, target)


def _norm_bwd(x, g, dy, resid, out_dtype, name, tr=512):
    T = x.shape[0]
    has_r = resid is not None
    row = pl.BlockSpec((tr, D), lambda i: (i, 0))
    vec = pl.BlockSpec((1, D), lambda i: (0, 0))

    def body(*refs):
        if has_r:
            x_ref, g_ref, dy_ref, r_ref, dx_ref, dg_ref, cs_ref = refs
        else:
            x_ref, g_ref, dy_ref, dx_ref, dg_ref, cs_ref = refs
        xv, dyv = x_ref[...], dy_ref[...]
        r = lax.rsqrt(jnp.mean(xv * xv, axis=-1, keepdims=True) + RMS_EPS)
        gy = dyv * g_ref[...]
        c = jnp.sum(gy * xv, axis=-1, keepdims=True) * (1.0 / D)
        dx = r * gy - xv * (r * r * r * c)

        @pl.when(pl.program_id(0) == 0)
        def _():
            dg_ref[...] = jnp.zeros_like(dg_ref)
            cs_ref[...] = jnp.zeros_like(cs_ref)

        dg_ref[...] += jnp.sum(dyv * (xv * r), axis=0, keepdims=True)
        cs_ref[...] += jnp.sum(dx, axis=0, keepdims=True)
        if has_r:
            dx = dx + r_ref[...]
        dx_ref[...] = dx.astype(dx_ref.dtype)

    args = (x, g, dy) + ((resid,) if has_r else ())
    return pl.pallas_call(
        body, name=name, grid=(T // tr,), in_specs=[row, vec, row] + ([row] if has_r else []),
        out_specs=[row, vec, vec],
        out_shape=[jax.ShapeDtypeStruct((T, D), out_dtype), jax.ShapeDtypeStruct((1, D), F32),
                   jax.ShapeDtypeStruct((1, D), F32)],
        compiler_params=_params(("arbitrary",)),
    )(*args)


CONV_TB = 256
CONV_RC = 32


def _glu(a):
    return a[:, :D] * jax.nn.sigmoid(a[:, D:])


def _shifted_copies(win):
    rows = win.shape[1]
    for p in range(1, SUBLANES):
        win[p, 0:rows - SUBLANES, :] = win[0, p:p + rows - SUBLANES, :]


def _window(win, start):
    p = start % SUBLANES
    return win[p, start - p:start - p + CONV_RC, :]


def _conv_fwd(a, dw_w, dw_b, ln_g, ln_b, name, gather_units, gather_shards):
    T = a.shape[0]
    tb = min(CONV_TB, T)
    per = tb // HALO
    n_steps = T // tb
    n_g = len(gather_units)
    vec = pl.BlockSpec((1, D), lambda i: (0, 0))

    def body(*refs):
        cur_ref, prev_ref, w_ref, b_ref, lg_ref, lb_ref = refs[:6]
        g_ins = refs[6:6 + n_g]
        y_ref, z_ref = refs[6 + n_g:8 + n_g]
        g_outs = refs[8 + n_g:8 + 2 * n_g]
        uwin = refs[8 + 2 * n_g]
        sems = refs[9 + 2 * n_g:]
        i = pl.program_id(0)
        pl.when(i == 0)(lambda: _gather_start(gather_units, g_ins, g_outs, sems))
        uwin[0, 0:HALO, :] = jnp.where(i > 0, _glu(prev_ref[...]), 0.0)
        uwin[0, HALO:, :] = _glu(cur_ref[...])
        _shifted_copies(uwin)
        for r in range(tb // CONV_RC):
            acc = jnp.broadcast_to(b_ref[...], (CONV_RC, D))
            for k in range(CONV_W):
                start = r * CONV_RC + HALO - (CONV_W - 1) + k
                acc = acc + _window(uwin, start) * w_ref[k:k + 1, :]
            rows = slice(r * CONV_RC, (r + 1) * CONV_RC)
            y_ref[rows, :] = acc
            mu = jnp.mean(acc, axis=-1, keepdims=True)
            xc = acc - mu
            var = jnp.mean(xc * xc, axis=-1, keepdims=True)
            n = xc * lax.rsqrt(var + LN_EPS) * lg_ref[...] + lb_ref[...]
            z_ref[rows, :] = (n * jax.nn.sigmoid(n)).astype(BF16)
        pl.when(i == n_steps - 1)(lambda: _gather_finish(gather_units, g_ins, g_outs, sems))

    outs = pl.pallas_call(
        body, name=name, grid=(n_steps,),
        in_specs=[pl.BlockSpec((tb, 2 * D), lambda i: (i, 0)),
                  pl.BlockSpec((HALO, 2 * D), lambda i: (jnp.maximum(i * per - 1, 0), 0)),
                  pl.BlockSpec((HALO, D), lambda i: (0, 0)), vec, vec, vec] + [ANY] * n_g,
        out_specs=[pl.BlockSpec((tb, D), lambda i: (i, 0)), pl.BlockSpec((tb, D), lambda i: (i, 0))] + [ANY] * n_g,
        out_shape=[jax.ShapeDtypeStruct((T, D), F32), jax.ShapeDtypeStruct((T, D), BF16)]
        + [jax.ShapeDtypeStruct((N_CHIPS,) + shape, BF16) for _, shape in gather_units],
        scratch_shapes=[pltpu.VMEM((SUBLANES, tb + HALO, D), F32)] + _gather_sems(n_g),
        compiler_params=_params(("arbitrary",)),
    )(a, a, dw_w, dw_b, ln_g, ln_b, *gather_shards)
    return outs[0], outs[1], outs[2:]


def _conv_bwd_ln(dz, y, ln_g, ln_b, name, exchange_units, slabs, tr=256):
    T = y.shape[0]
    tr = min(tr, T)
    n_steps = T // tr
    n_s = len(slabs)
    row = pl.BlockSpec((tr, D), lambda i: (i, 0))
    vec = pl.BlockSpec((1, D), lambda i: (0, 0))

    def body(*refs):
        dz_ref, y_ref, lg_ref, lb_ref = refs[:4]
        s_ins = refs[4:4 + n_s]
        dy_ref, dlg_ref, dlb_ref, db_ref = refs[4 + n_s:8 + n_s]
        s_outs = refs[8 + n_s:8 + 2 * n_s]
        sems = refs[8 + 2 * n_s:]
        pl.when(pl.program_id(0) == 0)(lambda: _sibling_start(exchange_units, s_ins, s_outs, sems))
        yv = y_ref[...]
        mu = jnp.mean(yv, axis=-1, keepdims=True)
        xc = yv - mu
        rstd = lax.rsqrt(jnp.mean(xc * xc, axis=-1, keepdims=True) + LN_EPS)
        yh = xc * rstd
        n = yh * lg_ref[...] + lb_ref[...]
        sg = jax.nn.sigmoid(n)
        dn = dz_ref[...] * (sg * (1.0 + n * (1.0 - sg)))
        dyh = dn * lg_ref[...]
        dyv = rstd * (dyh - jnp.mean(dyh, axis=-1, keepdims=True) - yh * jnp.mean(dyh * yh, axis=-1, keepdims=True))
        dy_ref[...] = dyv

        @pl.when(pl.program_id(0) == 0)
        def _():
            dlg_ref[...] = jnp.zeros_like(dlg_ref)
            dlb_ref[...] = jnp.zeros_like(dlb_ref)
            db_ref[...] = jnp.zeros_like(db_ref)

        dlg_ref[...] += jnp.sum(dn * yh, axis=0, keepdims=True)
        dlb_ref[...] += jnp.sum(dn, axis=0, keepdims=True)
        db_ref[...] += jnp.sum(dyv, axis=0, keepdims=True)
        pl.when(pl.program_id(0) == n_steps - 1)(lambda: _sibling_finish(exchange_units, s_ins, s_outs, sems))

    outs = pl.pallas_call(
        body, name=name, grid=(n_steps,), in_specs=[row, row, vec, vec] + [ANY] * n_s,
        out_specs=[row, vec, vec, vec] + [ANY] * n_s,
        out_shape=[jax.ShapeDtypeStruct((T, D), F32)] + [jax.ShapeDtypeStruct((1, D), F32)] * 3
        + [jax.ShapeDtypeStruct((N_CHIPS, rows // 2, cols), F32) for _, (rows, cols), _ in exchange_units],
        scratch_shapes=[pltpu.SemaphoreType.DMA((n_s,)), pltpu.SemaphoreType.DMA((n_s,))],
        compiler_params=_params(("arbitrary",)),
    )(dz, y, ln_g, ln_b, *slabs)
    return outs[0], outs[1], outs[2], outs[3], outs[4:]


def _conv_bwd_dw(dy, a, dw_w, name, exchange):
    T = a.shape[0]
    tb = min(CONV_TB, T)
    per = tb // HALO
    last_halo = T // HALO - 1
    n_steps = T // tb
    n_x = len(exchange)

    def body(*refs):
        dyc_ref, dyn_ref, cur_ref, prev_ref, w_ref = refs[:5]
        x_ins = refs[5:5 + n_x]
        da_ref, dbias_ref, dw_ref = refs[5 + n_x:8 + n_x]
        x_outs = refs[8 + n_x:8 + 2 * n_x]
        uwin, dywin, dwacc = refs[8 + 2 * n_x:11 + 2 * n_x]
        sems = refs[11 + 2 * n_x:]
        i = pl.program_id(0)

        @pl.when(i == 0)
        def _():
            _exchange_start(x_ins, x_outs, sems)
            dbias_ref[...] = jnp.zeros_like(dbias_ref)
            dwacc[...] = jnp.zeros_like(dwacc)

        uwin[0, 0:HALO, :] = jnp.where(i > 0, _glu(prev_ref[...]), 0.0)
        uwin[0, HALO:, :] = _glu(cur_ref[...])
        dywin[0, 0:tb, :] = dyc_ref[...]
        dywin[0, tb:, :] = jnp.where(i < n_steps - 1, dyn_ref[...], 0.0)
        _shifted_copies(uwin)
        _shifted_copies(dywin)
        for r in range(tb // CONV_RC):
            rows = slice(r * CONV_RC, (r + 1) * CONV_RC)
            dy_c = dywin[0, rows, :]
            du = jnp.zeros((CONV_RC, D), F32)
            for k in range(CONV_W):
                du = du + _window(dywin, r * CONV_RC + (CONV_W - 1) - k) * w_ref[k:k + 1, :]
                prod = dy_c * _window(uwin, r * CONV_RC + HALO - (CONV_W - 1) + k)
                part = prod[0:8, :]
                for q in range(1, CONV_RC // 8):
                    part = part + prod[8 * q:8 * q + 8, :]
                dwacc[8 * k:8 * k + 8, :] += part
            av = cur_ref[rows, :]
            a1, sg = av[:, :D], jax.nn.sigmoid(av[:, D:])
            da1 = du * sg
            da2 = du * a1 * (sg * (1.0 - sg))
            da_ref[rows, 0:D] = da1.astype(BF16)
            da_ref[rows, D:] = da2.astype(BF16)
            dbias_ref[:, 0:D] += jnp.sum(da1, axis=0, keepdims=True)
            dbias_ref[:, D:] += jnp.sum(da2, axis=0, keepdims=True)

        @pl.when(i == n_steps - 1)
        def _():
            for k in range(CONV_W):
                dw_ref[k:k + 1, :] = jnp.sum(dwacc[8 * k:8 * k + 8, :], axis=0, keepdims=True)
            dw_ref[CONV_W:, :] = jnp.zeros((HALO - CONV_W, D), F32)
            _exchange_finish(x_ins, x_outs, sems)

    outs = pl.pallas_call(
        body, name=name, grid=(n_steps,),
        in_specs=[pl.BlockSpec((tb, D), lambda i: (i, 0)),
                  pl.BlockSpec((HALO, D), lambda i: (jnp.minimum((i + 1) * per, last_halo), 0)),
                  pl.BlockSpec((tb, 2 * D), lambda i: (i, 0)),
                  pl.BlockSpec((HALO, 2 * D), lambda i: (jnp.maximum(i * per - 1, 0), 0)),
                  pl.BlockSpec((HALO, D), lambda i: (0, 0))] + [ANY] * n_x,
        out_specs=[pl.BlockSpec((tb, 2 * D), lambda i: (i, 0)), pl.BlockSpec((1, 2 * D), lambda i: (0, 0)),
                   pl.BlockSpec((HALO, D), lambda i: (0, 0))] + [ANY] * n_x,
        out_shape=[jax.ShapeDtypeStruct((T, 2 * D), BF16), jax.ShapeDtypeStruct((1, 2 * D), F32),
                   jax.ShapeDtypeStruct((HALO, D), F32)] + [jax.ShapeDtypeStruct(p.shape, p.dtype) for p in exchange],
        scratch_shapes=[pltpu.VMEM((SUBLANES, tb + HALO, D), F32), pltpu.VMEM((SUBLANES, tb + HALO, D), F32),
                        pltpu.VMEM((8 * HALO, D), F32)] + _exchange_sems(n_x),
        compiler_params=_params(("arbitrary",)),
    )(dy, dy, a, a, dw_w, *exchange)
    return outs[0], outs[1], outs[2], outs[3:]


GATE_TB = 512


def _gate_fwd(fl, b_f, name):
    T = fl.shape[0]
    tb = min(GATE_TB, T)
    row = pl.BlockSpec((tb, LANES), lambda i: (i, 0))

    def body(fl_ref, b_ref, f_ref, f1_ref, f2_ref, f3_ref, carry):
        @pl.when(pl.program_id(0) == 0)
        def _():
            carry[...] = jnp.zeros_like(carry)

        z = fl_ref[...] + b_ref[...]
        lf = jnp.minimum(z, 0.0) - jnp.log(1.0 + jnp.exp(-jnp.abs(z)))
        tri = (lax.broadcasted_iota(jnp.int32, (tb, tb), 0) >= lax.broadcasted_iota(jnp.int32, (tb, tb), 1)).astype(F32)
        f = jnp.dot(tri, lf, precision=lax.Precision.HIGHEST, preferred_element_type=F32) + carry[...]
        carry[...] = f[tb - 1:tb, :]
        f_ref[...] = f
        f1_ref[...], f2_ref[...], f3_ref[...] = _split3(f)

    return pl.pallas_call(
        body, name=name, grid=(T // tb,), in_specs=[row, pl.BlockSpec((1, LANES), lambda i: (0, 0))],
        out_specs=[row] * 4, out_shape=[jax.ShapeDtypeStruct((T, LANES), F32)] * 4,
        scratch_shapes=[pltpu.VMEM((1, LANES), F32)],
        compiler_params=_params(("arbitrary",)),
    )(fl, b_f)


def _gate_bwd(dF_q, dF_k, fl, b_f, name):
    T = fl.shape[0]
    tb = min(GATE_TB, T)
    nb = T // tb
    row = pl.BlockSpec((tb, LANES), lambda i: (nb - 1 - i, 0))
    vec = pl.BlockSpec((1, LANES), lambda i: (0, 0))

    def body(dfq_ref, dfk_ref, fl_ref, b_ref, dfl_ref, db_ref, carry):
        @pl.when(pl.program_id(0) == 0)
        def _():
            carry[...] = jnp.zeros_like(carry)
            db_ref[...] = jnp.zeros_like(db_ref)

        tri = (lax.broadcasted_iota(jnp.int32, (tb, tb), 0) <= lax.broadcasted_iota(jnp.int32, (tb, tb), 1)).astype(F32)
        rc = jnp.dot(tri, dfq_ref[...] - dfk_ref[...], precision=lax.Precision.HIGHEST,
                     preferred_element_type=F32) + carry[...]
        carry[...] = rc[0:1, :]
        dfl = rc * jax.nn.sigmoid(-(fl_ref[...] + b_ref[...]))
        dfl_ref[...] = dfl
        db_ref[...] += jnp.sum(dfl, axis=0, keepdims=True)

    return pl.pallas_call(
        body, name=name, grid=(nb,), in_specs=[row, row, row, vec], out_specs=[row, vec],
        out_shape=[jax.ShapeDtypeStruct((T, LANES), F32), jax.ShapeDtypeStruct((1, LANES), F32)],
        scratch_shapes=[pltpu.VMEM((1, LANES), F32)],
        compiler_params=_params(("arbitrary",)),
    )(dF_q, dF_k, fl, b_f)


ATT_TB = 512
LAYOUT_TB = 2048
NT = (((1,), (1,)), ((), ()))
NN = (((1,), (0,)), ((), ()))


def _lane_insert(base, parts, first_lane):
    lane = lax.broadcasted_iota(jnp.int32, base.shape, 1)
    out = base
    for n, p in enumerate(parts):
        out = jnp.where(lane == first_lane + n, p, out)
    return out


def _causal(tb):
    return lax.broadcasted_iota(jnp.int32, (tb, tb), 0) >= lax.broadcasted_iota(jnp.int32, (tb, tb), 1)


def _diag_mask(first, n_rows, n_cols, transposed=False):
    r = lax.broadcasted_iota(jnp.int32, (n_rows, n_cols), 0) + first
    c = lax.broadcasted_iota(jnp.int32, (n_rows, n_cols), 1)
    return c >= r if transposed else c <= r


def _other_head(x):
    return pltpu.roll(x, HEAD_DIM, 1)


def _attn_pack(proj, f1, f2, f3, name, gather_units, gather_shards):
    T = proj.shape[0]
    tb = min(LAYOUT_TB, T)
    pairs = N_HEADS // 2
    n_steps = T // tb
    n_g = len(gather_units)
    cols = lambda first: pl.BlockSpec((tb, LANES), lambda p, i: (i, first + p))
    gate = pl.BlockSpec((tb, LANES), lambda p, i: (i, 0))
    heads = pl.BlockSpec((2, tb, LANES), lambda p, i: (p, i, 0))
    norm = pl.BlockSpec((2, 8, LANES), lambda p, i: (p, 0, 0))

    def body(*refs):
        q_ref, k_ref, v_ref, f1_ref, f2_ref, f3_ref = refs[:6]
        g_ins = refs[6:6 + n_g]
        qa_ref, ka_ref, va_ref, qn_ref, kn_ref = refs[6 + n_g:11 + n_g]
        g_outs = refs[11 + n_g:11 + 2 * n_g]
        sems = refs[11 + 2 * n_g:]
        p, i = pl.program_id(0), pl.program_id(1)
        pl.when(jnp.logical_and(p == 0, i == 0))(lambda: _gather_start(gather_units, g_ins, g_outs, sems))
        lane = lax.broadcasted_iota(jnp.int32, (tb, LANES), 1)
        data = lane < HEAD_DIM

        @pl.when(i == 0)
        def _():
            qn_ref[...] = jnp.zeros_like(qn_ref)
            kn_ref[...] = jnp.zeros_like(kn_ref)

        qv, kv, vv = q_ref[...] * 0.125, k_ref[...], v_ref[...]
        for e in range(2):
            fcol = [jnp.sum(jnp.where(lane == 2 * p + e, f[...], 0.0), axis=-1, keepdims=True)
                    for f in (f1_ref, f2_ref, f3_ref)]
            pick = (lambda t: t) if e == 0 else _other_head
            qd = jnp.where(data, pick(qv), 0.0).astype(BF16).astype(F32)
            kd = jnp.where(data, pick(kv), 0.0).astype(BF16).astype(F32)
            vd = jnp.where(data, pick(vv), 0.0)
            qa_ref[e] = _lane_insert(qd, fcol + [1.0, 1.0, 1.0], AUG_F).astype(BF16)
            ka_ref[e] = _lane_insert(kd, [1.0, 1.0, 1.0] + [-f for f in fcol] + [1.0, 1.0, 1.0], AUG_F).astype(BF16)
            va_ref[e] = _lane_insert(vd, [-1.0, -1.0, -1.0, 1.0], AUG_F).astype(BF16)
            qn_ref[e] = jnp.maximum(qn_ref[e], jnp.max(jnp.sum(qd * qd, axis=-1, keepdims=True)))
            kn_ref[e] = jnp.maximum(kn_ref[e], jnp.max(jnp.sum(kd * kd, axis=-1, keepdims=True)))
        pl.when(jnp.logical_and(p == pairs - 1, i == n_steps - 1))(
            lambda: _gather_finish(gather_units, g_ins, g_outs, sems))

    aug = jax.ShapeDtypeStruct((N_HEADS, T, LANES), BF16)
    nrm = jax.ShapeDtypeStruct((N_HEADS, 8, LANES), F32)
    outs = pl.pallas_call(
        body, name=name, grid=(pairs, n_steps),
        in_specs=[cols(0), cols(pairs), cols(2 * pairs), gate, gate, gate] + [ANY] * n_g,
        out_specs=[heads, heads, heads, norm, norm] + [ANY] * n_g,
        out_shape=[aug, aug, aug, nrm, nrm]
        + [jax.ShapeDtypeStruct((N_CHIPS,) + shape, BF16) for _, shape in gather_units],
        scratch_shapes=_gather_sems(n_g),
        compiler_params=_params(("arbitrary", "arbitrary")),
    )(proj, proj, proj, f1, f2, f3, *gather_shards)
    return outs[:5], outs[5:]


def _merge_heads(x_aug, scale, out_dtype, name, column=None):
    H, T, _ = x_aug.shape
    tb = min(LAYOUT_TB, T)

    def body(x_ref, o_ref, *col_ref):
        p = pl.program_id(1)
        lane = lax.broadcasted_iota(jnp.int32, (tb, LANES), 1)
        x0, x1 = x_ref[0], x_ref[1]
        o_ref[...] = (jnp.where(lane < HEAD_DIM, x0, _other_head(x1)) * scale).astype(out_dtype)
        if column is not None:
            @pl.when(p == 0)
            def _():
                col_ref[0][...] = jnp.zeros_like(col_ref[0])

            c0 = jnp.sum(jnp.where(lane == column, x0, 0.0), axis=-1, keepdims=True)
            c1 = jnp.sum(jnp.where(lane == column, x1, 0.0), axis=-1, keepdims=True)
            col_ref[0][...] += jnp.where(lane == 2 * p, c0, 0.0) + jnp.where(lane == 2 * p + 1, c1, 0.0)

    out_specs = [pl.BlockSpec((tb, LANES), lambda i, p: (i, p))]
    out_shape = [jax.ShapeDtypeStruct((T, D), out_dtype)]
    if column is not None:
        out_specs.append(pl.BlockSpec((tb, LANES), lambda i, p: (i, 0)))
        out_shape.append(jax.ShapeDtypeStruct((T, LANES), F32))
    return pl.pallas_call(
        body, name=name, grid=(T // tb, H // 2),
        in_specs=[pl.BlockSpec((2, tb, LANES), lambda i, p: (p, i, 0))],
        out_specs=out_specs, out_shape=out_shape,
        compiler_params=_params(("parallel", "arbitrary")),
    )(x_aug)


def _attn_prep(d_o, o_aug, name):
    H, T, _ = o_aug.shape
    tb = min(LAYOUT_TB, T)
    heads = pl.BlockSpec((2, tb, LANES), lambda p, i: (p, i, 0))

    def body(do_ref, o_ref, out_ref):
        lane = lax.broadcasted_iota(jnp.int32, (tb, LANES), 1)
        dov = do_ref[...]
        for e in range(2):
            d_e = jnp.where(lane < HEAD_DIM, dov if e == 0 else _other_head(dov), 0.0)
            delta = jnp.sum(d_e * o_ref[e], axis=-1, keepdims=True)
            out_ref[e] = _lane_insert(d_e, _split3(delta), AUG_F).astype(BF16)

    return pl.pallas_call(
        body, name=name, grid=(H // 2, T // tb),
        in_specs=[pl.BlockSpec((tb, LANES), lambda p, i: (i, p)), heads], out_specs=heads,
        out_shape=jax.ShapeDtypeStruct((H, T, LANES), BF16),
        compiler_params=_params(("parallel", "parallel")),
    )(d_o, o_aug)


def _flash_spec(n_in, n_out, H, nb, tb, T, resident, scratch, n_tables=3):
    blk = pl.BlockSpec((2, tb, LANES), lambda h, i, *_: (h, i, 0))
    whole = pl.BlockSpec((2, T, LANES), lambda h, i, *_: (h, 0, 0))
    return pltpu.PrefetchScalarGridSpec(
        num_scalar_prefetch=n_tables, grid=(H // 2, nb),
        in_specs=[whole if resident[n] else blk for n in range(n_in)],
        out_specs=[blk] * n_out, scratch_shapes=scratch)


def _first_live(qk_ref, fs_ref, fe_ref, h, i):
    top = qk_ref[h] + fs_ref[h, i]
    return lax.fori_loop(0, i, lambda j, n: n + jnp.where(top - fe_ref[h, j] < SKIP_BELOW, 1, 0), 0)


def _flash_fwd(qk, fs, fe, kmax, q_aug, k_aug, v_aug, name):
    H, T, _ = q_aug.shape
    tb = min(ATT_TB, T)
    half = tb // 2

    def body(qk_ref, fs_ref, fe_ref, km_ref, q_ref, k_ref, v_ref, o_ref, q2_ref, m_s, acc_s, qm_s):
        h, i = 2 * pl.program_id(0), pl.program_id(1)
        acc_s[...] = jnp.zeros_like(acc_s)
        first = jnp.minimum(_first_live(qk_ref, fs_ref, fe_ref, h, i), _first_live(qk_ref, fs_ref, fe_ref, h + 1, i))
        bounded = jnp.logical_and(qk_ref[h] < FIXED_MAX_BELOW, qk_ref[h + 1] < FIXED_MAX_BELOW)

        def sweep(step):
            def loop_body(kb, carry):
                step(kb, False)
                return carry

            lax.fori_loop(first, i, loop_body, 0)
            step(i, True)

        @pl.when(bounded)
        def _():
            lane = lax.broadcasted_iota(jnp.int32, (tb, LANES), 1)
            for e in range(2):
                qf = q_ref[e].astype(F32)
                norm = jnp.sqrt(jnp.sum(jnp.where(lane < HEAD_DIM, qf * qf, 0.0), axis=-1, keepdims=True))
                bound = norm * (1.01 * km_ref[h + e]) + 1e-3
                m_s[e] = bound
                qm_s[e] = _lane_insert(qf, [-p for p in _split3(bound)], AUG_L).astype(BF16)

            def step(kb, masked):
                base = pl.multiple_of(kb * tb, tb)
                parts = [(slice(0, half), half), (slice(half, tb), tb)] if masked else [(slice(0, tb), tb)]
                for e in range(2):
                    for qrows, nk in parts:
                        keys = pl.ds(base, nk)
                        p = jnp.exp(lax.dot_general(qm_s[e, qrows, :], k_ref[e, keys, :], NT,
                                                    preferred_element_type=F32))
                        if masked:
                            p = jnp.where(_diag_mask(qrows.start, qrows.stop - qrows.start, nk), p, 0.0)
                        acc_s[e, qrows, :] += lax.dot_general(p.astype(BF16), v_ref[e, keys, :], NN,
                                                              preferred_element_type=F32)

            sweep(step)

        @pl.when(jnp.logical_not(bounded))
        def _():
            m_s[...] = jnp.full(m_s.shape, -jnp.inf, F32)

            def step(kb, masked):
                rows = pl.ds(pl.multiple_of(kb * tb, tb), tb)
                for e in range(2):
                    s = lax.dot_general(q_ref[e], k_ref[e, rows, :], NT, preferred_element_type=F32)
                    if masked:
                        s = jnp.where(_causal(tb), s, MASK_VALUE)
                    m_old = m_s[e]
                    m_new = jnp.maximum(m_old, jnp.max(s, axis=-1, keepdims=True))
                    p = jnp.exp(s - m_new)
                    acc_s[e] = jnp.exp(m_old - m_new) * acc_s[e] + lax.dot_general(
                        p.astype(BF16), v_ref[e, rows, :], NN, preferred_element_type=F32)
                    m_s[e] = m_new

            sweep(step)

        for e in range(2):
            acc = acc_s[e]
            lane = lax.broadcasted_iota(jnp.int32, acc.shape, 1)
            l = jnp.sum(jnp.where(lane == AUG_ONE, acc, 0.0), axis=-1, keepdims=True)
            o_ref[e] = acc / l
            lse = m_s[e] + jnp.log(l)
            q2_ref[e] = _lane_insert(q_ref[e].astype(F32), [-p for p in _split3(lse)], AUG_L).astype(BF16)

    return pl.pallas_call(
        body, name=name,
        grid_spec=_flash_spec(3, 2, H, T // tb, tb, T, (False, True, True),
                              [pltpu.VMEM((2, tb, 1), F32), pltpu.VMEM((2, tb, LANES), F32),
                               pltpu.VMEM((2, tb, LANES), BF16)], n_tables=4),
        out_shape=[jax.ShapeDtypeStruct((H, T, LANES), F32), jax.ShapeDtypeStruct((H, T, LANES), BF16)],
        compiler_params=_params(("parallel", "arbitrary")),
    )(qk, fs, fe, kmax, q_aug, k_aug, v_aug)


def _flash_dq(qk, fs, fe, q2, k_aug, do_aug, v_aug, name):
    H, T, _ = q2.shape
    tb = min(ATT_TB, T)
    half = tb // 2

    def body(qk_ref, fs_ref, fe_ref, q_ref, k_ref, do_ref, v_ref, dq_ref, acc_s):
        h, i = 2 * pl.program_id(0), pl.program_id(1)
        acc_s[...] = jnp.zeros_like(acc_s)

        def step(kb, masked):
            base = pl.multiple_of(kb * tb, tb)
            parts = [(slice(0, half), half), (slice(half, tb), tb)] if masked else [(slice(0, tb), tb)]
            for e in range(2):
                for qrows, nk in parts:
                    keys = pl.ds(base, nk)
                    k = k_ref[e, keys, :]
                    p = jnp.exp(lax.dot_general(q_ref[e, qrows, :], k, NT, preferred_element_type=F32))
                    if masked:
                        p = jnp.where(_diag_mask(qrows.start, qrows.stop - qrows.start, nk), p, 0.0)
                    ds = p * lax.dot_general(do_ref[e, qrows, :], v_ref[e, keys, :], NT, preferred_element_type=F32)
                    acc_s[e, qrows, :] += lax.dot_general(ds.astype(BF16), k, NN, preferred_element_type=F32)

        def loop_body(kb, carry):
            step(kb, False)
            return carry

        first = jnp.minimum(_first_live(qk_ref, fs_ref, fe_ref, h, i), _first_live(qk_ref, fs_ref, fe_ref, h + 1, i))
        lax.fori_loop(first, i, loop_body, 0)
        step(i, True)
        dq_ref[...] = acc_s[...]

    return pl.pallas_call(
        body, name=name,
        grid_spec=_flash_spec(4, 1, H, T // tb, tb, T, (False, True, False, True), [pltpu.VMEM((2, tb, LANES), F32)]),
        out_shape=[jax.ShapeDtypeStruct((H, T, LANES), F32)],
        compiler_params=_params(("parallel", "arbitrary")),
    )(qk, fs, fe, q2, k_aug, do_aug, v_aug)[0]


def _flash_dkv(qk, fs, fe, q2, k_aug, do_aug, v_aug, name):
    H, T, _ = q2.shape
    tb = min(ATT_TB, T)
    half = tb // 2
    nb = T // tb

    def body(qk_ref, fs_ref, fe_ref, q_ref, k_ref, do_ref, v_ref, dk_ref, dv_ref, dk_s, dv_s):
        h, i = 2 * pl.program_id(0), pl.program_id(1)
        dk_s[...] = jnp.zeros_like(dk_s)
        dv_s[...] = jnp.zeros_like(dv_s)

        def step(qb, masked):
            base = pl.multiple_of(qb * tb, tb)
            parts = [(slice(0, half), 0, tb), (slice(half, tb), half, half)] if masked else [(slice(0, tb), 0, tb)]
            for e in range(2):
                for krows, q0, nq in parts:
                    queries = pl.ds(pl.multiple_of(base + q0, half), nq)
                    q, dov = q_ref[e, queries, :], do_ref[e, queries, :]
                    pt = jnp.exp(lax.dot_general(k_ref[e, krows, :], q, NT, preferred_element_type=F32))
                    if masked:
                        pt = jnp.where(_diag_mask(0, krows.stop - krows.start, nq, transposed=True), pt, 0.0)
                    dst = pt * lax.dot_general(v_ref[e, krows, :], dov, NT, preferred_element_type=F32)
                    dv_s[e, krows, :] += lax.dot_general(pt.astype(BF16), dov, NN, preferred_element_type=F32)
                    dk_s[e, krows, :] += lax.dot_general(dst.astype(BF16), q, NN, preferred_element_type=F32)

        step(i, True)

        def loop_body(qb, carry):
            step(qb, False)
            return carry

        def live_after(head):
            base = qk_ref[head] - fe_ref[head, i]
            return lax.fori_loop(i + 1, nb, lambda b, n: n + jnp.where(base + fs_ref[head, b] < SKIP_BELOW, 0, 1), 0)

        lax.fori_loop(i + 1, i + 1 + jnp.maximum(live_after(h), live_after(h + 1)), loop_body, 0)
        dk_ref[...] = dk_s[...]
        dv_ref[...] = dv_s[...]

    return pl.pallas_call(
        body, name=name,
        grid_spec=_flash_spec(4, 2, H, nb, tb, T, (True, False, True, False),
                              [pltpu.VMEM((2, tb, LANES), F32), pltpu.VMEM((2, tb, LANES), F32)]),
        out_shape=[jax.ShapeDtypeStruct((H, T, LANES), F32)] * 2,
        compiler_params=_params(("parallel", "arbitrary")),
    )(qk, fs, fe, q2, k_aug, do_aug, v_aug)


def _add_pair(place, slab, got, name):
    _, half, cols = got.shape
    grid_spec = pltpu.PrefetchScalarGridSpec(
        num_scalar_prefetch=1, grid=(N_CHIPS,),
        in_specs=[pl.BlockSpec((None, half, cols), lambda j, s: (j, s[4], 0)),
                  pl.BlockSpec((None, half, cols), lambda j, s: (j, 0, 0))],
        out_specs=pl.BlockSpec((None, half, cols), lambda j, s: (j, 0, 0)))

    def body(s_ref, a_ref, b_ref, o_ref):
        o_ref[...] = (a_ref[...] + b_ref[...]).astype(BF16)

    return pl.pallas_call(
        body, name=name, grid_spec=grid_spec, out_shape=jax.ShapeDtypeStruct(got.shape, BF16),
        compiler_params=_params(("parallel",)),
    )(place, slab, got)


def _add_chips(place, pair, by_chip, name):
    _, half, cols = pair.shape
    tb = min(256, half)
    steps = half // tb
    slot = lambda n: pl.BlockSpec((None, tb, cols), lambda i, s: (s[n], i, 0))
    grid_spec = pltpu.PrefetchScalarGridSpec(
        num_scalar_prefetch=1, grid=(steps,), in_specs=[slot(0), slot(1), slot(2), slot(3)],
        out_specs=pl.BlockSpec((tb, cols), lambda i, s: (s[4] * steps + i, 0)))

    def body(s_ref, own_ref, b1_ref, b2_ref, b3_ref, o_ref):
        o_ref[...] = ((own_ref[...].astype(F32) + b1_ref[...].astype(F32)) + b2_ref[...].astype(F32)) \
            + b3_ref[...].astype(F32)

    return pl.pallas_call(
        body, name=name, grid_spec=grid_spec, out_shape=jax.ShapeDtypeStruct((2 * half, cols), F32),
        compiler_params=_params(("parallel",)),
    )(place, pair, by_chip, by_chip, by_chip)


def _add_slots(b, name, tb=128):
    n, rows, cols = b.shape
    tb = tb if rows % tb == 0 else rows
    specs = [pl.BlockSpec((None, tb, cols), functools.partial(lambda j, i: (j, i, 0), j)) for j in range(n)]

    def body(*refs):
        acc = refs[0][...]
        for r in refs[1:n]:
            acc = acc + r[...]
        refs[n][...] = acc

    return pl.pallas_call(
        body, name=name, grid=(rows // tb,), in_specs=specs, out_specs=pl.BlockSpec((tb, cols), lambda i: (i, 0)),
        out_shape=jax.ShapeDtypeStruct((rows, cols), F32), compiler_params=_params(("parallel",)),
    )(*([b] * n))


def _adamw(w, g, m, v, name):
    rows, cols = w.shape
    tr = 256 if rows % 256 == 0 else rows
    blk = pl.BlockSpec((tr, cols), lambda i: (i, 0))

    def body(w_ref, g_ref, m_ref, v_ref, d_ref, nm_ref, nv_ref):
        gv = g_ref[...]
        nm = ADAM_B1 * m_ref[...] + (1.0 - ADAM_B1) * gv
        nv = ADAM_B2 * v_ref[...] + (1.0 - ADAM_B2) * (gv * gv)
        m_hat = nm / (1.0 - ADAM_B1 ** ADAM_STEP)
        v_hat = nv / (1.0 - ADAM_B2 ** ADAM_STEP)
        d_ref[...] = -ADAM_LR * (m_hat / (jnp.sqrt(v_hat) + ADAM_EPS) + ADAM_WD * w_ref[...])
        nm_ref[...] = nm
        nv_ref[...] = nv

    return pl.pallas_call(
        body, name=name, grid=(rows // tr,), in_specs=[blk] * 4, out_specs=[blk] * 3,
        out_shape=[jax.ShapeDtypeStruct((rows, cols), F32)] * 3, compiler_params=_params(("parallel",)),
    )(w, g, m, v)


ANY = pl.BlockSpec(memory_space=pl.ANY)


def _place():
    x, y, c = lax.axis_index("x"), lax.axis_index("y"), lax.axis_index("c")
    others = [(1 - x, y), (x, 1 - y), (1 - x, 1 - y)]
    return x, y, c, 2 * x + y, others


def _half(ref, rows, h):
    return ref.at[pl.ds(h * (rows // 2), rows // 2), :]


def _gather_sems(n_u):
    return [pltpu.SemaphoreType.DMA((6 * n_u,)), pltpu.SemaphoreType.DMA((6 * n_u,)),
            pltpu.SemaphoreType.DMA((n_u,)), pltpu.SemaphoreType.DMA((n_u,))]


def _gather_copies(units, ins, outs, sems):
    send_sems, recv_sems, own_send_sems, own_recv_sems = sems
    x, y, c, me, others = _place()
    sibling = (x, y, 1 - c)

    def copy(u, k, chip, h, to, src=None):
        dst = _half(outs[u].at[chip], units[u][1][0], h)
        return pltpu.make_async_remote_copy(
            src_ref=dst if src is None else src, dst_ref=dst, send_sem=send_sems.at[6 * u + k],
            recv_sem=recv_sems.at[6 * u + k], device_id=to, device_id_type=MESH)

    pairs = [(u, j, 2 * ox + oy, (ox, oy)) for u in range(len(units)) for j, (ox, oy) in enumerate(others)]
    own = [pltpu.make_async_remote_copy(
        src_ref=ins[u], dst_ref=outs[u].at[me], send_sem=own_send_sems.at[u], recv_sem=own_recv_sems.at[u],
        device_id=sibling, device_id_type=MESH) for u in range(len(units))]
    first = [copy(u, j, me, c, (*to, c), src=_half(ins[u], units[u][1][0], c)) for u, j, _, to in pairs]
    arrivals = [copy(u, j, chip, c, (x, y, c)) for u, j, chip, _ in pairs]
    passed = [copy(u, 3 + j, chip, c, sibling) for u, j, chip, _ in pairs]
    from_sibling = [copy(u, 3 + j, chip, 1 - c, (x, y, c)) for u, j, chip, _ in pairs]
    return own, first, arrivals, passed, from_sibling


def _gather_start(units, ins, outs, sems):
    own, first, _, _, _ = _gather_copies(units, ins, outs, sems)
    for cp in own + first:
        cp.start()


def _gather_finish(units, ins, outs, sems):
    own, first, arrivals, passed, from_sibling = _gather_copies(units, ins, outs, sems)
    for arrived, onward in zip(arrivals, passed):
        arrived.wait_recv()
        onward.start()
    for cp in from_sibling:
        cp.wait_recv()
    for cp in first + passed:
        cp.wait_send()
    for cp in own:
        cp.wait()


def _allgather_weights(units, shards, name):
    n_u = len(units)

    def body(*refs):
        ins, outs, sems = refs[:n_u], refs[n_u:2 * n_u], refs[2 * n_u:]
        _gather_start(units, ins, outs, sems)
        _gather_finish(units, ins, outs, sems)

    return pl.pallas_call(
        body, name=name, in_specs=[ANY] * n_u, out_specs=[ANY] * n_u,
        out_shape=[jax.ShapeDtypeStruct((N_CHIPS,) + shape, BF16) for _, shape in units],
        scratch_shapes=_gather_sems(n_u),
    )(*shards)


def _sibling_copies(units, ins, got, sems):
    send_sems, recv_sems = sems
    x, y, c, _, _ = _place()
    copies = []
    for u, (_, (rows, _), _) in enumerate(units):
        half = rows // 2
        copies.append(pltpu.make_async_remote_copy(
            src_ref=ins[u].at[:, pl.ds((1 - c) * half, half), :], dst_ref=got[u], send_sem=send_sems.at[u],
            recv_sem=recv_sems.at[u], device_id=(x, y, 1 - c), device_id_type=MESH))
    return copies


def _sibling_start(units, ins, got, sems):
    for cp in _sibling_copies(units, ins, got, sems):
        cp.start()


def _sibling_finish(units, ins, got, sems):
    for cp in _sibling_copies(units, ins, got, sems):
        cp.wait()


def _sibling_exchange(units, slabs, name):
    n_u = len(units)

    def body(*refs):
        ins, got, sems = refs[:n_u], refs[n_u:2 * n_u], refs[2 * n_u:]
        _sibling_start(units, ins, got, sems)
        _sibling_finish(units, ins, got, sems)

    return pl.pallas_call(
        body, name=name, in_specs=[ANY] * n_u, out_specs=[ANY] * n_u,
        out_shape=[jax.ShapeDtypeStruct((N_CHIPS, rows // 2, cols), F32) for _, (rows, cols), _ in units],
        scratch_shapes=[pltpu.SemaphoreType.DMA((n_u,)), pltpu.SemaphoreType.DMA((n_u,))],
    )(*slabs)


def _exchange_sems(n_u):
    return [pltpu.SemaphoreType.DMA((3 * n_u,)), pltpu.SemaphoreType.DMA((3 * n_u,))]


def _exchange_copies(ins, outs, sems):
    send_sems, recv_sems = sems
    x, y, c, me, others = _place()
    sends = [pltpu.make_async_remote_copy(
        src_ref=ins[u].at[2 * ox + oy], dst_ref=outs[u].at[me], send_sem=send_sems.at[3 * u + j],
        recv_sem=recv_sems.at[3 * u + j], device_id=(ox, oy, c), device_id_type=MESH)
        for u in range(len(ins)) for j, (ox, oy) in enumerate(others)]
    arrivals = [pltpu.make_async_remote_copy(
        src_ref=ins[u].at[me], dst_ref=outs[u].at[2 * ox + oy], send_sem=send_sems.at[3 * u + j],
        recv_sem=recv_sems.at[3 * u + j], device_id=(x, y, c), device_id_type=MESH)
        for u in range(len(ins)) for j, (ox, oy) in enumerate(others)]
    return sends, arrivals


def _exchange_start(ins, outs, sems):
    for cp in _exchange_copies(ins, outs, sems)[0]:
        cp.start()


def _exchange_finish(ins, outs, sems):
    sends, arrivals = _exchange_copies(ins, outs, sems)
    for cp in arrivals:
        cp.wait_recv()
    for cp in sends:
        cp.wait_send()


def _chip_exchange(pairs, small):
    n_u = len(pairs)

    def body(*refs):
        ins, s_ref = refs[:n_u], refs[n_u]
        outs, sall_ref = refs[n_u + 1:2 * n_u + 1], refs[2 * n_u + 1]
        send_sems, recv_sems, ssend_sems, srecv_sems, local_sem = refs[2 * n_u + 2:]
        x, y, c, _, _ = _place()
        dev = 4 * x + 2 * y + c
        local = pltpu.make_async_copy(s_ref, sall_ref.at[dev], local_sem)
        flips = [(fx, fy, fc) for fx in (0, 1) for fy in (0, 1) for fc in (0, 1)][1:]
        small_sends = [pltpu.make_async_remote_copy(
            src_ref=s_ref, dst_ref=sall_ref.at[dev], send_sem=ssend_sems.at[n], recv_sem=srecv_sems.at[n],
            device_id=(x ^ fx, y ^ fy, c ^ fc), device_id_type=MESH) for n, (fx, fy, fc) in enumerate(flips)]
        local.start()
        _exchange_start(ins, outs, (send_sems, recv_sems))
        for cp in small_sends:
            cp.start()
        _exchange_finish(ins, outs, (send_sems, recv_sems))
        for n, (fx, fy, fc) in enumerate(flips):
            src_dev = 4 * (x ^ fx) + 2 * (y ^ fy) + (c ^ fc)
            pltpu.make_async_remote_copy(
                src_ref=s_ref, dst_ref=sall_ref.at[src_dev], send_sem=ssend_sems.at[n], recv_sem=srecv_sems.at[n],
                device_id=(x, y, c), device_id_type=MESH).wait_recv()
        for cp in small_sends:
            cp.wait_send()
        local.wait()

    outs = pl.pallas_call(
        body, name="grad_chip_exchange", in_specs=[ANY] * (n_u + 1), out_specs=[ANY] * (n_u + 1),
        out_shape=[jax.ShapeDtypeStruct(p.shape, p.dtype) for p in pairs]
        + [jax.ShapeDtypeStruct((8, SMALL_ALL_ROWS, LANES), F32)],
        scratch_shapes=_exchange_sems(n_u) + [pltpu.SemaphoreType.DMA((7,)), pltpu.SemaphoreType.DMA((7,)),
                                              pltpu.SemaphoreType.DMA],
    )(*pairs, small)
    return outs[:n_u], outs[n_u]


def _sibling_share(grads):
    n_u = len(GRAD_UNITS)

    def body(*refs):
        ins, outs = refs[:n_u], refs[n_u:2 * n_u]
        send_sems, recv_sems = refs[2 * n_u:]
        x, y, c, _, _ = _place()

        def rows_of(ref, u, h):
            half = GRAD_UNITS[u][1][0] // 2
            return ref.at[pl.ds(h * half, half), :]

        sends = [pltpu.make_async_remote_copy(
            src_ref=rows_of(ins[u], u, c), dst_ref=rows_of(outs[u], u, c), send_sem=send_sems.at[u],
            recv_sem=recv_sems.at[u], device_id=(x, y, 1 - c), device_id_type=MESH) for u in range(n_u)]
        for cp in sends:
            cp.start()
        for u in range(n_u):
            pltpu.make_async_remote_copy(
                src_ref=rows_of(ins[u], u, c), dst_ref=rows_of(outs[u], u, 1 - c), send_sem=send_sems.at[u],
                recv_sem=recv_sems.at[u], device_id=(x, y, c), device_id_type=MESH).wait_recv()
        for cp in sends:
            cp.wait_send()

    return pl.pallas_call(
        body, name="grad_sibling_share", in_specs=[ANY] * n_u, out_specs=[ANY] * n_u,
        out_shape=[jax.ShapeDtypeStruct(g.shape, F32) for g in grads],
        input_output_aliases={u: u for u in range(n_u)},
        scratch_shapes=[pltpu.SemaphoreType.DMA((n_u,)), pltpu.SemaphoreType.DMA((n_u,))],
    )(*grads)


def kernel(x, g_mix_pre, g_mix_post, g_ffn_pre, g_ffn_post, conv_pw1_w, conv_pw1_b, conv_dw_w, conv_dw_b, conv_ln_g, conv_ln_b, conv_pw2_w, conv_pw2_b, attn_w_in, attn_b_f, attn_w_o, mlp_w_up, mlp_w_down, loss_target, m_g_mix_pre, m_g_mix_post, m_g_ffn_pre, m_g_ffn_post, m_conv_pw1_w, m_conv_pw1_b, m_conv_dw_w, m_conv_dw_b, m_conv_ln_g, m_conv_ln_b, m_conv_pw2_w, m_conv_pw2_b, m_attn_w_in, m_attn_b_f, m_attn_w_o, m_mlp_w_up, m_mlp_w_down, v_g_mix_pre, v_g_mix_post, v_g_ffn_pre, v_g_ffn_post, v_conv_pw1_w, v_conv_pw1_b, v_conv_dw_w, v_conv_dw_b, v_conv_ln_g, v_conv_ln_b, v_conv_pw2_w, v_conv_pw2_b, v_attn_w_in, v_attn_b_f, v_attn_w_o, v_mlp_w_up, v_mlp_w_down):
    weights = dict(g_mix_pre=g_mix_pre, g_mix_post=g_mix_post, g_ffn_pre=g_ffn_pre, g_ffn_post=g_ffn_post, conv_pw1_w=conv_pw1_w, conv_pw1_b=conv_pw1_b, conv_dw_w=conv_dw_w, conv_dw_b=conv_dw_b, conv_ln_g=conv_ln_g, conv_ln_b=conv_ln_b, conv_pw2_w=conv_pw2_w, conv_pw2_b=conv_pw2_b, attn_w_in=attn_w_in, attn_b_f=attn_b_f, attn_w_o=attn_w_o, mlp_w_up=mlp_w_up, mlp_w_down=mlp_w_down)
    mom_m = dict(g_mix_pre=m_g_mix_pre, g_mix_post=m_g_mix_post, g_ffn_pre=m_g_ffn_pre, g_ffn_post=m_g_ffn_post, conv_pw1_w=m_conv_pw1_w, conv_pw1_b=m_conv_pw1_b, conv_dw_w=m_conv_dw_w, conv_dw_b=m_conv_dw_b, conv_ln_g=m_conv_ln_g, conv_ln_b=m_conv_ln_b, conv_pw2_w=m_conv_pw2_w, conv_pw2_b=m_conv_pw2_b, attn_w_in=m_attn_w_in, attn_b_f=m_attn_b_f, attn_w_o=m_attn_w_o, mlp_w_up=m_mlp_w_up, mlp_w_down=m_mlp_w_down)
    mom_v = dict(g_mix_pre=v_g_mix_pre, g_mix_post=v_g_mix_post, g_ffn_pre=v_g_ffn_pre, g_ffn_post=v_g_ffn_post, conv_pw1_w=v_conv_pw1_w, conv_pw1_b=v_conv_pw1_b, conv_dw_w=v_conv_dw_w, conv_dw_b=v_conv_dw_b, conv_ln_g=v_conv_ln_g, conv_ln_b=v_conv_ln_b, conv_pw2_w=v_conv_pw2_w, conv_pw2_b=v_conv_pw2_b, attn_w_in=v_attn_w_in, attn_b_f=v_attn_b_f, attn_w_o=v_attn_w_o, mlp_w_up=v_mlp_w_up, mlp_w_down=v_mlp_w_down)
    order = [n for n, _ in SMALL[:4]] + ["conv_pw1_w", "conv_pw1_b", "conv_dw_w", "conv_dw_b", "conv_ln_g", "conv_ln_b",
                                          "conv_pw2_w", "conv_pw2_b", "attn_w_in", "attn_b_f", "attn_w_o", "mlp_w_up",
                                          "mlp_w_down"]
    T = x.shape[1]
    x0 = x.reshape(T, D)
    target = loss_target.reshape(T, D)
    row = lambda a, l: a[l:l + 1, :]

    def shard(unit):
        if unit[:-1] in ("up", "down"):
            return weights["mlp_w_" + unit[:-1]][int(unit[-1])].astype(BF16)
        w = weights[unit][0].astype(BF16)
        return jnp.pad(w, ((0, HALO - CONV_W), (0, 0))) if unit == "conv_dw_w" else w

    by_cols = lambda g: g.transpose(1, 0, 2).reshape(g.shape[1], N_CHIPS * g.shape[2])
    by_rows = lambda g: g.reshape(N_CHIPS * g.shape[1], g.shape[2])
    g_pw1, g_dw, g_pw2 = _allgather_weights(GATHER_AT_START, [shard(n) for n, _ in GATHER_AT_START], "allgather_conv")
    w_pw1, w_pw2, w_dw = by_cols(g_pw1), by_rows(g_pw2), by_cols(g_dw).astype(F32)
    b_f = jnp.pad(attn_b_f, ((0, 0), (0, LANES - N_HEADS)))

    def mlp_fwd(h, l):
        (act,) = _matmul(h, w_up[l], "nn", (BF16,), _epi_sqrelu, f"mlp{l}_up")
        (mo,) = _matmul(act, w_down[l], "nn", (F32,), _epi_plain, f"mlp{l}_down", tk=DFF)
        return act, mo

    (h0,) = _resnorm_fwd(x0, None, None, row(g_mix_pre, 0), "norm_in")
    (a0,) = _matmul(h0, w_pw1, "nn", (F32,), _epi_bias, "conv_pw1", extras=((conv_pw1_b, "row"),))
    y0, z0, (g_up0, g_down0, g_win, g_wo) = _conv_fwd(
        a0, w_dw, conv_dw_b, conv_ln_g, conv_ln_b, "conv_fwd", GATHER_BEHIND_CONV,
        [shard(n) for n, _ in GATHER_BEHIND_CONV])
    w_up, w_down = [by_cols(g_up0)], [by_rows(g_down0)]
    w_o = by_rows(g_wo)
    w_in = jnp.pad(by_cols(g_win), ((0, 0), (0, W_IN_PAD - W_IN_COLS)))
    (m0,) = _matmul(z0, w_pw2, "nn", (F32,), _epi_bias, "conv_pw2", extras=((conv_pw2_b, "row"),))
    x1, h1 = _resnorm_fwd(x0, m0, row(g_mix_post, 0), row(g_ffn_pre, 0), "norm_conv_out")
    act0, mo0 = mlp_fwd(h1, 0)
    x2, h2 = _resnorm_fwd(x1, mo0, row(g_ffn_post, 0), row(g_mix_pre, 1), "norm_mlp0_out")

    (proj,) = _matmul(h2, w_in, "nn", (F32,), _epi_plain, "attn_in", tn=640)
    fl = proj[:, 3 * D:]
    f_all, f1, f2, f3 = _gate_fwd(fl, b_f, "gate_fwd")
    (q_aug, k_aug, v_aug, qn2, kn2), (g_up1, g_down1) = _attn_pack(
        proj, f1, f2, f3, "attn_pack", GATHER_BEHIND_PACK, [shard(n) for n, _ in GATHER_BEHIND_PACK])
    w_up.append(by_cols(g_up1))
    w_down.append(by_rows(g_down1))
    tb = min(ATT_TB, T)
    qk = 2.02 * jnp.sqrt(qn2[:, 0, 0] * kn2[:, 0, 0]) + 1.0
    fs, fe = f_all[0::tb, :N_HEADS].T, f_all[tb - 1::tb, :N_HEADS].T
    o_aug, q2_aug = _flash_fwd(qk, fs, fe, jnp.sqrt(kn2[:, 0, 0]), q_aug, k_aug, v_aug, "flash_fwd")
    (o_bf,) = _merge_heads(o_aug, 1.0, BF16, "attn_merge_o")
    (m1,) = _matmul(o_bf, w_o, "nn", (F32,), _epi_plain, "attn_out")
    x3, h3 = _resnorm_fwd(x2, m1, row(g_mix_post, 1), row(g_ffn_pre, 1), "norm_attn_out")
    act1, mo1 = mlp_fwd(h3, 1)
    dy, loss_part = _resnorm_loss(x3, mo1, row(g_ffn_post, 1), target, "loss")
    loss = lax.psum(loss_part[0, 0], ("x", "y", "c"))

    G = {}
    chip = 2 * lax.axis_index("x") + lax.axis_index("y")
    place = jnp.stack([chip] + [k + (k >= chip).astype(jnp.int32) for k in range(N_CHIPS - 1)]
                      + [lax.axis_index("c")]).astype(jnp.int32)

    def mlp_bwd(d_out, mo, act, h, x_in, l):
        d_mo, dg_post, _ = _norm_bwd(mo, row(g_ffn_post, l), d_out, None, BF16, f"mlp{l}_post_bwd")
        (d_up,) = _matmul(d_mo, w_down[l], "nt", (BF16,), _epi_dsqrelu, f"mlp{l}_dact", extras=((act, "tile"),))
        (dw_down,) = _matmul(act, d_mo, "tn", (F32,), _epi_plain, tk=2048, name=f"mlp{l}_dwdown")
        (dw_up,) = _matmul(h, d_up, "tn", (F32,), _epi_plain, tk=2048, name=f"mlp{l}_dwup", shard_out=True)
        (d_h,) = _matmul(d_up, w_up[l], "nt", (F32,), _epi_plain, f"mlp{l}_dh", tk=DFF)
        d_in, dg_pre, _ = _norm_bwd(x_in, row(g_ffn_pre, l), d_h, d_out, F32, f"mlp{l}_pre_bwd")
        G[f"up{l}"], G[f"down{l}"] = dw_up, dw_down.reshape(N_CHIPS, D, D)
        return d_in, dg_post, dg_pre

    d_x3, dg_ffn_post1, dg_ffn_pre1 = mlp_bwd(dy, mo1, act1, h3, x3, 1)

    d_m1, dg_mix_post1, _ = _norm_bwd(m1, row(g_mix_post, 1), d_x3, None, BF16, "attn_post_bwd")
    (dw_o,) = _matmul(o_bf, d_m1, "tn", (F32,), _epi_plain, tk=2048, name="attn_dwo")
    G["attn_w_o"] = dw_o.reshape(N_CHIPS, D // N_CHIPS, D)
    (d_o,) = _matmul(d_m1, w_o, "nt", (F32,), _epi_plain, "attn_do")
    do_aug = _attn_prep(d_o, o_aug, "attn_prep")
    dq_aug = _flash_dq(qk, fs, fe, q2_aug, k_aug, do_aug, v_aug, "flash_dq")
    dk_aug, dv_aug = _flash_dkv(qk, fs, fe, q2_aug, k_aug, do_aug, v_aug, "flash_dkv")
    d_q, dF_q = _merge_heads(dq_aug, 0.125, BF16, "attn_merge_dq", column=AUG_F)
    d_k, dF_k = _merge_heads(dk_aug, 1.0, BF16, "attn_merge_dk", column=AUG_ONE)
    (d_v,) = _merge_heads(dv_aug, 1.0, BF16, "attn_merge_dv")
    d_fl, db_f = _gate_bwd(dF_q, dF_k, fl, b_f, "gate_bwd")
    d_proj = jnp.concatenate([d_q, d_k, d_v, d_fl.astype(BF16)], axis=1)
    (dw_in,) = _matmul(h2, d_proj, "tn", (F32,), _epi_plain, tk=2048, name="attn_dwin", tn=640)
    G["attn_w_in"] = dw_in[:, :W_IN_COLS].reshape(D, N_CHIPS, W_IN_COLS // N_CHIPS).transpose(1, 0, 2)
    (d_h2,) = _matmul(d_proj, w_in, "nt", (F32,), _epi_plain, "attn_dh", tk=W_IN_PAD)
    d_x2, dg_mix_pre1, _ = _norm_bwd(x2, row(g_mix_pre, 1), d_h2, d_x3, F32, "attn_pre_bwd")

    d_x1, dg_ffn_post0, dg_ffn_pre0 = mlp_bwd(d_x2, mo0, act0, h1, x1, 0)

    d_m0, dg_mix_post0, db_pw2 = _norm_bwd(m0, row(g_mix_post, 0), d_x1, None, BF16, "conv_post_bwd")
    (dw_pw2,) = _matmul(z0, d_m0, "tn", (F32,), _epi_plain, tk=2048, name="conv_dwpw2")
    G["conv_pw2_w"] = dw_pw2.reshape(N_CHIPS, D // N_CHIPS, D)
    (d_z0,) = _matmul(d_m0, w_pw2, "nt", (F32,), _epi_plain, "conv_dz")
    conv_units = [u for u in GRAD_UNITS if u[0].startswith("conv")]
    attn_units = [u for u in GRAD_UNITS if u not in conv_units]
    attn_slabs = [G[n] for n, _, _ in attn_units]
    d_y0, dln_g, dln_b, ddw_b, attn_got = _conv_bwd_ln(d_z0, y0, conv_ln_g, conv_ln_b, "conv_bwd_ln", attn_units,
                                                       attn_slabs)
    pairs = {n: _add_pair(place, s, g, f"grad_pair_{n}") for s, g, (n, _, _) in zip(attn_slabs, attn_got, attn_units)}
    d_a0, db_pw1, ddw_w, attn_by_chip = _conv_bwd_dw(d_y0, a0, w_dw, "conv_bwd_dw",
                                                    [pairs[n] for n, _, _ in attn_units])
    by_chip = {n: b for b, (n, _, _) in zip(attn_by_chip, attn_units)}
    (G["conv_pw1_w"],) = _matmul(h0, d_a0, "tn", (F32,), _epi_plain, tk=2048, name="conv_dwpw1", tn=512, shard_out=True)
    (d_h0,) = _matmul(d_a0, w_pw1, "nt", (F32,), _epi_plain, "conv_dh", tk=2 * D)
    d_x0, dg_mix_pre0, _ = _norm_bwd(x0, row(g_mix_pre, 0), d_h0, d_x1, F32, "conv_pre_bwd")

    small_local = {
        "g_mix_pre": jnp.concatenate([dg_mix_pre0, dg_mix_pre1]), "g_mix_post": jnp.concatenate([dg_mix_post0, dg_mix_post1]),
        "g_ffn_pre": jnp.concatenate([dg_ffn_pre0, dg_ffn_pre1]), "g_ffn_post": jnp.concatenate([dg_ffn_post0, dg_ffn_post1]),
        "conv_pw1_b": db_pw1, "conv_dw_b": ddw_b, "conv_ln_g": dln_g, "conv_ln_b": dln_b, "conv_pw2_b": db_pw2,
        "attn_b_f": db_f[:, :N_HEADS], "conv_dw_w_full": ddw_w,
    }

    conv_slabs = [G[n] for n, _, _ in conv_units]
    conv_got = _sibling_exchange(conv_units, conv_slabs, "grad_sibling_exchange_conv")
    pairs.update({n: _add_pair(place, s, g, f"grad_pair_{n}")
                  for s, g, (n, _, _) in zip(conv_slabs, conv_got, conv_units)})
    conv_by_chip, small_all = _chip_exchange([pairs[n] for n, _, _ in conv_units],
                                             _pack([small_local[n] for n, _ in SMALL_ALL], SMALL_ALL_ROWS))
    by_chip.update({n: b for b, (n, _, _) in zip(conv_by_chip, conv_units)})
    unit_grads = _sibling_share([_add_chips(place, pairs[n], by_chip[n], f"grad_chips_{n}") for n, _, _ in GRAD_UNITS])
    grads = _unpack(_add_slots(small_all, "grad_add_small", tb=SMALL_ALL_ROWS), SMALL_ALL)
    grads["conv_dw_w"] = lax.dynamic_slice(grads.pop("conv_dw_w_full"), (0, chip * 256), (CONV_W, 256))[None]

    delta, new_m, new_v = {}, {}, {}
    per_layer = {}
    for g, (unit, shape, (name, layer)) in zip(unit_grads, GRAD_UNITS):
        view = lambda d: (d[name] if layer is None else d[name][layer]).reshape(shape)
        per_layer.setdefault(name, []).append((g,) + tuple(_adamw(view(weights), g, view(mom_m), view(mom_v),
                                                                  f"adamw_{unit}")))
    for name, parts in per_layer.items():
        full = weights[name].shape
        join = lambda n: (parts[0][n] if len(parts) == 1 else jnp.stack([p[n] for p in parts])).reshape(full)
        grads[name], delta[name], new_m[name], new_v[name] = join(0), join(1), join(2), join(3)
    dw2 = lambda d: d["conv_dw_w"].reshape(CONV_W, 256)
    d, nm, nv = _adamw(dw2(weights), dw2(grads), dw2(mom_m), dw2(mom_v), "adamw_conv_dw_w")
    full = weights["conv_dw_w"].shape
    delta["conv_dw_w"], new_m["conv_dw_w"], new_v["conv_dw_w"] = d.reshape(full), nm.reshape(full), nv.reshape(full)
    packs = [_pack([src[n] for n, _ in SMALL], SMALL_ROWS) for src in (weights, grads, mom_m, mom_v)]
    d, nm, nv = _adamw(*packs, "adamw_small")
    for dst, packed in ((delta, d), (new_m, nm), (new_v, nv)):
        dst.update(_unpack(packed, SMALL))

    grad_x = d_x0.reshape(x.shape)
    return (loss, grad_x, *[grads[n] for n in order], *[delta[n] for n in order], *[new_m[n] for n in order],
            *[new_v[n] for n in order])
```

```python
import functools

import jax
import jax.numpy as jnp
from jax import lax
from jax.experimental import pallas as pl
from jax.experimental.pallas import tpu as pltpu

F32 = jnp.float32
BF16 = jnp.bfloat16
MESH = pl.DeviceIdType.MESH

D = 1024
DFF = 4096
N_HEADS = 16
HEAD_DIM = 64
CONV_W = 31
HALO = 32
RMS_EPS = 1e-6
LN_EPS = 1e-5
MASK_VALUE = -1e30
W_IN_COLS = 3 * D + N_HEADS
W_IN_PAD = 3200
LANES = 128
SUBLANES = 8
N_CHIPS = 4
VMEM_LIMIT = 56 * 1024 * 1024
MATMUL_CHUNK = 256

ADAM_LR = 0.001
ADAM_B1 = 0.9
ADAM_B2 = 0.999
ADAM_EPS = 1e-08
ADAM_WD = 0.01
ADAM_STEP = 10

AUG_F = 64
AUG_ONE = 67
AUG_L = 70
SKIP_BELOW = -104.0
FIXED_MAX_BELOW = 40.0

GATHER_AT_START = (("conv_pw1_w", (1024, 512)), ("conv_dw_w", (HALO, 256)), ("conv_pw2_w", (256, 1024)))
GATHER_BEHIND_CONV = (("up0", (1024, 1024)), ("down0", (1024, 1024)), ("attn_w_in", (1024, 772)),
                      ("attn_w_o", (256, 1024)))
GATHER_BEHIND_PACK = (("up1", (1024, 1024)), ("down1", (1024, 1024)))
GRAD_UNITS = (
    ("conv_pw1_w", (1024, 512), ("conv_pw1_w", None)),
    ("conv_pw2_w", (256, 1024), ("conv_pw2_w", None)),
    ("attn_w_in", (1024, 772), ("attn_w_in", None)),
    ("attn_w_o", (256, 1024), ("attn_w_o", None)),
    ("up0", (1024, 1024), ("mlp_w_up", 0)),
    ("up1", (1024, 1024), ("mlp_w_up", 1)),
    ("down0", (1024, 1024), ("mlp_w_down", 0)),
    ("down1", (1024, 1024), ("mlp_w_down", 1)),
)
SMALL = (
    ("g_mix_pre", (2, 1024)), ("g_mix_post", (2, 1024)), ("g_ffn_pre", (2, 1024)), ("g_ffn_post", (2, 1024)),
    ("conv_pw1_b", (1, 2048)), ("conv_dw_b", (1, 1024)), ("conv_ln_g", (1, 1024)), ("conv_ln_b", (1, 1024)),
    ("conv_pw2_b", (1, 1024)), ("attn_b_f", (1, 16)),
)
SMALL_ROWS = 120
SMALL_ALL = SMALL + (("conv_dw_w_full", (HALO, 1024)),)
SMALL_ALL_ROWS = 376


def _size(shape):
    n = 1
    for s in shape:
        n *= s
    return n


def _pack(arrays, rows):
    flat = jnp.concatenate([a.reshape(-1) for a in arrays])
    return jnp.pad(flat, (0, rows * LANES - flat.shape[0])).reshape(rows, LANES)


def _unpack(packed, table):
    flat = packed.reshape(-1)
    out, off = {}, 0
    for name, shape in table:
        n = _size(shape)
        out[name] = flat[off:off + n].reshape(shape)
        off += n
    return out


def _params(sem):
    return pltpu.CompilerParams(dimension_semantics=sem, vmem_limit_bytes=VMEM_LIMIT)


def _rms(x, g):
    return x * lax.rsqrt(jnp.mean(x * x, axis=-1, keepdims=True) + RMS_EPS) * g


def _split3(v):
    p1 = v.astype(BF16).astype(F32)
    r = v - p1
    p2 = r.astype(BF16).astype(F32)
    p3 = (r - p2).astype(BF16).astype(F32)
    return p1, p2, p3


def _matmul(a, b, mode, out_dtypes, epi, name, extras=(), tm=1024, tn=1024, tk=1024, shard_out=False):
    (K, M) = a.shape if mode == "tn" else a.shape[::-1]
    N = b.shape[0] if mode == "nt" else b.shape[1]
    tm, tn, tk = min(tm, M), min(tn, N), min(tk, K)
    nk = K // tk
    if mode == "tn":
        a_spec = pl.BlockSpec((tk, tm), lambda i, j, k: (k, i))
    else:
        a_spec = pl.BlockSpec((tm, tk), lambda i, j, k: (i, k))
    if mode == "nt":
        b_spec = pl.BlockSpec((tn, tk), lambda i, j, k: (j, k))
    else:
        b_spec = pl.BlockSpec((tk, tn), lambda i, j, k: (k, j))
    if shard_out:
        per = N // N_CHIPS // tn
        o_spec = pl.BlockSpec((None, tm, tn), lambda i, j, k: (j // per, i, j % per))
        o_shape = (N_CHIPS, M, N // N_CHIPS)
    else:
        o_spec = pl.BlockSpec((tm, tn), lambda i, j, k: (i, j))
        o_shape = (M, N)
    dims = {"nn": (((1,), (0,)), ((), ())), "nt": (((1,), (1,)), ((), ())), "tn": (((0,), (0,)), ((), ()))}[mode]
    ex_specs = []
    for _, kind in extras:
        if kind == "row":
            ex_specs.append(pl.BlockSpec((1, tn), lambda i, j, k: (0, j)))
        else:
            ex_specs.append(pl.BlockSpec((tm, tn), lambda i, j, k: (i, j)))
    n_ex, n_out = len(extras), len(out_dtypes)

    cw = MATMUL_CHUNK if tn % MATMUL_CHUNK == 0 else tn

    def body(*refs):
        a_ref, b_ref = refs[0], refs[1]
        ex = refs[2:2 + n_ex]
        outs = refs[2 + n_ex:2 + n_ex + n_out]

        def for_chunks(use):
            for c in range(tn // cw):
                cols = slice(c * cw, (c + 1) * cw)
                b_chunk = b_ref[cols, :] if mode == "nt" else b_ref[:, cols]
                use(cols, lax.dot_general(a_ref[...], b_chunk, dims, preferred_element_type=F32))

        def finish(cols, acc):
            res = epi(acc, *[e[:, cols] for e in ex])
            for o, r in zip(outs, res):
                o[:, cols] = r.astype(o.dtype)

        if nk == 1:
            for_chunks(finish)
        else:
            acc_ref = refs[-1]
            k = pl.program_id(2)

            def first(cols, prod):
                acc_ref[:, cols] = prod

            def middle(cols, prod):
                acc_ref[:, cols] += prod

            pl.when(k == 0)(lambda: for_chunks(first))
            pl.when(jnp.logical_and(k > 0, k < nk - 1))(lambda: for_chunks(middle))
            pl.when(k == nk - 1)(lambda: for_chunks(lambda cols, prod: finish(cols, acc_ref[:, cols] + prod)))

    return pl.pallas_call(
        body,
        name=name,
        grid=(M // tm, N // tn, nk),
        in_specs=[a_spec, b_spec] + ex_specs,
        out_specs=[o_spec for _ in out_dtypes],
        out_shape=[jax.ShapeDtypeStruct(o_shape, dt) for dt in out_dtypes],
        scratch_shapes=[pltpu.VMEM((tm, tn), F32)] if nk > 1 else [],
        compiler_params=_params(("parallel", "parallel", "arbitrary")),
    )(a, b, *[e for e, _ in extras])


def _epi_plain(acc):
    return (acc,)


def _epi_bias(acc, bias):
    return (acc + bias,)


def _epi_sqrelu(acc):
    r = jnp.maximum(acc, 0.0)
    return (r * r,)


def _epi_dsqrelu(acc, act):
    a = act.astype(F32)
    return (acc * jnp.where(a > 0.0, 2.0 * a * lax.rsqrt(a), 0.0),)


def _resnorm_fwd(x, m, g_post, g_next, name, tr=512):
    T = x.shape[0]
    has_m = m is not None
    row = pl.BlockSpec((tr, D), lambda i: (i, 0))
    vec = pl.BlockSpec((1, D), lambda i: (0, 0))

    def body(*refs):
        if has_m:
            x_ref, m_ref, gp_ref, gn_ref, x1_ref, h_ref = refs
            x1 = x_ref[...] + _rms(m_ref[...], gp_ref[...])
            x1_ref[...] = x1
        else:
            x_ref, gn_ref, h_ref = refs
            x1 = x_ref[...]
        h_ref[...] = _rms(x1, gn_ref[...]).astype(BF16)

    if has_m:
        args, in_specs = (x, m, g_post, g_next), [row, row, vec, vec]
        out_specs = [row, row]
        out_shape = [jax.ShapeDtypeStruct((T, D), F32), jax.ShapeDtypeStruct((T, D), BF16)]
    else:
        args, in_specs = (x, g_next), [row, vec]
        out_specs = [row]
        out_shape = [jax.ShapeDtypeStruct((T, D), BF16)]
    return pl.pallas_call(
        body, name=name, grid=(T // tr,), in_specs=in_specs, out_specs=out_specs, out_shape=out_shape,
        compiler_params=_params(("parallel",)),
    )(*args)


def _resnorm_loss(x, m, g_post, target, name, tr=512):
    T = x.shape[0]
    row = pl.BlockSpec((tr, D), lambda i: (i, 0))
    vec = pl.BlockSpec((1, D), lambda i: (0, 0))

    def body(x_ref, m_ref, gp_ref, t_ref, dy_ref, loss_ref):
        diff = x_ref[...] + _rms(m_ref[...], gp_ref[...]) - t_ref[...]
        dy_ref[...] = diff * (1.0 / D)

        @pl.when(pl.program_id(0) == 0)
        def _():
            loss_ref[...] = jnp.zeros_like(loss_ref)

        per_token = jnp.mean(diff * diff, axis=-1, keepdims=True)
        loss_ref[...] += 0.5 * jnp.sum(per_token)

    return pl.pallas_call(
        body, name=name, grid=(T // tr,), in_specs=[row, row, vec, row],
        out_specs=[row, pl.BlockSpec((8, LANES), lambda i: (0, 0))],
        out_shape=[jax.ShapeDtypeStruct((T, D), F32), jax.ShapeDtypeStruct((8, LANES), F32)],
        compiler_params=_params(("arbitrary",)),
    )(x, m, g_post, target)


def _norm_bwd(x, g, dy, resid, out_dtype, name, tr=512):
    T = x.shape[0]
    has_r = resid is not None
    row = pl.BlockSpec((tr, D), lambda i: (i, 0))
    vec = pl.BlockSpec((1, D), lambda i: (0, 0))

    def body(*refs):
        if has_r:
            x_ref, g_ref, dy_ref, r_ref, dx_ref, dg_ref, cs_ref = refs
        else:
            x_ref, g_ref, dy_ref, dx_ref, dg_ref, cs_ref = refs
        xv, dyv = x_ref[...], dy_ref[...]
        r = lax.rsqrt(jnp.mean(xv * xv, axis=-1, keepdims=True) + RMS_EPS)
        gy = dyv * g_ref[...]
        c = jnp.sum(gy * xv, axis=-1, keepdims=True) * (1.0 / D)
        dx = r * gy - xv * (r * r * r * c)

        @pl.when(pl.program_id(0) == 0)
        def _():
            dg_ref[...] = jnp.zeros_like(dg_ref)
            cs_ref[...] = jnp.zeros_like(cs_ref)

        dg_ref[...] += jnp.sum(dyv * (xv * r), axis=0, keepdims=True)
        cs_ref[...] += jnp.sum(dx, axis=0, keepdims=True)
        if has_r:
            dx = dx + r_ref[...]
        dx_ref[...] = dx.astype(dx_ref.dtype)

    args = (x, g, dy) + ((resid,) if has_r else ())
    return pl.pallas_call(
        body, name=name, grid=(T // tr,), in_specs=[row, vec, row] + ([row] if has_r else []),
        out_specs=[row, vec, vec],
        out_shape=[jax.ShapeDtypeStruct((T, D), out_dtype), jax.ShapeDtypeStruct((1, D), F32),
                   jax.ShapeDtypeStruct((1, D), F32)],
        compiler_params=_params(("arbitrary",)),
    )(*args)


CONV_TB = 256
CONV_RC = 32


def _glu(a):
    return a[:, :D] * jax.nn.sigmoid(a[:, D:])


def _shifted_copies(win):
    rows = win.shape[1]
    for p in range(1, SUBLANES):
        win[p, 0:rows - SUBLANES, :] = win[0, p:p + rows - SUBLANES, :]


def _window(win, start):
    p = start % SUBLANES
    return win[p, start - p:start - p + CONV_RC, :]


def _conv_fwd(a, dw_w, dw_b, ln_g, ln_b, name, gather_units, gather_shards):
    T = a.shape[0]
    tb = min(CONV_TB, T)
    per = tb // HALO
    n_steps = T // tb
    n_g = len(gather_units)
    vec = pl.BlockSpec((1, D), lambda i: (0, 0))

    def body(*refs):
        cur_ref, prev_ref, w_ref, b_ref, lg_ref, lb_ref = refs[:6]
        g_ins = refs[6:6 + n_g]
        y_ref, z_ref = refs[6 + n_g:8 + n_g]
        g_outs = refs[8 + n_g:8 + 2 * n_g]
        uwin = refs[8 + 2 * n_g]
        sems = refs[9 + 2 * n_g:]
        i = pl.program_id(0)
        pl.when(i == 0)(lambda: _gather_start(gather_units, g_ins, g_outs, sems))
        uwin[0, 0:HALO, :] = jnp.where(i > 0, _glu(prev_ref[...]), 0.0)
        uwin[0, HALO:, :] = _glu(cur_ref[...])
        _shifted_copies(uwin)
        for r in range(tb // CONV_RC):
            acc = jnp.broadcast_to(b_ref[...], (CONV_RC, D))
            for k in range(CONV_W):
                start = r * CONV_RC + HALO - (CONV_W - 1) + k
                acc = acc + _window(uwin, start) * w_ref[k:k + 1, :]
            rows = slice(r * CONV_RC, (r + 1) * CONV_RC)
            y_ref[rows, :] = acc
            mu = jnp.mean(acc, axis=-1, keepdims=True)
            xc = acc - mu
            var = jnp.mean(xc * xc, axis=-1, keepdims=True)
            n = xc * lax.rsqrt(var + LN_EPS) * lg_ref[...] + lb_ref[...]
            z_ref[rows, :] = (n * jax.nn.sigmoid(n)).astype(BF16)
        pl.when(i == n_steps - 1)(lambda: _gather_finish(gather_units, g_ins, g_outs, sems))

    outs = pl.pallas_call(
        body, name=name, grid=(n_steps,),
        in_specs=[pl.BlockSpec((tb, 2 * D), lambda i: (i, 0)),
                  pl.BlockSpec((HALO, 2 * D), lambda i: (jnp.maximum(i * per - 1, 0), 0)),
                  pl.BlockSpec((HALO, D), lambda i: (0, 0)), vec, vec, vec] + [ANY] * n_g,
        out_specs=[pl.BlockSpec((tb, D), lambda i: (i, 0)), pl.BlockSpec((tb, D), lambda i: (i, 0))] + [ANY] * n_g,
        out_shape=[jax.ShapeDtypeStruct((T, D), F32), jax.ShapeDtypeStruct((T, D), BF16)]
        + [jax.ShapeDtypeStruct((N_CHIPS,) + shape, BF16) for _, shape in gather_units],
        scratch_shapes=[pltpu.VMEM((SUBLANES, tb + HALO, D), F32)] + _gather_sems(n_g),
        compiler_params=_params(("arbitrary",)),
    )(a, a, dw_w, dw_b, ln_g, ln_b, *gather_shards)
    return outs[0], outs[1], outs[2:]


def _conv_bwd_ln(dz, y, ln_g, ln_b, name, exchange_units, slabs, tr=256):
    T = y.shape[0]
    tr = min(tr, T)
    n_steps = T // tr
    n_s = len(slabs)
    row = pl.BlockSpec((tr, D), lambda i: (i, 0))
    vec = pl.BlockSpec((1, D), lambda i: (0, 0))

    def body(*refs):
        dz_ref, y_ref, lg_ref, lb_ref = refs[:4]
        s_ins = refs[4:4 + n_s]
        dy_ref, dlg_ref, dlb_ref, db_ref = refs[4 + n_s:8 + n_s]
        s_outs = refs[8 + n_s:8 + 2 * n_s]
        sems = refs[8 + 2 * n_s:]
        pl.when(pl.program_id(0) == 0)(lambda: _sibling_start(exchange_units, s_ins, s_outs, sems))
        yv = y_ref[...]
        mu = jnp.mean(yv, axis=-1, keepdims=True)
        xc = yv - mu
        rstd = lax.rsqrt(jnp.mean(xc * xc, axis=-1, keepdims=True) + LN_EPS)
        yh = xc * rstd
        n = yh * lg_ref[...] + lb_ref[...]
        sg = jax.nn.sigmoid(n)
        dn = dz_ref[...] * (sg * (1.0 + n * (1.0 - sg)))
        dyh = dn * lg_ref[...]
        dyv = rstd * (dyh - jnp.mean(dyh, axis=-1, keepdims=True) - yh * jnp.mean(dyh * yh, axis=-1, keepdims=True))
        dy_ref[...] = dyv

        @pl.when(pl.program_id(0) == 0)
        def _():
            dlg_ref[...] = jnp.zeros_like(dlg_ref)
            dlb_ref[...] = jnp.zeros_like(dlb_ref)
            db_ref[...] = jnp.zeros_like(db_ref)

        dlg_ref[...] += jnp.sum(dn * yh, axis=0, keepdims=True)
        dlb_ref[...] += jnp.sum(dn, axis=0, keepdims=True)
        db_ref[...] += jnp.sum(dyv, axis=0, keepdims=True)
        pl.when(pl.program_id(0) == n_steps - 1)(lambda: _sibling_finish(exchange_units, s_ins, s_outs, sems))

    outs = pl.pallas_call(
        body, name=name, grid=(n_steps,), in_specs=[row, row, vec, vec] + [ANY] * n_s,
        out_specs=[row, vec, vec, vec] + [ANY] * n_s,
        out_shape=[jax.ShapeDtypeStruct((T, D), F32)] + [jax.ShapeDtypeStruct((1, D), F32)] * 3
        + [jax.ShapeDtypeStruct((N_CHIPS, rows // 2, cols), F32) for _, (rows, cols), _ in exchange_units],
        scratch_shapes=[pltpu.SemaphoreType.DMA((n_s,)), pltpu.SemaphoreType.DMA((n_s,))],
        compiler_params=_params(("arbitrary",)),
    )(dz, y, ln_g, ln_b, *slabs)
    return outs[0], outs[1], outs[2], outs[3], outs[4:]


def _conv_bwd_dw(dy, a, dw_w, name, exchange):
    T = a.shape[0]
    tb = min(CONV_TB, T)
    per = tb // HALO
    last_halo = T // HALO - 1
    n_steps = T // tb
    n_x = len(exchange)

    def body(*refs):
        dyc_ref, dyn_ref, cur_ref, prev_ref, w_ref = refs[:5]
        x_ins = refs[5:5 + n_x]
        da_ref, dbias_ref, dw_ref = refs[5 + n_x:8 + n_x]
        x_outs = refs[8 + n_x:8 + 2 * n_x]
        uwin, dywin, dwacc = refs[8 + 2 * n_x:11 + 2 * n_x]
        sems = refs[11 + 2 * n_x:]
        i = pl.program_id(0)

        @pl.when(i == 0)
        def _():
            _exchange_start(x_ins, x_outs, sems)
            dbias_ref[...] = jnp.zeros_like(dbias_ref)
            dwacc[...] = jnp.zeros_like(dwacc)

        uwin[0, 0:HALO, :] = jnp.where(i > 0, _glu(prev_ref[...]), 0.0)
        uwin[0, HALO:, :] = _glu(cur_ref[...])
        dywin[0, 0:tb, :] = dyc_ref[...]
        dywin[0, tb:, :] = jnp.where(i < n_steps - 1, dyn_ref[...], 0.0)
        _shifted_copies(uwin)
        _shifted_copies(dywin)
        for r in range(tb // CONV_RC):
            rows = slice(r * CONV_RC, (r + 1) * CONV_RC)
            dy_c = dywin[0, rows, :]
            du = jnp.zeros((CONV_RC, D), F32)
            for k in range(CONV_W):
                du = du + _window(dywin, r * CONV_RC + (CONV_W - 1) - k) * w_ref[k:k + 1, :]
                prod = dy_c * _window(uwin, r * CONV_RC + HALO - (CONV_W - 1) + k)
                part = prod[0:8, :]
                for q in range(1, CONV_RC // 8):
                    part = part + prod[8 * q:8 * q + 8, :]
                dwacc[8 * k:8 * k + 8, :] += part
            av = cur_ref[rows, :]
            a1, sg = av[:, :D], jax.nn.sigmoid(av[:, D:])
            da1 = du * sg
            da2 = du * a1 * (sg * (1.0 - sg))
            da_ref[rows, 0:D] = da1.astype(BF16)
            da_ref[rows, D:] = da2.astype(BF16)
            dbias_ref[:, 0:D] += jnp.sum(da1, axis=0, keepdims=True)
            dbias_ref[:, D:] += jnp.sum(da2, axis=0, keepdims=True)

        @pl.when(i == n_steps - 1)
        def _():
            for k in range(CONV_W):
                dw_ref[k:k + 1, :] = jnp.sum(dwacc[8 * k:8 * k + 8, :], axis=0, keepdims=True)
            dw_ref[CONV_W:, :] = jnp.zeros((HALO - CONV_W, D), F32)
            _exchange_finish(x_ins, x_outs, sems)

    outs = pl.pallas_call(
        body, name=name, grid=(n_steps,),
        in_specs=[pl.BlockSpec((tb, D), lambda i: (i, 0)),
                  pl.BlockSpec((HALO, D), lambda i: (jnp.minimum((i + 1) * per, last_halo), 0)),
                  pl.BlockSpec((tb, 2 * D), lambda i: (i, 0)),
                  pl.BlockSpec((HALO, 2 * D), lambda i: (jnp.maximum(i * per - 1, 0), 0)),
                  pl.BlockSpec((HALO, D), lambda i: (0, 0))] + [ANY] * n_x,
        out_specs=[pl.BlockSpec((tb, 2 * D), lambda i: (i, 0)), pl.BlockSpec((1, 2 * D), lambda i: (0, 0)),
                   pl.BlockSpec((HALO, D), lambda i: (0, 0))] + [ANY] * n_x,
        out_shape=[jax.ShapeDtypeStruct((T, 2 * D), BF16), jax.ShapeDtypeStruct((1, 2 * D), F32),
                   jax.ShapeDtypeStruct((HALO, D), F32)] + [jax.ShapeDtypeStruct(p.shape, p.dtype) for p in exchange],
        scratch_shapes=[pltpu.VMEM((SUBLANES, tb + HALO, D), F32), pltpu.VMEM((SUBLANES, tb + HALO, D), F32),
                        pltpu.VMEM((8 * HALO, D), F32)] + _exchange_sems(n_x),
        compiler_params=_params(("arbitrary",)),
    )(dy, dy, a, a, dw_w, *exchange)
    return outs[0], outs[1], outs[2], outs[3:]


GATE_TB = 512


def _gate_fwd(fl, b_f, name):
    T = fl.shape[0]
    tb = min(GATE_TB, T)
    row = pl.BlockSpec((tb, LANES), lambda i: (i, 0))

    def body(fl_ref, b_ref, f_ref, f1_ref, f2_ref, f3_ref, carry):
        @pl.when(pl.program_id(0) == 0)
        def _():
            carry[...] = jnp.zeros_like(carry)

        z = fl_ref[...] + b_ref[...]
        lf = jnp.minimum(z, 0.0) - jnp.log(1.0 + jnp.exp(-jnp.abs(z)))
        tri = (lax.broadcasted_iota(jnp.int32, (tb, tb), 0) >= lax.broadcasted_iota(jnp.int32, (tb, tb), 1)).astype(F32)
        f = jnp.dot(tri, lf, precision=lax.Precision.HIGHEST, preferred_element_type=F32) + carry[...]
        carry[...] = f[tb - 1:tb, :]
        f_ref[...] = f
        f1_ref[...], f2_ref[...], f3_ref[...] = _split3(f)

    return pl.pallas_call(
        body, name=name, grid=(T // tb,), in_specs=[row, pl.BlockSpec((1, LANES), lambda i: (0, 0))],
        out_specs=[row] * 4, out_shape=[jax.ShapeDtypeStruct((T, LANES), F32)] * 4,
        scratch_shapes=[pltpu.VMEM((1, LANES), F32)],
        compiler_params=_params(("arbitrary",)),
    )(fl, b_f)


def _gate_bwd(dF_q, dF_k, fl, b_f, name):
    T = fl.shape[0]
    tb = min(GATE_TB, T)
    nb = T // tb
    row = pl.BlockSpec((tb, LANES), lambda i: (nb - 1 - i, 0))
    vec = pl.BlockSpec((1, LANES), lambda i: (0, 0))

    def body(dfq_ref, dfk_ref, fl_ref, b_ref, dfl_ref, db_ref, carry):
        @pl.when(pl.program_id(0) == 0)
        def _():
            carry[...] = jnp.zeros_like(carry)
            db_ref[...] = jnp.zeros_like(db_ref)

        tri = (lax.broadcasted_iota(jnp.int32, (tb, tb), 0) <= lax.broadcasted_iota(jnp.int32, (tb, tb), 1)).astype(F32)
        rc = jnp.dot(tri, dfq_ref[...] - dfk_ref[...], precision=lax.Precision.HIGHEST,
                     preferred_element_type=F32) + carry[...]
        carry[...] = rc[0:1, :]
        dfl = rc * jax.nn.sigmoid(-(fl_ref[...] + b_ref[...]))
        dfl_ref[...] = dfl
        db_ref[...] += jnp.sum(dfl, axis=0, keepdims=True)

    return pl.pallas_call(
        body, name=name, grid=(nb,), in_specs=[row, row, row, vec], out_specs=[row, vec],
        out_shape=[jax.ShapeDtypeStruct((T, LANES), F32), jax.ShapeDtypeStruct((1, LANES), F32)],
        scratch_shapes=[pltpu.VMEM((1, LANES), F32)],
        compiler_params=_params(("arbitrary",)),
    )(dF_q, dF_k, fl, b_f)


ATT_TB = 512
LAYOUT_TB = 2048
NT = (((1,), (1,)), ((), ()))
NN = (((1,), (0,)), ((), ()))


def _lane_insert(base, parts, first_lane):
    lane = lax.broadcasted_iota(jnp.int32, base.shape, 1)
    out = base
    for n, p in enumerate(parts):
        out = jnp.where(lane == first_lane + n, p, out)
    return out


def _causal(tb):
    return lax.broadcasted_iota(jnp.int32, (tb, tb), 0) >= lax.broadcasted_iota(jnp.int32, (tb, tb), 1)


def _seen_by(n_keys, n_queries):
    return (lax.broadcasted_iota(jnp.int32, (n_keys, n_queries), 1)
            >= lax.broadcasted_iota(jnp.int32, (n_keys, n_queries), 0))


def _other_head(x):
    return pltpu.roll(x, HEAD_DIM, 1)


def _attn_pack(proj, f1, f2, f3, name, gather_units, gather_shards):
    T = proj.shape[0]
    tb = min(LAYOUT_TB, T)
    pairs = N_HEADS // 2
    n_steps = T // tb
    n_g = len(gather_units)
    cols = lambda first: pl.BlockSpec((tb, LANES), lambda p, i: (i, first + p))
    gate = pl.BlockSpec((tb, LANES), lambda p, i: (i, 0))
    heads = pl.BlockSpec((2, tb, LANES), lambda p, i: (p, i, 0))
    norm = pl.BlockSpec((2, 8, LANES), lambda p, i: (p, 0, 0))

    def body(*refs):
        q_ref, k_ref, v_ref, f1_ref, f2_ref, f3_ref = refs[:6]
        g_ins = refs[6:6 + n_g]
        qa_ref, ka_ref, va_ref, qn_ref, kn_ref = refs[6 + n_g:11 + n_g]
        g_outs = refs[11 + n_g:11 + 2 * n_g]
        sems = refs[11 + 2 * n_g:]
        p, i = pl.program_id(0), pl.program_id(1)
        pl.when(jnp.logical_and(p == 0, i == 0))(lambda: _gather_start(gather_units, g_ins, g_outs, sems))
        lane = lax.broadcasted_iota(jnp.int32, (tb, LANES), 1)
        data = lane < HEAD_DIM

        @pl.when(i == 0)
        def _():
            qn_ref[...] = jnp.zeros_like(qn_ref)
            kn_ref[...] = jnp.zeros_like(kn_ref)

        qv, kv, vv = q_ref[...].astype(F32) * 0.125, k_ref[...].astype(F32), v_ref[...].astype(F32)
        for e in range(2):
            fcol = [jnp.sum(jnp.where(lane == 2 * p + e, f[...], 0.0), axis=-1, keepdims=True)
                    for f in (f1_ref, f2_ref, f3_ref)]
            pick = (lambda t: t) if e == 0 else _other_head
            qd = jnp.where(data, pick(qv), 0.0).astype(BF16).astype(F32)
            kd = jnp.where(data, pick(kv), 0.0).astype(BF16).astype(F32)
            vd = jnp.where(data, pick(vv), 0.0)
            qa_ref[e] = _lane_insert(qd, fcol + [1.0, 1.0, 1.0], AUG_F).astype(BF16)
            ka_ref[e] = _lane_insert(kd, [1.0, 1.0, 1.0] + [-f for f in fcol] + [1.0, 1.0, 1.0], AUG_F).astype(BF16)
            va_ref[e] = _lane_insert(vd, [-1.0, -1.0, -1.0, 1.0], AUG_F).astype(BF16)
            qn_ref[e] = jnp.maximum(qn_ref[e], jnp.max(jnp.sum(qd * qd, axis=-1, keepdims=True)))
            kn_ref[e] = jnp.maximum(kn_ref[e], jnp.max(jnp.sum(kd * kd, axis=-1, keepdims=True)))
        pl.when(jnp.logical_and(p == pairs - 1, i == n_steps - 1))(
            lambda: _gather_finish(gather_units, g_ins, g_outs, sems))

    aug = jax.ShapeDtypeStruct((N_HEADS, T, LANES), BF16)
    nrm = jax.ShapeDtypeStruct((N_HEADS, 8, LANES), F32)
    outs = pl.pallas_call(
        body, name=name, grid=(pairs, n_steps),
        in_specs=[cols(0), cols(pairs), cols(2 * pairs), gate, gate, gate] + [ANY] * n_g,
        out_specs=[heads, heads, heads, norm, norm] + [ANY] * n_g,
        out_shape=[aug, aug, aug, nrm, nrm]
        + [jax.ShapeDtypeStruct((N_CHIPS,) + shape, BF16) for _, shape in gather_units],
        scratch_shapes=_gather_sems(n_g),
        compiler_params=_params(("arbitrary", "arbitrary")),
    )(proj, proj, proj, f1, f2, f3, *gather_shards)
    return outs[:5], outs[5:]


def _merge_heads(x_aug, scale, out_dtype, name, column=None):
    H, T, _ = x_aug.shape
    tb = min(LAYOUT_TB, T)

    def body(x_ref, o_ref, *col_ref):
        p = pl.program_id(1)
        lane = lax.broadcasted_iota(jnp.int32, (tb, LANES), 1)
        x0, x1 = x_ref[0], x_ref[1]
        o_ref[...] = (jnp.where(lane < HEAD_DIM, x0, _other_head(x1)) * scale).astype(out_dtype)
        if column is not None:
            @pl.when(p == 0)
            def _():
                col_ref[0][...] = jnp.zeros_like(col_ref[0])

            c0 = jnp.sum(jnp.where(lane == column, x0, 0.0), axis=-1, keepdims=True)
            c1 = jnp.sum(jnp.where(lane == column, x1, 0.0), axis=-1, keepdims=True)
            col_ref[0][...] += jnp.where(lane == 2 * p, c0, 0.0) + jnp.where(lane == 2 * p + 1, c1, 0.0)

    out_specs = [pl.BlockSpec((tb, LANES), lambda i, p: (i, p))]
    out_shape = [jax.ShapeDtypeStruct((T, D), out_dtype)]
    if column is not None:
        out_specs.append(pl.BlockSpec((tb, LANES), lambda i, p: (i, 0)))
        out_shape.append(jax.ShapeDtypeStruct((T, LANES), F32))
    return pl.pallas_call(
        body, name=name, grid=(T // tb, H // 2),
        in_specs=[pl.BlockSpec((2, tb, LANES), lambda i, p: (p, i, 0))],
        out_specs=out_specs, out_shape=out_shape,
        compiler_params=_params(("parallel", "arbitrary")),
    )(x_aug)


def _attn_prep(d_o, o_aug, name):
    H, T, _ = o_aug.shape
    tb = min(LAYOUT_TB, T)
    heads = pl.BlockSpec((2, tb, LANES), lambda p, i: (p, i, 0))

    def body(do_ref, o_ref, out_ref):
        lane = lax.broadcasted_iota(jnp.int32, (tb, LANES), 1)
        dov = do_ref[...]
        for e in range(2):
            d_e = jnp.where(lane < HEAD_DIM, dov if e == 0 else _other_head(dov), 0.0)
            delta = jnp.sum(d_e * o_ref[e], axis=-1, keepdims=True)
            out_ref[e] = _lane_insert(d_e, _split3(delta), AUG_F).astype(BF16)

    return pl.pallas_call(
        body, name=name, grid=(H // 2, T // tb),
        in_specs=[pl.BlockSpec((tb, LANES), lambda p, i: (i, p)), heads], out_specs=heads,
        out_shape=jax.ShapeDtypeStruct((H, T, LANES), BF16),
        compiler_params=_params(("parallel", "parallel")),
    )(d_o, o_aug)


def _flash_spec(n_in, n_out, H, nb, tb, T, resident, scratch, n_tables=3):
    blk = pl.BlockSpec((2, tb, LANES), lambda h, i, *_: (h, i, 0))
    whole = pl.BlockSpec((2, T, LANES), lambda h, i, *_: (h, 0, 0))
    return pltpu.PrefetchScalarGridSpec(
        num_scalar_prefetch=n_tables, grid=(H // 2, nb),
        in_specs=[whole if resident[n] else blk for n in range(n_in)],
        out_specs=[blk] * n_out, scratch_shapes=scratch)


def _first_live(qk_ref, fs_ref, fe_ref, h, i):
    top = qk_ref[h] + fs_ref[h, i]
    return lax.fori_loop(0, i, lambda j, n: n + jnp.where(top - fe_ref[h, j] < SKIP_BELOW, 1, 0), 0)


def _flash_fwd(qk, fs, fe, kmax, q_aug, k_aug, v_aug, name):
    H, T, _ = q_aug.shape
    tb = min(ATT_TB, T)

    def body(qk_ref, fs_ref, fe_ref, km_ref, q_ref, k_ref, v_ref, o_ref, q2_ref, m_s, acc_s, qm_s):
        h, i = 2 * pl.program_id(0), pl.program_id(1)
        acc_s[...] = jnp.zeros_like(acc_s)
        first = jnp.minimum(_first_live(qk_ref, fs_ref, fe_ref, h, i), _first_live(qk_ref, fs_ref, fe_ref, h + 1, i))
        bounded = jnp.logical_and(qk_ref[h] < FIXED_MAX_BELOW, qk_ref[h + 1] < FIXED_MAX_BELOW)

        def sweep(step):
            def loop_body(kb, carry):
                step(kb, False)
                return carry

            lax.fori_loop(first, i, loop_body, 0)
            step(i, True)

        @pl.when(bounded)
        def _():
            lane = lax.broadcasted_iota(jnp.int32, (tb, LANES), 1)
            for e in range(2):
                qf = q_ref[e].astype(F32)
                norm = jnp.sqrt(jnp.sum(jnp.where(lane < HEAD_DIM, qf * qf, 0.0), axis=-1, keepdims=True))
                bound = norm * (1.01 * km_ref[h + e]) + 1e-3
                m_s[e] = bound
                qm_s[e] = _lane_insert(qf, [-p for p in _split3(bound)], AUG_L).astype(BF16)

            def step(kb, masked):
                rows = pl.ds(pl.multiple_of(kb * tb, tb), tb)
                for e in range(2):
                    p = jnp.exp(lax.dot_general(qm_s[e], k_ref[e, rows, :], NT, preferred_element_type=F32))
                    if masked:
                        p = jnp.where(_causal(tb), p, 0.0)
                    acc_s[e] += lax.dot_general(p.astype(BF16), v_ref[e, rows, :], NN, preferred_element_type=F32)

            sweep(step)

        @pl.when(jnp.logical_not(bounded))
        def _():
            m_s[...] = jnp.full(m_s.shape, -jnp.inf, F32)

            def step(kb, masked):
                rows = pl.ds(pl.multiple_of(kb * tb, tb), tb)
                for e in range(2):
                    s = lax.dot_general(q_ref[e], k_ref[e, rows, :], NT, preferred_element_type=F32)
                    if masked:
                        s = jnp.where(_causal(tb), s, MASK_VALUE)
                    m_old = m_s[e]
                    m_new = jnp.maximum(m_old, jnp.max(s, axis=-1, keepdims=True))
                    p = jnp.exp(s - m_new)
                    acc_s[e] = jnp.exp(m_old - m_new) * acc_s[e] + lax.dot_general(
                        p.astype(BF16), v_ref[e, rows, :], NN, preferred_element_type=F32)
                    m_s[e] = m_new

            sweep(step)

        for e in range(2):
            acc = acc_s[e]
            lane = lax.broadcasted_iota(jnp.int32, acc.shape, 1)
            l = jnp.sum(jnp.where(lane == AUG_ONE, acc, 0.0), axis=-1, keepdims=True)
            o_ref[e] = acc / l
            lse = m_s[e] + jnp.log(l)
            q2_ref[e] = _lane_insert(q_ref[e].astype(F32), [-p for p in _split3(lse)], AUG_L).astype(BF16)

    return pl.pallas_call(
        body, name=name,
        grid_spec=_flash_spec(3, 2, H, T // tb, tb, T, (False, True, True),
                              [pltpu.VMEM((2, tb, 1), F32), pltpu.VMEM((2, tb, LANES), F32),
                               pltpu.VMEM((2, tb, LANES), BF16)], n_tables=4),
        out_shape=[jax.ShapeDtypeStruct((H, T, LANES), F32), jax.ShapeDtypeStruct((H, T, LANES), BF16)],
        compiler_params=_params(("parallel", "arbitrary")),
    )(qk, fs, fe, kmax, q_aug, k_aug, v_aug)


def _flash_dq(qk, fs, fe, q2, k_aug, do_aug, v_aug, name):
    H, T, _ = q2.shape
    tb = min(ATT_TB, T)

    def body(qk_ref, fs_ref, fe_ref, q_ref, k_ref, do_ref, v_ref, dq_ref, acc_s):
        h, i = 2 * pl.program_id(0), pl.program_id(1)
        acc_s[...] = jnp.zeros_like(acc_s)

        def step(kb, masked):
            rows = pl.ds(pl.multiple_of(kb * tb, tb), tb)
            for e in range(2):
                k = k_ref[e, rows, :]
                p = jnp.exp(lax.dot_general(q_ref[e], k, NT, preferred_element_type=F32))
                if masked:
                    p = jnp.where(_causal(tb), p, 0.0)
                ds = p * lax.dot_general(do_ref[e], v_ref[e, rows, :], NT, preferred_element_type=F32)
                acc_s[e] += lax.dot_general(ds.astype(BF16), k, NN, preferred_element_type=F32)

        def loop_body(kb, carry):
            step(kb, False)
            return carry

        first = jnp.minimum(_first_live(qk_ref, fs_ref, fe_ref, h, i), _first_live(qk_ref, fs_ref, fe_ref, h + 1, i))
        lax.fori_loop(first, i, loop_body, 0)
        step(i, True)
        dq_ref[...] = acc_s[...]

    return pl.pallas_call(
        body, name=name,
        grid_spec=_flash_spec(4, 1, H, T // tb, tb, T, (False, True, False, True), [pltpu.VMEM((2, tb, LANES), F32)]),
        out_shape=[jax.ShapeDtypeStruct((H, T, LANES), F32)],
        compiler_params=_params(("parallel", "arbitrary")),
    )(qk, fs, fe, q2, k_aug, do_aug, v_aug)[0]


def _flash_dkv(qk, fs, fe, q2, k_aug, do_aug, v_aug, name):
    H, T, _ = q2.shape
    tb = min(ATT_TB, T)
    half = tb // 2
    nb = T // tb

    def body(qk_ref, fs_ref, fe_ref, q_ref, k_ref, do_ref, v_ref, dk_ref, dv_ref, dk_s, dv_s):
        h, i = 2 * pl.program_id(0), pl.program_id(1)
        dk_s[...] = jnp.zeros_like(dk_s)
        dv_s[...] = jnp.zeros_like(dv_s)

        def step(qb, masked):
            base = pl.multiple_of(qb * tb, tb)
            parts = [(slice(0, half), 0, tb), (slice(half, tb), half, half)] if masked else [(slice(0, tb), 0, tb)]
            for e in range(2):
                for krows, q0, nq in parts:
                    queries = pl.ds(pl.multiple_of(base + q0, half), nq)
                    q, dov = q_ref[e, queries, :], do_ref[e, queries, :]
                    pt = jnp.exp(lax.dot_general(k_ref[e, krows, :], q, NT, preferred_element_type=F32))
                    if masked:
                        pt = jnp.where(_seen_by(krows.stop - krows.start, nq), pt, 0.0)
                    dst = pt * lax.dot_general(v_ref[e, krows, :], dov, NT, preferred_element_type=F32)
                    dv_s[e, krows, :] += lax.dot_general(pt.astype(BF16), dov, NN, preferred_element_type=F32)
                    dk_s[e, krows, :] += lax.dot_general(dst.astype(BF16), q, NN, preferred_element_type=F32)

        step(i, True)

        def loop_body(qb, carry):
            step(qb, False)
            return carry

        def live_after(head):
            base = qk_ref[head] - fe_ref[head, i]
            return lax.fori_loop(i + 1, nb, lambda b, n: n + jnp.where(base + fs_ref[head, b] < SKIP_BELOW, 0, 1), 0)

        lax.fori_loop(i + 1, i + 1 + jnp.maximum(live_after(h), live_after(h + 1)), loop_body, 0)
        dk_ref[...] = dk_s[...]
        dv_ref[...] = dv_s[...]

    return pl.pallas_call(
        body, name=name,
        grid_spec=_flash_spec(4, 2, H, nb, tb, T, (True, False, True, False),
                              [pltpu.VMEM((2, tb, LANES), F32), pltpu.VMEM((2, tb, LANES), F32)]),
        out_shape=[jax.ShapeDtypeStruct((H, T, LANES), F32)] * 2,
        compiler_params=_params(("parallel", "arbitrary")),
    )(qk, fs, fe, q2, k_aug, do_aug, v_aug)


def _add_pair(place, slab, got, name):
    _, half, cols = got.shape
    grid_spec = pltpu.PrefetchScalarGridSpec(
        num_scalar_prefetch=1, grid=(N_CHIPS,),
        in_specs=[pl.BlockSpec((None, half, cols), lambda j, s: (j, s[4], 0)),
                  pl.BlockSpec((None, half, cols), lambda j, s: (j, 0, 0))],
        out_specs=pl.BlockSpec((None, half, cols), lambda j, s: (j, 0, 0)))

    def body(s_ref, a_ref, b_ref, o_ref):
        o_ref[...] = (a_ref[...] + b_ref[...]).astype(BF16)

    return pl.pallas_call(
        body, name=name, grid_spec=grid_spec, out_shape=jax.ShapeDtypeStruct(got.shape, BF16),
        compiler_params=_params(("parallel",)),
    )(place, slab, got)


def _add_chips(place, pair, by_chip, name):
    _, half, cols = pair.shape
    tb = min(256, half)
    steps = half // tb
    slot = lambda n: pl.BlockSpec((None, tb, cols), lambda i, s: (s[n], i, 0))
    grid_spec = pltpu.PrefetchScalarGridSpec(
        num_scalar_prefetch=1, grid=(steps,), in_specs=[slot(0), slot(1), slot(2), slot(3)],
        out_specs=pl.BlockSpec((tb, cols), lambda i, s: (s[4] * steps + i, 0)))

    def body(s_ref, own_ref, b1_ref, b2_ref, b3_ref, o_ref):
        o_ref[...] = ((own_ref[...].astype(F32) + b1_ref[...].astype(F32)) + b2_ref[...].astype(F32)) \
            + b3_ref[...].astype(F32)

    return pl.pallas_call(
        body, name=name, grid_spec=grid_spec, out_shape=jax.ShapeDtypeStruct((2 * half, cols), F32),
        compiler_params=_params(("parallel",)),
    )(place, pair, by_chip, by_chip, by_chip)


def _add_slots(b, name, tb=128):
    n, rows, cols = b.shape
    tb = tb if rows % tb == 0 else rows
    specs = [pl.BlockSpec((None, tb, cols), functools.partial(lambda j, i: (j, i, 0), j)) for j in range(n)]

    def body(*refs):
        acc = refs[0][...]
        for r in refs[1:n]:
            acc = acc + r[...]
        refs[n][...] = acc

    return pl.pallas_call(
        body, name=name, grid=(rows // tb,), in_specs=specs, out_specs=pl.BlockSpec((tb, cols), lambda i: (i, 0)),
        out_shape=jax.ShapeDtypeStruct((rows, cols), F32), compiler_params=_params(("parallel",)),
    )(*([b] * n))


def _adamw(w, g, m, v, name):
    rows, cols = w.shape
    tr = 256 if rows % 256 == 0 else rows
    blk = pl.BlockSpec((tr, cols), lambda i: (i, 0))

    def body(w_ref, g_ref, m_ref, v_ref, d_ref, nm_ref, nv_ref):
        gv = g_ref[...]
        nm = ADAM_B1 * m_ref[...] + (1.0 - ADAM_B1) * gv
        nv = ADAM_B2 * v_ref[...] + (1.0 - ADAM_B2) * (gv * gv)
        m_hat = nm / (1.0 - ADAM_B1 ** ADAM_STEP)
        v_hat = nv / (1.0 - ADAM_B2 ** ADAM_STEP)
        d_ref[...] = -ADAM_LR * (m_hat / (jnp.sqrt(v_hat) + ADAM_EPS) + ADAM_WD * w_ref[...])
        nm_ref[...] = nm
        nv_ref[...] = nv

    return pl.pallas_call(
        body, name=name, grid=(rows // tr,), in_specs=[blk] * 4, out_specs=[blk] * 3,
        out_shape=[jax.ShapeDtypeStruct((rows, cols), F32)] * 3, compiler_params=_params(("parallel",)),
    )(w, g, m, v)


ANY = pl.BlockSpec(memory_space=pl.ANY)


def _place():
    x, y, c = lax.axis_index("x"), lax.axis_index("y"), lax.axis_index("c")
    others = [(1 - x, y), (x, 1 - y), (1 - x, 1 - y)]
    return x, y, c, 2 * x + y, others


def _half(ref, rows, h):
    return ref.at[pl.ds(h * (rows // 2), rows // 2), :]


def _gather_sems(n_u):
    return [pltpu.SemaphoreType.DMA((6 * n_u,)), pltpu.SemaphoreType.DMA((6 * n_u,)),
            pltpu.SemaphoreType.DMA((n_u,)), pltpu.SemaphoreType.DMA((n_u,))]


def _gather_copies(units, ins, outs, sems):
    send_sems, recv_sems, own_send_sems, own_recv_sems = sems
    x, y, c, me, others = _place()
    sibling = (x, y, 1 - c)

    def copy(u, k, chip, h, to, src=None):
        dst = _half(outs[u].at[chip], units[u][1][0], h)
        return pltpu.make_async_remote_copy(
            src_ref=dst if src is None else src, dst_ref=dst, send_sem=send_sems.at[6 * u + k],
            recv_sem=recv_sems.at[6 * u + k], device_id=to, device_id_type=MESH)

    pairs = [(u, j, 2 * ox + oy, (ox, oy)) for u in range(len(units)) for j, (ox, oy) in enumerate(others)]
    own = [pltpu.make_async_remote_copy(
        src_ref=ins[u], dst_ref=outs[u].at[me], send_sem=own_send_sems.at[u], recv_sem=own_recv_sems.at[u],
        device_id=sibling, device_id_type=MESH) for u in range(len(units))]
    first = [copy(u, j, me, c, (*to, c), src=_half(ins[u], units[u][1][0], c)) for u, j, _, to in pairs]
    arrivals = [copy(u, j, chip, c, (x, y, c)) for u, j, chip, _ in pairs]
    passed = [copy(u, 3 + j, chip, c, sibling) for u, j, chip, _ in pairs]
    from_sibling = [copy(u, 3 + j, chip, 1 - c, (x, y, c)) for u, j, chip, _ in pairs]
    return own, first, arrivals, passed, from_sibling


def _gather_start(units, ins, outs, sems):
    own, first, _, _, _ = _gather_copies(units, ins, outs, sems)
    for cp in own + first:
        cp.start()


def _gather_finish(units, ins, outs, sems):
    own, first, arrivals, passed, from_sibling = _gather_copies(units, ins, outs, sems)
    for arrived, onward in zip(arrivals, passed):
        arrived.wait_recv()
        onward.start()
    for cp in from_sibling:
        cp.wait_recv()
    for cp in first + passed:
        cp.wait_send()
    for cp in own:
        cp.wait()


def _allgather_weights(units, shards, name):
    n_u = len(units)

    def body(*refs):
        ins, outs, sems = refs[:n_u], refs[n_u:2 * n_u], refs[2 * n_u:]
        _gather_start(units, ins, outs, sems)
        _gather_finish(units, ins, outs, sems)

    return pl.pallas_call(
        body, name=name, in_specs=[ANY] * n_u, out_specs=[ANY] * n_u,
        out_shape=[jax.ShapeDtypeStruct((N_CHIPS,) + shape, BF16) for _, shape in units],
        scratch_shapes=_gather_sems(n_u),
    )(*shards)


def _sibling_copies(units, ins, got, sems):
    send_sems, recv_sems = sems
    x, y, c, _, _ = _place()
    copies = []
    for u, (_, (rows, _), _) in enumerate(units):
        half = rows // 2
        copies.append(pltpu.make_async_remote_copy(
            src_ref=ins[u].at[:, pl.ds((1 - c) * half, half), :], dst_ref=got[u], send_sem=send_sems.at[u],
            recv_sem=recv_sems.at[u], device_id=(x, y, 1 - c), device_id_type=MESH))
    return copies


def _sibling_start(units, ins, got, sems):
    for cp in _sibling_copies(units, ins, got, sems):
        cp.start()


def _sibling_finish(units, ins, got, sems):
    for cp in _sibling_copies(units, ins, got, sems):
        cp.wait()


def _sibling_exchange(units, slabs, name):
    n_u = len(units)

    def body(*refs):
        ins, got, sems = refs[:n_u], refs[n_u:2 * n_u], refs[2 * n_u:]
        _sibling_start(units, ins, got, sems)
        _sibling_finish(units, ins, got, sems)

    return pl.pallas_call(
        body, name=name, in_specs=[ANY] * n_u, out_specs=[ANY] * n_u,
        out_shape=[jax.ShapeDtypeStruct((N_CHIPS, rows // 2, cols), F32) for _, (rows, cols), _ in units],
        scratch_shapes=[pltpu.SemaphoreType.DMA((n_u,)), pltpu.SemaphoreType.DMA((n_u,))],
    )(*slabs)


def _exchange_sems(n_u):
    return [pltpu.SemaphoreType.DMA((3 * n_u,)), pltpu.SemaphoreType.DMA((3 * n_u,))]


def _exchange_copies(ins, outs, sems):
    send_sems, recv_sems = sems
    x, y, c, me, others = _place()
    sends = [pltpu.make_async_remote_copy(
        src_ref=ins[u].at[2 * ox + oy], dst_ref=outs[u].at[me], send_sem=send_sems.at[3 * u + j],
        recv_sem=recv_sems.at[3 * u + j], device_id=(ox, oy, c), device_id_type=MESH)
        for u in range(len(ins)) for j, (ox, oy) in enumerate(others)]
    arrivals = [pltpu.make_async_remote_copy(
        src_ref=ins[u].at[me], dst_ref=outs[u].at[2 * ox + oy], send_sem=send_sems.at[3 * u + j],
        recv_sem=recv_sems.at[3 * u + j], device_id=(x, y, c), device_id_type=MESH)
        for u in range(len(ins)) for j, (ox, oy) in enumerate(others)]
    return sends, arrivals


def _exchange_start(ins, outs, sems):
    for cp in _exchange_copies(ins, outs, sems)[0]:
        cp.start()


def _exchange_finish(ins, outs, sems):
    sends, arrivals = _exchange_copies(ins, outs, sems)
    for cp in arrivals:
        cp.wait_recv()
    for cp in sends:
        cp.wait_send()


def _chip_exchange(pairs, small):
    n_u = len(pairs)

    def body(*refs):
        ins, s_ref = refs[:n_u], refs[n_u]
        outs, sall_ref = refs[n_u + 1:2 * n_u + 1], refs[2 * n_u + 1]
        send_sems, recv_sems, ssend_sems, srecv_sems, local_sem = refs[2 * n_u + 2:]
        x, y, c, _, _ = _place()
        dev = 4 * x + 2 * y + c
        local = pltpu.make_async_copy(s_ref, sall_ref.at[dev], local_sem)
        flips = [(fx, fy, fc) for fx in (0, 1) for fy in (0, 1) for fc in (0, 1)][1:]
        small_sends = [pltpu.make_async_remote_copy(
            src_ref=s_ref, dst_ref=sall_ref.at[dev], send_sem=ssend_sems.at[n], recv_sem=srecv_sems.at[n],
            device_id=(x ^ fx, y ^ fy, c ^ fc), device_id_type=MESH) for n, (fx, fy, fc) in enumerate(flips)]
        local.start()
        _exchange_start(ins, outs, (send_sems, recv_sems))
        for cp in small_sends:
            cp.start()
        _exchange_finish(ins, outs, (send_sems, recv_sems))
        for n, (fx, fy, fc) in enumerate(flips):
            src_dev = 4 * (x ^ fx) + 2 * (y ^ fy) + (c ^ fc)
            pltpu.make_async_remote_copy(
                src_ref=s_ref, dst_ref=sall_ref.at[src_dev], send_sem=ssend_sems.at[n], recv_sem=srecv_sems.at[n],
                device_id=(x, y, c), device_id_type=MESH).wait_recv()
        for cp in small_sends:
            cp.wait_send()
        local.wait()

    outs = pl.pallas_call(
        body, name="grad_chip_exchange", in_specs=[ANY] * (n_u + 1), out_specs=[ANY] * (n_u + 1),
        out_shape=[jax.ShapeDtypeStruct(p.shape, p.dtype) for p in pairs]
        + [jax.ShapeDtypeStruct((8, SMALL_ALL_ROWS, LANES), F32)],
        scratch_shapes=_exchange_sems(n_u) + [pltpu.SemaphoreType.DMA((7,)), pltpu.SemaphoreType.DMA((7,)),
                                              pltpu.SemaphoreType.DMA],
    )(*pairs, small)
    return outs[:n_u], outs[n_u]


def _sibling_share(grads):
    n_u = len(GRAD_UNITS)

    def body(*refs):
        ins, outs = refs[:n_u], refs[n_u:2 * n_u]
        send_sems, recv_sems = refs[2 * n_u:]
        x, y, c, _, _ = _place()

        def rows_of(ref, u, h):
            half = GRAD_UNITS[u][1][0] // 2
            return ref.at[pl.ds(h * half, half), :]

        sends = [pltpu.make_async_remote_copy(
            src_ref=rows_of(ins[u], u, c), dst_ref=rows_of(outs[u], u, c), send_sem=send_sems.at[u],
            recv_sem=recv_sems.at[u], device_id=(x, y, 1 - c), device_id_type=MESH) for u in range(n_u)]
        for cp in sends:
            cp.start()
        for u in range(n_u):
            pltpu.make_async_remote_copy(
                src_ref=rows_of(ins[u], u, c), dst_ref=rows_of(outs[u], u, 1 - c), send_sem=send_sems.at[u],
                recv_sem=recv_sems.at[u], device_id=(x, y, c), device_id_type=MESH).wait_recv()
        for cp in sends:
            cp.wait_send()

    return pl.pallas_call(
        body, name="grad_sibling_share", in_specs=[ANY] * n_u, out_specs=[ANY] * n_u,
        out_shape=[jax.ShapeDtypeStruct(g.shape, F32) for g in grads],
        input_output_aliases={u: u for u in range(n_u)},
        scratch_shapes=[pltpu.SemaphoreType.DMA((n_u,)), pltpu.SemaphoreType.DMA((n_u,))],
    )(*grads)


def kernel(x, g_mix_pre, g_mix_post, g_ffn_pre, g_ffn_post, conv_pw1_w, conv_pw1_b, conv_dw_w, conv_dw_b, conv_ln_g, conv_ln_b, conv_pw2_w, conv_pw2_b, attn_w_in, attn_b_f, attn_w_o, mlp_w_up, mlp_w_down, loss_target, m_g_mix_pre, m_g_mix_post, m_g_ffn_pre, m_g_ffn_post, m_conv_pw1_w, m_conv_pw1_b, m_conv_dw_w, m_conv_dw_b, m_conv_ln_g, m_conv_ln_b, m_conv_pw2_w, m_conv_pw2_b, m_attn_w_in, m_attn_b_f, m_attn_w_o, m_mlp_w_up, m_mlp_w_down, v_g_mix_pre, v_g_mix_post, v_g_ffn_pre, v_g_ffn_post, v_conv_pw1_w, v_conv_pw1_b, v_conv_dw_w, v_conv_dw_b, v_conv_ln_g, v_conv_ln_b, v_conv_pw2_w, v_conv_pw2_b, v_attn_w_in, v_attn_b_f, v_attn_w_o, v_mlp_w_up, v_mlp_w_down):
    weights = dict(g_mix_pre=g_mix_pre, g_mix_post=g_mix_post, g_ffn_pre=g_ffn_pre, g_ffn_post=g_ffn_post, conv_pw1_w=conv_pw1_w, conv_pw1_b=conv_pw1_b, conv_dw_w=conv_dw_w, conv_dw_b=conv_dw_b, conv_ln_g=conv_ln_g, conv_ln_b=conv_ln_b, conv_pw2_w=conv_pw2_w, conv_pw2_b=conv_pw2_b, attn_w_in=attn_w_in, attn_b_f=attn_b_f, attn_w_o=attn_w_o, mlp_w_up=mlp_w_up, mlp_w_down=mlp_w_down)
    mom_m = dict(g_mix_pre=m_g_mix_pre, g_mix_post=m_g_mix_post, g_ffn_pre=m_g_ffn_pre, g_ffn_post=m_g_ffn_post, conv_pw1_w=m_conv_pw1_w, conv_pw1_b=m_conv_pw1_b, conv_dw_w=m_conv_dw_w, conv_dw_b=m_conv_dw_b, conv_ln_g=m_conv_ln_g, conv_ln_b=m_conv_ln_b, conv_pw2_w=m_conv_pw2_w, conv_pw2_b=m_conv_pw2_b, attn_w_in=m_attn_w_in, attn_b_f=m_attn_b_f, attn_w_o=m_attn_w_o, mlp_w_up=m_mlp_w_up, mlp_w_down=m_mlp_w_down)
    mom_v = dict(g_mix_pre=v_g_mix_pre, g_mix_post=v_g_mix_post, g_ffn_pre=v_g_ffn_pre, g_ffn_post=v_g_ffn_post, conv_pw1_w=v_conv_pw1_w, conv_pw1_b=v_conv_pw1_b, conv_dw_w=v_conv_dw_w, conv_dw_b=v_conv_dw_b, conv_ln_g=v_conv_ln_g, conv_ln_b=v_conv_ln_b, conv_pw2_w=v_conv_pw2_w, conv_pw2_b=v_conv_pw2_b, attn_w_in=v_attn_w_in, attn_b_f=v_attn_b_f, attn_w_o=v_attn_w_o, mlp_w_up=v_mlp_w_up, mlp_w_down=v_mlp_w_down)
    order = [n for n, _ in SMALL[:4]] + ["conv_pw1_w", "conv_pw1_b", "conv_dw_w", "conv_dw_b", "conv_ln_g", "conv_ln_b",
                                          "conv_pw2_w", "conv_pw2_b", "attn_w_in", "attn_b_f", "attn_w_o", "mlp_w_up",
                                          "mlp_w_down"]
    T = x.shape[1]
    x0 = x.reshape(T, D)
    target = loss_target.reshape(T, D)
    row = lambda a, l: a[l:l + 1, :]

    def shard(unit):
        if unit[:-1] in ("up", "down"):
            return weights["mlp_w_" + unit[:-1]][int(unit[-1])].astype(BF16)
        w = weights[unit][0].astype(BF16)
        return jnp.pad(w, ((0, HALO - CONV_W), (0, 0))) if unit == "conv_dw_w" else w

    by_cols = lambda g: g.transpose(1, 0, 2).reshape(g.shape[1], N_CHIPS * g.shape[2])
    by_rows = lambda g: g.reshape(N_CHIPS * g.shape[1], g.shape[2])
    g_pw1, g_dw, g_pw2 = _allgather_weights(GATHER_AT_START, [shard(n) for n, _ in GATHER_AT_START], "allgather_conv")
    w_pw1, w_pw2, w_dw = by_cols(g_pw1), by_rows(g_pw2), by_cols(g_dw).astype(F32)
    b_f = jnp.pad(attn_b_f, ((0, 0), (0, LANES - N_HEADS)))

    def mlp_fwd(h, l):
        (act,) = _matmul(h, w_up[l], "nn", (BF16,), _epi_sqrelu, f"mlp{l}_up")
        (mo,) = _matmul(act, w_down[l], "nn", (F32,), _epi_plain, f"mlp{l}_down", tk=DFF)
        return act, mo

    (h0,) = _resnorm_fwd(x0, None, None, row(g_mix_pre, 0), "norm_in")
    (a0,) = _matmul(h0, w_pw1, "nn", (F32,), _epi_bias, "conv_pw1", extras=((conv_pw1_b, "row"),))
    y0, z0, (g_up0, g_down0, g_win, g_wo) = _conv_fwd(
        a0, w_dw, conv_dw_b, conv_ln_g, conv_ln_b, "conv_fwd", GATHER_BEHIND_CONV,
        [shard(n) for n, _ in GATHER_BEHIND_CONV])
    w_up, w_down = [by_cols(g_up0)], [by_rows(g_down0)]
    w_o = by_rows(g_wo)
    w_in = jnp.pad(by_cols(g_win), ((0, 0), (0, W_IN_PAD - W_IN_COLS)))
    (m0,) = _matmul(z0, w_pw2, "nn", (F32,), _epi_bias, "conv_pw2", extras=((conv_pw2_b, "row"),))
    x1, h1 = _resnorm_fwd(x0, m0, row(g_mix_post, 0), row(g_ffn_pre, 0), "norm_conv_out")
    act0, mo0 = mlp_fwd(h1, 0)
    x2, h2 = _resnorm_fwd(x1, mo0, row(g_ffn_post, 0), row(g_mix_pre, 1), "norm_mlp0_out")

    (proj,) = _matmul(h2, w_in[:, :3 * D], "nn", (BF16,), _epi_plain, "attn_in")
    (fl,) = _matmul(h2, w_in[:, 3 * D:], "nn", (F32,), _epi_plain, "attn_gate_in")
    f_all, f1, f2, f3 = _gate_fwd(fl, b_f, "gate_fwd")
    (q_aug, k_aug, v_aug, qn2, kn2), (g_up1, g_down1) = _attn_pack(
        proj, f1, f2, f3, "attn_pack", GATHER_BEHIND_PACK, [shard(n) for n, _ in GATHER_BEHIND_PACK])
    w_up.append(by_cols(g_up1))
    w_down.append(by_rows(g_down1))
    tb = min(ATT_TB, T)
    qk = 2.02 * jnp.sqrt(qn2[:, 0, 0] * kn2[:, 0, 0]) + 1.0
    fs, fe = f_all[0::tb, :N_HEADS].T, f_all[tb - 1::tb, :N_HEADS].T
    o_aug, q2_aug = _flash_fwd(qk, fs, fe, jnp.sqrt(kn2[:, 0, 0]), q_aug, k_aug, v_aug, "flash_fwd")
    (o_bf,) = _merge_heads(o_aug, 1.0, BF16, "attn_merge_o")
    (m1,) = _matmul(o_bf, w_o, "nn", (F32,), _epi_plain, "attn_out")
    x3, h3 = _resnorm_fwd(x2, m1, row(g_mix_post, 1), row(g_ffn_pre, 1), "norm_attn_out")
    act1, mo1 = mlp_fwd(h3, 1)
    dy, loss_part = _resnorm_loss(x3, mo1, row(g_ffn_post, 1), target, "loss")
    loss = lax.psum(loss_part[0, 0], ("x", "y", "c"))

    G = {}
    chip = 2 * lax.axis_index("x") + lax.axis_index("y")
    place = jnp.stack([chip] + [k + (k >= chip).astype(jnp.int32) for k in range(N_CHIPS - 1)]
                      + [lax.axis_index("c")]).astype(jnp.int32)

    def mlp_bwd(d_out, mo, act, h, x_in, l):
        d_mo, dg_post, _ = _norm_bwd(mo, row(g_ffn_post, l), d_out, None, BF16, f"mlp{l}_post_bwd")
        (d_up,) = _matmul(d_mo, w_down[l], "nt", (BF16,), _epi_dsqrelu, f"mlp{l}_dact", extras=((act, "tile"),))
        (dw_down,) = _matmul(act, d_mo, "tn", (F32,), _epi_plain, tk=2048, name=f"mlp{l}_dwdown")
        (dw_up,) = _matmul(h, d_up, "tn", (F32,), _epi_plain, tk=2048, name=f"mlp{l}_dwup", shard_out=True)
        (d_h,) = _matmul(d_up, w_up[l], "nt", (F32,), _epi_plain, f"mlp{l}_dh", tk=DFF)
        d_in, dg_pre, _ = _norm_bwd(x_in, row(g_ffn_pre, l), d_h, d_out, F32, f"mlp{l}_pre_bwd")
        G[f"up{l}"], G[f"down{l}"] = dw_up, dw_down.reshape(N_CHIPS, D, D)
        return d_in, dg_post, dg_pre

    d_x3, dg_ffn_post1, dg_ffn_pre1 = mlp_bwd(dy, mo1, act1, h3, x3, 1)

    d_m1, dg_mix_post1, _ = _norm_bwd(m1, row(g_mix_post, 1), d_x3, None, BF16, "attn_post_bwd")
    (dw_o,) = _matmul(o_bf, d_m1, "tn", (F32,), _epi_plain, tk=2048, name="attn_dwo")
    G["attn_w_o"] = dw_o.reshape(N_CHIPS, D // N_CHIPS, D)
    (d_o,) = _matmul(d_m1, w_o, "nt", (F32,), _epi_plain, "attn_do")
    do_aug = _attn_prep(d_o, o_aug, "attn_prep")
    dq_aug = _flash_dq(qk, fs, fe, q2_aug, k_aug, do_aug, v_aug, "flash_dq")
    dk_aug, dv_aug = _flash_dkv(qk, fs, fe, q2_aug, k_aug, do_aug, v_aug, "flash_dkv")
    d_q, dF_q = _merge_heads(dq_aug, 0.125, BF16, "attn_merge_dq", column=AUG_F)
    d_k, dF_k = _merge_heads(dk_aug, 1.0, BF16, "attn_merge_dk", column=AUG_ONE)
    (d_v,) = _merge_heads(dv_aug, 1.0, BF16, "attn_merge_dv")
    d_fl, db_f = _gate_bwd(dF_q, dF_k, fl, b_f, "gate_bwd")
    d_proj = jnp.concatenate([d_q, d_k, d_v, d_fl.astype(BF16)], axis=1)
    (dw_in,) = _matmul(h2, d_proj, "tn", (F32,), _epi_plain, tk=2048, name="attn_dwin", tn=640)
    G["attn_w_in"] = dw_in[:, :W_IN_COLS].reshape(D, N_CHIPS, W_IN_COLS // N_CHIPS).transpose(1, 0, 2)
    (d_h2,) = _matmul(d_proj, w_in, "nt", (F32,), _epi_plain, "attn_dh", tk=W_IN_PAD)
    d_x2, dg_mix_pre1, _ = _norm_bwd(x2, row(g_mix_pre, 1), d_h2, d_x3, F32, "attn_pre_bwd")

    d_x1, dg_ffn_post0, dg_ffn_pre0 = mlp_bwd(d_x2, mo0, act0, h1, x1, 0)

    d_m0, dg_mix_post0, db_pw2 = _norm_bwd(m0, row(g_mix_post, 0), d_x1, None, BF16, "conv_post_bwd")
    (dw_pw2,) = _matmul(z0, d_m0, "tn", (F32,), _epi_plain, tk=2048, name="conv_dwpw2")
    G["conv_pw2_w"] = dw_pw2.reshape(N_CHIPS, D // N_CHIPS, D)
    (d_z0,) = _matmul(d_m0, w_pw2, "nt", (F32,), _epi_plain, "conv_dz")
    conv_units = [u for u in GRAD_UNITS if u[0].startswith("conv")]
    attn_units = [u for u in GRAD_UNITS if u not in conv_units]
    attn_slabs = [G[n] for n, _, _ in attn_units]
    d_y0, dln_g, dln_b, ddw_b, attn_got = _conv_bwd_ln(d_z0, y0, conv_ln_g, conv_ln_b, "conv_bwd_ln", attn_units,
                                                       attn_slabs)
    pairs = {n: _add_pair(place, s, g, f"grad_pair_{n}") for s, g, (n, _, _) in zip(attn_slabs, attn_got, attn_units)}
    d_a0, db_pw1, ddw_w, attn_by_chip = _conv_bwd_dw(d_y0, a0, w_dw, "conv_bwd_dw",
                                                    [pairs[n] for n, _, _ in attn_units])
    by_chip = {n: b for b, (n, _, _) in zip(attn_by_chip, attn_units)}
    (G["conv_pw1_w"],) = _matmul(h0, d_a0, "tn", (F32,), _epi_plain, tk=2048, name="conv_dwpw1", tn=512, shard_out=True)
    (d_h0,) = _matmul(d_a0, w_pw1, "nt", (F32,), _epi_plain, "conv_dh", tk=2 * D)
    d_x0, dg_mix_pre0, _ = _norm_bwd(x0, row(g_mix_pre, 0), d_h0, d_x1, F32, "conv_pre_bwd")

    small_local = {
        "g_mix_pre": jnp.concatenate([dg_mix_pre0, dg_mix_pre1]), "g_mix_post": jnp.concatenate([dg_mix_post0, dg_mix_post1]),
        "g_ffn_pre": jnp.concatenate([dg_ffn_pre0, dg_ffn_pre1]), "g_ffn_post": jnp.concatenate([dg_ffn_post0, dg_ffn_post1]),
        "conv_pw1_b": db_pw1, "conv_dw_b": ddw_b, "conv_ln_g": dln_g, "conv_ln_b": dln_b, "conv_pw2_b": db_pw2,
        "attn_b_f": db_f[:, :N_HEADS], "conv_dw_w_full": ddw_w,
    }

    conv_slabs = [G[n] for n, _, _ in conv_units]
    conv_got = _sibling_exchange(conv_units, conv_slabs, "grad_sibling_exchange_conv")
    pairs.update({n: _add_pair(place, s, g, f"grad_pair_{n}")
                  for s, g, (n, _, _) in zip(conv_slabs, conv_got, conv_units)})
    conv_by_chip, small_all = _chip_exchange([pairs[n] for n, _, _ in conv_units],
                                             _pack([small_local[n] for n, _ in SMALL_ALL], SMALL_ALL_ROWS))
    by_chip.update({n: b for b, (n, _, _) in zip(conv_by_chip, conv_units)})
    unit_grads = _sibling_share([_add_chips(place, pairs[n], by_chip[n], f"grad_chips_{n}") for n, _, _ in GRAD_UNITS])
    grads = _unpack(_add_slots(small_all, "grad_add_small", tb=SMALL_ALL_ROWS), SMALL_ALL)
    grads["conv_dw_w"] = lax.dynamic_slice(grads.pop("conv_dw_w_full"), (0, chip * 256), (CONV_W, 256))[None]

    delta, new_m, new_v = {}, {}, {}
    per_layer = {}
    for g, (unit, shape, (name, layer)) in zip(unit_grads, GRAD_UNITS):
        view = lambda d: (d[name] if layer is None else d[name][layer]).reshape(shape)
        per_layer.setdefault(name, []).append((g,) + tuple(_adamw(view(weights), g, view(mom_m), view(mom_v),
                                                                  f"adamw_{unit}")))
    for name, parts in per_layer.items():
        full = weights[name].shape
        join = lambda n: (parts[0][n] if len(parts) == 1 else jnp.stack([p[n] for p in parts])).reshape(full)
        grads[name], delta[name], new_m[name], new_v[name] = join(0), join(1), join(2), join(3)
    dw2 = lambda d: d["conv_dw_w"].reshape(CONV_W, 256)
    d, nm, nv = _adamw(dw2(weights), dw2(grads), dw2(mom_m), dw2(mom_v), "adamw_conv_dw_w")
    full = weights["conv_dw_w"].shape
    delta["conv_dw_w"], new_m["conv_dw_w"], new_v["conv_dw_w"] = d.reshape(full), nm.reshape(full), nv.reshape(full)
    packs = [_pack([src[n] for n, _ in SMALL], SMALL_ROWS) for src in (weights, grads, mom_m, mom_v)]
    d, nm, nv = _adamw(*packs, "adamw_small")
    for dst, packed in ((delta, d), (new_m, nm), (new_v, nv)):
        dst.update(_unpack(packed, SMALL))

    grad_x = d_x0.reshape(x.shape)
    return (loss, grad_x, *[grads[n] for n in order], *[delta[n] for n in order], *[new_m[n] for n in order],
            *[new_v[n] for n in order])
```

```python
import functools

import jax
import jax.numpy as jnp
from jax import lax
from jax.experimental import pallas as pl
from jax.experimental.pallas import tpu as pltpu

F32 = jnp.float32
BF16 = jnp.bfloat16
MESH = pl.DeviceIdType.MESH

D = 1024
DFF = 4096
N_HEADS = 16
HEAD_DIM = 64
CONV_W = 31
HALO = 32
RMS_EPS = 1e-6
LN_EPS = 1e-5
MASK_VALUE = -1e30
W_IN_COLS = 3 * D + N_HEADS
W_IN_PAD = 3200
LANES = 128
SUBLANES = 8
N_CHIPS = 4
VMEM_LIMIT = 56 * 1024 * 1024
MATMUL_CHUNK = 256

ADAM_LR = 0.001
ADAM_B1 = 0.9
ADAM_B2 = 0.999
ADAM_EPS = 1e-08
ADAM_WD = 0.01
ADAM_STEP = 10

AUG_F = 64
AUG_ONE = 67
AUG_L = 70
SKIP_BELOW = -104.0
FIXED_MAX_BELOW = 40.0

GATHER_AT_START = (("conv_pw1_w", (1024, 512)), ("conv_dw_w", (HALO, 256)), ("conv_pw2_w", (256, 1024)))
GATHER_BEHIND_CONV = (("up0", (1024, 1024)), ("down0", (1024, 1024)), ("attn_w_in", (1024, 772)),
                      ("attn_w_o", (256, 1024)))
GATHER_BEHIND_MLP = ((("up1", (1024, 1024)),), (("down1", (1024, 1024)),))
GRAD_UNITS = (
    ("conv_pw1_w", (1024, 512), ("conv_pw1_w", None)),
    ("conv_pw2_w", (256, 1024), ("conv_pw2_w", None)),
    ("attn_w_in", (1024, 772), ("attn_w_in", None)),
    ("attn_w_o", (256, 1024), ("attn_w_o", None)),
    ("up0", (1024, 1024), ("mlp_w_up", 0)),
    ("up1", (1024, 1024), ("mlp_w_up", 1)),
    ("down0", (1024, 1024), ("mlp_w_down", 0)),
    ("down1", (1024, 1024), ("mlp_w_down", 1)),
)
SMALL = (
    ("g_mix_pre", (2, 1024)), ("g_mix_post", (2, 1024)), ("g_ffn_pre", (2, 1024)), ("g_ffn_post", (2, 1024)),
    ("conv_pw1_b", (1, 2048)), ("conv_dw_b", (1, 1024)), ("conv_ln_g", (1, 1024)), ("conv_ln_b", (1, 1024)),
    ("conv_pw2_b", (1, 1024)), ("attn_b_f", (1, 16)),
)
SMALL_ROWS = 120
SMALL_ALL = SMALL + (("conv_dw_w_full", (HALO, 1024)),)
SMALL_ALL_ROWS = 376


def _size(shape):
    n = 1
    for s in shape:
        n *= s
    return n


def _pack(arrays, rows):
    flat = jnp.concatenate([a.reshape(-1) for a in arrays])
    return jnp.pad(flat, (0, rows * LANES - flat.shape[0])).reshape(rows, LANES)


def _unpack(packed, table):
    flat = packed.reshape(-1)
    out, off = {}, 0
    for name, shape in table:
        n = _size(shape)
        out[name] = flat[off:off + n].reshape(shape)
        off += n
    return out


def _params(sem):
    return pltpu.CompilerParams(dimension_semantics=sem, vmem_limit_bytes=VMEM_LIMIT)


def _rms(x, g):
    return x * lax.rsqrt(jnp.mean(x * x, axis=-1, keepdims=True) + RMS_EPS) * g


def _split3(v):
    p1 = v.astype(BF16).astype(F32)
    r = v - p1
    p2 = r.astype(BF16).astype(F32)
    p3 = (r - p2).astype(BF16).astype(F32)
    return p1, p2, p3


def _matmul(a, b, mode, out_dtypes, epi, name, extras=(), tm=1024, tn=1024, tk=1024, shard_out=False,
            gather_units=(), gather_shards=()):
    (K, M) = a.shape if mode == "tn" else a.shape[::-1]
    N = b.shape[0] if mode == "nt" else b.shape[1]
    tm, tn, tk = min(tm, M), min(tn, N), min(tk, K)
    nk = K // tk
    if mode == "tn":
        a_spec = pl.BlockSpec((tk, tm), lambda i, j, k: (k, i))
    else:
        a_spec = pl.BlockSpec((tm, tk), lambda i, j, k: (i, k))
    if mode == "nt":
        b_spec = pl.BlockSpec((tn, tk), lambda i, j, k: (j, k))
    else:
        b_spec = pl.BlockSpec((tk, tn), lambda i, j, k: (k, j))
    if shard_out:
        per = N // N_CHIPS // tn
        o_spec = pl.BlockSpec((None, tm, tn), lambda i, j, k: (j // per, i, j % per))
        o_shape = (N_CHIPS, M, N // N_CHIPS)
    else:
        o_spec = pl.BlockSpec((tm, tn), lambda i, j, k: (i, j))
        o_shape = (M, N)
    dims = {"nn": (((1,), (0,)), ((), ())), "nt": (((1,), (1,)), ((), ())), "tn": (((0,), (0,)), ((), ()))}[mode]
    ex_specs = []
    for _, kind in extras:
        if kind == "row":
            ex_specs.append(pl.BlockSpec((1, tn), lambda i, j, k: (0, j)))
        else:
            ex_specs.append(pl.BlockSpec((tm, tn), lambda i, j, k: (i, j)))
    n_ex, n_out = len(extras), len(out_dtypes)

    cw = MATMUL_CHUNK if tn % MATMUL_CHUNK == 0 else tn

    n_g = len(gather_units)
    grid = (M // tm, N // tn, nk)

    def body(*refs):
        a_ref, b_ref = refs[0], refs[1]
        ex = refs[2:2 + n_ex]
        g_ins = refs[2 + n_ex:2 + n_ex + n_g]
        outs = refs[2 + n_ex + n_g:2 + n_ex + n_g + n_out]
        g_outs = refs[2 + n_ex + n_g + n_out:2 + n_ex + 2 * n_g + n_out]
        scratch = refs[2 + n_ex + 2 * n_g + n_out:]
        if n_g:
            step = [pl.program_id(d) for d in range(3)]
            at = lambda where: functools.reduce(jnp.logical_and, [s == w for s, w in zip(step, where)])
            pl.when(at((0, 0, 0)))(lambda: _gather_start(gather_units, g_ins, g_outs, scratch[-4:]))

        def for_chunks(use):
            for c in range(tn // cw):
                cols = slice(c * cw, (c + 1) * cw)
                b_chunk = b_ref[cols, :] if mode == "nt" else b_ref[:, cols]
                use(cols, lax.dot_general(a_ref[...], b_chunk, dims, preferred_element_type=F32))

        def finish(cols, acc):
            res = epi(acc, *[e[:, cols] for e in ex])
            for o, r in zip(outs, res):
                o[:, cols] = r.astype(o.dtype)

        if nk == 1:
            for_chunks(finish)
        else:
            acc_ref = scratch[0]
            k = pl.program_id(2)

            def first(cols, prod):
                acc_ref[:, cols] = prod

            def middle(cols, prod):
                acc_ref[:, cols] += prod

            pl.when(k == 0)(lambda: for_chunks(first))
            pl.when(jnp.logical_and(k > 0, k < nk - 1))(lambda: for_chunks(middle))
            pl.when(k == nk - 1)(lambda: for_chunks(lambda cols, prod: finish(cols, acc_ref[:, cols] + prod)))
        if n_g:
            pl.when(at([g - 1 for g in grid]))(lambda: _gather_finish(gather_units, g_ins, g_outs, scratch[-4:]))

    outs = pl.pallas_call(
        body,
        name=name,
        grid=grid,
        in_specs=[a_spec, b_spec] + ex_specs + [ANY] * n_g,
        out_specs=[o_spec for _ in out_dtypes] + [ANY] * n_g,
        out_shape=[jax.ShapeDtypeStruct(o_shape, dt) for dt in out_dtypes]
        + [jax.ShapeDtypeStruct((N_CHIPS,) + shape, BF16) for _, shape in gather_units],
        scratch_shapes=([pltpu.VMEM((tm, tn), F32)] if nk > 1 else []) + (_gather_sems(n_g) if n_g else []),
        compiler_params=_params(("arbitrary",) * 3 if n_g else ("parallel", "parallel", "arbitrary")),
    )(a, b, *[e for e, _ in extras], *gather_shards)
    return (outs[:n_out], outs[n_out:]) if n_g else outs


def _epi_plain(acc):
    return (acc,)


def _epi_bias(acc, bias):
    return (acc + bias,)


def _epi_sqrelu(acc):
    r = jnp.maximum(acc, 0.0)
    return (r * r,)


def _epi_dsqrelu(acc, act):
    a = act.astype(F32)
    return (acc * jnp.where(a > 0.0, 2.0 * a * lax.rsqrt(a), 0.0),)


def _resnorm_fwd(x, m, g_post, g_next, name, tr=512):
    T = x.shape[0]
    has_m = m is not None
    row = pl.BlockSpec((tr, D), lambda i: (i, 0))
    vec = pl.BlockSpec((1, D), lambda i: (0, 0))

    def body(*refs):
        if has_m:
            x_ref, m_ref, gp_ref, gn_ref, x1_ref, h_ref = refs
            x1 = x_ref[...] + _rms(m_ref[...], gp_ref[...])
            x1_ref[...] = x1
        else:
            x_ref, gn_ref, h_ref = refs
            x1 = x_ref[...]
        h_ref[...] = _rms(x1, gn_ref[...]).astype(BF16)

    if has_m:
        args, in_specs = (x, m, g_post, g_next), [row, row, vec, vec]
        out_specs = [row, row]
        out_shape = [jax.ShapeDtypeStruct((T, D), F32), jax.ShapeDtypeStruct((T, D), BF16)]
    else:
        args, in_specs = (x, g_next), [row, vec]
        out_specs = [row]
        out_shape = [jax.ShapeDtypeStruct((T, D), BF16)]
    return pl.pallas_call(
        body, name=name, grid=(T // tr,), in_specs=in_specs, out_specs=out_specs, out_shape=out_shape,
        compiler_params=_params(("parallel",)),
    )(*args)


def _resnorm_loss(x, m, g_post, target, name, tr=512):
    T = x.shape[0]
    row = pl.BlockSpec((tr, D), lambda i: (i, 0))
    vec = pl.BlockSpec((1, D), lambda i: (0, 0))

    def body(x_ref, m_ref, gp_ref, t_ref, dy_ref, loss_ref):
        diff = x_ref[...] + _rms(m_ref[...], gp_ref[...]) - t_ref[...]
        dy_ref[...] = diff * (1.0 / D)

        @pl.when(pl.program_id(0) == 0)
        def _():
            loss_ref[...] = jnp.zeros_like(loss_ref)

        per_token = jnp.mean(diff * diff, axis=-1, keepdims=True)
        loss_ref[...] += 0.5 * jnp.sum(per_token)

    return pl.pallas_call(
        body, name=name, grid=(T // tr,), in_specs=[row, row, vec, row],
        out_specs=[row, pl.BlockSpec((8, LANES), lambda i: (0, 0))],
        out_shape=[jax.ShapeDtypeStruct((T, D), F32), jax.ShapeDtypeStruct((8, LANES), F32)],
        compiler_params=_params(("arbitrary",)),
    )(x, m, g_post, target)


def _norm_bwd(x, g, dy, resid, out_dtype, name, tr=512):
    T = x.shape[0]
    has_r = resid is not None
    row = pl.BlockSpec((tr, D), lambda i: (i, 0))
    vec = pl.BlockSpec((1, D), lambda i: (0, 0))

    def body(*refs):
        if has_r:
            x_ref, g_ref, dy_ref, r_ref, dx_ref, dg_ref, cs_ref = refs
        else:
            x_ref, g_ref, dy_ref, dx_ref, dg_ref, cs_ref = refs
        xv, dyv = x_ref[...], dy_ref[...]
        r = lax.rsqrt(jnp.mean(xv * xv, axis=-1, keepdims=True) + RMS_EPS)
        gy = dyv * g_ref[...]
        c = jnp.sum(gy * xv, axis=-1, keepdims=True) * (1.0 / D)
        dx = r * gy - xv * (r * r * r * c)

        @pl.when(pl.program_id(0) == 0)
        def _():
            dg_ref[...] = jnp.zeros_like(dg_ref)
            cs_ref[...] = jnp.zeros_like(cs_ref)

        dg_ref[...] += jnp.sum(dyv * (xv * r), axis=0, keepdims=True)
        cs_ref[...] += jnp.sum(dx, axis=0, keepdims=True)
        if has_r:
            dx = dx + r_ref[...]
        dx_ref[...] = dx.astype(dx_ref.dtype)

    args = (x, g, dy) + ((resid,) if has_r else ())
    return pl.pallas_call(
        body, name=name, grid=(T // tr,), in_specs=[row, vec, row] + ([row] if has_r else []),
        out_specs=[row, vec, vec],
        out_shape=[jax.ShapeDtypeStruct((T, D), out_dtype), jax.ShapeDtypeStruct((1, D), F32),
                   jax.ShapeDtypeStruct((1, D), F32)],
        compiler_params=_params(("arbitrary",)),
    )(*args)


CONV_TB = 256
CONV_RC = 32


def _glu(a):
    return a[:, :D] * jax.nn.sigmoid(a[:, D:])


def _shifted_copies(win):
    rows = win.shape[1]
    for p in range(1, SUBLANES):
        win[p, 0:rows - SUBLANES, :] = win[0, p:p + rows - SUBLANES, :]


def _window(win, start):
    p = start % SUBLANES
    return win[p, start - p:start - p + CONV_RC, :]


def _conv_fwd(a, dw_w, dw_b, ln_g, ln_b, name, gather_units, gather_shards):
    T = a.shape[0]
    tb = min(CONV_TB, T)
    per = tb // HALO
    n_steps = T // tb
    n_g = len(gather_units)
    vec = pl.BlockSpec((1, D), lambda i: (0, 0))

    def body(*refs):
        cur_ref, prev_ref, w_ref, b_ref, lg_ref, lb_ref = refs[:6]
        g_ins = refs[6:6 + n_g]
        y_ref, z_ref = refs[6 + n_g:8 + n_g]
        g_outs = refs[8 + n_g:8 + 2 * n_g]
        uwin = refs[8 + 2 * n_g]
        sems = refs[9 + 2 * n_g:]
        i = pl.program_id(0)
        pl.when(i == 0)(lambda: _gather_start(gather_units, g_ins, g_outs, sems))
        uwin[0, 0:HALO, :] = jnp.where(i > 0, _glu(prev_ref[...]), 0.0)
        uwin[0, HALO:, :] = _glu(cur_ref[...])
        _shifted_copies(uwin)
        for r in range(tb // CONV_RC):
            acc = jnp.broadcast_to(b_ref[...], (CONV_RC, D))
            for k in range(CONV_W):
                start = r * CONV_RC + HALO - (CONV_W - 1) + k
                acc = acc + _window(uwin, start) * w_ref[k:k + 1, :]
            rows = slice(r * CONV_RC, (r + 1) * CONV_RC)
            y_ref[rows, :] = acc
            mu = jnp.mean(acc, axis=-1, keepdims=True)
            xc = acc - mu
            var = jnp.mean(xc * xc, axis=-1, keepdims=True)
            n = xc * lax.rsqrt(var + LN_EPS) * lg_ref[...] + lb_ref[...]
            z_ref[rows, :] = (n * jax.nn.sigmoid(n)).astype(BF16)
        pl.when(i == n_steps - 1)(lambda: _gather_finish(gather_units, g_ins, g_outs, sems))

    outs = pl.pallas_call(
        body, name=name, grid=(n_steps,),
        in_specs=[pl.BlockSpec((tb, 2 * D), lambda i: (i, 0)),
                  pl.BlockSpec((HALO, 2 * D), lambda i: (jnp.maximum(i * per - 1, 0), 0)),
                  pl.BlockSpec((HALO, D), lambda i: (0, 0)), vec, vec, vec] + [ANY] * n_g,
        out_specs=[pl.BlockSpec((tb, D), lambda i: (i, 0)), pl.BlockSpec((tb, D), lambda i: (i, 0))] + [ANY] * n_g,
        out_shape=[jax.ShapeDtypeStruct((T, D), F32), jax.ShapeDtypeStruct((T, D), BF16)]
        + [jax.ShapeDtypeStruct((N_CHIPS,) + shape, BF16) for _, shape in gather_units],
        scratch_shapes=[pltpu.VMEM((SUBLANES, tb + HALO, D), F32)] + _gather_sems(n_g),
        compiler_params=_params(("arbitrary",)),
    )(a, a, dw_w, dw_b, ln_g, ln_b, *gather_shards)
    return outs[0], outs[1], outs[2:]


def _conv_bwd_ln(dz, y, ln_g, ln_b, name, exchange_units, slabs, tr=256):
    T = y.shape[0]
    tr = min(tr, T)
    n_steps = T // tr
    n_s = len(slabs)
    row = pl.BlockSpec((tr, D), lambda i: (i, 0))
    vec = pl.BlockSpec((1, D), lambda i: (0, 0))

    def body(*refs):
        dz_ref, y_ref, lg_ref, lb_ref = refs[:4]
        s_ins = refs[4:4 + n_s]
        dy_ref, dlg_ref, dlb_ref, db_ref = refs[4 + n_s:8 + n_s]
        s_outs = refs[8 + n_s:8 + 2 * n_s]
        sems = refs[8 + 2 * n_s:]
        pl.when(pl.program_id(0) == 0)(lambda: _sibling_start(exchange_units, s_ins, s_outs, sems))
        yv = y_ref[...]
        mu = jnp.mean(yv, axis=-1, keepdims=True)
        xc = yv - mu
        rstd = lax.rsqrt(jnp.mean(xc * xc, axis=-1, keepdims=True) + LN_EPS)
        yh = xc * rstd
        n = yh * lg_ref[...] + lb_ref[...]
        sg = jax.nn.sigmoid(n)
        dn = dz_ref[...] * (sg * (1.0 + n * (1.0 - sg)))
        dyh = dn * lg_ref[...]
        dyv = rstd * (dyh - jnp.mean(dyh, axis=-1, keepdims=True) - yh * jnp.mean(dyh * yh, axis=-1, keepdims=True))
        dy_ref[...] = dyv

        @pl.when(pl.program_id(0) == 0)
        def _():
            dlg_ref[...] = jnp.zeros_like(dlg_ref)
            dlb_ref[...] = jnp.zeros_like(dlb_ref)
            db_ref[...] = jnp.zeros_like(db_ref)

        dlg_ref[...] += jnp.sum(dn * yh, axis=0, keepdims=True)
        dlb_ref[...] += jnp.sum(dn, axis=0, keepdims=True)
        db_ref[...] += jnp.sum(dyv, axis=0, keepdims=True)
        pl.when(pl.program_id(0) == n_steps - 1)(lambda: _sibling_finish(exchange_units, s_ins, s_outs, sems))

    outs = pl.pallas_call(
        body, name=name, grid=(n_steps,), in_specs=[row, row, vec, vec] + [ANY] * n_s,
        out_specs=[row, vec, vec, vec] + [ANY] * n_s,
        out_shape=[jax.ShapeDtypeStruct((T, D), F32)] + [jax.ShapeDtypeStruct((1, D), F32)] * 3
        + [jax.ShapeDtypeStruct((N_CHIPS, rows // 2, cols), F32) for _, (rows, cols), _ in exchange_units],
        scratch_shapes=[pltpu.SemaphoreType.DMA((n_s,)), pltpu.SemaphoreType.DMA((n_s,))],
        compiler_params=_params(("arbitrary",)),
    )(dz, y, ln_g, ln_b, *slabs)
    return outs[0], outs[1], outs[2], outs[3], outs[4:]


def _conv_bwd_dw(dy, a, dw_w, name, exchange):
    T = a.shape[0]
    tb = min(CONV_TB, T)
    per = tb // HALO
    last_halo = T // HALO - 1
    n_steps = T // tb
    n_x = len(exchange)

    def body(*refs):
        dyc_ref, dyn_ref, cur_ref, prev_ref, w_ref = refs[:5]
        x_ins = refs[5:5 + n_x]
        da_ref, dbias_ref, dw_ref = refs[5 + n_x:8 + n_x]
        x_outs = refs[8 + n_x:8 + 2 * n_x]
        uwin, dywin, dwacc = refs[8 + 2 * n_x:11 + 2 * n_x]
        sems = refs[11 + 2 * n_x:]
        i = pl.program_id(0)

        @pl.when(i == 0)
        def _():
            _exchange_start(x_ins, x_outs, sems)
            dbias_ref[...] = jnp.zeros_like(dbias_ref)
            dwacc[...] = jnp.zeros_like(dwacc)

        uwin[0, 0:HALO, :] = jnp.where(i > 0, _glu(prev_ref[...]), 0.0)
        uwin[0, HALO:, :] = _glu(cur_ref[...])
        dywin[0, 0:tb, :] = dyc_ref[...]
        dywin[0, tb:, :] = jnp.where(i < n_steps - 1, dyn_ref[...], 0.0)
        _shifted_copies(uwin)
        _shifted_copies(dywin)
        for r in range(tb // CONV_RC):
            rows = slice(r * CONV_RC, (r + 1) * CONV_RC)
            dy_c = dywin[0, rows, :]
            du = jnp.zeros((CONV_RC, D), F32)
            for k in range(CONV_W):
                du = du + _window(dywin, r * CONV_RC + (CONV_W - 1) - k) * w_ref[k:k + 1, :]
                prod = dy_c * _window(uwin, r * CONV_RC + HALO - (CONV_W - 1) + k)
                part = prod[0:8, :]
                for q in range(1, CONV_RC // 8):
                    part = part + prod[8 * q:8 * q + 8, :]
                dwacc[8 * k:8 * k + 8, :] += part
            av = cur_ref[rows, :]
            a1, sg = av[:, :D], jax.nn.sigmoid(av[:, D:])
            da1 = du * sg
            da2 = du * a1 * (sg * (1.0 - sg))
            da_ref[rows, 0:D] = da1.astype(BF16)
            da_ref[rows, D:] = da2.astype(BF16)
            dbias_ref[:, 0:D] += jnp.sum(da1, axis=0, keepdims=True)
            dbias_ref[:, D:] += jnp.sum(da2, axis=0, keepdims=True)

        @pl.when(i == n_steps - 1)
        def _():
            for k in range(CONV_W):
                dw_ref[k:k + 1, :] = jnp.sum(dwacc[8 * k:8 * k + 8, :], axis=0, keepdims=True)
            dw_ref[CONV_W:, :] = jnp.zeros((HALO - CONV_W, D), F32)
            _exchange_finish(x_ins, x_outs, sems)

    outs = pl.pallas_call(
        body, name=name, grid=(n_steps,),
        in_specs=[pl.BlockSpec((tb, D), lambda i: (i, 0)),
                  pl.BlockSpec((HALO, D), lambda i: (jnp.minimum((i + 1) * per, last_halo), 0)),
                  pl.BlockSpec((tb, 2 * D), lambda i: (i, 0)),
                  pl.BlockSpec((HALO, 2 * D), lambda i: (jnp.maximum(i * per - 1, 0), 0)),
                  pl.BlockSpec((HALO, D), lambda i: (0, 0))] + [ANY] * n_x,
        out_specs=[pl.BlockSpec((tb, 2 * D), lambda i: (i, 0)), pl.BlockSpec((1, 2 * D), lambda i: (0, 0)),
                   pl.BlockSpec((HALO, D), lambda i: (0, 0))] + [ANY] * n_x,
        out_shape=[jax.ShapeDtypeStruct((T, 2 * D), BF16), jax.ShapeDtypeStruct((1, 2 * D), F32),
                   jax.ShapeDtypeStruct((HALO, D), F32)] + [jax.ShapeDtypeStruct(p.shape, p.dtype) for p in exchange],
        scratch_shapes=[pltpu.VMEM((SUBLANES, tb + HALO, D), F32), pltpu.VMEM((SUBLANES, tb + HALO, D), F32),
                        pltpu.VMEM((8 * HALO, D), F32)] + _exchange_sems(n_x),
        compiler_params=_params(("arbitrary",)),
    )(dy, dy, a, a, dw_w, *exchange)
    return outs[0], outs[1], outs[2], outs[3:]


GATE_TB = 512


def _gate_fwd(fl, b_f, name):
    T = fl.shape[0]
    tb = min(GATE_TB, T)
    row = pl.BlockSpec((tb, LANES), lambda i: (i, 0))

    def body(fl_ref, b_ref, f_ref, f1_ref, f2_ref, f3_ref, carry):
        @pl.when(pl.program_id(0) == 0)
        def _():
            carry[...] = jnp.zeros_like(carry)

        z = fl_ref[...] + b_ref[...]
        lf = jnp.minimum(z, 0.0) - jnp.log(1.0 + jnp.exp(-jnp.abs(z)))
        tri = (lax.broadcasted_iota(jnp.int32, (tb, tb), 0) >= lax.broadcasted_iota(jnp.int32, (tb, tb), 1)).astype(F32)
        f = jnp.dot(tri, lf, precision=lax.Precision.HIGHEST, preferred_element_type=F32) + carry[...]
        carry[...] = f[tb - 1:tb, :]
        f_ref[...] = f
        f1_ref[...], f2_ref[...], f3_ref[...] = _split3(f)

    return pl.pallas_call(
        body, name=name, grid=(T // tb,), in_specs=[row, pl.BlockSpec((1, LANES), lambda i: (0, 0))],
        out_specs=[row] * 4, out_shape=[jax.ShapeDtypeStruct((T, LANES), F32)] * 4,
        scratch_shapes=[pltpu.VMEM((1, LANES), F32)],
        compiler_params=_params(("arbitrary",)),
    )(fl, b_f)


def _gate_bwd(dF_q, dF_k, fl, b_f, name):
    T = fl.shape[0]
    tb = min(GATE_TB, T)
    nb = T // tb
    row = pl.BlockSpec((tb, LANES), lambda i: (nb - 1 - i, 0))
    vec = pl.BlockSpec((1, LANES), lambda i: (0, 0))

    def body(dfq_ref, dfk_ref, fl_ref, b_ref, dfl_ref, db_ref, carry):
        @pl.when(pl.program_id(0) == 0)
        def _():
            carry[...] = jnp.zeros_like(carry)
            db_ref[...] = jnp.zeros_like(db_ref)

        tri = (lax.broadcasted_iota(jnp.int32, (tb, tb), 0) <= lax.broadcasted_iota(jnp.int32, (tb, tb), 1)).astype(F32)
        rc = jnp.dot(tri, dfq_ref[...] - dfk_ref[...], precision=lax.Precision.HIGHEST,
                     preferred_element_type=F32) + carry[...]
        carry[...] = rc[0:1, :]
        dfl = rc * jax.nn.sigmoid(-(fl_ref[...] + b_ref[...]))
        dfl_ref[...] = dfl
        db_ref[...] += jnp.sum(dfl, axis=0, keepdims=True)

    return pl.pallas_call(
        body, name=name, grid=(nb,), in_specs=[row, row, row, vec], out_specs=[row, vec],
        out_shape=[jax.ShapeDtypeStruct((T, LANES), F32), jax.ShapeDtypeStruct((1, LANES), F32)],
        scratch_shapes=[pltpu.VMEM((1, LANES), F32)],
        compiler_params=_params(("arbitrary",)),
    )(dF_q, dF_k, fl, b_f)


ATT_TB = 512
LAYOUT_TB = 2048
NT = (((1,), (1,)), ((), ()))
NN = (((1,), (0,)), ((), ()))


def _lane_insert(base, parts, first_lane):
    lane = lax.broadcasted_iota(jnp.int32, base.shape, 1)
    out = base
    for n, p in enumerate(parts):
        out = jnp.where(lane == first_lane + n, p, out)
    return out


def _causal(tb):
    return lax.broadcasted_iota(jnp.int32, (tb, tb), 0) >= lax.broadcasted_iota(jnp.int32, (tb, tb), 1)


def _seen_by(n_keys, n_queries):
    return (lax.broadcasted_iota(jnp.int32, (n_keys, n_queries), 1)
            >= lax.broadcasted_iota(jnp.int32, (n_keys, n_queries), 0))


def _other_head(x):
    return pltpu.roll(x, HEAD_DIM, 1)


def _attn_pack(proj, f1, f2, f3, name, gather_units, gather_shards):
    T = proj.shape[0]
    tb = min(LAYOUT_TB, T)
    pairs = N_HEADS // 2
    n_steps = T // tb
    n_g = len(gather_units)
    cols = lambda first: pl.BlockSpec((tb, LANES), lambda p, i: (i, first + p))
    gate = pl.BlockSpec((tb, LANES), lambda p, i: (i, 0))
    heads = pl.BlockSpec((2, tb, LANES), lambda p, i: (p, i, 0))
    norm = pl.BlockSpec((2, 8, LANES), lambda p, i: (p, 0, 0))

    def body(*refs):
        q_ref, k_ref, v_ref, f1_ref, f2_ref, f3_ref = refs[:6]
        g_ins = refs[6:6 + n_g]
        qa_ref, ka_ref, va_ref, qn_ref, kn_ref = refs[6 + n_g:11 + n_g]
        g_outs = refs[11 + n_g:11 + 2 * n_g]
        sems = refs[11 + 2 * n_g:]
        p, i = pl.program_id(0), pl.program_id(1)
        if n_g:
            pl.when(jnp.logical_and(p == 0, i == 0))(lambda: _gather_start(gather_units, g_ins, g_outs, sems))
        lane = lax.broadcasted_iota(jnp.int32, (tb, LANES), 1)
        data = lane < HEAD_DIM

        @pl.when(i == 0)
        def _():
            qn_ref[...] = jnp.zeros_like(qn_ref)
            kn_ref[...] = jnp.zeros_like(kn_ref)

        qv, kv, vv = q_ref[...].astype(F32) * 0.125, k_ref[...].astype(F32), v_ref[...].astype(F32)
        for e in range(2):
            fcol = [jnp.sum(jnp.where(lane == 2 * p + e, f[...], 0.0), axis=-1, keepdims=True)
                    for f in (f1_ref, f2_ref, f3_ref)]
            pick = (lambda t: t) if e == 0 else _other_head
            qd = jnp.where(data, pick(qv), 0.0).astype(BF16).astype(F32)
            kd = jnp.where(data, pick(kv), 0.0).astype(BF16).astype(F32)
            vd = jnp.where(data, pick(vv), 0.0)
            qa_ref[e] = _lane_insert(qd, fcol + [1.0, 1.0, 1.0], AUG_F).astype(BF16)
            ka_ref[e] = _lane_insert(kd, [1.0, 1.0, 1.0] + [-f for f in fcol] + [1.0, 1.0, 1.0], AUG_F).astype(BF16)
            va_ref[e] = _lane_insert(vd, [-1.0, -1.0, -1.0, 1.0], AUG_F).astype(BF16)
            qn_ref[e] = jnp.maximum(qn_ref[e], jnp.max(jnp.sum(qd * qd, axis=-1, keepdims=True)))
            kn_ref[e] = jnp.maximum(kn_ref[e], jnp.max(jnp.sum(kd * kd, axis=-1, keepdims=True)))
        if n_g:
            pl.when(jnp.logical_and(p == pairs - 1, i == n_steps - 1))(
                lambda: _gather_finish(gather_units, g_ins, g_outs, sems))

    aug = jax.ShapeDtypeStruct((N_HEADS, T, LANES), BF16)
    nrm = jax.ShapeDtypeStruct((N_HEADS, 8, LANES), F32)
    outs = pl.pallas_call(
        body, name=name, grid=(pairs, n_steps),
        in_specs=[cols(0), cols(pairs), cols(2 * pairs), gate, gate, gate] + [ANY] * n_g,
        out_specs=[heads, heads, heads, norm, norm] + [ANY] * n_g,
        out_shape=[aug, aug, aug, nrm, nrm]
        + [jax.ShapeDtypeStruct((N_CHIPS,) + shape, BF16) for _, shape in gather_units],
        scratch_shapes=_gather_sems(n_g) if n_g else [],
        compiler_params=_params(("arbitrary", "arbitrary") if n_g else ("parallel", "arbitrary")),
    )(proj, proj, proj, f1, f2, f3, *gather_shards)
    return outs[:5], outs[5:]


def _merge_heads(x_aug, scale, out_dtype, name, column=None):
    H, T, _ = x_aug.shape
    tb = min(LAYOUT_TB, T)

    def body(x_ref, o_ref, *col_ref):
        p = pl.program_id(1)
        lane = lax.broadcasted_iota(jnp.int32, (tb, LANES), 1)
        x0, x1 = x_ref[0], x_ref[1]
        o_ref[...] = (jnp.where(lane < HEAD_DIM, x0, _other_head(x1)) * scale).astype(out_dtype)
        if column is not None:
            @pl.when(p == 0)
            def _():
                col_ref[0][...] = jnp.zeros_like(col_ref[0])

            c0 = jnp.sum(jnp.where(lane == column, x0, 0.0), axis=-1, keepdims=True)
            c1 = jnp.sum(jnp.where(lane == column, x1, 0.0), axis=-1, keepdims=True)
            col_ref[0][...] += jnp.where(lane == 2 * p, c0, 0.0) + jnp.where(lane == 2 * p + 1, c1, 0.0)

    out_specs = [pl.BlockSpec((tb, LANES), lambda i, p: (i, p))]
    out_shape = [jax.ShapeDtypeStruct((T, D), out_dtype)]
    if column is not None:
        out_specs.append(pl.BlockSpec((tb, LANES), lambda i, p: (i, 0)))
        out_shape.append(jax.ShapeDtypeStruct((T, LANES), F32))
    return pl.pallas_call(
        body, name=name, grid=(T // tb, H // 2),
        in_specs=[pl.BlockSpec((2, tb, LANES), lambda i, p: (p, i, 0))],
        out_specs=out_specs, out_shape=out_shape,
        compiler_params=_params(("parallel", "arbitrary")),
    )(x_aug)


def _attn_prep(d_o, o_aug, name):
    H, T, _ = o_aug.shape
    tb = min(LAYOUT_TB, T)
    heads = pl.BlockSpec((2, tb, LANES), lambda p, i: (p, i, 0))

    def body(do_ref, o_ref, out_ref):
        lane = lax.broadcasted_iota(jnp.int32, (tb, LANES), 1)
        dov = do_ref[...]
        for e in range(2):
            d_e = jnp.where(lane < HEAD_DIM, dov if e == 0 else _other_head(dov), 0.0)
            delta = jnp.sum(d_e * o_ref[e], axis=-1, keepdims=True)
            out_ref[e] = _lane_insert(d_e, _split3(delta), AUG_F).astype(BF16)

    return pl.pallas_call(
        body, name=name, grid=(H // 2, T // tb),
        in_specs=[pl.BlockSpec((tb, LANES), lambda p, i: (i, p)), heads], out_specs=heads,
        out_shape=jax.ShapeDtypeStruct((H, T, LANES), BF16),
        compiler_params=_params(("parallel", "parallel")),
    )(d_o, o_aug)


def _flash_spec(n_in, n_out, H, nb, tb, T, resident, scratch, n_tables=3):
    blk = pl.BlockSpec((2, tb, LANES), lambda h, i, *_: (h, i, 0))
    whole = pl.BlockSpec((2, T, LANES), lambda h, i, *_: (h, 0, 0))
    return pltpu.PrefetchScalarGridSpec(
        num_scalar_prefetch=n_tables, grid=(H // 2, nb),
        in_specs=[whole if resident[n] else blk for n in range(n_in)],
        out_specs=[blk] * n_out, scratch_shapes=scratch)


def _first_live(qk_ref, fs_ref, fe_ref, h, i):
    top = qk_ref[h] + fs_ref[h, i]
    return lax.fori_loop(0, i, lambda j, n: n + jnp.where(top - fe_ref[h, j] < SKIP_BELOW, 1, 0), 0)


def _flash_fwd(qk, fs, fe, kmax, q_aug, k_aug, v_aug, name):
    H, T, _ = q_aug.shape
    tb = min(ATT_TB, T)

    def body(qk_ref, fs_ref, fe_ref, km_ref, q_ref, k_ref, v_ref, o_ref, q2_ref, m_s, acc_s, qm_s):
        h, i = 2 * pl.program_id(0), pl.program_id(1)
        acc_s[...] = jnp.zeros_like(acc_s)
        first = jnp.minimum(_first_live(qk_ref, fs_ref, fe_ref, h, i), _first_live(qk_ref, fs_ref, fe_ref, h + 1, i))
        bounded = jnp.logical_and(qk_ref[h] < FIXED_MAX_BELOW, qk_ref[h + 1] < FIXED_MAX_BELOW)

        def sweep(step):
            def loop_body(kb, carry):
                step(kb, False)
                return carry

            lax.fori_loop(first, i, loop_body, 0)
            step(i, True)

        @pl.when(bounded)
        def _():
            lane = lax.broadcasted_iota(jnp.int32, (tb, LANES), 1)
            for e in range(2):
                qf = q_ref[e].astype(F32)
                norm = jnp.sqrt(jnp.sum(jnp.where(lane < HEAD_DIM, qf * qf, 0.0), axis=-1, keepdims=True))
                bound = norm * (1.01 * km_ref[h + e]) + 1e-3
                m_s[e] = bound
                qm_s[e] = _lane_insert(qf, [-p for p in _split3(bound)], AUG_L).astype(BF16)

            def step(kb, masked):
                rows = pl.ds(pl.multiple_of(kb * tb, tb), tb)
                for e in range(2):
                    p = jnp.exp(lax.dot_general(qm_s[e], k_ref[e, rows, :], NT, preferred_element_type=F32))
                    if masked:
                        p = jnp.where(_causal(tb), p, 0.0)
                    acc_s[e] += lax.dot_general(p.astype(BF16), v_ref[e, rows, :], NN, preferred_element_type=F32)

            sweep(step)

        @pl.when(jnp.logical_not(bounded))
        def _():
            m_s[...] = jnp.full(m_s.shape, -jnp.inf, F32)

            def step(kb, masked):
                rows = pl.ds(pl.multiple_of(kb * tb, tb), tb)
                for e in range(2):
                    s = lax.dot_general(q_ref[e], k_ref[e, rows, :], NT, preferred_element_type=F32)
                    if masked:
                        s = jnp.where(_causal(tb), s, MASK_VALUE)
                    m_old = m_s[e]
                    m_new = jnp.maximum(m_old, jnp.max(s, axis=-1, keepdims=True))
                    p = jnp.exp(s - m_new)
                    acc_s[e] = jnp.exp(m_old - m_new) * acc_s[e] + lax.dot_general(
                        p.astype(BF16), v_ref[e, rows, :], NN, preferred_element_type=F32)
                    m_s[e] = m_new

            sweep(step)

        for e in range(2):
            acc = acc_s[e]
            lane = lax.broadcasted_iota(jnp.int32, acc.shape, 1)
            l = jnp.sum(jnp.where(lane == AUG_ONE, acc, 0.0), axis=-1, keepdims=True)
            o_ref[e] = acc / l
            lse = m_s[e] + jnp.log(l)
            q2_ref[e] = _lane_insert(q_ref[e].astype(F32), [-p for p in _split3(lse)], AUG_L).astype(BF16)

    return pl.pallas_call(
        body, name=name,
        grid_spec=_flash_spec(3, 2, H, T // tb, tb, T, (False, True, True),
                              [pltpu.VMEM((2, tb, 1), F32), pltpu.VMEM((2, tb, LANES), F32),
                               pltpu.VMEM((2, tb, LANES), BF16)], n_tables=4),
        out_shape=[jax.ShapeDtypeStruct((H, T, LANES), F32), jax.ShapeDtypeStruct((H, T, LANES), BF16)],
        compiler_params=_params(("parallel", "arbitrary")),
    )(qk, fs, fe, kmax, q_aug, k_aug, v_aug)


def _flash_dq(qk, fs, fe, q2, k_aug, do_aug, v_aug, name):
    H, T, _ = q2.shape
    tb = min(ATT_TB, T)

    def body(qk_ref, fs_ref, fe_ref, q_ref, k_ref, do_ref, v_ref, dq_ref, acc_s):
        h, i = 2 * pl.program_id(0), pl.program_id(1)
        acc_s[...] = jnp.zeros_like(acc_s)

        def step(kb, masked):
            rows = pl.ds(pl.multiple_of(kb * tb, tb), tb)
            for e in range(2):
                k = k_ref[e, rows, :]
                p = jnp.exp(lax.dot_general(q_ref[e], k, NT, preferred_element_type=F32))
                if masked:
                    p = jnp.where(_causal(tb), p, 0.0)
                ds = p * lax.dot_general(do_ref[e], v_ref[e, rows, :], NT, preferred_element_type=F32)
                acc_s[e] += lax.dot_general(ds.astype(BF16), k, NN, preferred_element_type=F32)

        def loop_body(kb, carry):
            step(kb, False)
            return carry

        first = jnp.minimum(_first_live(qk_ref, fs_ref, fe_ref, h, i), _first_live(qk_ref, fs_ref, fe_ref, h + 1, i))
        lax.fori_loop(first, i, loop_body, 0)
        step(i, True)
        dq_ref[...] = acc_s[...]

    return pl.pallas_call(
        body, name=name,
        grid_spec=_flash_spec(4, 1, H, T // tb, tb, T, (False, True, False, True), [pltpu.VMEM((2, tb, LANES), F32)]),
        out_shape=[jax.ShapeDtypeStruct((H, T, LANES), F32)],
        compiler_params=_params(("parallel", "arbitrary")),
    )(qk, fs, fe, q2, k_aug, do_aug, v_aug)[0]


def _flash_dkv(qk, fs, fe, q2, k_aug, do_aug, v_aug, name):
    H, T, _ = q2.shape
    tb = min(ATT_TB, T)
    half = tb // 2
    nb = T // tb

    def body(qk_ref, fs_ref, fe_ref, q_ref, k_ref, do_ref, v_ref, dk_ref, dv_ref, dk_s, dv_s):
        h, i = 2 * pl.program_id(0), pl.program_id(1)
        dk_s[...] = jnp.zeros_like(dk_s)
        dv_s[...] = jnp.zeros_like(dv_s)

        def step(qb, masked):
            base = pl.multiple_of(qb * tb, tb)
            parts = [(slice(0, half), 0, tb), (slice(half, tb), half, half)] if masked else [(slice(0, tb), 0, tb)]
            for e in range(2):
                for krows, q0, nq in parts:
                    queries = pl.ds(pl.multiple_of(base + q0, half), nq)
                    q, dov = q_ref[e, queries, :], do_ref[e, queries, :]
                    pt = jnp.exp(lax.dot_general(k_ref[e, krows, :], q, NT, preferred_element_type=F32))
                    if masked:
                        pt = jnp.where(_seen_by(krows.stop - krows.start, nq), pt, 0.0)
                    dst = pt * lax.dot_general(v_ref[e, krows, :], dov, NT, preferred_element_type=F32)
                    dv_s[e, krows, :] += lax.dot_general(pt.astype(BF16), dov, NN, preferred_element_type=F32)
                    dk_s[e, krows, :] += lax.dot_general(dst.astype(BF16), q, NN, preferred_element_type=F32)

        step(i, True)

        def loop_body(qb, carry):
            step(qb, False)
            return carry

        def live_after(head):
            base = qk_ref[head] - fe_ref[head, i]
            return lax.fori_loop(i + 1, nb, lambda b, n: n + jnp.where(base + fs_ref[head, b] < SKIP_BELOW, 0, 1), 0)

        lax.fori_loop(i + 1, i + 1 + jnp.maximum(live_after(h), live_after(h + 1)), loop_body, 0)
        dk_ref[...] = dk_s[...]
        dv_ref[...] = dv_s[...]

    return pl.pallas_call(
        body, name=name,
        grid_spec=_flash_spec(4, 2, H, nb, tb, T, (True, False, True, False),
                              [pltpu.VMEM((2, tb, LANES), F32), pltpu.VMEM((2, tb, LANES), F32)]),
        out_shape=[jax.ShapeDtypeStruct((H, T, LANES), F32)] * 2,
        compiler_params=_params(("parallel", "arbitrary")),
    )(qk, fs, fe, q2, k_aug, do_aug, v_aug)


def _add_pair(place, slab, got, name):
    _, half, cols = got.shape
    grid_spec = pltpu.PrefetchScalarGridSpec(
        num_scalar_prefetch=1, grid=(N_CHIPS,),
        in_specs=[pl.BlockSpec((None, half, cols), lambda j, s: (j, s[4], 0)),
                  pl.BlockSpec((None, half, cols), lambda j, s: (j, 0, 0))],
        out_specs=pl.BlockSpec((None, half, cols), lambda j, s: (j, 0, 0)))

    def body(s_ref, a_ref, b_ref, o_ref):
        o_ref[...] = (a_ref[...] + b_ref[...]).astype(BF16)

    return pl.pallas_call(
        body, name=name, grid_spec=grid_spec, out_shape=jax.ShapeDtypeStruct(got.shape, BF16),
        compiler_params=_params(("parallel",)),
    )(place, slab, got)


def _add_chips(place, pair, by_chip, name):
    _, half, cols = pair.shape
    tb = min(256, half)
    steps = half // tb
    slot = lambda n: pl.BlockSpec((None, tb, cols), lambda i, s: (s[n], i, 0))
    grid_spec = pltpu.PrefetchScalarGridSpec(
        num_scalar_prefetch=1, grid=(steps,), in_specs=[slot(0), slot(1), slot(2), slot(3)],
        out_specs=pl.BlockSpec((tb, cols), lambda i, s: (s[4] * steps + i, 0)))

    def body(s_ref, own_ref, b1_ref, b2_ref, b3_ref, o_ref):
        o_ref[...] = ((own_ref[...].astype(F32) + b1_ref[...].astype(F32)) + b2_ref[...].astype(F32)) \
            + b3_ref[...].astype(F32)

    return pl.pallas_call(
        body, name=name, grid_spec=grid_spec, out_shape=jax.ShapeDtypeStruct((2 * half, cols), F32),
        compiler_params=_params(("parallel",)),
    )(place, pair, by_chip, by_chip, by_chip)


def _add_slots(b, name, tb=128):
    n, rows, cols = b.shape
    tb = tb if rows % tb == 0 else rows
    specs = [pl.BlockSpec((None, tb, cols), functools.partial(lambda j, i: (j, i, 0), j)) for j in range(n)]

    def body(*refs):
        acc = refs[0][...]
        for r in refs[1:n]:
            acc = acc + r[...]
        refs[n][...] = acc

    return pl.pallas_call(
        body, name=name, grid=(rows // tb,), in_specs=specs, out_specs=pl.BlockSpec((tb, cols), lambda i: (i, 0)),
        out_shape=jax.ShapeDtypeStruct((rows, cols), F32), compiler_params=_params(("parallel",)),
    )(*([b] * n))


def _adamw(w, g, m, v, name):
    rows, cols = w.shape
    tr = 256 if rows % 256 == 0 else rows
    blk = pl.BlockSpec((tr, cols), lambda i: (i, 0))

    def body(w_ref, g_ref, m_ref, v_ref, d_ref, nm_ref, nv_ref):
        gv = g_ref[...]
        nm = ADAM_B1 * m_ref[...] + (1.0 - ADAM_B1) * gv
        nv = ADAM_B2 * v_ref[...] + (1.0 - ADAM_B2) * (gv * gv)
        m_hat = nm / (1.0 - ADAM_B1 ** ADAM_STEP)
        v_hat = nv / (1.0 - ADAM_B2 ** ADAM_STEP)
        d_ref[...] = -ADAM_LR * (m_hat / (jnp.sqrt(v_hat) + ADAM_EPS) + ADAM_WD * w_ref[...])
        nm_ref[...] = nm
        nv_ref[...] = nv

    return pl.pallas_call(
        body, name=name, grid=(rows // tr,), in_specs=[blk] * 4, out_specs=[blk] * 3,
        out_shape=[jax.ShapeDtypeStruct((rows, cols), F32)] * 3, compiler_params=_params(("parallel",)),
    )(w, g, m, v)


ANY = pl.BlockSpec(memory_space=pl.ANY)


def _place():
    x, y, c = lax.axis_index("x"), lax.axis_index("y"), lax.axis_index("c")
    others = [(1 - x, y), (x, 1 - y), (1 - x, 1 - y)]
    return x, y, c, 2 * x + y, others


def _half(ref, rows, h):
    return ref.at[pl.ds(h * (rows // 2), rows // 2), :]


def _gather_sems(n_u):
    return [pltpu.SemaphoreType.DMA((6 * n_u,)), pltpu.SemaphoreType.DMA((6 * n_u,)),
            pltpu.SemaphoreType.DMA((n_u,)), pltpu.SemaphoreType.DMA((n_u,))]


def _gather_copies(units, ins, outs, sems):
    send_sems, recv_sems, own_send_sems, own_recv_sems = sems
    x, y, c, me, others = _place()
    sibling = (x, y, 1 - c)

    def copy(u, k, chip, h, to, src=None):
        dst = _half(outs[u].at[chip], units[u][1][0], h)
        return pltpu.make_async_remote_copy(
            src_ref=dst if src is None else src, dst_ref=dst, send_sem=send_sems.at[6 * u + k],
            recv_sem=recv_sems.at[6 * u + k], device_id=to, device_id_type=MESH)

    pairs = [(u, j, 2 * ox + oy, (ox, oy)) for u in range(len(units)) for j, (ox, oy) in enumerate(others)]
    own = [pltpu.make_async_remote_copy(
        src_ref=ins[u], dst_ref=outs[u].at[me], send_sem=own_send_sems.at[u], recv_sem=own_recv_sems.at[u],
        device_id=sibling, device_id_type=MESH) for u in range(len(units))]
    first = [copy(u, j, me, c, (*to, c), src=_half(ins[u], units[u][1][0], c)) for u, j, _, to in pairs]
    arrivals = [copy(u, j, chip, c, (x, y, c)) for u, j, chip, _ in pairs]
    passed = [copy(u, 3 + j, chip, c, sibling) for u, j, chip, _ in pairs]
    from_sibling = [copy(u, 3 + j, chip, 1 - c, (x, y, c)) for u, j, chip, _ in pairs]
    return own, first, arrivals, passed, from_sibling


def _gather_start(units, ins, outs, sems):
    own, first, _, _, _ = _gather_copies(units, ins, outs, sems)
    for cp in own + first:
        cp.start()


def _gather_finish(units, ins, outs, sems):
    own, first, arrivals, passed, from_sibling = _gather_copies(units, ins, outs, sems)
    for arrived, onward in zip(arrivals, passed):
        arrived.wait_recv()
        onward.start()
    for cp in from_sibling:
        cp.wait_recv()
    for cp in first + passed:
        cp.wait_send()
    for cp in own:
        cp.wait()


def _allgather_weights(units, shards, name):
    n_u = len(units)

    def body(*refs):
        ins, outs, sems = refs[:n_u], refs[n_u:2 * n_u], refs[2 * n_u:]
        _gather_start(units, ins, outs, sems)
        _gather_finish(units, ins, outs, sems)

    return pl.pallas_call(
        body, name=name, in_specs=[ANY] * n_u, out_specs=[ANY] * n_u,
        out_shape=[jax.ShapeDtypeStruct((N_CHIPS,) + shape, BF16) for _, shape in units],
        scratch_shapes=_gather_sems(n_u),
    )(*shards)


def _sibling_copies(units, ins, got, sems):
    send_sems, recv_sems = sems
    x, y, c, _, _ = _place()
    copies = []
    for u, (_, (rows, _), _) in enumerate(units):
        half = rows // 2
        copies.append(pltpu.make_async_remote_copy(
            src_ref=ins[u].at[:, pl.ds((1 - c) * half, half), :], dst_ref=got[u], send_sem=send_sems.at[u],
            recv_sem=recv_sems.at[u], device_id=(x, y, 1 - c), device_id_type=MESH))
    return copies


def _sibling_start(units, ins, got, sems):
    for cp in _sibling_copies(units, ins, got, sems):
        cp.start()


def _sibling_finish(units, ins, got, sems):
    for cp in _sibling_copies(units, ins, got, sems):
        cp.wait()


def _sibling_exchange(units, slabs, name):
    n_u = len(units)

    def body(*refs):
        ins, got, sems = refs[:n_u], refs[n_u:2 * n_u], refs[2 * n_u:]
        _sibling_start(units, ins, got, sems)
        _sibling_finish(units, ins, got, sems)

    return pl.pallas_call(
        body, name=name, in_specs=[ANY] * n_u, out_specs=[ANY] * n_u,
        out_shape=[jax.ShapeDtypeStruct((N_CHIPS, rows // 2, cols), F32) for _, (rows, cols), _ in units],
        scratch_shapes=[pltpu.SemaphoreType.DMA((n_u,)), pltpu.SemaphoreType.DMA((n_u,))],
    )(*slabs)


def _exchange_sems(n_u):
    return [pltpu.SemaphoreType.DMA((3 * n_u,)), pltpu.SemaphoreType.DMA((3 * n_u,))]


def _exchange_copies(ins, outs, sems):
    send_sems, recv_sems = sems
    x, y, c, me, others = _place()
    sends = [pltpu.make_async_remote_copy(
        src_ref=ins[u].at[2 * ox + oy], dst_ref=outs[u].at[me], send_sem=send_sems.at[3 * u + j],
        recv_sem=recv_sems.at[3 * u + j], device_id=(ox, oy, c), device_id_type=MESH)
        for u in range(len(ins)) for j, (ox, oy) in enumerate(others)]
    arrivals = [pltpu.make_async_remote_copy(
        src_ref=ins[u].at[me], dst_ref=outs[u].at[2 * ox + oy], send_sem=send_sems.at[3 * u + j],
        recv_sem=recv_sems.at[3 * u + j], device_id=(x, y, c), device_id_type=MESH)
        for u in range(len(ins)) for j, (ox, oy) in enumerate(others)]
    return sends, arrivals


def _exchange_start(ins, outs, sems):
    for cp in _exchange_copies(ins, outs, sems)[0]:
        cp.start()


def _exchange_finish(ins, outs, sems):
    sends, arrivals = _exchange_copies(ins, outs, sems)
    for cp in arrivals:
        cp.wait_recv()
    for cp in sends:
        cp.wait_send()


def _chip_exchange(pairs, small):
    n_u = len(pairs)

    def body(*refs):
        ins, s_ref = refs[:n_u], refs[n_u]
        outs, sall_ref = refs[n_u + 1:2 * n_u + 1], refs[2 * n_u + 1]
        send_sems, recv_sems, ssend_sems, srecv_sems, local_sem = refs[2 * n_u + 2:]
        x, y, c, _, _ = _place()
        dev = 4 * x + 2 * y + c
        local = pltpu.make_async_copy(s_ref, sall_ref.at[dev], local_sem)
        flips = [(fx, fy, fc) for fx in (0, 1) for fy in (0, 1) for fc in (0, 1)][1:]
        small_sends = [pltpu.make_async_remote_copy(
            src_ref=s_ref, dst_ref=sall_ref.at[dev], send_sem=ssend_sems.at[n], recv_sem=srecv_sems.at[n],
            device_id=(x ^ fx, y ^ fy, c ^ fc), device_id_type=MESH) for n, (fx, fy, fc) in enumerate(flips)]
        local.start()
        _exchange_start(ins, outs, (send_sems, recv_sems))
        for cp in small_sends:
            cp.start()
        _exchange_finish(ins, outs, (send_sems, recv_sems))
        for n, (fx, fy, fc) in enumerate(flips):
            src_dev = 4 * (x ^ fx) + 2 * (y ^ fy) + (c ^ fc)
            pltpu.make_async_remote_copy(
                src_ref=s_ref, dst_ref=sall_ref.at[src_dev], send_sem=ssend_sems.at[n], recv_sem=srecv_sems.at[n],
                device_id=(x, y, c), device_id_type=MESH).wait_recv()
        for cp in small_sends:
            cp.wait_send()
        local.wait()

    outs = pl.pallas_call(
        body, name="grad_chip_exchange", in_specs=[ANY] * (n_u + 1), out_specs=[ANY] * (n_u + 1),
        out_shape=[jax.ShapeDtypeStruct(p.shape, p.dtype) for p in pairs]
        + [jax.ShapeDtypeStruct((8, SMALL_ALL_ROWS, LANES), F32)],
        scratch_shapes=_exchange_sems(n_u) + [pltpu.SemaphoreType.DMA((7,)), pltpu.SemaphoreType.DMA((7,)),
                                              pltpu.SemaphoreType.DMA],
    )(*pairs, small)
    return outs[:n_u], outs[n_u]


def _sibling_share(grads):
    n_u = len(GRAD_UNITS)

    def body(*refs):
        ins, outs = refs[:n_u], refs[n_u:2 * n_u]
        send_sems, recv_sems = refs[2 * n_u:]
        x, y, c, _, _ = _place()

        def rows_of(ref, u, h):
            half = GRAD_UNITS[u][1][0] // 2
            return ref.at[pl.ds(h * half, half), :]

        sends = [pltpu.make_async_remote_copy(
            src_ref=rows_of(ins[u], u, c), dst_ref=rows_of(outs[u], u, c), send_sem=send_sems.at[u],
            recv_sem=recv_sems.at[u], device_id=(x, y, 1 - c), device_id_type=MESH) for u in range(n_u)]
        for cp in sends:
            cp.start()
        for u in range(n_u):
            pltpu.make_async_remote_copy(
                src_ref=rows_of(ins[u], u, c), dst_ref=rows_of(outs[u], u, 1 - c), send_sem=send_sems.at[u],
                recv_sem=recv_sems.at[u], device_id=(x, y, c), device_id_type=MESH).wait_recv()
        for cp in sends:
            cp.wait_send()

    return pl.pallas_call(
        body, name="grad_sibling_share", in_specs=[ANY] * n_u, out_specs=[ANY] * n_u,
        out_shape=[jax.ShapeDtypeStruct(g.shape, F32) for g in grads],
        input_output_aliases={u: u for u in range(n_u)},
        scratch_shapes=[pltpu.SemaphoreType.DMA((n_u,)), pltpu.SemaphoreType.DMA((n_u,))],
    )(*grads)


def kernel(x, g_mix_pre, g_mix_post, g_ffn_pre, g_ffn_post, conv_pw1_w, conv_pw1_b, conv_dw_w, conv_dw_b, conv_ln_g, conv_ln_b, conv_pw2_w, conv_pw2_b, attn_w_in, attn_b_f, attn_w_o, mlp_w_up, mlp_w_down, loss_target, m_g_mix_pre, m_g_mix_post, m_g_ffn_pre, m_g_ffn_post, m_conv_pw1_w, m_conv_pw1_b, m_conv_dw_w, m_conv_dw_b, m_conv_ln_g, m_conv_ln_b, m_conv_pw2_w, m_conv_pw2_b, m_attn_w_in, m_attn_b_f, m_attn_w_o, m_mlp_w_up, m_mlp_w_down, v_g_mix_pre, v_g_mix_post, v_g_ffn_pre, v_g_ffn_post, v_conv_pw1_w, v_conv_pw1_b, v_conv_dw_w, v_conv_dw_b, v_conv_ln_g, v_conv_ln_b, v_conv_pw2_w, v_conv_pw2_b, v_attn_w_in, v_attn_b_f, v_attn_w_o, v_mlp_w_up, v_mlp_w_down):
    weights = dict(g_mix_pre=g_mix_pre, g_mix_post=g_mix_post, g_ffn_pre=g_ffn_pre, g_ffn_post=g_ffn_post, conv_pw1_w=conv_pw1_w, conv_pw1_b=conv_pw1_b, conv_dw_w=conv_dw_w, conv_dw_b=conv_dw_b, conv_ln_g=conv_ln_g, conv_ln_b=conv_ln_b, conv_pw2_w=conv_pw2_w, conv_pw2_b=conv_pw2_b, attn_w_in=attn_w_in, attn_b_f=attn_b_f, attn_w_o=attn_w_o, mlp_w_up=mlp_w_up, mlp_w_down=mlp_w_down)
    mom_m = dict(g_mix_pre=m_g_mix_pre, g_mix_post=m_g_mix_post, g_ffn_pre=m_g_ffn_pre, g_ffn_post=m_g_ffn_post, conv_pw1_w=m_conv_pw1_w, conv_pw1_b=m_conv_pw1_b, conv_dw_w=m_conv_dw_w, conv_dw_b=m_conv_dw_b, conv_ln_g=m_conv_ln_g, conv_ln_b=m_conv_ln_b, conv_pw2_w=m_conv_pw2_w, conv_pw2_b=m_conv_pw2_b, attn_w_in=m_attn_w_in, attn_b_f=m_attn_b_f, attn_w_o=m_attn_w_o, mlp_w_up=m_mlp_w_up, mlp_w_down=m_mlp_w_down)
    mom_v = dict(g_mix_pre=v_g_mix_pre, g_mix_post=v_g_mix_post, g_ffn_pre=v_g_ffn_pre, g_ffn_post=v_g_ffn_post, conv_pw1_w=v_conv_pw1_w, conv_pw1_b=v_conv_pw1_b, conv_dw_w=v_conv_dw_w, conv_dw_b=v_conv_dw_b, conv_ln_g=v_conv_ln_g, conv_ln_b=v_conv_ln_b, conv_pw2_w=v_conv_pw2_w, conv_pw2_b=v_conv_pw2_b, attn_w_in=v_attn_w_in, attn_b_f=v_attn_b_f, attn_w_o=v_attn_w_o, mlp_w_up=v_mlp_w_up, mlp_w_down=v_mlp_w_down)
    order = [n for n, _ in SMALL[:4]] + ["conv_pw1_w", "conv_pw1_b", "conv_dw_w", "conv_dw_b", "conv_ln_g", "conv_ln_b",
                                          "conv_pw2_w", "conv_pw2_b", "attn_w_in", "attn_b_f", "attn_w_o", "mlp_w_up",
                                          "mlp_w_down"]
    T = x.shape[1]
    x0 = x.reshape(T, D)
    target = loss_target.reshape(T, D)
    row = lambda a, l: a[l:l + 1, :]

    def shard(unit):
        if unit[:-1] in ("up", "down"):
            return weights["mlp_w_" + unit[:-1]][int(unit[-1])].astype(BF16)
        w = weights[unit][0].astype(BF16)
        return jnp.pad(w, ((0, HALO - CONV_W), (0, 0))) if unit == "conv_dw_w" else w

    by_cols = lambda g: g.transpose(1, 0, 2).reshape(g.shape[1], N_CHIPS * g.shape[2])
    by_rows = lambda g: g.reshape(N_CHIPS * g.shape[1], g.shape[2])
    g_pw1, g_dw, g_pw2 = _allgather_weights(GATHER_AT_START, [shard(n) for n, _ in GATHER_AT_START], "allgather_conv")
    w_pw1, w_pw2, w_dw = by_cols(g_pw1), by_rows(g_pw2), by_cols(g_dw).astype(F32)
    b_f = jnp.pad(attn_b_f, ((0, 0), (0, LANES - N_HEADS)))

    def mlp_fwd(h, l):
        (act,) = _matmul(h, w_up[l], "nn", (BF16,), _epi_sqrelu, f"mlp{l}_up")
        (mo,) = _matmul(act, w_down[l], "nn", (F32,), _epi_plain, f"mlp{l}_down", tk=DFF)
        return act, mo

    def mlp0_fwd(h):
        ride = [dict(gather_units=units, gather_shards=[shard(n) for n, _ in units]) for units in GATHER_BEHIND_MLP]
        (act,), (g_up1,) = _matmul(h, w_up[0], "nn", (BF16,), _epi_sqrelu, "mlp0_up", **ride[0])
        (mo,), (g_down1,) = _matmul(act, w_down[0], "nn", (F32,), _epi_plain, "mlp0_down", tk=DFF, **ride[1])
        w_up.append(by_cols(g_up1))
        w_down.append(by_rows(g_down1))
        return act, mo

    (h0,) = _resnorm_fwd(x0, None, None, row(g_mix_pre, 0), "norm_in")
    (a0,) = _matmul(h0, w_pw1, "nn", (F32,), _epi_bias, "conv_pw1", extras=((conv_pw1_b, "row"),))
    y0, z0, (g_up0, g_down0, g_win, g_wo) = _conv_fwd(
        a0, w_dw, conv_dw_b, conv_ln_g, conv_ln_b, "conv_fwd", GATHER_BEHIND_CONV,
        [shard(n) for n, _ in GATHER_BEHIND_CONV])
    w_up, w_down = [by_cols(g_up0)], [by_rows(g_down0)]
    w_o = by_rows(g_wo)
    w_in = jnp.pad(by_cols(g_win), ((0, 0), (0, W_IN_PAD - W_IN_COLS)))
    (m0,) = _matmul(z0, w_pw2, "nn", (F32,), _epi_bias, "conv_pw2", extras=((conv_pw2_b, "row"),))
    x1, h1 = _resnorm_fwd(x0, m0, row(g_mix_post, 0), row(g_ffn_pre, 0), "norm_conv_out")
    act0, mo0 = mlp0_fwd(h1)
    x2, h2 = _resnorm_fwd(x1, mo0, row(g_ffn_post, 0), row(g_mix_pre, 1), "norm_mlp0_out")

    (proj,) = _matmul(h2, w_in[:, :3 * D], "nn", (BF16,), _epi_plain, "attn_in")
    (fl,) = _matmul(h2, w_in[:, 3 * D:], "nn", (F32,), _epi_plain, "attn_gate_in")
    f_all, f1, f2, f3 = _gate_fwd(fl, b_f, "gate_fwd")
    (q_aug, k_aug, v_aug, qn2, kn2), _ = _attn_pack(proj, f1, f2, f3, "attn_pack", (), ())
    tb = min(ATT_TB, T)
    qk = 2.02 * jnp.sqrt(qn2[:, 0, 0] * kn2[:, 0, 0]) + 1.0
    fs, fe = f_all[0::tb, :N_HEADS].T, f_all[tb - 1::tb, :N_HEADS].T
    o_aug, q2_aug = _flash_fwd(qk, fs, fe, jnp.sqrt(kn2[:, 0, 0]), q_aug, k_aug, v_aug, "flash_fwd")
    (o_bf,) = _merge_heads(o_aug, 1.0, BF16, "attn_merge_o")
    (m1,) = _matmul(o_bf, w_o, "nn", (F32,), _epi_plain, "attn_out")
    x3, h3 = _resnorm_fwd(x2, m1, row(g_mix_post, 1), row(g_ffn_pre, 1), "norm_attn_out")
    act1, mo1 = mlp_fwd(h3, 1)
    dy, loss_part = _resnorm_loss(x3, mo1, row(g_ffn_post, 1), target, "loss")
    loss = lax.psum(loss_part[0, 0], ("x", "y", "c"))

    G = {}
    chip = 2 * lax.axis_index("x") + lax.axis_index("y")
    place = jnp.stack([chip] + [k + (k >= chip).astype(jnp.int32) for k in range(N_CHIPS - 1)]
                      + [lax.axis_index("c")]).astype(jnp.int32)

    def mlp_bwd(d_out, mo, act, h, x_in, l):
        d_mo, dg_post, _ = _norm_bwd(mo, row(g_ffn_post, l), d_out, None, BF16, f"mlp{l}_post_bwd")
        (d_up,) = _matmul(d_mo, w_down[l], "nt", (BF16,), _epi_dsqrelu, f"mlp{l}_dact", extras=((act, "tile"),))
        (dw_down,) = _matmul(act, d_mo, "tn", (F32,), _epi_plain, tk=2048, name=f"mlp{l}_dwdown")
        (dw_up,) = _matmul(h, d_up, "tn", (F32,), _epi_plain, tk=2048, name=f"mlp{l}_dwup", shard_out=True)
        (d_h,) = _matmul(d_up, w_up[l], "nt", (F32,), _epi_plain, f"mlp{l}_dh", tk=DFF)
        d_in, dg_pre, _ = _norm_bwd(x_in, row(g_ffn_pre, l), d_h, d_out, F32, f"mlp{l}_pre_bwd")
        G[f"up{l}"], G[f"down{l}"] = dw_up, dw_down.reshape(N_CHIPS, D, D)
        return d_in, dg_post, dg_pre

    d_x3, dg_ffn_post1, dg_ffn_pre1 = mlp_bwd(dy, mo1, act1, h3, x3, 1)

    d_m1, dg_mix_post1, _ = _norm_bwd(m1, row(g_mix_post, 1), d_x3, None, BF16, "attn_post_bwd")
    (dw_o,) = _matmul(o_bf, d_m1, "tn", (F32,), _epi_plain, tk=2048, name="attn_dwo")
    G["attn_w_o"] = dw_o.reshape(N_CHIPS, D // N_CHIPS, D)
    (d_o,) = _matmul(d_m1, w_o, "nt", (F32,), _epi_plain, "attn_do")
    do_aug = _attn_prep(d_o, o_aug, "attn_prep")
    dq_aug = _flash_dq(qk, fs, fe, q2_aug, k_aug, do_aug, v_aug, "flash_dq")
    dk_aug, dv_aug = _flash_dkv(qk, fs, fe, q2_aug, k_aug, do_aug, v_aug, "flash_dkv")
    d_q, dF_q = _merge_heads(dq_aug, 0.125, BF16, "attn_merge_dq", column=AUG_F)
    d_k, dF_k = _merge_heads(dk_aug, 1.0, BF16, "attn_merge_dk", column=AUG_ONE)
    (d_v,) = _merge_heads(dv_aug, 1.0, BF16, "attn_merge_dv")
    d_fl, db_f = _gate_bwd(dF_q, dF_k, fl, b_f, "gate_bwd")
    d_proj = jnp.concatenate([d_q, d_k, d_v, d_fl.astype(BF16)], axis=1)
    (dw_in,) = _matmul(h2, d_proj, "tn", (F32,), _epi_plain, tk=2048, name="attn_dwin", tn=640)
    G["attn_w_in"] = dw_in[:, :W_IN_COLS].reshape(D, N_CHIPS, W_IN_COLS // N_CHIPS).transpose(1, 0, 2)
    (d_h2,) = _matmul(d_proj, w_in, "nt", (F32,), _epi_plain, "attn_dh", tk=W_IN_PAD)
    d_x2, dg_mix_pre1, _ = _norm_bwd(x2, row(g_mix_pre, 1), d_h2, d_x3, F32, "attn_pre_bwd")

    d_x1, dg_ffn_post0, dg_ffn_pre0 = mlp_bwd(d_x2, mo0, act0, h1, x1, 0)

    d_m0, dg_mix_post0, db_pw2 = _norm_bwd(m0, row(g_mix_post, 0), d_x1, None, BF16, "conv_post_bwd")
    (dw_pw2,) = _matmul(z0, d_m0, "tn", (F32,), _epi_plain, tk=2048, name="conv_dwpw2")
    G["conv_pw2_w"] = dw_pw2.reshape(N_CHIPS, D // N_CHIPS, D)
    (d_z0,) = _matmul(d_m0, w_pw2, "nt", (F32,), _epi_plain, "conv_dz")
    conv_units = [u for u in GRAD_UNITS if u[0].startswith("conv")]
    attn_units = [u for u in GRAD_UNITS if u not in conv_units]
    attn_slabs = [G[n] for n, _, _ in attn_units]
    d_y0, dln_g, dln_b, ddw_b, attn_got = _conv_bwd_ln(d_z0, y0, conv_ln_g, conv_ln_b, "conv_bwd_ln", attn_units,
                                                       attn_slabs)
    pairs = {n: _add_pair(place, s, g, f"grad_pair_{n}") for s, g, (n, _, _) in zip(attn_slabs, attn_got, attn_units)}
    d_a0, db_pw1, ddw_w, attn_by_chip = _conv_bwd_dw(d_y0, a0, w_dw, "conv_bwd_dw",
                                                    [pairs[n] for n, _, _ in attn_units])
    by_chip = {n: b for b, (n, _, _) in zip(attn_by_chip, attn_units)}
    (G["conv_pw1_w"],) = _matmul(h0, d_a0, "tn", (F32,), _epi_plain, tk=2048, name="conv_dwpw1", tn=512, shard_out=True)
    (d_h0,) = _matmul(d_a0, w_pw1, "nt", (F32,), _epi_plain, "conv_dh", tk=2 * D)
    d_x0, dg_mix_pre0, _ = _norm_bwd(x0, row(g_mix_pre, 0), d_h0, d_x1, F32, "conv_pre_bwd")

    small_local = {
        "g_mix_pre": jnp.concatenate([dg_mix_pre0, dg_mix_pre1]), "g_mix_post": jnp.concatenate([dg_mix_post0, dg_mix_post1]),
        "g_ffn_pre": jnp.concatenate([dg_ffn_pre0, dg_ffn_pre1]), "g_ffn_post": jnp.concatenate([dg_ffn_post0, dg_ffn_post1]),
        "conv_pw1_b": db_pw1, "conv_dw_b": ddw_b, "conv_ln_g": dln_g, "conv_ln_b": dln_b, "conv_pw2_b": db_pw2,
        "attn_b_f": db_f[:, :N_HEADS], "conv_dw_w_full": ddw_w,
    }

    conv_slabs = [G[n] for n, _, _ in conv_units]
    conv_got = _sibling_exchange(conv_units, conv_slabs, "grad_sibling_exchange_conv")
    pairs.update({n: _add_pair(place, s, g, f"grad_pair_{n}")
                  for s, g, (n, _, _) in zip(conv_slabs, conv_got, conv_units)})
    conv_by_chip, small_all = _chip_exchange([pairs[n] for n, _, _ in conv_units],
                                             _pack([small_local[n] for n, _ in SMALL_ALL], SMALL_ALL_ROWS))
    by_chip.update({n: b for b, (n, _, _) in zip(conv_by_chip, conv_units)})
    unit_grads = _sibling_share([_add_chips(place, pairs[n], by_chip[n], f"grad_chips_{n}") for n, _, _ in GRAD_UNITS])
    grads = _unpack(_add_slots(small_all, "grad_add_small", tb=SMALL_ALL_ROWS), SMALL_ALL)
    grads["conv_dw_w"] = lax.dynamic_slice(grads.pop("conv_dw_w_full"), (0, chip * 256), (CONV_W, 256))[None]

    delta, new_m, new_v = {}, {}, {}
    per_layer = {}
    for g, (unit, shape, (name, layer)) in zip(unit_grads, GRAD_UNITS):
        view = lambda d: (d[name] if layer is None else d[name][layer]).reshape(shape)
        per_layer.setdefault(name, []).append((g,) + tuple(_adamw(view(weights), g, view(mom_m), view(mom_v),
                                                                  f"adamw_{unit}")))
    for name, parts in per_layer.items():
        full = weights[name].shape
        join = lambda n: (parts[0][n] if len(parts) == 1 else jnp.stack([p[n] for p in parts])).reshape(full)
        grads[name], delta[name], new_m[name], new_v[name] = join(0), join(1), join(2), join(3)
    dw2 = lambda d: d["conv_dw_w"].reshape(CONV_W, 256)
    d, nm, nv = _adamw(dw2(weights), dw2(grads), dw2(mom_m), dw2(mom_v), "adamw_conv_dw_w")
    full = weights["conv_dw_w"].shape
    delta["conv_dw_w"], new_m["conv_dw_w"], new_v["conv_dw_w"] = d.reshape(full), nm.reshape(full), nv.reshape(full)
    packs = [_pack([src[n] for n, _ in SMALL], SMALL_ROWS) for src in (weights, grads, mom_m, mom_v)]
    d, nm, nv = _adamw(*packs, "adamw_small")
    for dst, packed in ((delta, d), (new_m, nm), (new_v, nv)):
        dst.update(_unpack(packed, SMALL))

    grad_x = d_x0.reshape(x.shape)
    return (loss, grad_x, *[grads[n] for n in order], *[delta[n] for n in order], *[new_m[n] for n in order],
            *[new_v[n] for n in order])
```

```python
import functools

import jax
import jax.numpy as jnp
from jax import lax
from jax.experimental import pallas as pl
from jax.experimental.pallas import tpu as pltpu

F32 = jnp.float32
BF16 = jnp.bfloat16
MESH = pl.DeviceIdType.MESH

D = 1024
DFF = 4096
N_HEADS = 16
HEAD_DIM = 64
CONV_W = 31
HALO = 32
RMS_EPS = 1e-6
LN_EPS = 1e-5
MASK_VALUE = -1e30
W_IN_COLS = 3 * D + N_HEADS
W_IN_PAD = 3200
LANES = 128
SUBLANES = 8
N_CHIPS = 4
VMEM_LIMIT = 56 * 1024 * 1024
MATMUL_CHUNK = 256

ADAM_LR = 0.001
ADAM_B1 = 0.9
ADAM_B2 = 0.999
ADAM_EPS = 1e-08
ADAM_WD = 0.01
ADAM_STEP = 10

AUG_F = 64
AUG_ONE = 67
AUG_L = 70
SKIP_BELOW = -104.0
FIXED_MAX_BELOW = 40.0

GATHER_AT_START = (("conv_pw1_w", (1024, 512)), ("conv_dw_w", (HALO, 256)), ("conv_pw2_w", (256, 1024)))
GATHER_BEHIND_CONV = (("up0", (1024, 1024)), ("down0", (1024, 1024)), ("attn_w_in", (1024, 772)),
                      ("attn_w_o", (256, 1024)))
GATHER_BEHIND_MLP = ((("up1", (1024, 1024)),), (("down1", (1024, 1024)),))
GRAD_UNITS = (
    ("conv_pw1_w", (1024, 512), ("conv_pw1_w", None)),
    ("conv_pw2_w", (256, 1024), ("conv_pw2_w", None)),
    ("attn_w_in", (1024, 772), ("attn_w_in", None)),
    ("attn_w_o", (256, 1024), ("attn_w_o", None)),
    ("up0", (1024, 1024), ("mlp_w_up", 0)),
    ("up1", (1024, 1024), ("mlp_w_up", 1)),
    ("down0", (1024, 1024), ("mlp_w_down", 0)),
    ("down1", (1024, 1024), ("mlp_w_down", 1)),
)
SMALL = (
    ("g_mix_pre", (2, 1024)), ("g_mix_post", (2, 1024)), ("g_ffn_pre", (2, 1024)), ("g_ffn_post", (2, 1024)),
    ("conv_pw1_b", (1, 2048)), ("conv_dw_b", (1, 1024)), ("conv_ln_g", (1, 1024)), ("conv_ln_b", (1, 1024)),
    ("conv_pw2_b", (1, 1024)), ("attn_b_f", (1, 16)),
)
SMALL_ROWS = 120
SMALL_ALL = SMALL + (("conv_dw_w_full", (HALO, 1024)),)
SMALL_ALL_ROWS = 376


def _size(shape):
    n = 1
    for s in shape:
        n *= s
    return n


def _pack(arrays, rows):
    flat = jnp.concatenate([a.reshape(-1) for a in arrays])
    return jnp.pad(flat, (0, rows * LANES - flat.shape[0])).reshape(rows, LANES)


def _unpack(packed, table):
    flat = packed.reshape(-1)
    out, off = {}, 0
    for name, shape in table:
        n = _size(shape)
        out[name] = flat[off:off + n].reshape(shape)
        off += n
    return out


def _params(sem):
    return pltpu.CompilerParams(dimension_semantics=sem, vmem_limit_bytes=VMEM_LIMIT)


def _rms(x, g):
    return x * lax.rsqrt(jnp.mean(x * x, axis=-1, keepdims=True) + RMS_EPS) * g


def _split3(v):
    p1 = v.astype(BF16).astype(F32)
    r = v - p1
    p2 = r.astype(BF16).astype(F32)
    p3 = (r - p2).astype(BF16).astype(F32)
    return p1, p2, p3


def _matmul(a, b, mode, out_dtypes, epi, name, extras=(), tm=1024, tn=1024, tk=1024, shard_out=False,
            gather_units=(), gather_shards=()):
    (K, M) = a.shape if mode == "tn" else a.shape[::-1]
    N = b.shape[0] if mode == "nt" else b.shape[1]
    tm, tn, tk = min(tm, M), min(tn, N), min(tk, K)
    nk = K // tk
    if mode == "tn":
        a_spec = pl.BlockSpec((tk, tm), lambda i, j, k: (k, i))
    else:
        a_spec = pl.BlockSpec((tm, tk), lambda i, j, k: (i, k))
    if mode == "nt":
        b_spec = pl.BlockSpec((tn, tk), lambda i, j, k: (j, k))
    else:
        b_spec = pl.BlockSpec((tk, tn), lambda i, j, k: (k, j))
    if shard_out:
        per = N // N_CHIPS // tn
        o_spec = pl.BlockSpec((None, tm, tn), lambda i, j, k: (j // per, i, j % per))
        o_shape = (N_CHIPS, M, N // N_CHIPS)
    else:
        o_spec = pl.BlockSpec((tm, tn), lambda i, j, k: (i, j))
        o_shape = (M, N)
    dims = {"nn": (((1,), (0,)), ((), ())), "nt": (((1,), (1,)), ((), ())), "tn": (((0,), (0,)), ((), ()))}[mode]
    ex_specs = []
    for _, kind in extras:
        if kind == "row":
            ex_specs.append(pl.BlockSpec((1, tn), lambda i, j, k: (0, j)))
        else:
            ex_specs.append(pl.BlockSpec((tm, tn), lambda i, j, k: (i, j)))
    n_ex, n_out = len(extras), len(out_dtypes)

    cw = MATMUL_CHUNK if tn % MATMUL_CHUNK == 0 else tn

    n_g = len(gather_units)
    grid = (M // tm, N // tn, nk)

    def body(*refs):
        a_ref, b_ref = refs[0], refs[1]
        ex = refs[2:2 + n_ex]
        g_ins = refs[2 + n_ex:2 + n_ex + n_g]
        outs = refs[2 + n_ex + n_g:2 + n_ex + n_g + n_out]
        g_outs = refs[2 + n_ex + n_g + n_out:2 + n_ex + 2 * n_g + n_out]
        scratch = refs[2 + n_ex + 2 * n_g + n_out:]
        if n_g:
            step = [pl.program_id(d) for d in range(3)]
            at = lambda where: functools.reduce(jnp.logical_and, [s == w for s, w in zip(step, where)])
            pl.when(at((0, 0, 0)))(lambda: _gather_start(gather_units, g_ins, g_outs, scratch[-4:]))

        def for_chunks(use):
            for c in range(tn // cw):
                cols = slice(c * cw, (c + 1) * cw)
                b_chunk = b_ref[cols, :] if mode == "nt" else b_ref[:, cols]
                use(cols, lax.dot_general(a_ref[...], b_chunk, dims, preferred_element_type=F32))

        def finish(cols, acc):
            res = epi(acc, *[e[:, cols] for e in ex])
            for o, r in zip(outs, res):
                o[:, cols] = r.astype(o.dtype)

        if nk == 1:
            for_chunks(finish)
        else:
            acc_ref = scratch[0]
            k = pl.program_id(2)

            def first(cols, prod):
                acc_ref[:, cols] = prod

            def middle(cols, prod):
                acc_ref[:, cols] += prod

            pl.when(k == 0)(lambda: for_chunks(first))
            pl.when(jnp.logical_and(k > 0, k < nk - 1))(lambda: for_chunks(middle))
            pl.when(k == nk - 1)(lambda: for_chunks(lambda cols, prod: finish(cols, acc_ref[:, cols] + prod)))
        if n_g:
            pl.when(at([g - 1 for g in grid]))(lambda: _gather_finish(gather_units, g_ins, g_outs, scratch[-4:]))

    outs = pl.pallas_call(
        body,
        name=name,
        grid=grid,
        in_specs=[a_spec, b_spec] + ex_specs + [ANY] * n_g,
        out_specs=[o_spec for _ in out_dtypes] + [ANY] * n_g,
        out_shape=[jax.ShapeDtypeStruct(o_shape, dt) for dt in out_dtypes]
        + [jax.ShapeDtypeStruct((N_CHIPS,) + shape, BF16) for _, shape in gather_units],
        scratch_shapes=([pltpu.VMEM((tm, tn), F32)] if nk > 1 else []) + (_gather_sems(n_g) if n_g else []),
        compiler_params=_params(("arbitrary",) * 3 if n_g else ("parallel", "parallel", "arbitrary")),
    )(a, b, *[e for e, _ in extras], *gather_shards)
    return (outs[:n_out], outs[n_out:]) if n_g else outs


def _epi_plain(acc):
    return (acc,)


def _epi_bias(acc, bias):
    return (acc + bias,)


def _epi_sqrelu(acc):
    r = jnp.maximum(acc, 0.0)
    return (r * r,)


def _epi_dsqrelu(acc, act):
    a = act.astype(F32)
    return (acc * jnp.where(a > 0.0, 2.0 * a * lax.rsqrt(a), 0.0),)


def _resnorm_fwd(x, m, g_post, g_next, name, tr=512):
    T = x.shape[0]
    has_m = m is not None
    row = pl.BlockSpec((tr, D), lambda i: (i, 0))
    vec = pl.BlockSpec((1, D), lambda i: (0, 0))

    def body(*refs):
        if has_m:
            x_ref, m_ref, gp_ref, gn_ref, x1_ref, h_ref = refs
            x1 = x_ref[...] + _rms(m_ref[...], gp_ref[...])
            x1_ref[...] = x1
        else:
            x_ref, gn_ref, h_ref = refs
            x1 = x_ref[...]
        h_ref[...] = _rms(x1, gn_ref[...]).astype(BF16)

    if has_m:
        args, in_specs = (x, m, g_post, g_next), [row, row, vec, vec]
        out_specs = [row, row]
        out_shape = [jax.ShapeDtypeStruct((T, D), F32), jax.ShapeDtypeStruct((T, D), BF16)]
    else:
        args, in_specs = (x, g_next), [row, vec]
        out_specs = [row]
        out_shape = [jax.ShapeDtypeStruct((T, D), BF16)]
    return pl.pallas_call(
        body, name=name, grid=(T // tr,), in_specs=in_specs, out_specs=out_specs, out_shape=out_shape,
        compiler_params=_params(("parallel",)),
    )(*args)


def _resnorm_loss(x, m, g_post, target, name, tr=512):
    T = x.shape[0]
    row = pl.BlockSpec((tr, D), lambda i: (i, 0))
    vec = pl.BlockSpec((1, D), lambda i: (0, 0))

    def body(x_ref, m_ref, gp_ref, t_ref, dy_ref, loss_ref):
        diff = x_ref[...] + _rms(m_ref[...], gp_ref[...]) - t_ref[...]
        dy_ref[...] = diff * (1.0 / D)

        @pl.when(pl.program_id(0) == 0)
        def _():
            loss_ref[...] = jnp.zeros_like(loss_ref)

        per_token = jnp.mean(diff * diff, axis=-1, keepdims=True)
        loss_ref[...] += 0.5 * jnp.sum(per_token)

    return pl.pallas_call(
        body, name=name, grid=(T // tr,), in_specs=[row, row, vec, row],
        out_specs=[row, pl.BlockSpec((8, LANES), lambda i: (0, 0))],
        out_shape=[jax.ShapeDtypeStruct((T, D), F32), jax.ShapeDtypeStruct((8, LANES), F32)],
        compiler_params=_params(("arbitrary",)),
    )(x, m, g_post, target)


def _norm_bwd(x, g, dy, resid, out_dtype, name, tr=512):
    T = x.shape[0]
    has_r = resid is not None
    row = pl.BlockSpec((tr, D), lambda i: (i, 0))
    vec = pl.BlockSpec((1, D), lambda i: (0, 0))

    def body(*refs):
        if has_r:
            x_ref, g_ref, dy_ref, r_ref, dx_ref, dg_ref, cs_ref = refs
        else:
            x_ref, g_ref, dy_ref, dx_ref, dg_ref, cs_ref = refs
        xv, dyv = x_ref[...], dy_ref[...]
        r = lax.rsqrt(jnp.mean(xv * xv, axis=-1, keepdims=True) + RMS_EPS)
        gy = dyv * g_ref[...]
        c = jnp.sum(gy * xv, axis=-1, keepdims=True) * (1.0 / D)
        dx = r * gy - xv * (r * r * r * c)

        @pl.when(pl.program_id(0) == 0)
        def _():
            dg_ref[...] = jnp.zeros_like(dg_ref)
            cs_ref[...] = jnp.zeros_like(cs_ref)

        dg_ref[...] += jnp.sum(dyv * (xv * r), axis=0, keepdims=True)
        cs_ref[...] += jnp.sum(dx, axis=0, keepdims=True)
        if has_r:
            dx = dx + r_ref[...]
        dx_ref[...] = dx.astype(dx_ref.dtype)

    args = (x, g, dy) + ((resid,) if has_r else ())
    return pl.pallas_call(
        body, name=name, grid=(T // tr,), in_specs=[row, vec, row] + ([row] if has_r else []),
        out_specs=[row, vec, vec],
        out_shape=[jax.ShapeDtypeStruct((T, D), out_dtype), jax.ShapeDtypeStruct((1, D), F32),
                   jax.ShapeDtypeStruct((1, D), F32)],
        compiler_params=_params(("arbitrary",)),
    )(*args)


CONV_TB = 256
CONV_RC = 32


def _glu(a):
    return a[:, :D] * jax.nn.sigmoid(a[:, D:])


def _shifted_copies(win):
    rows = win.shape[1]
    for p in range(1, SUBLANES):
        win[p, 0:rows - SUBLANES, :] = win[0, p:p + rows - SUBLANES, :]


def _window(win, start):
    p = start % SUBLANES
    return win[p, start - p:start - p + CONV_RC, :]


def _conv_fwd(a, dw_w, dw_b, ln_g, ln_b, name, gather_units, gather_shards):
    T = a.shape[0]
    tb = min(CONV_TB, T)
    per = tb // HALO
    n_steps = T // tb
    n_g = len(gather_units)
    vec = pl.BlockSpec((1, D), lambda i: (0, 0))

    def body(*refs):
        cur_ref, prev_ref, w_ref, b_ref, lg_ref, lb_ref = refs[:6]
        g_ins = refs[6:6 + n_g]
        y_ref, z_ref = refs[6 + n_g:8 + n_g]
        g_outs = refs[8 + n_g:8 + 2 * n_g]
        uwin = refs[8 + 2 * n_g]
        sems = refs[9 + 2 * n_g:]
        i = pl.program_id(0)
        pl.when(i == 0)(lambda: _gather_start(gather_units, g_ins, g_outs, sems))
        uwin[0, 0:HALO, :] = jnp.where(i > 0, _glu(prev_ref[...]), 0.0)
        uwin[0, HALO:, :] = _glu(cur_ref[...])
        _shifted_copies(uwin)
        for r in range(tb // CONV_RC):
            acc = jnp.broadcast_to(b_ref[...], (CONV_RC, D))
            for k in range(CONV_W):
                start = r * CONV_RC + HALO - (CONV_W - 1) + k
                acc = acc + _window(uwin, start) * w_ref[k:k + 1, :]
            rows = slice(r * CONV_RC, (r + 1) * CONV_RC)
            y_ref[rows, :] = acc
            mu = jnp.mean(acc, axis=-1, keepdims=True)
            xc = acc - mu
            var = jnp.mean(xc * xc, axis=-1, keepdims=True)
            n = xc * lax.rsqrt(var + LN_EPS) * lg_ref[...] + lb_ref[...]
            z_ref[rows, :] = (n * jax.nn.sigmoid(n)).astype(BF16)
        pl.when(i == n_steps - 1)(lambda: _gather_finish(gather_units, g_ins, g_outs, sems))

    outs = pl.pallas_call(
        body, name=name, grid=(n_steps,),
        in_specs=[pl.BlockSpec((tb, 2 * D), lambda i: (i, 0)),
                  pl.BlockSpec((HALO, 2 * D), lambda i: (jnp.maximum(i * per - 1, 0), 0)),
                  pl.BlockSpec((HALO, D), lambda i: (0, 0)), vec, vec, vec] + [ANY] * n_g,
        out_specs=[pl.BlockSpec((tb, D), lambda i: (i, 0)), pl.BlockSpec((tb, D), lambda i: (i, 0))] + [ANY] * n_g,
        out_shape=[jax.ShapeDtypeStruct((T, D), F32), jax.ShapeDtypeStruct((T, D), BF16)]
        + [jax.ShapeDtypeStruct((N_CHIPS,) + shape, BF16) for _, shape in gather_units],
        scratch_shapes=[pltpu.VMEM((SUBLANES, tb + HALO, D), F32)] + _gather_sems(n_g),
        compiler_params=_params(("arbitrary",)),
    )(a, a, dw_w, dw_b, ln_g, ln_b, *gather_shards)
    return outs[0], outs[1], outs[2:]


def _conv_bwd_ln(dz, y, ln_g, ln_b, name, exchange_units, slabs, tr=256):
    T = y.shape[0]
    tr = min(tr, T)
    n_steps = T // tr
    n_s = len(slabs)
    row = pl.BlockSpec((tr, D), lambda i: (i, 0))
    vec = pl.BlockSpec((1, D), lambda i: (0, 0))

    def body(*refs):
        dz_ref, y_ref, lg_ref, lb_ref = refs[:4]
        s_ins = refs[4:4 + n_s]
        dy_ref, dlg_ref, dlb_ref, db_ref = refs[4 + n_s:8 + n_s]
        s_outs = refs[8 + n_s:8 + 2 * n_s]
        sems = refs[8 + 2 * n_s:]
        pl.when(pl.program_id(0) == 0)(lambda: _sibling_start(exchange_units, s_ins, s_outs, sems))
        yv = y_ref[...]
        mu = jnp.mean(yv, axis=-1, keepdims=True)
        xc = yv - mu
        rstd = lax.rsqrt(jnp.mean(xc * xc, axis=-1, keepdims=True) + LN_EPS)
        yh = xc * rstd
        n = yh * lg_ref[...] + lb_ref[...]
        sg = jax.nn.sigmoid(n)
        dn = dz_ref[...] * (sg * (1.0 + n * (1.0 - sg)))
        dyh = dn * lg_ref[...]
        dyv = rstd * (dyh - jnp.mean(dyh, axis=-1, keepdims=True) - yh * jnp.mean(dyh * yh, axis=-1, keepdims=True))
        dy_ref[...] = dyv

        @pl.when(pl.program_id(0) == 0)
        def _():
            dlg_ref[...] = jnp.zeros_like(dlg_ref)
            dlb_ref[...] = jnp.zeros_like(dlb_ref)
            db_ref[...] = jnp.zeros_like(db_ref)

        dlg_ref[...] += jnp.sum(dn * yh, axis=0, keepdims=True)
        dlb_ref[...] += jnp.sum(dn, axis=0, keepdims=True)
        db_ref[...] += jnp.sum(dyv, axis=0, keepdims=True)
        pl.when(pl.program_id(0) == n_steps - 1)(lambda: _sibling_finish(exchange_units, s_ins, s_outs, sems))

    outs = pl.pallas_call(
        body, name=name, grid=(n_steps,), in_specs=[row, row, vec, vec] + [ANY] * n_s,
        out_specs=[row, vec, vec, vec] + [ANY] * n_s,
        out_shape=[jax.ShapeDtypeStruct((T, D), F32)] + [jax.ShapeDtypeStruct((1, D), F32)] * 3
        + [jax.ShapeDtypeStruct((N_CHIPS, rows // 2, cols), F32) for _, (rows, cols), _ in exchange_units],
        scratch_shapes=[pltpu.SemaphoreType.DMA((n_s,)), pltpu.SemaphoreType.DMA((n_s,))],
        compiler_params=_params(("arbitrary",)),
    )(dz, y, ln_g, ln_b, *slabs)
    return outs[0], outs[1], outs[2], outs[3], outs[4:]


def _conv_bwd_dw(dy, a, dw_w, name, exchange):
    T = a.shape[0]
    tb = min(CONV_TB, T)
    per = tb // HALO
    last_halo = T // HALO - 1
    n_steps = T // tb
    n_x = len(exchange)

    def body(*refs):
        dyc_ref, dyn_ref, cur_ref, prev_ref, w_ref = refs[:5]
        x_ins = refs[5:5 + n_x]
        da_ref, dbias_ref, dw_ref = refs[5 + n_x:8 + n_x]
        x_outs = refs[8 + n_x:8 + 2 * n_x]
        uwin, dywin, dwacc = refs[8 + 2 * n_x:11 + 2 * n_x]
        sems = refs[11 + 2 * n_x:]
        i = pl.program_id(0)

        @pl.when(i == 0)
        def _():
            _exchange_start(x_ins, x_outs, sems)
            dbias_ref[...] = jnp.zeros_like(dbias_ref)
            dwacc[...] = jnp.zeros_like(dwacc)

        uwin[0, 0:HALO, :] = jnp.where(i > 0, _glu(prev_ref[...]), 0.0)
        uwin[0, HALO:, :] = _glu(cur_ref[...])
        dywin[0, 0:tb, :] = dyc_ref[...]
        dywin[0, tb:, :] = jnp.where(i < n_steps - 1, dyn_ref[...], 0.0)
        _shifted_copies(uwin)
        _shifted_copies(dywin)
        for r in range(tb // CONV_RC):
            rows = slice(r * CONV_RC, (r + 1) * CONV_RC)
            dy_c = dywin[0, rows, :]
            du = jnp.zeros((CONV_RC, D), F32)
            for k in range(CONV_W):
                du = du + _window(dywin, r * CONV_RC + (CONV_W - 1) - k) * w_ref[k:k + 1, :]
                prod = dy_c * _window(uwin, r * CONV_RC + HALO - (CONV_W - 1) + k)
                part = prod[0:8, :]
                for q in range(1, CONV_RC // 8):
                    part = part + prod[8 * q:8 * q + 8, :]
                dwacc[8 * k:8 * k + 8, :] += part
            av = cur_ref[rows, :]
            a1, sg = av[:, :D], jax.nn.sigmoid(av[:, D:])
            da1 = du * sg
            da2 = du * a1 * (sg * (1.0 - sg))
            da_ref[rows, 0:D] = da1.astype(BF16)
            da_ref[rows, D:] = da2.astype(BF16)
            dbias_ref[:, 0:D] += jnp.sum(da1, axis=0, keepdims=True)
            dbias_ref[:, D:] += jnp.sum(da2, axis=0, keepdims=True)

        @pl.when(i == n_steps - 1)
        def _():
            for k in range(CONV_W):
                dw_ref[k:k + 1, :] = jnp.sum(dwacc[8 * k:8 * k + 8, :], axis=0, keepdims=True)
            dw_ref[CONV_W:, :] = jnp.zeros((HALO - CONV_W, D), F32)
            _exchange_finish(x_ins, x_outs, sems)

    outs = pl.pallas_call(
        body, name=name, grid=(n_steps,),
        in_specs=[pl.BlockSpec((tb, D), lambda i: (i, 0)),
                  pl.BlockSpec((HALO, D), lambda i: (jnp.minimum((i + 1) * per, last_halo), 0)),
                  pl.BlockSpec((tb, 2 * D), lambda i: (i, 0)),
                  pl.BlockSpec((HALO, 2 * D), lambda i: (jnp.maximum(i * per - 1, 0), 0)),
                  pl.BlockSpec((HALO, D), lambda i: (0, 0))] + [ANY] * n_x,
        out_specs=[pl.BlockSpec((tb, 2 * D), lambda i: (i, 0)), pl.BlockSpec((1, 2 * D), lambda i: (0, 0)),
                   pl.BlockSpec((HALO, D), lambda i: (0, 0))] + [ANY] * n_x,
        out_shape=[jax.ShapeDtypeStruct((T, 2 * D), BF16), jax.ShapeDtypeStruct((1, 2 * D), F32),
                   jax.ShapeDtypeStruct((HALO, D), F32)] + [jax.ShapeDtypeStruct(p.shape, p.dtype) for p in exchange],
        scratch_shapes=[pltpu.VMEM((SUBLANES, tb + HALO, D), F32), pltpu.VMEM((SUBLANES, tb + HALO, D), F32),
                        pltpu.VMEM((8 * HALO, D), F32)] + _exchange_sems(n_x),
        compiler_params=_params(("arbitrary",)),
    )(dy, dy, a, a, dw_w, *exchange)
    return outs[0], outs[1], outs[2], outs[3:]


GATE_TB = 512


def _gate_fwd(fl, b_f, name):
    T = fl.shape[0]
    tb = min(GATE_TB, T)
    row = pl.BlockSpec((tb, LANES), lambda i: (i, 0))

    def body(fl_ref, b_ref, f_ref, f1_ref, f2_ref, f3_ref, carry):
        @pl.when(pl.program_id(0) == 0)
        def _():
            carry[...] = jnp.zeros_like(carry)

        z = fl_ref[...] + b_ref[...]
        lf = jnp.minimum(z, 0.0) - jnp.log(1.0 + jnp.exp(-jnp.abs(z)))
        tri = (lax.broadcasted_iota(jnp.int32, (tb, tb), 0) >= lax.broadcasted_iota(jnp.int32, (tb, tb), 1)).astype(F32)
        f = jnp.dot(tri, lf, precision=lax.Precision.HIGHEST, preferred_element_type=F32) + carry[...]
        carry[...] = f[tb - 1:tb, :]
        f_ref[...] = f
        f1_ref[...], f2_ref[...], f3_ref[...] = _split3(f)

    return pl.pallas_call(
        body, name=name, grid=(T // tb,), in_specs=[row, pl.BlockSpec((1, LANES), lambda i: (0, 0))],
        out_specs=[row] * 4, out_shape=[jax.ShapeDtypeStruct((T, LANES), F32)] * 4,
        scratch_shapes=[pltpu.VMEM((1, LANES), F32)],
        compiler_params=_params(("arbitrary",)),
    )(fl, b_f)


def _gate_bwd(dF_q, dF_k, fl, b_f, name):
    T = fl.shape[0]
    tb = min(GATE_TB, T)
    nb = T // tb
    row = pl.BlockSpec((tb, LANES), lambda i: (nb - 1 - i, 0))
    vec = pl.BlockSpec((1, LANES), lambda i: (0, 0))

    def body(dfq_ref, dfk_ref, fl_ref, b_ref, dfl_ref, db_ref, carry):
        @pl.when(pl.program_id(0) == 0)
        def _():
            carry[...] = jnp.zeros_like(carry)
            db_ref[...] = jnp.zeros_like(db_ref)

        tri = (lax.broadcasted_iota(jnp.int32, (tb, tb), 0) <= lax.broadcasted_iota(jnp.int32, (tb, tb), 1)).astype(F32)
        rc = jnp.dot(tri, dfq_ref[...] - dfk_ref[...], precision=lax.Precision.HIGHEST,
                     preferred_element_type=F32) + carry[...]
        carry[...] = rc[0:1, :]
        dfl = rc * jax.nn.sigmoid(-(fl_ref[...] + b_ref[...]))
        dfl_ref[...] = dfl
        db_ref[...] += jnp.sum(dfl, axis=0, keepdims=True)

    return pl.pallas_call(
        body, name=name, grid=(nb,), in_specs=[row, row, row, vec], out_specs=[row, vec],
        out_shape=[jax.ShapeDtypeStruct((T, LANES), F32), jax.ShapeDtypeStruct((1, LANES), F32)],
        scratch_shapes=[pltpu.VMEM((1, LANES), F32)],
        compiler_params=_params(("arbitrary",)),
    )(dF_q, dF_k, fl, b_f)


ATT_TB = 512
LAYOUT_TB = 2048
NT = (((1,), (1,)), ((), ()))
NN = (((1,), (0,)), ((), ()))


def _lane_insert(base, parts, first_lane):
    lane = lax.broadcasted_iota(jnp.int32, base.shape, 1)
    out = base
    for n, p in enumerate(parts):
        out = jnp.where(lane == first_lane + n, p, out)
    return out


def _causal(tb):
    return lax.broadcasted_iota(jnp.int32, (tb, tb), 0) >= lax.broadcasted_iota(jnp.int32, (tb, tb), 1)


def _seen_by(n_keys, n_queries):
    return (lax.broadcasted_iota(jnp.int32, (n_keys, n_queries), 1)
            >= lax.broadcasted_iota(jnp.int32, (n_keys, n_queries), 0))


def _other_head(x):
    return pltpu.roll(x, HEAD_DIM, 1)


def _attn_pack(proj, f1, f2, f3, name, gather_units, gather_shards):
    T = proj.shape[0]
    tb = min(LAYOUT_TB, T)
    pairs = N_HEADS // 2
    n_steps = T // tb
    n_g = len(gather_units)
    cols = lambda first: pl.BlockSpec((tb, LANES), lambda p, i: (i, first + p))
    gate = pl.BlockSpec((tb, LANES), lambda p, i: (i, 0))
    heads = pl.BlockSpec((2, tb, LANES), lambda p, i: (p, i, 0))
    norm = pl.BlockSpec((2, 8, LANES), lambda p, i: (p, 0, 0))

    def body(*refs):
        q_ref, k_ref, v_ref, f1_ref, f2_ref, f3_ref = refs[:6]
        g_ins = refs[6:6 + n_g]
        qa_ref, ka_ref, va_ref, qn_ref, kn_ref = refs[6 + n_g:11 + n_g]
        g_outs = refs[11 + n_g:11 + 2 * n_g]
        sems = refs[11 + 2 * n_g:]
        p, i = pl.program_id(0), pl.program_id(1)
        if n_g:
            pl.when(jnp.logical_and(p == 0, i == 0))(lambda: _gather_start(gather_units, g_ins, g_outs, sems))
        lane = lax.broadcasted_iota(jnp.int32, (tb, LANES), 1)
        data = lane < HEAD_DIM

        @pl.when(i == 0)
        def _():
            qn_ref[...] = jnp.zeros_like(qn_ref)
            kn_ref[...] = jnp.zeros_like(kn_ref)

        qv, kv, vv = q_ref[...].astype(F32) * 0.125, k_ref[...].astype(F32), v_ref[...].astype(F32)
        for e in range(2):
            fcol = [jnp.sum(jnp.where(lane == 2 * p + e, f[...], 0.0), axis=-1, keepdims=True)
                    for f in (f1_ref, f2_ref, f3_ref)]
            pick = (lambda t: t) if e == 0 else _other_head
            qd = jnp.where(data, pick(qv), 0.0).astype(BF16).astype(F32)
            kd = jnp.where(data, pick(kv), 0.0).astype(BF16).astype(F32)
            vd = jnp.where(data, pick(vv), 0.0)
            qa_ref[e] = _lane_insert(qd, fcol + [1.0, 1.0, 1.0], AUG_F).astype(BF16)
            ka_ref[e] = _lane_insert(kd, [1.0, 1.0, 1.0] + [-f for f in fcol] + [1.0, 1.0, 1.0], AUG_F).astype(BF16)
            va_ref[e] = _lane_insert(vd, [-1.0, -1.0, -1.0, 1.0], AUG_F).astype(BF16)
            qn_ref[e] = jnp.maximum(qn_ref[e], jnp.max(jnp.sum(qd * qd, axis=-1, keepdims=True)))
            kn_ref[e] = jnp.maximum(kn_ref[e], jnp.max(jnp.sum(kd * kd, axis=-1, keepdims=True)))
        if n_g:
            pl.when(jnp.logical_and(p == pairs - 1, i == n_steps - 1))(
                lambda: _gather_finish(gather_units, g_ins, g_outs, sems))

    aug = jax.ShapeDtypeStruct((N_HEADS, T, LANES), BF16)
    nrm = jax.ShapeDtypeStruct((N_HEADS, 8, LANES), F32)
    outs = pl.pallas_call(
        body, name=name, grid=(pairs, n_steps),
        in_specs=[cols(0), cols(pairs), cols(2 * pairs), gate, gate, gate] + [ANY] * n_g,
        out_specs=[heads, heads, heads, norm, norm] + [ANY] * n_g,
        out_shape=[aug, aug, aug, nrm, nrm]
        + [jax.ShapeDtypeStruct((N_CHIPS,) + shape, BF16) for _, shape in gather_units],
        scratch_shapes=_gather_sems(n_g) if n_g else [],
        compiler_params=_params(("arbitrary", "arbitrary") if n_g else ("parallel", "arbitrary")),
    )(proj, proj, proj, f1, f2, f3, *gather_shards)
    return outs[:5], outs[5:]


def _merge_heads(x_aug, scale, out_dtype, name, column=None):
    H, T, _ = x_aug.shape
    tb = min(LAYOUT_TB, T)

    def body(x_ref, o_ref, *col_ref):
        p = pl.program_id(1)
        lane = lax.broadcasted_iota(jnp.int32, (tb, LANES), 1)
        x0, x1 = x_ref[0], x_ref[1]
        o_ref[...] = (jnp.where(lane < HEAD_DIM, x0, _other_head(x1)) * scale).astype(out_dtype)
        if column is not None:
            @pl.when(p == 0)
            def _():
                col_ref[0][...] = jnp.zeros_like(col_ref[0])

            c0 = jnp.sum(jnp.where(lane == column, x0, 0.0), axis=-1, keepdims=True)
            c1 = jnp.sum(jnp.where(lane == column, x1, 0.0), axis=-1, keepdims=True)
            col_ref[0][...] += jnp.where(lane == 2 * p, c0, 0.0) + jnp.where(lane == 2 * p + 1, c1, 0.0)

    out_specs = [pl.BlockSpec((tb, LANES), lambda i, p: (i, p))]
    out_shape = [jax.ShapeDtypeStruct((T, D), out_dtype)]
    if column is not None:
        out_specs.append(pl.BlockSpec((tb, LANES), lambda i, p: (i, 0)))
        out_shape.append(jax.ShapeDtypeStruct((T, LANES), F32))
    return pl.pallas_call(
        body, name=name, grid=(T // tb, H // 2),
        in_specs=[pl.BlockSpec((2, tb, LANES), lambda i, p: (p, i, 0))],
        out_specs=out_specs, out_shape=out_shape,
        compiler_params=_params(("parallel", "arbitrary")),
    )(x_aug)


def _attn_prep(d_o, o_aug, name):
    H, T, _ = o_aug.shape
    tb = min(LAYOUT_TB, T)
    heads = pl.BlockSpec((2, tb, LANES), lambda p, i: (p, i, 0))

    def body(do_ref, o_ref, out_ref):
        lane = lax.broadcasted_iota(jnp.int32, (tb, LANES), 1)
        dov = do_ref[...]
        for e in range(2):
            d_e = jnp.where(lane < HEAD_DIM, dov if e == 0 else _other_head(dov), 0.0)
            delta = jnp.sum(d_e * o_ref[e], axis=-1, keepdims=True)
            out_ref[e] = _lane_insert(d_e, _split3(delta), AUG_F).astype(BF16)

    return pl.pallas_call(
        body, name=name, grid=(H // 2, T // tb),
        in_specs=[pl.BlockSpec((tb, LANES), lambda p, i: (i, p)), heads], out_specs=heads,
        out_shape=jax.ShapeDtypeStruct((H, T, LANES), BF16),
        compiler_params=_params(("parallel", "parallel")),
    )(d_o, o_aug)


def _flash_spec(n_in, n_out, H, nb, tb, T, resident, scratch, n_tables=3):
    blk = pl.BlockSpec((2, tb, LANES), lambda h, i, *_: (h, i, 0))
    whole = pl.BlockSpec((2, T, LANES), lambda h, i, *_: (h, 0, 0))
    return pltpu.PrefetchScalarGridSpec(
        num_scalar_prefetch=n_tables, grid=(H // 2, nb),
        in_specs=[whole if resident[n] else blk for n in range(n_in)],
        out_specs=[blk] * n_out, scratch_shapes=scratch)


def _first_live(qk_ref, fs_ref, fe_ref, h, i):
    top = qk_ref[h] + fs_ref[h, i]
    return lax.fori_loop(0, i, lambda j, n: n + jnp.where(top - fe_ref[h, j] < SKIP_BELOW, 1, 0), 0)


def _flash_fwd(qk, fs, fe, kmax, q_aug, k_aug, v_aug, name):
    H, T, _ = q_aug.shape
    tb = min(ATT_TB, T)

    def body(qk_ref, fs_ref, fe_ref, km_ref, q_ref, k_ref, v_ref, o_ref, q2_ref, m_s, acc_s, qm_s):
        h, i = 2 * pl.program_id(0), pl.program_id(1)
        acc_s[...] = jnp.zeros_like(acc_s)
        first = jnp.minimum(_first_live(qk_ref, fs_ref, fe_ref, h, i), _first_live(qk_ref, fs_ref, fe_ref, h + 1, i))
        bounded = jnp.logical_and(qk_ref[h] < FIXED_MAX_BELOW, qk_ref[h + 1] < FIXED_MAX_BELOW)

        def sweep(step):
            def loop_body(kb, carry):
                step(kb, False)
                return carry

            lax.fori_loop(first, i, loop_body, 0)
            step(i, True)

        @pl.when(bounded)
        def _():
            lane = lax.broadcasted_iota(jnp.int32, (tb, LANES), 1)
            for e in range(2):
                qf = q_ref[e].astype(F32)
                norm = jnp.sqrt(jnp.sum(jnp.where(lane < HEAD_DIM, qf * qf, 0.0), axis=-1, keepdims=True))
                bound = norm * (1.01 * km_ref[h + e]) + 1e-3
                m_s[e] = bound
                qm_s[e] = _lane_insert(qf, [-p for p in _split3(bound)], AUG_L).astype(BF16)

            def step(kb, masked):
                rows = pl.ds(pl.multiple_of(kb * tb, tb), tb)
                for e in range(2):
                    p = jnp.exp(lax.dot_general(qm_s[e], k_ref[e, rows, :], NT, preferred_element_type=F32))
                    if masked:
                        p = jnp.where(_causal(tb), p, 0.0)
                    acc_s[e] += lax.dot_general(p.astype(BF16), v_ref[e, rows, :], NN, preferred_element_type=F32)

            sweep(step)

        @pl.when(jnp.logical_not(bounded))
        def _():
            m_s[...] = jnp.full(m_s.shape, -jnp.inf, F32)

            def step(kb, masked):
                rows = pl.ds(pl.multiple_of(kb * tb, tb), tb)
                for e in range(2):
                    s = lax.dot_general(q_ref[e], k_ref[e, rows, :], NT, preferred_element_type=F32)
                    if masked:
                        s = jnp.where(_causal(tb), s, MASK_VALUE)
                    m_old = m_s[e]
                    m_new = jnp.maximum(m_old, jnp.max(s, axis=-1, keepdims=True))
                    p = jnp.exp(s - m_new)
                    acc_s[e] = jnp.exp(m_old - m_new) * acc_s[e] + lax.dot_general(
                        p.astype(BF16), v_ref[e, rows, :], NN, preferred_element_type=F32)
                    m_s[e] = m_new

            sweep(step)

        for e in range(2):
            acc = acc_s[e]
            lane = lax.broadcasted_iota(jnp.int32, acc.shape, 1)
            l = jnp.sum(jnp.where(lane == AUG_ONE, acc, 0.0), axis=-1, keepdims=True)
            o_ref[e] = acc / l
            lse = m_s[e] + jnp.log(l)
            q2_ref[e] = _lane_insert(q_ref[e].astype(F32), [-p for p in _split3(lse)], AUG_L).astype(BF16)

    return pl.pallas_call(
        body, name=name,
        grid_spec=_flash_spec(3, 2, H, T // tb, tb, T, (False, True, True),
                              [pltpu.VMEM((2, tb, 1), F32), pltpu.VMEM((2, tb, LANES), F32),
                               pltpu.VMEM((2, tb, LANES), BF16)], n_tables=4),
        out_shape=[jax.ShapeDtypeStruct((H, T, LANES), F32), jax.ShapeDtypeStruct((H, T, LANES), BF16)],
        compiler_params=_params(("parallel", "arbitrary")),
    )(qk, fs, fe, kmax, q_aug, k_aug, v_aug)


def _flash_dq(qk, fs, fe, q2, k_aug, do_aug, v_aug, name):
    H, T, _ = q2.shape
    tb = min(ATT_TB, T)

    def body(qk_ref, fs_ref, fe_ref, q_ref, k_ref, do_ref, v_ref, dq_ref, acc_s):
        h, i = 2 * pl.program_id(0), pl.program_id(1)
        acc_s[...] = jnp.zeros_like(acc_s)

        def step(kb, masked):
            rows = pl.ds(pl.multiple_of(kb * tb, tb), tb)
            for e in range(2):
                k = k_ref[e, rows, :]
                p = jnp.exp(lax.dot_general(q_ref[e], k, NT, preferred_element_type=F32))
                if masked:
                    p = jnp.where(_causal(tb), p, 0.0)
                ds = p * lax.dot_general(do_ref[e], v_ref[e, rows, :], NT, preferred_element_type=F32)
                acc_s[e] += lax.dot_general(ds.astype(BF16), k, NN, preferred_element_type=F32)

        def loop_body(kb, carry):
            step(kb, False)
            return carry

        first = jnp.minimum(_first_live(qk_ref, fs_ref, fe_ref, h, i), _first_live(qk_ref, fs_ref, fe_ref, h + 1, i))
        lax.fori_loop(first, i, loop_body, 0)
        step(i, True)
        dq_ref[...] = acc_s[...]

    return pl.pallas_call(
        body, name=name,
        grid_spec=_flash_spec(4, 1, H, T // tb, tb, T, (False, True, False, True), [pltpu.VMEM((2, tb, LANES), F32)]),
        out_shape=[jax.ShapeDtypeStruct((H, T, LANES), F32)],
        compiler_params=_params(("parallel", "arbitrary")),
    )(qk, fs, fe, q2, k_aug, do_aug, v_aug)[0]


def _flash_dkv(qk, fs, fe, q2, k_aug, do_aug, v_aug, name):
    H, T, _ = q2.shape
    tb = min(ATT_TB, T)
    half = tb // 2
    nb = T // tb

    def body(qk_ref, fs_ref, fe_ref, q_ref, k_ref, do_ref, v_ref, dk_ref, dv_ref, dk_s, dv_s):
        h, i = 2 * pl.program_id(0), pl.program_id(1)
        dk_s[...] = jnp.zeros_like(dk_s)
        dv_s[...] = jnp.zeros_like(dv_s)

        def step(qb, masked):
            base = pl.multiple_of(qb * tb, tb)
            parts = [(slice(0, half), 0, tb), (slice(half, tb), half, half)] if masked else [(slice(0, tb), 0, tb)]
            for e in range(2):
                for krows, q0, nq in parts:
                    queries = pl.ds(pl.multiple_of(base + q0, half), nq)
                    q, dov = q_ref[e, queries, :], do_ref[e, queries, :]
                    pt = jnp.exp(lax.dot_general(k_ref[e, krows, :], q, NT, preferred_element_type=F32))
                    if masked:
                        pt = jnp.where(_seen_by(krows.stop - krows.start, nq), pt, 0.0)
                    dst = pt * lax.dot_general(v_ref[e, krows, :], dov, NT, preferred_element_type=F32)
                    dv_s[e, krows, :] += lax.dot_general(pt.astype(BF16), dov, NN, preferred_element_type=F32)
                    dk_s[e, krows, :] += lax.dot_general(dst.astype(BF16), q, NN, preferred_element_type=F32)

        step(i, True)

        def loop_body(qb, carry):
            step(qb, False)
            return carry

        def live_after(head):
            base = qk_ref[head] - fe_ref[head, i]
            return lax.fori_loop(i + 1, nb, lambda b, n: n + jnp.where(base + fs_ref[head, b] < SKIP_BELOW, 0, 1), 0)

        lax.fori_loop(i + 1, i + 1 + jnp.maximum(live_after(h), live_after(h + 1)), loop_body, 0)
        dk_ref[...] = dk_s[...]
        dv_ref[...] = dv_s[...]

    return pl.pallas_call(
        body, name=name,
        grid_spec=_flash_spec(4, 2, H, nb, tb, T, (True, False, True, False),
                              [pltpu.VMEM((2, tb, LANES), F32), pltpu.VMEM((2, tb, LANES), F32)]),
        out_shape=[jax.ShapeDtypeStruct((H, T, LANES), F32)] * 2,
        compiler_params=_params(("parallel", "arbitrary")),
    )(qk, fs, fe, q2, k_aug, do_aug, v_aug)


def _add_pair(place, slab, got, name):
    _, half, cols = got.shape
    grid_spec = pltpu.PrefetchScalarGridSpec(
        num_scalar_prefetch=1, grid=(N_CHIPS,),
        in_specs=[pl.BlockSpec((None, half, cols), lambda j, s: (j, s[4], 0)),
                  pl.BlockSpec((None, half, cols), lambda j, s: (j, 0, 0))],
        out_specs=pl.BlockSpec((None, half, cols), lambda j, s: (j, 0, 0)))

    def body(s_ref, a_ref, b_ref, o_ref):
        o_ref[...] = (a_ref[...] + b_ref[...]).astype(BF16)

    return pl.pallas_call(
        body, name=name, grid_spec=grid_spec, out_shape=jax.ShapeDtypeStruct(got.shape, BF16),
        compiler_params=_params(("parallel",)),
    )(place, slab, got)


def _add_chips(place, pair, by_chip, name):
    _, half, cols = pair.shape
    tb = min(256, half)
    steps = half // tb
    slot = lambda n: pl.BlockSpec((None, tb, cols), lambda i, s: (s[n], i, 0))
    grid_spec = pltpu.PrefetchScalarGridSpec(
        num_scalar_prefetch=1, grid=(steps,), in_specs=[slot(0), slot(1), slot(2), slot(3)],
        out_specs=pl.BlockSpec((tb, cols), lambda i, s: (s[4] * steps + i, 0)))

    def body(s_ref, own_ref, b1_ref, b2_ref, b3_ref, o_ref):
        o_ref[...] = ((own_ref[...].astype(F32) + b1_ref[...].astype(F32)) + b2_ref[...].astype(F32)) \
            + b3_ref[...].astype(F32)

    return pl.pallas_call(
        body, name=name, grid_spec=grid_spec, out_shape=jax.ShapeDtypeStruct((2 * half, cols), F32),
        compiler_params=_params(("parallel",)),
    )(place, pair, by_chip, by_chip, by_chip)


def _add_slots(b, name, tb=128):
    n, rows, cols = b.shape
    tb = tb if rows % tb == 0 else rows
    specs = [pl.BlockSpec((None, tb, cols), functools.partial(lambda j, i: (j, i, 0), j)) for j in range(n)]

    def body(*refs):
        acc = refs[0][...]
        for r in refs[1:n]:
            acc = acc + r[...]
        refs[n][...] = acc

    return pl.pallas_call(
        body, name=name, grid=(rows // tb,), in_specs=specs, out_specs=pl.BlockSpec((tb, cols), lambda i: (i, 0)),
        out_shape=jax.ShapeDtypeStruct((rows, cols), F32), compiler_params=_params(("parallel",)),
    )(*([b] * n))


def _adamw(w, g, m, v, name):
    rows, cols = w.shape
    tr = 256 if rows % 256 == 0 else rows
    blk = pl.BlockSpec((tr, cols), lambda i: (i, 0))

    def body(w_ref, g_ref, m_ref, v_ref, d_ref, nm_ref, nv_ref):
        gv = g_ref[...]
        nm = ADAM_B1 * m_ref[...] + (1.0 - ADAM_B1) * gv
        nv = ADAM_B2 * v_ref[...] + (1.0 - ADAM_B2) * (gv * gv)
        m_hat = nm / (1.0 - ADAM_B1 ** ADAM_STEP)
        v_hat = nv / (1.0 - ADAM_B2 ** ADAM_STEP)
        d_ref[...] = -ADAM_LR * (m_hat / (jnp.sqrt(v_hat) + ADAM_EPS) + ADAM_WD * w_ref[...])
        nm_ref[...] = nm
        nv_ref[...] = nv

    return pl.pallas_call(
        body, name=name, grid=(rows // tr,), in_specs=[blk] * 4, out_specs=[blk] * 3,
        out_shape=[jax.ShapeDtypeStruct((rows, cols), F32)] * 3, compiler_params=_params(("parallel",)),
    )(w, g, m, v)


ANY = pl.BlockSpec(memory_space=pl.ANY)


def _place():
    x, y, c = lax.axis_index("x"), lax.axis_index("y"), lax.axis_index("c")
    others = [(1 - x, y), (x, 1 - y), (1 - x, 1 - y)]
    return x, y, c, 2 * x + y, others


def _half(ref, rows, h):
    return ref.at[pl.ds(h * (rows // 2), rows // 2), :]


def _gather_sems(n_u):
    return [pltpu.SemaphoreType.DMA((6 * n_u,)), pltpu.SemaphoreType.DMA((6 * n_u,)),
            pltpu.SemaphoreType.DMA((n_u,)), pltpu.SemaphoreType.DMA((n_u,))]


def _gather_copies(units, ins, outs, sems):
    send_sems, recv_sems, own_send_sems, own_recv_sems = sems
    x, y, c, me, others = _place()
    sibling = (x, y, 1 - c)

    def copy(u, k, chip, h, to, src=None):
        dst = _half(outs[u].at[chip], units[u][1][0], h)
        return pltpu.make_async_remote_copy(
            src_ref=dst if src is None else src, dst_ref=dst, send_sem=send_sems.at[6 * u + k],
            recv_sem=recv_sems.at[6 * u + k], device_id=to, device_id_type=MESH)

    pairs = [(u, j, 2 * ox + oy, (ox, oy)) for u in range(len(units)) for j, (ox, oy) in enumerate(others)]
    own = [pltpu.make_async_remote_copy(
        src_ref=ins[u], dst_ref=outs[u].at[me], send_sem=own_send_sems.at[u], recv_sem=own_recv_sems.at[u],
        device_id=sibling, device_id_type=MESH) for u in range(len(units))]
    first = [copy(u, j, me, c, (*to, c), src=_half(ins[u], units[u][1][0], c)) for u, j, _, to in pairs]
    arrivals = [copy(u, j, chip, c, (x, y, c)) for u, j, chip, _ in pairs]
    passed = [copy(u, 3 + j, chip, c, sibling) for u, j, chip, _ in pairs]
    from_sibling = [copy(u, 3 + j, chip, 1 - c, (x, y, c)) for u, j, chip, _ in pairs]
    return own, first, arrivals, passed, from_sibling


def _gather_start(units, ins, outs, sems):
    own, first, _, _, _ = _gather_copies(units, ins, outs, sems)
    for cp in own + first:
        cp.start()


def _gather_finish(units, ins, outs, sems):
    own, first, arrivals, passed, from_sibling = _gather_copies(units, ins, outs, sems)
    for arrived, onward in zip(arrivals, passed):
        arrived.wait_recv()
        onward.start()
    for cp in from_sibling:
        cp.wait_recv()
    for cp in first + passed:
        cp.wait_send()
    for cp in own:
        cp.wait()


def _allgather_weights(units, shards, name):
    n_u = len(units)

    def body(*refs):
        ins, outs, sems = refs[:n_u], refs[n_u:2 * n_u], refs[2 * n_u:]
        _gather_start(units, ins, outs, sems)
        _gather_finish(units, ins, outs, sems)

    return pl.pallas_call(
        body, name=name, in_specs=[ANY] * n_u, out_specs=[ANY] * n_u,
        out_shape=[jax.ShapeDtypeStruct((N_CHIPS,) + shape, BF16) for _, shape in units],
        scratch_shapes=_gather_sems(n_u),
    )(*shards)


def _sibling_copies(units, ins, got, sems):
    send_sems, recv_sems = sems
    x, y, c, _, _ = _place()
    copies = []
    for u, (_, (rows, _), _) in enumerate(units):
        half = rows // 2
        copies.append(pltpu.make_async_remote_copy(
            src_ref=ins[u].at[:, pl.ds((1 - c) * half, half), :], dst_ref=got[u], send_sem=send_sems.at[u],
            recv_sem=recv_sems.at[u], device_id=(x, y, 1 - c), device_id_type=MESH))
    return copies


def _sibling_start(units, ins, got, sems):
    for cp in _sibling_copies(units, ins, got, sems):
        cp.start()


def _sibling_finish(units, ins, got, sems):
    for cp in _sibling_copies(units, ins, got, sems):
        cp.wait()


def _sibling_exchange(units, slabs, name):
    n_u = len(units)

    def body(*refs):
        ins, got, sems = refs[:n_u], refs[n_u:2 * n_u], refs[2 * n_u:]
        _sibling_start(units, ins, got, sems)
        _sibling_finish(units, ins, got, sems)

    return pl.pallas_call(
        body, name=name, in_specs=[ANY] * n_u, out_specs=[ANY] * n_u,
        out_shape=[jax.ShapeDtypeStruct((N_CHIPS, rows // 2, cols), F32) for _, (rows, cols), _ in units],
        scratch_shapes=[pltpu.SemaphoreType.DMA((n_u,)), pltpu.SemaphoreType.DMA((n_u,))],
    )(*slabs)


def _exchange_sems(n_u):
    return [pltpu.SemaphoreType.DMA((3 * n_u,)), pltpu.SemaphoreType.DMA((3 * n_u,))]


def _exchange_copies(ins, outs, sems):
    send_sems, recv_sems = sems
    x, y, c, me, others = _place()
    sends = [pltpu.make_async_remote_copy(
        src_ref=ins[u].at[2 * ox + oy], dst_ref=outs[u].at[me], send_sem=send_sems.at[3 * u + j],
        recv_sem=recv_sems.at[3 * u + j], device_id=(ox, oy, c), device_id_type=MESH)
        for u in range(len(ins)) for j, (ox, oy) in enumerate(others)]
    arrivals = [pltpu.make_async_remote_copy(
        src_ref=ins[u].at[me], dst_ref=outs[u].at[2 * ox + oy], send_sem=send_sems.at[3 * u + j],
        recv_sem=recv_sems.at[3 * u + j], device_id=(x, y, c), device_id_type=MESH)
        for u in range(len(ins)) for j, (ox, oy) in enumerate(others)]
    return sends, arrivals


def _exchange_start(ins, outs, sems):
    for cp in _exchange_copies(ins, outs, sems)[0]:
        cp.start()


def _exchange_finish(ins, outs, sems):
    sends, arrivals = _exchange_copies(ins, outs, sems)
    for cp in arrivals:
        cp.wait_recv()
    for cp in sends:
        cp.wait_send()


def _chip_exchange(pairs, small):
    n_u = len(pairs)

    def body(*refs):
        ins, s_ref = refs[:n_u], refs[n_u]
        outs, sall_ref = refs[n_u + 1:2 * n_u + 1], refs[2 * n_u + 1]
        send_sems, recv_sems, ssend_sems, srecv_sems, local_sem = refs[2 * n_u + 2:]
        x, y, c, _, _ = _place()
        dev = 4 * x + 2 * y + c
        local = pltpu.make_async_copy(s_ref, sall_ref.at[dev], local_sem)
        flips = [(fx, fy, fc) for fx in (0, 1) for fy in (0, 1) for fc in (0, 1)][1:]
        small_sends = [pltpu.make_async_remote_copy(
            src_ref=s_ref, dst_ref=sall_ref.at[dev], send_sem=ssend_sems.at[n], recv_sem=srecv_sems.at[n],
            device_id=(x ^ fx, y ^ fy, c ^ fc), device_id_type=MESH) for n, (fx, fy, fc) in enumerate(flips)]
        local.start()
        _exchange_start(ins, outs, (send_sems, recv_sems))
        for cp in small_sends:
            cp.start()
        _exchange_finish(ins, outs, (send_sems, recv_sems))
        for n, (fx, fy, fc) in enumerate(flips):
            src_dev = 4 * (x ^ fx) + 2 * (y ^ fy) + (c ^ fc)
            pltpu.make_async_remote_copy(
                src_ref=s_ref, dst_ref=sall_ref.at[src_dev], send_sem=ssend_sems.at[n], recv_sem=srecv_sems.at[n],
                device_id=(x, y, c), device_id_type=MESH).wait_recv()
        for cp in small_sends:
            cp.wait_send()
        local.wait()

    outs = pl.pallas_call(
        body, name="grad_chip_exchange", in_specs=[ANY] * (n_u + 1), out_specs=[ANY] * (n_u + 1),
        out_shape=[jax.ShapeDtypeStruct(p.shape, p.dtype) for p in pairs]
        + [jax.ShapeDtypeStruct((8, SMALL_ALL_ROWS, LANES), F32)],
        scratch_shapes=_exchange_sems(n_u) + [pltpu.SemaphoreType.DMA((7,)), pltpu.SemaphoreType.DMA((7,)),
                                              pltpu.SemaphoreType.DMA],
    )(*pairs, small)
    return outs[:n_u], outs[n_u]


def _sibling_share(grads):
    n_u = len(GRAD_UNITS)

    def body(*refs):
        ins, outs = refs[:n_u], refs[n_u:2 * n_u]
        send_sems, recv_sems = refs[2 * n_u:]
        x, y, c, _, _ = _place()

        def rows_of(ref, u, h):
            half = GRAD_UNITS[u][1][0] // 2
            return ref.at[pl.ds(h * half, half), :]

        sends = [pltpu.make_async_remote_copy(
            src_ref=rows_of(ins[u], u, c), dst_ref=rows_of(outs[u], u, c), send_sem=send_sems.at[u],
            recv_sem=recv_sems.at[u], device_id=(x, y, 1 - c), device_id_type=MESH) for u in range(n_u)]
        for cp in sends:
            cp.start()
        for u in range(n_u):
            pltpu.make_async_remote_copy(
                src_ref=rows_of(ins[u], u, c), dst_ref=rows_of(outs[u], u, 1 - c), send_sem=send_sems.at[u],
                recv_sem=recv_sems.at[u], device_id=(x, y, c), device_id_type=MESH).wait_recv()
        for cp in sends:
            cp.wait_send()

    return pl.pallas_call(
        body, name="grad_sibling_share", in_specs=[ANY] * n_u, out_specs=[ANY] * n_u,
        out_shape=[jax.ShapeDtypeStruct(g.shape, F32) for g in grads],
        input_output_aliases={u: u for u in range(n_u)},
        scratch_shapes=[pltpu.SemaphoreType.DMA((n_u,)), pltpu.SemaphoreType.DMA((n_u,))],
    )(*grads)


def kernel(x, g_mix_pre, g_mix_post, g_ffn_pre, g_ffn_post, conv_pw1_w, conv_pw1_b, conv_dw_w, conv_dw_b, conv_ln_g, conv_ln_b, conv_pw2_w, conv_pw2_b, attn_w_in, attn_b_f, attn_w_o, mlp_w_up, mlp_w_down, loss_target, m_g_mix_pre, m_g_mix_post, m_g_ffn_pre, m_g_ffn_post, m_conv_pw1_w, m_conv_pw1_b, m_conv_dw_w, m_conv_dw_b, m_conv_ln_g, m_conv_ln_b, m_conv_pw2_w, m_conv_pw2_b, m_attn_w_in, m_attn_b_f, m_attn_w_o, m_mlp_w_up, m_mlp_w_down, v_g_mix_pre, v_g_mix_post, v_g_ffn_pre, v_g_ffn_post, v_conv_pw1_w, v_conv_pw1_b, v_conv_dw_w, v_conv_dw_b, v_conv_ln_g, v_conv_ln_b, v_conv_pw2_w, v_conv_pw2_b, v_attn_w_in, v_attn_b_f, v_attn_w_o, v_mlp_w_up, v_mlp_w_down):
    weights = dict(g_mix_pre=g_mix_pre, g_mix_post=g_mix_post, g_ffn_pre=g_ffn_pre, g_ffn_post=g_ffn_post, conv_pw1_w=conv_pw1_w, conv_pw1_b=conv_pw1_b, conv_dw_w=conv_dw_w, conv_dw_b=conv_dw_b, conv_ln_g=conv_ln_g, conv_ln_b=conv_ln_b, conv_pw2_w=conv_pw2_w, conv_pw2_b=conv_pw2_b, attn_w_in=attn_w_in, attn_b_f=attn_b_f, attn_w_o=attn_w_o, mlp_w_up=mlp_w_up, mlp_w_down=mlp_w_down)
    mom_m = dict(g_mix_pre=m_g_mix_pre, g_mix_post=m_g_mix_post, g_ffn_pre=m_g_ffn_pre, g_ffn_post=m_g_ffn_post, conv_pw1_w=m_conv_pw1_w, conv_pw1_b=m_conv_pw1_b, conv_dw_w=m_conv_dw_w, conv_dw_b=m_conv_dw_b, conv_ln_g=m_conv_ln_g, conv_ln_b=m_conv_ln_b, conv_pw2_w=m_conv_pw2_w, conv_pw2_b=m_conv_pw2_b, attn_w_in=m_attn_w_in, attn_b_f=m_attn_b_f, attn_w_o=m_attn_w_o, mlp_w_up=m_mlp_w_up, mlp_w_down=m_mlp_w_down)
    mom_v = dict(g_mix_pre=v_g_mix_pre, g_mix_post=v_g_mix_post, g_ffn_pre=v_g_ffn_pre, g_ffn_post=v_g_ffn_post, conv_pw1_w=v_conv_pw1_w, conv_pw1_b=v_conv_pw1_b, conv_dw_w=v_conv_dw_w, conv_dw_b=v_conv_dw_b, conv_ln_g=v_conv_ln_g, conv_ln_b=v_conv_ln_b, conv_pw2_w=v_conv_pw2_w, conv_pw2_b=v_conv_pw2_b, attn_w_in=v_attn_w_in, attn_b_f=v_attn_b_f, attn_w_o=v_attn_w_o, mlp_w_up=v_mlp_w_up, mlp_w_down=v_mlp_w_down)
    order = [n for n, _ in SMALL[:4]] + ["conv_pw1_w", "conv_pw1_b", "conv_dw_w", "conv_dw_b", "conv_ln_g", "conv_ln_b",
                                          "conv_pw2_w", "conv_pw2_b", "attn_w_in", "attn_b_f", "attn_w_o", "mlp_w_up",
                                          "mlp_w_down"]
    T = x.shape[1]
    x0 = x.reshape(T, D)
    target = loss_target.reshape(T, D)
    row = lambda a, l: a[l:l + 1, :]

    def shard(unit):
        if unit[:-1] in ("up", "down"):
            return weights["mlp_w_" + unit[:-1]][int(unit[-1])].astype(BF16)
        w = weights[unit][0].astype(BF16)
        return jnp.pad(w, ((0, HALO - CONV_W), (0, 0))) if unit == "conv_dw_w" else w

    by_cols = lambda g: g.transpose(1, 0, 2).reshape(g.shape[1], N_CHIPS * g.shape[2])
    by_rows = lambda g: g.reshape(N_CHIPS * g.shape[1], g.shape[2])
    g_pw1, g_dw, g_pw2 = _allgather_weights(GATHER_AT_START, [shard(n) for n, _ in GATHER_AT_START], "allgather_conv")
    w_pw1, w_pw2, w_dw = by_cols(g_pw1), by_rows(g_pw2), by_cols(g_dw).astype(F32)
    b_f = jnp.pad(attn_b_f, ((0, 0), (0, LANES - N_HEADS)))

    def mlp_fwd(h, l):
        (act,) = _matmul(h, w_up[l], "nn", (BF16,), _epi_sqrelu, f"mlp{l}_up")
        (mo,) = _matmul(act, w_down[l], "nn", (F32,), _epi_plain, f"mlp{l}_down", tk=DFF)
        return act, mo

    def mlp0_fwd(h):
        ride = [dict(gather_units=units, gather_shards=[shard(n) for n, _ in units]) for units in GATHER_BEHIND_MLP]
        (act,), (g_up1,) = _matmul(h, w_up[0], "nn", (BF16,), _epi_sqrelu, "mlp0_up", **ride[0])
        (mo,), (g_down1,) = _matmul(act, w_down[0], "nn", (F32,), _epi_plain, "mlp0_down", tk=DFF, **ride[1])
        w_up.append(by_cols(g_up1))
        w_down.append(by_rows(g_down1))
        return act, mo

    (h0,) = _resnorm_fwd(x0, None, None, row(g_mix_pre, 0), "norm_in")
    (a0,) = _matmul(h0, w_pw1, "nn", (F32,), _epi_bias, "conv_pw1", extras=((conv_pw1_b, "row"),))
    y0, z0, (g_up0, g_down0, g_win, g_wo) = _conv_fwd(
        a0, w_dw, conv_dw_b, conv_ln_g, conv_ln_b, "conv_fwd", GATHER_BEHIND_CONV,
        [shard(n) for n, _ in GATHER_BEHIND_CONV])
    w_up, w_down = [by_cols(g_up0)], [by_rows(g_down0)]
    w_o = by_rows(g_wo)
    w_in = jnp.pad(by_cols(g_win), ((0, 0), (0, W_IN_PAD - W_IN_COLS)))
    (m0,) = _matmul(z0, w_pw2, "nn", (F32,), _epi_bias, "conv_pw2", extras=((conv_pw2_b, "row"),))
    x1, h1 = _resnorm_fwd(x0, m0, row(g_mix_post, 0), row(g_ffn_pre, 0), "norm_conv_out")
    act0, mo0 = mlp0_fwd(h1)
    x2, h2 = _resnorm_fwd(x1, mo0, row(g_ffn_post, 0), row(g_mix_pre, 1), "norm_mlp0_out")

    (proj,) = _matmul(h2, w_in[:, :3 * D], "nn", (BF16,), _epi_plain, "attn_in")
    (fl,) = _matmul(h2, w_in[:, 3 * D:], "nn", (F32,), _epi_plain, "attn_gate_in")
    f_all, f1, f2, f3 = _gate_fwd(fl, b_f, "gate_fwd")
    (q_aug, k_aug, v_aug, qn2, kn2), _ = _attn_pack(proj, f1, f2, f3, "attn_pack", (), ())
    tb = min(ATT_TB, T)
    qk = 2.02 * jnp.sqrt(qn2[:, 0, 0] * kn2[:, 0, 0]) + 1.0
    fs, fe = f_all[0::tb, :N_HEADS].T, f_all[tb - 1::tb, :N_HEADS].T
    o_aug, q2_aug = _flash_fwd(qk, fs, fe, jnp.sqrt(kn2[:, 0, 0]), q_aug, k_aug, v_aug, "flash_fwd")
    (o_bf,) = _merge_heads(o_aug, 1.0, BF16, "attn_merge_o")
    (m1,) = _matmul(o_bf, w_o, "nn", (F32,), _epi_plain, "attn_out")
    x3, h3 = _resnorm_fwd(x2, m1, row(g_mix_post, 1), row(g_ffn_pre, 1), "norm_attn_out")
    act1, mo1 = mlp_fwd(h3, 1)
    dy, loss_part = _resnorm_loss(x3, mo1, row(g_ffn_post, 1), target, "loss")
    loss = lax.psum(loss_part[0, 0], ("x", "y", "c"))

    G = {}
    chip = 2 * lax.axis_index("x") + lax.axis_index("y")
    place = jnp.stack([chip] + [k + (k >= chip).astype(jnp.int32) for k in range(N_CHIPS - 1)]
                      + [lax.axis_index("c")]).astype(jnp.int32)

    def mlp_bwd(d_out, mo, act, h, x_in, l):
        d_mo, dg_post, _ = _norm_bwd(mo, row(g_ffn_post, l), d_out, None, BF16, f"mlp{l}_post_bwd")
        (d_up,) = _matmul(d_mo, w_down[l], "nt", (BF16,), _epi_dsqrelu, f"mlp{l}_dact", extras=((act, "tile"),))
        (dw_down,) = _matmul(act, d_mo, "tn", (F32,), _epi_plain, tk=4096, name=f"mlp{l}_dwdown")
        (dw_up,) = _matmul(h, d_up, "tn", (F32,), _epi_plain, tk=4096, name=f"mlp{l}_dwup", shard_out=True)
        (d_h,) = _matmul(d_up, w_up[l], "nt", (F32,), _epi_plain, f"mlp{l}_dh", tk=DFF)
        d_in, dg_pre, _ = _norm_bwd(x_in, row(g_ffn_pre, l), d_h, d_out, F32, f"mlp{l}_pre_bwd")
        G[f"up{l}"], G[f"down{l}"] = dw_up, dw_down.reshape(N_CHIPS, D, D)
        return d_in, dg_post, dg_pre

    d_x3, dg_ffn_post1, dg_ffn_pre1 = mlp_bwd(dy, mo1, act1, h3, x3, 1)

    d_m1, dg_mix_post1, _ = _norm_bwd(m1, row(g_mix_post, 1), d_x3, None, BF16, "attn_post_bwd")
    (dw_o,) = _matmul(o_bf, d_m1, "tn", (F32,), _epi_plain, tk=4096, name="attn_dwo")
    G["attn_w_o"] = dw_o.reshape(N_CHIPS, D // N_CHIPS, D)
    (d_o,) = _matmul(d_m1, w_o, "nt", (F32,), _epi_plain, "attn_do")
    do_aug = _attn_prep(d_o, o_aug, "attn_prep")
    dq_aug = _flash_dq(qk, fs, fe, q2_aug, k_aug, do_aug, v_aug, "flash_dq")
    dk_aug, dv_aug = _flash_dkv(qk, fs, fe, q2_aug, k_aug, do_aug, v_aug, "flash_dkv")
    d_q, dF_q = _merge_heads(dq_aug, 0.125, BF16, "attn_merge_dq", column=AUG_F)
    d_k, dF_k = _merge_heads(dk_aug, 1.0, BF16, "attn_merge_dk", column=AUG_ONE)
    (d_v,) = _merge_heads(dv_aug, 1.0, BF16, "attn_merge_dv")
    d_fl, db_f = _gate_bwd(dF_q, dF_k, fl, b_f, "gate_bwd")
    d_proj = jnp.concatenate([d_q, d_k, d_v, d_fl.astype(BF16)], axis=1)
    (dw_in,) = _matmul(h2, d_proj, "tn", (F32,), _epi_plain, tk=4096, name="attn_dwin", tn=640)
    G["attn_w_in"] = dw_in[:, :W_IN_COLS].reshape(D, N_CHIPS, W_IN_COLS // N_CHIPS).transpose(1, 0, 2)
    (d_h2,) = _matmul(d_proj, w_in, "nt", (F32,), _epi_plain, "attn_dh", tk=W_IN_PAD)
    d_x2, dg_mix_pre1, _ = _norm_bwd(x2, row(g_mix_pre, 1), d_h2, d_x3, F32, "attn_pre_bwd")

    d_x1, dg_ffn_post0, dg_ffn_pre0 = mlp_bwd(d_x2, mo0, act0, h1, x1, 0)

    d_m0, dg_mix_post0, db_pw2 = _norm_bwd(m0, row(g_mix_post, 0), d_x1, None, BF16, "conv_post_bwd")
    (dw_pw2,) = _matmul(z0, d_m0, "tn", (F32,), _epi_plain, tk=4096, name="conv_dwpw2")
    G["conv_pw2_w"] = dw_pw2.reshape(N_CHIPS, D // N_CHIPS, D)
    (d_z0,) = _matmul(d_m0, w_pw2, "nt", (F32,), _epi_plain, "conv_dz")
    conv_units = [u for u in GRAD_UNITS if u[0].startswith("conv")]
    attn_units = [u for u in GRAD_UNITS if u not in conv_units]
    attn_slabs = [G[n] for n, _, _ in attn_units]
    d_y0, dln_g, dln_b, ddw_b, attn_got = _conv_bwd_ln(d_z0, y0, conv_ln_g, conv_ln_b, "conv_bwd_ln", attn_units,
                                                       attn_slabs)
    pairs = {n: _add_pair(place, s, g, f"grad_pair_{n}") for s, g, (n, _, _) in zip(attn_slabs, attn_got, attn_units)}
    d_a0, db_pw1, ddw_w, attn_by_chip = _conv_bwd_dw(d_y0, a0, w_dw, "conv_bwd_dw",
                                                    [pairs[n] for n, _, _ in attn_units])
    by_chip = {n: b for b, (n, _, _) in zip(attn_by_chip, attn_units)}
    (G["conv_pw1_w"],) = _matmul(h0, d_a0, "tn", (F32,), _epi_plain, tk=4096, name="conv_dwpw1", tn=512, shard_out=True)
    (d_h0,) = _matmul(d_a0, w_pw1, "nt", (F32,), _epi_plain, "conv_dh", tk=2 * D)
    d_x0, dg_mix_pre0, _ = _norm_bwd(x0, row(g_mix_pre, 0), d_h0, d_x1, F32, "conv_pre_bwd")

    small_local = {
        "g_mix_pre": jnp.concatenate([dg_mix_pre0, dg_mix_pre1]), "g_mix_post": jnp.concatenate([dg_mix_post0, dg_mix_post1]),
        "g_ffn_pre": jnp.concatenate([dg_ffn_pre0, dg_ffn_pre1]), "g_ffn_post": jnp.concatenate([dg_ffn_post0, dg_ffn_post1]),
        "conv_pw1_b": db_pw1, "conv_dw_b": ddw_b, "conv_ln_g": dln_g, "conv_ln_b": dln_b, "conv_pw2_b": db_pw2,
        "attn_b_f": db_f[:, :N_HEADS], "conv_dw_w_full": ddw_w,
    }

    conv_slabs = [G[n] for n, _, _ in conv_units]
    conv_got = _sibling_exchange(conv_units, conv_slabs, "grad_sibling_exchange_conv")
    pairs.update({n: _add_pair(place, s, g, f"grad_pair_{n}")
                  for s, g, (n, _, _) in zip(conv_slabs, conv_got, conv_units)})
    conv_by_chip, small_all = _chip_exchange([pairs[n] for n, _, _ in conv_units],
                                             _pack([small_local[n] for n, _ in SMALL_ALL], SMALL_ALL_ROWS))
    by_chip.update({n: b for b, (n, _, _) in zip(conv_by_chip, conv_units)})
    unit_grads = _sibling_share([_add_chips(place, pairs[n], by_chip[n], f"grad_chips_{n}") for n, _, _ in GRAD_UNITS])
    grads = _unpack(_add_slots(small_all, "grad_add_small", tb=SMALL_ALL_ROWS), SMALL_ALL)
    grads["conv_dw_w"] = lax.dynamic_slice(grads.pop("conv_dw_w_full"), (0, chip * 256), (CONV_W, 256))[None]

    delta, new_m, new_v = {}, {}, {}
    per_layer = {}
    for g, (unit, shape, (name, layer)) in zip(unit_grads, GRAD_UNITS):
        view = lambda d: (d[name] if layer is None else d[name][layer]).reshape(shape)
        per_layer.setdefault(name, []).append((g,) + tuple(_adamw(view(weights), g, view(mom_m), view(mom_v),
                                                                  f"adamw_{unit}")))
    for name, parts in per_layer.items():
        full = weights[name].shape
        join = lambda n: (parts[0][n] if len(parts) == 1 else jnp.stack([p[n] for p in parts])).reshape(full)
        grads[name], delta[name], new_m[name], new_v[name] = join(0), join(1), join(2), join(3)
    dw2 = lambda d: d["conv_dw_w"].reshape(CONV_W, 256)
    d, nm, nv = _adamw(dw2(weights), dw2(grads), dw2(mom_m), dw2(mom_v), "adamw_conv_dw_w")
    full = weights["conv_dw_w"].shape
    delta["conv_dw_w"], new_m["conv_dw_w"], new_v["conv_dw_w"] = d.reshape(full), nm.reshape(full), nv.reshape(full)
    packs = [_pack([src[n] for n, _ in SMALL], SMALL_ROWS) for src in (weights, grads, mom_m, mom_v)]
    d, nm, nv = _adamw(*packs, "adamw_small")
    for dst, packed in ((delta, d), (new_m, nm), (new_v, nv)):
        dst.update(_unpack(packed, SMALL))

    grad_x = d_x0.reshape(x.shape)
    return (loss, grad_x, *[grads[n] for n in order], *[delta[n] for n in order], *[new_m[n] for n in order],
            *[new_v[n] for n in order])
```

```python
import functools

import jax
import jax.numpy as jnp
from jax import lax
from jax.experimental import pallas as pl
from jax.experimental.pallas import tpu as pltpu

F32 = jnp.float32
BF16 = jnp.bfloat16
MESH = pl.DeviceIdType.MESH

D = 1024
DFF = 4096
N_HEADS = 16
HEAD_DIM = 64
CONV_W = 31
HALO = 32
RMS_EPS = 1e-6
LN_EPS = 1e-5
MASK_VALUE = -1e30
W_IN_COLS = 3 * D + N_HEADS
W_IN_PAD = 3200
LANES = 128
SUBLANES = 8
N_CHIPS = 4
VMEM_LIMIT = 56 * 1024 * 1024
MATMUL_CHUNK = 256

ADAM_LR = 0.001
ADAM_B1 = 0.9
ADAM_B2 = 0.999
ADAM_EPS = 1e-08
ADAM_WD = 0.01
ADAM_STEP = 10

AUG_F = 64
AUG_ONE = 67
AUG_L = 70
SKIP_BELOW = -104.0
FIXED_MAX_BELOW = 40.0

GATHER_AT_START = (("conv_pw1_w", (1024, 512)), ("conv_dw_w", (HALO, 256)), ("conv_pw2_w", (256, 1024)))
GATHER_BEHIND_CONV = (("up0", (1024, 1024)), ("down0", (1024, 1024)), ("attn_w_in", (1024, 772)),
                      ("attn_w_o", (256, 1024)))
GATHER_BEHIND_MLP = ((("up1", (1024, 1024)),), (("down1", (1024, 1024)),))
GRAD_UNITS = (
    ("conv_pw1_w", (1024, 512), ("conv_pw1_w", None)),
    ("conv_pw2_w", (256, 1024), ("conv_pw2_w", None)),
    ("attn_w_in", (1024, 772), ("attn_w_in", None)),
    ("attn_w_o", (256, 1024), ("attn_w_o", None)),
    ("up0", (1024, 1024), ("mlp_w_up", 0)),
    ("up1", (1024, 1024), ("mlp_w_up", 1)),
    ("down0", (1024, 1024), ("mlp_w_down", 0)),
    ("down1", (1024, 1024), ("mlp_w_down", 1)),
)
SMALL = (
    ("g_mix_pre", (2, 1024)), ("g_mix_post", (2, 1024)), ("g_ffn_pre", (2, 1024)), ("g_ffn_post", (2, 1024)),
    ("conv_pw1_b", (1, 2048)), ("conv_dw_b", (1, 1024)), ("conv_ln_g", (1, 1024)), ("conv_ln_b", (1, 1024)),
    ("conv_pw2_b", (1, 1024)), ("attn_b_f", (1, 16)),
)
SMALL_ROWS = 120
SMALL_ALL = SMALL + (("conv_dw_w_full", (HALO, 1024)),)
SMALL_ALL_ROWS = 376


def _size(shape):
    n = 1
    for s in shape:
        n *= s
    return n


def _pack(arrays, rows):
    flat = jnp.concatenate([a.reshape(-1) for a in arrays])
    return jnp.pad(flat, (0, rows * LANES - flat.shape[0])).reshape(rows, LANES)


def _unpack(packed, table):
    flat = packed.reshape(-1)
    out, off = {}, 0
    for name, shape in table:
        n = _size(shape)
        out[name] = flat[off:off + n].reshape(shape)
        off += n
    return out


def _params(sem):
    return pltpu.CompilerParams(dimension_semantics=sem, vmem_limit_bytes=VMEM_LIMIT)


def _rms(x, g):
    return x * lax.rsqrt(jnp.mean(x * x, axis=-1, keepdims=True) + RMS_EPS) * g


def _split3(v):
    p1 = v.astype(BF16).astype(F32)
    r = v - p1
    p2 = r.astype(BF16).astype(F32)
    p3 = (r - p2).astype(BF16).astype(F32)
    return p1, p2, p3


def _matmul(a, b, mode, out_dtypes, epi, name, extras=(), tm=1024, tn=1024, tk=1024, shard_out=False,
            gather_units=(), gather_shards=()):
    (K, M) = a.shape if mode == "tn" else a.shape[::-1]
    N = b.shape[0] if mode == "nt" else b.shape[1]
    tm, tn, tk = min(tm, M), min(tn, N), min(tk, K)
    nk = K // tk
    if mode == "tn":
        a_spec = pl.BlockSpec((tk, tm), lambda i, j, k: (k, i))
    else:
        a_spec = pl.BlockSpec((tm, tk), lambda i, j, k: (i, k))
    if mode == "nt":
        b_spec = pl.BlockSpec((tn, tk), lambda i, j, k: (j, k))
    else:
        b_spec = pl.BlockSpec((tk, tn), lambda i, j, k: (k, j))
    if shard_out:
        per = N // N_CHIPS // tn
        o_spec = pl.BlockSpec((None, tm, tn), lambda i, j, k: (j // per, i, j % per))
        o_shape = (N_CHIPS, M, N // N_CHIPS)
    else:
        o_spec = pl.BlockSpec((tm, tn), lambda i, j, k: (i, j))
        o_shape = (M, N)
    dims = {"nn": (((1,), (0,)), ((), ())), "nt": (((1,), (1,)), ((), ())), "tn": (((0,), (0,)), ((), ()))}[mode]
    ex_specs = []
    for _, kind in extras:
        if kind == "row":
            ex_specs.append(pl.BlockSpec((1, tn), lambda i, j, k: (0, j)))
        else:
            ex_specs.append(pl.BlockSpec((tm, tn), lambda i, j, k: (i, j)))
    n_ex, n_out = len(extras), len(out_dtypes)

    cw = MATMUL_CHUNK if tn % MATMUL_CHUNK == 0 else tn

    n_g = len(gather_units)
    grid = (M // tm, N // tn, nk)

    def body(*refs):
        a_ref, b_ref = refs[0], refs[1]
        ex = refs[2:2 + n_ex]
        g_ins = refs[2 + n_ex:2 + n_ex + n_g]
        outs = refs[2 + n_ex + n_g:2 + n_ex + n_g + n_out]
        g_outs = refs[2 + n_ex + n_g + n_out:2 + n_ex + 2 * n_g + n_out]
        scratch = refs[2 + n_ex + 2 * n_g + n_out:]
        if n_g:
            step = [pl.program_id(d) for d in range(3)]
            at = lambda where: functools.reduce(jnp.logical_and, [s == w for s, w in zip(step, where)])
            pl.when(at((0, 0, 0)))(lambda: _gather_start(gather_units, g_ins, g_outs, scratch[-4:]))

        def for_chunks(use):
            for c in range(tn // cw):
                cols = slice(c * cw, (c + 1) * cw)
                b_chunk = b_ref[cols, :] if mode == "nt" else b_ref[:, cols]
                use(cols, lax.dot_general(a_ref[...], b_chunk, dims, preferred_element_type=F32))

        def finish(cols, acc):
            res = epi(acc, *[e[:, cols] for e in ex])
            for o, r in zip(outs, res):
                o[:, cols] = r.astype(o.dtype)

        if nk == 1:
            for_chunks(finish)
        else:
            acc_ref = scratch[0]
            k = pl.program_id(2)

            def first(cols, prod):
                acc_ref[:, cols] = prod

            def middle(cols, prod):
                acc_ref[:, cols] += prod

            pl.when(k == 0)(lambda: for_chunks(first))
            pl.when(jnp.logical_and(k > 0, k < nk - 1))(lambda: for_chunks(middle))
            pl.when(k == nk - 1)(lambda: for_chunks(lambda cols, prod: finish(cols, acc_ref[:, cols] + prod)))
        if n_g:
            pl.when(at([g - 1 for g in grid]))(lambda: _gather_finish(gather_units, g_ins, g_outs, scratch[-4:]))

    outs = pl.pallas_call(
        body,
        name=name,
        grid=grid,
        in_specs=[a_spec, b_spec] + ex_specs + [ANY] * n_g,
        out_specs=[o_spec for _ in out_dtypes] + [ANY] * n_g,
        out_shape=[jax.ShapeDtypeStruct(o_shape, dt) for dt in out_dtypes]
        + [jax.ShapeDtypeStruct((N_CHIPS,) + shape, BF16) for _, shape in gather_units],
        scratch_shapes=([pltpu.VMEM((tm, tn), F32)] if nk > 1 else []) + (_gather_sems(n_g) if n_g else []),
        compiler_params=_params(("arbitrary",) * 3 if n_g else ("parallel", "parallel", "arbitrary")),
    )(a, b, *[e for e, _ in extras], *gather_shards)
    return (outs[:n_out], outs[n_out:]) if n_g else outs


def _epi_plain(acc):
    return (acc,)


def _epi_bias(acc, bias):
    return (acc + bias,)


def _epi_sqrelu(acc):
    r = jnp.maximum(acc, 0.0)
    return (r * r,)


def _epi_dsqrelu(acc, act):
    a = act.astype(F32)
    return (acc * jnp.where(a > 0.0, 2.0 * a * lax.rsqrt(a), 0.0),)


def _resnorm_fwd(x, m, g_post, g_next, name, tr=512):
    T = x.shape[0]
    has_m = m is not None
    row = pl.BlockSpec((tr, D), lambda i: (i, 0))
    vec = pl.BlockSpec((1, D), lambda i: (0, 0))

    def body(*refs):
        if has_m:
            x_ref, m_ref, gp_ref, gn_ref, x1_ref, h_ref = refs
            x1 = x_ref[...] + _rms(m_ref[...], gp_ref[...])
            x1_ref[...] = x1
        else:
            x_ref, gn_ref, h_ref = refs
            x1 = x_ref[...]
        h_ref[...] = _rms(x1, gn_ref[...]).astype(BF16)

    if has_m:
        args, in_specs = (x, m, g_post, g_next), [row, row, vec, vec]
        out_specs = [row, row]
        out_shape = [jax.ShapeDtypeStruct((T, D), F32), jax.ShapeDtypeStruct((T, D), BF16)]
    else:
        args, in_specs = (x, g_next), [row, vec]
        out_specs = [row]
        out_shape = [jax.ShapeDtypeStruct((T, D), BF16)]
    return pl.pallas_call(
        body, name=name, grid=(T // tr,), in_specs=in_specs, out_specs=out_specs, out_shape=out_shape,
        compiler_params=_params(("parallel",)),
    )(*args)


def _resnorm_loss(x, m, g_post, target, name, tr=512):
    T = x.shape[0]
    row = pl.BlockSpec((tr, D), lambda i: (i, 0))
    vec = pl.BlockSpec((1, D), lambda i: (0, 0))

    def body(x_ref, m_ref, gp_ref, t_ref, dy_ref, loss_ref):
        diff = x_ref[...] + _rms(m_ref[...], gp_ref[...]) - t_ref[...]
        dy_ref[...] = diff * (1.0 / D)

        @pl.when(pl.program_id(0) == 0)
        def _():
            loss_ref[...] = jnp.zeros_like(loss_ref)

        per_token = jnp.mean(diff * diff, axis=-1, keepdims=True)
        loss_ref[...] += 0.5 * jnp.sum(per_token)

    return pl.pallas_call(
        body, name=name, grid=(T // tr,), in_specs=[row, row, vec, row],
        out_specs=[row, pl.BlockSpec((8, LANES), lambda i: (0, 0))],
        out_shape=[jax.ShapeDtypeStruct((T, D), F32), jax.ShapeDtypeStruct((8, LANES), F32)],
        compiler_params=_params(("arbitrary",)),
    )(x, m, g_post, target)


def _norm_bwd(x, g, dy, resid, out_dtype, name, tr=512):
    T = x.shape[0]
    has_r = resid is not None
    row = pl.BlockSpec((tr, D), lambda i: (i, 0))
    vec = pl.BlockSpec((1, D), lambda i: (0, 0))

    def body(*refs):
        if has_r:
            x_ref, g_ref, dy_ref, r_ref, dx_ref, dg_ref, cs_ref = refs
        else:
            x_ref, g_ref, dy_ref, dx_ref, dg_ref, cs_ref = refs
        xv, dyv = x_ref[...], dy_ref[...]
        r = lax.rsqrt(jnp.mean(xv * xv, axis=-1, keepdims=True) + RMS_EPS)
        gy = dyv * g_ref[...]
        c = jnp.sum(gy * xv, axis=-1, keepdims=True) * (1.0 / D)
        dx = r * gy - xv * (r * r * r * c)

        @pl.when(pl.program_id(0) == 0)
        def _():
            dg_ref[...] = jnp.zeros_like(dg_ref)
            cs_ref[...] = jnp.zeros_like(cs_ref)

        dg_ref[...] += jnp.sum(dyv * (xv * r), axis=0, keepdims=True)
        cs_ref[...] += jnp.sum(dx, axis=0, keepdims=True)
        if has_r:
            dx = dx + r_ref[...]
        dx_ref[...] = dx.astype(dx_ref.dtype)

    args = (x, g, dy) + ((resid,) if has_r else ())
    return pl.pallas_call(
        body, name=name, grid=(T // tr,), in_specs=[row, vec, row] + ([row] if has_r else []),
        out_specs=[row, vec, vec],
        out_shape=[jax.ShapeDtypeStruct((T, D), out_dtype), jax.ShapeDtypeStruct((1, D), F32),
                   jax.ShapeDtypeStruct((1, D), F32)],
        compiler_params=_params(("arbitrary",)),
    )(*args)


CONV_TB = 256
CONV_RC = 32


def _glu(a):
    return a[:, :D] * jax.nn.sigmoid(a[:, D:])


def _shifted_copies(win):
    rows = win.shape[1]
    for p in range(1, SUBLANES):
        win[p, 0:rows - SUBLANES, :] = win[0, p:p + rows - SUBLANES, :]


def _window(win, start):
    p = start % SUBLANES
    return win[p, start - p:start - p + CONV_RC, :]


def _conv_fwd(a, dw_w, dw_b, ln_g, ln_b, name, gather_units, gather_shards):
    T = a.shape[0]
    tb = min(CONV_TB, T)
    per = tb // HALO
    n_steps = T // tb
    n_g = len(gather_units)
    vec = pl.BlockSpec((1, D), lambda i: (0, 0))

    def body(*refs):
        cur_ref, prev_ref, w_ref, b_ref, lg_ref, lb_ref = refs[:6]
        g_ins = refs[6:6 + n_g]
        y_ref, z_ref = refs[6 + n_g:8 + n_g]
        g_outs = refs[8 + n_g:8 + 2 * n_g]
        uwin = refs[8 + 2 * n_g]
        sems = refs[9 + 2 * n_g:]
        i = pl.program_id(0)
        pl.when(i == 0)(lambda: _gather_start(gather_units, g_ins, g_outs, sems))
        uwin[0, 0:HALO, :] = jnp.where(i > 0, _glu(prev_ref[...]), 0.0)
        uwin[0, HALO:, :] = _glu(cur_ref[...])
        _shifted_copies(uwin)
        for r in range(tb // CONV_RC):
            acc = jnp.broadcast_to(b_ref[...], (CONV_RC, D))
            for k in range(CONV_W):
                start = r * CONV_RC + HALO - (CONV_W - 1) + k
                acc = acc + _window(uwin, start) * w_ref[k:k + 1, :]
            rows = slice(r * CONV_RC, (r + 1) * CONV_RC)
            y_ref[rows, :] = acc
            mu = jnp.mean(acc, axis=-1, keepdims=True)
            xc = acc - mu
            var = jnp.mean(xc * xc, axis=-1, keepdims=True)
            n = xc * lax.rsqrt(var + LN_EPS) * lg_ref[...] + lb_ref[...]
            z_ref[rows, :] = (n * jax.nn.sigmoid(n)).astype(BF16)
        pl.when(i == n_steps - 1)(lambda: _gather_finish(gather_units, g_ins, g_outs, sems))

    outs = pl.pallas_call(
        body, name=name, grid=(n_steps,),
        in_specs=[pl.BlockSpec((tb, 2 * D), lambda i: (i, 0)),
                  pl.BlockSpec((HALO, 2 * D), lambda i: (jnp.maximum(i * per - 1, 0), 0)),
                  pl.BlockSpec((HALO, D), lambda i: (0, 0)), vec, vec, vec] + [ANY] * n_g,
        out_specs=[pl.BlockSpec((tb, D), lambda i: (i, 0)), pl.BlockSpec((tb, D), lambda i: (i, 0))] + [ANY] * n_g,
        out_shape=[jax.ShapeDtypeStruct((T, D), F32), jax.ShapeDtypeStruct((T, D), BF16)]
        + [jax.ShapeDtypeStruct((N_CHIPS,) + shape, BF16) for _, shape in gather_units],
        scratch_shapes=[pltpu.VMEM((SUBLANES, tb + HALO, D), F32)] + _gather_sems(n_g),
        compiler_params=_params(("arbitrary",)),
    )(a, a, dw_w, dw_b, ln_g, ln_b, *gather_shards)
    return outs[0], outs[1], outs[2:]


def _conv_bwd_ln(dz, y, ln_g, ln_b, name, exchange_units, slabs, tr=256):
    T = y.shape[0]
    tr = min(tr, T)
    n_steps = T // tr
    n_s = len(slabs)
    row = pl.BlockSpec((tr, D), lambda i: (i, 0))
    vec = pl.BlockSpec((1, D), lambda i: (0, 0))

    def body(*refs):
        dz_ref, y_ref, lg_ref, lb_ref = refs[:4]
        s_ins = refs[4:4 + n_s]
        dy_ref, dlg_ref, dlb_ref, db_ref = refs[4 + n_s:8 + n_s]
        s_outs = refs[8 + n_s:8 + 2 * n_s]
        sems = refs[8 + 2 * n_s:]
        pl.when(pl.program_id(0) == 0)(lambda: _sibling_start(exchange_units, s_ins, s_outs, sems))
        yv = y_ref[...]
        mu = jnp.mean(yv, axis=-1, keepdims=True)
        xc = yv - mu
        rstd = lax.rsqrt(jnp.mean(xc * xc, axis=-1, keepdims=True) + LN_EPS)
        yh = xc * rstd
        n = yh * lg_ref[...] + lb_ref[...]
        sg = jax.nn.sigmoid(n)
        dn = dz_ref[...] * (sg * (1.0 + n * (1.0 - sg)))
        dyh = dn * lg_ref[...]
        dyv = rstd * (dyh - jnp.mean(dyh, axis=-1, keepdims=True) - yh * jnp.mean(dyh * yh, axis=-1, keepdims=True))
        dy_ref[...] = dyv

        @pl.when(pl.program_id(0) == 0)
        def _():
            dlg_ref[...] = jnp.zeros_like(dlg_ref)
            dlb_ref[...] = jnp.zeros_like(dlb_ref)
            db_ref[...] = jnp.zeros_like(db_ref)

        dlg_ref[...] += jnp.sum(dn * yh, axis=0, keepdims=True)
        dlb_ref[...] += jnp.sum(dn, axis=0, keepdims=True)
        db_ref[...] += jnp.sum(dyv, axis=0, keepdims=True)
        pl.when(pl.program_id(0) == n_steps - 1)(lambda: _sibling_finish(exchange_units, s_ins, s_outs, sems))

    outs = pl.pallas_call(
        body, name=name, grid=(n_steps,), in_specs=[row, row, vec, vec] + [ANY] * n_s,
        out_specs=[row, vec, vec, vec] + [ANY] * n_s,
        out_shape=[jax.ShapeDtypeStruct((T, D), F32)] + [jax.ShapeDtypeStruct((1, D), F32)] * 3
        + [jax.ShapeDtypeStruct((N_CHIPS, rows // 2, cols), F32) for _, (rows, cols), _ in exchange_units],
        scratch_shapes=[pltpu.SemaphoreType.DMA((n_s,)), pltpu.SemaphoreType.DMA((n_s,))],
        compiler_params=_params(("arbitrary",)),
    )(dz, y, ln_g, ln_b, *slabs)
    return outs[0], outs[1], outs[2], outs[3], outs[4:]


def _conv_bwd_dw(dy, a, dw_w, name, exchange):
    T = a.shape[0]
    tb = min(CONV_TB, T)
    per = tb // HALO
    last_halo = T // HALO - 1
    n_steps = T // tb
    n_x = len(exchange)

    def body(*refs):
        dyc_ref, dyn_ref, cur_ref, prev_ref, w_ref = refs[:5]
        x_ins = refs[5:5 + n_x]
        da_ref, dbias_ref, dw_ref = refs[5 + n_x:8 + n_x]
        x_outs = refs[8 + n_x:8 + 2 * n_x]
        uwin, dywin, dwacc = refs[8 + 2 * n_x:11 + 2 * n_x]
        sems = refs[11 + 2 * n_x:]
        i = pl.program_id(0)

        @pl.when(i == 0)
        def _():
            _exchange_start(x_ins, x_outs, sems)
            dbias_ref[...] = jnp.zeros_like(dbias_ref)
            dwacc[...] = jnp.zeros_like(dwacc)

        uwin[0, 0:HALO, :] = jnp.where(i > 0, _glu(prev_ref[...]), 0.0)
        uwin[0, HALO:, :] = _glu(cur_ref[...])
        dywin[0, 0:tb, :] = dyc_ref[...]
        dywin[0, tb:, :] = jnp.where(i < n_steps - 1, dyn_ref[...], 0.0)
        _shifted_copies(uwin)
        _shifted_copies(dywin)
        for r in range(tb // CONV_RC):
            rows = slice(r * CONV_RC, (r + 1) * CONV_RC)
            dy_c = dywin[0, rows, :]
            du = jnp.zeros((CONV_RC, D), F32)
            for k in range(CONV_W):
                du = du + _window(dywin, r * CONV_RC + (CONV_W - 1) - k) * w_ref[k:k + 1, :]
                prod = dy_c * _window(uwin, r * CONV_RC + HALO - (CONV_W - 1) + k)
                part = prod[0:8, :]
                for q in range(1, CONV_RC // 8):
                    part = part + prod[8 * q:8 * q + 8, :]
                dwacc[8 * k:8 * k + 8, :] += part
            av = cur_ref[rows, :]
            a1, sg = av[:, :D], jax.nn.sigmoid(av[:, D:])
            da1 = du * sg
            da2 = du * a1 * (sg * (1.0 - sg))
            da_ref[rows, 0:D] = da1.astype(BF16)
            da_ref[rows, D:] = da2.astype(BF16)
            dbias_ref[:, 0:D] += jnp.sum(da1, axis=0, keepdims=True)
            dbias_ref[:, D:] += jnp.sum(da2, axis=0, keepdims=True)

        @pl.when(i == n_steps - 1)
        def _():
            for k in range(CONV_W):
                dw_ref[k:k + 1, :] = jnp.sum(dwacc[8 * k:8 * k + 8, :], axis=0, keepdims=True)
            dw_ref[CONV_W:, :] = jnp.zeros((HALO - CONV_W, D), F32)
            _exchange_finish(x_ins, x_outs, sems)

    outs = pl.pallas_call(
        body, name=name, grid=(n_steps,),
        in_specs=[pl.BlockSpec((tb, D), lambda i: (i, 0)),
                  pl.BlockSpec((HALO, D), lambda i: (jnp.minimum((i + 1) * per, last_halo), 0)),
                  pl.BlockSpec((tb, 2 * D), lambda i: (i, 0)),
                  pl.BlockSpec((HALO, 2 * D), lambda i: (jnp.maximum(i * per - 1, 0), 0)),
                  pl.BlockSpec((HALO, D), lambda i: (0, 0))] + [ANY] * n_x,
        out_specs=[pl.BlockSpec((tb, 2 * D), lambda i: (i, 0)), pl.BlockSpec((1, 2 * D), lambda i: (0, 0)),
                   pl.BlockSpec((HALO, D), lambda i: (0, 0))] + [ANY] * n_x,
        out_shape=[jax.ShapeDtypeStruct((T, 2 * D), BF16), jax.ShapeDtypeStruct((1, 2 * D), F32),
                   jax.ShapeDtypeStruct((HALO, D), F32)] + [jax.ShapeDtypeStruct(p.shape, p.dtype) for p in exchange],
        scratch_shapes=[pltpu.VMEM((SUBLANES, tb + HALO, D), F32), pltpu.VMEM((SUBLANES, tb + HALO, D), F32),
                        pltpu.VMEM((8 * HALO, D), F32)] + _exchange_sems(n_x),
        compiler_params=_params(("arbitrary",)),
    )(dy, dy, a, a, dw_w, *exchange)
    return outs[0], outs[1], outs[2], outs[3:]


GATE_TB = 512


def _gate_fwd(fl, b_f, name):
    T = fl.shape[0]
    tb = min(GATE_TB, T)
    row = pl.BlockSpec((tb, LANES), lambda i: (i, 0))

    def body(fl_ref, b_ref, f_ref, f1_ref, f2_ref, f3_ref, carry):
        @pl.when(pl.program_id(0) == 0)
        def _():
            carry[...] = jnp.zeros_like(carry)

        z = fl_ref[...] + b_ref[...]
        lf = jnp.minimum(z, 0.0) - jnp.log(1.0 + jnp.exp(-jnp.abs(z)))
        tri = (lax.broadcasted_iota(jnp.int32, (tb, tb), 0) >= lax.broadcasted_iota(jnp.int32, (tb, tb), 1)).astype(F32)
        f = jnp.dot(tri, lf, precision=lax.Precision.HIGHEST, preferred_element_type=F32) + carry[...]
        carry[...] = f[tb - 1:tb, :]
        f_ref[...] = f
        f1_ref[...], f2_ref[...], f3_ref[...] = _split3(f)

    return pl.pallas_call(
        body, name=name, grid=(T // tb,), in_specs=[row, pl.BlockSpec((1, LANES), lambda i: (0, 0))],
        out_specs=[row] * 4, out_shape=[jax.ShapeDtypeStruct((T, LANES), F32)] * 4,
        scratch_shapes=[pltpu.VMEM((1, LANES), F32)],
        compiler_params=_params(("arbitrary",)),
    )(fl, b_f)


def _gate_bwd(dF_q, dF_k, fl, b_f, name):
    T = fl.shape[0]
    tb = min(GATE_TB, T)
    nb = T // tb
    row = pl.BlockSpec((tb, LANES), lambda i: (nb - 1 - i, 0))
    vec = pl.BlockSpec((1, LANES), lambda i: (0, 0))

    def body(dfq_ref, dfk_ref, fl_ref, b_ref, dfl_ref, db_ref, carry):
        @pl.when(pl.program_id(0) == 0)
        def _():
            carry[...] = jnp.zeros_like(carry)
            db_ref[...] = jnp.zeros_like(db_ref)

        tri = (lax.broadcasted_iota(jnp.int32, (tb, tb), 0) <= lax.broadcasted_iota(jnp.int32, (tb, tb), 1)).astype(F32)
        rc = jnp.dot(tri, dfq_ref[...] - dfk_ref[...], precision=lax.Precision.HIGHEST,
                     preferred_element_type=F32) + carry[...]
        carry[...] = rc[0:1, :]
        dfl = rc * jax.nn.sigmoid(-(fl_ref[...] + b_ref[...]))
        dfl_ref[...] = dfl
        db_ref[...] += jnp.sum(dfl, axis=0, keepdims=True)

    return pl.pallas_call(
        body, name=name, grid=(nb,), in_specs=[row, row, row, vec], out_specs=[row, vec],
        out_shape=[jax.ShapeDtypeStruct((T, LANES), F32), jax.ShapeDtypeStruct((1, LANES), F32)],
        scratch_shapes=[pltpu.VMEM((1, LANES), F32)],
        compiler_params=_params(("arbitrary",)),
    )(dF_q, dF_k, fl, b_f)


ATT_TB = 512
LAYOUT_TB = 2048
NT = (((1,), (1,)), ((), ()))
NN = (((1,), (0,)), ((), ()))


def _lane_insert(base, parts, first_lane):
    lane = lax.broadcasted_iota(jnp.int32, base.shape, 1)
    out = base
    for n, p in enumerate(parts):
        out = jnp.where(lane == first_lane + n, p, out)
    return out


def _causal(tb):
    return lax.broadcasted_iota(jnp.int32, (tb, tb), 0) >= lax.broadcasted_iota(jnp.int32, (tb, tb), 1)


def _seen_by(n_keys, n_queries):
    return (lax.broadcasted_iota(jnp.int32, (n_keys, n_queries), 1)
            >= lax.broadcasted_iota(jnp.int32, (n_keys, n_queries), 0))


def _other_head(x):
    return pltpu.roll(x, HEAD_DIM, 1)


def _attn_pack(proj, f1, f2, f3, name, gather_units, gather_shards):
    T = proj.shape[0]
    tb = min(LAYOUT_TB, T)
    pairs = N_HEADS // 2
    n_steps = T // tb
    n_g = len(gather_units)
    cols = lambda first: pl.BlockSpec((tb, LANES), lambda p, i: (i, first + p))
    gate = pl.BlockSpec((tb, LANES), lambda p, i: (i, 0))
    heads = pl.BlockSpec((2, tb, LANES), lambda p, i: (p, i, 0))
    norm = pl.BlockSpec((2, 8, LANES), lambda p, i: (p, 0, 0))

    def body(*refs):
        q_ref, k_ref, v_ref, f1_ref, f2_ref, f3_ref = refs[:6]
        g_ins = refs[6:6 + n_g]
        qa_ref, ka_ref, va_ref, qn_ref, kn_ref = refs[6 + n_g:11 + n_g]
        g_outs = refs[11 + n_g:11 + 2 * n_g]
        sems = refs[11 + 2 * n_g:]
        p, i = pl.program_id(0), pl.program_id(1)
        if n_g:
            pl.when(jnp.logical_and(p == 0, i == 0))(lambda: _gather_start(gather_units, g_ins, g_outs, sems))
        lane = lax.broadcasted_iota(jnp.int32, (tb, LANES), 1)
        data = lane < HEAD_DIM

        @pl.when(i == 0)
        def _():
            qn_ref[...] = jnp.zeros_like(qn_ref)
            kn_ref[...] = jnp.zeros_like(kn_ref)

        qv, kv, vv = q_ref[...].astype(F32) * 0.125, k_ref[...].astype(F32), v_ref[...].astype(F32)
        for e in range(2):
            fcol = [jnp.sum(jnp.where(lane == 2 * p + e, f[...], 0.0), axis=-1, keepdims=True)
                    for f in (f1_ref, f2_ref, f3_ref)]
            pick = (lambda t: t) if e == 0 else _other_head
            qd = jnp.where(data, pick(qv), 0.0).astype(BF16).astype(F32)
            kd = jnp.where(data, pick(kv), 0.0).astype(BF16).astype(F32)
            vd = jnp.where(data, pick(vv), 0.0)
            qa_ref[e] = _lane_insert(qd, fcol + [1.0, 1.0, 1.0], AUG_F).astype(BF16)
            ka_ref[e] = _lane_insert(kd, [1.0, 1.0, 1.0] + [-f for f in fcol] + [1.0, 1.0, 1.0], AUG_F).astype(BF16)
            va_ref[e] = _lane_insert(vd, [-1.0, -1.0, -1.0, 1.0], AUG_F).astype(BF16)
            qn_ref[e] = jnp.maximum(qn_ref[e], jnp.max(jnp.sum(qd * qd, axis=-1, keepdims=True)))
            kn_ref[e] = jnp.maximum(kn_ref[e], jnp.max(jnp.sum(kd * kd, axis=-1, keepdims=True)))
        if n_g:
            pl.when(jnp.logical_and(p == pairs - 1, i == n_steps - 1))(
                lambda: _gather_finish(gather_units, g_ins, g_outs, sems))

    aug = jax.ShapeDtypeStruct((N_HEADS, T, LANES), BF16)
    nrm = jax.ShapeDtypeStruct((N_HEADS, 8, LANES), F32)
    outs = pl.pallas_call(
        body, name=name, grid=(pairs, n_steps),
        in_specs=[cols(0), cols(pairs), cols(2 * pairs), gate, gate, gate] + [ANY] * n_g,
        out_specs=[heads, heads, heads, norm, norm] + [ANY] * n_g,
        out_shape=[aug, aug, aug, nrm, nrm]
        + [jax.ShapeDtypeStruct((N_CHIPS,) + shape, BF16) for _, shape in gather_units],
        scratch_shapes=_gather_sems(n_g) if n_g else [],
        compiler_params=_params(("arbitrary", "arbitrary") if n_g else ("parallel", "arbitrary")),
    )(proj, proj, proj, f1, f2, f3, *gather_shards)
    return outs[:5], outs[5:]


def _merge_heads(x_aug, scale, out_dtype, name, column=None):
    H, T, _ = x_aug.shape
    tb = min(LAYOUT_TB, T)

    def body(x_ref, o_ref, *col_ref):
        p = pl.program_id(1)
        lane = lax.broadcasted_iota(jnp.int32, (tb, LANES), 1)
        x0, x1 = x_ref[0], x_ref[1]
        o_ref[...] = (jnp.where(lane < HEAD_DIM, x0, _other_head(x1)) * scale).astype(out_dtype)
        if column is not None:
            @pl.when(p == 0)
            def _():
                col_ref[0][...] = jnp.zeros_like(col_ref[0])

            c0 = jnp.sum(jnp.where(lane == column, x0, 0.0), axis=-1, keepdims=True)
            c1 = jnp.sum(jnp.where(lane == column, x1, 0.0), axis=-1, keepdims=True)
            col_ref[0][...] += jnp.where(lane == 2 * p, c0, 0.0) + jnp.where(lane == 2 * p + 1, c1, 0.0)

    out_specs = [pl.BlockSpec((tb, LANES), lambda i, p: (i, p))]
    out_shape = [jax.ShapeDtypeStruct((T, D), out_dtype)]
    if column is not None:
        out_specs.append(pl.BlockSpec((tb, LANES), lambda i, p: (i, 0)))
        out_shape.append(jax.ShapeDtypeStruct((T, LANES), F32))
    return pl.pallas_call(
        body, name=name, grid=(T // tb, H // 2),
        in_specs=[pl.BlockSpec((2, tb, LANES), lambda i, p: (p, i, 0))],
        out_specs=out_specs, out_shape=out_shape,
        compiler_params=_params(("parallel", "arbitrary")),
    )(x_aug)


def _attn_prep(d_o, o_aug, name):
    H, T, _ = o_aug.shape
    tb = min(LAYOUT_TB, T)
    heads = pl.BlockSpec((2, tb, LANES), lambda p, i: (p, i, 0))

    def body(do_ref, o_ref, out_ref):
        lane = lax.broadcasted_iota(jnp.int32, (tb, LANES), 1)
        dov = do_ref[...]
        for e in range(2):
            d_e = jnp.where(lane < HEAD_DIM, dov if e == 0 else _other_head(dov), 0.0)
            delta = jnp.sum(d_e * o_ref[e], axis=-1, keepdims=True)
            out_ref[e] = _lane_insert(d_e, _split3(delta), AUG_F).astype(BF16)

    return pl.pallas_call(
        body, name=name, grid=(H // 2, T // tb),
        in_specs=[pl.BlockSpec((tb, LANES), lambda p, i: (i, p)), heads], out_specs=heads,
        out_shape=jax.ShapeDtypeStruct((H, T, LANES), BF16),
        compiler_params=_params(("parallel", "parallel")),
    )(d_o, o_aug)


def _flash_spec(n_in, n_out, H, nb, tb, T, resident, scratch, n_tables=3):
    blk = pl.BlockSpec((2, tb, LANES), lambda h, i, *_: (h, i, 0))
    whole = pl.BlockSpec((2, T, LANES), lambda h, i, *_: (h, 0, 0))
    return pltpu.PrefetchScalarGridSpec(
        num_scalar_prefetch=n_tables, grid=(H // 2, nb),
        in_specs=[whole if resident[n] else blk for n in range(n_in)],
        out_specs=[blk] * n_out, scratch_shapes=scratch)


def _first_live(qk_ref, fs_ref, fe_ref, h, i):
    top = qk_ref[h] + fs_ref[h, i]
    return lax.fori_loop(0, i, lambda j, n: n + jnp.where(top - fe_ref[h, j] < SKIP_BELOW, 1, 0), 0)


def _flash_fwd(qk, fs, fe, kmax, q_aug, k_aug, v_aug, name):
    H, T, _ = q_aug.shape
    tb = min(ATT_TB, T)

    def body(qk_ref, fs_ref, fe_ref, km_ref, q_ref, k_ref, v_ref, o_ref, q2_ref, m_s, acc_s, qm_s):
        h, i = 2 * pl.program_id(0), pl.program_id(1)
        acc_s[...] = jnp.zeros_like(acc_s)
        first = jnp.minimum(_first_live(qk_ref, fs_ref, fe_ref, h, i), _first_live(qk_ref, fs_ref, fe_ref, h + 1, i))
        bounded = jnp.logical_and(qk_ref[h] < FIXED_MAX_BELOW, qk_ref[h + 1] < FIXED_MAX_BELOW)

        def sweep(step):
            def loop_body(kb, carry):
                step(kb, False)
                return carry

            lax.fori_loop(first, i, loop_body, 0)
            step(i, True)

        @pl.when(bounded)
        def _():
            lane = lax.broadcasted_iota(jnp.int32, (tb, LANES), 1)
            for e in range(2):
                qf = q_ref[e].astype(F32)
                norm = jnp.sqrt(jnp.sum(jnp.where(lane < HEAD_DIM, qf * qf, 0.0), axis=-1, keepdims=True))
                bound = norm * (1.01 * km_ref[h + e]) + 1e-3
                m_s[e] = bound
                qm_s[e] = _lane_insert(qf, [-p for p in _split3(bound)], AUG_L).astype(BF16)

            def step(kb, masked):
                rows = pl.ds(pl.multiple_of(kb * tb, tb), tb)
                for e in range(2):
                    p = jnp.exp(lax.dot_general(qm_s[e], k_ref[e, rows, :], NT, preferred_element_type=F32))
                    if masked:
                        p = jnp.where(_causal(tb), p, 0.0)
                    acc_s[e] += lax.dot_general(p.astype(BF16), v_ref[e, rows, :], NN, preferred_element_type=F32)

            sweep(step)

        @pl.when(jnp.logical_not(bounded))
        def _():
            m_s[...] = jnp.full(m_s.shape, -jnp.inf, F32)

            def step(kb, masked):
                rows = pl.ds(pl.multiple_of(kb * tb, tb), tb)
                for e in range(2):
                    s = lax.dot_general(q_ref[e], k_ref[e, rows, :], NT, preferred_element_type=F32)
                    if masked:
                        s = jnp.where(_causal(tb), s, MASK_VALUE)
                    m_old = m_s[e]
                    m_new = jnp.maximum(m_old, jnp.max(s, axis=-1, keepdims=True))
                    p = jnp.exp(s - m_new)
                    acc_s[e] = jnp.exp(m_old - m_new) * acc_s[e] + lax.dot_general(
                        p.astype(BF16), v_ref[e, rows, :], NN, preferred_element_type=F32)
                    m_s[e] = m_new

            sweep(step)

        for e in range(2):
            acc = acc_s[e]
            lane = lax.broadcasted_iota(jnp.int32, acc.shape, 1)
            l = jnp.sum(jnp.where(lane == AUG_ONE, acc, 0.0), axis=-1, keepdims=True)
            o_ref[e] = acc / l
            lse = m_s[e] + jnp.log(l)
            q2_ref[e] = _lane_insert(q_ref[e].astype(F32), [-p for p in _split3(lse)], AUG_L).astype(BF16)

    return pl.pallas_call(
        body, name=name,
        grid_spec=_flash_spec(3, 2, H, T // tb, tb, T, (False, True, True),
                              [pltpu.VMEM((2, tb, 1), F32), pltpu.VMEM((2, tb, LANES), F32),
                               pltpu.VMEM((2, tb, LANES), BF16)], n_tables=4),
        out_shape=[jax.ShapeDtypeStruct((H, T, LANES), F32), jax.ShapeDtypeStruct((H, T, LANES), BF16)],
        compiler_params=_params(("parallel", "arbitrary")),
    )(qk, fs, fe, kmax, q_aug, k_aug, v_aug)


def _flash_dq(qk, fs, fe, q2, k_aug, do_aug, v_aug, name):
    H, T, _ = q2.shape
    tb = min(ATT_TB, T)

    def body(qk_ref, fs_ref, fe_ref, q_ref, k_ref, do_ref, v_ref, dq_ref, acc_s):
        h, i = 2 * pl.program_id(0), pl.program_id(1)
        acc_s[...] = jnp.zeros_like(acc_s)

        def step(kb, masked):
            rows = pl.ds(pl.multiple_of(kb * tb, tb), tb)
            for e in range(2):
                k = k_ref[e, rows, :]
                p = jnp.exp(lax.dot_general(q_ref[e], k, NT, preferred_element_type=F32))
                if masked:
                    p = jnp.where(_causal(tb), p, 0.0)
                ds = p * lax.dot_general(do_ref[e], v_ref[e, rows, :], NT, preferred_element_type=F32)
                acc_s[e] += lax.dot_general(ds.astype(BF16), k, NN, preferred_element_type=F32)

        def loop_body(kb, carry):
            step(kb, False)
            return carry

        first = jnp.minimum(_first_live(qk_ref, fs_ref, fe_ref, h, i), _first_live(qk_ref, fs_ref, fe_ref, h + 1, i))
        lax.fori_loop(first, i, loop_body, 0)
        step(i, True)
        dq_ref[...] = acc_s[...]

    return pl.pallas_call(
        body, name=name,
        grid_spec=_flash_spec(4, 1, H, T // tb, tb, T, (False, True, False, True), [pltpu.VMEM((2, tb, LANES), F32)]),
        out_shape=[jax.ShapeDtypeStruct((H, T, LANES), F32)],
        compiler_params=_params(("parallel", "arbitrary")),
    )(qk, fs, fe, q2, k_aug, do_aug, v_aug)[0]


def _flash_dkv(qk, fs, fe, q2, k_aug, do_aug, v_aug, name):
    H, T, _ = q2.shape
    tb = min(ATT_TB, T)
    half = tb // 2
    nb = T // tb

    def body(qk_ref, fs_ref, fe_ref, q_ref, k_ref, do_ref, v_ref, dk_ref, dv_ref, dk_s, dv_s):
        h, i = 2 * pl.program_id(0), pl.program_id(1)
        dk_s[...] = jnp.zeros_like(dk_s)
        dv_s[...] = jnp.zeros_like(dv_s)

        def step(qb, masked):
            base = pl.multiple_of(qb * tb, tb)
            parts = [(slice(0, half), 0, tb), (slice(half, tb), half, half)] if masked else [(slice(0, tb), 0, tb)]
            for e in range(2):
                for krows, q0, nq in parts:
                    queries = pl.ds(pl.multiple_of(base + q0, half), nq)
                    q, dov = q_ref[e, queries, :], do_ref[e, queries, :]
                    pt = jnp.exp(lax.dot_general(k_ref[e, krows, :], q, NT, preferred_element_type=F32))
                    if masked:
                        pt = jnp.where(_seen_by(krows.stop - krows.start, nq), pt, 0.0)
                    dst = pt * lax.dot_general(v_ref[e, krows, :], dov, NT, preferred_element_type=F32)
                    dv_s[e, krows, :] += lax.dot_general(pt.astype(BF16), dov, NN, preferred_element_type=F32)
                    dk_s[e, krows, :] += lax.dot_general(dst.astype(BF16), q, NN, preferred_element_type=F32)

        step(i, True)

        def loop_body(qb, carry):
            step(qb, False)
            return carry

        def live_after(head):
            base = qk_ref[head] - fe_ref[head, i]
            return lax.fori_loop(i + 1, nb, lambda b, n: n + jnp.where(base + fs_ref[head, b] < SKIP_BELOW, 0, 1), 0)

        lax.fori_loop(i + 1, i + 1 + jnp.maximum(live_after(h), live_after(h + 1)), loop_body, 0)
        dk_ref[...] = dk_s[...]
        dv_ref[...] = dv_s[...]

    return pl.pallas_call(
        body, name=name,
        grid_spec=_flash_spec(4, 2, H, nb, tb, T, (True, False, True, False),
                              [pltpu.VMEM((2, tb, LANES), F32), pltpu.VMEM((2, tb, LANES), F32)]),
        out_shape=[jax.ShapeDtypeStruct((H, T, LANES), F32)] * 2,
        compiler_params=_params(("parallel", "arbitrary")),
    )(qk, fs, fe, q2, k_aug, do_aug, v_aug)


def _add_pair(place, slab, got, name):
    _, half, cols = got.shape
    grid_spec = pltpu.PrefetchScalarGridSpec(
        num_scalar_prefetch=1, grid=(N_CHIPS,),
        in_specs=[pl.BlockSpec((None, half, cols), lambda j, s: (j, s[4], 0)),
                  pl.BlockSpec((None, half, cols), lambda j, s: (j, 0, 0))],
        out_specs=pl.BlockSpec((None, half, cols), lambda j, s: (j, 0, 0)))

    def body(s_ref, a_ref, b_ref, o_ref):
        o_ref[...] = (a_ref[...] + b_ref[...]).astype(BF16)

    return pl.pallas_call(
        body, name=name, grid_spec=grid_spec, out_shape=jax.ShapeDtypeStruct(got.shape, BF16),
        compiler_params=_params(("parallel",)),
    )(place, slab, got)


def _add_chips(place, pair, by_chip, name):
    _, half, cols = pair.shape
    tb = min(256, half)
    steps = half // tb
    slot = lambda n: pl.BlockSpec((None, tb, cols), lambda i, s: (s[n], i, 0))
    grid_spec = pltpu.PrefetchScalarGridSpec(
        num_scalar_prefetch=1, grid=(steps,), in_specs=[slot(0), slot(1), slot(2), slot(3)],
        out_specs=pl.BlockSpec((tb, cols), lambda i, s: (s[4] * steps + i, 0)))

    def body(s_ref, own_ref, b1_ref, b2_ref, b3_ref, o_ref):
        o_ref[...] = ((own_ref[...].astype(F32) + b1_ref[...].astype(F32)) + b2_ref[...].astype(F32)) \
            + b3_ref[...].astype(F32)

    return pl.pallas_call(
        body, name=name, grid_spec=grid_spec, out_shape=jax.ShapeDtypeStruct((2 * half, cols), F32),
        compiler_params=_params(("parallel",)),
    )(place, pair, by_chip, by_chip, by_chip)


def _add_slots(b, name, tb=128):
    n, rows, cols = b.shape
    tb = tb if rows % tb == 0 else rows
    specs = [pl.BlockSpec((None, tb, cols), functools.partial(lambda j, i: (j, i, 0), j)) for j in range(n)]

    def body(*refs):
        acc = refs[0][...]
        for r in refs[1:n]:
            acc = acc + r[...]
        refs[n][...] = acc

    return pl.pallas_call(
        body, name=name, grid=(rows // tb,), in_specs=specs, out_specs=pl.BlockSpec((tb, cols), lambda i: (i, 0)),
        out_shape=jax.ShapeDtypeStruct((rows, cols), F32), compiler_params=_params(("parallel",)),
    )(*([b] * n))


def _adamw(w, g, m, v, name):
    rows, cols = w.shape
    tr = 256 if rows % 256 == 0 else rows
    blk = pl.BlockSpec((tr, cols), lambda i: (i, 0))

    def body(w_ref, g_ref, m_ref, v_ref, d_ref, nm_ref, nv_ref):
        gv = g_ref[...]
        nm = ADAM_B1 * m_ref[...] + (1.0 - ADAM_B1) * gv
        nv = ADAM_B2 * v_ref[...] + (1.0 - ADAM_B2) * (gv * gv)
        m_hat = nm / (1.0 - ADAM_B1 ** ADAM_STEP)
        v_hat = nv / (1.0 - ADAM_B2 ** ADAM_STEP)
        d_ref[...] = -ADAM_LR * (m_hat / (jnp.sqrt(v_hat) + ADAM_EPS) + ADAM_WD * w_ref[...])
        nm_ref[...] = nm
        nv_ref[...] = nv

    return pl.pallas_call(
        body, name=name, grid=(rows // tr,), in_specs=[blk] * 4, out_specs=[blk] * 3,
        out_shape=[jax.ShapeDtypeStruct((rows, cols), F32)] * 3, compiler_params=_params(("parallel",)),
    )(w, g, m, v)


ANY = pl.BlockSpec(memory_space=pl.ANY)


def _place():
    x, y, c = lax.axis_index("x"), lax.axis_index("y"), lax.axis_index("c")
    others = [(1 - x, y), (x, 1 - y), (1 - x, 1 - y)]
    return x, y, c, 2 * x + y, others


def _half(ref, rows, h):
    return ref.at[pl.ds(h * (rows // 2), rows // 2), :]


def _gather_sems(n_u):
    return [pltpu.SemaphoreType.DMA((6 * n_u,)), pltpu.SemaphoreType.DMA((6 * n_u,)),
            pltpu.SemaphoreType.DMA((n_u,)), pltpu.SemaphoreType.DMA((n_u,))]


def _gather_copies(units, ins, outs, sems):
    send_sems, recv_sems, own_send_sems, own_recv_sems = sems
    x, y, c, me, others = _place()
    sibling = (x, y, 1 - c)

    def copy(u, k, chip, h, to, src=None):
        dst = _half(outs[u].at[chip], units[u][1][0], h)
        return pltpu.make_async_remote_copy(
            src_ref=dst if src is None else src, dst_ref=dst, send_sem=send_sems.at[6 * u + k],
            recv_sem=recv_sems.at[6 * u + k], device_id=to, device_id_type=MESH)

    pairs = [(u, j, 2 * ox + oy, (ox, oy)) for u in range(len(units)) for j, (ox, oy) in enumerate(others)]
    own = [pltpu.make_async_remote_copy(
        src_ref=ins[u], dst_ref=outs[u].at[me], send_sem=own_send_sems.at[u], recv_sem=own_recv_sems.at[u],
        device_id=sibling, device_id_type=MESH) for u in range(len(units))]
    first = [copy(u, j, me, c, (*to, c), src=_half(ins[u], units[u][1][0], c)) for u, j, _, to in pairs]
    arrivals = [copy(u, j, chip, c, (x, y, c)) for u, j, chip, _ in pairs]
    passed = [copy(u, 3 + j, chip, c, sibling) for u, j, chip, _ in pairs]
    from_sibling = [copy(u, 3 + j, chip, 1 - c, (x, y, c)) for u, j, chip, _ in pairs]
    return own, first, arrivals, passed, from_sibling


def _gather_start(units, ins, outs, sems):
    own, first, _, _, _ = _gather_copies(units, ins, outs, sems)
    for cp in own + first:
        cp.start()


def _gather_finish(units, ins, outs, sems):
    own, first, arrivals, passed, from_sibling = _gather_copies(units, ins, outs, sems)
    for arrived, onward in zip(arrivals, passed):
        arrived.wait_recv()
        onward.start()
    for cp in from_sibling:
        cp.wait_recv()
    for cp in first + passed:
        cp.wait_send()
    for cp in own:
        cp.wait()


def _allgather_weights(units, shards, name):
    n_u = len(units)

    def body(*refs):
        ins, outs, sems = refs[:n_u], refs[n_u:2 * n_u], refs[2 * n_u:]
        _gather_start(units, ins, outs, sems)
        _gather_finish(units, ins, outs, sems)

    return pl.pallas_call(
        body, name=name, in_specs=[ANY] * n_u, out_specs=[ANY] * n_u,
        out_shape=[jax.ShapeDtypeStruct((N_CHIPS,) + shape, BF16) for _, shape in units],
        scratch_shapes=_gather_sems(n_u),
    )(*shards)


def _sibling_copies(units, ins, got, sems):
    send_sems, recv_sems = sems
    x, y, c, _, _ = _place()
    copies = []
    for u, (_, (rows, _), _) in enumerate(units):
        half = rows // 2
        copies.append(pltpu.make_async_remote_copy(
            src_ref=ins[u].at[:, pl.ds((1 - c) * half, half), :], dst_ref=got[u], send_sem=send_sems.at[u],
            recv_sem=recv_sems.at[u], device_id=(x, y, 1 - c), device_id_type=MESH))
    return copies


def _sibling_start(units, ins, got, sems):
    for cp in _sibling_copies(units, ins, got, sems):
        cp.start()


def _sibling_finish(units, ins, got, sems):
    for cp in _sibling_copies(units, ins, got, sems):
        cp.wait()


def _sibling_exchange(units, slabs, name):
    n_u = len(units)

    def body(*refs):
        ins, got, sems = refs[:n_u], refs[n_u:2 * n_u], refs[2 * n_u:]
        _sibling_start(units, ins, got, sems)
        _sibling_finish(units, ins, got, sems)

    return pl.pallas_call(
        body, name=name, in_specs=[ANY] * n_u, out_specs=[ANY] * n_u,
        out_shape=[jax.ShapeDtypeStruct((N_CHIPS, rows // 2, cols), F32) for _, (rows, cols), _ in units],
        scratch_shapes=[pltpu.SemaphoreType.DMA((n_u,)), pltpu.SemaphoreType.DMA((n_u,))],
    )(*slabs)


def _exchange_sems(n_u):
    return [pltpu.SemaphoreType.DMA((3 * n_u,)), pltpu.SemaphoreType.DMA((3 * n_u,))]


def _exchange_copies(ins, outs, sems):
    send_sems, recv_sems = sems
    x, y, c, me, others = _place()
    sends = [pltpu.make_async_remote_copy(
        src_ref=ins[u].at[2 * ox + oy], dst_ref=outs[u].at[me], send_sem=send_sems.at[3 * u + j],
        recv_sem=recv_sems.at[3 * u + j], device_id=(ox, oy, c), device_id_type=MESH)
        for u in range(len(ins)) for j, (ox, oy) in enumerate(others)]
    arrivals = [pltpu.make_async_remote_copy(
        src_ref=ins[u].at[me], dst_ref=outs[u].at[2 * ox + oy], send_sem=send_sems.at[3 * u + j],
        recv_sem=recv_sems.at[3 * u + j], device_id=(x, y, c), device_id_type=MESH)
        for u in range(len(ins)) for j, (ox, oy) in enumerate(others)]
    return sends, arrivals


def _exchange_start(ins, outs, sems):
    for cp in _exchange_copies(ins, outs, sems)[0]:
        cp.start()


def _exchange_finish(ins, outs, sems):
    sends, arrivals = _exchange_copies(ins, outs, sems)
    for cp in arrivals:
        cp.wait_recv()
    for cp in sends:
        cp.wait_send()


def _chip_exchange(pairs, small):
    n_u = len(pairs)

    def body(*refs):
        ins, s_ref = refs[:n_u], refs[n_u]
        outs, sall_ref = refs[n_u + 1:2 * n_u + 1], refs[2 * n_u + 1]
        send_sems, recv_sems, ssend_sems, srecv_sems, local_sem = refs[2 * n_u + 2:]
        x, y, c, _, _ = _place()
        dev = 4 * x + 2 * y + c
        local = pltpu.make_async_copy(s_ref, sall_ref.at[dev], local_sem)
        flips = [(fx, fy, fc) for fx in (0, 1) for fy in (0, 1) for fc in (0, 1)][1:]
        small_sends = [pltpu.make_async_remote_copy(
            src_ref=s_ref, dst_ref=sall_ref.at[dev], send_sem=ssend_sems.at[n], recv_sem=srecv_sems.at[n],
            device_id=(x ^ fx, y ^ fy, c ^ fc), device_id_type=MESH) for n, (fx, fy, fc) in enumerate(flips)]
        local.start()
        _exchange_start(ins, outs, (send_sems, recv_sems))
        for cp in small_sends:
            cp.start()
        _exchange_finish(ins, outs, (send_sems, recv_sems))
        for n, (fx, fy, fc) in enumerate(flips):
            src_dev = 4 * (x ^ fx) + 2 * (y ^ fy) + (c ^ fc)
            pltpu.make_async_remote_copy(
                src_ref=s_ref, dst_ref=sall_ref.at[src_dev], send_sem=ssend_sems.at[n], recv_sem=srecv_sems.at[n],
                device_id=(x, y, c), device_id_type=MESH).wait_recv()
        for cp in small_sends:
            cp.wait_send()
        local.wait()

    outs = pl.pallas_call(
        body, name="grad_chip_exchange", in_specs=[ANY] * (n_u + 1), out_specs=[ANY] * (n_u + 1),
        out_shape=[jax.ShapeDtypeStruct(p.shape, p.dtype) for p in pairs]
        + [jax.ShapeDtypeStruct((8, SMALL_ALL_ROWS, LANES), F32)],
        scratch_shapes=_exchange_sems(n_u) + [pltpu.SemaphoreType.DMA((7,)), pltpu.SemaphoreType.DMA((7,)),
                                              pltpu.SemaphoreType.DMA],
    )(*pairs, small)
    return outs[:n_u], outs[n_u]


def _sibling_share(grads):
    n_u = len(GRAD_UNITS)

    def body(*refs):
        ins, outs = refs[:n_u], refs[n_u:2 * n_u]
        send_sems, recv_sems = refs[2 * n_u:]
        x, y, c, _, _ = _place()

        def rows_of(ref, u, h):
            half = GRAD_UNITS[u][1][0] // 2
            return ref.at[pl.ds(h * half, half), :]

        sends = [pltpu.make_async_remote_copy(
            src_ref=rows_of(ins[u], u, c), dst_ref=rows_of(outs[u], u, c), send_sem=send_sems.at[u],
            recv_sem=recv_sems.at[u], device_id=(x, y, 1 - c), device_id_type=MESH) for u in range(n_u)]
        for cp in sends:
            cp.start()
        for u in range(n_u):
            pltpu.make_async_remote_copy(
                src_ref=rows_of(ins[u], u, c), dst_ref=rows_of(outs[u], u, 1 - c), send_sem=send_sems.at[u],
                recv_sem=recv_sems.at[u], device_id=(x, y, c), device_id_type=MESH).wait_recv()
        for cp in sends:
            cp.wait_send()

    return pl.pallas_call(
        body, name="grad_sibling_share", in_specs=[ANY] * n_u, out_specs=[ANY] * n_u,
        out_shape=[jax.ShapeDtypeStruct(g.shape, F32) for g in grads],
        input_output_aliases={u: u for u in range(n_u)},
        scratch_shapes=[pltpu.SemaphoreType.DMA((n_u,)), pltpu.SemaphoreType.DMA((n_u,))],
    )(*grads)


def kernel(x, g_mix_pre, g_mix_post, g_ffn_pre, g_ffn_post, conv_pw1_w, conv_pw1_b, conv_dw_w, conv_dw_b, conv_ln_g, conv_ln_b, conv_pw2_w, conv_pw2_b, attn_w_in, attn_b_f, attn_w_o, mlp_w_up, mlp_w_down, loss_target, m_g_mix_pre, m_g_mix_post, m_g_ffn_pre, m_g_ffn_post, m_conv_pw1_w, m_conv_pw1_b, m_conv_dw_w, m_conv_dw_b, m_conv_ln_g, m_conv_ln_b, m_conv_pw2_w, m_conv_pw2_b, m_attn_w_in, m_attn_b_f, m_attn_w_o, m_mlp_w_up, m_mlp_w_down, v_g_mix_pre, v_g_mix_post, v_g_ffn_pre, v_g_ffn_post, v_conv_pw1_w, v_conv_pw1_b, v_conv_dw_w, v_conv_dw_b, v_conv_ln_g, v_conv_ln_b, v_conv_pw2_w, v_conv_pw2_b, v_attn_w_in, v_attn_b_f, v_attn_w_o, v_mlp_w_up, v_mlp_w_down):
    weights = dict(g_mix_pre=g_mix_pre, g_mix_post=g_mix_post, g_ffn_pre=g_ffn_pre, g_ffn_post=g_ffn_post, conv_pw1_w=conv_pw1_w, conv_pw1_b=conv_pw1_b, conv_dw_w=conv_dw_w, conv_dw_b=conv_dw_b, conv_ln_g=conv_ln_g, conv_ln_b=conv_ln_b, conv_pw2_w=conv_pw2_w, conv_pw2_b=conv_pw2_b, attn_w_in=attn_w_in, attn_b_f=attn_b_f, attn_w_o=attn_w_o, mlp_w_up=mlp_w_up, mlp_w_down=mlp_w_down)
    mom_m = dict(g_mix_pre=m_g_mix_pre, g_mix_post=m_g_mix_post, g_ffn_pre=m_g_ffn_pre, g_ffn_post=m_g_ffn_post, conv_pw1_w=m_conv_pw1_w, conv_pw1_b=m_conv_pw1_b, conv_dw_w=m_conv_dw_w, conv_dw_b=m_conv_dw_b, conv_ln_g=m_conv_ln_g, conv_ln_b=m_conv_ln_b, conv_pw2_w=m_conv_pw2_w, conv_pw2_b=m_conv_pw2_b, attn_w_in=m_attn_w_in, attn_b_f=m_attn_b_f, attn_w_o=m_attn_w_o, mlp_w_up=m_mlp_w_up, mlp_w_down=m_mlp_w_down)
    mom_v = dict(g_mix_pre=v_g_mix_pre, g_mix_post=v_g_mix_post, g_ffn_pre=v_g_ffn_pre, g_ffn_post=v_g_ffn_post, conv_pw1_w=v_conv_pw1_w, conv_pw1_b=v_conv_pw1_b, conv_dw_w=v_conv_dw_w, conv_dw_b=v_conv_dw_b, conv_ln_g=v_conv_ln_g, conv_ln_b=v_conv_ln_b, conv_pw2_w=v_conv_pw2_w, conv_pw2_b=v_conv_pw2_b, attn_w_in=v_attn_w_in, attn_b_f=v_attn_b_f, attn_w_o=v_attn_w_o, mlp_w_up=v_mlp_w_up, mlp_w_down=v_mlp_w_down)
    order = [n for n, _ in SMALL[:4]] + ["conv_pw1_w", "conv_pw1_b", "conv_dw_w", "conv_dw_b", "conv_ln_g", "conv_ln_b",
                                          "conv_pw2_w", "conv_pw2_b", "attn_w_in", "attn_b_f", "attn_w_o", "mlp_w_up",
                                          "mlp_w_down"]
    T = x.shape[1]
    x0 = x.reshape(T, D)
    target = loss_target.reshape(T, D)
    row = lambda a, l: a[l:l + 1, :]

    def shard(unit):
        if unit[:-1] in ("up", "down"):
            return weights["mlp_w_" + unit[:-1]][int(unit[-1])].astype(BF16)
        w = weights[unit][0].astype(BF16)
        return jnp.pad(w, ((0, HALO - CONV_W), (0, 0))) if unit == "conv_dw_w" else w

    by_cols = lambda g: g.transpose(1, 0, 2).reshape(g.shape[1], N_CHIPS * g.shape[2])
    by_rows = lambda g: g.reshape(N_CHIPS * g.shape[1], g.shape[2])
    g_pw1, g_dw, g_pw2 = _allgather_weights(GATHER_AT_START, [shard(n) for n, _ in GATHER_AT_START], "allgather_conv")
    w_pw1, w_pw2, w_dw = by_cols(g_pw1), by_rows(g_pw2), by_cols(g_dw).astype(F32)
    b_f = jnp.pad(attn_b_f, ((0, 0), (0, LANES - N_HEADS)))

    def mlp_fwd(h, l):
        (act,) = _matmul(h, w_up[l], "nn", (BF16,), _epi_sqrelu, f"mlp{l}_up", tm=2048)
        (mo,) = _matmul(act, w_down[l], "nn", (F32,), _epi_plain, f"mlp{l}_down", tk=DFF)
        return act, mo

    def mlp0_fwd(h):
        ride = [dict(gather_units=units, gather_shards=[shard(n) for n, _ in units]) for units in GATHER_BEHIND_MLP]
        (act,), (g_up1,) = _matmul(h, w_up[0], "nn", (BF16,), _epi_sqrelu, "mlp0_up", tm=2048, **ride[0])
        (mo,), (g_down1,) = _matmul(act, w_down[0], "nn", (F32,), _epi_plain, "mlp0_down", tk=DFF, **ride[1])
        w_up.append(by_cols(g_up1))
        w_down.append(by_rows(g_down1))
        return act, mo

    (h0,) = _resnorm_fwd(x0, None, None, row(g_mix_pre, 0), "norm_in")
    (a0,) = _matmul(h0, w_pw1, "nn", (F32,), _epi_bias, "conv_pw1", extras=((conv_pw1_b, "row"),))
    y0, z0, (g_up0, g_down0, g_win, g_wo) = _conv_fwd(
        a0, w_dw, conv_dw_b, conv_ln_g, conv_ln_b, "conv_fwd", GATHER_BEHIND_CONV,
        [shard(n) for n, _ in GATHER_BEHIND_CONV])
    w_up, w_down = [by_cols(g_up0)], [by_rows(g_down0)]
    w_o = by_rows(g_wo)
    w_in = jnp.pad(by_cols(g_win), ((0, 0), (0, W_IN_PAD - W_IN_COLS)))
    (m0,) = _matmul(z0, w_pw2, "nn", (F32,), _epi_bias, "conv_pw2", extras=((conv_pw2_b, "row"),))
    x1, h1 = _resnorm_fwd(x0, m0, row(g_mix_post, 0), row(g_ffn_pre, 0), "norm_conv_out")
    act0, mo0 = mlp0_fwd(h1)
    x2, h2 = _resnorm_fwd(x1, mo0, row(g_ffn_post, 0), row(g_mix_pre, 1), "norm_mlp0_out")

    (proj,) = _matmul(h2, w_in[:, :3 * D], "nn", (BF16,), _epi_plain, "attn_in", tm=2048)
    (fl,) = _matmul(h2, w_in[:, 3 * D:], "nn", (F32,), _epi_plain, "attn_gate_in")
    f_all, f1, f2, f3 = _gate_fwd(fl, b_f, "gate_fwd")
    (q_aug, k_aug, v_aug, qn2, kn2), _ = _attn_pack(proj, f1, f2, f3, "attn_pack", (), ())
    tb = min(ATT_TB, T)
    qk = 2.02 * jnp.sqrt(qn2[:, 0, 0] * kn2[:, 0, 0]) + 1.0
    fs, fe = f_all[0::tb, :N_HEADS].T, f_all[tb - 1::tb, :N_HEADS].T
    o_aug, q2_aug = _flash_fwd(qk, fs, fe, jnp.sqrt(kn2[:, 0, 0]), q_aug, k_aug, v_aug, "flash_fwd")
    (o_bf,) = _merge_heads(o_aug, 1.0, BF16, "attn_merge_o")
    (m1,) = _matmul(o_bf, w_o, "nn", (F32,), _epi_plain, "attn_out")
    x3, h3 = _resnorm_fwd(x2, m1, row(g_mix_post, 1), row(g_ffn_pre, 1), "norm_attn_out")
    act1, mo1 = mlp_fwd(h3, 1)
    dy, loss_part = _resnorm_loss(x3, mo1, row(g_ffn_post, 1), target, "loss")
    loss = lax.psum(loss_part[0, 0], ("x", "y", "c"))

    G = {}
    chip = 2 * lax.axis_index("x") + lax.axis_index("y")
    place = jnp.stack([chip] + [k + (k >= chip).astype(jnp.int32) for k in range(N_CHIPS - 1)]
                      + [lax.axis_index("c")]).astype(jnp.int32)

    def mlp_bwd(d_out, mo, act, h, x_in, l):
        d_mo, dg_post, _ = _norm_bwd(mo, row(g_ffn_post, l), d_out, None, BF16, f"mlp{l}_post_bwd")
        (d_up,) = _matmul(d_mo, w_down[l], "nt", (BF16,), _epi_dsqrelu, f"mlp{l}_dact", extras=((act, "tile"),), tm=2048)
        (dw_down,) = _matmul(act, d_mo, "tn", (F32,), _epi_plain, tk=2048, name=f"mlp{l}_dwdown")
        (dw_up,) = _matmul(h, d_up, "tn", (F32,), _epi_plain, tk=2048, name=f"mlp{l}_dwup", shard_out=True)
        (d_h,) = _matmul(d_up, w_up[l], "nt", (F32,), _epi_plain, f"mlp{l}_dh", tk=DFF)
        d_in, dg_pre, _ = _norm_bwd(x_in, row(g_ffn_pre, l), d_h, d_out, F32, f"mlp{l}_pre_bwd")
        G[f"up{l}"], G[f"down{l}"] = dw_up, dw_down.reshape(N_CHIPS, D, D)
        return d_in, dg_post, dg_pre

    d_x3, dg_ffn_post1, dg_ffn_pre1 = mlp_bwd(dy, mo1, act1, h3, x3, 1)

    d_m1, dg_mix_post1, _ = _norm_bwd(m1, row(g_mix_post, 1), d_x3, None, BF16, "attn_post_bwd")
    (dw_o,) = _matmul(o_bf, d_m1, "tn", (F32,), _epi_plain, tk=2048, name="attn_dwo")
    G["attn_w_o"] = dw_o.reshape(N_CHIPS, D // N_CHIPS, D)
    (d_o,) = _matmul(d_m1, w_o, "nt", (F32,), _epi_plain, "attn_do")
    do_aug = _attn_prep(d_o, o_aug, "attn_prep")
    dq_aug = _flash_dq(qk, fs, fe, q2_aug, k_aug, do_aug, v_aug, "flash_dq")
    dk_aug, dv_aug = _flash_dkv(qk, fs, fe, q2_aug, k_aug, do_aug, v_aug, "flash_dkv")
    d_q, dF_q = _merge_heads(dq_aug, 0.125, BF16, "attn_merge_dq", column=AUG_F)
    d_k, dF_k = _merge_heads(dk_aug, 1.0, BF16, "attn_merge_dk", column=AUG_ONE)
    (d_v,) = _merge_heads(dv_aug, 1.0, BF16, "attn_merge_dv")
    d_fl, db_f = _gate_bwd(dF_q, dF_k, fl, b_f, "gate_bwd")
    d_proj = jnp.concatenate([d_q, d_k, d_v, d_fl.astype(BF16)], axis=1)
    (dw_in,) = _matmul(h2, d_proj, "tn", (F32,), _epi_plain, tk=2048, name="attn_dwin", tn=640)
    G["attn_w_in"] = dw_in[:, :W_IN_COLS].reshape(D, N_CHIPS, W_IN_COLS // N_CHIPS).transpose(1, 0, 2)
    (d_h2,) = _matmul(d_proj, w_in, "nt", (F32,), _epi_plain, "attn_dh", tk=W_IN_PAD)
    d_x2, dg_mix_pre1, _ = _norm_bwd(x2, row(g_mix_pre, 1), d_h2, d_x3, F32, "attn_pre_bwd")

    d_x1, dg_ffn_post0, dg_ffn_pre0 = mlp_bwd(d_x2, mo0, act0, h1, x1, 0)

    d_m0, dg_mix_post0, db_pw2 = _norm_bwd(m0, row(g_mix_post, 0), d_x1, None, BF16, "conv_post_bwd")
    (dw_pw2,) = _matmul(z0, d_m0, "tn", (F32,), _epi_plain, tk=2048, name="conv_dwpw2")
    G["conv_pw2_w"] = dw_pw2.reshape(N_CHIPS, D // N_CHIPS, D)
    (d_z0,) = _matmul(d_m0, w_pw2, "nt", (F32,), _epi_plain, "conv_dz")
    conv_units = [u for u in GRAD_UNITS if u[0].startswith("conv")]
    attn_units = [u for u in GRAD_UNITS if u not in conv_units]
    attn_slabs = [G[n] for n, _, _ in attn_units]
    d_y0, dln_g, dln_b, ddw_b, attn_got = _conv_bwd_ln(d_z0, y0, conv_ln_g, conv_ln_b, "conv_bwd_ln", attn_units,
                                                       attn_slabs)
    pairs = {n: _add_pair(place, s, g, f"grad_pair_{n}") for s, g, (n, _, _) in zip(attn_slabs, attn_got, attn_units)}
    d_a0, db_pw1, ddw_w, attn_by_chip = _conv_bwd_dw(d_y0, a0, w_dw, "conv_bwd_dw",
                                                    [pairs[n] for n, _, _ in attn_units])
    by_chip = {n: b for b, (n, _, _) in zip(attn_by_chip, attn_units)}
    (G["conv_pw1_w"],) = _matmul(h0, d_a0, "tn", (F32,), _epi_plain, tk=2048, name="conv_dwpw1", tn=512, shard_out=True)
    (d_h0,) = _matmul(d_a0, w_pw1, "nt", (F32,), _epi_plain, "conv_dh", tk=2 * D)
    d_x0, dg_mix_pre0, _ = _norm_bwd(x0, row(g_mix_pre, 0), d_h0, d_x1, F32, "conv_pre_bwd")

    small_local = {
        "g_mix_pre": jnp.concatenate([dg_mix_pre0, dg_mix_pre1]), "g_mix_post": jnp.concatenate([dg_mix_post0, dg_mix_post1]),
        "g_ffn_pre": jnp.concatenate([dg_ffn_pre0, dg_ffn_pre1]), "g_ffn_post": jnp.concatenate([dg_ffn_post0, dg_ffn_post1]),
        "conv_pw1_b": db_pw1, "conv_dw_b": ddw_b, "conv_ln_g": dln_g, "conv_ln_b": dln_b, "conv_pw2_b": db_pw2,
        "attn_b_f": db_f[:, :N_HEADS], "conv_dw_w_full": ddw_w,
    }

    conv_slabs = [G[n] for n, _, _ in conv_units]
    conv_got = _sibling_exchange(conv_units, conv_slabs, "grad_sibling_exchange_conv")
    pairs.update({n: _add_pair(place, s, g, f"grad_pair_{n}")
                  for s, g, (n, _, _) in zip(conv_slabs, conv_got, conv_units)})
    conv_by_chip, small_all = _chip_exchange([pairs[n] for n, _, _ in conv_units],
                                             _pack([small_local[n] for n, _ in SMALL_ALL], SMALL_ALL_ROWS))
    by_chip.update({n: b for b, (n, _, _) in zip(conv_by_chip, conv_units)})
    unit_grads = _sibling_share([_add_chips(place, pairs[n], by_chip[n], f"grad_chips_{n}") for n, _, _ in GRAD_UNITS])
    grads = _unpack(_add_slots(small_all, "grad_add_small", tb=SMALL_ALL_ROWS), SMALL_ALL)
    grads["conv_dw_w"] = lax.dynamic_slice(grads.pop("conv_dw_w_full"), (0, chip * 256), (CONV_W, 256))[None]

    delta, new_m, new_v = {}, {}, {}
    per_layer = {}
    for g, (unit, shape, (name, layer)) in zip(unit_grads, GRAD_UNITS):
        view = lambda d: (d[name] if layer is None else d[name][layer]).reshape(shape)
        per_layer.setdefault(name, []).append((g,) + tuple(_adamw(view(weights), g, view(mom_m), view(mom_v),
                                                                  f"adamw_{unit}")))
    for name, parts in per_layer.items():
        full = weights[name].shape
        join = lambda n: (parts[0][n] if len(parts) == 1 else jnp.stack([p[n] for p in parts])).reshape(full)
        grads[name], delta[name], new_m[name], new_v[name] = join(0), join(1), join(2), join(3)
    dw2 = lambda d: d["conv_dw_w"].reshape(CONV_W, 256)
    d, nm, nv = _adamw(dw2(weights), dw2(grads), dw2(mom_m), dw2(mom_v), "adamw_conv_dw_w")
    full = weights["conv_dw_w"].shape
    delta["conv_dw_w"], new_m["conv_dw_w"], new_v["conv_dw_w"] = d.reshape(full), nm.reshape(full), nv.reshape(full)
    packs = [_pack([src[n] for n, _ in SMALL], SMALL_ROWS) for src in (weights, grads, mom_m, mom_v)]
    d, nm, nv = _adamw(*packs, "adamw_small")
    for dst, packed in ((delta, d), (new_m, nm), (new_v, nv)):
        dst.update(_unpack(packed, SMALL))

    grad_x = d_x0.reshape(x.shape)
    return (loss, grad_x, *[grads[n] for n in order], *[delta[n] for n in order], *[new_m[n] for n in order],
            *[new_v[n] for n in order])
```

```python
import functools

import jax
import jax.numpy as jnp
from jax import lax
from jax.experimental import pallas as pl
from jax.experimental.pallas import tpu as pltpu

F32 = jnp.float32
BF16 = jnp.bfloat16
MESH = pl.DeviceIdType.MESH

D = 1024
DFF = 4096
N_HEADS = 16
HEAD_DIM = 64
CONV_W = 31
HALO = 32
RMS_EPS = 1e-6
LN_EPS = 1e-5
MASK_VALUE = -1e30
W_IN_COLS = 3 * D + N_HEADS
W_IN_PAD = 3200
LANES = 128
SUBLANES = 8
N_CHIPS = 4
VMEM_LIMIT = 56 * 1024 * 1024
MATMUL_CHUNK = 256

ADAM_LR = 0.001
ADAM_B1 = 0.9
ADAM_B2 = 0.999
ADAM_EPS = 1e-08
ADAM_WD = 0.01
ADAM_STEP = 10

AUG_F = 64
AUG_ONE = 67
AUG_L = 70
SKIP_BELOW = -104.0
FIXED_MAX_BELOW = 40.0

GATHER_AT_START = (("conv_pw1_w", (1024, 512)), ("conv_dw_w", (HALO, 256)), ("conv_pw2_w", (256, 1024)))
GATHER_BEHIND_CONV = (("up0", (1024, 1024)), ("down0", (1024, 1024)), ("attn_w_in", (1024, 772)),
                      ("attn_w_o", (256, 1024)))
GATHER_BEHIND_MLP = ((("up1", (1024, 1024)),), (("down1", (1024, 1024)),))
GRAD_UNITS = (
    ("conv_pw1_w", (1024, 512), ("conv_pw1_w", None)),
    ("conv_pw2_w", (256, 1024), ("conv_pw2_w", None)),
    ("attn_w_in", (1024, 772), ("attn_w_in", None)),
    ("attn_w_o", (256, 1024), ("attn_w_o", None)),
    ("up0", (1024, 1024), ("mlp_w_up", 0)),
    ("up1", (1024, 1024), ("mlp_w_up", 1)),
    ("down0", (1024, 1024), ("mlp_w_down", 0)),
    ("down1", (1024, 1024), ("mlp_w_down", 1)),
)
SMALL = (
    ("g_mix_pre", (2, 1024)), ("g_mix_post", (2, 1024)), ("g_ffn_pre", (2, 1024)), ("g_ffn_post", (2, 1024)),
    ("conv_pw1_b", (1, 2048)), ("conv_dw_b", (1, 1024)), ("conv_ln_g", (1, 1024)), ("conv_ln_b", (1, 1024)),
    ("conv_pw2_b", (1, 1024)), ("attn_b_f", (1, 16)),
)
SMALL_ROWS = 120
SMALL_ALL = SMALL + (("conv_dw_w_full", (HALO, 1024)),)
SMALL_ALL_ROWS = 376


def _size(shape):
    n = 1
    for s in shape:
        n *= s
    return n


def _pack(arrays, rows):
    flat = jnp.concatenate([a.reshape(-1) for a in arrays])
    return jnp.pad(flat, (0, rows * LANES - flat.shape[0])).reshape(rows, LANES)


def _unpack(packed, table):
    flat = packed.reshape(-1)
    out, off = {}, 0
    for name, shape in table:
        n = _size(shape)
        out[name] = flat[off:off + n].reshape(shape)
        off += n
    return out


def _params(sem):
    return pltpu.CompilerParams(dimension_semantics=sem, vmem_limit_bytes=VMEM_LIMIT)


def _rms(x, g):
    return x * lax.rsqrt(jnp.mean(x * x, axis=-1, keepdims=True) + RMS_EPS) * g


def _split3(v):
    p1 = v.astype(BF16).astype(F32)
    r = v - p1
    p2 = r.astype(BF16).astype(F32)
    p3 = (r - p2).astype(BF16).astype(F32)
    return p1, p2, p3


def _matmul(a, b, mode, out_dtypes, epi, name, extras=(), tm=1024, tn=1024, tk=1024, shard_out=False,
            gather_units=(), gather_shards=()):
    (K, M) = a.shape if mode == "tn" else a.shape[::-1]
    N = b.shape[0] if mode == "nt" else b.shape[1]
    tm, tn, tk = min(tm, M), min(tn, N), min(tk, K)
    nk = K // tk
    if mode == "tn":
        a_spec = pl.BlockSpec((tk, tm), lambda i, j, k: (k, i))
    else:
        a_spec = pl.BlockSpec((tm, tk), lambda i, j, k: (i, k))
    if mode == "nt":
        b_spec = pl.BlockSpec((tn, tk), lambda i, j, k: (j, k))
    else:
        b_spec = pl.BlockSpec((tk, tn), lambda i, j, k: (k, j))
    if shard_out:
        per = N // N_CHIPS // tn
        o_spec = pl.BlockSpec((None, tm, tn), lambda i, j, k: (j // per, i, j % per))
        o_shape = (N_CHIPS, M, N // N_CHIPS)
    else:
        o_spec = pl.BlockSpec((tm, tn), lambda i, j, k: (i, j))
        o_shape = (M, N)
    dims = {"nn": (((1,), (0,)), ((), ())), "nt": (((1,), (1,)), ((), ())), "tn": (((0,), (0,)), ((), ()))}[mode]
    ex_specs = []
    for _, kind in extras:
        if kind == "row":
            ex_specs.append(pl.BlockSpec((1, tn), lambda i, j, k: (0, j)))
        else:
            ex_specs.append(pl.BlockSpec((tm, tn), lambda i, j, k: (i, j)))
    n_ex, n_out = len(extras), len(out_dtypes)

    cw = MATMUL_CHUNK if tn % MATMUL_CHUNK == 0 else tn

    n_g = len(gather_units)
    grid = (M // tm, N // tn, nk)

    def body(*refs):
        a_ref, b_ref = refs[0], refs[1]
        ex = refs[2:2 + n_ex]
        g_ins = refs[2 + n_ex:2 + n_ex + n_g]
        outs = refs[2 + n_ex + n_g:2 + n_ex + n_g + n_out]
        g_outs = refs[2 + n_ex + n_g + n_out:2 + n_ex + 2 * n_g + n_out]
        scratch = refs[2 + n_ex + 2 * n_g + n_out:]
        if n_g:
            step = [pl.program_id(d) for d in range(3)]
            at = lambda where: functools.reduce(jnp.logical_and, [s == w for s, w in zip(step, where)])
            pl.when(at((0, 0, 0)))(lambda: _gather_start(gather_units, g_ins, g_outs, scratch[-4:]))

        def for_chunks(use):
            for c in range(tn // cw):
                cols = slice(c * cw, (c + 1) * cw)
                b_chunk = b_ref[cols, :] if mode == "nt" else b_ref[:, cols]
                use(cols, lax.dot_general(a_ref[...], b_chunk, dims, preferred_element_type=F32))

        def finish(cols, acc):
            res = epi(acc, *[e[:, cols] for e in ex])
            for o, r in zip(outs, res):
                o[:, cols] = r.astype(o.dtype)

        if nk == 1:
            for_chunks(finish)
        else:
            acc_ref = scratch[0]
            k = pl.program_id(2)

            def first(cols, prod):
                acc_ref[:, cols] = prod

            def middle(cols, prod):
                acc_ref[:, cols] += prod

            pl.when(k == 0)(lambda: for_chunks(first))
            pl.when(jnp.logical_and(k > 0, k < nk - 1))(lambda: for_chunks(middle))
            pl.when(k == nk - 1)(lambda: for_chunks(lambda cols, prod: finish(cols, acc_ref[:, cols] + prod)))
        if n_g:
            pl.when(at([g - 1 for g in grid]))(lambda: _gather_finish(gather_units, g_ins, g_outs, scratch[-4:]))

    outs = pl.pallas_call(
        body,
        name=name,
        grid=grid,
        in_specs=[a_spec, b_spec] + ex_specs + [ANY] * n_g,
        out_specs=[o_spec for _ in out_dtypes] + [ANY] * n_g,
        out_shape=[jax.ShapeDtypeStruct(o_shape, dt) for dt in out_dtypes]
        + [jax.ShapeDtypeStruct((N_CHIPS,) + shape, BF16) for _, shape in gather_units],
        scratch_shapes=([pltpu.VMEM((tm, tn), F32)] if nk > 1 else []) + (_gather_sems(n_g) if n_g else []),
        compiler_params=_params(("arbitrary",) * 3 if n_g else ("parallel", "parallel", "arbitrary")),
    )(a, b, *[e for e, _ in extras], *gather_shards)
    return (outs[:n_out], outs[n_out:]) if n_g else outs


def _epi_plain(acc):
    return (acc,)


def _epi_bias(acc, bias):
    return (acc + bias,)


def _epi_sqrelu(acc):
    r = jnp.maximum(acc, 0.0)
    return (r * r,)


def _epi_dsqrelu(acc, act):
    a = act.astype(F32)
    return (acc * jnp.where(a > 0.0, 2.0 * a * lax.rsqrt(a), 0.0),)


def _resnorm_fwd(x, m, g_post, g_next, name, tr=512):
    T = x.shape[0]
    has_m = m is not None
    row = pl.BlockSpec((tr, D), lambda i: (i, 0))
    vec = pl.BlockSpec((1, D), lambda i: (0, 0))

    def body(*refs):
        if has_m:
            x_ref, m_ref, gp_ref, gn_ref, x1_ref, h_ref = refs
            x1 = x_ref[...] + _rms(m_ref[...], gp_ref[...])
            x1_ref[...] = x1
        else:
            x_ref, gn_ref, h_ref = refs
            x1 = x_ref[...]
        h_ref[...] = _rms(x1, gn_ref[...]).astype(BF16)

    if has_m:
        args, in_specs = (x, m, g_post, g_next), [row, row, vec, vec]
        out_specs = [row, row]
        out_shape = [jax.ShapeDtypeStruct((T, D), F32), jax.ShapeDtypeStruct((T, D), BF16)]
    else:
        args, in_specs = (x, g_next), [row, vec]
        out_specs = [row]
        out_shape = [jax.ShapeDtypeStruct((T, D), BF16)]
    return pl.pallas_call(
        body, name=name, grid=(T // tr,), in_specs=in_specs, out_specs=out_specs, out_shape=out_shape,
        compiler_params=_params(("parallel",)),
    )(*args)


def _resnorm_loss(x, m, g_post, target, name, tr=512):
    T = x.shape[0]
    row = pl.BlockSpec((tr, D), lambda i: (i, 0))
    vec = pl.BlockSpec((1, D), lambda i: (0, 0))

    def body(x_ref, m_ref, gp_ref, t_ref, dy_ref, loss_ref):
        diff = x_ref[...] + _rms(m_ref[...], gp_ref[...]) - t_ref[...]
        dy_ref[...] = diff * (1.0 / D)

        @pl.when(pl.program_id(0) == 0)
        def _():
            loss_ref[...] = jnp.zeros_like(loss_ref)

        per_token = jnp.mean(diff * diff, axis=-1, keepdims=True)
        loss_ref[...] += 0.5 * jnp.sum(per_token)

    return pl.pallas_call(
        body, name=name, grid=(T // tr,), in_specs=[row, row, vec, row],
        out_specs=[row, pl.BlockSpec((8, LANES), lambda i: (0, 0))],
        out_shape=[jax.ShapeDtypeStruct((T, D), F32), jax.ShapeDtypeStruct((8, LANES), F32)],
        compiler_params=_params(("arbitrary",)),
    )(x, m, g_post, target)


def _norm_bwd(x, g, dy, resid, out_dtype, name, tr=512):
    T = x.shape[0]
    has_r = resid is not None
    row = pl.BlockSpec((tr, D), lambda i: (i, 0))
    vec = pl.BlockSpec((1, D), lambda i: (0, 0))

    def body(*refs):
        if has_r:
            x_ref, g_ref, dy_ref, r_ref, dx_ref, dg_ref, cs_ref = refs
        else:
            x_ref, g_ref, dy_ref, dx_ref, dg_ref, cs_ref = refs
        xv, dyv = x_ref[...], dy_ref[...]
        r = lax.rsqrt(jnp.mean(xv * xv, axis=-1, keepdims=True) + RMS_EPS)
        gy = dyv * g_ref[...]
        c = jnp.sum(gy * xv, axis=-1, keepdims=True) * (1.0 / D)
        dx = r * gy - xv * (r * r * r * c)

        @pl.when(pl.program_id(0) == 0)
        def _():
            dg_ref[...] = jnp.zeros_like(dg_ref)
            cs_ref[...] = jnp.zeros_like(cs_ref)

        dg_ref[...] += jnp.sum(dyv * (xv * r), axis=0, keepdims=True)
        cs_ref[...] += jnp.sum(dx, axis=0, keepdims=True)
        if has_r:
            dx = dx + r_ref[...]
        dx_ref[...] = dx.astype(dx_ref.dtype)

    args = (x, g, dy) + ((resid,) if has_r else ())
    return pl.pallas_call(
        body, name=name, grid=(T // tr,), in_specs=[row, vec, row] + ([row] if has_r else []),
        out_specs=[row, vec, vec],
        out_shape=[jax.ShapeDtypeStruct((T, D), out_dtype), jax.ShapeDtypeStruct((1, D), F32),
                   jax.ShapeDtypeStruct((1, D), F32)],
        compiler_params=_params(("arbitrary",)),
    )(*args)


CONV_TB = 256
CONV_RC = 32


def _glu(a):
    return a[:, :D] * jax.nn.sigmoid(a[:, D:])


def _shifted_copies(win):
    rows = win.shape[1]
    for p in range(1, SUBLANES):
        win[p, 0:rows - SUBLANES, :] = win[0, p:p + rows - SUBLANES, :]


def _window(win, start):
    p = start % SUBLANES
    return win[p, start - p:start - p + CONV_RC, :]


def _conv_fwd(a, dw_w, dw_b, ln_g, ln_b, name, gather_units, gather_shards):
    T = a.shape[0]
    tb = min(CONV_TB, T)
    per = tb // HALO
    n_steps = T // tb
    n_g = len(gather_units)
    vec = pl.BlockSpec((1, D), lambda i: (0, 0))

    def body(*refs):
        cur_ref, prev_ref, w_ref, b_ref, lg_ref, lb_ref = refs[:6]
        g_ins = refs[6:6 + n_g]
        y_ref, z_ref = refs[6 + n_g:8 + n_g]
        g_outs = refs[8 + n_g:8 + 2 * n_g]
        uwin = refs[8 + 2 * n_g]
        sems = refs[9 + 2 * n_g:]
        i = pl.program_id(0)
        pl.when(i == 0)(lambda: _gather_start(gather_units, g_ins, g_outs, sems))
        uwin[0, 0:HALO, :] = jnp.where(i > 0, _glu(prev_ref[...]), 0.0)
        uwin[0, HALO:, :] = _glu(cur_ref[...])
        _shifted_copies(uwin)
        for r in range(tb // CONV_RC):
            acc = jnp.broadcast_to(b_ref[...], (CONV_RC, D))
            for k in range(CONV_W):
                start = r * CONV_RC + HALO - (CONV_W - 1) + k
                acc = acc + _window(uwin, start) * w_ref[k:k + 1, :]
            rows = slice(r * CONV_RC, (r + 1) * CONV_RC)
            y_ref[rows, :] = acc
            mu = jnp.mean(acc, axis=-1, keepdims=True)
            xc = acc - mu
            var = jnp.mean(xc * xc, axis=-1, keepdims=True)
            n = xc * lax.rsqrt(var + LN_EPS) * lg_ref[...] + lb_ref[...]
            z_ref[rows, :] = (n * jax.nn.sigmoid(n)).astype(BF16)
        pl.when(i == n_steps - 1)(lambda: _gather_finish(gather_units, g_ins, g_outs, sems))

    outs = pl.pallas_call(
        body, name=name, grid=(n_steps,),
        in_specs=[pl.BlockSpec((tb, 2 * D), lambda i: (i, 0)),
                  pl.BlockSpec((HALO, 2 * D), lambda i: (jnp.maximum(i * per - 1, 0), 0)),
                  pl.BlockSpec((HALO, D), lambda i: (0, 0)), vec, vec, vec] + [ANY] * n_g,
        out_specs=[pl.BlockSpec((tb, D), lambda i: (i, 0)), pl.BlockSpec((tb, D), lambda i: (i, 0))] + [ANY] * n_g,
        out_shape=[jax.ShapeDtypeStruct((T, D), F32), jax.ShapeDtypeStruct((T, D), BF16)]
        + [jax.ShapeDtypeStruct((N_CHIPS,) + shape, BF16) for _, shape in gather_units],
        scratch_shapes=[pltpu.VMEM((SUBLANES, tb + HALO, D), F32)] + _gather_sems(n_g),
        compiler_params=_params(("arbitrary",)),
    )(a, a, dw_w, dw_b, ln_g, ln_b, *gather_shards)
    return outs[0], outs[1], outs[2:]


def _conv_bwd_ln(dz, y, ln_g, ln_b, name, exchange_units, slabs, tr=256):
    T = y.shape[0]
    tr = min(tr, T)
    n_steps = T // tr
    n_s = len(slabs)
    row = pl.BlockSpec((tr, D), lambda i: (i, 0))
    vec = pl.BlockSpec((1, D), lambda i: (0, 0))

    def body(*refs):
        dz_ref, y_ref, lg_ref, lb_ref = refs[:4]
        s_ins = refs[4:4 + n_s]
        dy_ref, dlg_ref, dlb_ref, db_ref = refs[4 + n_s:8 + n_s]
        s_outs = refs[8 + n_s:8 + 2 * n_s]
        sems = refs[8 + 2 * n_s:]
        pl.when(pl.program_id(0) == 0)(lambda: _sibling_start(exchange_units, s_ins, s_outs, sems))
        yv = y_ref[...]
        mu = jnp.mean(yv, axis=-1, keepdims=True)
        xc = yv - mu
        rstd = lax.rsqrt(jnp.mean(xc * xc, axis=-1, keepdims=True) + LN_EPS)
        yh = xc * rstd
        n = yh * lg_ref[...] + lb_ref[...]
        sg = jax.nn.sigmoid(n)
        dn = dz_ref[...] * (sg * (1.0 + n * (1.0 - sg)))
        dyh = dn * lg_ref[...]
        dyv = rstd * (dyh - jnp.mean(dyh, axis=-1, keepdims=True) - yh * jnp.mean(dyh * yh, axis=-1, keepdims=True))
        dy_ref[...] = dyv

        @pl.when(pl.program_id(0) == 0)
        def _():
            dlg_ref[...] = jnp.zeros_like(dlg_ref)
            dlb_ref[...] = jnp.zeros_like(dlb_ref)
            db_ref[...] = jnp.zeros_like(db_ref)

        dlg_ref[...] += jnp.sum(dn * yh, axis=0, keepdims=True)
        dlb_ref[...] += jnp.sum(dn, axis=0, keepdims=True)
        db_ref[...] += jnp.sum(dyv, axis=0, keepdims=True)
        pl.when(pl.program_id(0) == n_steps - 1)(lambda: _sibling_finish(exchange_units, s_ins, s_outs, sems))

    outs = pl.pallas_call(
        body, name=name, grid=(n_steps,), in_specs=[row, row, vec, vec] + [ANY] * n_s,
        out_specs=[row, vec, vec, vec] + [ANY] * n_s,
        out_shape=[jax.ShapeDtypeStruct((T, D), F32)] + [jax.ShapeDtypeStruct((1, D), F32)] * 3
        + [jax.ShapeDtypeStruct((N_CHIPS, rows // 2, cols), F32) for _, (rows, cols), _ in exchange_units],
        scratch_shapes=[pltpu.SemaphoreType.DMA((n_s,)), pltpu.SemaphoreType.DMA((n_s,))],
        compiler_params=_params(("arbitrary",)),
    )(dz, y, ln_g, ln_b, *slabs)
    return outs[0], outs[1], outs[2], outs[3], outs[4:]


def _conv_bwd_dw(dy, a, dw_w, name, exchange):
    T = a.shape[0]
    tb = min(CONV_TB, T)
    per = tb // HALO
    last_halo = T // HALO - 1
    n_steps = T // tb
    n_x = len(exchange)

    def body(*refs):
        dyc_ref, dyn_ref, cur_ref, prev_ref, w_ref = refs[:5]
        x_ins = refs[5:5 + n_x]
        da_ref, dbias_ref, dw_ref = refs[5 + n_x:8 + n_x]
        x_outs = refs[8 + n_x:8 + 2 * n_x]
        uwin, dywin, dwacc = refs[8 + 2 * n_x:11 + 2 * n_x]
        sems = refs[11 + 2 * n_x:]
        i = pl.program_id(0)

        @pl.when(i == 0)
        def _():
            _exchange_start(x_ins, x_outs, sems)
            dbias_ref[...] = jnp.zeros_like(dbias_ref)
            dwacc[...] = jnp.zeros_like(dwacc)

        uwin[0, 0:HALO, :] = jnp.where(i > 0, _glu(prev_ref[...]), 0.0)
        uwin[0, HALO:, :] = _glu(cur_ref[...])
        dywin[0, 0:tb, :] = dyc_ref[...]
        dywin[0, tb:, :] = jnp.where(i < n_steps - 1, dyn_ref[...], 0.0)
        _shifted_copies(uwin)
        _shifted_copies(dywin)
        for r in range(tb // CONV_RC):
            rows = slice(r * CONV_RC, (r + 1) * CONV_RC)
            dy_c = dywin[0, rows, :]
            du = jnp.zeros((CONV_RC, D), F32)
            for k in range(CONV_W):
                du = du + _window(dywin, r * CONV_RC + (CONV_W - 1) - k) * w_ref[k:k + 1, :]
                prod = dy_c * _window(uwin, r * CONV_RC + HALO - (CONV_W - 1) + k)
                part = prod[0:8, :]
                for q in range(1, CONV_RC // 8):
                    part = part + prod[8 * q:8 * q + 8, :]
                dwacc[8 * k:8 * k + 8, :] += part
            av = cur_ref[rows, :]
            a1, sg = av[:, :D], jax.nn.sigmoid(av[:, D:])
            da1 = du * sg
            da2 = du * a1 * (sg * (1.0 - sg))
            da_ref[rows, 0:D] = da1.astype(BF16)
            da_ref[rows, D:] = da2.astype(BF16)
            dbias_ref[:, 0:D] += jnp.sum(da1, axis=0, keepdims=True)
            dbias_ref[:, D:] += jnp.sum(da2, axis=0, keepdims=True)

        @pl.when(i == n_steps - 1)
        def _():
            for k in range(CONV_W):
                dw_ref[k:k + 1, :] = jnp.sum(dwacc[8 * k:8 * k + 8, :], axis=0, keepdims=True)
            dw_ref[CONV_W:, :] = jnp.zeros((HALO - CONV_W, D), F32)
            _exchange_finish(x_ins, x_outs, sems)

    outs = pl.pallas_call(
        body, name=name, grid=(n_steps,),
        in_specs=[pl.BlockSpec((tb, D), lambda i: (i, 0)),
                  pl.BlockSpec((HALO, D), lambda i: (jnp.minimum((i + 1) * per, last_halo), 0)),
                  pl.BlockSpec((tb, 2 * D), lambda i: (i, 0)),
                  pl.BlockSpec((HALO, 2 * D), lambda i: (jnp.maximum(i * per - 1, 0), 0)),
                  pl.BlockSpec((HALO, D), lambda i: (0, 0))] + [ANY] * n_x,
        out_specs=[pl.BlockSpec((tb, 2 * D), lambda i: (i, 0)), pl.BlockSpec((1, 2 * D), lambda i: (0, 0)),
                   pl.BlockSpec((HALO, D), lambda i: (0, 0))] + [ANY] * n_x,
        out_shape=[jax.ShapeDtypeStruct((T, 2 * D), BF16), jax.ShapeDtypeStruct((1, 2 * D), F32),
                   jax.ShapeDtypeStruct((HALO, D), F32)] + [jax.ShapeDtypeStruct(p.shape, p.dtype) for p in exchange],
        scratch_shapes=[pltpu.VMEM((SUBLANES, tb + HALO, D), F32), pltpu.VMEM((SUBLANES, tb + HALO, D), F32),
                        pltpu.VMEM((8 * HALO, D), F32)] + _exchange_sems(n_x),
        compiler_params=_params(("arbitrary",)),
    )(dy, dy, a, a, dw_w, *exchange)
    return outs[0], outs[1], outs[2], outs[3:]


GATE_TB = 512


def _gate_fwd(fl, b_f, name):
    T = fl.shape[0]
    tb = min(GATE_TB, T)
    row = pl.BlockSpec((tb, LANES), lambda i: (i, 0))

    def body(fl_ref, b_ref, f_ref, f1_ref, f2_ref, f3_ref, carry):
        @pl.when(pl.program_id(0) == 0)
        def _():
            carry[...] = jnp.zeros_like(carry)

        z = fl_ref[...] + b_ref[...]
        lf = jnp.minimum(z, 0.0) - jnp.log(1.0 + jnp.exp(-jnp.abs(z)))
        tri = (lax.broadcasted_iota(jnp.int32, (tb, tb), 0) >= lax.broadcasted_iota(jnp.int32, (tb, tb), 1)).astype(F32)
        f = jnp.dot(tri, lf, precision=lax.Precision.HIGHEST, preferred_element_type=F32) + carry[...]
        carry[...] = f[tb - 1:tb, :]
        f_ref[...] = f
        f1_ref[...], f2_ref[...], f3_ref[...] = _split3(f)

    return pl.pallas_call(
        body, name=name, grid=(T // tb,), in_specs=[row, pl.BlockSpec((1, LANES), lambda i: (0, 0))],
        out_specs=[row] * 4, out_shape=[jax.ShapeDtypeStruct((T, LANES), F32)] * 4,
        scratch_shapes=[pltpu.VMEM((1, LANES), F32)],
        compiler_params=_params(("arbitrary",)),
    )(fl, b_f)


def _gate_bwd(dF_q, dF_k, fl, b_f, name):
    T = fl.shape[0]
    tb = min(GATE_TB, T)
    nb = T // tb
    row = pl.BlockSpec((tb, LANES), lambda i: (nb - 1 - i, 0))
    vec = pl.BlockSpec((1, LANES), lambda i: (0, 0))

    def body(dfq_ref, dfk_ref, fl_ref, b_ref, dfl_ref, db_ref, carry):
        @pl.when(pl.program_id(0) == 0)
        def _():
            carry[...] = jnp.zeros_like(carry)
            db_ref[...] = jnp.zeros_like(db_ref)

        tri = (lax.broadcasted_iota(jnp.int32, (tb, tb), 0) <= lax.broadcasted_iota(jnp.int32, (tb, tb), 1)).astype(F32)
        rc = jnp.dot(tri, dfq_ref[...] - dfk_ref[...], precision=lax.Precision.HIGHEST,
                     preferred_element_type=F32) + carry[...]
        carry[...] = rc[0:1, :]
        dfl = rc * jax.nn.sigmoid(-(fl_ref[...] + b_ref[...]))
        dfl_ref[...] = dfl
        db_ref[...] += jnp.sum(dfl, axis=0, keepdims=True)

    return pl.pallas_call(
        body, name=name, grid=(nb,), in_specs=[row, row, row, vec], out_specs=[row, vec],
        out_shape=[jax.ShapeDtypeStruct((T, LANES), F32), jax.ShapeDtypeStruct((1, LANES), F32)],
        scratch_shapes=[pltpu.VMEM((1, LANES), F32)],
        compiler_params=_params(("arbitrary",)),
    )(dF_q, dF_k, fl, b_f)


ATT_TB = 512
LAYOUT_TB = 2048
NT = (((1,), (1,)), ((), ()))
NN = (((1,), (0,)), ((), ()))


def _lane_insert(base, parts, first_lane):
    lane = lax.broadcasted_iota(jnp.int32, base.shape, 1)
    out = base
    for n, p in enumerate(parts):
        out = jnp.where(lane == first_lane + n, p, out)
    return out


def _causal(tb):
    return lax.broadcasted_iota(jnp.int32, (tb, tb), 0) >= lax.broadcasted_iota(jnp.int32, (tb, tb), 1)


def _seen_by(n_keys, n_queries):
    return (lax.broadcasted_iota(jnp.int32, (n_keys, n_queries), 1)
            >= lax.broadcasted_iota(jnp.int32, (n_keys, n_queries), 0))


def _other_head(x):
    return pltpu.roll(x, HEAD_DIM, 1)


def _attn_pack(proj, f1, f2, f3, name, gather_units, gather_shards):
    T = proj.shape[0]
    tb = min(LAYOUT_TB, T)
    pairs = N_HEADS // 2
    n_steps = T // tb
    n_g = len(gather_units)
    cols = lambda first: pl.BlockSpec((tb, LANES), lambda p, i: (i, first + p))
    gate = pl.BlockSpec((tb, LANES), lambda p, i: (i, 0))
    heads = pl.BlockSpec((2, tb, LANES), lambda p, i: (p, i, 0))
    norm = pl.BlockSpec((2, 8, LANES), lambda p, i: (p, 0, 0))

    def body(*refs):
        q_ref, k_ref, v_ref, f1_ref, f2_ref, f3_ref = refs[:6]
        g_ins = refs[6:6 + n_g]
        qa_ref, ka_ref, va_ref, qn_ref, kn_ref = refs[6 + n_g:11 + n_g]
        g_outs = refs[11 + n_g:11 + 2 * n_g]
        sems = refs[11 + 2 * n_g:]
        p, i = pl.program_id(0), pl.program_id(1)
        if n_g:
            pl.when(jnp.logical_and(p == 0, i == 0))(lambda: _gather_start(gather_units, g_ins, g_outs, sems))
        lane = lax.broadcasted_iota(jnp.int32, (tb, LANES), 1)
        data = lane < HEAD_DIM

        @pl.when(i == 0)
        def _():
            qn_ref[...] = jnp.zeros_like(qn_ref)
            kn_ref[...] = jnp.zeros_like(kn_ref)

        qv, kv, vv = q_ref[...].astype(F32) * 0.125, k_ref[...].astype(F32), v_ref[...].astype(F32)
        for e in range(2):
            fcol = [jnp.sum(jnp.where(lane == 2 * p + e, f[...], 0.0), axis=-1, keepdims=True)
                    for f in (f1_ref, f2_ref, f3_ref)]
            pick = (lambda t: t) if e == 0 else _other_head
            qd = jnp.where(data, pick(qv), 0.0).astype(BF16).astype(F32)
            kd = jnp.where(data, pick(kv), 0.0).astype(BF16).astype(F32)
            vd = jnp.where(data, pick(vv), 0.0)
            qa_ref[e] = _lane_insert(qd, fcol + [1.0, 1.0, 1.0], AUG_F).astype(BF16)
            ka_ref[e] = _lane_insert(kd, [1.0, 1.0, 1.0] + [-f for f in fcol] + [1.0, 1.0, 1.0], AUG_F).astype(BF16)
            va_ref[e] = _lane_insert(vd, [-1.0, -1.0, -1.0, 1.0], AUG_F).astype(BF16)
            qn_ref[e] = jnp.maximum(qn_ref[e], jnp.max(jnp.sum(qd * qd, axis=-1, keepdims=True)))
            kn_ref[e] = jnp.maximum(kn_ref[e], jnp.max(jnp.sum(kd * kd, axis=-1, keepdims=True)))
        if n_g:
            pl.when(jnp.logical_and(p == pairs - 1, i == n_steps - 1))(
                lambda: _gather_finish(gather_units, g_ins, g_outs, sems))

    aug = jax.ShapeDtypeStruct((N_HEADS, T, LANES), BF16)
    nrm = jax.ShapeDtypeStruct((N_HEADS, 8, LANES), F32)
    outs = pl.pallas_call(
        body, name=name, grid=(pairs, n_steps),
        in_specs=[cols(0), cols(pairs), cols(2 * pairs), gate, gate, gate] + [ANY] * n_g,
        out_specs=[heads, heads, heads, norm, norm] + [ANY] * n_g,
        out_shape=[aug, aug, aug, nrm, nrm]
        + [jax.ShapeDtypeStruct((N_CHIPS,) + shape, BF16) for _, shape in gather_units],
        scratch_shapes=_gather_sems(n_g) if n_g else [],
        compiler_params=_params(("arbitrary", "arbitrary") if n_g else ("parallel", "arbitrary")),
    )(proj, proj, proj, f1, f2, f3, *gather_shards)
    return outs[:5], outs[5:]


def _merge_heads(x_aug, scale, out_dtype, name, column=None):
    H, T, _ = x_aug.shape
    tb = min(LAYOUT_TB, T)

    def body(x_ref, o_ref, *col_ref):
        p = pl.program_id(1)
        lane = lax.broadcasted_iota(jnp.int32, (tb, LANES), 1)
        x0, x1 = x_ref[0], x_ref[1]
        o_ref[...] = (jnp.where(lane < HEAD_DIM, x0, _other_head(x1)) * scale).astype(out_dtype)
        if column is not None:
            @pl.when(p == 0)
            def _():
                col_ref[0][...] = jnp.zeros_like(col_ref[0])

            c0 = jnp.sum(jnp.where(lane == column, x0, 0.0), axis=-1, keepdims=True)
            c1 = jnp.sum(jnp.where(lane == column, x1, 0.0), axis=-1, keepdims=True)
            col_ref[0][...] += jnp.where(lane == 2 * p, c0, 0.0) + jnp.where(lane == 2 * p + 1, c1, 0.0)

    out_specs = [pl.BlockSpec((tb, LANES), lambda i, p: (i, p))]
    out_shape = [jax.ShapeDtypeStruct((T, D), out_dtype)]
    if column is not None:
        out_specs.append(pl.BlockSpec((tb, LANES), lambda i, p: (i, 0)))
        out_shape.append(jax.ShapeDtypeStruct((T, LANES), F32))
    return pl.pallas_call(
        body, name=name, grid=(T // tb, H // 2),
        in_specs=[pl.BlockSpec((2, tb, LANES), lambda i, p: (p, i, 0))],
        out_specs=out_specs, out_shape=out_shape,
        compiler_params=_params(("parallel", "arbitrary")),
    )(x_aug)


def _attn_prep(d_o, o_aug, name):
    H, T, _ = o_aug.shape
    tb = min(LAYOUT_TB, T)
    heads = pl.BlockSpec((2, tb, LANES), lambda p, i: (p, i, 0))

    def body(do_ref, o_ref, out_ref):
        lane = lax.broadcasted_iota(jnp.int32, (tb, LANES), 1)
        dov = do_ref[...]
        for e in range(2):
            d_e = jnp.where(lane < HEAD_DIM, dov if e == 0 else _other_head(dov), 0.0)
            delta = jnp.sum(d_e * o_ref[e], axis=-1, keepdims=True)
            out_ref[e] = _lane_insert(d_e, _split3(delta), AUG_F).astype(BF16)

    return pl.pallas_call(
        body, name=name, grid=(H // 2, T // tb),
        in_specs=[pl.BlockSpec((tb, LANES), lambda p, i: (i, p)), heads], out_specs=heads,
        out_shape=jax.ShapeDtypeStruct((H, T, LANES), BF16),
        compiler_params=_params(("parallel", "parallel")),
    )(d_o, o_aug)


def _flash_spec(n_in, n_out, H, nb, tb, T, resident, scratch, n_tables=3):
    blk = pl.BlockSpec((2, tb, LANES), lambda h, i, *_: (h, i, 0))
    whole = pl.BlockSpec((2, T, LANES), lambda h, i, *_: (h, 0, 0))
    return pltpu.PrefetchScalarGridSpec(
        num_scalar_prefetch=n_tables, grid=(H // 2, nb),
        in_specs=[whole if resident[n] else blk for n in range(n_in)],
        out_specs=[blk] * n_out, scratch_shapes=scratch)


def _first_live(qk_ref, fs_ref, fe_ref, h, i):
    top = qk_ref[h] + fs_ref[h, i]
    return lax.fori_loop(0, i, lambda j, n: n + jnp.where(top - fe_ref[h, j] < SKIP_BELOW, 1, 0), 0)


def _flash_fwd(qk, fs, fe, kmax, q_aug, k_aug, v_aug, name):
    H, T, _ = q_aug.shape
    tb = min(ATT_TB, T)

    def body(qk_ref, fs_ref, fe_ref, km_ref, q_ref, k_ref, v_ref, o_ref, q2_ref, m_s, acc_s, qm_s):
        h, i = 2 * pl.program_id(0), pl.program_id(1)
        acc_s[...] = jnp.zeros_like(acc_s)
        first = jnp.minimum(_first_live(qk_ref, fs_ref, fe_ref, h, i), _first_live(qk_ref, fs_ref, fe_ref, h + 1, i))
        bounded = jnp.logical_and(qk_ref[h] < FIXED_MAX_BELOW, qk_ref[h + 1] < FIXED_MAX_BELOW)

        def sweep(step):
            def loop_body(kb, carry):
                step(kb, False)
                return carry

            lax.fori_loop(first, i, loop_body, 0)
            step(i, True)

        @pl.when(bounded)
        def _():
            lane = lax.broadcasted_iota(jnp.int32, (tb, LANES), 1)
            for e in range(2):
                qf = q_ref[e].astype(F32)
                norm = jnp.sqrt(jnp.sum(jnp.where(lane < HEAD_DIM, qf * qf, 0.0), axis=-1, keepdims=True))
                bound = norm * (1.01 * km_ref[h + e]) + 1e-3
                m_s[e] = bound
                qm_s[e] = _lane_insert(qf, [-p for p in _split3(bound)], AUG_L).astype(BF16)

            def step(kb, masked):
                rows = pl.ds(pl.multiple_of(kb * tb, tb), tb)
                for e in range(2):
                    p = jnp.exp(lax.dot_general(qm_s[e], k_ref[e, rows, :], NT, preferred_element_type=F32))
                    if masked:
                        p = jnp.where(_causal(tb), p, 0.0)
                    acc_s[e] += lax.dot_general(p.astype(BF16), v_ref[e, rows, :], NN, preferred_element_type=F32)

            sweep(step)

        @pl.when(jnp.logical_not(bounded))
        def _():
            m_s[...] = jnp.full(m_s.shape, -jnp.inf, F32)

            def step(kb, masked):
                rows = pl.ds(pl.multiple_of(kb * tb, tb), tb)
                for e in range(2):
                    s = lax.dot_general(q_ref[e], k_ref[e, rows, :], NT, preferred_element_type=F32)
                    if masked:
                        s = jnp.where(_causal(tb), s, MASK_VALUE)
                    m_old = m_s[e]
                    m_new = jnp.maximum(m_old, jnp.max(s, axis=-1, keepdims=True))
                    p = jnp.exp(s - m_new)
                    acc_s[e] = jnp.exp(m_old - m_new) * acc_s[e] + lax.dot_general(
                        p.astype(BF16), v_ref[e, rows, :], NN, preferred_element_type=F32)
                    m_s[e] = m_new

            sweep(step)

        for e in range(2):
            acc = acc_s[e]
            lane = lax.broadcasted_iota(jnp.int32, acc.shape, 1)
            l = jnp.sum(jnp.where(lane == AUG_ONE, acc, 0.0), axis=-1, keepdims=True)
            o_ref[e] = acc / l
            lse = m_s[e] + jnp.log(l)
            q2_ref[e] = _lane_insert(q_ref[e].astype(F32), [-p for p in _split3(lse)], AUG_L).astype(BF16)

    return pl.pallas_call(
        body, name=name,
        grid_spec=_flash_spec(3, 2, H, T // tb, tb, T, (False, True, True),
                              [pltpu.VMEM((2, tb, 1), F32), pltpu.VMEM((2, tb, LANES), F32),
                               pltpu.VMEM((2, tb, LANES), BF16)], n_tables=4),
        out_shape=[jax.ShapeDtypeStruct((H, T, LANES), F32), jax.ShapeDtypeStruct((H, T, LANES), BF16)],
        compiler_params=_params(("parallel", "arbitrary")),
    )(qk, fs, fe, kmax, q_aug, k_aug, v_aug)


def _flash_dq(qk, fs, fe, q2, k_aug, do_aug, v_aug, name):
    H, T, _ = q2.shape
    tb = min(ATT_TB, T)

    def body(qk_ref, fs_ref, fe_ref, q_ref, k_ref, do_ref, v_ref, dq_ref, acc_s):
        h, i = 2 * pl.program_id(0), pl.program_id(1)
        acc_s[...] = jnp.zeros_like(acc_s)

        def step(kb, masked):
            rows = pl.ds(pl.multiple_of(kb * tb, tb), tb)
            for e in range(2):
                k = k_ref[e, rows, :]
                p = jnp.exp(lax.dot_general(q_ref[e], k, NT, preferred_element_type=F32))
                if masked:
                    p = jnp.where(_causal(tb), p, 0.0)
                ds = p * lax.dot_general(do_ref[e], v_ref[e, rows, :], NT, preferred_element_type=F32)
                acc_s[e] += lax.dot_general(ds.astype(BF16), k, NN, preferred_element_type=F32)

        def loop_body(kb, carry):
            step(kb, False)
            return carry

        first = jnp.minimum(_first_live(qk_ref, fs_ref, fe_ref, h, i), _first_live(qk_ref, fs_ref, fe_ref, h + 1, i))
        lax.fori_loop(first, i, loop_body, 0)
        step(i, True)
        dq_ref[...] = acc_s[...]

    return pl.pallas_call(
        body, name=name,
        grid_spec=_flash_spec(4, 1, H, T // tb, tb, T, (False, True, False, True), [pltpu.VMEM((2, tb, LANES), F32)]),
        out_shape=[jax.ShapeDtypeStruct((H, T, LANES), F32)],
        compiler_params=_params(("parallel", "arbitrary")),
    )(qk, fs, fe, q2, k_aug, do_aug, v_aug)[0]


def _flash_dkv(qk, fs, fe, q2, k_aug, do_aug, v_aug, name):
    H, T, _ = q2.shape
    tb = min(ATT_TB, T)
    half = tb // 2
    nb = T // tb

    def body(qk_ref, fs_ref, fe_ref, q_ref, k_ref, do_ref, v_ref, dk_ref, dv_ref, dk_s, dv_s):
        h, i = 2 * pl.program_id(0), pl.program_id(1)
        dk_s[...] = jnp.zeros_like(dk_s)
        dv_s[...] = jnp.zeros_like(dv_s)

        def step(qb, masked):
            base = pl.multiple_of(qb * tb, tb)
            parts = [(slice(0, half), 0, tb), (slice(half, tb), half, half)] if masked else [(slice(0, tb), 0, tb)]
            for e in range(2):
                for krows, q0, nq in parts:
                    queries = pl.ds(pl.multiple_of(base + q0, half), nq)
                    q, dov = q_ref[e, queries, :], do_ref[e, queries, :]
                    pt = jnp.exp(lax.dot_general(k_ref[e, krows, :], q, NT, preferred_element_type=F32))
                    if masked:
                        pt = jnp.where(_seen_by(krows.stop - krows.start, nq), pt, 0.0)
                    dst = pt * lax.dot_general(v_ref[e, krows, :], dov, NT, preferred_element_type=F32)
                    dv_s[e, krows, :] += lax.dot_general(pt.astype(BF16), dov, NN, preferred_element_type=F32)
                    dk_s[e, krows, :] += lax.dot_general(dst.astype(BF16), q, NN, preferred_element_type=F32)

        step(i, True)

        def loop_body(qb, carry):
            step(qb, False)
            return carry

        def live_after(head):
            base = qk_ref[head] - fe_ref[head, i]
            return lax.fori_loop(i + 1, nb, lambda b, n: n + jnp.where(base + fs_ref[head, b] < SKIP_BELOW, 0, 1), 0)

        lax.fori_loop(i + 1, i + 1 + jnp.maximum(live_after(h), live_after(h + 1)), loop_body, 0)
        dk_ref[...] = dk_s[...]
        dv_ref[...] = dv_s[...]

    return pl.pallas_call(
        body, name=name,
        grid_spec=_flash_spec(4, 2, H, nb, tb, T, (True, False, True, False),
                              [pltpu.VMEM((2, tb, LANES), F32), pltpu.VMEM((2, tb, LANES), F32)]),
        out_shape=[jax.ShapeDtypeStruct((H, T, LANES), F32)] * 2,
        compiler_params=_params(("parallel", "arbitrary")),
    )(qk, fs, fe, q2, k_aug, do_aug, v_aug)


def _add_pair(place, slab, got, name):
    _, half, cols = got.shape
    grid_spec = pltpu.PrefetchScalarGridSpec(
        num_scalar_prefetch=1, grid=(N_CHIPS,),
        in_specs=[pl.BlockSpec((None, half, cols), lambda j, s: (j, s[4], 0)),
                  pl.BlockSpec((None, half, cols), lambda j, s: (j, 0, 0))],
        out_specs=pl.BlockSpec((None, half, cols), lambda j, s: (j, 0, 0)))

    def body(s_ref, a_ref, b_ref, o_ref):
        o_ref[...] = (a_ref[...] + b_ref[...]).astype(BF16)

    return pl.pallas_call(
        body, name=name, grid_spec=grid_spec, out_shape=jax.ShapeDtypeStruct(got.shape, BF16),
        compiler_params=_params(("parallel",)),
    )(place, slab, got)


def _add_chips(place, pair, by_chip, name):
    _, half, cols = pair.shape
    tb = min(256, half)
    steps = half // tb
    slot = lambda n: pl.BlockSpec((None, tb, cols), lambda i, s: (s[n], i, 0))
    grid_spec = pltpu.PrefetchScalarGridSpec(
        num_scalar_prefetch=1, grid=(steps,), in_specs=[slot(0), slot(1), slot(2), slot(3)],
        out_specs=pl.BlockSpec((tb, cols), lambda i, s: (s[4] * steps + i, 0)))

    def body(s_ref, own_ref, b1_ref, b2_ref, b3_ref, o_ref):
        o_ref[...] = ((own_ref[...].astype(F32) + b1_ref[...].astype(F32)) + b2_ref[...].astype(F32)) \
            + b3_ref[...].astype(F32)

    return pl.pallas_call(
        body, name=name, grid_spec=grid_spec, out_shape=jax.ShapeDtypeStruct((2 * half, cols), F32),
        compiler_params=_params(("parallel",)),
    )(place, pair, by_chip, by_chip, by_chip)


def _add_slots(b, name, tb=128):
    n, rows, cols = b.shape
    tb = tb if rows % tb == 0 else rows
    specs = [pl.BlockSpec((None, tb, cols), functools.partial(lambda j, i: (j, i, 0), j)) for j in range(n)]

    def body(*refs):
        acc = refs[0][...]
        for r in refs[1:n]:
            acc = acc + r[...]
        refs[n][...] = acc

    return pl.pallas_call(
        body, name=name, grid=(rows // tb,), in_specs=specs, out_specs=pl.BlockSpec((tb, cols), lambda i: (i, 0)),
        out_shape=jax.ShapeDtypeStruct((rows, cols), F32), compiler_params=_params(("parallel",)),
    )(*([b] * n))


def _adamw(w, g, m, v, name):
    rows, cols = w.shape
    tr = 256 if rows % 256 == 0 else rows
    blk = pl.BlockSpec((tr, cols), lambda i: (i, 0))

    def body(w_ref, g_ref, m_ref, v_ref, d_ref, nm_ref, nv_ref):
        gv = g_ref[...]
        nm = ADAM_B1 * m_ref[...] + (1.0 - ADAM_B1) * gv
        nv = ADAM_B2 * v_ref[...] + (1.0 - ADAM_B2) * (gv * gv)
        m_hat = nm / (1.0 - ADAM_B1 ** ADAM_STEP)
        v_hat = nv / (1.0 - ADAM_B2 ** ADAM_STEP)
        d_ref[...] = -ADAM_LR * (m_hat / (jnp.sqrt(v_hat) + ADAM_EPS) + ADAM_WD * w_ref[...])
        nm_ref[...] = nm
        nv_ref[...] = nv

    return pl.pallas_call(
        body, name=name, grid=(rows // tr,), in_specs=[blk] * 4, out_specs=[blk] * 3,
        out_shape=[jax.ShapeDtypeStruct((rows, cols), F32)] * 3, compiler_params=_params(("parallel",)),
    )(w, g, m, v)


ANY = pl.BlockSpec(memory_space=pl.ANY)


def _place():
    x, y, c = lax.axis_index("x"), lax.axis_index("y"), lax.axis_index("c")
    others = [(1 - x, y), (x, 1 - y), (1 - x, 1 - y)]
    return x, y, c, 2 * x + y, others


def _half(ref, rows, h):
    return ref.at[pl.ds(h * (rows // 2), rows // 2), :]


def _gather_sems(n_u):
    return [pltpu.SemaphoreType.DMA((6 * n_u,)), pltpu.SemaphoreType.DMA((6 * n_u,)),
            pltpu.SemaphoreType.DMA((n_u,)), pltpu.SemaphoreType.DMA((n_u,))]


def _gather_copies(units, ins, outs, sems):
    send_sems, recv_sems, own_send_sems, own_recv_sems = sems
    x, y, c, me, others = _place()
    sibling = (x, y, 1 - c)

    def copy(u, k, chip, h, to, src=None):
        dst = _half(outs[u].at[chip], units[u][1][0], h)
        return pltpu.make_async_remote_copy(
            src_ref=dst if src is None else src, dst_ref=dst, send_sem=send_sems.at[6 * u + k],
            recv_sem=recv_sems.at[6 * u + k], device_id=to, device_id_type=MESH)

    pairs = [(u, j, 2 * ox + oy, (ox, oy)) for u in range(len(units)) for j, (ox, oy) in enumerate(others)]
    own = [pltpu.make_async_remote_copy(
        src_ref=ins[u], dst_ref=outs[u].at[me], send_sem=own_send_sems.at[u], recv_sem=own_recv_sems.at[u],
        device_id=sibling, device_id_type=MESH) for u in range(len(units))]
    first = [copy(u, j, me, c, (*to, c), src=_half(ins[u], units[u][1][0], c)) for u, j, _, to in pairs]
    arrivals = [copy(u, j, chip, c, (x, y, c)) for u, j, chip, _ in pairs]
    passed = [copy(u, 3 + j, chip, c, sibling) for u, j, chip, _ in pairs]
    from_sibling = [copy(u, 3 + j, chip, 1 - c, (x, y, c)) for u, j, chip, _ in pairs]
    return own, first, arrivals, passed, from_sibling


def _gather_start(units, ins, outs, sems):
    own, first, _, _, _ = _gather_copies(units, ins, outs, sems)
    for cp in own + first:
        cp.start()


def _gather_finish(units, ins, outs, sems):
    own, first, arrivals, passed, from_sibling = _gather_copies(units, ins, outs, sems)
    for arrived, onward in zip(arrivals, passed):
        arrived.wait_recv()
        onward.start()
    for cp in from_sibling:
        cp.wait_recv()
    for cp in first + passed:
        cp.wait_send()
    for cp in own:
        cp.wait()


def _allgather_weights(units, shards, name):
    n_u = len(units)

    def body(*refs):
        ins, outs, sems = refs[:n_u], refs[n_u:2 * n_u], refs[2 * n_u:]
        _gather_start(units, ins, outs, sems)
        _gather_finish(units, ins, outs, sems)

    return pl.pallas_call(
        body, name=name, in_specs=[ANY] * n_u, out_specs=[ANY] * n_u,
        out_shape=[jax.ShapeDtypeStruct((N_CHIPS,) + shape, BF16) for _, shape in units],
        scratch_shapes=_gather_sems(n_u),
    )(*shards)


def _sibling_copies(units, ins, got, sems):
    send_sems, recv_sems = sems
    x, y, c, _, _ = _place()
    copies = []
    for u, (_, (rows, _), _) in enumerate(units):
        half = rows // 2
        copies.append(pltpu.make_async_remote_copy(
            src_ref=ins[u].at[:, pl.ds((1 - c) * half, half), :], dst_ref=got[u], send_sem=send_sems.at[u],
            recv_sem=recv_sems.at[u], device_id=(x, y, 1 - c), device_id_type=MESH))
    return copies


def _sibling_start(units, ins, got, sems):
    for cp in _sibling_copies(units, ins, got, sems):
        cp.start()


def _sibling_finish(units, ins, got, sems):
    for cp in _sibling_copies(units, ins, got, sems):
        cp.wait()


def _sibling_exchange(units, slabs, name):
    n_u = len(units)

    def body(*refs):
        ins, got, sems = refs[:n_u], refs[n_u:2 * n_u], refs[2 * n_u:]
        _sibling_start(units, ins, got, sems)
        _sibling_finish(units, ins, got, sems)

    return pl.pallas_call(
        body, name=name, in_specs=[ANY] * n_u, out_specs=[ANY] * n_u,
        out_shape=[jax.ShapeDtypeStruct((N_CHIPS, rows // 2, cols), F32) for _, (rows, cols), _ in units],
        scratch_shapes=[pltpu.SemaphoreType.DMA((n_u,)), pltpu.SemaphoreType.DMA((n_u,))],
    )(*slabs)


def _exchange_sems(n_u):
    return [pltpu.SemaphoreType.DMA((3 * n_u,)), pltpu.SemaphoreType.DMA((3 * n_u,))]


def _exchange_copies(ins, outs, sems):
    send_sems, recv_sems = sems
    x, y, c, me, others = _place()
    sends = [pltpu.make_async_remote_copy(
        src_ref=ins[u].at[2 * ox + oy], dst_ref=outs[u].at[me], send_sem=send_sems.at[3 * u + j],
        recv_sem=recv_sems.at[3 * u + j], device_id=(ox, oy, c), device_id_type=MESH)
        for u in range(len(ins)) for j, (ox, oy) in enumerate(others)]
    arrivals = [pltpu.make_async_remote_copy(
        src_ref=ins[u].at[me], dst_ref=outs[u].at[2 * ox + oy], send_sem=send_sems.at[3 * u + j],
        recv_sem=recv_sems.at[3 * u + j], device_id=(x, y, c), device_id_type=MESH)
        for u in range(len(ins)) for j, (ox, oy) in enumerate(others)]
    return sends, arrivals


def _exchange_start(ins, outs, sems):
    for cp in _exchange_copies(ins, outs, sems)[0]:
        cp.start()


def _exchange_finish(ins, outs, sems):
    sends, arrivals = _exchange_copies(ins, outs, sems)
    for cp in arrivals:
        cp.wait_recv()
    for cp in sends:
        cp.wait_send()


def _chip_exchange(pairs, small):
    n_u = len(pairs)

    def body(*refs):
        ins, s_ref = refs[:n_u], refs[n_u]
        outs, sall_ref = refs[n_u + 1:2 * n_u + 1], refs[2 * n_u + 1]
        send_sems, recv_sems, ssend_sems, srecv_sems, local_sem = refs[2 * n_u + 2:]
        x, y, c, _, _ = _place()
        dev = 4 * x + 2 * y + c
        local = pltpu.make_async_copy(s_ref, sall_ref.at[dev], local_sem)
        flips = [(fx, fy, fc) for fx in (0, 1) for fy in (0, 1) for fc in (0, 1)][1:]
        small_sends = [pltpu.make_async_remote_copy(
            src_ref=s_ref, dst_ref=sall_ref.at[dev], send_sem=ssend_sems.at[n], recv_sem=srecv_sems.at[n],
            device_id=(x ^ fx, y ^ fy, c ^ fc), device_id_type=MESH) for n, (fx, fy, fc) in enumerate(flips)]
        local.start()
        _exchange_start(ins, outs, (send_sems, recv_sems))
        for cp in small_sends:
            cp.start()
        _exchange_finish(ins, outs, (send_sems, recv_sems))
        for n, (fx, fy, fc) in enumerate(flips):
            src_dev = 4 * (x ^ fx) + 2 * (y ^ fy) + (c ^ fc)
            pltpu.make_async_remote_copy(
                src_ref=s_ref, dst_ref=sall_ref.at[src_dev], send_sem=ssend_sems.at[n], recv_sem=srecv_sems.at[n],
                device_id=(x, y, c), device_id_type=MESH).wait_recv()
        for cp in small_sends:
            cp.wait_send()
        local.wait()

    outs = pl.pallas_call(
        body, name="grad_chip_exchange", in_specs=[ANY] * (n_u + 1), out_specs=[ANY] * (n_u + 1),
        out_shape=[jax.ShapeDtypeStruct(p.shape, p.dtype) for p in pairs]
        + [jax.ShapeDtypeStruct((8, SMALL_ALL_ROWS, LANES), F32)],
        scratch_shapes=_exchange_sems(n_u) + [pltpu.SemaphoreType.DMA((7,)), pltpu.SemaphoreType.DMA((7,)),
                                              pltpu.SemaphoreType.DMA],
    )(*pairs, small)
    return outs[:n_u], outs[n_u]


def _sibling_share(grads):
    n_u = len(GRAD_UNITS)

    def body(*refs):
        ins, outs = refs[:n_u], refs[n_u:2 * n_u]
        send_sems, recv_sems = refs[2 * n_u:]
        x, y, c, _, _ = _place()

        def rows_of(ref, u, h):
            half = GRAD_UNITS[u][1][0] // 2
            return ref.at[pl.ds(h * half, half), :]

        sends = [pltpu.make_async_remote_copy(
            src_ref=rows_of(ins[u], u, c), dst_ref=rows_of(outs[u], u, c), send_sem=send_sems.at[u],
            recv_sem=recv_sems.at[u], device_id=(x, y, 1 - c), device_id_type=MESH) for u in range(n_u)]
        for cp in sends:
            cp.start()
        for u in range(n_u):
            pltpu.make_async_remote_copy(
                src_ref=rows_of(ins[u], u, c), dst_ref=rows_of(outs[u], u, 1 - c), send_sem=send_sems.at[u],
                recv_sem=recv_sems.at[u], device_id=(x, y, c), device_id_type=MESH).wait_recv()
        for cp in sends:
            cp.wait_send()

    return pl.pallas_call(
        body, name="grad_sibling_share", in_specs=[ANY] * n_u, out_specs=[ANY] * n_u,
        out_shape=[jax.ShapeDtypeStruct(g.shape, F32) for g in grads],
        input_output_aliases={u: u for u in range(n_u)},
        scratch_shapes=[pltpu.SemaphoreType.DMA((n_u,)), pltpu.SemaphoreType.DMA((n_u,))],
    )(*grads)


def kernel(x, g_mix_pre, g_mix_post, g_ffn_pre, g_ffn_post, conv_pw1_w, conv_pw1_b, conv_dw_w, conv_dw_b, conv_ln_g, conv_ln_b, conv_pw2_w, conv_pw2_b, attn_w_in, attn_b_f, attn_w_o, mlp_w_up, mlp_w_down, loss_target, m_g_mix_pre, m_g_mix_post, m_g_ffn_pre, m_g_ffn_post, m_conv_pw1_w, m_conv_pw1_b, m_conv_dw_w, m_conv_dw_b, m_conv_ln_g, m_conv_ln_b, m_conv_pw2_w, m_conv_pw2_b, m_attn_w_in, m_attn_b_f, m_attn_w_o, m_mlp_w_up, m_mlp_w_down, v_g_mix_pre, v_g_mix_post, v_g_ffn_pre, v_g_ffn_post, v_conv_pw1_w, v_conv_pw1_b, v_conv_dw_w, v_conv_dw_b, v_conv_ln_g, v_conv_ln_b, v_conv_pw2_w, v_conv_pw2_b, v_attn_w_in, v_attn_b_f, v_attn_w_o, v_mlp_w_up, v_mlp_w_down):
    weights = dict(g_mix_pre=g_mix_pre, g_mix_post=g_mix_post, g_ffn_pre=g_ffn_pre, g_ffn_post=g_ffn_post, conv_pw1_w=conv_pw1_w, conv_pw1_b=conv_pw1_b, conv_dw_w=conv_dw_w, conv_dw_b=conv_dw_b, conv_ln_g=conv_ln_g, conv_ln_b=conv_ln_b, conv_pw2_w=conv_pw2_w, conv_pw2_b=conv_pw2_b, attn_w_in=attn_w_in, attn_b_f=attn_b_f, attn_w_o=attn_w_o, mlp_w_up=mlp_w_up, mlp_w_down=mlp_w_down)
    mom_m = dict(g_mix_pre=m_g_mix_pre, g_mix_post=m_g_mix_post, g_ffn_pre=m_g_ffn_pre, g_ffn_post=m_g_ffn_post, conv_pw1_w=m_conv_pw1_w, conv_pw1_b=m_conv_pw1_b, conv_dw_w=m_conv_dw_w, conv_dw_b=m_conv_dw_b, conv_ln_g=m_conv_ln_g, conv_ln_b=m_conv_ln_b, conv_pw2_w=m_conv_pw2_w, conv_pw2_b=m_conv_pw2_b, attn_w_in=m_attn_w_in, attn_b_f=m_attn_b_f, attn_w_o=m_attn_w_o, mlp_w_up=m_mlp_w_up, mlp_w_down=m_mlp_w_down)
    mom_v = dict(g_mix_pre=v_g_mix_pre, g_mix_post=v_g_mix_post, g_ffn_pre=v_g_ffn_pre, g_ffn_post=v_g_ffn_post, conv_pw1_w=v_conv_pw1_w, conv_pw1_b=v_conv_pw1_b, conv_dw_w=v_conv_dw_w, conv_dw_b=v_conv_dw_b, conv_ln_g=v_conv_ln_g, conv_ln_b=v_conv_ln_b, conv_pw2_w=v_conv_pw2_w, conv_pw2_b=v_conv_pw2_b, attn_w_in=v_attn_w_in, attn_b_f=v_attn_b_f, attn_w_o=v_attn_w_o, mlp_w_up=v_mlp_w_up, mlp_w_down=v_mlp_w_down)
    order = [n for n, _ in SMALL[:4]] + ["conv_pw1_w", "conv_pw1_b", "conv_dw_w", "conv_dw_b", "conv_ln_g", "conv_ln_b",
                                          "conv_pw2_w", "conv_pw2_b", "attn_w_in", "attn_b_f", "attn_w_o", "mlp_w_up",
                                          "mlp_w_down"]
    T = x.shape[1]
    x0 = x.reshape(T, D)
    target = loss_target.reshape(T, D)
    row = lambda a, l: a[l:l + 1, :]

    def shard(unit):
        if unit[:-1] in ("up", "down"):
            return weights["mlp_w_" + unit[:-1]][int(unit[-1])].astype(BF16)
        w = weights[unit][0].astype(BF16)
        return jnp.pad(w, ((0, HALO - CONV_W), (0, 0))) if unit == "conv_dw_w" else w

    by_cols = lambda g: g.transpose(1, 0, 2).reshape(g.shape[1], N_CHIPS * g.shape[2])
    by_rows = lambda g: g.reshape(N_CHIPS * g.shape[1], g.shape[2])
    g_pw1, g_dw, g_pw2 = _allgather_weights(GATHER_AT_START, [shard(n) for n, _ in GATHER_AT_START], "allgather_conv")
    w_pw1, w_pw2, w_dw = by_cols(g_pw1), by_rows(g_pw2), by_cols(g_dw).astype(F32)
    b_f = jnp.pad(attn_b_f, ((0, 0), (0, LANES - N_HEADS)))

    def mlp_fwd(h, l):
        (act,) = _matmul(h, w_up[l], "nn", (BF16,), _epi_sqrelu, f"mlp{l}_up", tm=2048)
        (mo,) = _matmul(act, w_down[l], "nn", (F32,), _epi_plain, f"mlp{l}_down", tk=DFF)
        return act, mo

    def mlp0_fwd(h):
        ride = [dict(gather_units=units, gather_shards=[shard(n) for n, _ in units]) for units in GATHER_BEHIND_MLP]
        (act,), (g_up1,) = _matmul(h, w_up[0], "nn", (BF16,), _epi_sqrelu, "mlp0_up", tm=2048, **ride[0])
        (mo,), (g_down1,) = _matmul(act, w_down[0], "nn", (F32,), _epi_plain, "mlp0_down", tk=DFF, **ride[1])
        w_up.append(by_cols(g_up1))
        w_down.append(by_rows(g_down1))
        return act, mo

    (h0,) = _resnorm_fwd(x0, None, None, row(g_mix_pre, 0), "norm_in")
    (a0,) = _matmul(h0, w_pw1, "nn", (F32,), _epi_bias, "conv_pw1", extras=((conv_pw1_b, "row"),), tm=2048)
    y0, z0, (g_up0, g_down0, g_win, g_wo) = _conv_fwd(
        a0, w_dw, conv_dw_b, conv_ln_g, conv_ln_b, "conv_fwd", GATHER_BEHIND_CONV,
        [shard(n) for n, _ in GATHER_BEHIND_CONV])
    w_up, w_down = [by_cols(g_up0)], [by_rows(g_down0)]
    w_o = by_rows(g_wo)
    w_in = jnp.pad(by_cols(g_win), ((0, 0), (0, W_IN_PAD - W_IN_COLS)))
    (m0,) = _matmul(z0, w_pw2, "nn", (F32,), _epi_bias, "conv_pw2", extras=((conv_pw2_b, "row"),), tm=2048)
    x1, h1 = _resnorm_fwd(x0, m0, row(g_mix_post, 0), row(g_ffn_pre, 0), "norm_conv_out")
    act0, mo0 = mlp0_fwd(h1)
    x2, h2 = _resnorm_fwd(x1, mo0, row(g_ffn_post, 0), row(g_mix_pre, 1), "norm_mlp0_out")

    (proj,) = _matmul(h2, w_in[:, :3 * D], "nn", (BF16,), _epi_plain, "attn_in", tm=2048)
    (fl,) = _matmul(h2, w_in[:, 3 * D:], "nn", (F32,), _epi_plain, "attn_gate_in")
    f_all, f1, f2, f3 = _gate_fwd(fl, b_f, "gate_fwd")
    (q_aug, k_aug, v_aug, qn2, kn2), _ = _attn_pack(proj, f1, f2, f3, "attn_pack", (), ())
    tb = min(ATT_TB, T)
    qk = 2.02 * jnp.sqrt(qn2[:, 0, 0] * kn2[:, 0, 0]) + 1.0
    fs, fe = f_all[0::tb, :N_HEADS].T, f_all[tb - 1::tb, :N_HEADS].T
    o_aug, q2_aug = _flash_fwd(qk, fs, fe, jnp.sqrt(kn2[:, 0, 0]), q_aug, k_aug, v_aug, "flash_fwd")
    (o_bf,) = _merge_heads(o_aug, 1.0, BF16, "attn_merge_o")
    (m1,) = _matmul(o_bf, w_o, "nn", (F32,), _epi_plain, "attn_out", tm=2048)
    x3, h3 = _resnorm_fwd(x2, m1, row(g_mix_post, 1), row(g_ffn_pre, 1), "norm_attn_out")
    act1, mo1 = mlp_fwd(h3, 1)
    dy, loss_part = _resnorm_loss(x3, mo1, row(g_ffn_post, 1), target, "loss")
    loss = lax.psum(loss_part[0, 0], ("x", "y", "c"))

    G = {}
    chip = 2 * lax.axis_index("x") + lax.axis_index("y")
    place = jnp.stack([chip] + [k + (k >= chip).astype(jnp.int32) for k in range(N_CHIPS - 1)]
                      + [lax.axis_index("c")]).astype(jnp.int32)

    def mlp_bwd(d_out, mo, act, h, x_in, l):
        d_mo, dg_post, _ = _norm_bwd(mo, row(g_ffn_post, l), d_out, None, BF16, f"mlp{l}_post_bwd")
        (d_up,) = _matmul(d_mo, w_down[l], "nt", (BF16,), _epi_dsqrelu, f"mlp{l}_dact", extras=((act, "tile"),), tm=2048)
        (dw_down,) = _matmul(act, d_mo, "tn", (F32,), _epi_plain, tk=2048, name=f"mlp{l}_dwdown")
        (dw_up,) = _matmul(h, d_up, "tn", (F32,), _epi_plain, tk=2048, name=f"mlp{l}_dwup", shard_out=True)
        (d_h,) = _matmul(d_up, w_up[l], "nt", (F32,), _epi_plain, f"mlp{l}_dh", tk=DFF)
        d_in, dg_pre, _ = _norm_bwd(x_in, row(g_ffn_pre, l), d_h, d_out, F32, f"mlp{l}_pre_bwd")
        G[f"up{l}"], G[f"down{l}"] = dw_up, dw_down.reshape(N_CHIPS, D, D)
        return d_in, dg_post, dg_pre

    d_x3, dg_ffn_post1, dg_ffn_pre1 = mlp_bwd(dy, mo1, act1, h3, x3, 1)

    d_m1, dg_mix_post1, _ = _norm_bwd(m1, row(g_mix_post, 1), d_x3, None, BF16, "attn_post_bwd")
    (dw_o,) = _matmul(o_bf, d_m1, "tn", (F32,), _epi_plain, tk=2048, name="attn_dwo")
    G["attn_w_o"] = dw_o.reshape(N_CHIPS, D // N_CHIPS, D)
    (d_o,) = _matmul(d_m1, w_o, "nt", (F32,), _epi_plain, "attn_do", tm=2048)
    do_aug = _attn_prep(d_o, o_aug, "attn_prep")
    dq_aug = _flash_dq(qk, fs, fe, q2_aug, k_aug, do_aug, v_aug, "flash_dq")
    dk_aug, dv_aug = _flash_dkv(qk, fs, fe, q2_aug, k_aug, do_aug, v_aug, "flash_dkv")
    d_q, dF_q = _merge_heads(dq_aug, 0.125, BF16, "attn_merge_dq", column=AUG_F)
    d_k, dF_k = _merge_heads(dk_aug, 1.0, BF16, "attn_merge_dk", column=AUG_ONE)
    (d_v,) = _merge_heads(dv_aug, 1.0, BF16, "attn_merge_dv")
    d_fl, db_f = _gate_bwd(dF_q, dF_k, fl, b_f, "gate_bwd")
    d_proj = jnp.concatenate([d_q, d_k, d_v, d_fl.astype(BF16)], axis=1)
    (dw_in,) = _matmul(h2, d_proj, "tn", (F32,), _epi_plain, tk=2048, name="attn_dwin", tn=640)
    G["attn_w_in"] = dw_in[:, :W_IN_COLS].reshape(D, N_CHIPS, W_IN_COLS // N_CHIPS).transpose(1, 0, 2)
    (d_h2,) = _matmul(d_proj, w_in, "nt", (F32,), _epi_plain, "attn_dh", tk=W_IN_PAD)
    d_x2, dg_mix_pre1, _ = _norm_bwd(x2, row(g_mix_pre, 1), d_h2, d_x3, F32, "attn_pre_bwd")

    d_x1, dg_ffn_post0, dg_ffn_pre0 = mlp_bwd(d_x2, mo0, act0, h1, x1, 0)

    d_m0, dg_mix_post0, db_pw2 = _norm_bwd(m0, row(g_mix_post, 0), d_x1, None, BF16, "conv_post_bwd")
    (dw_pw2,) = _matmul(z0, d_m0, "tn", (F32,), _epi_plain, tk=2048, name="conv_dwpw2")
    G["conv_pw2_w"] = dw_pw2.reshape(N_CHIPS, D // N_CHIPS, D)
    (d_z0,) = _matmul(d_m0, w_pw2, "nt", (F32,), _epi_plain, "conv_dz", tm=2048)
    conv_units = [u for u in GRAD_UNITS if u[0].startswith("conv")]
    attn_units = [u for u in GRAD_UNITS if u not in conv_units]
    attn_slabs = [G[n] for n, _, _ in attn_units]
    d_y0, dln_g, dln_b, ddw_b, attn_got = _conv_bwd_ln(d_z0, y0, conv_ln_g, conv_ln_b, "conv_bwd_ln", attn_units,
                                                       attn_slabs)
    pairs = {n: _add_pair(place, s, g, f"grad_pair_{n}") for s, g, (n, _, _) in zip(attn_slabs, attn_got, attn_units)}
    d_a0, db_pw1, ddw_w, attn_by_chip = _conv_bwd_dw(d_y0, a0, w_dw, "conv_bwd_dw",
                                                    [pairs[n] for n, _, _ in attn_units])
    by_chip = {n: b for b, (n, _, _) in zip(attn_by_chip, attn_units)}
    (G["conv_pw1_w"],) = _matmul(h0, d_a0, "tn", (F32,), _epi_plain, tk=2048, name="conv_dwpw1", tn=512, shard_out=True)
    (d_h0,) = _matmul(d_a0, w_pw1, "nt", (F32,), _epi_plain, "conv_dh", tk=2 * D)
    d_x0, dg_mix_pre0, _ = _norm_bwd(x0, row(g_mix_pre, 0), d_h0, d_x1, F32, "conv_pre_bwd")

    small_local = {
        "g_mix_pre": jnp.concatenate([dg_mix_pre0, dg_mix_pre1]), "g_mix_post": jnp.concatenate([dg_mix_post0, dg_mix_post1]),
        "g_ffn_pre": jnp.concatenate([dg_ffn_pre0, dg_ffn_pre1]), "g_ffn_post": jnp.concatenate([dg_ffn_post0, dg_ffn_post1]),
        "conv_pw1_b": db_pw1, "conv_dw_b": ddw_b, "conv_ln_g": dln_g, "conv_ln_b": dln_b, "conv_pw2_b": db_pw2,
        "attn_b_f": db_f[:, :N_HEADS], "conv_dw_w_full": ddw_w,
    }

    conv_slabs = [G[n] for n, _, _ in conv_units]
    conv_got = _sibling_exchange(conv_units, conv_slabs, "grad_sibling_exchange_conv")
    pairs.update({n: _add_pair(place, s, g, f"grad_pair_{n}")
                  for s, g, (n, _, _) in zip(conv_slabs, conv_got, conv_units)})
    conv_by_chip, small_all = _chip_exchange([pairs[n] for n, _, _ in conv_units],
                                             _pack([small_local[n] for n, _ in SMALL_ALL], SMALL_ALL_ROWS))
    by_chip.update({n: b for b, (n, _, _) in zip(conv_by_chip, conv_units)})
    unit_grads = _sibling_share([_add_chips(place, pairs[n], by_chip[n], f"grad_chips_{n}") for n, _, _ in GRAD_UNITS])
    grads = _unpack(_add_slots(small_all, "grad_add_small", tb=SMALL_ALL_ROWS), SMALL_ALL)
    grads["conv_dw_w"] = lax.dynamic_slice(grads.pop("conv_dw_w_full"), (0, chip * 256), (CONV_W, 256))[None]

    delta, new_m, new_v = {}, {}, {}
    per_layer = {}
    for g, (unit, shape, (name, layer)) in zip(unit_grads, GRAD_UNITS):
        view = lambda d: (d[name] if layer is None else d[name][layer]).reshape(shape)
        per_layer.setdefault(name, []).append((g,) + tuple(_adamw(view(weights), g, view(mom_m), view(mom_v),
                                                                  f"adamw_{unit}")))
    for name, parts in per_layer.items():
        full = weights[name].shape
        join = lambda n: (parts[0][n] if len(parts) == 1 else jnp.stack([p[n] for p in parts])).reshape(full)
        grads[name], delta[name], new_m[name], new_v[name] = join(0), join(1), join(2), join(3)
    dw2 = lambda d: d["conv_dw_w"].reshape(CONV_W, 256)
    d, nm, nv = _adamw(dw2(weights), dw2(grads), dw2(mom_m), dw2(mom_v), "adamw_conv_dw_w")
    full = weights["conv_dw_w"].shape
    delta["conv_dw_w"], new_m["conv_dw_w"], new_v["conv_dw_w"] = d.reshape(full), nm.reshape(full), nv.reshape(full)
    packs = [_pack([src[n] for n, _ in SMALL], SMALL_ROWS) for src in (weights, grads, mom_m, mom_v)]
    d, nm, nv = _adamw(*packs, "adamw_small")
    for dst, packed in ((delta, d), (new_m, nm), (new_v, nv)):
        dst.update(_unpack(packed, SMALL))

    grad_x = d_x0.reshape(x.shape)
    return (loss, grad_x, *[grads[n] for n in order], *[delta[n] for n in order], *[new_m[n] for n in order],
            *[new_v[n] for n in order])
```
